```python
import jax, jax.numpy as jnp
from jax import lax
import numpy as np

D_MODEL = 1024
BATCH = 32
SEQ = 2048
DEPTH = 1

D_MIX = D_MODEL
MLA_HEADS = 4
MLA_NOPE = 128
MLA_ROPE = 64
MLA_V = 128
MLA_WIDTH = MLA_HEADS * MLA_V
Q_LORA = 256
KV_LORA = 128
ROPE_THETA = 10000.0
Q_BLOCK = 128
RW_HEAD = 64
RW_WIDTH = D_MIX - MLA_WIDTH
RW_HEADS = RW_WIDTH // RW_HEAD
W_LORA = 64
A_LORA = 64
RW_GN_EPS = 64e-5
NORM_EPS = 1e-6
MLA_COLS = Q_LORA + KV_LORA + MLA_ROPE
RW_SHIFT_COLS = 3 * RW_WIDTH + W_LORA + A_LORA
GATE_COLS = D_MIX
D_IN = MLA_COLS + RW_SHIFT_COLS + GATE_COLS

kernel_name = 'hymba_mla_rwkv7_sandwich'


def rmsnorm(x, g):
    xf = x.astype(jnp.float32)
    y = xf * lax.rsqrt(jnp.mean(xf * xf, axis=-1, keepdims=True) + NORM_EPS)
    return (y * g.astype(jnp.float32)).astype(x.dtype)


def rope_tables(positions):
    inv_freq = ROPE_THETA ** (-jnp.arange(0, MLA_ROPE, 2, dtype=jnp.float32) / MLA_ROPE)
    ang = positions.astype(jnp.float32)[..., None] * inv_freq
    ang = jnp.concatenate([ang, ang], axis=-1)
    return jnp.cos(ang), jnp.sin(ang)


def apply_rope(x, cos, sin):
    x1, x2 = jnp.split(x, 2, axis=-1)
    rot = jnp.concatenate([-x2, x1], axis=-1)
    return (x.astype(jnp.float32) * cos + rot.astype(jnp.float32) * sin).astype(x.dtype)


def token_shift(p):
    return jnp.pad(p[:, :-1], ((0, 0), (1, 0), (0, 0)))


def mla_attention(q_nope, q_rope, k_nope, k_rope, v):
    T = q_nope.shape[1]
    scale = (MLA_NOPE + MLA_ROPE) ** -0.5
    outs = []
    for i in range(T // Q_BLOCK):
        s, e = i * Q_BLOCK, (i + 1) * Q_BLOCK
        scores = (jnp.einsum('bqhd,bkhd->bhqk', q_nope[:, s:e], k_nope[:, :e])
                  + jnp.einsum('bqhr,bkr->bhqk', q_rope[:, s:e], k_rope[:, :e])).astype(jnp.float32) * scale
        mask = (s + jnp.arange(Q_BLOCK))[:, None] >= jnp.arange(e)[None, :]
        scores = jnp.where(mask, scores, -jnp.inf)
        probs = jax.nn.softmax(scores, axis=-1).astype(v.dtype)
        outs.append(jnp.einsum('bhqk,bkhd->bqhd', probs, v[:, :e]))
    return jnp.concatenate(outs, axis=1)


def wkv7_scan(r, w, k, v, kk, a):
    B, T, H, N = r.shape

    def step(S, inp):
        r_t, w_t, k_t, v_t, kk_t, a_t = inp
        sa = jnp.einsum('bhvk,bhk->bhv', S, -kk_t)
        S = (S * w_t[:, :, None, :] + sa[..., None] * (kk_t * a_t)[:, :, None, :]
             + v_t[..., None] * k_t[:, :, None, :])
        return S, jnp.einsum('bhvk,bhk->bhv', S, r_t)

    xs = tuple(jnp.moveaxis(t.astype(jnp.float32), 1, 0) for t in (r, w, k, v, kk, a))
    S0 = jnp.zeros((B, H, N, N), jnp.float32)
    _, ys = lax.scan(step, S0, xs)
    return jnp.moveaxis(ys, 0, 1)


def _fwd_setup_inputs(seed: int = 0) -> dict:
    key = jax.random.key(seed)
    ks = jax.random.split(key, 24)
    L = DEPTH
    f32 = jnp.float32

    def nrm(k, shape, scale):
        return jax.random.normal(k, shape, f32) * scale

    x = nrm(ks[0], (BATCH, SEQ, D_MODEL), 1.0)
    offset = jax.random.randint(ks[1], (BATCH, 1), 0, 4096, dtype=jnp.int32)
    positions = offset + jnp.arange(SEQ, dtype=jnp.int32)[None, :]
    return {
        'x': x,
        'positions': positions,
        'norm_pre_g': 1.0 + nrm(ks[2], (L, D_MODEL), 0.02),
        'w_in': nrm(ks[3], (L, D_MODEL, D_IN), D_MODEL ** -0.5),
        'mla_q_norm_g': 1.0 + nrm(ks[4], (L, Q_LORA), 0.02),
        'mla_w_uq': nrm(ks[5], (L, Q_LORA, MLA_HEADS * (MLA_NOPE + MLA_ROPE)), Q_LORA ** -0.5),
        'mla_kv_norm_g': 1.0 + nrm(ks[6], (L, KV_LORA), 0.02),
        'mla_w_ukv': nrm(ks[7], (L, KV_LORA, MLA_HEADS * (MLA_NOPE + MLA_V)), KV_LORA ** -0.5),
        'rw_mu': jax.random.uniform(ks[8], (L, RW_SHIFT_COLS), f32),
        'rw_w0': -2.5 + nrm(ks[9], (L, RW_WIDTH), 0.5),
        'rw_w2': nrm(ks[10], (L, W_LORA, RW_WIDTH), 0.5 * W_LORA ** -0.5),
        'rw_a0': nrm(ks[11], (L, RW_WIDTH), 0.1),
        'rw_a2': nrm(ks[12], (L, A_LORA, RW_WIDTH), 0.5 * A_LORA ** -0.5),
        'rw_k_k': 0.85 + nrm(ks[13], (L, RW_WIDTH), 0.05),
        'rw_k_a': 1.0 + nrm(ks[14], (L, RW_WIDTH), 0.05),
        'rw_r_k': nrm(ks[15], (L, RW_HEADS, RW_HEAD), 0.1),
        'rw_ln_g': 1.0 + nrm(ks[16], (L, RW_WIDTH), 0.02),
        'rw_ln_b': nrm(ks[17], (L, RW_WIDTH), 0.02),
        'w_out': nrm(ks[18], (L, D_MIX, D_MODEL), D_MIX ** -0.5),
        'norm_post_g': 1.0 + nrm(ks[19], (L, D_MODEL), 0.02),
    }


def _fwd_reference(x, positions, norm_pre_g, w_in, mla_q_norm_g, mla_w_uq, mla_kv_norm_g, mla_w_ukv,
              rw_mu, rw_w0, rw_w2, rw_a0, rw_a2, rw_k_k, rw_k_a, rw_r_k, rw_ln_g, rw_ln_b,
              w_out, norm_post_g):
    B, T, _ = x.shape
    f32 = jnp.float32
    cos, sin = rope_tables(positions)
    h = x
    for l in range(DEPTH):
        u = rmsnorm(h, norm_pre_g[l])
        p = u @ w_in[l]
        p_mla, p_rw, z = jnp.split(p, [MLA_COLS, MLA_COLS + RW_SHIFT_COLS], axis=-1)

        c_q, c_kv, k_r = jnp.split(p_mla, [Q_LORA, Q_LORA + KV_LORA], axis=-1)
        q = (rmsnorm(c_q, mla_q_norm_g[l]) @ mla_w_uq[l]).reshape(B, T, MLA_HEADS, MLA_NOPE + MLA_ROPE)
        q_nope, q_rope = jnp.split(q, [MLA_NOPE], axis=-1)
        kv = (rmsnorm(c_kv, mla_kv_norm_g[l]) @ mla_w_ukv[l]).reshape(B, T, MLA_HEADS, MLA_NOPE + MLA_V)
        k_nope, v_mla = jnp.split(kv, [MLA_NOPE], axis=-1)
        q_rope = apply_rope(q_rope, cos[:, :, None, :], sin[:, :, None, :])
        k_r = apply_rope(k_r, cos, sin)
        y_mla = mla_attention(q_nope, q_rope, k_nope, k_r, v_mla).reshape(B, T, MLA_WIDTH)

        ps = p_rw + (token_shift(p_rw) - p_rw) * rw_mu[l]
        r, k, v, xw, xa = jnp.split(
            ps, [RW_WIDTH, 2 * RW_WIDTH, 3 * RW_WIDTH, 3 * RW_WIDTH + W_LORA], axis=-1)
        w_log = -jax.nn.softplus(-(rw_w0[l] + jnp.tanh(xw) @ rw_w2[l]).astype(f32)) - 0.5
        decay = jnp.exp(-jnp.exp(w_log))
        a = jax.nn.sigmoid((rw_a0[l] + xa @ rw_a2[l]).astype(f32))
        kk = (k * rw_k_k[l]).astype(f32).reshape(B, T, RW_HEADS, RW_HEAD)
        kk = kk / jnp.maximum(jnp.linalg.norm(kk, axis=-1, keepdims=True), 1e-12)
        k = k.astype(f32) * (1.0 + (a - 1.0) * rw_k_a[l].astype(f32))
        heads = lambda t: t.reshape(B, T, RW_HEADS, RW_HEAD)
        r_h, k_h, v_h = heads(r.astype(f32)), heads(k), heads(v.astype(f32))
        y = wkv7_scan(r_h, heads(decay), k_h, v_h, kk, heads(a))
        mean = jnp.mean(y, axis=-1, keepdims=True)
        var = jnp.mean(jnp.square(y - mean), axis=-1, keepdims=True)
        y = ((y - mean) * lax.rsqrt(var + RW_GN_EPS)).reshape(B, T, RW_WIDTH)
        y = y * rw_ln_g[l].astype(f32) + rw_ln_b[l].astype(f32)
        bonus = jnp.sum(r_h * k_h * rw_r_k[l].astype(f32), axis=-1, keepdims=True) * v_h
        y_rw = (y + bonus.reshape(B, T, RW_WIDTH)).astype(x.dtype)

        y_cat = jnp.concatenate([y_mla, y_rw], axis=-1) * jax.nn.silu(z)
        out = y_cat @ w_out[l]
        h = h + rmsnorm(out, norm_post_g[l])
    return h


import jax as _jax
import jax.numpy as _jnp

TWIN_FORMAT = 'train_step'
FWD_PARAMS = ['x', 'positions', 'norm_pre_g', 'w_in', 'mla_q_norm_g', 'mla_w_uq', 'mla_kv_norm_g', 'mla_w_ukv', 'rw_mu', 'rw_w0', 'rw_w2', 'rw_a0', 'rw_a2', 'rw_k_k', 'rw_k_a', 'rw_r_k', 'rw_ln_g', 'rw_ln_b', 'w_out', 'norm_post_g']
TWIN_WEIGHTS = ['norm_pre_g', 'w_in', 'mla_q_norm_g', 'mla_w_uq', 'mla_kv_norm_g', 'mla_w_ukv', 'rw_mu', 'rw_w0', 'rw_w2', 'rw_a0', 'rw_a2', 'rw_k_k', 'rw_k_a', 'rw_r_k', 'rw_ln_g', 'rw_ln_b', 'w_out', 'norm_post_g']
TWIN_DIFF_INPUT = 'x'
TWIN_INPUTS = ['x', 'positions', 'norm_pre_g', 'w_in', 'mla_q_norm_g', 'mla_w_uq', 'mla_kv_norm_g', 'mla_w_ukv', 'rw_mu', 'rw_w0', 'rw_w2', 'rw_a0', 'rw_a2', 'rw_k_k', 'rw_k_a', 'rw_r_k', 'rw_ln_g', 'rw_ln_b', 'w_out', 'norm_post_g', 'loss_target', 'm_norm_pre_g', 'm_w_in', 'm_mla_q_norm_g', 'm_mla_w_uq', 'm_mla_kv_norm_g', 'm_mla_w_ukv', 'm_rw_mu', 'm_rw_w0', 'm_rw_w2', 'm_rw_a0', 'm_rw_a2', 'm_rw_k_k', 'm_rw_k_a', 'm_rw_r_k', 'm_rw_ln_g', 'm_rw_ln_b', 'm_w_out', 'm_norm_post_g', 'v_norm_pre_g', 'v_w_in', 'v_mla_q_norm_g', 'v_mla_w_uq', 'v_mla_kv_norm_g', 'v_mla_w_ukv', 'v_rw_mu', 'v_rw_w0', 'v_rw_w2', 'v_rw_a0', 'v_rw_a2', 'v_rw_k_k', 'v_rw_k_a', 'v_rw_r_k', 'v_rw_ln_g', 'v_rw_ln_b', 'v_w_out', 'v_norm_post_g']
TWIN_OUTPUTS = ['loss', 'grad_x', 'grad_norm_pre_g', 'grad_w_in', 'grad_mla_q_norm_g', 'grad_mla_w_uq', 'grad_mla_kv_norm_g', 'grad_mla_w_ukv', 'grad_rw_mu', 'grad_rw_w0', 'grad_rw_w2', 'grad_rw_a0', 'grad_rw_a2', 'grad_rw_k_k', 'grad_rw_k_a', 'grad_rw_r_k', 'grad_rw_ln_g', 'grad_rw_ln_b', 'grad_w_out', 'grad_norm_post_g', 'delta_norm_pre_g', 'delta_w_in', 'delta_mla_q_norm_g', 'delta_mla_w_uq', 'delta_mla_kv_norm_g', 'delta_mla_w_ukv', 'delta_rw_mu', 'delta_rw_w0', 'delta_rw_w2', 'delta_rw_a0', 'delta_rw_a2', 'delta_rw_k_k', 'delta_rw_k_a', 'delta_rw_r_k', 'delta_rw_ln_g', 'delta_rw_ln_b', 'delta_w_out', 'delta_norm_post_g', 'new_m_norm_pre_g', 'new_m_w_in', 'new_m_mla_q_norm_g', 'new_m_mla_w_uq', 'new_m_mla_kv_norm_g', 'new_m_mla_w_ukv', 'new_m_rw_mu', 'new_m_rw_w0', 'new_m_rw_w2', 'new_m_rw_a0', 'new_m_rw_a2', 'new_m_rw_k_k', 'new_m_rw_k_a', 'new_m_rw_r_k', 'new_m_rw_ln_g', 'new_m_rw_ln_b', 'new_m_w_out', 'new_m_norm_post_g', 'new_v_norm_pre_g', 'new_v_w_in', 'new_v_mla_q_norm_g', 'new_v_mla_w_uq', 'new_v_mla_kv_norm_g', 'new_v_mla_w_ukv', 'new_v_rw_mu', 'new_v_rw_w0', 'new_v_rw_w2', 'new_v_rw_a0', 'new_v_rw_a2', 'new_v_rw_k_k', 'new_v_rw_k_a', 'new_v_rw_r_k', 'new_v_rw_ln_g', 'new_v_rw_ln_b', 'new_v_w_out', 'new_v_norm_post_g']
TWIN_LEAF_KINDS = {'loss': 'loss', 'grad_x': 'grad_x', 'grad_norm_pre_g': 'grad_w', 'grad_w_in': 'grad_w', 'grad_mla_q_norm_g': 'grad_w', 'grad_mla_w_uq': 'grad_w', 'grad_mla_kv_norm_g': 'grad_w', 'grad_mla_w_ukv': 'grad_w', 'grad_rw_mu': 'grad_w', 'grad_rw_w0': 'grad_w', 'grad_rw_w2': 'grad_w', 'grad_rw_a0': 'grad_w', 'grad_rw_a2': 'grad_w', 'grad_rw_k_k': 'grad_w', 'grad_rw_k_a': 'grad_w', 'grad_rw_r_k': 'grad_w', 'grad_rw_ln_g': 'grad_w', 'grad_rw_ln_b': 'grad_w', 'grad_w_out': 'grad_w', 'grad_norm_post_g': 'grad_w', 'delta_norm_pre_g': 'delta_w', 'delta_w_in': 'delta_w', 'delta_mla_q_norm_g': 'delta_w', 'delta_mla_w_uq': 'delta_w', 'delta_mla_kv_norm_g': 'delta_w', 'delta_mla_w_ukv': 'delta_w', 'delta_rw_mu': 'delta_w', 'delta_rw_w0': 'delta_w', 'delta_rw_w2': 'delta_w', 'delta_rw_a0': 'delta_w', 'delta_rw_a2': 'delta_w', 'delta_rw_k_k': 'delta_w', 'delta_rw_k_a': 'delta_w', 'delta_rw_r_k': 'delta_w', 'delta_rw_ln_g': 'delta_w', 'delta_rw_ln_b': 'delta_w', 'delta_w_out': 'delta_w', 'delta_norm_post_g': 'delta_w', 'new_m_norm_pre_g': 'new_m', 'new_m_w_in': 'new_m', 'new_m_mla_q_norm_g': 'new_m', 'new_m_mla_w_uq': 'new_m', 'new_m_mla_kv_norm_g': 'new_m', 'new_m_mla_w_ukv': 'new_m', 'new_m_rw_mu': 'new_m', 'new_m_rw_w0': 'new_m', 'new_m_rw_w2': 'new_m', 'new_m_rw_a0': 'new_m', 'new_m_rw_a2': 'new_m', 'new_m_rw_k_k': 'new_m', 'new_m_rw_k_a': 'new_m', 'new_m_rw_r_k': 'new_m', 'new_m_rw_ln_g': 'new_m', 'new_m_rw_ln_b': 'new_m', 'new_m_w_out': 'new_m', 'new_m_norm_post_g': 'new_m', 'new_v_norm_pre_g': 'new_v', 'new_v_w_in': 'new_v', 'new_v_mla_q_norm_g': 'new_v', 'new_v_mla_w_uq': 'new_v', 'new_v_mla_kv_norm_g': 'new_v', 'new_v_mla_w_ukv': 'new_v', 'new_v_rw_mu': 'new_v', 'new_v_rw_w0': 'new_v', 'new_v_rw_w2': 'new_v', 'new_v_rw_a0': 'new_v', 'new_v_rw_a2': 'new_v', 'new_v_rw_k_k': 'new_v', 'new_v_rw_k_a': 'new_v', 'new_v_rw_r_k': 'new_v', 'new_v_rw_ln_g': 'new_v', 'new_v_rw_ln_b': 'new_v', 'new_v_w_out': 'new_v', 'new_v_norm_post_g': 'new_v'}


def _forward(args):
    return _fwd_reference(*[args[k] for k in FWD_PARAMS])


def _output_shape():
    out = _jax.eval_shape(lambda: _forward(_fwd_setup_inputs(0)))
    return out.shape, out.dtype

N_MICROBATCH = 1
ADAM_LR = 0.001
ADAM_B1 = 0.9
ADAM_B2 = 0.999
ADAM_EPS = 1e-08
ADAM_WD = 0.01
ADAM_STEP = 10
PER_EXAMPLE_BATCH_AXIS = {'x': 0, 'positions': 0, 'loss_target': 0}
SHARED_INPUTS = []
_WEIGHT_DTYPES = {'norm_pre_g': _jnp.float32, 'w_in': _jnp.float32, 'mla_q_norm_g': _jnp.float32, 'mla_w_uq': _jnp.float32, 'mla_kv_norm_g': _jnp.float32, 'mla_w_ukv': _jnp.float32, 'rw_mu': _jnp.float32, 'rw_w0': _jnp.float32, 'rw_w2': _jnp.float32, 'rw_a0': _jnp.float32, 'rw_a2': _jnp.float32, 'rw_k_k': _jnp.float32, 'rw_k_a': _jnp.float32, 'rw_r_k': _jnp.float32, 'rw_ln_g': _jnp.float32, 'rw_ln_b': _jnp.float32, 'w_out': _jnp.float32, 'norm_post_g': _jnp.float32}
MOMENT_SCALE = {'norm_pre_g': 7.987812e-01, 'w_in': 4.491744e-01, 'mla_q_norm_g': 1.308782e-01, 'mla_w_uq': 7.418401e-02, 'mla_kv_norm_g': 2.859503e-01, 'mla_w_ukv': 9.192786e-02, 'rw_mu': 9.332345e-01, 'rw_w0': 3.659999e-01, 'rw_w2': 4.689042e-02, 'rw_a0': 2.736901e-01, 'rw_a2': 2.258121e-01, 'rw_k_k': 2.811597e-01, 'rw_k_a': 6.246710e-01, 'rw_r_k': 1.058417e+00, 'rw_ln_g': 5.183680e-01, 'rw_ln_b': 1.266267e+00, 'w_out': 3.595634e-01, 'norm_post_g': 6.396189e+01}


def _to_microbatches(a, axis):
    t = _jnp.moveaxis(a, axis, 0)
    t = t.reshape((N_MICROBATCH, t.shape[0] // N_MICROBATCH) + t.shape[1:])
    return _jnp.moveaxis(t, 1, axis + 1)


def setup_inputs(seed: int = 0) -> dict:
    inp = _fwd_setup_inputs(seed)
    key = _jax.random.fold_in(_jax.random.key(seed), 7919)
    shape, _ = _output_shape()
    out = dict(inp)
    out["loss_target"] = _jax.random.normal(_jax.random.fold_in(key, 0), shape, _jnp.float32)
    for i, name in enumerate(TWIN_WEIGHTS):
        w = inp[name].astype(_jnp.float32)
        if MOMENT_SCALE is None:
            s = _jnp.sqrt(_jnp.mean(_jnp.square(w)) + 1e-30)
        else:
            s = MOMENT_SCALE[name]
        km, kv = _jax.random.split(_jax.random.fold_in(key, i + 1))
        out[name] = w
        out["m_" + name] = s * _jax.random.normal(km, w.shape, _jnp.float32)
        out["v_" + name] = (s * s) * _jax.random.uniform(kv, w.shape, _jnp.float32, 0.5, 1.5)
    if N_MICROBATCH > 1:
        for name, axis in PER_EXAMPLE_BATCH_AXIS.items():
            out[name] = _to_microbatches(out[name], axis)
    return {'x': out['x'], 'positions': out['positions'], 'norm_pre_g': out['norm_pre_g'], 'w_in': out['w_in'], 'mla_q_norm_g': out['mla_q_norm_g'], 'mla_w_uq': out['mla_w_uq'], 'mla_kv_norm_g': out['mla_kv_norm_g'], 'mla_w_ukv': out['mla_w_ukv'], 'rw_mu': out['rw_mu'], 'rw_w0': out['rw_w0'], 'rw_w2': out['rw_w2'], 'rw_a0': out['rw_a0'], 'rw_a2': out['rw_a2'], 'rw_k_k': out['rw_k_k'], 'rw_k_a': out['rw_k_a'], 'rw_r_k': out['rw_r_k'], 'rw_ln_g': out['rw_ln_g'], 'rw_ln_b': out['rw_ln_b'], 'w_out': out['w_out'], 'norm_post_g': out['norm_post_g'], 'loss_target': out['loss_target'], 'm_norm_pre_g': out['m_norm_pre_g'], 'm_w_in': out['m_w_in'], 'm_mla_q_norm_g': out['m_mla_q_norm_g'], 'm_mla_w_uq': out['m_mla_w_uq'], 'm_mla_kv_norm_g': out['m_mla_kv_norm_g'], 'm_mla_w_ukv': out['m_mla_w_ukv'], 'm_rw_mu': out['m_rw_mu'], 'm_rw_w0': out['m_rw_w0'], 'm_rw_w2': out['m_rw_w2'], 'm_rw_a0': out['m_rw_a0'], 'm_rw_a2': out['m_rw_a2'], 'm_rw_k_k': out['m_rw_k_k'], 'm_rw_k_a': out['m_rw_k_a'], 'm_rw_r_k': out['m_rw_r_k'], 'm_rw_ln_g': out['m_rw_ln_g'], 'm_rw_ln_b': out['m_rw_ln_b'], 'm_w_out': out['m_w_out'], 'm_norm_post_g': out['m_norm_post_g'], 'v_norm_pre_g': out['v_norm_pre_g'], 'v_w_in': out['v_w_in'], 'v_mla_q_norm_g': out['v_mla_q_norm_g'], 'v_mla_w_uq': out['v_mla_w_uq'], 'v_mla_kv_norm_g': out['v_mla_kv_norm_g'], 'v_mla_w_ukv': out['v_mla_w_ukv'], 'v_rw_mu': out['v_rw_mu'], 'v_rw_w0': out['v_rw_w0'], 'v_rw_w2': out['v_rw_w2'], 'v_rw_a0': out['v_rw_a0'], 'v_rw_a2': out['v_rw_a2'], 'v_rw_k_k': out['v_rw_k_k'], 'v_rw_k_a': out['v_rw_k_a'], 'v_rw_r_k': out['v_rw_r_k'], 'v_rw_ln_g': out['v_rw_ln_g'], 'v_rw_ln_b': out['v_rw_ln_b'], 'v_w_out': out['v_w_out'], 'v_norm_post_g': out['v_norm_post_g']}


def _loss(weights, diff, rest, loss_target):
    with _jax.named_scope("forward"):
        args = {**rest, TWIN_DIFF_INPUT: diff, **{k: w.astype(_WEIGHT_DTYPES[k]) for k, w in weights.items()}}
        y = _forward(args)
    with _jax.named_scope("loss_head"):
        err = _jnp.square(y.astype(_jnp.float32) - loss_target)
        return 0.5 * _jnp.sum(_jnp.mean(err, axis=-1)) if err.ndim else 0.5 * err


def _adamw(w, g, m, v):
    m = ADAM_B1 * m + (1.0 - ADAM_B1) * g
    v = ADAM_B2 * v + (1.0 - ADAM_B2) * _jnp.square(g)
    m_hat = m / (1.0 - ADAM_B1 ** ADAM_STEP)
    v_hat = v / (1.0 - ADAM_B2 ** ADAM_STEP)
    delta = -ADAM_LR * (m_hat / (_jnp.sqrt(v_hat) + ADAM_EPS) + ADAM_WD * w)
    return delta, m, v


def reference(x, positions, norm_pre_g, w_in, mla_q_norm_g, mla_w_uq, mla_kv_norm_g, mla_w_ukv, rw_mu, rw_w0, rw_w2, rw_a0, rw_a2, rw_k_k, rw_k_a, rw_r_k, rw_ln_g, rw_ln_b, w_out, norm_post_g, loss_target, m_norm_pre_g, m_w_in, m_mla_q_norm_g, m_mla_w_uq, m_mla_kv_norm_g, m_mla_w_ukv, m_rw_mu, m_rw_w0, m_rw_w2, m_rw_a0, m_rw_a2, m_rw_k_k, m_rw_k_a, m_rw_r_k, m_rw_ln_g, m_rw_ln_b, m_w_out, m_norm_post_g, v_norm_pre_g, v_w_in, v_mla_q_norm_g, v_mla_w_uq, v_mla_kv_norm_g, v_mla_w_ukv, v_rw_mu, v_rw_w0, v_rw_w2, v_rw_a0, v_rw_a2, v_rw_k_k, v_rw_k_a, v_rw_r_k, v_rw_ln_g, v_rw_ln_b, v_w_out, v_norm_post_g):
    given = dict(x=x, positions=positions, norm_pre_g=norm_pre_g, w_in=w_in, mla_q_norm_g=mla_q_norm_g, mla_w_uq=mla_w_uq, mla_kv_norm_g=mla_kv_norm_g, mla_w_ukv=mla_w_ukv, rw_mu=rw_mu, rw_w0=rw_w0, rw_w2=rw_w2, rw_a0=rw_a0, rw_a2=rw_a2, rw_k_k=rw_k_k, rw_k_a=rw_k_a, rw_r_k=rw_r_k, rw_ln_g=rw_ln_g, rw_ln_b=rw_ln_b, w_out=w_out, norm_post_g=norm_post_g, loss_target=loss_target, m_norm_pre_g=m_norm_pre_g, m_w_in=m_w_in, m_mla_q_norm_g=m_mla_q_norm_g, m_mla_w_uq=m_mla_w_uq, m_mla_kv_norm_g=m_mla_kv_norm_g, m_mla_w_ukv=m_mla_w_ukv, m_rw_mu=m_rw_mu, m_rw_w0=m_rw_w0, m_rw_w2=m_rw_w2, m_rw_a0=m_rw_a0, m_rw_a2=m_rw_a2, m_rw_k_k=m_rw_k_k, m_rw_k_a=m_rw_k_a, m_rw_r_k=m_rw_r_k, m_rw_ln_g=m_rw_ln_g, m_rw_ln_b=m_rw_ln_b, m_w_out=m_w_out, m_norm_post_g=m_norm_post_g, v_norm_pre_g=v_norm_pre_g, v_w_in=v_w_in, v_mla_q_norm_g=v_mla_q_norm_g, v_mla_w_uq=v_mla_w_uq, v_mla_kv_norm_g=v_mla_kv_norm_g, v_mla_w_ukv=v_mla_w_ukv, v_rw_mu=v_rw_mu, v_rw_w0=v_rw_w0, v_rw_w2=v_rw_w2, v_rw_a0=v_rw_a0, v_rw_a2=v_rw_a2, v_rw_k_k=v_rw_k_k, v_rw_k_a=v_rw_k_a, v_rw_r_k=v_rw_r_k, v_rw_ln_g=v_rw_ln_g, v_rw_ln_b=v_rw_ln_b, v_w_out=v_w_out, v_norm_post_g=v_norm_post_g)
    weights = {n: given[n] for n in TWIN_WEIGHTS}
    shared = {n: given[n] for n in SHARED_INPUTS}
    per_example = {n: given[n] for n in ['x', 'positions']}
    grad_fn = _jax.value_and_grad(_loss, argnums=(0, 1))

    def one_microbatch(ex, loss_target):
        ex = dict(ex)
        diff = ex.pop(TWIN_DIFF_INPUT)
        return grad_fn(weights, diff, {**shared, **ex}, loss_target)

    if N_MICROBATCH == 1:
        loss, (grad_w, grad_x) = one_microbatch(per_example, given["loss_target"])
    else:
        def body(carry, xs):
            loss_sum, grad_sum = carry
            l_k, (gw_k, gx_k) = one_microbatch(xs[0], xs[1])
            with _jax.named_scope("update"):
                return (loss_sum + l_k, _jax.tree.map(_jnp.add, grad_sum, gw_k)), gx_k

        init = (_jnp.zeros((), _jnp.float32), _jax.tree.map(_jnp.zeros_like, weights))
        (loss, grad_w), grad_x = _jax.lax.scan(body, init, (per_example, given["loss_target"]))
    with _jax.named_scope("update"):
        delta_w, new_m, new_v = {}, {}, {}
        for n in TWIN_WEIGHTS:
            delta_w[n], new_m[n], new_v[n] = _adamw(weights[n], grad_w[n], given["m_" + n], given["v_" + n])
    return (loss, grad_x, *[grad_w[n] for n in TWIN_WEIGHTS], *[delta_w[n] for n in TWIN_WEIGHTS],
            *[new_m[n] for n in TWIN_WEIGHTS], *[new_v[n] for n in TWIN_WEIGHTS])
```

```python
import functools

import jax
import jax.numpy as jnp
from jax import lax
from jax.experimental import pallas as pl
from jax.experimental.pallas import tpu as pltpu

F32 = jnp.float32
BF16 = jnp.bfloat16

D_MODEL = 1024
MLA_HEADS = 4
MLA_NOPE = 128
MLA_ROPE = 64
MLA_V = 128
MLA_WIDTH = MLA_HEADS * MLA_V
Q_LORA = 256
KV_LORA = 128
ROPE_THETA = 10000.0
RW_HEAD = 64
RW_WIDTH = 512
RW_HEADS = RW_WIDTH // RW_HEAD
LORA = 64
RW_COLS = 3 * RW_WIDTH + 2 * LORA
MLA_COLS = Q_LORA + KV_LORA + MLA_ROPE
D_IN = MLA_COLS + RW_COLS + D_MODEL
RW_GN_EPS = 64e-5
NORM_EPS = 1e-6
ATT_SCALE = (MLA_NOPE + MLA_ROPE) ** -0.5
ADAM_LR, ADAM_B1, ADAM_B2, ADAM_EPS, ADAM_WD, ADAM_STEP = 0.001, 0.9, 0.999, 1e-08, 0.01, 10
N_DEV = 8
LANES = 128
MXU = 256

PM_W = 512
WP_COLS = PM_W + RW_COLS + D_MODEL
RW_PIECES = ((0, 512), (512, 1024), (1024, 1536), (1536, 1664))

SHARD_ROWS = (("w_in", 1024 * 392 // LANES), ("mla_w_uq", 256 * 96 // LANES), ("mla_w_ukv", 128 * 128 // LANES),
              ("rw_w2", 64 * 64 // LANES), ("rw_a2", 64 * 64 // LANES), ("w_out", 128 * 1024 // LANES))
PACK_ROWS = sum(r for _, r in SHARD_ROWS)
SMALL = (("norm_pre_g", 1024), ("mla_q_norm_g", 256), ("mla_kv_norm_g", 128), ("rw_mu", 1664), ("rw_w0", 512),
         ("rw_a0", 512), ("rw_k_k", 512), ("rw_k_a", 512), ("rw_r_k", 512), ("rw_ln_g", 512), ("rw_ln_b", 512),
         ("norm_post_g", 1024))
SMALL_N = sum(n for _, n in SMALL)
SMALL_ROWS = 72
WEIGHTS = ("norm_pre_g", "w_in", "mla_q_norm_g", "mla_w_uq", "mla_kv_norm_g", "mla_w_ukv", "rw_mu", "rw_w0", "rw_w2",
           "rw_a0", "rw_a2", "rw_k_k", "rw_k_a", "rw_r_k", "rw_ln_g", "rw_ln_b", "w_out", "norm_post_g")


def _seg_ones():
    r = lax.broadcasted_iota(jnp.int32, (MXU, MXU), 0) >> 6
    c = lax.broadcasted_iota(jnp.int32, (MXU, MXU), 1) >> 6
    return jnp.where(r == c, 1.0, 0.0).astype(BF16)


def _seg_dot(x, ones, passes):
    parts, rem = [], x
    for p in range(passes):
        hb = rem.astype(BF16)
        parts.append(hb)
        if p + 1 < passes:
            rem = rem - hb.astype(F32)
    outs = []
    for j in range(x.shape[1] // MXU):
        acc = None
        for hb in parts:
            d = jnp.dot(hb[:, MXU * j:MXU * (j + 1)], ones, preferred_element_type=F32)
            acc = d if acc is None else acc + d
        outs.append(acc)
    return outs[0] if len(outs) == 1 else jnp.concatenate(outs, axis=1)


@jax.custom_vjp
def _segsum(x):
    return _seg_dot(x, _seg_ones(), 3)


_segsum.defvjp(lambda x: (_segsum(x), None), lambda _, g: (_segsum(g),))


@jax.custom_vjp
def _bdot(a, w):
    return jnp.dot(a.astype(BF16), w.astype(BF16), preferred_element_type=F32)


def _bdot_fwd(a, w):
    return _bdot(a, w), (a, w)


def _bdot_bwd(res, g):
    a, w = res
    gb = g.astype(BF16)
    da = lax.dot_general(gb, w.astype(BF16), (((1,), (1,)), ((), ())), preferred_element_type=F32)
    dw = lax.dot_general(a.astype(BF16), gb, (((0,), (0,)), ((), ())), preferred_element_type=F32)
    return da, dw


_bdot.defvjp(_bdot_fwd, _bdot_bwd)


def _rot_impl(x):
    w = x.shape[1]
    lane = lax.broadcasted_iota(jnp.int32, x.shape, 1)
    return jnp.where((lane & 63) < 32, -pltpu.roll(x, w - 32, 1), pltpu.roll(x, 32, 1))


@jax.custom_vjp
def _rot(x):
    return _rot_impl(x)


_rot.defvjp(lambda x: (_rot_impl(x), None), lambda _, g: (-_rot_impl(g),))


def _rms(x, g):
    return x * lax.rsqrt(jnp.mean(x * x, axis=-1, keepdims=True) + NORM_EPS) * g


def _shift_rows(p, prev_row):
    row = lax.broadcasted_iota(jnp.int32, p.shape, 0)
    return jnp.where(row == 0, prev_row, pltpu.roll(p, 1, 0))


def _unshift_rows(g, next_row):
    row = lax.broadcasted_iota(jnp.int32, g.shape, 0)
    return jnp.where(row == g.shape[0] - 1, next_row, pltpu.roll(g, g.shape[0] - 1, 0))


def _f_mla(cq, ckv, kr, cos, sin, g_q, wqn, wqr, g_kv, wkv):
    qn = _rms(cq, g_q)
    q_nope = _bdot(qn, wqn)
    q_r = _bdot(qn, wqr)
    cos4 = jnp.concatenate([cos] * MLA_HEADS, axis=1)
    sin4 = jnp.concatenate([sin] * MLA_HEADS, axis=1)
    q_rope = q_r * cos4 + _rot(q_r) * sin4
    kv = _bdot(_rms(ckv, g_kv), wkv)
    k_rope = kr * cos + _rot(kr) * sin
    return q_nope, q_rope, kv, k_rope


def _f_rw(pr, pk, pv, pt, sr, sk, sv, st, mu_r, mu_k, mu_v, mu_t, w0, w2p, a0, a2p, k_k, k_a):
    r = pr + (sr - pr) * mu_r
    k = pk + (sk - pk) * mu_k
    v = pv + (sv - pv) * mu_v
    t = pt + (st - pt) * mu_t
    nwl = -(w0 + _bdot(jnp.tanh(t), w2p))
    softplus = jnp.maximum(nwl, 0.0) + jnp.log(1.0 + jnp.exp(-jnp.abs(nwl)))
    decay = jnp.exp(-jnp.exp(-softplus - 0.5))
    a = jax.nn.sigmoid(a0 + _bdot(t, a2p))
    kk = k * k_k
    kk = kk / jnp.maximum(jnp.sqrt(_segsum(kk * kk)), 1e-12)
    k2 = k * (1.0 + (a - 1.0) * k_a)
    return r, decay, k2, v, -kk, kk * a


def _f_head(ys, r, k, v, ym, z1, z2, x, tgt, ln_g, ln_b, r_k, w1, w2, g_post):
    inv = 1.0 / RW_HEAD
    yc = ys - _segsum(ys) * inv
    var = _segsum(yc * yc) * inv
    y = yc * lax.rsqrt(var + RW_GN_EPS) * ln_g + ln_b
    y_rw = y + _segsum(r * k * r_k) * v
    c1 = ym * (z1 * jax.nn.sigmoid(z1))
    c2 = y_rw * (z2 * jax.nn.sigmoid(z2))
    out = _bdot(c1, w1) + _bdot(c2, w2)
    err = x + _rms(out, g_post) - tgt
    per_row = jnp.sum(err * err, axis=1, keepdims=True)
    return jnp.sum(per_row, axis=0, keepdims=True) * (0.5 / D_MODEL)


def _rows(tm, width):
    return pl.BlockSpec((tm, width), lambda i: (i, 0))


def _whole(shape):
    zeros = (0,) * len(shape)
    return pl.BlockSpec(shape, lambda i: zeros)


def _sds(shape, dtype=F32):
    return jax.ShapeDtypeStruct(shape, dtype)


_ARB1 = pltpu.CompilerParams(dimension_semantics=("arbitrary",))


def _acc(ref, val, first):
    @pl.when(first)
    def _():
        ref[...] = val

    @pl.when(jnp.logical_not(first))
    def _():
        ref[...] += val


def _fwd_a(x2, g_pre, wp, tm):
    n = x2.shape[0]

    def body(x_ref, g_ref, w_ref, u_ref, pm_ref, prw_ref, z_ref):
        ub = _rms(x_ref[...], g_ref[...]).astype(BF16)
        u_ref[...] = ub
        pm_ref[...] = jnp.dot(ub, w_ref[:, 0:PM_W], preferred_element_type=F32)
        prw_ref[...] = jnp.dot(ub, w_ref[:, PM_W:PM_W + RW_COLS], preferred_element_type=F32)
        z_ref[...] = jnp.dot(ub, w_ref[:, PM_W + RW_COLS:WP_COLS], preferred_element_type=F32)

    return pl.pallas_call(
        body, name="fwd_a", grid=(n // tm,),
        in_specs=[_rows(tm, D_MODEL), _whole((1, D_MODEL)), _whole((D_MODEL, WP_COLS))],
        out_specs=[_rows(tm, D_MODEL), _rows(tm, PM_W), _rows(tm, RW_COLS), _rows(tm, D_MODEL)],
        out_shape=[_sds((n, D_MODEL), BF16), _sds((n, PM_W)), _sds((n, RW_COLS)), _sds((n, D_MODEL))],
        compiler_params=_ARB1,
    )(x2, g_pre, wp)


def _rope_tables(posf, invf, tm):
    n = posf.shape[0]

    def body(p_ref, f_ref, c_ref, s_ref):
        ang = p_ref[...] * f_ref[...]
        c_ref[...] = jnp.cos(ang)
        s_ref[...] = jnp.sin(ang)

    return pl.pallas_call(
        body, name="rope_tables", grid=(n // tm,),
        in_specs=[_rows(tm, 1), _whole((1, LANES))],
        out_specs=[_rows(tm, LANES), _rows(tm, LANES)],
        out_shape=[_sds((n, LANES)), _sds((n, LANES))],
        compiler_params=_ARB1,
    )(posf, invf)


_B_WEIGHT_SHAPES = ((1, Q_LORA), (Q_LORA, 512), (Q_LORA, 512), (1, KV_LORA), (KV_LORA, 1024), (1, RW_COLS), (1, RW_WIDTH),
                    (LANES, RW_WIDTH), (1, RW_WIDTH), (LANES, RW_WIDTH), (1, RW_WIDTH), (1, RW_WIDTH))


def _halo_prev(tm):
    return pl.BlockSpec((8, RW_COLS), lambda i: (jnp.maximum(i * (tm // 8) - 1, 0), 0))


def _b_operands(pm_ref, prw_ref, halo_ref, wrefs, tile, tiles_per_seq):
    g_q, wqn, wqr, g_kv, wkv, mu, w0, w2p, a0, a2p, k_k, k_a = wrefs
    mla_in = (pm_ref[:, 0:Q_LORA], pm_ref[:, Q_LORA:Q_LORA + KV_LORA], pm_ref[:, Q_LORA + KV_LORA:PM_W])
    mla_w = (g_q[...], wqn[...], wqr[...], g_kv[...], wkv[...])
    keep = jnp.where(tile % tiles_per_seq == 0, 0.0, 1.0)
    prev = halo_ref[7:8, :] * keep
    ps = tuple(prw_ref[:, a:b] for a, b in RW_PIECES)
    ss = tuple(_shift_rows(p, prev[:, a:b]) for p, (a, b) in zip(ps, RW_PIECES))
    rw_w = tuple(mu[:, a:b] for a, b in RW_PIECES) + (w0[...], w2p[...], a0[...], a2p[...], k_k[...], k_a[...])
    return mla_in, mla_w, ps + ss, rw_w


def _fwd_b(pm, prw, cos, sin, bw, tm, tiles_per_seq):
    n = pm.shape[0]

    def body(pm_ref, prw_ref, halo_ref, cos_ref, sin_ref, *refs):
        wrefs, outs = refs[:12], refs[12:]
        mla_in, mla_w, rw_in, rw_w = _b_operands(pm_ref, prw_ref, halo_ref, wrefs, pl.program_id(0), tiles_per_seq)
        res = _f_mla(*mla_in, cos_ref[...], sin_ref[...], *mla_w) + _f_rw(*rw_in, *rw_w)
        for o_ref, val in zip(outs, res):
            o_ref[...] = val

    widths = (512, 512, 1024, LANES) + (RW_WIDTH,) * 6
    return pl.pallas_call(
        body, name="fwd_b", grid=(n // tm,),
        in_specs=[_rows(tm, PM_W), _rows(tm, RW_COLS), _halo_prev(tm), _rows(tm, LANES), _rows(tm, LANES)]
        + [_whole(s) for s in _B_WEIGHT_SHAPES],
        out_specs=[_rows(tm, w) for w in widths],
        out_shape=[_sds((n, w)) for w in widths],
        compiler_params=_ARB1,
    )(pm, prw, prw, cos, sin, *bw)


def _bwd_b(pm, prw, cos, sin, bw, cts, dkr_heads, tm, tiles_per_seq):
    n = pm.shape[0]

    ct_widths = (512, 512, 1024) + (RW_WIDTH,) * 9
    n_ct = len(ct_widths)

    def body(pm_ref, prw_ref, halo_ref, cos_ref, sin_ref, *refs):
        wrefs, ct_refs, dkr_ref = refs[:12], refs[12:12 + n_ct], refs[12 + n_ct]
        dpm_ref, dprw_ref, dps_ref = refs[13 + n_ct:16 + n_ct]
        wg_refs = refs[16 + n_ct:]
        tile = pl.program_id(0)
        first = tile == 0
        mla_in, mla_w, rw_in, rw_w = _b_operands(pm_ref, prw_ref, halo_ref, wrefs, tile, tiles_per_seq)
        cos, sin = cos_ref[...], sin_ref[...]
        ct = [r[...] for r in ct_refs]
        _, vjp_mla = jax.vjp(lambda *a: _f_mla(*a[:3], cos, sin, *a[3:]), *mla_in, *mla_w)
        dkr = dkr_ref[0] + dkr_ref[1] + dkr_ref[2] + dkr_ref[3]
        d_mla = vjp_mla((ct[0], ct[1], ct[2], dkr))
        dpm_ref[:, 0:Q_LORA] = d_mla[0]
        dpm_ref[:, Q_LORA:Q_LORA + KV_LORA] = d_mla[1]
        dpm_ref[:, Q_LORA + KV_LORA:PM_W] = d_mla[2]
        _, vjp_rw = jax.vjp(_f_rw, *rw_in, *rw_w)
        d_rw = vjp_rw((ct[3] + ct[4], ct[5], ct[6] + ct[7], ct[8] + ct[9], ct[10], ct[11]))
        for j, (a, b) in enumerate(RW_PIECES):
            dprw_ref[:, a:b] = d_rw[j]
            dps_ref[:, a:b] = d_rw[4 + j]
        g_q, wqn, wqr, g_kv, wkv, mu, w0, w2p, a0, a2p, k_k, k_a = wg_refs
        for ref, val in zip((g_q, wqn, wqr, g_kv, wkv), d_mla[3:]):
            _acc(ref, val, first)
        for j, (a, b) in enumerate(RW_PIECES):
            _acc(mu.at[:, a:b], d_rw[8 + j], first)
        for ref, val in zip((w0, w2p, a0, a2p, k_k, k_a), d_rw[12:]):
            _acc(ref, val, first)

    return pl.pallas_call(
        body, name="bwd_b", grid=(n // tm,),
        in_specs=[_rows(tm, PM_W), _rows(tm, RW_COLS), _halo_prev(tm), _rows(tm, LANES), _rows(tm, LANES)]
        + [_whole(s) for s in _B_WEIGHT_SHAPES] + [_rows(tm, w) for w in ct_widths]
        + [pl.BlockSpec((MLA_HEADS, tm, LANES), lambda i: (0, i, 0))],
        out_specs=[_rows(tm, PM_W), _rows(tm, RW_COLS), _rows(tm, RW_COLS)] + [_whole(s) for s in _B_WEIGHT_SHAPES],
        out_shape=[_sds((n, PM_W)), _sds((n, RW_COLS)), _sds((n, RW_COLS))] + [_sds(s) for s in _B_WEIGHT_SHAPES],
        compiler_params=_ARB1,
    )(pm, prw, prw, cos, sin, *bw, *cts, dkr_heads)


def _head(ys, r, k, v, ym, z, x2, tgt, hw, tm):
    n = x2.shape[0]
    h_shapes = ((1, RW_WIDTH), (1, RW_WIDTH), (1, RW_WIDTH), (D_MODEL, D_MODEL), (1, D_MODEL))

    def body(ys_ref, r_ref, k_ref, v_ref, ym_ref, z_ref, x_ref, t_ref, lng, lnb, rk, wout, gpost,
             dys_ref, dr_ref, dk_ref, dv_ref, dym_ref, dz_ref, dx_ref, loss_ref, dlng, dlnb, drk, dwout, dgpost):
        first = pl.program_id(0) == 0
        tgt_v = t_ref[...]
        args = (ys_ref[...], r_ref[...], k_ref[...], v_ref[...], ym_ref[...], z_ref[:, 0:MLA_WIDTH], z_ref[:, MLA_WIDTH:D_MODEL],
                x_ref[...], lng[...], lnb[...], rk[...], wout[0:MLA_WIDTH, :], wout[MLA_WIDTH:D_MODEL, :], gpost[...])
        loss, vjp = jax.vjp(lambda *a: _f_head(*a[:8], tgt_v, *a[8:]), *args)
        d = vjp(jnp.ones((1, 1), F32))
        dys_ref[...] = d[0]
        dr_ref[...] = d[1]
        dk_ref[...] = d[2]
        dv_ref[...] = d[3]
        dym_ref[...] = d[4]
        dz_ref[:, 0:MLA_WIDTH] = d[5]
        dz_ref[:, MLA_WIDTH:D_MODEL] = d[6]
        dx_ref[...] = d[7]
        _acc(loss_ref, jnp.broadcast_to(loss, (8, LANES)), first)
        _acc(dlng, d[8], first)
        _acc(dlnb, d[9], first)
        _acc(drk, d[10], first)
        _acc(dwout.at[0:MLA_WIDTH, :], d[11], first)
        _acc(dwout.at[MLA_WIDTH:D_MODEL, :], d[12], first)
        _acc(dgpost, d[13], first)

    widths = (RW_WIDTH,) * 4 + (MLA_WIDTH, D_MODEL, D_MODEL)
    return pl.pallas_call(
        body, name="head", grid=(n // tm,),
        in_specs=[_rows(tm, RW_WIDTH)] * 4 + [_rows(tm, MLA_WIDTH), _rows(tm, D_MODEL), _rows(tm, D_MODEL), _rows(tm, D_MODEL)]
        + [_whole(s) for s in h_shapes],
        out_specs=[_rows(tm, w) for w in widths] + [_whole((8, LANES))] + [_whole(s) for s in h_shapes],
        out_shape=[_sds((n, w)) for w in widths] + [_sds((8, LANES))] + [_sds(s) for s in h_shapes],
        compiler_params=_ARB1,
    )(ys, r, k, v, ym, z, x2, tgt, *hw)


def _halo_next(tm, n):
    last = n // 8 - 1
    return pl.BlockSpec((8, RW_COLS), lambda i: (jnp.minimum((i + 1) * (tm // 8), last), 0))


def _bwd_a(x2, g_pre, wp, dpm, dprw, dps, dz, dxres, tm, tiles_per_seq):
    n = x2.shape[0]
    nt_dims = (((1,), (1,)), ((), ()))

    def body(x_ref, g_ref, w_ref, dpm_ref, dprw_ref, dps_ref, nxt_ref, dz_ref, dxres_ref, gx_ref, dpb_ref, dg_ref):
        tile = pl.program_id(0)
        keep = jnp.where((tile + 1) % tiles_per_seq == 0, 0.0, 1.0)
        dprw_v = dprw_ref[...] + _unshift_rows(dps_ref[...], nxt_ref[0:1, :] * keep)
        dpm_b, dprw_b, dz_b = dpm_ref[...].astype(BF16), dprw_v.astype(BF16), dz_ref[...].astype(BF16)
        dpb_ref[:, 0:PM_W] = dpm_b
        dpb_ref[:, PM_W:PM_W + RW_COLS] = dprw_b
        dpb_ref[:, PM_W + RW_COLS:WP_COLS] = dz_b
        du = (lax.dot_general(dpm_b, w_ref[:, 0:PM_W], nt_dims, preferred_element_type=F32)
              + lax.dot_general(dprw_b, w_ref[:, PM_W:PM_W + RW_COLS], nt_dims, preferred_element_type=F32)
              + lax.dot_general(dz_b, w_ref[:, PM_W + RW_COLS:WP_COLS], nt_dims, preferred_element_type=F32))
        x = x_ref[...]
        xhat = x * lax.rsqrt(jnp.mean(x * x, axis=-1, keepdims=True) + NORM_EPS)
        dxn = du * g_ref[...]
        dx = (dxn - xhat * jnp.mean(dxn * xhat, axis=-1, keepdims=True)) * lax.rsqrt(jnp.mean(x * x, axis=-1, keepdims=True) + NORM_EPS)
        gx_ref[...] = dx + dxres_ref[...]
        _acc(dg_ref, jnp.sum(du * xhat, axis=0, keepdims=True), tile == 0)

    return pl.pallas_call(
        body, name="bwd_a", grid=(n // tm,),
        in_specs=[_rows(tm, D_MODEL), _whole((1, D_MODEL)), _whole((D_MODEL, WP_COLS)), _rows(tm, PM_W), _rows(tm, RW_COLS),
                  _rows(tm, RW_COLS), _halo_next(tm, n), _rows(tm, D_MODEL), _rows(tm, D_MODEL)],
        out_specs=[_rows(tm, D_MODEL), _rows(tm, WP_COLS), _whole((1, D_MODEL))],
        out_shape=[_sds((n, D_MODEL)), _sds((n, WP_COLS), BF16), _sds((1, D_MODEL))],
        compiler_params=_ARB1,
    )(x2, g_pre, wp, dpm, dprw, dps, dps, dz, dxres)


def _dw_in(u, dpb, tk, tn):
    n = u.shape[0]

    def body(u_ref, d_ref, o_ref):
        val = lax.dot_general(u_ref[...], d_ref[...], (((0,), (0,)), ((), ())), preferred_element_type=F32)
        _acc(o_ref, val, pl.program_id(1) == 0)

    return pl.pallas_call(
        body, name="dw_in", grid=(WP_COLS // tn, n // tk),
        in_specs=[pl.BlockSpec((tk, D_MODEL), lambda j, k: (k, 0)), pl.BlockSpec((tk, tn), lambda j, k: (k, j))],
        out_specs=pl.BlockSpec((D_MODEL, tn), lambda j, k: (0, j)),
        out_shape=_sds((D_MODEL, WP_COLS)),
        compiler_params=pltpu.CompilerParams(dimension_semantics=("arbitrary", "arbitrary")),
    )(u, dpb)


ATT_BLK = 256
_NT = (((1,), (1,)), ((), ()))
_TN = (((0,), (0,)), ((), ()))


def _causal(q0, k0, blk):
    row = q0 + lax.broadcasted_iota(jnp.int32, (blk, blk), 0)
    col = k0 + lax.broadcasted_iota(jnp.int32, (blk, blk), 1)
    return row >= col


def _attn_fwd(qn, qr, kv, kr):
    bsz, t, _ = qn.shape
    blk = min(ATT_BLK, t)

    def body(qn_ref, qr_ref, kn_ref, kr_ref, v_ref, o_ref, lse_ref):
        qi = pl.program_id(2)
        q = jnp.concatenate([qn_ref[...], qr_ref[...]], axis=1).astype(BF16)

        def kv_step(j, carry):
            m, l, acc = carry
            ks = pl.multiple_of(j * blk, blk)
            k = jnp.concatenate([kn_ref[pl.ds(ks, blk), :], kr_ref[pl.ds(ks, blk), :]], axis=1).astype(BF16)
            s = lax.dot_general(q, k, _NT, preferred_element_type=F32) * ATT_SCALE
            s = jnp.where(_causal(qi * blk, j * blk, blk), s, -1e30)
            m_new = jnp.maximum(m, jnp.max(s, axis=1, keepdims=True))
            alpha = jnp.exp(m - m_new)
            p = jnp.exp(s - m_new)
            l = alpha * l + jnp.sum(p, axis=1, keepdims=True)
            acc = alpha * acc + jnp.dot(p.astype(BF16), v_ref[pl.ds(ks, blk), :].astype(BF16), preferred_element_type=F32)
            return m_new, l, acc

        init = (jnp.full((blk, 1), -1e30, F32), jnp.zeros((blk, 1), F32), jnp.zeros((blk, MLA_V), F32))
        m, l, acc = lax.fori_loop(0, qi + 1, kv_step, init)
        o_ref[...] = acc / l
        lse_ref[...] = jnp.broadcast_to(m + jnp.log(l), (blk, LANES))

    return pl.pallas_call(
        body, name="attn_fwd", grid=(bsz, MLA_HEADS, t // blk),
        in_specs=[pl.BlockSpec((None, blk, LANES), lambda b, h, i: (b, i, h)),
                  pl.BlockSpec((None, blk, LANES), lambda b, h, i: (b, i, h)),
                  pl.BlockSpec((None, t, LANES), lambda b, h, i: (b, 0, 2 * h)),
                  pl.BlockSpec((None, t, LANES), lambda b, h, i: (b, 0, 0)),
                  pl.BlockSpec((None, t, LANES), lambda b, h, i: (b, 0, 2 * h + 1))],
        out_specs=[pl.BlockSpec((None, blk, LANES), lambda b, h, i: (b, i, h)),
                   pl.BlockSpec((None, None, blk, LANES), lambda b, h, i: (b, h, i, 0))],
        out_shape=[_sds((bsz, t, MLA_WIDTH)), _sds((bsz, MLA_HEADS, t, LANES))],
        compiler_params=pltpu.CompilerParams(dimension_semantics=("arbitrary", "arbitrary", "arbitrary")),
    )(qn, qr, kv, kr, kv)


def _attn_bwd(qn, qr, kv, kr, o, do, lse):
    bsz, t, _ = qn.shape
    blk = min(ATT_BLK, t)
    nb = t // blk

    def body(qn_ref, qr_ref, kn_ref, kr_ref, v_ref, o_ref, do_ref, lse_ref, dqn_ref, dqr_ref, dkv_ref, dkr_ref, dq_sc, delta_sc):
        dq_sc[...] = jnp.zeros_like(dq_sc)
        delta_sc[...] = jnp.sum(do_ref[...] * o_ref[...], axis=1, keepdims=True)

        def kv_loop(j, _):
            ks = pl.multiple_of(j * blk, blk)
            k = jnp.concatenate([kn_ref[pl.ds(ks, blk), :], kr_ref[pl.ds(ks, blk), :]], axis=1).astype(BF16)
            vb = v_ref[pl.ds(ks, blk), :].astype(BF16)

            def q_loop(i, carry):
                dk, dv = carry
                qs = pl.multiple_of(i * blk, blk)
                q = jnp.concatenate([qn_ref[pl.ds(qs, blk), :], qr_ref[pl.ds(qs, blk), :]], axis=1).astype(BF16)
                dob = do_ref[pl.ds(qs, blk), :].astype(BF16)
                s = lax.dot_general(q, k, _NT, preferred_element_type=F32) * ATT_SCALE
                p = jnp.where(_causal(i * blk, j * blk, blk), jnp.exp(s - lse_ref[pl.ds(qs, blk), 0:1]), 0.0)
                dv = dv + lax.dot_general(p.astype(BF16), dob, _TN, preferred_element_type=F32)
                dp = lax.dot_general(dob, vb, _NT, preferred_element_type=F32)
                ds = (p * (dp - delta_sc[pl.ds(qs, blk), :]) * ATT_SCALE).astype(BF16)
                dq_sc[pl.ds(qs, blk), :] += jnp.dot(ds, k, preferred_element_type=F32)
                dk = dk + lax.dot_general(ds, q, _TN, preferred_element_type=F32)
                return dk, dv

            dk, dv = lax.fori_loop(j, nb, q_loop, (jnp.zeros((blk, 2 * LANES), F32), jnp.zeros((blk, MLA_V), F32)))
            dkv_ref[pl.ds(ks, blk), 0:LANES] = dk[:, 0:LANES]
            dkv_ref[pl.ds(ks, blk), LANES:2 * LANES] = dv
            dkr_ref[pl.ds(ks, blk), :] = dk[:, LANES:2 * LANES]
            return 0

        lax.fori_loop(0, nb, kv_loop, 0)
        dqn_ref[...] = dq_sc[:, 0:LANES]
        dqr_ref[...] = dq_sc[:, LANES:2 * LANES]

    head_col = lambda b, h: (b, 0, h)
    return pl.pallas_call(
        body, name="attn_bwd", grid=(bsz, MLA_HEADS),
        in_specs=[pl.BlockSpec((None, t, LANES), head_col), pl.BlockSpec((None, t, LANES), head_col),
                  pl.BlockSpec((None, t, LANES), lambda b, h: (b, 0, 2 * h)),
                  pl.BlockSpec((None, t, LANES), lambda b, h: (b, 0, 0)),
                  pl.BlockSpec((None, t, LANES), lambda b, h: (b, 0, 2 * h + 1)),
                  pl.BlockSpec((None, t, LANES), head_col), pl.BlockSpec((None, t, LANES), head_col),
                  pl.BlockSpec((None, None, t, LANES), lambda b, h: (b, h, 0, 0))],
        out_specs=[pl.BlockSpec((None, t, LANES), head_col), pl.BlockSpec((None, t, LANES), head_col),
                   pl.BlockSpec((None, t, 2 * LANES), head_col),
                   pl.BlockSpec((None, None, t, LANES), lambda b, h: (h, b, 0, 0))],
        out_shape=[_sds((bsz, t, MLA_WIDTH)), _sds((bsz, t, MLA_WIDTH)), _sds((bsz, t, 2 * MLA_WIDTH)),
                   _sds((MLA_HEADS, bsz, t, LANES))],
        scratch_shapes=[pltpu.VMEM((t, 2 * LANES), F32), pltpu.VMEM((t, 1), F32)],
        compiler_params=pltpu.CompilerParams(dimension_semantics=("arbitrary", "arbitrary")),
    )(qn, qr, kv, kr, kv, o, do, lse)


SCAN_CHUNK = 16


def _diag_mask():
    row = lax.broadcasted_iota(jnp.int32, (RW_HEAD, RW_WIDTH), 0)
    lane = lax.broadcasted_iota(jnp.int32, (RW_HEAD, RW_WIDTH), 1)
    return jnp.where(row == (lane & (RW_HEAD - 1)), 1.0, 0.0)


def _scan_fwd(r, w, k, v, nkk, b):
    bsz, t, _ = r.shape
    tc = min(SCAN_CHUNK, t)

    def body(r_ref, w_ref, k_ref, v_ref, n_ref, b_ref, y_ref, st_ref, s_sc):
        @pl.when(pl.program_id(0) == 0)
        def _():
            s_sc[...] = jnp.zeros_like(s_sc)

        ones = _seg_ones()
        diag = _diag_mask()
        seg = lambda x: _seg_dot(x, ones, 2)

        def step(i, _):
            row = lambda ref, bi: ref[bi, pl.ds(i, 1), :]
            for bi in range(bsz):
                s_old = s_sc[bi]
                sa = seg(s_old * row(n_ref, bi))
                vc = seg(diag * row(v_ref, bi))
                s_new = s_old * row(w_ref, bi) + sa * row(b_ref, bi) + vc * row(k_ref, bi)
                s_sc[bi] = s_new
                st_ref[bi, i] = s_new
                yb = seg(s_new * row(r_ref, bi))
                y_ref[bi, pl.ds(i, 1), :] = jnp.sum(yb * diag, axis=0, keepdims=True)
            return 0

        lax.fori_loop(0, tc, step, 0)

    vec = pl.BlockSpec((bsz, tc, RW_WIDTH), lambda c: (0, c, 0))
    return pl.pallas_call(
        body, name="scan_fwd", grid=(t // tc,),
        in_specs=[vec] * 6,
        out_specs=[vec, pl.BlockSpec((bsz, tc, RW_HEAD, RW_WIDTH), lambda c: (0, c, 0, 0))],
        out_shape=[_sds((bsz, t, RW_WIDTH)), _sds((bsz, t, RW_HEAD, RW_WIDTH))],
        scratch_shapes=[pltpu.VMEM((bsz, RW_HEAD, RW_WIDTH), F32)],
        compiler_params=_ARB1,
    )(r, w, k, v, nkk, b)


def _scan_bwd(r, w, k, v, nkk, b, st, dy):
    bsz, t, _ = r.shape
    tc = min(SCAN_CHUNK, t)
    nc = t // tc

    def body(r_ref, w_ref, k_ref, v_ref, n_ref, b_ref, dy_ref, st_ref, halo_ref,
             dr_ref, dw_ref, dk_ref, dv_ref, dn_ref, db_ref, g_sc):
        c = pl.program_id(0)

        @pl.when(c == 0)
        def _():
            g_sc[...] = jnp.zeros_like(g_sc)

        ones = _seg_ones()
        diag = _diag_mask()
        seg = lambda x: _seg_dot(x, ones, 2)
        rsum = lambda x: jnp.sum(x, axis=0, keepdims=True)
        has_prev = jnp.where(c == nc - 1, 0.0, 1.0)

        def step(ii, _):
            i = tc - 1 - ii
            row = lambda ref, bi: ref[bi, pl.ds(i, 1), :]
            put = lambda ref, bi, val: ref.__setitem__((bi, pl.ds(i, 1), slice(None)), val)
            for bi in range(bsz):
                s_t = st_ref[bi, i]
                s_in = st_ref[bi, jnp.maximum(i - 1, 0)]
                s_p = jnp.where(i == 0, halo_ref[bi, 0] * has_prev, s_in)
                dc = seg(diag * row(dy_ref, bi))
                vc = seg(diag * row(v_ref, bi))
                g = g_sc[bi] + dc * row(r_ref, bi)
                put(dr_ref, bi, rsum(s_t * dc))
                sa = seg(s_p * row(n_ref, bi))
                dsa = seg(g * row(b_ref, bi))
                put(dv_ref, bi, rsum(seg(g * row(k_ref, bi)) * diag))
                put(dw_ref, bi, rsum(g * s_p))
                put(db_ref, bi, rsum(g * sa))
                put(dk_ref, bi, rsum(g * vc))
                put(dn_ref, bi, rsum(s_p * dsa))
                g_sc[bi] = g * row(w_ref, bi) + dsa * row(n_ref, bi)
            return 0

        lax.fori_loop(0, tc, step, 0)

    vec = pl.BlockSpec((bsz, tc, RW_WIDTH), lambda c: (0, nc - 1 - c, 0))
    return pl.pallas_call(
        body, name="scan_bwd", grid=(nc,),
        in_specs=[vec] * 7 + [pl.BlockSpec((bsz, tc, RW_HEAD, RW_WIDTH), lambda c: (0, nc - 1 - c, 0, 0)),
                              pl.BlockSpec((bsz, 1, RW_HEAD, RW_WIDTH), lambda c: (0, jnp.maximum((nc - 1 - c) * tc - 1, 0), 0, 0))],
        out_specs=[vec] * 6,
        out_shape=[_sds((bsz, t, RW_WIDTH))] * 6,
        scratch_shapes=[pltpu.VMEM((bsz, RW_HEAD, RW_WIDTH), F32)],
        compiler_params=_ARB1,
    )(r, w, k, v, nkk, b, dy, st, st)


TOKEN_TILE = 256


def _padded_weights(wt):
    f = lambda a: a.astype(F32)
    w_in = f(wt["w_in"][0])
    zeros = lambda r, c: jnp.zeros((r, c), F32)
    wp = jnp.concatenate([w_in[:, :MLA_COLS], zeros(D_MODEL, PM_W - MLA_COLS), w_in[:, MLA_COLS:]], axis=1)
    w_uq = f(wt["mla_w_uq"][0]).reshape(Q_LORA, MLA_HEADS, MLA_NOPE + MLA_ROPE)
    wqn = w_uq[:, :, :MLA_NOPE].reshape(Q_LORA, MLA_HEADS * MLA_NOPE)
    wqr = jnp.concatenate([w_uq[:, :, MLA_NOPE:], jnp.zeros((Q_LORA, MLA_HEADS, LANES - MLA_ROPE), F32)], axis=2)
    wqr = wqr.reshape(Q_LORA, MLA_HEADS * LANES)
    w2p = jnp.concatenate([f(wt["rw_w2"][0]), zeros(LORA, RW_WIDTH)], axis=0)
    a2p = jnp.concatenate([zeros(LORA, RW_WIDTH), f(wt["rw_a2"][0])], axis=0)
    bw = (f(wt["mla_q_norm_g"]), wqn, wqr, f(wt["mla_kv_norm_g"]), f(wt["mla_w_ukv"][0]), f(wt["rw_mu"]), f(wt["rw_w0"]),
          w2p, f(wt["rw_a0"]), a2p, f(wt["rw_k_k"]), f(wt["rw_k_a"]))
    hw = (f(wt["rw_ln_g"]), f(wt["rw_ln_b"]), f(wt["rw_r_k"]).reshape(1, RW_WIDTH), f(wt["w_out"][0]), f(wt["norm_post_g"]))
    return wp, bw, hw


def _local_step(x, positions, target, wt):
    bsz, t, _ = x.shape
    n = bsz * t
    tm = min(TOKEN_TILE, t)
    tps = t // tm
    wp, bw, hw = _padded_weights(wt)
    wpb = wp.astype(BF16)
    g_pre = wt["norm_pre_g"].astype(F32)
    x2 = x.reshape(n, D_MODEL)
    tgt2 = target.reshape(n, D_MODEL)
    inv_freq = ROPE_THETA ** (-jnp.arange(0, MLA_ROPE, 2, dtype=F32) / MLA_ROPE)
    invf = jnp.tile(inv_freq, LANES // (MLA_ROPE // 2)).reshape(1, LANES)
    cos, sin = _rope_tables(positions.astype(F32).reshape(n, 1), invf, tm)

    u, pm, prw, z = _fwd_a(x2, g_pre, wpb, tm)
    qn, qr, kv, kr, r, w, k, v, nkk, b = _fwd_b(pm, prw, cos, sin, bw, tm, tps)
    b3 = lambda a: a.reshape(bsz, t, a.shape[-1])
    ym, lse = _attn_fwd(b3(qn), b3(qr), b3(kv), b3(kr))
    ys, st = _scan_fwd(b3(r), b3(w), b3(k), b3(v), b3(nkk), b3(b))
    (dys, dr_h, dk_h, dv_h, dym, dz, dxres, loss, d_lng, d_lnb, d_rk, d_wout, d_gpost) = _head(
        ys.reshape(n, RW_WIDTH), r, k, v, ym.reshape(n, MLA_WIDTH), z, x2, tgt2, hw, tm)
    dqn, dqr, dkv, dkr_heads = _attn_bwd(b3(qn), b3(qr), b3(kv), b3(kr), ym, b3(dym), lse)
    dr_s, dw_s, dk_s, dv_s, dn_s, db_s = _scan_bwd(b3(r), b3(w), b3(k), b3(v), b3(nkk), b3(b), st, b3(dys))
    f2 = lambda a: a.reshape(n, a.shape[-1])
    cts = (f2(dqn), f2(dqr), f2(dkv), f2(dr_s), dr_h, f2(dw_s), f2(dk_s), dk_h, f2(dv_s), dv_h, f2(dn_s), f2(db_s))
    (dpm, dprw, dps, d_gq, d_wqn, d_wqr, d_gkv, d_wkv, d_mu, d_w0, d_w2p, d_a0, d_a2p, d_kk, d_ka) = _bwd_b(
        pm, prw, cos, sin, bw, cts, dkr_heads.reshape(MLA_HEADS, n, LANES), tm, tps)
    grad_x, dpb, d_gpre = _bwd_a(x2, g_pre, wpb, dpm, dprw, dps, dz, dxres, tm, tps)
    d_wp = _dw_in(u, dpb, min(1024, n), 640)

    d_w_in = jnp.concatenate([d_wp[:, :MLA_COLS], d_wp[:, PM_W:]], axis=1)
    d_w_uq = jnp.concatenate([d_wqn.reshape(Q_LORA, MLA_HEADS, MLA_NOPE),
                              d_wqr.reshape(Q_LORA, MLA_HEADS, LANES)[:, :, :MLA_ROPE]], axis=2)
    grads = {
        "norm_pre_g": d_gpre, "w_in": d_w_in[None], "mla_q_norm_g": d_gq,
        "mla_w_uq": d_w_uq.reshape(1, Q_LORA, MLA_HEADS * (MLA_NOPE + MLA_ROPE)), "mla_kv_norm_g": d_gkv,
        "mla_w_ukv": d_wkv[None], "rw_mu": d_mu, "rw_w0": d_w0, "rw_w2": d_w2p[None, :LORA], "rw_a0": d_a0,
        "rw_a2": d_a2p[None, LORA:], "rw_k_k": d_kk, "rw_k_a": d_ka, "rw_r_k": d_rk.reshape(1, RW_HEADS, RW_HEAD),
        "rw_ln_g": d_lng, "rw_ln_b": d_lnb, "w_out": d_wout[None], "norm_post_g": d_gpost,
    }
    return loss, grad_x.reshape(bsz, t, D_MODEL), grads


_MESH = pl.DeviceIdType.MESH


def _gather_shards(packed):
    rows, lanes = packed.shape

    def body(x_ref, out_ref, send_sems, recv_sems, local_sem):
        x, y, c = lax.axis_index("x"), lax.axis_index("y"), lax.axis_index("c")
        me, sibling = (x, y, c), (x, y, 1 - c)
        chips = [(1 - x, y), (x, 1 - y), (1 - x, 1 - y)]

        def slot(px, py, pc):
            return out_ref.at[4 * px + 2 * py + pc]

        def copy(k, block, to, src=None):
            return pltpu.make_async_remote_copy(
                src_ref=slot(*block) if src is None else src, dst_ref=slot(*block),
                send_sem=send_sems.at[k], recv_sem=recv_sems.at[k], device_id=to, device_id_type=_MESH)

        mine = pltpu.make_async_copy(x_ref, slot(*me), local_sem)
        mine.start()
        first = [copy(0, me, sibling, src=x_ref)]
        first += [copy(1 + j, me, (*chip, c), src=x_ref) for j, chip in enumerate(chips)]
        for cp in first:
            cp.start()
        passed = [copy(4 + j, (*chip, c), sibling) for j, chip in enumerate(chips)]
        for j, chip in enumerate(chips):
            copy(1 + j, (*chip, c), me).wait_recv()
            passed[j].start()
        copy(0, sibling, me).wait_recv()
        for j, chip in enumerate(chips):
            copy(4 + j, (*chip, 1 - c), me).wait_recv()
        for cp in first + passed:
            cp.wait_send()
        mine.wait()

    return pl.pallas_call(
        body, name="gather_shards",
        out_shape=_sds((N_DEV, rows, lanes), packed.dtype),
        in_specs=[pl.BlockSpec(memory_space=pltpu.VMEM)],
        out_specs=pl.BlockSpec(memory_space=pltpu.VMEM),
        scratch_shapes=[pltpu.SemaphoreType.DMA((7,)), pltpu.SemaphoreType.DMA((7,)), pltpu.SemaphoreType.DMA],
    )(packed)


def _exchange_grads(gbig, gsmall):
    _, rows, lanes = gbig.shape

    def body(big_ref, small_ref, rbig_ref, rsmall_ref, send_b, recv_b, send_s, recv_s, local_sem):
        x, y, c = lax.axis_index("x"), lax.axis_index("y"), lax.axis_index("c")
        me_lin = 4 * x + 2 * y + c
        mine = pltpu.make_async_copy(big_ref.at[me_lin], rbig_ref.at[0], local_sem)
        mine.start()
        rsmall_ref[me_lin] = small_ref[...]
        copies = []
        for k in range(1, N_DEV):
            px, py, pc = x ^ (k >> 2), y ^ ((k >> 1) & 1), c ^ (k & 1)
            peer = (px, py, pc)
            copies.append(pltpu.make_async_remote_copy(
                src_ref=big_ref.at[4 * px + 2 * py + pc], dst_ref=rbig_ref.at[k],
                send_sem=send_b.at[k - 1], recv_sem=recv_b.at[k - 1], device_id=peer, device_id_type=_MESH))
            copies.append(pltpu.make_async_remote_copy(
                src_ref=small_ref, dst_ref=rsmall_ref.at[me_lin],
                send_sem=send_s.at[k - 1], recv_sem=recv_s.at[k - 1], device_id=peer, device_id_type=_MESH))
        for cp in copies:
            cp.start()
        for cp in copies:
            cp.wait_recv()
        for cp in copies:
            cp.wait_send()
        mine.wait()

    return pl.pallas_call(
        body, name="exchange_grads",
        out_shape=[_sds((N_DEV, rows, lanes)), _sds((N_DEV, SMALL_ROWS, lanes))],
        in_specs=[pl.BlockSpec(memory_space=pl.ANY), pl.BlockSpec(memory_space=pltpu.VMEM)],
        out_specs=[pl.BlockSpec(memory_space=pl.ANY), pl.BlockSpec(memory_space=pltpu.VMEM)],
        scratch_shapes=[pltpu.SemaphoreType.DMA((7,)), pltpu.SemaphoreType.DMA((7,)), pltpu.SemaphoreType.DMA((7,)),
                        pltpu.SemaphoreType.DMA((7,)), pltpu.SemaphoreType.DMA],
    )(gbig, gsmall)


def _adamw_math(w, g, m, v):
    m = ADAM_B1 * m + (1.0 - ADAM_B1) * g
    v = ADAM_B2 * v + (1.0 - ADAM_B2) * (g * g)
    m_hat = m / (1.0 - ADAM_B1 ** ADAM_STEP)
    v_hat = v / (1.0 - ADAM_B2 ** ADAM_STEP)
    return -ADAM_LR * (m_hat / (jnp.sqrt(v_hat) + ADAM_EPS) + ADAM_WD * w), m, v


def _reduce_adamw(parts, w, m, v, rb):
    _, rows, lanes = parts.shape

    def body(p_ref, w_ref, m_ref, v_ref, g_out, d_out, m_out, v_out):
        g = p_ref[0]
        for s in range(1, N_DEV):
            g = g + p_ref[s]
        g_out[...] = g
        d_out[...], m_out[...], v_out[...] = _adamw_math(w_ref[...], g, m_ref[...], v_ref[...])

    blk = pl.BlockSpec((rb, lanes), lambda i: (i, 0))
    return pl.pallas_call(
        body, name=f"reduce_adamw_{rows}", grid=(rows // rb,),
        in_specs=[pl.BlockSpec((N_DEV, rb, lanes), lambda i: (0, i, 0)), blk, blk, blk],
        out_specs=[blk] * 4, out_shape=[_sds((rows, lanes))] * 4,
        compiler_params=_ARB1,
    )(parts, w, m, v)


def _pack_rows(arrs):
    return jnp.concatenate([a.reshape(-1, LANES) for a in arrs], axis=0)


def _shard_blocks(name, full):
    a = full[0]
    if name == "w_out":
        return a.reshape(N_DEV, -1, LANES)
    rows, cols = a.shape
    return a.reshape(rows, N_DEV, cols // N_DEV).transpose(1, 0, 2).reshape(N_DEV, -1, LANES)


def _unshard(name, blocks, shard_shape):
    _, rows, cols = shard_shape
    a = blocks.reshape(N_DEV, rows, cols)
    if name == "w_out":
        return a.reshape(1, N_DEV * rows, cols)
    return a.transpose(1, 0, 2).reshape(1, rows, N_DEV * cols)


def _small_pack(vals, tail):
    flat = jnp.concatenate([vals[nm].reshape(-1) for nm, _ in SMALL] + [tail])
    return jnp.pad(flat, (0, SMALL_ROWS * LANES - flat.shape[0])).reshape(SMALL_ROWS, LANES)


def _small_unpack(packed, like):
    flat, out, off = packed.reshape(-1), {}, 0
    for nm, cnt in SMALL:
        out[nm] = flat[off:off + cnt].reshape(like[nm].shape)
        off += cnt
    return out


def _big_unpack(packed, like):
    out, off = {}, 0
    for nm, cnt in SHARD_ROWS:
        out[nm] = packed[off:off + cnt].reshape(like[nm].shape)
        off += cnt
    return out


def kernel(x, positions, norm_pre_g, w_in, mla_q_norm_g, mla_w_uq, mla_kv_norm_g, mla_w_ukv, rw_mu, rw_w0, rw_w2, rw_a0, rw_a2, rw_k_k, rw_k_a, rw_r_k, rw_ln_g, rw_ln_b, w_out, norm_post_g, loss_target, m_norm_pre_g, m_w_in, m_mla_q_norm_g, m_mla_w_uq, m_mla_kv_norm_g, m_mla_w_ukv, m_rw_mu, m_rw_w0, m_rw_w2, m_rw_a0, m_rw_a2, m_rw_k_k, m_rw_k_a, m_rw_r_k, m_rw_ln_g, m_rw_ln_b, m_w_out, m_norm_post_g, v_norm_pre_g, v_w_in, v_mla_q_norm_g, v_mla_w_uq, v_mla_kv_norm_g, v_mla_w_ukv, v_rw_mu, v_rw_w0, v_rw_w2, v_rw_a0, v_rw_a2, v_rw_k_k, v_rw_k_a, v_rw_r_k, v_rw_ln_g, v_rw_ln_b, v_w_out, v_norm_post_g):
    given = dict(locals())
    w = {nm: given[nm] for nm in WEIGHTS}
    mom = {nm: given["m_" + nm] for nm in WEIGHTS}
    var = {nm: given["v_" + nm] for nm in WEIGHTS}
    sharded = [nm for nm, _ in SHARD_ROWS]

    gathered = _gather_shards(_pack_rows([w[nm] for nm in sharded]).astype(BF16))
    full, off = dict(w), 0
    for nm, cnt in SHARD_ROWS:
        full[nm] = _unshard(nm, gathered[:, off:off + cnt], w[nm].shape)
        off += cnt

    loss_part, grad_x, grads = _local_step(x, positions, loss_target, full)

    gbig = jnp.concatenate([_shard_blocks(nm, grads[nm]) for nm in sharded], axis=1)
    gsmall = _small_pack(grads, loss_part[0, 0:1])
    parts_big, parts_small = _exchange_grads(gbig, gsmall)

    g_b, d_b, m_b, v_b = _reduce_adamw(parts_big, _pack_rows([w[nm] for nm in sharded]), _pack_rows([mom[nm] for nm in sharded]),
                                       _pack_rows([var[nm] for nm in sharded]), PACK_ROWS // 8)
    zero1 = jnp.zeros((1,), F32)
    g_s, d_s, m_s, v_s = _reduce_adamw(parts_small, _small_pack(w, zero1), _small_pack(mom, zero1), _small_pack(var, zero1),
                                       SMALL_ROWS)
    loss = g_s.reshape(-1)[SMALL_N]
    outs = []
    for big, small in ((g_b, g_s), (d_b, d_s), (m_b, m_s), (v_b, v_s)):
        vals = {**_big_unpack(big, w), **_small_unpack(small, w)}
        outs += [vals[nm] for nm in WEIGHTS]
    return (loss, grad_x, *outs)
```

```python
import functools

import jax
import jax.numpy as jnp
from jax import lax
from jax.experimental import pallas as pl
from jax.experimental.pallas import tpu as pltpu

F32 = jnp.float32
BF16 = jnp.bfloat16

D_MODEL = 1024
MLA_HEADS = 4
MLA_NOPE = 128
MLA_ROPE = 64
MLA_V = 128
MLA_WIDTH = MLA_HEADS * MLA_V
Q_LORA = 256
KV_LORA = 128
ROPE_THETA = 10000.0
RW_HEAD = 64
RW_WIDTH = 512
RW_HEADS = RW_WIDTH // RW_HEAD
LORA = 64
RW_COLS = 3 * RW_WIDTH + 2 * LORA
MLA_COLS = Q_LORA + KV_LORA + MLA_ROPE
D_IN = MLA_COLS + RW_COLS + D_MODEL
RW_GN_EPS = 64e-5
NORM_EPS = 1e-6
ATT_SCALE = (MLA_NOPE + MLA_ROPE) ** -0.5
ADAM_LR, ADAM_B1, ADAM_B2, ADAM_EPS, ADAM_WD, ADAM_STEP = 0.001, 0.9, 0.999, 1e-08, 0.01, 10
N_DEV = 8
LANES = 128
MXU = 256

PM_W = 512
WP_COLS = PM_W + RW_COLS + D_MODEL
RW_PIECES = ((0, 512), (512, 1024), (1024, 1536), (1536, 1664))

SHARD_ROWS = (("w_in", 1024 * 392 // LANES), ("mla_w_uq", 256 * 96 // LANES), ("mla_w_ukv", 128 * 128 // LANES),
              ("rw_w2", 64 * 64 // LANES), ("rw_a2", 64 * 64 // LANES), ("w_out", 128 * 1024 // LANES))
PACK_ROWS = sum(r for _, r in SHARD_ROWS)
SMALL = (("norm_pre_g", 1024), ("mla_q_norm_g", 256), ("mla_kv_norm_g", 128), ("rw_mu", 1664), ("rw_w0", 512),
         ("rw_a0", 512), ("rw_k_k", 512), ("rw_k_a", 512), ("rw_r_k", 512), ("rw_ln_g", 512), ("rw_ln_b", 512),
         ("norm_post_g", 1024))
SMALL_N = sum(n for _, n in SMALL)
SMALL_ROWS = 72
WEIGHTS = ("norm_pre_g", "w_in", "mla_q_norm_g", "mla_w_uq", "mla_kv_norm_g", "mla_w_ukv", "rw_mu", "rw_w0", "rw_w2",
           "rw_a0", "rw_a2", "rw_k_k", "rw_k_a", "rw_r_k", "rw_ln_g", "rw_ln_b", "w_out", "norm_post_g")


def _seg_ones():
    r = lax.broadcasted_iota(jnp.int32, (MXU, MXU), 0) >> 6
    c = lax.broadcasted_iota(jnp.int32, (MXU, MXU), 1) >> 6
    return jnp.where(r == c, 1.0, 0.0).astype(BF16)


def _seg_dot(x, ones, passes):
    parts, rem = [], x
    for p in range(passes):
        hb = rem.astype(BF16)
        parts.append(hb)
        if p + 1 < passes:
            rem = rem - hb.astype(F32)
    outs = []
    for j in range(x.shape[1] // MXU):
        acc = None
        for hb in parts:
            d = jnp.dot(hb[:, MXU * j:MXU * (j + 1)], ones, preferred_element_type=F32)
            acc = d if acc is None else acc + d
        outs.append(acc)
    return outs[0] if len(outs) == 1 else jnp.concatenate(outs, axis=1)


def _seg_multi(xs, ones, passes):
    his = [x.astype(BF16) for x in xs]
    hi = jnp.concatenate(his, axis=0)
    if passes == 2:
        lo = jnp.concatenate([(x - h.astype(F32)).astype(BF16) for x, h in zip(xs, his)], axis=0)
        rhs = jnp.concatenate([ones, ones], axis=0)
    halves = []
    for j in range(hi.shape[1] // MXU):
        sl = slice(MXU * j, MXU * (j + 1))
        if passes == 2:
            halves.append(jnp.dot(jnp.concatenate([hi[:, sl], lo[:, sl]], axis=1), rhs, preferred_element_type=F32))
        else:
            halves.append(jnp.dot(hi[:, sl], ones, preferred_element_type=F32))
    full = jnp.concatenate(halves, axis=1)
    m = xs[0].shape[0]
    return [full[m * i:m * (i + 1)] for i in range(len(xs))]


@jax.custom_vjp
def _segsum(x):
    return _seg_dot(x, _seg_ones(), 3)


_segsum.defvjp(lambda x: (_segsum(x), None), lambda _, g: (_segsum(g),))


@jax.custom_vjp
def _bdot(a, w):
    return jnp.dot(a.astype(BF16), w.astype(BF16), preferred_element_type=F32)


def _bdot_fwd(a, w):
    return _bdot(a, w), (a, w)


def _bdot_bwd(res, g):
    a, w = res
    gb = g.astype(BF16)
    da = lax.dot_general(gb, w.astype(BF16), (((1,), (1,)), ((), ())), preferred_element_type=F32)
    dw = lax.dot_general(a.astype(BF16), gb, (((0,), (0,)), ((), ())), preferred_element_type=F32)
    return da, dw


_bdot.defvjp(_bdot_fwd, _bdot_bwd)


def _rot_impl(x):
    w = x.shape[1]
    lane = lax.broadcasted_iota(jnp.int32, x.shape, 1)
    return jnp.where((lane & 63) < 32, -pltpu.roll(x, w - 32, 1), pltpu.roll(x, 32, 1))


@jax.custom_vjp
def _rot(x):
    return _rot_impl(x)


_rot.defvjp(lambda x: (_rot_impl(x), None), lambda _, g: (-_rot_impl(g),))


def _rms(x, g):
    return x * lax.rsqrt(jnp.mean(x * x, axis=-1, keepdims=True) + NORM_EPS) * g


def _shift_rows(p, prev_row):
    row = lax.broadcasted_iota(jnp.int32, p.shape, 0)
    return jnp.where(row == 0, prev_row, pltpu.roll(p, 1, 0))


def _unshift_rows(g, next_row):
    row = lax.broadcasted_iota(jnp.int32, g.shape, 0)
    return jnp.where(row == g.shape[0] - 1, next_row, pltpu.roll(g, g.shape[0] - 1, 0))


def _f_mla(cq, ckv, kr, cos, sin, g_q, wqn, wqr, g_kv, wkv):
    qn = _rms(cq, g_q)
    q_nope = _bdot(qn, wqn)
    q_r = _bdot(qn, wqr)
    cos4 = jnp.concatenate([cos] * MLA_HEADS, axis=1)
    sin4 = jnp.concatenate([sin] * MLA_HEADS, axis=1)
    q_rope = q_r * cos4 + _rot(q_r) * sin4
    kv = _bdot(_rms(ckv, g_kv), wkv)
    k_rope = kr * cos + _rot(kr) * sin
    return q_nope, q_rope, kv, k_rope


def _f_rw(pr, pk, pv, pt, sr, sk, sv, st, mu_r, mu_k, mu_v, mu_t, w0, w2p, a0, a2p, k_k, k_a):
    r = pr + (sr - pr) * mu_r
    k = pk + (sk - pk) * mu_k
    v = pv + (sv - pv) * mu_v
    t = pt + (st - pt) * mu_t
    nwl = -(w0 + _bdot(jnp.tanh(t), w2p))
    softplus = jnp.maximum(nwl, 0.0) + jnp.log(1.0 + jnp.exp(-jnp.abs(nwl)))
    decay = jnp.exp(-jnp.exp(-softplus - 0.5))
    a = jax.nn.sigmoid(a0 + _bdot(t, a2p))
    kk = k * k_k
    kk = kk / jnp.maximum(jnp.sqrt(_segsum(kk * kk)), 1e-12)
    k2 = k * (1.0 + (a - 1.0) * k_a)
    return r, decay, k2, v, -kk, kk * a


def _f_head(ys, r, k, v, ym, z1, z2, x, tgt, ln_g, ln_b, r_k, w1, w2, g_post):
    inv = 1.0 / RW_HEAD
    yc = ys - _segsum(ys) * inv
    var = _segsum(yc * yc) * inv
    y = yc * lax.rsqrt(var + RW_GN_EPS) * ln_g + ln_b
    y_rw = y + _segsum(r * k * r_k) * v
    c1 = ym * (z1 * jax.nn.sigmoid(z1))
    c2 = y_rw * (z2 * jax.nn.sigmoid(z2))
    out = _bdot(c1, w1) + _bdot(c2, w2)
    err = x + _rms(out, g_post) - tgt
    per_row = jnp.sum(err * err, axis=1, keepdims=True)
    return jnp.sum(per_row, axis=0, keepdims=True) * (0.5 / D_MODEL)


def _rows(tm, width):
    return pl.BlockSpec((tm, width), lambda i: (i, 0))


def _whole(shape):
    zeros = (0,) * len(shape)
    return pl.BlockSpec(shape, lambda i: zeros)


def _sds(shape, dtype=F32):
    return jax.ShapeDtypeStruct(shape, dtype)


_ARB1 = pltpu.CompilerParams(dimension_semantics=("arbitrary",))


def _acc(ref, val, first):
    @pl.when(first)
    def _():
        ref[...] = val

    @pl.when(jnp.logical_not(first))
    def _():
        ref[...] += val


def _fwd_a(x2, g_pre, wp, tm):
    n = x2.shape[0]

    def body(x_ref, g_ref, w_ref, u_ref, pm_ref, prw_ref, z_ref):
        ub = _rms(x_ref[...], g_ref[...]).astype(BF16)
        u_ref[...] = ub
        pm_ref[...] = jnp.dot(ub, w_ref[:, 0:PM_W], preferred_element_type=F32)
        prw_ref[...] = jnp.dot(ub, w_ref[:, PM_W:PM_W + RW_COLS], preferred_element_type=F32)
        z_ref[...] = jnp.dot(ub, w_ref[:, PM_W + RW_COLS:WP_COLS], preferred_element_type=F32)

    return pl.pallas_call(
        body, name="fwd_a", grid=(n // tm,),
        in_specs=[_rows(tm, D_MODEL), _whole((1, D_MODEL)), _whole((D_MODEL, WP_COLS))],
        out_specs=[_rows(tm, D_MODEL), _rows(tm, PM_W), _rows(tm, RW_COLS), _rows(tm, D_MODEL)],
        out_shape=[_sds((n, D_MODEL), BF16), _sds((n, PM_W)), _sds((n, RW_COLS)), _sds((n, D_MODEL))],
        compiler_params=_ARB1,
    )(x2, g_pre, wp)


def _rope_tables(posf, invf, tm):
    n = posf.shape[0]

    def body(p_ref, f_ref, c_ref, s_ref):
        ang = p_ref[...] * f_ref[...]
        c_ref[...] = jnp.cos(ang)
        s_ref[...] = jnp.sin(ang)

    return pl.pallas_call(
        body, name="rope_tables", grid=(n // tm,),
        in_specs=[_rows(tm, 1), _whole((1, LANES))],
        out_specs=[_rows(tm, LANES), _rows(tm, LANES)],
        out_shape=[_sds((n, LANES)), _sds((n, LANES))],
        compiler_params=_ARB1,
    )(posf, invf)


_B_WEIGHT_SHAPES = ((1, Q_LORA), (Q_LORA, 512), (Q_LORA, 512), (1, KV_LORA), (KV_LORA, 1024), (1, RW_COLS), (1, RW_WIDTH),
                    (LANES, RW_WIDTH), (1, RW_WIDTH), (LANES, RW_WIDTH), (1, RW_WIDTH), (1, RW_WIDTH))


def _halo_prev(tm):
    return pl.BlockSpec((8, RW_COLS), lambda i: (jnp.maximum(i * (tm // 8) - 1, 0), 0))


def _b_operands(pm_ref, prw_ref, halo_ref, wrefs, tile, tiles_per_seq):
    g_q, wqn, wqr, g_kv, wkv, mu, w0, w2p, a0, a2p, k_k, k_a = wrefs
    mla_in = (pm_ref[:, 0:Q_LORA], pm_ref[:, Q_LORA:Q_LORA + KV_LORA], pm_ref[:, Q_LORA + KV_LORA:PM_W])
    mla_w = (g_q[...], wqn[...], wqr[...], g_kv[...], wkv[...])
    keep = jnp.where(tile % tiles_per_seq == 0, 0.0, 1.0)
    prev = halo_ref[7:8, :] * keep
    ps = tuple(prw_ref[:, a:b] for a, b in RW_PIECES)
    ss = tuple(_shift_rows(p, prev[:, a:b]) for p, (a, b) in zip(ps, RW_PIECES))
    rw_w = tuple(mu[:, a:b] for a, b in RW_PIECES) + (w0[...], w2p[...], a0[...], a2p[...], k_k[...], k_a[...])
    return mla_in, mla_w, ps + ss, rw_w


def _fwd_b(pm, prw, cos, sin, bw, tm, tiles_per_seq):
    n = pm.shape[0]

    def body(pm_ref, prw_ref, halo_ref, cos_ref, sin_ref, *refs):
        wrefs, outs = refs[:12], refs[12:]
        mla_in, mla_w, rw_in, rw_w = _b_operands(pm_ref, prw_ref, halo_ref, wrefs, pl.program_id(0), tiles_per_seq)
        res = _f_mla(*mla_in, cos_ref[...], sin_ref[...], *mla_w) + _f_rw(*rw_in, *rw_w)
        for o_ref, val in zip(outs, res):
            o_ref[...] = val

    widths = (512, 512, 1024, LANES) + (RW_WIDTH,) * 6
    return pl.pallas_call(
        body, name="fwd_b", grid=(n // tm,),
        in_specs=[_rows(tm, PM_W), _rows(tm, RW_COLS), _halo_prev(tm), _rows(tm, LANES), _rows(tm, LANES)]
        + [_whole(s) for s in _B_WEIGHT_SHAPES],
        out_specs=[_rows(tm, w) for w in widths],
        out_shape=[_sds((n, w)) for w in widths],
        compiler_params=_ARB1,
    )(pm, prw, prw, cos, sin, *bw)


def _bwd_b(pm, prw, cos, sin, bw, cts, dkr_heads, tm, tiles_per_seq):
    n = pm.shape[0]

    ct_widths = (512, 512, 1024) + (RW_WIDTH,) * 9
    n_ct = len(ct_widths)

    def body(pm_ref, prw_ref, halo_ref, cos_ref, sin_ref, *refs):
        wrefs, ct_refs, dkr_ref = refs[:12], refs[12:12 + n_ct], refs[12 + n_ct]
        dpm_ref, dprw_ref, dps_ref = refs[13 + n_ct:16 + n_ct]
        wg_refs = refs[16 + n_ct:]
        tile = pl.program_id(0)
        first = tile == 0
        mla_in, mla_w, rw_in, rw_w = _b_operands(pm_ref, prw_ref, halo_ref, wrefs, tile, tiles_per_seq)
        cos, sin = cos_ref[...], sin_ref[...]
        ct = [r[...] for r in ct_refs]
        _, vjp_mla = jax.vjp(lambda *a: _f_mla(*a[:3], cos, sin, *a[3:]), *mla_in, *mla_w)
        dkr = dkr_ref[0] + dkr_ref[1] + dkr_ref[2] + dkr_ref[3]
        d_mla = vjp_mla((ct[0], ct[1], ct[2], dkr))
        dpm_ref[:, 0:Q_LORA] = d_mla[0]
        dpm_ref[:, Q_LORA:Q_LORA + KV_LORA] = d_mla[1]
        dpm_ref[:, Q_LORA + KV_LORA:PM_W] = d_mla[2]
        _, vjp_rw = jax.vjp(_f_rw, *rw_in, *rw_w)
        d_rw = vjp_rw((ct[3] + ct[4], ct[5], ct[6] + ct[7], ct[8] + ct[9], ct[10], ct[11]))
        for j, (a, b) in enumerate(RW_PIECES):
            dprw_ref[:, a:b] = d_rw[j]
            dps_ref[:, a:b] = d_rw[4 + j]
        g_q, wqn, wqr, g_kv, wkv, mu, w0, w2p, a0, a2p, k_k, k_a = wg_refs
        for ref, val in zip((g_q, wqn, wqr, g_kv, wkv), d_mla[3:]):
            _acc(ref, val, first)
        for j, (a, b) in enumerate(RW_PIECES):
            _acc(mu.at[:, a:b], d_rw[8 + j], first)
        for ref, val in zip((w0, w2p, a0, a2p, k_k, k_a), d_rw[12:]):
            _acc(ref, val, first)

    return pl.pallas_call(
        body, name="bwd_b", grid=(n // tm,),
        in_specs=[_rows(tm, PM_W), _rows(tm, RW_COLS), _halo_prev(tm), _rows(tm, LANES), _rows(tm, LANES)]
        + [_whole(s) for s in _B_WEIGHT_SHAPES] + [_rows(tm, w) for w in ct_widths]
        + [pl.BlockSpec((MLA_HEADS, tm, LANES), lambda i: (0, i, 0))],
        out_specs=[_rows(tm, PM_W), _rows(tm, RW_COLS), _rows(tm, RW_COLS)] + [_whole(s) for s in _B_WEIGHT_SHAPES],
        out_shape=[_sds((n, PM_W)), _sds((n, RW_COLS)), _sds((n, RW_COLS))] + [_sds(s) for s in _B_WEIGHT_SHAPES],
        compiler_params=_ARB1,
    )(pm, prw, prw, cos, sin, *bw, *cts, dkr_heads)


def _head(ys, r, k, v, ym, z, x2, tgt, hw, tm):
    n = x2.shape[0]
    h_shapes = ((1, RW_WIDTH), (1, RW_WIDTH), (1, RW_WIDTH), (D_MODEL, D_MODEL), (1, D_MODEL))

    def body(ys_ref, r_ref, k_ref, v_ref, ym_ref, z_ref, x_ref, t_ref, lng, lnb, rk, wout, gpost,
             dys_ref, dr_ref, dk_ref, dv_ref, dym_ref, dz_ref, dx_ref, loss_ref, dlng, dlnb, drk, dwout, dgpost):
        first = pl.program_id(0) == 0
        tgt_v = t_ref[...]
        args = (ys_ref[...], r_ref[...], k_ref[...], v_ref[...], ym_ref[...], z_ref[:, 0:MLA_WIDTH], z_ref[:, MLA_WIDTH:D_MODEL],
                x_ref[...], lng[...], lnb[...], rk[...], wout[0:MLA_WIDTH, :], wout[MLA_WIDTH:D_MODEL, :], gpost[...])
        loss, vjp = jax.vjp(lambda *a: _f_head(*a[:8], tgt_v, *a[8:]), *args)
        d = vjp(jnp.ones((1, 1), F32))
        dys_ref[...] = d[0]
        dr_ref[...] = d[1]
        dk_ref[...] = d[2]
        dv_ref[...] = d[3]
        dym_ref[...] = d[4]
        dz_ref[:, 0:MLA_WIDTH] = d[5]
        dz_ref[:, MLA_WIDTH:D_MODEL] = d[6]
        dx_ref[...] = d[7]
        _acc(loss_ref, jnp.broadcast_to(loss, (8, LANES)), first)
        _acc(dlng, d[8], first)
        _acc(dlnb, d[9], first)
        _acc(drk, d[10], first)
        _acc(dwout.at[0:MLA_WIDTH, :], d[11], first)
        _acc(dwout.at[MLA_WIDTH:D_MODEL, :], d[12], first)
        _acc(dgpost, d[13], first)

    widths = (RW_WIDTH,) * 4 + (MLA_WIDTH, D_MODEL, D_MODEL)
    return pl.pallas_call(
        body, name="head", grid=(n // tm,),
        in_specs=[_rows(tm, RW_WIDTH)] * 4 + [_rows(tm, MLA_WIDTH), _rows(tm, D_MODEL), _rows(tm, D_MODEL), _rows(tm, D_MODEL)]
        + [_whole(s) for s in h_shapes],
        out_specs=[_rows(tm, w) for w in widths] + [_whole((8, LANES))] + [_whole(s) for s in h_shapes],
        out_shape=[_sds((n, w)) for w in widths] + [_sds((8, LANES))] + [_sds(s) for s in h_shapes],
        compiler_params=_ARB1,
    )(ys, r, k, v, ym, z, x2, tgt, *hw)


def _halo_next(tm, n):
    last = n // 8 - 1
    return pl.BlockSpec((8, RW_COLS), lambda i: (jnp.minimum((i + 1) * (tm // 8), last), 0))


def _bwd_a(x2, g_pre, wp, dpm, dprw, dps, dz, dxres, tm, tiles_per_seq):
    n = x2.shape[0]
    nt_dims = (((1,), (1,)), ((), ()))

    def body(x_ref, g_ref, w_ref, dpm_ref, dprw_ref, dps_ref, nxt_ref, dz_ref, dxres_ref, gx_ref, dpb_ref, dg_ref):
        tile = pl.program_id(0)
        keep = jnp.where((tile + 1) % tiles_per_seq == 0, 0.0, 1.0)
        dprw_v = dprw_ref[...] + _unshift_rows(dps_ref[...], nxt_ref[0:1, :] * keep)
        dpm_b, dprw_b, dz_b = dpm_ref[...].astype(BF16), dprw_v.astype(BF16), dz_ref[...].astype(BF16)
        dpb_ref[:, 0:PM_W] = dpm_b
        dpb_ref[:, PM_W:PM_W + RW_COLS] = dprw_b
        dpb_ref[:, PM_W + RW_COLS:WP_COLS] = dz_b
        du = (lax.dot_general(dpm_b, w_ref[:, 0:PM_W], nt_dims, preferred_element_type=F32)
              + lax.dot_general(dprw_b, w_ref[:, PM_W:PM_W + RW_COLS], nt_dims, preferred_element_type=F32)
              + lax.dot_general(dz_b, w_ref[:, PM_W + RW_COLS:WP_COLS], nt_dims, preferred_element_type=F32))
        x = x_ref[...]
        xhat = x * lax.rsqrt(jnp.mean(x * x, axis=-1, keepdims=True) + NORM_EPS)
        dxn = du * g_ref[...]
        dx = (dxn - xhat * jnp.mean(dxn * xhat, axis=-1, keepdims=True)) * lax.rsqrt(jnp.mean(x * x, axis=-1, keepdims=True) + NORM_EPS)
        gx_ref[...] = dx + dxres_ref[...]
        _acc(dg_ref, jnp.sum(du * xhat, axis=0, keepdims=True), tile == 0)

    return pl.pallas_call(
        body, name="bwd_a", grid=(n // tm,),
        in_specs=[_rows(tm, D_MODEL), _whole((1, D_MODEL)), _whole((D_MODEL, WP_COLS)), _rows(tm, PM_W), _rows(tm, RW_COLS),
                  _rows(tm, RW_COLS), _halo_next(tm, n), _rows(tm, D_MODEL), _rows(tm, D_MODEL)],
        out_specs=[_rows(tm, D_MODEL), _rows(tm, WP_COLS), _whole((1, D_MODEL))],
        out_shape=[_sds((n, D_MODEL)), _sds((n, WP_COLS), BF16), _sds((1, D_MODEL))],
        compiler_params=_ARB1,
    )(x2, g_pre, wp, dpm, dprw, dps, dps, dz, dxres)


def _dw_in(u, dpb, tk, tn):
    n = u.shape[0]

    def body(u_ref, d_ref, o_ref):
        val = lax.dot_general(u_ref[...], d_ref[...], (((0,), (0,)), ((), ())), preferred_element_type=F32)
        _acc(o_ref, val, pl.program_id(1) == 0)

    return pl.pallas_call(
        body, name="dw_in", grid=(WP_COLS // tn, n // tk),
        in_specs=[pl.BlockSpec((tk, D_MODEL), lambda j, k: (k, 0)), pl.BlockSpec((tk, tn), lambda j, k: (k, j))],
        out_specs=pl.BlockSpec((D_MODEL, tn), lambda j, k: (0, j)),
        out_shape=_sds((D_MODEL, WP_COLS)),
        compiler_params=pltpu.CompilerParams(dimension_semantics=("arbitrary", "arbitrary")),
    )(u, dpb)


ATT_BLK = 256
_NT = (((1,), (1,)), ((), ()))
_TN = (((0,), (0,)), ((), ()))


def _causal(q0, k0, blk):
    row = q0 + lax.broadcasted_iota(jnp.int32, (blk, blk), 0)
    col = k0 + lax.broadcasted_iota(jnp.int32, (blk, blk), 1)
    return row >= col


def _attn_fwd(qn, qr, kv, kr):
    bsz, t, _ = qn.shape
    blk = min(ATT_BLK, t)

    def body(qn_ref, qr_ref, kn_ref, kr_ref, v_ref, o_ref, lse_ref):
        qi = pl.program_id(2)
        q = jnp.concatenate([qn_ref[...], qr_ref[...]], axis=1).astype(BF16)

        def kv_step(j, carry):
            m, l, acc = carry
            ks = pl.multiple_of(j * blk, blk)
            k = jnp.concatenate([kn_ref[pl.ds(ks, blk), :], kr_ref[pl.ds(ks, blk), :]], axis=1).astype(BF16)
            s = lax.dot_general(q, k, _NT, preferred_element_type=F32) * ATT_SCALE
            s = jnp.where(_causal(qi * blk, j * blk, blk), s, -1e30)
            m_new = jnp.maximum(m, jnp.max(s, axis=1, keepdims=True))
            alpha = jnp.exp(m - m_new)
            p = jnp.exp(s - m_new)
            l = alpha * l + jnp.sum(p, axis=1, keepdims=True)
            acc = alpha * acc + jnp.dot(p.astype(BF16), v_ref[pl.ds(ks, blk), :].astype(BF16), preferred_element_type=F32)
            return m_new, l, acc

        init = (jnp.full((blk, 1), -1e30, F32), jnp.zeros((blk, 1), F32), jnp.zeros((blk, MLA_V), F32))
        m, l, acc = lax.fori_loop(0, qi + 1, kv_step, init)
        o_ref[...] = acc / l
        lse_ref[...] = jnp.broadcast_to(m + jnp.log(l), (blk, LANES))

    return pl.pallas_call(
        body, name="attn_fwd", grid=(bsz, MLA_HEADS, t // blk),
        in_specs=[pl.BlockSpec((None, blk, LANES), lambda b, h, i: (b, i, h)),
                  pl.BlockSpec((None, blk, LANES), lambda b, h, i: (b, i, h)),
                  pl.BlockSpec((None, t, LANES), lambda b, h, i: (b, 0, 2 * h)),
                  pl.BlockSpec((None, t, LANES), lambda b, h, i: (b, 0, 0)),
                  pl.BlockSpec((None, t, LANES), lambda b, h, i: (b, 0, 2 * h + 1))],
        out_specs=[pl.BlockSpec((None, blk, LANES), lambda b, h, i: (b, i, h)),
                   pl.BlockSpec((None, None, blk, LANES), lambda b, h, i: (b, h, i, 0))],
        out_shape=[_sds((bsz, t, MLA_WIDTH)), _sds((bsz, MLA_HEADS, t, LANES))],
        compiler_params=pltpu.CompilerParams(dimension_semantics=("arbitrary", "arbitrary", "arbitrary")),
    )(qn, qr, kv, kr, kv)


def _attn_bwd(qn, qr, kv, kr, o, do, lse):
    bsz, t, _ = qn.shape
    blk = min(ATT_BLK, t)
    nb = t // blk

    def body(qn_ref, qr_ref, kn_ref, kr_ref, v_ref, o_ref, do_ref, lse_ref, dqn_ref, dqr_ref, dkv_ref, dkr_ref, dq_sc, delta_sc):
        dq_sc[...] = jnp.zeros_like(dq_sc)
        delta_sc[...] = jnp.sum(do_ref[...] * o_ref[...], axis=1, keepdims=True)

        def kv_loop(j, _):
            ks = pl.multiple_of(j * blk, blk)
            k = jnp.concatenate([kn_ref[pl.ds(ks, blk), :], kr_ref[pl.ds(ks, blk), :]], axis=1).astype(BF16)
            vb = v_ref[pl.ds(ks, blk), :].astype(BF16)

            def q_loop(i, carry):
                dk, dv = carry
                qs = pl.multiple_of(i * blk, blk)
                q = jnp.concatenate([qn_ref[pl.ds(qs, blk), :], qr_ref[pl.ds(qs, blk), :]], axis=1).astype(BF16)
                dob = do_ref[pl.ds(qs, blk), :].astype(BF16)
                s = lax.dot_general(q, k, _NT, preferred_element_type=F32) * ATT_SCALE
                p = jnp.where(_causal(i * blk, j * blk, blk), jnp.exp(s - lse_ref[pl.ds(qs, blk), 0:1]), 0.0)
                dv = dv + lax.dot_general(p.astype(BF16), dob, _TN, preferred_element_type=F32)
                dp = lax.dot_general(dob, vb, _NT, preferred_element_type=F32)
                ds = (p * (dp - delta_sc[pl.ds(qs, blk), :]) * ATT_SCALE).astype(BF16)
                dq_sc[pl.ds(qs, blk), :] += jnp.dot(ds, k, preferred_element_type=F32)
                dk = dk + lax.dot_general(ds, q, _TN, preferred_element_type=F32)
                return dk, dv

            dk, dv = lax.fori_loop(j, nb, q_loop, (jnp.zeros((blk, 2 * LANES), F32), jnp.zeros((blk, MLA_V), F32)))
            dkv_ref[pl.ds(ks, blk), 0:LANES] = dk[:, 0:LANES]
            dkv_ref[pl.ds(ks, blk), LANES:2 * LANES] = dv
            dkr_ref[pl.ds(ks, blk), :] = dk[:, LANES:2 * LANES]
            return 0

        lax.fori_loop(0, nb, kv_loop, 0)
        dqn_ref[...] = dq_sc[:, 0:LANES]
        dqr_ref[...] = dq_sc[:, LANES:2 * LANES]

    head_col = lambda b, h: (b, 0, h)
    return pl.pallas_call(
        body, name="attn_bwd", grid=(bsz, MLA_HEADS),
        in_specs=[pl.BlockSpec((None, t, LANES), head_col), pl.BlockSpec((None, t, LANES), head_col),
                  pl.BlockSpec((None, t, LANES), lambda b, h: (b, 0, 2 * h)),
                  pl.BlockSpec((None, t, LANES), lambda b, h: (b, 0, 0)),
                  pl.BlockSpec((None, t, LANES), lambda b, h: (b, 0, 2 * h + 1)),
                  pl.BlockSpec((None, t, LANES), head_col), pl.BlockSpec((None, t, LANES), head_col),
                  pl.BlockSpec((None, None, t, LANES), lambda b, h: (b, h, 0, 0))],
        out_specs=[pl.BlockSpec((None, t, LANES), head_col), pl.BlockSpec((None, t, LANES), head_col),
                   pl.BlockSpec((None, t, 2 * LANES), head_col),
                   pl.BlockSpec((None, None, t, LANES), lambda b, h: (h, b, 0, 0))],
        out_shape=[_sds((bsz, t, MLA_WIDTH)), _sds((bsz, t, MLA_WIDTH)), _sds((bsz, t, 2 * MLA_WIDTH)),
                   _sds((MLA_HEADS, bsz, t, LANES))],
        scratch_shapes=[pltpu.VMEM((t, 2 * LANES), F32), pltpu.VMEM((t, 1), F32)],
        compiler_params=pltpu.CompilerParams(dimension_semantics=("arbitrary", "arbitrary")),
    )(qn, qr, kv, kr, kv, o, do, lse)


SCAN_CHUNK = 16


def _diag_mask():
    row = lax.broadcasted_iota(jnp.int32, (RW_HEAD, RW_WIDTH), 0)
    lane = lax.broadcasted_iota(jnp.int32, (RW_HEAD, RW_WIDTH), 1)
    return jnp.where(row == (lane & (RW_HEAD - 1)), 1.0, 0.0)


def _scan_fwd(r, w, k, v, nkk, b):
    bsz, t, _ = r.shape
    tc = min(SCAN_CHUNK, t)

    def body(r_ref, w_ref, k_ref, v_ref, n_ref, b_ref, y_ref, st_ref, s_sc):
        @pl.when(pl.program_id(0) == 0)
        def _():
            s_sc[...] = jnp.zeros_like(s_sc)

        ones = _seg_ones()
        diag = _diag_mask()
        seg = lambda xs, passes: _seg_multi(xs, ones, passes)
        seqs = range(bsz)

        def step(i, _):
            row = lambda ref, bi: ref[bi, pl.ds(i, 1), :]
            s_old = [s_sc[bi] for bi in seqs]
            sa = seg([s_old[bi] * row(n_ref, bi) for bi in seqs], 2)
            vc = seg([diag * row(v_ref, bi) for bi in seqs], 2)
            s_new = [s_old[bi] * row(w_ref, bi) + sa[bi] * row(b_ref, bi) + vc[bi] * row(k_ref, bi) for bi in seqs]
            for bi in seqs:
                s_sc[bi] = s_new[bi]
                st_ref[bi, i] = s_new[bi]
            yb = seg([s_new[bi] * row(r_ref, bi) for bi in seqs], 1)
            for bi in seqs:
                y_ref[bi, pl.ds(i, 1), :] = jnp.sum(yb[bi] * diag, axis=0, keepdims=True)
            return 0

        lax.fori_loop(0, tc, step, 0)

    vec = pl.BlockSpec((bsz, tc, RW_WIDTH), lambda c: (0, c, 0))
    return pl.pallas_call(
        body, name="scan_fwd", grid=(t // tc,),
        in_specs=[vec] * 6,
        out_specs=[vec, pl.BlockSpec((bsz, tc, RW_HEAD, RW_WIDTH), lambda c: (0, c, 0, 0))],
        out_shape=[_sds((bsz, t, RW_WIDTH)), _sds((bsz, t, RW_HEAD, RW_WIDTH))],
        scratch_shapes=[pltpu.VMEM((bsz, RW_HEAD, RW_WIDTH), F32)],
        compiler_params=_ARB1,
    )(r, w, k, v, nkk, b)


def _scan_bwd(r, w, k, v, nkk, b, st, dy):
    bsz, t, _ = r.shape
    tc = min(SCAN_CHUNK, t)
    nc = t // tc

    def body(r_ref, w_ref, k_ref, v_ref, n_ref, b_ref, dy_ref, st_ref, halo_ref,
             dr_ref, dw_ref, dk_ref, dv_ref, dn_ref, db_ref, g_sc):
        c = pl.program_id(0)

        @pl.when(c == 0)
        def _():
            g_sc[...] = jnp.zeros_like(g_sc)

        ones = _seg_ones()
        diag = _diag_mask()
        seg = lambda xs, passes: _seg_multi(xs, ones, passes)
        rsum = lambda x: jnp.sum(x, axis=0, keepdims=True)
        has_prev = jnp.where(c == nc - 1, 0.0, 1.0)
        seqs = range(bsz)

        def step(ii, _):
            i = tc - 1 - ii
            row = lambda ref, bi: ref[bi, pl.ds(i, 1), :]
            put = lambda ref, bi, val: ref.__setitem__((bi, pl.ds(i, 1), slice(None)), val)
            dc = seg([diag * row(dy_ref, bi) for bi in seqs], 1)
            vc = seg([diag * row(v_ref, bi) for bi in seqs], 1)
            g = [g_sc[bi] + dc[bi] * row(r_ref, bi) for bi in seqs]
            s_p = [jnp.where(i == 0, halo_ref[bi, 0] * has_prev, st_ref[bi, jnp.maximum(i - 1, 0)]) for bi in seqs]
            sa = seg([s_p[bi] * row(n_ref, bi) for bi in seqs], 1)
            dsa = seg([g[bi] * row(b_ref, bi) for bi in seqs], 2)
            dvb = seg([g[bi] * row(k_ref, bi) for bi in seqs], 1)
            for bi in seqs:
                put(dr_ref, bi, rsum(st_ref[bi, i] * dc[bi]))
                put(dv_ref, bi, rsum(dvb[bi] * diag))
                put(dw_ref, bi, rsum(g[bi] * s_p[bi]))
                put(db_ref, bi, rsum(g[bi] * sa[bi]))
                put(dk_ref, bi, rsum(g[bi] * vc[bi]))
                put(dn_ref, bi, rsum(s_p[bi] * dsa[bi]))
                g_sc[bi] = g[bi] * row(w_ref, bi) + dsa[bi] * row(n_ref, bi)
            return 0

        lax.fori_loop(0, tc, step, 0)

    vec = pl.BlockSpec((bsz, tc, RW_WIDTH), lambda c: (0, nc - 1 - c, 0))
    return pl.pallas_call(
        body, name="scan_bwd", grid=(nc,),
        in_specs=[vec] * 7 + [pl.BlockSpec((bsz, tc, RW_HEAD, RW_WIDTH), lambda c: (0, nc - 1 - c, 0, 0)),
                              pl.BlockSpec((bsz, 1, RW_HEAD, RW_WIDTH), lambda c: (0, jnp.maximum((nc - 1 - c) * tc - 1, 0), 0, 0))],
        out_specs=[vec] * 6,
        out_shape=[_sds((bsz, t, RW_WIDTH))] * 6,
        scratch_shapes=[pltpu.VMEM((bsz, RW_HEAD, RW_WIDTH), F32)],
        compiler_params=_ARB1,
    )(r, w, k, v, nkk, b, dy, st, st)


TOKEN_TILE = 256


def _padded_weights(wt):
    f = lambda a: a.astype(F32)
    w_in = f(wt["w_in"][0])
    zeros = lambda r, c: jnp.zeros((r, c), F32)
    wp = jnp.concatenate([w_in[:, :MLA_COLS], zeros(D_MODEL, PM_W - MLA_COLS), w_in[:, MLA_COLS:]], axis=1)
    w_uq = f(wt["mla_w_uq"][0]).reshape(Q_LORA, MLA_HEADS, MLA_NOPE + MLA_ROPE)
    wqn = w_uq[:, :, :MLA_NOPE].reshape(Q_LORA, MLA_HEADS * MLA_NOPE)
    wqr = jnp.concatenate([w_uq[:, :, MLA_NOPE:], jnp.zeros((Q_LORA, MLA_HEADS, LANES - MLA_ROPE), F32)], axis=2)
    wqr = wqr.reshape(Q_LORA, MLA_HEADS * LANES)
    w2p = jnp.concatenate([f(wt["rw_w2"][0]), zeros(LORA, RW_WIDTH)], axis=0)
    a2p = jnp.concatenate([zeros(LORA, RW_WIDTH), f(wt["rw_a2"][0])], axis=0)
    bw = (f(wt["mla_q_norm_g"]), wqn, wqr, f(wt["mla_kv_norm_g"]), f(wt["mla_w_ukv"][0]), f(wt["rw_mu"]), f(wt["rw_w0"]),
          w2p, f(wt["rw_a0"]), a2p, f(wt["rw_k_k"]), f(wt["rw_k_a"]))
    hw = (f(wt["rw_ln_g"]), f(wt["rw_ln_b"]), f(wt["rw_r_k"]).reshape(1, RW_WIDTH), f(wt["w_out"][0]), f(wt["norm_post_g"]))
    return wp, bw, hw


def _local_step(x, positions, target, wt):
    bsz, t, _ = x.shape
    n = bsz * t
    tm = min(TOKEN_TILE, t)
    tps = t // tm
    wp, bw, hw = _padded_weights(wt)
    wpb = wp.astype(BF16)
    g_pre = wt["norm_pre_g"].astype(F32)
    x2 = x.reshape(n, D_MODEL)
    tgt2 = target.reshape(n, D_MODEL)
    inv_freq = ROPE_THETA ** (-jnp.arange(0, MLA_ROPE, 2, dtype=F32) / MLA_ROPE)
    invf = jnp.tile(inv_freq, LANES // (MLA_ROPE // 2)).reshape(1, LANES)
    cos, sin = _rope_tables(positions.astype(F32).reshape(n, 1), invf, tm)

    u, pm, prw, z = _fwd_a(x2, g_pre, wpb, tm)
    qn, qr, kv, kr, r, w, k, v, nkk, b = _fwd_b(pm, prw, cos, sin, bw, tm, tps)
    b3 = lambda a: a.reshape(bsz, t, a.shape[-1])
    ym, lse = _attn_fwd(b3(qn), b3(qr), b3(kv), b3(kr))
    ys, st = _scan_fwd(b3(r), b3(w), b3(k), b3(v), b3(nkk), b3(b))
    (dys, dr_h, dk_h, dv_h, dym, dz, dxres, loss, d_lng, d_lnb, d_rk, d_wout, d_gpost) = _head(
        ys.reshape(n, RW_WIDTH), r, k, v, ym.reshape(n, MLA_WIDTH), z, x2, tgt2, hw, tm)
    dqn, dqr, dkv, dkr_heads = _attn_bwd(b3(qn), b3(qr), b3(kv), b3(kr), ym, b3(dym), lse)
    dr_s, dw_s, dk_s, dv_s, dn_s, db_s = _scan_bwd(b3(r), b3(w), b3(k), b3(v), b3(nkk), b3(b), st, b3(dys))
    f2 = lambda a: a.reshape(n, a.shape[-1])
    cts = (f2(dqn), f2(dqr), f2(dkv), f2(dr_s), dr_h, f2(dw_s), f2(dk_s), dk_h, f2(dv_s), dv_h, f2(dn_s), f2(db_s))
    (dpm, dprw, dps, d_gq, d_wqn, d_wqr, d_gkv, d_wkv, d_mu, d_w0, d_w2p, d_a0, d_a2p, d_kk, d_ka) = _bwd_b(
        pm, prw, cos, sin, bw, cts, dkr_heads.reshape(MLA_HEADS, n, LANES), tm, tps)
    grad_x, dpb, d_gpre = _bwd_a(x2, g_pre, wpb, dpm, dprw, dps, dz, dxres, tm, tps)
    d_wp = _dw_in(u, dpb, min(1024, n), 640)

    d_w_in = jnp.concatenate([d_wp[:, :MLA_COLS], d_wp[:, PM_W:]], axis=1)
    d_w_uq = jnp.concatenate([d_wqn.reshape(Q_LORA, MLA_HEADS, MLA_NOPE),
                              d_wqr.reshape(Q_LORA, MLA_HEADS, LANES)[:, :, :MLA_ROPE]], axis=2)
    grads = {
        "norm_pre_g": d_gpre, "w_in": d_w_in[None], "mla_q_norm_g": d_gq,
        "mla_w_uq": d_w_uq.reshape(1, Q_LORA, MLA_HEADS * (MLA_NOPE + MLA_ROPE)), "mla_kv_norm_g": d_gkv,
        "mla_w_ukv": d_wkv[None], "rw_mu": d_mu, "rw_w0": d_w0, "rw_w2": d_w2p[None, :LORA], "rw_a0": d_a0,
        "rw_a2": d_a2p[None, LORA:], "rw_k_k": d_kk, "rw_k_a": d_ka, "rw_r_k": d_rk.reshape(1, RW_HEADS, RW_HEAD),
        "rw_ln_g": d_lng, "rw_ln_b": d_lnb, "w_out": d_wout[None], "norm_post_g": d_gpost,
    }
    return loss, grad_x.reshape(bsz, t, D_MODEL), grads


_MESH = pl.DeviceIdType.MESH


def _gather_shards(packed):
    rows, lanes = packed.shape

    def body(x_ref, out_ref, send_sems, recv_sems, local_sem):
        x, y, c = lax.axis_index("x"), lax.axis_index("y"), lax.axis_index("c")
        me, sibling = (x, y, c), (x, y, 1 - c)
        chips = [(1 - x, y), (x, 1 - y), (1 - x, 1 - y)]

        def slot(px, py, pc):
            return out_ref.at[4 * px + 2 * py + pc]

        def copy(k, block, to, src=None):
            return pltpu.make_async_remote_copy(
                src_ref=slot(*block) if src is None else src, dst_ref=slot(*block),
                send_sem=send_sems.at[k], recv_sem=recv_sems.at[k], device_id=to, device_id_type=_MESH)

        mine = pltpu.make_async_copy(x_ref, slot(*me), local_sem)
        mine.start()
        first = [copy(0, me, sibling, src=x_ref)]
        first += [copy(1 + j, me, (*chip, c), src=x_ref) for j, chip in enumerate(chips)]
        for cp in first:
            cp.start()
        passed = [copy(4 + j, (*chip, c), sibling) for j, chip in enumerate(chips)]
        for j, chip in enumerate(chips):
            copy(1 + j, (*chip, c), me).wait_recv()
            passed[j].start()
        copy(0, sibling, me).wait_recv()
        for j, chip in enumerate(chips):
            copy(4 + j, (*chip, 1 - c), me).wait_recv()
        for cp in first + passed:
            cp.wait_send()
        mine.wait()

    return pl.pallas_call(
        body, name="gather_shards",
        out_shape=_sds((N_DEV, rows, lanes), packed.dtype),
        in_specs=[pl.BlockSpec(memory_space=pltpu.VMEM)],
        out_specs=pl.BlockSpec(memory_space=pltpu.VMEM),
        scratch_shapes=[pltpu.SemaphoreType.DMA((7,)), pltpu.SemaphoreType.DMA((7,)), pltpu.SemaphoreType.DMA],
    )(packed)


def _exchange_grads(gbig, gsmall):
    _, rows, lanes = gbig.shape

    def body(big_ref, small_ref, rbig_ref, rsmall_ref, send_b, recv_b, send_s, recv_s, local_sem):
        x, y, c = lax.axis_index("x"), lax.axis_index("y"), lax.axis_index("c")
        me_lin = 4 * x + 2 * y + c
        mine = pltpu.make_async_copy(big_ref.at[me_lin], rbig_ref.at[0], local_sem)
        mine.start()
        rsmall_ref[me_lin] = small_ref[...]
        copies = []
        for k in range(1, N_DEV):
            px, py, pc = x ^ (k >> 2), y ^ ((k >> 1) & 1), c ^ (k & 1)
            peer = (px, py, pc)
            copies.append(pltpu.make_async_remote_copy(
                src_ref=big_ref.at[4 * px + 2 * py + pc], dst_ref=rbig_ref.at[k],
                send_sem=send_b.at[k - 1], recv_sem=recv_b.at[k - 1], device_id=peer, device_id_type=_MESH))
            copies.append(pltpu.make_async_remote_copy(
                src_ref=small_ref, dst_ref=rsmall_ref.at[me_lin],
                send_sem=send_s.at[k - 1], recv_sem=recv_s.at[k - 1], device_id=peer, device_id_type=_MESH))
        for cp in copies:
            cp.start()
        for cp in copies:
            cp.wait_recv()
        for cp in copies:
            cp.wait_send()
        mine.wait()

    return pl.pallas_call(
        body, name="exchange_grads",
        out_shape=[_sds((N_DEV, rows, lanes)), _sds((N_DEV, SMALL_ROWS, lanes))],
        in_specs=[pl.BlockSpec(memory_space=pl.ANY), pl.BlockSpec(memory_space=pltpu.VMEM)],
        out_specs=[pl.BlockSpec(memory_space=pl.ANY), pl.BlockSpec(memory_space=pltpu.VMEM)],
        scratch_shapes=[pltpu.SemaphoreType.DMA((7,)), pltpu.SemaphoreType.DMA((7,)), pltpu.SemaphoreType.DMA((7,)),
                        pltpu.SemaphoreType.DMA((7,)), pltpu.SemaphoreType.DMA],
    )(gbig, gsmall)


def _adamw_math(w, g, m, v):
    m = ADAM_B1 * m + (1.0 - ADAM_B1) * g
    v = ADAM_B2 * v + (1.0 - ADAM_B2) * (g * g)
    m_hat = m / (1.0 - ADAM_B1 ** ADAM_STEP)
    v_hat = v / (1.0 - ADAM_B2 ** ADAM_STEP)
    return -ADAM_LR * (m_hat / (jnp.sqrt(v_hat) + ADAM_EPS) + ADAM_WD * w), m, v


def _reduce_adamw(parts, w, m, v, rb):
    _, rows, lanes = parts.shape

    def body(p_ref, w_ref, m_ref, v_ref, g_out, d_out, m_out, v_out):
        g = p_ref[0]
        for s in range(1, N_DEV):
            g = g + p_ref[s]
        g_out[...] = g
        d_out[...], m_out[...], v_out[...] = _adamw_math(w_ref[...], g, m_ref[...], v_ref[...])

    blk = pl.BlockSpec((rb, lanes), lambda i: (i, 0))
    return pl.pallas_call(
        body, name=f"reduce_adamw_{rows}", grid=(rows // rb,),
        in_specs=[pl.BlockSpec((N_DEV, rb, lanes), lambda i: (0, i, 0)), blk, blk, blk],
        out_specs=[blk] * 4, out_shape=[_sds((rows, lanes))] * 4,
        compiler_params=_ARB1,
    )(parts, w, m, v)


def _pack_rows(arrs):
    return jnp.concatenate([a.reshape(-1, LANES) for a in arrs], axis=0)


def _shard_blocks(name, full):
    a = full[0]
    if name == "w_out":
        return a.reshape(N_DEV, -1, LANES)
    rows, cols = a.shape
    return a.reshape(rows, N_DEV, cols // N_DEV).transpose(1, 0, 2).reshape(N_DEV, -1, LANES)


def _unshard(name, blocks, shard_shape):
    _, rows, cols = shard_shape
    a = blocks.reshape(N_DEV, rows, cols)
    if name == "w_out":
        return a.reshape(1, N_DEV * rows, cols)
    return a.transpose(1, 0, 2).reshape(1, rows, N_DEV * cols)


def _small_pack(vals, tail):
    flat = jnp.concatenate([vals[nm].reshape(-1) for nm, _ in SMALL] + [tail])
    return jnp.pad(flat, (0, SMALL_ROWS * LANES - flat.shape[0])).reshape(SMALL_ROWS, LANES)


def _small_unpack(packed, like):
    flat, out, off = packed.reshape(-1), {}, 0
    for nm, cnt in SMALL:
        out[nm] = flat[off:off + cnt].reshape(like[nm].shape)
        off += cnt
    return out


def _big_unpack(packed, like):
    out, off = {}, 0
    for nm, cnt in SHARD_ROWS:
        out[nm] = packed[off:off + cnt].reshape(like[nm].shape)
        off += cnt
    return out


def kernel(x, positions, norm_pre_g, w_in, mla_q_norm_g, mla_w_uq, mla_kv_norm_g, mla_w_ukv, rw_mu, rw_w0, rw_w2, rw_a0, rw_a2, rw_k_k, rw_k_a, rw_r_k, rw_ln_g, rw_ln_b, w_out, norm_post_g, loss_target, m_norm_pre_g, m_w_in, m_mla_q_norm_g, m_mla_w_uq, m_mla_kv_norm_g, m_mla_w_ukv, m_rw_mu, m_rw_w0, m_rw_w2, m_rw_a0, m_rw_a2, m_rw_k_k, m_rw_k_a, m_rw_r_k, m_rw_ln_g, m_rw_ln_b, m_w_out, m_norm_post_g, v_norm_pre_g, v_w_in, v_mla_q_norm_g, v_mla_w_uq, v_mla_kv_norm_g, v_mla_w_ukv, v_rw_mu, v_rw_w0, v_rw_w2, v_rw_a0, v_rw_a2, v_rw_k_k, v_rw_k_a, v_rw_r_k, v_rw_ln_g, v_rw_ln_b, v_w_out, v_norm_post_g):
    given = dict(locals())
    w = {nm: given[nm] for nm in WEIGHTS}
    mom = {nm: given["m_" + nm] for nm in WEIGHTS}
    var = {nm: given["v_" + nm] for nm in WEIGHTS}
    sharded = [nm for nm, _ in SHARD_ROWS]

    gathered = _gather_shards(_pack_rows([w[nm] for nm in sharded]).astype(BF16))
    full, off = dict(w), 0
    for nm, cnt in SHARD_ROWS:
        full[nm] = _unshard(nm, gathered[:, off:off + cnt], w[nm].shape)
        off += cnt

    loss_part, grad_x, grads = _local_step(x, positions, loss_target, full)

    gbig = jnp.concatenate([_shard_blocks(nm, grads[nm]) for nm in sharded], axis=1)
    gsmall = _small_pack(grads, loss_part[0, 0:1])
    parts_big, parts_small = _exchange_grads(gbig, gsmall)

    g_b, d_b, m_b, v_b = _reduce_adamw(parts_big, _pack_rows([w[nm] for nm in sharded]), _pack_rows([mom[nm] for nm in sharded]),
                                       _pack_rows([var[nm] for nm in sharded]), PACK_ROWS // 8)
    zero1 = jnp.zeros((1,), F32)
    g_s, d_s, m_s, v_s = _reduce_adamw(parts_small, _small_pack(w, zero1), _small_pack(mom, zero1), _small_pack(var, zero1),
                                       SMALL_ROWS)
    loss = g_s.reshape(-1)[SMALL_N]
    outs = []
    for big, small in ((g_b, g_s), (d_b, d_s), (m_b, m_s), (v_b, v_s)):
        vals = {**_big_unpack(big, w), **_small_unpack(small, w)}
        outs += [vals[nm] for nm in WEIGHTS]
    return (loss, grad_x, *outs)
```

```python
import functools

import jax
import jax.numpy as jnp
from jax import lax
from jax.experimental import pallas as pl
from jax.experimental.pallas import tpu as pltpu

F32 = jnp.float32
BF16 = jnp.bfloat16

D_MODEL = 1024
MLA_HEADS = 4
MLA_NOPE = 128
MLA_ROPE = 64
MLA_V = 128
MLA_WIDTH = MLA_HEADS * MLA_V
Q_LORA = 256
KV_LORA = 128
ROPE_THETA = 10000.0
RW_HEAD = 64
RW_WIDTH = 512
RW_HEADS = RW_WIDTH // RW_HEAD
LORA = 64
RW_COLS = 3 * RW_WIDTH + 2 * LORA
MLA_COLS = Q_LORA + KV_LORA + MLA_ROPE
D_IN = MLA_COLS + RW_COLS + D_MODEL
RW_GN_EPS = 64e-5
NORM_EPS = 1e-6
ATT_SCALE = (MLA_NOPE + MLA_ROPE) ** -0.5
ADAM_LR, ADAM_B1, ADAM_B2, ADAM_EPS, ADAM_WD, ADAM_STEP = 0.001, 0.9, 0.999, 1e-08, 0.01, 10
N_DEV = 8
LANES = 128
MXU = 256

PM_W = 512
WP_COLS = PM_W + RW_COLS + D_MODEL
RW_PIECES = ((0, 512), (512, 1024), (1024, 1536), (1536, 1664))

SHARD_ROWS = (("w_in", 1024 * 392 // LANES), ("mla_w_uq", 256 * 96 // LANES), ("mla_w_ukv", 128 * 128 // LANES),
              ("rw_w2", 64 * 64 // LANES), ("rw_a2", 64 * 64 // LANES), ("w_out", 128 * 1024 // LANES))
PACK_ROWS = sum(r for _, r in SHARD_ROWS)
SMALL = (("norm_pre_g", 1024), ("mla_q_norm_g", 256), ("mla_kv_norm_g", 128), ("rw_mu", 1664), ("rw_w0", 512),
         ("rw_a0", 512), ("rw_k_k", 512), ("rw_k_a", 512), ("rw_r_k", 512), ("rw_ln_g", 512), ("rw_ln_b", 512),
         ("norm_post_g", 1024))
SMALL_N = sum(n for _, n in SMALL)
SMALL_ROWS = 72
WEIGHTS = ("norm_pre_g", "w_in", "mla_q_norm_g", "mla_w_uq", "mla_kv_norm_g", "mla_w_ukv", "rw_mu", "rw_w0", "rw_w2",
           "rw_a0", "rw_a2", "rw_k_k", "rw_k_a", "rw_r_k", "rw_ln_g", "rw_ln_b", "w_out", "norm_post_g")


def _seg_ones():
    r = lax.broadcasted_iota(jnp.int32, (MXU, MXU), 0) >> 6
    c = lax.broadcasted_iota(jnp.int32, (MXU, MXU), 1) >> 6
    return jnp.where(r == c, 1.0, 0.0).astype(BF16)


def _seg_dot(x, ones, passes):
    parts, rem = [], x
    for p in range(passes):
        hb = rem.astype(BF16)
        parts.append(hb)
        if p + 1 < passes:
            rem = rem - hb.astype(F32)
    outs = []
    for j in range(x.shape[1] // MXU):
        acc = None
        for hb in parts:
            d = jnp.dot(hb[:, MXU * j:MXU * (j + 1)], ones, preferred_element_type=F32)
            acc = d if acc is None else acc + d
        outs.append(acc)
    return outs[0] if len(outs) == 1 else jnp.concatenate(outs, axis=1)


def _seg_multi(xs, ones, passes):
    his = [x.astype(BF16) for x in xs]
    hi = jnp.concatenate(his, axis=0)
    if passes == 2:
        lo = jnp.concatenate([(x - h.astype(F32)).astype(BF16) for x, h in zip(xs, his)], axis=0)
        rhs = jnp.concatenate([ones, ones], axis=0)
    halves = []
    for j in range(hi.shape[1] // MXU):
        sl = slice(MXU * j, MXU * (j + 1))
        if passes == 2:
            halves.append(jnp.dot(jnp.concatenate([hi[:, sl], lo[:, sl]], axis=1), rhs, preferred_element_type=F32))
        else:
            halves.append(jnp.dot(hi[:, sl], ones, preferred_element_type=F32))
    full = jnp.concatenate(halves, axis=1)
    m = xs[0].shape[0]
    return [full[m * i:m * (i + 1)] for i in range(len(xs))]


@jax.custom_vjp
def _segsum(x):
    return _seg_dot(x, _seg_ones(), 3)


_segsum.defvjp(lambda x: (_segsum(x), None), lambda _, g: (_segsum(g),))


@jax.custom_vjp
def _bdot(a, w):
    return jnp.dot(a.astype(BF16), w.astype(BF16), preferred_element_type=F32)


def _bdot_fwd(a, w):
    return _bdot(a, w), (a, w)


def _bdot_bwd(res, g):
    a, w = res
    gb = g.astype(BF16)
    da = lax.dot_general(gb, w.astype(BF16), (((1,), (1,)), ((), ())), preferred_element_type=F32)
    dw = lax.dot_general(a.astype(BF16), gb, (((0,), (0,)), ((), ())), preferred_element_type=F32)
    return da, dw


_bdot.defvjp(_bdot_fwd, _bdot_bwd)


def _rot_impl(x):
    w = x.shape[1]
    lane = lax.broadcasted_iota(jnp.int32, x.shape, 1)
    return jnp.where((lane & 63) < 32, -pltpu.roll(x, w - 32, 1), pltpu.roll(x, 32, 1))


@jax.custom_vjp
def _rot(x):
    return _rot_impl(x)


_rot.defvjp(lambda x: (_rot_impl(x), None), lambda _, g: (-_rot_impl(g),))


def _rms(x, g):
    return x * lax.rsqrt(jnp.mean(x * x, axis=-1, keepdims=True) + NORM_EPS) * g


def _shift_rows(p, prev_row):
    row = lax.broadcasted_iota(jnp.int32, p.shape, 0)
    return jnp.where(row == 0, prev_row, pltpu.roll(p, 1, 0))


def _unshift_rows(g, next_row):
    row = lax.broadcasted_iota(jnp.int32, g.shape, 0)
    return jnp.where(row == g.shape[0] - 1, next_row, pltpu.roll(g, g.shape[0] - 1, 0))


def _f_mla(cq, ckv, kr, cos, sin, g_q, wqn, wqr, g_kv, wkv):
    qn = _rms(cq, g_q)
    q_nope = _bdot(qn, wqn)
    q_r = _bdot(qn, wqr)
    cos4 = jnp.concatenate([cos] * MLA_HEADS, axis=1)
    sin4 = jnp.concatenate([sin] * MLA_HEADS, axis=1)
    q_rope = q_r * cos4 + _rot(q_r) * sin4
    kv = _bdot(_rms(ckv, g_kv), wkv)
    k_rope = kr * cos + _rot(kr) * sin
    return q_nope, q_rope, kv, k_rope


def _f_rw(pr, pk, pv, pt, sr, sk, sv, st, mu_r, mu_k, mu_v, mu_t, w0, w2p, a0, a2p, k_k, k_a):
    r = pr + (sr - pr) * mu_r
    k = pk + (sk - pk) * mu_k
    v = pv + (sv - pv) * mu_v
    t = pt + (st - pt) * mu_t
    nwl = -(w0 + _bdot(jnp.tanh(t), w2p))
    softplus = jnp.maximum(nwl, 0.0) + jnp.log(1.0 + jnp.exp(-jnp.abs(nwl)))
    decay = jnp.exp(-jnp.exp(-softplus - 0.5))
    a = jax.nn.sigmoid(a0 + _bdot(t, a2p))
    kk = k * k_k
    kk = kk / jnp.maximum(jnp.sqrt(_segsum(kk * kk)), 1e-12)
    k2 = k * (1.0 + (a - 1.0) * k_a)
    return r, decay, k2, v, -kk, kk * a


def _f_head(ys, r, k, v, ym, z1, z2, x, tgt, ln_g, ln_b, r_k, w1, w2, g_post):
    inv = 1.0 / RW_HEAD
    yc = ys - _segsum(ys) * inv
    var = _segsum(yc * yc) * inv
    y = yc * lax.rsqrt(var + RW_GN_EPS) * ln_g + ln_b
    y_rw = y + _segsum(r * k * r_k) * v
    c1 = ym * (z1 * jax.nn.sigmoid(z1))
    c2 = y_rw * (z2 * jax.nn.sigmoid(z2))
    out = _bdot(c1, w1) + _bdot(c2, w2)
    err = x + _rms(out, g_post) - tgt
    per_row = jnp.sum(err * err, axis=1, keepdims=True)
    return jnp.sum(per_row, axis=0, keepdims=True) * (0.5 / D_MODEL)


def _rows(tm, width):
    return pl.BlockSpec((tm, width), lambda i: (i, 0))


def _whole(shape):
    zeros = (0,) * len(shape)
    return pl.BlockSpec(shape, lambda i: zeros)


def _sds(shape, dtype=F32):
    return jax.ShapeDtypeStruct(shape, dtype)


_ARB1 = pltpu.CompilerParams(dimension_semantics=("arbitrary",))


def _acc(ref, val, first):
    @pl.when(first)
    def _():
        ref[...] = val

    @pl.when(jnp.logical_not(first))
    def _():
        ref[...] += val


def _fwd_a(x2, g_pre, wp, tm):
    n = x2.shape[0]

    def body(x_ref, g_ref, w_ref, u_ref, pm_ref, prw_ref, z_ref):
        ub = _rms(x_ref[...], g_ref[...]).astype(BF16)
        u_ref[...] = ub
        pm_ref[...] = jnp.dot(ub, w_ref[:, 0:PM_W], preferred_element_type=F32)
        prw_ref[...] = jnp.dot(ub, w_ref[:, PM_W:PM_W + RW_COLS], preferred_element_type=F32)
        z_ref[...] = jnp.dot(ub, w_ref[:, PM_W + RW_COLS:WP_COLS], preferred_element_type=F32)

    return pl.pallas_call(
        body, name="fwd_a", grid=(n // tm,),
        in_specs=[_rows(tm, D_MODEL), _whole((1, D_MODEL)), _whole((D_MODEL, WP_COLS))],
        out_specs=[_rows(tm, D_MODEL), _rows(tm, PM_W), _rows(tm, RW_COLS), _rows(tm, D_MODEL)],
        out_shape=[_sds((n, D_MODEL), BF16), _sds((n, PM_W)), _sds((n, RW_COLS)), _sds((n, D_MODEL))],
        compiler_params=_ARB1,
    )(x2, g_pre, wp)


def _rope_tables(posf, invf, tm):
    n = posf.shape[0]

    def body(p_ref, f_ref, c_ref, s_ref):
        ang = p_ref[...] * f_ref[...]
        c_ref[...] = jnp.cos(ang)
        s_ref[...] = jnp.sin(ang)

    return pl.pallas_call(
        body, name="rope_tables", grid=(n // tm,),
        in_specs=[_rows(tm, 1), _whole((1, LANES))],
        out_specs=[_rows(tm, LANES), _rows(tm, LANES)],
        out_shape=[_sds((n, LANES)), _sds((n, LANES))],
        compiler_params=_ARB1,
    )(posf, invf)


_B_WEIGHT_SHAPES = ((1, Q_LORA), (Q_LORA, 512), (Q_LORA, 512), (1, KV_LORA), (KV_LORA, 1024), (1, RW_COLS), (1, RW_WIDTH),
                    (LANES, RW_WIDTH), (1, RW_WIDTH), (LANES, RW_WIDTH), (1, RW_WIDTH), (1, RW_WIDTH))


def _halo_prev(tm):
    return pl.BlockSpec((8, RW_COLS), lambda i: (jnp.maximum(i * (tm // 8) - 1, 0), 0))


def _b_operands(pm_ref, prw_ref, halo_ref, wrefs, tile, tiles_per_seq):
    g_q, wqn, wqr, g_kv, wkv, mu, w0, w2p, a0, a2p, k_k, k_a = wrefs
    mla_in = (pm_ref[:, 0:Q_LORA], pm_ref[:, Q_LORA:Q_LORA + KV_LORA], pm_ref[:, Q_LORA + KV_LORA:PM_W])
    mla_w = (g_q[...], wqn[...], wqr[...], g_kv[...], wkv[...])
    keep = jnp.where(tile % tiles_per_seq == 0, 0.0, 1.0)
    prev = halo_ref[7:8, :] * keep
    ps = tuple(prw_ref[:, a:b] for a, b in RW_PIECES)
    ss = tuple(_shift_rows(p, prev[:, a:b]) for p, (a, b) in zip(ps, RW_PIECES))
    rw_w = tuple(mu[:, a:b] for a, b in RW_PIECES) + (w0[...], w2p[...], a0[...], a2p[...], k_k[...], k_a[...])
    return mla_in, mla_w, ps + ss, rw_w


def _fwd_b(pm, prw, cos, sin, bw, tm, tiles_per_seq):
    n = pm.shape[0]

    def body(pm_ref, prw_ref, halo_ref, cos_ref, sin_ref, *refs):
        wrefs, outs = refs[:12], refs[12:]
        mla_in, mla_w, rw_in, rw_w = _b_operands(pm_ref, prw_ref, halo_ref, wrefs, pl.program_id(0), tiles_per_seq)
        res = _f_mla(*mla_in, cos_ref[...], sin_ref[...], *mla_w) + _f_rw(*rw_in, *rw_w)
        for o_ref, val in zip(outs, res):
            o_ref[...] = val

    widths = (512, 512, 1024, LANES) + (RW_WIDTH,) * 6
    return pl.pallas_call(
        body, name="fwd_b", grid=(n // tm,),
        in_specs=[_rows(tm, PM_W), _rows(tm, RW_COLS), _halo_prev(tm), _rows(tm, LANES), _rows(tm, LANES)]
        + [_whole(s) for s in _B_WEIGHT_SHAPES],
        out_specs=[_rows(tm, w) for w in widths],
        out_shape=[_sds((n, w)) for w in widths],
        compiler_params=_ARB1,
    )(pm, prw, prw, cos, sin, *bw)


def _bwd_b(pm, prw, cos, sin, bw, cts, dkr_heads, tm, tiles_per_seq):
    n = pm.shape[0]

    ct_widths = (512, 512, 1024) + (RW_WIDTH,) * 9
    n_ct = len(ct_widths)

    def body(pm_ref, prw_ref, halo_ref, cos_ref, sin_ref, *refs):
        wrefs, ct_refs, dkr_ref = refs[:12], refs[12:12 + n_ct], refs[12 + n_ct]
        dpm_ref, dprw_ref, dps_ref = refs[13 + n_ct:16 + n_ct]
        wg_refs = refs[16 + n_ct:]
        tile = pl.program_id(0)
        first = tile == 0
        mla_in, mla_w, rw_in, rw_w = _b_operands(pm_ref, prw_ref, halo_ref, wrefs, tile, tiles_per_seq)
        cos, sin = cos_ref[...], sin_ref[...]
        ct = [r[...] for r in ct_refs]
        _, vjp_mla = jax.vjp(lambda *a: _f_mla(*a[:3], cos, sin, *a[3:]), *mla_in, *mla_w)
        dkr = dkr_ref[0] + dkr_ref[1] + dkr_ref[2] + dkr_ref[3]
        d_mla = vjp_mla((ct[0], ct[1], ct[2], dkr))
        dpm_ref[:, 0:Q_LORA] = d_mla[0]
        dpm_ref[:, Q_LORA:Q_LORA + KV_LORA] = d_mla[1]
        dpm_ref[:, Q_LORA + KV_LORA:PM_W] = d_mla[2]
        _, vjp_rw = jax.vjp(_f_rw, *rw_in, *rw_w)
        d_rw = vjp_rw((ct[3] + ct[4], ct[5], ct[6] + ct[7], ct[8] + ct[9], ct[10], ct[11]))
        for j, (a, b) in enumerate(RW_PIECES):
            dprw_ref[:, a:b] = d_rw[j]
            dps_ref[:, a:b] = d_rw[4 + j]
        g_q, wqn, wqr, g_kv, wkv, mu, w0, w2p, a0, a2p, k_k, k_a = wg_refs
        for ref, val in zip((g_q, wqn, wqr, g_kv, wkv), d_mla[3:]):
            _acc(ref, val, first)
        for j, (a, b) in enumerate(RW_PIECES):
            _acc(mu.at[:, a:b], d_rw[8 + j], first)
        for ref, val in zip((w0, w2p, a0, a2p, k_k, k_a), d_rw[12:]):
            _acc(ref, val, first)

    return pl.pallas_call(
        body, name="bwd_b", grid=(n // tm,),
        in_specs=[_rows(tm, PM_W), _rows(tm, RW_COLS), _halo_prev(tm), _rows(tm, LANES), _rows(tm, LANES)]
        + [_whole(s) for s in _B_WEIGHT_SHAPES] + [_rows(tm, w) for w in ct_widths]
        + [pl.BlockSpec((MLA_HEADS, tm, LANES), lambda i: (0, i, 0))],
        out_specs=[_rows(tm, PM_W), _rows(tm, RW_COLS), _rows(tm, RW_COLS)] + [_whole(s) for s in _B_WEIGHT_SHAPES],
        out_shape=[_sds((n, PM_W)), _sds((n, RW_COLS)), _sds((n, RW_COLS))] + [_sds(s) for s in _B_WEIGHT_SHAPES],
        compiler_params=_ARB1,
    )(pm, prw, prw, cos, sin, *bw, *cts, dkr_heads)


def _head(ys, r, k, v, ym, z, x2, tgt, hw, tm):
    n = x2.shape[0]
    h_shapes = ((1, RW_WIDTH), (1, RW_WIDTH), (1, RW_WIDTH), (D_MODEL, D_MODEL), (1, D_MODEL))

    def body(ys_ref, r_ref, k_ref, v_ref, ym_ref, z_ref, x_ref, t_ref, lng, lnb, rk, wout, gpost,
             dys_ref, dr_ref, dk_ref, dv_ref, dym_ref, dz_ref, dx_ref, loss_ref, dlng, dlnb, drk, dwout, dgpost):
        first = pl.program_id(0) == 0
        tgt_v = t_ref[...]
        args = (ys_ref[...], r_ref[...], k_ref[...], v_ref[...], ym_ref[...], z_ref[:, 0:MLA_WIDTH], z_ref[:, MLA_WIDTH:D_MODEL],
                x_ref[...], lng[...], lnb[...], rk[...], wout[0:MLA_WIDTH, :], wout[MLA_WIDTH:D_MODEL, :], gpost[...])
        loss, vjp = jax.vjp(lambda *a: _f_head(*a[:8], tgt_v, *a[8:]), *args)
        d = vjp(jnp.ones((1, 1), F32))
        dys_ref[...] = d[0]
        dr_ref[...] = d[1]
        dk_ref[...] = d[2]
        dv_ref[...] = d[3]
        dym_ref[...] = d[4]
        dz_ref[:, 0:MLA_WIDTH] = d[5]
        dz_ref[:, MLA_WIDTH:D_MODEL] = d[6]
        dx_ref[...] = d[7]
        _acc(loss_ref, jnp.broadcast_to(loss, (8, LANES)), first)
        _acc(dlng, d[8], first)
        _acc(dlnb, d[9], first)
        _acc(drk, d[10], first)
        _acc(dwout.at[0:MLA_WIDTH, :], d[11], first)
        _acc(dwout.at[MLA_WIDTH:D_MODEL, :], d[12], first)
        _acc(dgpost, d[13], first)

    widths = (RW_WIDTH,) * 4 + (MLA_WIDTH, D_MODEL, D_MODEL)
    return pl.pallas_call(
        body, name="head", grid=(n // tm,),
        in_specs=[_rows(tm, RW_WIDTH)] * 4 + [_rows(tm, MLA_WIDTH), _rows(tm, D_MODEL), _rows(tm, D_MODEL), _rows(tm, D_MODEL)]
        + [_whole(s) for s in h_shapes],
        out_specs=[_rows(tm, w) for w in widths] + [_whole((8, LANES))] + [_whole(s) for s in h_shapes],
        out_shape=[_sds((n, w)) for w in widths] + [_sds((8, LANES))] + [_sds(s) for s in h_shapes],
        compiler_params=_ARB1,
    )(ys, r, k, v, ym, z, x2, tgt, *hw)


def _halo_next(tm, n):
    last = n // 8 - 1
    return pl.BlockSpec((8, RW_COLS), lambda i: (jnp.minimum((i + 1) * (tm // 8), last), 0))


def _bwd_a(x2, g_pre, wp, dpm, dprw, dps, dz, dxres, tm, tiles_per_seq):
    n = x2.shape[0]
    nt_dims = (((1,), (1,)), ((), ()))

    def body(x_ref, g_ref, w_ref, dpm_ref, dprw_ref, dps_ref, nxt_ref, dz_ref, dxres_ref, gx_ref, dpb_ref, dg_ref):
        tile = pl.program_id(0)
        keep = jnp.where((tile + 1) % tiles_per_seq == 0, 0.0, 1.0)
        dprw_v = dprw_ref[...] + _unshift_rows(dps_ref[...], nxt_ref[0:1, :] * keep)
        dpm_b, dprw_b, dz_b = dpm_ref[...].astype(BF16), dprw_v.astype(BF16), dz_ref[...].astype(BF16)
        dpb_ref[:, 0:PM_W] = dpm_b
        dpb_ref[:, PM_W:PM_W + RW_COLS] = dprw_b
        dpb_ref[:, PM_W + RW_COLS:WP_COLS] = dz_b
        du = (lax.dot_general(dpm_b, w_ref[:, 0:PM_W], nt_dims, preferred_element_type=F32)
              + lax.dot_general(dprw_b, w_ref[:, PM_W:PM_W + RW_COLS], nt_dims, preferred_element_type=F32)
              + lax.dot_general(dz_b, w_ref[:, PM_W + RW_COLS:WP_COLS], nt_dims, preferred_element_type=F32))
        x = x_ref[...]
        xhat = x * lax.rsqrt(jnp.mean(x * x, axis=-1, keepdims=True) + NORM_EPS)
        dxn = du * g_ref[...]
        dx = (dxn - xhat * jnp.mean(dxn * xhat, axis=-1, keepdims=True)) * lax.rsqrt(jnp.mean(x * x, axis=-1, keepdims=True) + NORM_EPS)
        gx_ref[...] = dx + dxres_ref[...]
        _acc(dg_ref, jnp.sum(du * xhat, axis=0, keepdims=True), tile == 0)

    return pl.pallas_call(
        body, name="bwd_a", grid=(n // tm,),
        in_specs=[_rows(tm, D_MODEL), _whole((1, D_MODEL)), _whole((D_MODEL, WP_COLS)), _rows(tm, PM_W), _rows(tm, RW_COLS),
                  _rows(tm, RW_COLS), _halo_next(tm, n), _rows(tm, D_MODEL), _rows(tm, D_MODEL)],
        out_specs=[_rows(tm, D_MODEL), _rows(tm, WP_COLS), _whole((1, D_MODEL))],
        out_shape=[_sds((n, D_MODEL)), _sds((n, WP_COLS), BF16), _sds((1, D_MODEL))],
        compiler_params=_ARB1,
    )(x2, g_pre, wp, dpm, dprw, dps, dps, dz, dxres)


def _dw_in(u, dpb, tk, tn):
    n = u.shape[0]

    def body(u_ref, d_ref, o_ref):
        val = lax.dot_general(u_ref[...], d_ref[...], (((0,), (0,)), ((), ())), preferred_element_type=F32)
        _acc(o_ref, val, pl.program_id(1) == 0)

    return pl.pallas_call(
        body, name="dw_in", grid=(WP_COLS // tn, n // tk),
        in_specs=[pl.BlockSpec((tk, D_MODEL), lambda j, k: (k, 0)), pl.BlockSpec((tk, tn), lambda j, k: (k, j))],
        out_specs=pl.BlockSpec((D_MODEL, tn), lambda j, k: (0, j)),
        out_shape=_sds((D_MODEL, WP_COLS)),
        compiler_params=pltpu.CompilerParams(dimension_semantics=("arbitrary", "arbitrary")),
    )(u, dpb)


ATT_BLK = 256
_NT = (((1,), (1,)), ((), ()))
_TN = (((0,), (0,)), ((), ()))


def _causal(q0, k0, blk):
    row = q0 + lax.broadcasted_iota(jnp.int32, (blk, blk), 0)
    col = k0 + lax.broadcasted_iota(jnp.int32, (blk, blk), 1)
    return row >= col


def _attn_fwd(qn, qr, kv, kr):
    bsz, t, _ = qn.shape
    blk = min(ATT_BLK, t)

    def body(qn_ref, qr_ref, kn_ref, kr_ref, v_ref, o_ref, lse_ref):
        qi = pl.program_id(2)
        q = jnp.concatenate([qn_ref[...], qr_ref[...]], axis=1).astype(BF16)

        def kv_step(j, carry):
            m, l, acc = carry
            ks = pl.multiple_of(j * blk, blk)
            k = jnp.concatenate([kn_ref[pl.ds(ks, blk), :], kr_ref[pl.ds(ks, blk), :]], axis=1).astype(BF16)
            s = lax.dot_general(q, k, _NT, preferred_element_type=F32) * ATT_SCALE
            s = jnp.where(_causal(qi * blk, j * blk, blk), s, -1e30)
            m_new = jnp.maximum(m, jnp.max(s, axis=1, keepdims=True))
            alpha = jnp.exp(m - m_new)
            p = jnp.exp(s - m_new)
            l = alpha * l + jnp.sum(p, axis=1, keepdims=True)
            acc = alpha * acc + jnp.dot(p.astype(BF16), v_ref[pl.ds(ks, blk), :].astype(BF16), preferred_element_type=F32)
            return m_new, l, acc

        init = (jnp.full((blk, 1), -1e30, F32), jnp.zeros((blk, 1), F32), jnp.zeros((blk, MLA_V), F32))
        m, l, acc = lax.fori_loop(0, qi + 1, kv_step, init)
        o_ref[...] = acc / l
        lse_ref[...] = jnp.broadcast_to(m + jnp.log(l), (blk, LANES))

    return pl.pallas_call(
        body, name="attn_fwd", grid=(bsz, MLA_HEADS, t // blk),
        in_specs=[pl.BlockSpec((None, blk, LANES), lambda b, h, i: (b, i, h)),
                  pl.BlockSpec((None, blk, LANES), lambda b, h, i: (b, i, h)),
                  pl.BlockSpec((None, t, LANES), lambda b, h, i: (b, 0, 2 * h)),
                  pl.BlockSpec((None, t, LANES), lambda b, h, i: (b, 0, 0)),
                  pl.BlockSpec((None, t, LANES), lambda b, h, i: (b, 0, 2 * h + 1))],
        out_specs=[pl.BlockSpec((None, blk, LANES), lambda b, h, i: (b, i, h)),
                   pl.BlockSpec((None, None, blk, LANES), lambda b, h, i: (b, h, i, 0))],
        out_shape=[_sds((bsz, t, MLA_WIDTH)), _sds((bsz, MLA_HEADS, t, LANES))],
        compiler_params=pltpu.CompilerParams(dimension_semantics=("arbitrary", "arbitrary", "arbitrary")),
    )(qn, qr, kv, kr, kv)


def _attn_bwd(qn, qr, kv, kr, o, do, lse):
    bsz, t, _ = qn.shape
    blk = min(ATT_BLK, t)
    nb = t // blk

    def body(qn_ref, qr_ref, kn_ref, kr_ref, v_ref, o_ref, do_ref, lse_ref, dqn_ref, dqr_ref, dkv_ref, dkr_ref, dq_sc, delta_sc):
        dq_sc[...] = jnp.zeros_like(dq_sc)
        delta_sc[...] = jnp.sum(do_ref[...] * o_ref[...], axis=1, keepdims=True)

        def kv_loop(j, _):
            ks = pl.multiple_of(j * blk, blk)
            k = jnp.concatenate([kn_ref[pl.ds(ks, blk), :], kr_ref[pl.ds(ks, blk), :]], axis=1).astype(BF16)
            vb = v_ref[pl.ds(ks, blk), :].astype(BF16)

            def q_loop(i, carry):
                dk, dv = carry
                qs = pl.multiple_of(i * blk, blk)
                q = jnp.concatenate([qn_ref[pl.ds(qs, blk), :], qr_ref[pl.ds(qs, blk), :]], axis=1).astype(BF16)
                dob = do_ref[pl.ds(qs, blk), :].astype(BF16)
                s = lax.dot_general(q, k, _NT, preferred_element_type=F32) * ATT_SCALE
                p = jnp.where(_causal(i * blk, j * blk, blk), jnp.exp(s - lse_ref[pl.ds(qs, blk), 0:1]), 0.0)
                dv = dv + lax.dot_general(p.astype(BF16), dob, _TN, preferred_element_type=F32)
                dp = lax.dot_general(dob, vb, _NT, preferred_element_type=F32)
                ds = (p * (dp - delta_sc[pl.ds(qs, blk), :]) * ATT_SCALE).astype(BF16)
                dq_sc[pl.ds(qs, blk), :] += jnp.dot(ds, k, preferred_element_type=F32)
                dk = dk + lax.dot_general(ds, q, _TN, preferred_element_type=F32)
                return dk, dv

            dk, dv = lax.fori_loop(j, nb, q_loop, (jnp.zeros((blk, 2 * LANES), F32), jnp.zeros((blk, MLA_V), F32)))
            dkv_ref[pl.ds(ks, blk), 0:LANES] = dk[:, 0:LANES]
            dkv_ref[pl.ds(ks, blk), LANES:2 * LANES] = dv
            dkr_ref[pl.ds(ks, blk), :] = dk[:, LANES:2 * LANES]
            return 0

        lax.fori_loop(0, nb, kv_loop, 0)
        dqn_ref[...] = dq_sc[:, 0:LANES]
        dqr_ref[...] = dq_sc[:, LANES:2 * LANES]

    head_col = lambda b, h: (b, 0, h)
    return pl.pallas_call(
        body, name="attn_bwd", grid=(bsz, MLA_HEADS),
        in_specs=[pl.BlockSpec((None, t, LANES), head_col), pl.BlockSpec((None, t, LANES), head_col),
                  pl.BlockSpec((None, t, LANES), lambda b, h: (b, 0, 2 * h)),
                  pl.BlockSpec((None, t, LANES), lambda b, h: (b, 0, 0)),
                  pl.BlockSpec((None, t, LANES), lambda b, h: (b, 0, 2 * h + 1)),
                  pl.BlockSpec((None, t, LANES), head_col), pl.BlockSpec((None, t, LANES), head_col),
                  pl.BlockSpec((None, None, t, LANES), lambda b, h: (b, h, 0, 0))],
        out_specs=[pl.BlockSpec((None, t, LANES), head_col), pl.BlockSpec((None, t, LANES), head_col),
                   pl.BlockSpec((None, t, 2 * LANES), head_col),
                   pl.BlockSpec((None, None, t, LANES), lambda b, h: (h, b, 0, 0))],
        out_shape=[_sds((bsz, t, MLA_WIDTH)), _sds((bsz, t, MLA_WIDTH)), _sds((bsz, t, 2 * MLA_WIDTH)),
                   _sds((MLA_HEADS, bsz, t, LANES))],
        scratch_shapes=[pltpu.VMEM((t, 2 * LANES), F32), pltpu.VMEM((t, 1), F32)],
        compiler_params=pltpu.CompilerParams(dimension_semantics=("arbitrary", "arbitrary")),
    )(qn, qr, kv, kr, kv, o, do, lse)


SCAN_CHUNK = 16


def _diag_mask():
    row = lax.broadcasted_iota(jnp.int32, (RW_HEAD, RW_WIDTH), 0)
    lane = lax.broadcasted_iota(jnp.int32, (RW_HEAD, RW_WIDTH), 1)
    return jnp.where(row == (lane & (RW_HEAD - 1)), 1.0, 0.0)


def _time_minor(a):
    bsz, t, _ = a.shape
    a = a.reshape(bsz, t // SCAN_CHUNK, SCAN_CHUNK, RW_HEADS, RW_HEAD)
    return a.transpose(0, 1, 4, 3, 2).reshape(bsz, t // SCAN_CHUNK, RW_HEAD, RW_HEADS * SCAN_CHUNK)


def _head_expand():
    l = lax.broadcasted_iota(jnp.int32, (2 * LANES, RW_WIDTH), 0)
    n = lax.broadcasted_iota(jnp.int32, (2 * LANES, RW_WIDTH), 1)
    return jnp.where(((l & (LANES - 1)) >> 4) == (n >> 6), 1.0, 0.0).astype(BF16)


def _col_bcast_chunk(tm_ref, out_sc, expand, seqs):
    step_of_lane = lax.broadcasted_iota(jnp.int32, (RW_HEAD, LANES), 1) & (SCAN_CHUNK - 1)
    tiles = [tm_ref[bi, 0] for bi in seqs]
    for t in range(SCAN_CHUNK):
        parts = []
        for tile in tiles:
            a = jnp.where(step_of_lane == t, tile, 0.0)
            hi = a.astype(BF16)
            parts.append(jnp.concatenate([hi, (a - hi.astype(F32)).astype(BF16)], axis=1))
        out_sc[t] = jnp.dot(jnp.concatenate(parts, axis=0), expand, preferred_element_type=F32)


def _fold8(x):
    acc = x[0:8]
    for j in range(1, x.shape[0] // 8):
        acc = acc + x[8 * j:8 * (j + 1)]
    return acc


def _rows8(at):
    return pl.ds(at * 8 if isinstance(at, int) else pl.multiple_of(at * 8, 8), 8)


def _put8(sc, bi, at, val):
    for j in range(RW_WIDTH // LANES):
        sc[bi * (RW_WIDTH // LANES) + j, _rows8(at), :] = val[:, LANES * j:LANES * (j + 1)]


def _unfold8(sc, bi, steps):
    tiles = []
    for j in range(RW_WIDTH // LANES):
        view = sc.at[bi * (RW_WIDTH // LANES) + j]
        acc = view[pl.ds(0, steps, stride=8), :]
        for s in range(1, 8):
            acc = acc + view[pl.ds(s, steps, stride=8), :]
        tiles.append(acc)
    return jnp.concatenate(tiles, axis=1)


def _scan_fwd(r, w, k, vt, nkk, b):
    bsz, t, _ = r.shape
    tc = SCAN_CHUNK

    def body(r_ref, w_ref, k_ref, n_ref, b_ref, vt_ref, y_ref, st_ref, s_sc, vc_sc, y_sc):
        @pl.when(pl.program_id(0) == 0)
        def _():
            s_sc[...] = jnp.zeros_like(s_sc)

        ones = _seg_ones()
        diag = _diag_mask()
        seqs = range(bsz)
        _col_bcast_chunk(vt_ref, vc_sc, _head_expand(), seqs)

        def put_y(ya, at):
            for bi in seqs:
                _put8(y_sc, bi, at, _fold8(ya[bi] * diag))

        def step(i, _):
            row = lambda ref, bi: ref[bi, pl.ds(i, 1), :]
            prev = jnp.maximum(i - 1, 0)
            s_old = [s_sc[bi] for bi in seqs]
            both = _seg_multi([s_old[bi] * row(n_ref, bi) for bi in seqs]
                              + [s_old[bi] * r_ref[bi, pl.ds(prev, 1), :] for bi in seqs], ones, 1)
            sa = both[:bsz]
            put_y(both[bsz:], prev)
            vc = vc_sc[i]
            for bi in seqs:
                s_new = (s_old[bi] * row(w_ref, bi) + sa[bi] * row(b_ref, bi)
                         + vc[RW_HEAD * bi:RW_HEAD * (bi + 1)] * row(k_ref, bi))
                s_sc[bi] = s_new
                st_ref[bi, i] = s_new
            return 0

        lax.fori_loop(0, tc, step, 0)
        put_y(_seg_multi([s_sc[bi] * r_ref[bi, tc - 1:tc, :] for bi in seqs], ones, 1), tc - 1)
        for bi in seqs:
            y_ref[bi] = _unfold8(y_sc, bi, tc)

    vec = pl.BlockSpec((bsz, tc, RW_WIDTH), lambda c: (0, c, 0))
    return pl.pallas_call(
        body, name="scan_fwd", grid=(t // tc,),
        in_specs=[vec] * 5 + [pl.BlockSpec((bsz, 1, RW_HEAD, LANES), lambda c: (0, c, 0, 0))],
        out_specs=[vec, pl.BlockSpec((bsz, tc, RW_HEAD, RW_WIDTH), lambda c: (0, c, 0, 0))],
        out_shape=[_sds((bsz, t, RW_WIDTH)), _sds((bsz, t, RW_HEAD, RW_WIDTH))],
        scratch_shapes=[pltpu.VMEM((bsz, RW_HEAD, RW_WIDTH), F32), pltpu.VMEM((tc, bsz * RW_HEAD, RW_WIDTH), F32),
                        pltpu.VMEM((bsz * RW_WIDTH // LANES, tc * 8, LANES), F32)],
        compiler_params=_ARB1,
    )(r, w, k, nkk, b, vt)


def _scan_bwd(r, w, k, vt, nkk, b, st, dyt):
    bsz, t, _ = r.shape
    tc = SCAN_CHUNK
    nc = t // tc

    def body(r_ref, w_ref, k_ref, n_ref, b_ref, vt_ref, dyt_ref, st_ref, halo_ref,
             dr_ref, dw_ref, dk_ref, dv_ref, dn_ref, db_ref, g_sc, vc_sc, dc_sc, *part_scs):
        c = pl.program_id(0)

        @pl.when(c == 0)
        def _():
            g_sc[...] = jnp.zeros_like(g_sc)

        ones = _seg_ones()
        diag = _diag_mask()
        has_prev = jnp.where(c == nc - 1, 0.0, 1.0)
        seqs = range(bsz)
        expand = _head_expand()
        _col_bcast_chunk(vt_ref, vc_sc, expand, seqs)
        _col_bcast_chunk(dyt_ref, dc_sc, expand, seqs)
        dr_sc, dw_sc, dk_sc, dv_sc, dn_sc, db_sc = part_scs

        def step(i, s_p):
            row = lambda ref, bi: ref[bi, pl.ds(i, 1), :] if not isinstance(i, int) else ref[bi, i:i + 1, :]
            per_seq = lambda a: [a[RW_HEAD * bi:RW_HEAD * (bi + 1)] for bi in seqs]
            dc, vc = per_seq(dc_sc[i]), per_seq(vc_sc[i])
            g = [g_sc[bi] + dc[bi] * row(r_ref, bi) for bi in seqs]
            res = _seg_multi([s_p[bi] * row(n_ref, bi) for bi in seqs] + [g[bi] * row(b_ref, bi) for bi in seqs]
                             + [g[bi] * row(k_ref, bi) for bi in seqs], ones, 1)
            sa, dsa, dvb = res[:bsz], res[bsz:2 * bsz], res[2 * bsz:]
            for bi in seqs:
                _put8(dr_sc, bi, i, _fold8(st_ref[bi, i] * dc[bi]))
                _put8(dv_sc, bi, i, _fold8(dvb[bi] * diag))
                _put8(dw_sc, bi, i, _fold8(g[bi] * s_p[bi]))
                _put8(db_sc, bi, i, _fold8(g[bi] * sa[bi]))
                _put8(dk_sc, bi, i, _fold8(g[bi] * vc[bi]))
                _put8(dn_sc, bi, i, _fold8(s_p[bi] * dsa[bi]))
                g_sc[bi] = g[bi] * row(w_ref, bi) + dsa[bi] * row(n_ref, bi)

        def loop_step(ii, _):
            i = tc - 1 - ii
            step(i, [st_ref[bi, i - 1] for bi in seqs])
            return 0

        lax.fori_loop(0, tc - 1, loop_step, 0)
        step(0, [halo_ref[bi, 0] * has_prev for bi in seqs])
        for out_ref, sc in zip((dr_ref, dw_ref, dk_ref, dv_ref, dn_ref, db_ref), part_scs):
            for bi in seqs:
                out_ref[bi] = _unfold8(sc, bi, tc)

    vec = pl.BlockSpec((bsz, tc, RW_WIDTH), lambda c: (0, nc - 1 - c, 0))
    tmin = pl.BlockSpec((bsz, 1, RW_HEAD, LANES), lambda c: (0, nc - 1 - c, 0, 0))
    bcast = pltpu.VMEM((tc, bsz * RW_HEAD, RW_WIDTH), F32)
    parts = pltpu.VMEM((bsz * RW_WIDTH // LANES, tc * 8, LANES), F32)
    return pl.pallas_call(
        body, name="scan_bwd", grid=(nc,),
        in_specs=[vec] * 5 + [tmin, tmin, pl.BlockSpec((bsz, tc, RW_HEAD, RW_WIDTH), lambda c: (0, nc - 1 - c, 0, 0)),
                              pl.BlockSpec((bsz, 1, RW_HEAD, RW_WIDTH), lambda c: (0, jnp.maximum((nc - 1 - c) * tc - 1, 0), 0, 0))],
        out_specs=[vec] * 6,
        out_shape=[_sds((bsz, t, RW_WIDTH))] * 6,
        scratch_shapes=[pltpu.VMEM((bsz, RW_HEAD, RW_WIDTH), F32), bcast, bcast] + [parts] * 6,
        compiler_params=_ARB1,
    )(r, w, k, nkk, b, vt, dyt, st, st)


TOKEN_TILE = 256


def _padded_weights(wt):
    f = lambda a: a.astype(F32)
    w_in = f(wt["w_in"][0])
    zeros = lambda r, c: jnp.zeros((r, c), F32)
    wp = jnp.concatenate([w_in[:, :MLA_COLS], zeros(D_MODEL, PM_W - MLA_COLS), w_in[:, MLA_COLS:]], axis=1)
    w_uq = f(wt["mla_w_uq"][0]).reshape(Q_LORA, MLA_HEADS, MLA_NOPE + MLA_ROPE)
    wqn = w_uq[:, :, :MLA_NOPE].reshape(Q_LORA, MLA_HEADS * MLA_NOPE)
    wqr = jnp.concatenate([w_uq[:, :, MLA_NOPE:], jnp.zeros((Q_LORA, MLA_HEADS, LANES - MLA_ROPE), F32)], axis=2)
    wqr = wqr.reshape(Q_LORA, MLA_HEADS * LANES)
    w2p = jnp.concatenate([f(wt["rw_w2"][0]), zeros(LORA, RW_WIDTH)], axis=0)
    a2p = jnp.concatenate([zeros(LORA, RW_WIDTH), f(wt["rw_a2"][0])], axis=0)
    bw = (f(wt["mla_q_norm_g"]), wqn, wqr, f(wt["mla_kv_norm_g"]), f(wt["mla_w_ukv"][0]), f(wt["rw_mu"]), f(wt["rw_w0"]),
          w2p, f(wt["rw_a0"]), a2p, f(wt["rw_k_k"]), f(wt["rw_k_a"]))
    hw = (f(wt["rw_ln_g"]), f(wt["rw_ln_b"]), f(wt["rw_r_k"]).reshape(1, RW_WIDTH), f(wt["w_out"][0]), f(wt["norm_post_g"]))
    return wp, bw, hw


def _local_step(x, positions, target, wt):
    bsz, t, _ = x.shape
    n = bsz * t
    tm = min(TOKEN_TILE, t)
    tps = t // tm
    wp, bw, hw = _padded_weights(wt)
    wpb = wp.astype(BF16)
    g_pre = wt["norm_pre_g"].astype(F32)
    x2 = x.reshape(n, D_MODEL)
    tgt2 = target.reshape(n, D_MODEL)
    inv_freq = ROPE_THETA ** (-jnp.arange(0, MLA_ROPE, 2, dtype=F32) / MLA_ROPE)
    invf = jnp.tile(inv_freq, LANES // (MLA_ROPE // 2)).reshape(1, LANES)
    cos, sin = _rope_tables(positions.astype(F32).reshape(n, 1), invf, tm)

    u, pm, prw, z = _fwd_a(x2, g_pre, wpb, tm)
    qn, qr, kv, kr, r, w, k, v, nkk, b = _fwd_b(pm, prw, cos, sin, bw, tm, tps)
    b3 = lambda a: a.reshape(bsz, t, a.shape[-1])
    ym, lse = _attn_fwd(b3(qn), b3(qr), b3(kv), b3(kr))
    vt = _time_minor(b3(v))
    ys, st = _scan_fwd(b3(r), b3(w), b3(k), vt, b3(nkk), b3(b))
    (dys, dr_h, dk_h, dv_h, dym, dz, dxres, loss, d_lng, d_lnb, d_rk, d_wout, d_gpost) = _head(
        ys.reshape(n, RW_WIDTH), r, k, v, ym.reshape(n, MLA_WIDTH), z, x2, tgt2, hw, tm)
    dqn, dqr, dkv, dkr_heads = _attn_bwd(b3(qn), b3(qr), b3(kv), b3(kr), ym, b3(dym), lse)
    dr_s, dw_s, dk_s, dv_s, dn_s, db_s = _scan_bwd(b3(r), b3(w), b3(k), vt, b3(nkk), b3(b), st, _time_minor(b3(dys)))
    f2 = lambda a: a.reshape(n, a.shape[-1])
    cts = (f2(dqn), f2(dqr), f2(dkv), f2(dr_s), dr_h, f2(dw_s), f2(dk_s), dk_h, f2(dv_s), dv_h, f2(dn_s), f2(db_s))
    (dpm, dprw, dps, d_gq, d_wqn, d_wqr, d_gkv, d_wkv, d_mu, d_w0, d_w2p, d_a0, d_a2p, d_kk, d_ka) = _bwd_b(
        pm, prw, cos, sin, bw, cts, dkr_heads.reshape(MLA_HEADS, n, LANES), tm, tps)
    grad_x, dpb, d_gpre = _bwd_a(x2, g_pre, wpb, dpm, dprw, dps, dz, dxres, tm, tps)
    d_wp = _dw_in(u, dpb, min(1024, n), 640)

    d_w_in = jnp.concatenate([d_wp[:, :MLA_COLS], d_wp[:, PM_W:]], axis=1)
    d_w_uq = jnp.concatenate([d_wqn.reshape(Q_LORA, MLA_HEADS, MLA_NOPE),
                              d_wqr.reshape(Q_LORA, MLA_HEADS, LANES)[:, :, :MLA_ROPE]], axis=2)
    grads = {
        "norm_pre_g": d_gpre, "w_in": d_w_in[None], "mla_q_norm_g": d_gq,
        "mla_w_uq": d_w_uq.reshape(1, Q_LORA, MLA_HEADS * (MLA_NOPE + MLA_ROPE)), "mla_kv_norm_g": d_gkv,
        "mla_w_ukv": d_wkv[None], "rw_mu": d_mu, "rw_w0": d_w0, "rw_w2": d_w2p[None, :LORA], "rw_a0": d_a0,
        "rw_a2": d_a2p[None, LORA:], "rw_k_k": d_kk, "rw_k_a": d_ka, "rw_r_k": d_rk.reshape(1, RW_HEADS, RW_HEAD),
        "rw_ln_g": d_lng, "rw_ln_b": d_lnb, "w_out": d_wout[None], "norm_post_g": d_gpost,
    }
    return loss, grad_x.reshape(bsz, t, D_MODEL), grads


_MESH = pl.DeviceIdType.MESH


def _gather_shards(packed):
    rows, lanes = packed.shape

    def body(x_ref, out_ref, send_sems, recv_sems, local_sem):
        x, y, c = lax.axis_index("x"), lax.axis_index("y"), lax.axis_index("c")
        me, sibling = (x, y, c), (x, y, 1 - c)
        chips = [(1 - x, y), (x, 1 - y), (1 - x, 1 - y)]

        def slot(px, py, pc):
            return out_ref.at[4 * px + 2 * py + pc]

        def copy(k, block, to, src=None):
            return pltpu.make_async_remote_copy(
                src_ref=slot(*block) if src is None else src, dst_ref=slot(*block),
                send_sem=send_sems.at[k], recv_sem=recv_sems.at[k], device_id=to, device_id_type=_MESH)

        mine = pltpu.make_async_copy(x_ref, slot(*me), local_sem)
        mine.start()
        first = [copy(0, me, sibling, src=x_ref)]
        first += [copy(1 + j, me, (*chip, c), src=x_ref) for j, chip in enumerate(chips)]
        for cp in first:
            cp.start()
        passed = [copy(4 + j, (*chip, c), sibling) for j, chip in enumerate(chips)]
        for j, chip in enumerate(chips):
            copy(1 + j, (*chip, c), me).wait_recv()
            passed[j].start()
        copy(0, sibling, me).wait_recv()
        for j, chip in enumerate(chips):
            copy(4 + j, (*chip, 1 - c), me).wait_recv()
        for cp in first + passed:
            cp.wait_send()
        mine.wait()

    return pl.pallas_call(
        body, name="gather_shards",
        out_shape=_sds((N_DEV, rows, lanes), packed.dtype),
        in_specs=[pl.BlockSpec(memory_space=pltpu.VMEM)],
        out_specs=pl.BlockSpec(memory_space=pltpu.VMEM),
        scratch_shapes=[pltpu.SemaphoreType.DMA((7,)), pltpu.SemaphoreType.DMA((7,)), pltpu.SemaphoreType.DMA],
    )(packed)


def _exchange_grads(gbig, gsmall):
    _, rows, lanes = gbig.shape

    def body(big_ref, small_ref, rbig_ref, rsmall_ref, send_b, recv_b, send_s, recv_s, local_sem):
        x, y, c = lax.axis_index("x"), lax.axis_index("y"), lax.axis_index("c")
        me_lin = 4 * x + 2 * y + c
        mine = pltpu.make_async_copy(big_ref.at[me_lin], rbig_ref.at[0], local_sem)
        mine.start()
        rsmall_ref[me_lin] = small_ref[...]
        copies = []
        for k in range(1, N_DEV):
            px, py, pc = x ^ (k >> 2), y ^ ((k >> 1) & 1), c ^ (k & 1)
            peer = (px, py, pc)
            copies.append(pltpu.make_async_remote_copy(
                src_ref=big_ref.at[4 * px + 2 * py + pc], dst_ref=rbig_ref.at[k],
                send_sem=send_b.at[k - 1], recv_sem=recv_b.at[k - 1], device_id=peer, device_id_type=_MESH))
            copies.append(pltpu.make_async_remote_copy(
                src_ref=small_ref, dst_ref=rsmall_ref.at[me_lin],
                send_sem=send_s.at[k - 1], recv_sem=recv_s.at[k - 1], device_id=peer, device_id_type=_MESH))
        for cp in copies:
            cp.start()
        for cp in copies:
            cp.wait_recv()
        for cp in copies:
            cp.wait_send()
        mine.wait()

    return pl.pallas_call(
        body, name="exchange_grads",
        out_shape=[_sds((N_DEV, rows, lanes)), _sds((N_DEV, SMALL_ROWS, lanes))],
        in_specs=[pl.BlockSpec(memory_space=pl.ANY), pl.BlockSpec(memory_space=pltpu.VMEM)],
        out_specs=[pl.BlockSpec(memory_space=pl.ANY), pl.BlockSpec(memory_space=pltpu.VMEM)],
        scratch_shapes=[pltpu.SemaphoreType.DMA((7,)), pltpu.SemaphoreType.DMA((7,)), pltpu.SemaphoreType.DMA((7,)),
                        pltpu.SemaphoreType.DMA((7,)), pltpu.SemaphoreType.DMA],
    )(gbig, gsmall)


def _adamw_math(w, g, m, v):
    m = ADAM_B1 * m + (1.0 - ADAM_B1) * g
    v = ADAM_B2 * v + (1.0 - ADAM_B2) * (g * g)
    m_hat = m / (1.0 - ADAM_B1 ** ADAM_STEP)
    v_hat = v / (1.0 - ADAM_B2 ** ADAM_STEP)
    return -ADAM_LR * (m_hat / (jnp.sqrt(v_hat) + ADAM_EPS) + ADAM_WD * w), m, v


def _reduce_adamw(parts, w, m, v, rb):
    _, rows, lanes = parts.shape

    def body(p_ref, w_ref, m_ref, v_ref, g_out, d_out, m_out, v_out):
        g = p_ref[0]
        for s in range(1, N_DEV):
            g = g + p_ref[s]
        g_out[...] = g
        d_out[...], m_out[...], v_out[...] = _adamw_math(w_ref[...], g, m_ref[...], v_ref[...])

    blk = pl.BlockSpec((rb, lanes), lambda i: (i, 0))
    return pl.pallas_call(
        body, name=f"reduce_adamw_{rows}", grid=(rows // rb,),
        in_specs=[pl.BlockSpec((N_DEV, rb, lanes), lambda i: (0, i, 0)), blk, blk, blk],
        out_specs=[blk] * 4, out_shape=[_sds((rows, lanes))] * 4,
        compiler_params=_ARB1,
    )(parts, w, m, v)


def _pack_rows(arrs):
    return jnp.concatenate([a.reshape(-1, LANES) for a in arrs], axis=0)


def _shard_blocks(name, full):
    a = full[0]
    if name == "w_out":
        return a.reshape(N_DEV, -1, LANES)
    rows, cols = a.shape
    return a.reshape(rows, N_DEV, cols // N_DEV).transpose(1, 0, 2).reshape(N_DEV, -1, LANES)


def _unshard(name, blocks, shard_shape):
    _, rows, cols = shard_shape
    a = blocks.reshape(N_DEV, rows, cols)
    if name == "w_out":
        return a.reshape(1, N_DEV * rows, cols)
    return a.transpose(1, 0, 2).reshape(1, rows, N_DEV * cols)


def _small_pack(vals, tail):
    flat = jnp.concatenate([vals[nm].reshape(-1) for nm, _ in SMALL] + [tail])
    return jnp.pad(flat, (0, SMALL_ROWS * LANES - flat.shape[0])).reshape(SMALL_ROWS, LANES)


def _small_unpack(packed, like):
    flat, out, off = packed.reshape(-1), {}, 0
    for nm, cnt in SMALL:
        out[nm] = flat[off:off + cnt].reshape(like[nm].shape)
        off += cnt
    return out


def _big_unpack(packed, like):
    out, off = {}, 0
    for nm, cnt in SHARD_ROWS:
        out[nm] = packed[off:off + cnt].reshape(like[nm].shape)
        off += cnt
    return out


def kernel(x, positions, norm_pre_g, w_in, mla_q_norm_g, mla_w_uq, mla_kv_norm_g, mla_w_ukv, rw_mu, rw_w0, rw_w2, rw_a0, rw_a2, rw_k_k, rw_k_a, rw_r_k, rw_ln_g, rw_ln_b, w_out, norm_post_g, loss_target, m_norm_pre_g, m_w_in, m_mla_q_norm_g, m_mla_w_uq, m_mla_kv_norm_g, m_mla_w_ukv, m_rw_mu, m_rw_w0, m_rw_w2, m_rw_a0, m_rw_a2, m_rw_k_k, m_rw_k_a, m_rw_r_k, m_rw_ln_g, m_rw_ln_b, m_w_out, m_norm_post_g, v_norm_pre_g, v_w_in, v_mla_q_norm_g, v_mla_w_uq, v_mla_kv_norm_g, v_mla_w_ukv, v_rw_mu, v_rw_w0, v_rw_w2, v_rw_a0, v_rw_a2, v_rw_k_k, v_rw_k_a, v_rw_r_k, v_rw_ln_g, v_rw_ln_b, v_w_out, v_norm_post_g):
    given = dict(locals())
    w = {nm: given[nm] for nm in WEIGHTS}
    mom = {nm: given["m_" + nm] for nm in WEIGHTS}
    var = {nm: given["v_" + nm] for nm in WEIGHTS}
    sharded = [nm for nm, _ in SHARD_ROWS]

    gathered = _gather_shards(_pack_rows([w[nm] for nm in sharded]).astype(BF16))
    full, off = dict(w), 0
    for nm, cnt in SHARD_ROWS:
        full[nm] = _unshard(nm, gathered[:, off:off + cnt], w[nm].shape)
        off += cnt

    loss_part, grad_x, grads = _local_step(x, positions, loss_target, full)

    gbig = jnp.concatenate([_shard_blocks(nm, grads[nm]) for nm in sharded], axis=1)
    gsmall = _small_pack(grads, loss_part[0, 0:1])
    parts_big, parts_small = _exchange_grads(gbig, gsmall)

    g_b, d_b, m_b, v_b = _reduce_adamw(parts_big, _pack_rows([w[nm] for nm in sharded]), _pack_rows([mom[nm] for nm in sharded]),
                                       _pack_rows([var[nm] for nm in sharded]), PACK_ROWS // 8)
    zero1 = jnp.zeros((1,), F32)
    g_s, d_s, m_s, v_s = _reduce_adamw(parts_small, _small_pack(w, zero1), _small_pack(mom, zero1), _small_pack(var, zero1),
                                       SMALL_ROWS)
    loss = g_s.reshape(-1)[SMALL_N]
    outs = []
    for big, small in ((g_b, g_s), (d_b, d_s), (m_b, m_s), (v_b, v_s)):
        vals = {**_big_unpack(big, w), **_small_unpack(small, w)}
        outs += [vals[nm] for nm in WEIGHTS]
    return (loss, grad_x, *outs)
```

```python
import functools

import jax
import jax.numpy as jnp
from jax import lax
from jax.experimental import pallas as pl
from jax.experimental.pallas import tpu as pltpu

F32 = jnp.float32
BF16 = jnp.bfloat16

D_MODEL = 1024
MLA_HEADS = 4
MLA_NOPE = 128
MLA_ROPE = 64
MLA_V = 128
MLA_WIDTH = MLA_HEADS * MLA_V
Q_LORA = 256
KV_LORA = 128
ROPE_THETA = 10000.0
RW_HEAD = 64
RW_WIDTH = 512
RW_HEADS = RW_WIDTH // RW_HEAD
LORA = 64
RW_COLS = 3 * RW_WIDTH + 2 * LORA
MLA_COLS = Q_LORA + KV_LORA + MLA_ROPE
D_IN = MLA_COLS + RW_COLS + D_MODEL
RW_GN_EPS = 64e-5
NORM_EPS = 1e-6
ATT_SCALE = (MLA_NOPE + MLA_ROPE) ** -0.5
ADAM_LR, ADAM_B1, ADAM_B2, ADAM_EPS, ADAM_WD, ADAM_STEP = 0.001, 0.9, 0.999, 1e-08, 0.01, 10
N_DEV = 8
LANES = 128
MXU = 256

PM_W = 512
WP_COLS = PM_W + RW_COLS + D_MODEL
RW_PIECES = ((0, 512), (512, 1024), (1024, 1536), (1536, 1664))

SHARD_ROWS = (("w_in", 1024 * 392 // LANES), ("mla_w_uq", 256 * 96 // LANES), ("mla_w_ukv", 128 * 128 // LANES),
              ("rw_w2", 64 * 64 // LANES), ("rw_a2", 64 * 64 // LANES), ("w_out", 128 * 1024 // LANES))
PACK_ROWS = sum(r for _, r in SHARD_ROWS)
SMALL = (("norm_pre_g", 1024), ("mla_q_norm_g", 256), ("mla_kv_norm_g", 128), ("rw_mu", 1664), ("rw_w0", 512),
         ("rw_a0", 512), ("rw_k_k", 512), ("rw_k_a", 512), ("rw_r_k", 512), ("rw_ln_g", 512), ("rw_ln_b", 512),
         ("norm_post_g", 1024))
SMALL_N = sum(n for _, n in SMALL)
SMALL_ROWS = 72
WEIGHTS = ("norm_pre_g", "w_in", "mla_q_norm_g", "mla_w_uq", "mla_kv_norm_g", "mla_w_ukv", "rw_mu", "rw_w0", "rw_w2",
           "rw_a0", "rw_a2", "rw_k_k", "rw_k_a", "rw_r_k", "rw_ln_g", "rw_ln_b", "w_out", "norm_post_g")


def _seg_ones():
    r = lax.broadcasted_iota(jnp.int32, (MXU, MXU), 0) >> 6
    c = lax.broadcasted_iota(jnp.int32, (MXU, MXU), 1) >> 6
    return jnp.where(r == c, 1.0, 0.0).astype(BF16)


def _seg_dot(x, ones, passes):
    parts, rem = [], x
    for p in range(passes):
        hb = rem.astype(BF16)
        parts.append(hb)
        if p + 1 < passes:
            rem = rem - hb.astype(F32)
    outs = []
    for j in range(x.shape[1] // MXU):
        acc = None
        for hb in parts:
            d = jnp.dot(hb[:, MXU * j:MXU * (j + 1)], ones, preferred_element_type=F32)
            acc = d if acc is None else acc + d
        outs.append(acc)
    return outs[0] if len(outs) == 1 else jnp.concatenate(outs, axis=1)


def _seg_multi(xs, ones, passes):
    his = [x.astype(BF16) for x in xs]
    hi = jnp.concatenate(his, axis=0)
    if passes == 2:
        lo = jnp.concatenate([(x - h.astype(F32)).astype(BF16) for x, h in zip(xs, his)], axis=0)
        rhs = jnp.concatenate([ones, ones], axis=0)
    halves = []
    for j in range(hi.shape[1] // MXU):
        sl = slice(MXU * j, MXU * (j + 1))
        if passes == 2:
            halves.append(jnp.dot(jnp.concatenate([hi[:, sl], lo[:, sl]], axis=1), rhs, preferred_element_type=F32))
        else:
            halves.append(jnp.dot(hi[:, sl], ones, preferred_element_type=F32))
    full = jnp.concatenate(halves, axis=1)
    m = xs[0].shape[0]
    return [full[m * i:m * (i + 1)] for i in range(len(xs))]


@jax.custom_vjp
def _segsum(x):
    return _seg_dot(x, _seg_ones(), 3)


_segsum.defvjp(lambda x: (_segsum(x), None), lambda _, g: (_segsum(g),))


@jax.custom_vjp
def _bdot(a, w):
    return jnp.dot(a.astype(BF16), w.astype(BF16), preferred_element_type=F32)


def _bdot_fwd(a, w):
    return _bdot(a, w), (a, w)


def _bdot_bwd(res, g):
    a, w = res
    gb = g.astype(BF16)
    da = lax.dot_general(gb, w.astype(BF16), (((1,), (1,)), ((), ())), preferred_element_type=F32)
    dw = lax.dot_general(a.astype(BF16), gb, (((0,), (0,)), ((), ())), preferred_element_type=F32)
    return da, dw


_bdot.defvjp(_bdot_fwd, _bdot_bwd)


def _rot_impl(x):
    w = x.shape[1]
    lane = lax.broadcasted_iota(jnp.int32, x.shape, 1)
    return jnp.where((lane & 63) < 32, -pltpu.roll(x, w - 32, 1), pltpu.roll(x, 32, 1))


@jax.custom_vjp
def _rot(x):
    return _rot_impl(x)


_rot.defvjp(lambda x: (_rot_impl(x), None), lambda _, g: (-_rot_impl(g),))


def _rms(x, g):
    return x * lax.rsqrt(jnp.mean(x * x, axis=-1, keepdims=True) + NORM_EPS) * g


def _shift_rows(p, prev_row):
    row = lax.broadcasted_iota(jnp.int32, p.shape, 0)
    return jnp.where(row == 0, prev_row, pltpu.roll(p, 1, 0))


def _unshift_rows(g, next_row):
    row = lax.broadcasted_iota(jnp.int32, g.shape, 0)
    return jnp.where(row == g.shape[0] - 1, next_row, pltpu.roll(g, g.shape[0] - 1, 0))


def _f_mla(cq, ckv, kr, cos, sin, g_q, wqn, wqr, g_kv, wkv):
    qn = _rms(cq, g_q)
    q_nope = _bdot(qn, wqn)
    q_r = _bdot(qn, wqr)
    cos4 = jnp.concatenate([cos] * MLA_HEADS, axis=1)
    sin4 = jnp.concatenate([sin] * MLA_HEADS, axis=1)
    q_rope = q_r * cos4 + _rot(q_r) * sin4
    kv = _bdot(_rms(ckv, g_kv), wkv)
    k_rope = kr * cos + _rot(kr) * sin
    return q_nope, q_rope, kv, k_rope


def _f_rw(pr, pk, pv, pt, sr, sk, sv, st, mu_r, mu_k, mu_v, mu_t, w0, w2p, a0, a2p, k_k, k_a):
    r = pr + (sr - pr) * mu_r
    k = pk + (sk - pk) * mu_k
    v = pv + (sv - pv) * mu_v
    t = pt + (st - pt) * mu_t
    nwl = -(w0 + _bdot(jnp.tanh(t), w2p))
    softplus = jnp.maximum(nwl, 0.0) + jnp.log(1.0 + jnp.exp(-jnp.abs(nwl)))
    decay = jnp.exp(-jnp.exp(-softplus - 0.5))
    a = jax.nn.sigmoid(a0 + _bdot(t, a2p))
    kk = k * k_k
    kk = kk / jnp.maximum(jnp.sqrt(_segsum(kk * kk)), 1e-12)
    k2 = k * (1.0 + (a - 1.0) * k_a)
    return r, decay, k2, v, -kk, kk * a


def _f_head(ys, r, k, v, ym, z1, z2, x, tgt, ln_g, ln_b, r_k, w1, w2, g_post):
    inv = 1.0 / RW_HEAD
    yc = ys - _segsum(ys) * inv
    var = _segsum(yc * yc) * inv
    y = yc * lax.rsqrt(var + RW_GN_EPS) * ln_g + ln_b
    y_rw = y + _segsum(r * k * r_k) * v
    c1 = ym * (z1 * jax.nn.sigmoid(z1))
    c2 = y_rw * (z2 * jax.nn.sigmoid(z2))
    out = _bdot(c1, w1) + _bdot(c2, w2)
    err = x + _rms(out, g_post) - tgt
    per_row = jnp.sum(err * err, axis=1, keepdims=True)
    return jnp.sum(per_row, axis=0, keepdims=True) * (0.5 / D_MODEL)


def _rows(tm, width):
    return pl.BlockSpec((tm, width), lambda i: (i, 0))


def _whole(shape):
    zeros = (0,) * len(shape)
    return pl.BlockSpec(shape, lambda i: zeros)


def _sds(shape, dtype=F32):
    return jax.ShapeDtypeStruct(shape, dtype)


_ARB1 = pltpu.CompilerParams(dimension_semantics=("arbitrary",))


def _acc(ref, val, first):
    @pl.when(first)
    def _():
        ref[...] = val

    @pl.when(jnp.logical_not(first))
    def _():
        ref[...] += val


def _fwd_a(x2, g_pre, wp, tm):
    n = x2.shape[0]

    def body(x_ref, g_ref, w_ref, u_ref, pm_ref, prw_ref, z_ref):
        ub = _rms(x_ref[...], g_ref[...]).astype(BF16)
        u_ref[...] = ub
        pm_ref[...] = jnp.dot(ub, w_ref[:, 0:PM_W], preferred_element_type=F32)
        prw_ref[...] = jnp.dot(ub, w_ref[:, PM_W:PM_W + RW_COLS], preferred_element_type=F32)
        z_ref[...] = jnp.dot(ub, w_ref[:, PM_W + RW_COLS:WP_COLS], preferred_element_type=F32)

    return pl.pallas_call(
        body, name="fwd_a", grid=(n // tm,),
        in_specs=[_rows(tm, D_MODEL), _whole((1, D_MODEL)), _whole((D_MODEL, WP_COLS))],
        out_specs=[_rows(tm, D_MODEL), _rows(tm, PM_W), _rows(tm, RW_COLS), _rows(tm, D_MODEL)],
        out_shape=[_sds((n, D_MODEL), BF16), _sds((n, PM_W)), _sds((n, RW_COLS)), _sds((n, D_MODEL))],
        compiler_params=_ARB1,
    )(x2, g_pre, wp)


def _rope_tables(posf, invf, tm):
    n = posf.shape[0]

    def body(p_ref, f_ref, c_ref, s_ref):
        ang = p_ref[...] * f_ref[...]
        c_ref[...] = jnp.cos(ang)
        s_ref[...] = jnp.sin(ang)

    return pl.pallas_call(
        body, name="rope_tables", grid=(n // tm,),
        in_specs=[_rows(tm, 1), _whole((1, LANES))],
        out_specs=[_rows(tm, LANES), _rows(tm, LANES)],
        out_shape=[_sds((n, LANES)), _sds((n, LANES))],
        compiler_params=_ARB1,
    )(posf, invf)


_B_WEIGHT_SHAPES = ((1, Q_LORA), (Q_LORA, 512), (Q_LORA, 512), (1, KV_LORA), (KV_LORA, 1024), (1, RW_COLS), (1, RW_WIDTH),
                    (LANES, RW_WIDTH), (1, RW_WIDTH), (LANES, RW_WIDTH), (1, RW_WIDTH), (1, RW_WIDTH))


def _halo_prev(tm):
    return pl.BlockSpec((8, RW_COLS), lambda i: (jnp.maximum(i * (tm // 8) - 1, 0), 0))


def _b_operands(pm_ref, prw_ref, halo_ref, wrefs, tile, tiles_per_seq):
    g_q, wqn, wqr, g_kv, wkv, mu, w0, w2p, a0, a2p, k_k, k_a = wrefs
    mla_in = (pm_ref[:, 0:Q_LORA], pm_ref[:, Q_LORA:Q_LORA + KV_LORA], pm_ref[:, Q_LORA + KV_LORA:PM_W])
    mla_w = (g_q[...], wqn[...], wqr[...], g_kv[...], wkv[...])
    keep = jnp.where(tile % tiles_per_seq == 0, 0.0, 1.0)
    prev = halo_ref[7:8, :] * keep
    ps = tuple(prw_ref[:, a:b] for a, b in RW_PIECES)
    ss = tuple(_shift_rows(p, prev[:, a:b]) for p, (a, b) in zip(ps, RW_PIECES))
    rw_w = tuple(mu[:, a:b] for a, b in RW_PIECES) + (w0[...], w2p[...], a0[...], a2p[...], k_k[...], k_a[...])
    return mla_in, mla_w, ps + ss, rw_w


def _fwd_b(pm, prw, cos, sin, bw, tm, tiles_per_seq):
    n = pm.shape[0]

    def body(pm_ref, prw_ref, halo_ref, cos_ref, sin_ref, *refs):
        wrefs, outs = refs[:12], refs[12:]
        mla_in, mla_w, rw_in, rw_w = _b_operands(pm_ref, prw_ref, halo_ref, wrefs, pl.program_id(0), tiles_per_seq)
        res = _f_mla(*mla_in, cos_ref[...], sin_ref[...], *mla_w) + _f_rw(*rw_in, *rw_w)
        for o_ref, val in zip(outs, res):
            o_ref[...] = val.astype(o_ref.dtype)

    widths = (512, 512, 1024, LANES) + (RW_WIDTH,) * 6
    return pl.pallas_call(
        body, name="fwd_b", grid=(n // tm,),
        in_specs=[_rows(tm, PM_W), _rows(tm, RW_COLS), _halo_prev(tm), _rows(tm, LANES), _rows(tm, LANES)]
        + [_whole(s) for s in _B_WEIGHT_SHAPES],
        out_specs=[_rows(tm, w) for w in widths],
        out_shape=[_sds((n, w), BF16 if j < 4 else F32) for j, w in enumerate(widths)],
        compiler_params=_ARB1,
    )(pm, prw, prw, cos, sin, *bw)


def _bwd_b(pm, prw, cos, sin, bw, cts, dkr_heads, tm, tiles_per_seq):
    n = pm.shape[0]

    ct_widths = (512, 512, 1024) + (RW_WIDTH,) * 9
    n_ct = len(ct_widths)

    def body(pm_ref, prw_ref, halo_ref, cos_ref, sin_ref, *refs):
        wrefs, ct_refs, dkr_ref = refs[:12], refs[12:12 + n_ct], refs[12 + n_ct]
        dpm_ref, dprw_ref, dps_ref = refs[13 + n_ct:16 + n_ct]
        wg_refs = refs[16 + n_ct:]
        tile = pl.program_id(0)
        first = tile == 0
        mla_in, mla_w, rw_in, rw_w = _b_operands(pm_ref, prw_ref, halo_ref, wrefs, tile, tiles_per_seq)
        cos, sin = cos_ref[...], sin_ref[...]
        ct = [r[...] for r in ct_refs]
        _, vjp_mla = jax.vjp(lambda *a: _f_mla(*a[:3], cos, sin, *a[3:]), *mla_in, *mla_w)
        dkr = dkr_ref[0] + dkr_ref[1] + dkr_ref[2] + dkr_ref[3]
        d_mla = vjp_mla((ct[0], ct[1], ct[2], dkr))
        dpm_ref[:, 0:Q_LORA] = d_mla[0]
        dpm_ref[:, Q_LORA:Q_LORA + KV_LORA] = d_mla[1]
        dpm_ref[:, Q_LORA + KV_LORA:PM_W] = d_mla[2]
        _, vjp_rw = jax.vjp(_f_rw, *rw_in, *rw_w)
        d_rw = vjp_rw((ct[3] + ct[4], ct[5], ct[6] + ct[7], ct[8] + ct[9], ct[10], ct[11]))
        for j, (a, b) in enumerate(RW_PIECES):
            dprw_ref[:, a:b] = d_rw[j]
            dps_ref[:, a:b] = d_rw[4 + j]
        g_q, wqn, wqr, g_kv, wkv, mu, w0, w2p, a0, a2p, k_k, k_a = wg_refs
        for ref, val in zip((g_q, wqn, wqr, g_kv, wkv), d_mla[3:]):
            _acc(ref, val, first)
        for j, (a, b) in enumerate(RW_PIECES):
            _acc(mu.at[:, a:b], d_rw[8 + j], first)
        for ref, val in zip((w0, w2p, a0, a2p, k_k, k_a), d_rw[12:]):
            _acc(ref, val, first)

    return pl.pallas_call(
        body, name="bwd_b", grid=(n // tm,),
        in_specs=[_rows(tm, PM_W), _rows(tm, RW_COLS), _halo_prev(tm), _rows(tm, LANES), _rows(tm, LANES)]
        + [_whole(s) for s in _B_WEIGHT_SHAPES] + [_rows(tm, w) for w in ct_widths]
        + [pl.BlockSpec((MLA_HEADS, tm, LANES), lambda i: (0, i, 0))],
        out_specs=[_rows(tm, PM_W), _rows(tm, RW_COLS), _rows(tm, RW_COLS)] + [_whole(s) for s in _B_WEIGHT_SHAPES],
        out_shape=[_sds((n, PM_W)), _sds((n, RW_COLS)), _sds((n, RW_COLS))] + [_sds(s) for s in _B_WEIGHT_SHAPES],
        compiler_params=_ARB1,
    )(pm, prw, prw, cos, sin, *bw, *cts, dkr_heads)


def _head(ys, r, k, v, ym, z, x2, tgt, hw, tm):
    n = x2.shape[0]
    h_shapes = ((1, RW_WIDTH), (1, RW_WIDTH), (1, RW_WIDTH), (D_MODEL, D_MODEL), (1, D_MODEL))

    def body(ys_ref, r_ref, k_ref, v_ref, ym_ref, z_ref, x_ref, t_ref, lng, lnb, rk, wout, gpost,
             dys_ref, dr_ref, dk_ref, dv_ref, dym_ref, dz_ref, dx_ref, loss_ref, dlng, dlnb, drk, dwout, dgpost):
        first = pl.program_id(0) == 0
        tgt_v = t_ref[...]
        args = (ys_ref[...], r_ref[...], k_ref[...], v_ref[...], ym_ref[...], z_ref[:, 0:MLA_WIDTH], z_ref[:, MLA_WIDTH:D_MODEL],
                x_ref[...], lng[...], lnb[...], rk[...], wout[0:MLA_WIDTH, :], wout[MLA_WIDTH:D_MODEL, :], gpost[...])
        loss, vjp = jax.vjp(lambda *a: _f_head(*a[:8], tgt_v, *a[8:]), *args)
        d = vjp(jnp.ones((1, 1), F32))
        dys_ref[...] = d[0]
        dr_ref[...] = d[1]
        dk_ref[...] = d[2]
        dv_ref[...] = d[3]
        dym_ref[...] = d[4].astype(BF16)
        dz_ref[:, 0:MLA_WIDTH] = d[5]
        dz_ref[:, MLA_WIDTH:D_MODEL] = d[6]
        dx_ref[...] = d[7]
        _acc(loss_ref, jnp.broadcast_to(loss, (8, LANES)), first)
        _acc(dlng, d[8], first)
        _acc(dlnb, d[9], first)
        _acc(drk, d[10], first)
        _acc(dwout.at[0:MLA_WIDTH, :], d[11], first)
        _acc(dwout.at[MLA_WIDTH:D_MODEL, :], d[12], first)
        _acc(dgpost, d[13], first)

    widths = (RW_WIDTH,) * 4 + (MLA_WIDTH, D_MODEL, D_MODEL)
    return pl.pallas_call(
        body, name="head", grid=(n // tm,),
        in_specs=[_rows(tm, RW_WIDTH)] * 4 + [_rows(tm, MLA_WIDTH), _rows(tm, D_MODEL), _rows(tm, D_MODEL), _rows(tm, D_MODEL)]
        + [_whole(s) for s in h_shapes],
        out_specs=[_rows(tm, w) for w in widths] + [_whole((8, LANES))] + [_whole(s) for s in h_shapes],
        out_shape=[_sds((n, w), BF16 if j == 4 else F32) for j, w in enumerate(widths)] + [_sds((8, LANES))]
        + [_sds(s) for s in h_shapes],
        compiler_params=_ARB1,
    )(ys, r, k, v, ym, z, x2, tgt, *hw)


def _halo_next(tm, n):
    last = n // 8 - 1
    return pl.BlockSpec((8, RW_COLS), lambda i: (jnp.minimum((i + 1) * (tm // 8), last), 0))


def _bwd_a(x2, g_pre, wp, dpm, dprw, dps, dz, dxres, tm, tiles_per_seq):
    n = x2.shape[0]
    nt_dims = (((1,), (1,)), ((), ()))

    def body(x_ref, g_ref, w_ref, dpm_ref, dprw_ref, dps_ref, nxt_ref, dz_ref, dxres_ref, gx_ref, dpb_ref, dg_ref):
        tile = pl.program_id(0)
        keep = jnp.where((tile + 1) % tiles_per_seq == 0, 0.0, 1.0)
        dprw_v = dprw_ref[...] + _unshift_rows(dps_ref[...], nxt_ref[0:1, :] * keep)
        dpm_b, dprw_b, dz_b = dpm_ref[...].astype(BF16), dprw_v.astype(BF16), dz_ref[...].astype(BF16)
        dpb_ref[:, 0:PM_W] = dpm_b
        dpb_ref[:, PM_W:PM_W + RW_COLS] = dprw_b
        dpb_ref[:, PM_W + RW_COLS:WP_COLS] = dz_b
        du = (lax.dot_general(dpm_b, w_ref[:, 0:PM_W], nt_dims, preferred_element_type=F32)
              + lax.dot_general(dprw_b, w_ref[:, PM_W:PM_W + RW_COLS], nt_dims, preferred_element_type=F32)
              + lax.dot_general(dz_b, w_ref[:, PM_W + RW_COLS:WP_COLS], nt_dims, preferred_element_type=F32))
        x = x_ref[...]
        xhat = x * lax.rsqrt(jnp.mean(x * x, axis=-1, keepdims=True) + NORM_EPS)
        dxn = du * g_ref[...]
        dx = (dxn - xhat * jnp.mean(dxn * xhat, axis=-1, keepdims=True)) * lax.rsqrt(jnp.mean(x * x, axis=-1, keepdims=True) + NORM_EPS)
        gx_ref[...] = dx + dxres_ref[...]
        _acc(dg_ref, jnp.sum(du * xhat, axis=0, keepdims=True), tile == 0)

    return pl.pallas_call(
        body, name="bwd_a", grid=(n // tm,),
        in_specs=[_rows(tm, D_MODEL), _whole((1, D_MODEL)), _whole((D_MODEL, WP_COLS)), _rows(tm, PM_W), _rows(tm, RW_COLS),
                  _rows(tm, RW_COLS), _halo_next(tm, n), _rows(tm, D_MODEL), _rows(tm, D_MODEL)],
        out_specs=[_rows(tm, D_MODEL), _rows(tm, WP_COLS), _whole((1, D_MODEL))],
        out_shape=[_sds((n, D_MODEL)), _sds((n, WP_COLS), BF16), _sds((1, D_MODEL))],
        compiler_params=_ARB1,
    )(x2, g_pre, wp, dpm, dprw, dps, dps, dz, dxres)


def _dw_in(u, dpb, tk, tn):
    n = u.shape[0]

    def body(u_ref, d_ref, o_ref):
        val = lax.dot_general(u_ref[...], d_ref[...], (((0,), (0,)), ((), ())), preferred_element_type=F32)
        _acc(o_ref, val, pl.program_id(1) == 0)

    return pl.pallas_call(
        body, name="dw_in", grid=(WP_COLS // tn, n // tk),
        in_specs=[pl.BlockSpec((tk, D_MODEL), lambda j, k: (k, 0)), pl.BlockSpec((tk, tn), lambda j, k: (k, j))],
        out_specs=pl.BlockSpec((D_MODEL, tn), lambda j, k: (0, j)),
        out_shape=_sds((D_MODEL, WP_COLS)),
        compiler_params=pltpu.CompilerParams(dimension_semantics=("arbitrary", "arbitrary")),
    )(u, dpb)


ATT_BLK = 256
_NT = (((1,), (1,)), ((), ()))
_TN = (((0,), (0,)), ((), ()))


def _causal(q0, k0, blk):
    row = q0 + lax.broadcasted_iota(jnp.int32, (blk, blk), 0)
    col = k0 + lax.broadcasted_iota(jnp.int32, (blk, blk), 1)
    return row >= col


def _attn_fwd(qn, qr, kv, kr):
    bsz, t, _ = qn.shape
    blk = min(ATT_BLK, t)

    heads = range(MLA_HEADS)

    def body(qn_ref, qr_ref, kv_ref, kr_ref, o_ref, lse_ref):
        qi = pl.program_id(1)
        q = [jnp.concatenate([qn_ref[:, LANES * h:LANES * (h + 1)], qr_ref[:, LANES * h:LANES * (h + 1)]], axis=1) for h in heads]
        lower = _causal(0, 0, blk)

        def kv_step(j, carry, diagonal):
            ks = pl.multiple_of(j * blk, blk)
            k_rope = kr_ref[pl.ds(ks, blk), :]
            out = []
            for h in heads:
                m, l, acc = carry[h]
                k = jnp.concatenate([kv_ref[pl.ds(ks, blk), 2 * LANES * h:2 * LANES * h + LANES], k_rope], axis=1)
                s = lax.dot_general(q[h], k, _NT, preferred_element_type=F32) * ATT_SCALE
                if diagonal:
                    s = jnp.where(lower, s, -1e30)
                m_new = jnp.maximum(m, jnp.max(s, axis=1, keepdims=True))
                alpha = jnp.exp(m - m_new)
                p = jnp.exp(s - m_new)
                l = alpha * l + jnp.sum(p, axis=1, keepdims=True)
                v = kv_ref[pl.ds(ks, blk), 2 * LANES * h + LANES:2 * LANES * (h + 1)]
                acc = alpha * acc + jnp.dot(p.astype(BF16), v, preferred_element_type=F32)
                out.append((m_new, l, acc))
            return tuple(out)

        one = (jnp.full((blk, 1), -1e30, F32), jnp.zeros((blk, 1), F32), jnp.zeros((blk, MLA_V), F32))
        carry = lax.fori_loop(0, qi, lambda j, c: kv_step(j, c, False), (one,) * MLA_HEADS)
        carry = kv_step(qi, carry, True)
        for h in heads:
            m, l, acc = carry[h]
            o_ref[:, LANES * h:LANES * (h + 1)] = acc / l
            lse_ref[h] = jnp.broadcast_to(m + jnp.log(l), (blk, LANES))

    return pl.pallas_call(
        body, name="attn_fwd", grid=(bsz, t // blk),
        in_specs=[pl.BlockSpec((None, blk, MLA_WIDTH), lambda b, i: (b, i, 0)),
                  pl.BlockSpec((None, blk, MLA_WIDTH), lambda b, i: (b, i, 0)),
                  pl.BlockSpec((None, t, 2 * MLA_WIDTH), lambda b, i: (b, 0, 0)),
                  pl.BlockSpec((None, t, LANES), lambda b, i: (b, 0, 0))],
        out_specs=[pl.BlockSpec((None, blk, MLA_WIDTH), lambda b, i: (b, i, 0)),
                   pl.BlockSpec((None, MLA_HEADS, blk, LANES), lambda b, i: (b, 0, i, 0))],
        out_shape=[_sds((bsz, t, MLA_WIDTH)), _sds((bsz, MLA_HEADS, t, LANES))],
        compiler_params=pltpu.CompilerParams(dimension_semantics=("arbitrary", "arbitrary")),
    )(qn, qr, kv, kr)


def _attn_bwd(qn, qr, kv, kr, o, do, lse):
    bsz, t, _ = qn.shape
    blk = min(ATT_BLK, t)
    nb = t // blk

    def body(qn_ref, qr_ref, kn_ref, kr_ref, v_ref, o_ref, do_ref, lse_ref, dqn_ref, dqr_ref, dkv_ref, dkr_ref, dq_sc, delta_sc):
        dq_sc[...] = jnp.zeros_like(dq_sc)
        delta_sc[...] = jnp.sum(do_ref[...].astype(F32) * o_ref[...], axis=1, keepdims=True)

        def kv_loop(j, _):
            ks = pl.multiple_of(j * blk, blk)
            k = jnp.concatenate([kn_ref[pl.ds(ks, blk), :], kr_ref[pl.ds(ks, blk), :]], axis=1)
            vb = v_ref[pl.ds(ks, blk), :]

            def q_pair(pair, carry):
                dk, dv = carry
                for sub in range(2):
                    i = 2 * pair + sub
                    qs = pl.multiple_of(i * blk, blk)
                    q = jnp.concatenate([qn_ref[pl.ds(qs, blk), :], qr_ref[pl.ds(qs, blk), :]], axis=1)
                    dob = do_ref[pl.ds(qs, blk), :]
                    s = lax.dot_general(q, k, _NT, preferred_element_type=F32) * ATT_SCALE
                    p = jnp.where(_causal(i * blk, j * blk, blk), jnp.exp(s - lse_ref[pl.ds(qs, blk), 0:1]), 0.0)
                    dv = dv + lax.dot_general(p.astype(BF16), dob, _TN, preferred_element_type=F32)
                    dp = lax.dot_general(dob, vb, _NT, preferred_element_type=F32)
                    ds = (p * (dp - delta_sc[pl.ds(qs, blk), :]) * ATT_SCALE).astype(BF16)
                    dq_sc[pl.ds(qs, blk), :] += jnp.dot(ds, k, preferred_element_type=F32)
                    dk = dk + lax.dot_general(ds, q, _TN, preferred_element_type=F32)
                return dk, dv

            dk, dv = lax.fori_loop(j // 2, nb // 2, q_pair, (jnp.zeros((blk, 2 * LANES), F32), jnp.zeros((blk, MLA_V), F32)))
            dkv_ref[pl.ds(ks, blk), 0:LANES] = dk[:, 0:LANES]
            dkv_ref[pl.ds(ks, blk), LANES:2 * LANES] = dv
            dkr_ref[pl.ds(ks, blk), :] = dk[:, LANES:2 * LANES]
            return 0

        lax.fori_loop(0, nb, kv_loop, 0)
        dqn_ref[...] = dq_sc[:, 0:LANES]
        dqr_ref[...] = dq_sc[:, LANES:2 * LANES]

    head_col = lambda b, h: (b, 0, h)
    return pl.pallas_call(
        body, name="attn_bwd", grid=(bsz, MLA_HEADS),
        in_specs=[pl.BlockSpec((None, t, LANES), head_col), pl.BlockSpec((None, t, LANES), head_col),
                  pl.BlockSpec((None, t, LANES), lambda b, h: (b, 0, 2 * h)),
                  pl.BlockSpec((None, t, LANES), lambda b, h: (b, 0, 0)),
                  pl.BlockSpec((None, t, LANES), lambda b, h: (b, 0, 2 * h + 1)),
                  pl.BlockSpec((None, t, LANES), head_col), pl.BlockSpec((None, t, LANES), head_col),
                  pl.BlockSpec((None, None, t, LANES), lambda b, h: (b, h, 0, 0))],
        out_specs=[pl.BlockSpec((None, t, LANES), head_col), pl.BlockSpec((None, t, LANES), head_col),
                   pl.BlockSpec((None, t, 2 * LANES), head_col),
                   pl.BlockSpec((None, None, t, LANES), lambda b, h: (h, b, 0, 0))],
        out_shape=[_sds((bsz, t, MLA_WIDTH)), _sds((bsz, t, MLA_WIDTH)), _sds((bsz, t, 2 * MLA_WIDTH)),
                   _sds((MLA_HEADS, bsz, t, LANES))],
        scratch_shapes=[pltpu.VMEM((t, 2 * LANES), F32), pltpu.VMEM((t, 1), F32)],
        compiler_params=pltpu.CompilerParams(dimension_semantics=("arbitrary", "arbitrary")),
    )(qn, qr, kv, kr, kv, o, do, lse)


SCAN_CHUNK = 16


def _diag_mask():
    row = lax.broadcasted_iota(jnp.int32, (RW_HEAD, RW_WIDTH), 0)
    lane = lax.broadcasted_iota(jnp.int32, (RW_HEAD, RW_WIDTH), 1)
    return jnp.where(row == (lane & (RW_HEAD - 1)), 1.0, 0.0)


def _time_minor(a):
    bsz, t, _ = a.shape
    a = a.reshape(bsz, t // SCAN_CHUNK, SCAN_CHUNK, RW_HEADS, RW_HEAD)
    return a.transpose(0, 1, 4, 3, 2).reshape(bsz, t // SCAN_CHUNK, RW_HEAD, RW_HEADS * SCAN_CHUNK)


def _head_expand():
    l = lax.broadcasted_iota(jnp.int32, (2 * LANES, RW_WIDTH), 0)
    n = lax.broadcasted_iota(jnp.int32, (2 * LANES, RW_WIDTH), 1)
    return jnp.where(((l & (LANES - 1)) >> 4) == (n >> 6), 1.0, 0.0).astype(BF16)


def _col_bcast_chunk(tm_ref, out_sc, expand, seqs):
    step_of_lane = lax.broadcasted_iota(jnp.int32, (RW_HEAD, LANES), 1) & (SCAN_CHUNK - 1)
    tiles = [tm_ref[bi, 0] for bi in seqs]
    for t in range(SCAN_CHUNK):
        parts = []
        for tile in tiles:
            a = jnp.where(step_of_lane == t, tile, 0.0)
            hi = a.astype(BF16)
            parts.append(jnp.concatenate([hi, (a - hi.astype(F32)).astype(BF16)], axis=1))
        out_sc[t] = jnp.dot(jnp.concatenate(parts, axis=0), expand, preferred_element_type=F32)


def _fold8(x):
    acc = x[0:8]
    for j in range(1, x.shape[0] // 8):
        acc = acc + x[8 * j:8 * (j + 1)]
    return acc


def _rows8(at):
    return pl.ds(at * 8 if isinstance(at, int) else pl.multiple_of(at * 8, 8), 8)


def _put8(sc, bi, at, val):
    for j in range(RW_WIDTH // LANES):
        sc[bi * (RW_WIDTH // LANES) + j, _rows8(at), :] = val[:, LANES * j:LANES * (j + 1)]


def _unfold8(sc, bi, steps):
    tiles = []
    for j in range(RW_WIDTH // LANES):
        view = sc.at[bi * (RW_WIDTH // LANES) + j]
        acc = view[pl.ds(0, steps, stride=8), :]
        for s in range(1, 8):
            acc = acc + view[pl.ds(s, steps, stride=8), :]
        tiles.append(acc)
    return jnp.concatenate(tiles, axis=1)


def _scan_fwd(r, w, k, vt, nkk, b):
    bsz, t, _ = r.shape
    tc = SCAN_CHUNK

    def body(r_ref, w_ref, k_ref, n_ref, b_ref, vt_ref, y_ref, st_ref, s_sc, vc_sc, y_sc):
        @pl.when(pl.program_id(0) == 0)
        def _():
            s_sc[...] = jnp.zeros_like(s_sc)

        ones = _seg_ones()
        diag = _diag_mask()
        seqs = range(bsz)
        _col_bcast_chunk(vt_ref, vc_sc, _head_expand(), seqs)

        def put_y(ya, at):
            for bi in seqs:
                _put8(y_sc, bi, at, _fold8(ya[bi] * diag))

        def step(i, _):
            row = lambda ref, bi: ref[bi, pl.ds(i, 1), :]
            prev = jnp.maximum(i - 1, 0)
            s_old = [s_sc[bi] for bi in seqs]
            both = _seg_multi([s_old[bi] * row(n_ref, bi) for bi in seqs]
                              + [s_old[bi] * r_ref[bi, pl.ds(prev, 1), :] for bi in seqs], ones, 1)
            sa = both[:bsz]
            put_y(both[bsz:], prev)
            vc = vc_sc[i]
            for bi in seqs:
                s_new = (s_old[bi] * row(w_ref, bi) + sa[bi] * row(b_ref, bi)
                         + vc[RW_HEAD * bi:RW_HEAD * (bi + 1)] * row(k_ref, bi))
                s_sc[bi] = s_new
                st_ref[bi, i] = s_new
            return 0

        lax.fori_loop(0, tc, step, 0)
        put_y(_seg_multi([s_sc[bi] * r_ref[bi, tc - 1:tc, :] for bi in seqs], ones, 1), tc - 1)
        for bi in seqs:
            y_ref[bi] = _unfold8(y_sc, bi, tc)

    vec = pl.BlockSpec((bsz, tc, RW_WIDTH), lambda c: (0, c, 0))
    return pl.pallas_call(
        body, name="scan_fwd", grid=(t // tc,),
        in_specs=[vec] * 5 + [pl.BlockSpec((bsz, 1, RW_HEAD, LANES), lambda c: (0, c, 0, 0))],
        out_specs=[vec, pl.BlockSpec((bsz, tc, RW_HEAD, RW_WIDTH), lambda c: (0, c, 0, 0))],
        out_shape=[_sds((bsz, t, RW_WIDTH)), _sds((bsz, t, RW_HEAD, RW_WIDTH))],
        scratch_shapes=[pltpu.VMEM((bsz, RW_HEAD, RW_WIDTH), F32), pltpu.VMEM((tc, bsz * RW_HEAD, RW_WIDTH), F32),
                        pltpu.VMEM((bsz * RW_WIDTH // LANES, tc * 8, LANES), F32)],
        compiler_params=_ARB1,
    )(r, w, k, nkk, b, vt)


def _scan_bwd(r, w, k, vt, nkk, b, st, dyt):
    bsz, t, _ = r.shape
    tc = SCAN_CHUNK
    nc = t // tc

    def body(r_ref, w_ref, k_ref, n_ref, b_ref, vt_ref, dyt_ref, st_ref, halo_ref,
             dr_ref, dw_ref, dk_ref, dv_ref, dn_ref, db_ref, g_sc, vc_sc, dc_sc, *part_scs):
        c = pl.program_id(0)

        @pl.when(c == 0)
        def _():
            g_sc[...] = jnp.zeros_like(g_sc)

        ones = _seg_ones()
        diag = _diag_mask()
        has_prev = jnp.where(c == nc - 1, 0.0, 1.0)
        seqs = range(bsz)
        expand = _head_expand()
        _col_bcast_chunk(vt_ref, vc_sc, expand, seqs)
        _col_bcast_chunk(dyt_ref, dc_sc, expand, seqs)
        dr_sc, dw_sc, dk_sc, dv_sc, dn_sc, db_sc = part_scs

        def step(i, s_p):
            row = lambda ref, bi: ref[bi, pl.ds(i, 1), :] if not isinstance(i, int) else ref[bi, i:i + 1, :]
            per_seq = lambda a: [a[RW_HEAD * bi:RW_HEAD * (bi + 1)] for bi in seqs]
            dc, vc = per_seq(dc_sc[i]), per_seq(vc_sc[i])
            g = [g_sc[bi] + dc[bi] * row(r_ref, bi) for bi in seqs]
            res = _seg_multi([s_p[bi] * row(n_ref, bi) for bi in seqs] + [g[bi] * row(b_ref, bi) for bi in seqs]
                             + [g[bi] * row(k_ref, bi) for bi in seqs], ones, 1)
            sa, dsa, dvb = res[:bsz], res[bsz:2 * bsz], res[2 * bsz:]
            for bi in seqs:
                _put8(dr_sc, bi, i, _fold8(st_ref[bi, i] * dc[bi]))
                _put8(dv_sc, bi, i, _fold8(dvb[bi] * diag))
                _put8(dw_sc, bi, i, _fold8(g[bi] * s_p[bi]))
                _put8(db_sc, bi, i, _fold8(g[bi] * sa[bi]))
                _put8(dk_sc, bi, i, _fold8(g[bi] * vc[bi]))
                _put8(dn_sc, bi, i, _fold8(s_p[bi] * dsa[bi]))
                g_sc[bi] = g[bi] * row(w_ref, bi) + dsa[bi] * row(n_ref, bi)

        def loop_step(ii, _):
            i = tc - 1 - ii
            step(i, [st_ref[bi, i - 1] for bi in seqs])
            return 0

        lax.fori_loop(0, tc - 1, loop_step, 0)
        step(0, [halo_ref[bi, 0] * has_prev for bi in seqs])
        for out_ref, sc in zip((dr_ref, dw_ref, dk_ref, dv_ref, dn_ref, db_ref), part_scs):
            for bi in seqs:
                out_ref[bi] = _unfold8(sc, bi, tc)

    vec = pl.BlockSpec((bsz, tc, RW_WIDTH), lambda c: (0, nc - 1 - c, 0))
    tmin = pl.BlockSpec((bsz, 1, RW_HEAD, LANES), lambda c: (0, nc - 1 - c, 0, 0))
    bcast = pltpu.VMEM((tc, bsz * RW_HEAD, RW_WIDTH), F32)
    parts = pltpu.VMEM((bsz * RW_WIDTH // LANES, tc * 8, LANES), F32)
    return pl.pallas_call(
        body, name="scan_bwd", grid=(nc,),
        in_specs=[vec] * 5 + [tmin, tmin, pl.BlockSpec((bsz, tc, RW_HEAD, RW_WIDTH), lambda c: (0, nc - 1 - c, 0, 0)),
                              pl.BlockSpec((bsz, 1, RW_HEAD, RW_WIDTH), lambda c: (0, jnp.maximum((nc - 1 - c) * tc - 1, 0), 0, 0))],
        out_specs=[vec] * 6,
        out_shape=[_sds((bsz, t, RW_WIDTH))] * 6,
        scratch_shapes=[pltpu.VMEM((bsz, RW_HEAD, RW_WIDTH), F32), bcast, bcast] + [parts] * 6,
        compiler_params=_ARB1,
    )(r, w, k, nkk, b, vt, dyt, st, st)


TOKEN_TILE = 256


def _padded_weights(wt):
    f = lambda a: a.astype(F32)
    w_in = f(wt["w_in"][0])
    zeros = lambda r, c: jnp.zeros((r, c), F32)
    wp = jnp.concatenate([w_in[:, :MLA_COLS], zeros(D_MODEL, PM_W - MLA_COLS), w_in[:, MLA_COLS:]], axis=1)
    w_uq = f(wt["mla_w_uq"][0]).reshape(Q_LORA, MLA_HEADS, MLA_NOPE + MLA_ROPE)
    wqn = w_uq[:, :, :MLA_NOPE].reshape(Q_LORA, MLA_HEADS * MLA_NOPE)
    wqr = jnp.concatenate([w_uq[:, :, MLA_NOPE:], jnp.zeros((Q_LORA, MLA_HEADS, LANES - MLA_ROPE), F32)], axis=2)
    wqr = wqr.reshape(Q_LORA, MLA_HEADS * LANES)
    w2p = jnp.concatenate([f(wt["rw_w2"][0]), zeros(LORA, RW_WIDTH)], axis=0)
    a2p = jnp.concatenate([zeros(LORA, RW_WIDTH), f(wt["rw_a2"][0])], axis=0)
    bw = (f(wt["mla_q_norm_g"]), wqn, wqr, f(wt["mla_kv_norm_g"]), f(wt["mla_w_ukv"][0]), f(wt["rw_mu"]), f(wt["rw_w0"]),
          w2p, f(wt["rw_a0"]), a2p, f(wt["rw_k_k"]), f(wt["rw_k_a"]))
    hw = (f(wt["rw_ln_g"]), f(wt["rw_ln_b"]), f(wt["rw_r_k"]).reshape(1, RW_WIDTH), f(wt["w_out"][0]), f(wt["norm_post_g"]))
    return wp, bw, hw


def _local_step(x, positions, target, wt):
    bsz, t, _ = x.shape
    n = bsz * t
    tm = min(TOKEN_TILE, t)
    tps = t // tm
    wp, bw, hw = _padded_weights(wt)
    wpb = wp.astype(BF16)
    g_pre = wt["norm_pre_g"].astype(F32)
    x2 = x.reshape(n, D_MODEL)
    tgt2 = target.reshape(n, D_MODEL)
    inv_freq = ROPE_THETA ** (-jnp.arange(0, MLA_ROPE, 2, dtype=F32) / MLA_ROPE)
    invf = jnp.tile(inv_freq, LANES // (MLA_ROPE // 2)).reshape(1, LANES)
    cos, sin = _rope_tables(positions.astype(F32).reshape(n, 1), invf, tm)

    u, pm, prw, z = _fwd_a(x2, g_pre, wpb, tm)
    qn, qr, kv, kr, r, w, k, v, nkk, b = _fwd_b(pm, prw, cos, sin, bw, tm, tps)
    b3 = lambda a: a.reshape(bsz, t, a.shape[-1])
    ym, lse = _attn_fwd(b3(qn), b3(qr), b3(kv), b3(kr))
    vt = _time_minor(b3(v))
    ys, st = _scan_fwd(b3(r), b3(w), b3(k), vt, b3(nkk), b3(b))
    (dys, dr_h, dk_h, dv_h, dym, dz, dxres, loss, d_lng, d_lnb, d_rk, d_wout, d_gpost) = _head(
        ys.reshape(n, RW_WIDTH), r, k, v, ym.reshape(n, MLA_WIDTH), z, x2, tgt2, hw, tm)
    dqn, dqr, dkv, dkr_heads = _attn_bwd(b3(qn), b3(qr), b3(kv), b3(kr), ym, b3(dym), lse)
    dr_s, dw_s, dk_s, dv_s, dn_s, db_s = _scan_bwd(b3(r), b3(w), b3(k), vt, b3(nkk), b3(b), st, _time_minor(b3(dys)))
    f2 = lambda a: a.reshape(n, a.shape[-1])
    cts = (f2(dqn), f2(dqr), f2(dkv), f2(dr_s), dr_h, f2(dw_s), f2(dk_s), dk_h, f2(dv_s), dv_h, f2(dn_s), f2(db_s))
    (dpm, dprw, dps, d_gq, d_wqn, d_wqr, d_gkv, d_wkv, d_mu, d_w0, d_w2p, d_a0, d_a2p, d_kk, d_ka) = _bwd_b(
        pm, prw, cos, sin, bw, cts, dkr_heads.reshape(MLA_HEADS, n, LANES), tm, tps)
    grad_x, dpb, d_gpre = _bwd_a(x2, g_pre, wpb, dpm, dprw, dps, dz, dxres, tm, tps)
    d_wp = _dw_in(u, dpb, min(1024, n), 640)

    d_w_in = jnp.concatenate([d_wp[:, :MLA_COLS], d_wp[:, PM_W:]], axis=1)
    d_w_uq = jnp.concatenate([d_wqn.reshape(Q_LORA, MLA_HEADS, MLA_NOPE),
                              d_wqr.reshape(Q_LORA, MLA_HEADS, LANES)[:, :, :MLA_ROPE]], axis=2)
    grads = {
        "norm_pre_g": d_gpre, "w_in": d_w_in[None], "mla_q_norm_g": d_gq,
        "mla_w_uq": d_w_uq.reshape(1, Q_LORA, MLA_HEADS * (MLA_NOPE + MLA_ROPE)), "mla_kv_norm_g": d_gkv,
        "mla_w_ukv": d_wkv[None], "rw_mu": d_mu, "rw_w0": d_w0, "rw_w2": d_w2p[None, :LORA], "rw_a0": d_a0,
        "rw_a2": d_a2p[None, LORA:], "rw_k_k": d_kk, "rw_k_a": d_ka, "rw_r_k": d_rk.reshape(1, RW_HEADS, RW_HEAD),
        "rw_ln_g": d_lng, "rw_ln_b": d_lnb, "w_out": d_wout[None], "norm_post_g": d_gpost,
    }
    return loss, grad_x.reshape(bsz, t, D_MODEL), grads


_MESH = pl.DeviceIdType.MESH


def _gather_shards(packed):
    rows, lanes = packed.shape

    def body(x_ref, out_ref, send_sems, recv_sems, local_sem):
        x, y, c = lax.axis_index("x"), lax.axis_index("y"), lax.axis_index("c")
        me, sibling = (x, y, c), (x, y, 1 - c)
        chips = [(1 - x, y), (x, 1 - y), (1 - x, 1 - y)]

        def slot(px, py, pc):
            return out_ref.at[4 * px + 2 * py + pc]

        def copy(k, block, to, src=None):
            return pltpu.make_async_remote_copy(
                src_ref=slot(*block) if src is None else src, dst_ref=slot(*block),
                send_sem=send_sems.at[k], recv_sem=recv_sems.at[k], device_id=to, device_id_type=_MESH)

        mine = pltpu.make_async_copy(x_ref, slot(*me), local_sem)
        mine.start()
        first = [copy(0, me, sibling, src=x_ref)]
        first += [copy(1 + j, me, (*chip, c), src=x_ref) for j, chip in enumerate(chips)]
        for cp in first:
            cp.start()
        passed = [copy(4 + j, (*chip, c), sibling) for j, chip in enumerate(chips)]
        for j, chip in enumerate(chips):
            copy(1 + j, (*chip, c), me).wait_recv()
            passed[j].start()
        copy(0, sibling, me).wait_recv()
        for j, chip in enumerate(chips):
            copy(4 + j, (*chip, 1 - c), me).wait_recv()
        for cp in first + passed:
            cp.wait_send()
        mine.wait()

    return pl.pallas_call(
        body, name="gather_shards",
        out_shape=_sds((N_DEV, rows, lanes), packed.dtype),
        in_specs=[pl.BlockSpec(memory_space=pltpu.VMEM)],
        out_specs=pl.BlockSpec(memory_space=pltpu.VMEM),
        scratch_shapes=[pltpu.SemaphoreType.DMA((7,)), pltpu.SemaphoreType.DMA((7,)), pltpu.SemaphoreType.DMA],
    )(packed)


def _exchange_grads(gbig, gsmall):
    _, rows, lanes = gbig.shape

    def body(big_ref, small_ref, rbig_ref, rsmall_ref, send_b, recv_b, send_s, recv_s, local_sem):
        x, y, c = lax.axis_index("x"), lax.axis_index("y"), lax.axis_index("c")
        me_lin = 4 * x + 2 * y + c
        mine = pltpu.make_async_copy(big_ref.at[me_lin], rbig_ref.at[0], local_sem)
        mine.start()
        rsmall_ref[me_lin] = small_ref[...]
        copies = []
        for k in range(1, N_DEV):
            px, py, pc = x ^ (k >> 2), y ^ ((k >> 1) & 1), c ^ (k & 1)
            peer = (px, py, pc)
            copies.append(pltpu.make_async_remote_copy(
                src_ref=big_ref.at[4 * px + 2 * py + pc], dst_ref=rbig_ref.at[k],
                send_sem=send_b.at[k - 1], recv_sem=recv_b.at[k - 1], device_id=peer, device_id_type=_MESH))
            copies.append(pltpu.make_async_remote_copy(
                src_ref=small_ref, dst_ref=rsmall_ref.at[me_lin],
                send_sem=send_s.at[k - 1], recv_sem=recv_s.at[k - 1], device_id=peer, device_id_type=_MESH))
        for cp in copies:
            cp.start()
        for cp in copies:
            cp.wait_recv()
        for cp in copies:
            cp.wait_send()
        mine.wait()

    return pl.pallas_call(
        body, name="exchange_grads",
        out_shape=[_sds((N_DEV, rows, lanes)), _sds((N_DEV, SMALL_ROWS, lanes))],
        in_specs=[pl.BlockSpec(memory_space=pl.ANY), pl.BlockSpec(memory_space=pltpu.VMEM)],
        out_specs=[pl.BlockSpec(memory_space=pl.ANY), pl.BlockSpec(memory_space=pltpu.VMEM)],
        scratch_shapes=[pltpu.SemaphoreType.DMA((7,)), pltpu.SemaphoreType.DMA((7,)), pltpu.SemaphoreType.DMA((7,)),
                        pltpu.SemaphoreType.DMA((7,)), pltpu.SemaphoreType.DMA],
    )(gbig, gsmall)


def _adamw_math(w, g, m, v):
    m = ADAM_B1 * m + (1.0 - ADAM_B1) * g
    v = ADAM_B2 * v + (1.0 - ADAM_B2) * (g * g)
    m_hat = m / (1.0 - ADAM_B1 ** ADAM_STEP)
    v_hat = v / (1.0 - ADAM_B2 ** ADAM_STEP)
    return -ADAM_LR * (m_hat / (jnp.sqrt(v_hat) + ADAM_EPS) + ADAM_WD * w), m, v


def _reduce_adamw(parts, w, m, v, rb):
    _, rows, lanes = parts.shape

    def body(p_ref, w_ref, m_ref, v_ref, g_out, d_out, m_out, v_out):
        g = p_ref[0]
        for s in range(1, N_DEV):
            g = g + p_ref[s]
        g_out[...] = g
        d_out[...], m_out[...], v_out[...] = _adamw_math(w_ref[...], g, m_ref[...], v_ref[...])

    blk = pl.BlockSpec((rb, lanes), lambda i: (i, 0))
    return pl.pallas_call(
        body, name=f"reduce_adamw_{rows}", grid=(rows // rb,),
        in_specs=[pl.BlockSpec((N_DEV, rb, lanes), lambda i: (0, i, 0)), blk, blk, blk],
        out_specs=[blk] * 4, out_shape=[_sds((rows, lanes))] * 4,
        compiler_params=_ARB1,
    )(parts, w, m, v)


def _pack_rows(arrs):
    return jnp.concatenate([a.reshape(-1, LANES) for a in arrs], axis=0)


def _shard_blocks(name, full):
    a = full[0]
    if name == "w_out":
        return a.reshape(N_DEV, -1, LANES)
    rows, cols = a.shape
    return a.reshape(rows, N_DEV, cols // N_DEV).transpose(1, 0, 2).reshape(N_DEV, -1, LANES)


def _unshard(name, blocks, shard_shape):
    _, rows, cols = shard_shape
    a = blocks.reshape(N_DEV, rows, cols)
    if name == "w_out":
        return a.reshape(1, N_DEV * rows, cols)
    return a.transpose(1, 0, 2).reshape(1, rows, N_DEV * cols)


def _small_pack(vals, tail):
    flat = jnp.concatenate([vals[nm].reshape(-1) for nm, _ in SMALL] + [tail])
    return jnp.pad(flat, (0, SMALL_ROWS * LANES - flat.shape[0])).reshape(SMALL_ROWS, LANES)


def _small_unpack(packed, like):
    flat, out, off = packed.reshape(-1), {}, 0
    for nm, cnt in SMALL:
        out[nm] = flat[off:off + cnt].reshape(like[nm].shape)
        off += cnt
    return out


def _big_unpack(packed, like):
    out, off = {}, 0
    for nm, cnt in SHARD_ROWS:
        out[nm] = packed[off:off + cnt].reshape(like[nm].shape)
        off += cnt
    return out


def kernel(x, positions, norm_pre_g, w_in, mla_q_norm_g, mla_w_uq, mla_kv_norm_g, mla_w_ukv, rw_mu, rw_w0, rw_w2, rw_a0, rw_a2, rw_k_k, rw_k_a, rw_r_k, rw_ln_g, rw_ln_b, w_out, norm_post_g, loss_target, m_norm_pre_g, m_w_in, m_mla_q_norm_g, m_mla_w_uq, m_mla_kv_norm_g, m_mla_w_ukv, m_rw_mu, m_rw_w0, m_rw_w2, m_rw_a0, m_rw_a2, m_rw_k_k, m_rw_k_a, m_rw_r_k, m_rw_ln_g, m_rw_ln_b, m_w_out, m_norm_post_g, v_norm_pre_g, v_w_in, v_mla_q_norm_g, v_mla_w_uq, v_mla_kv_norm_g, v_mla_w_ukv, v_rw_mu, v_rw_w0, v_rw_w2, v_rw_a0, v_rw_a2, v_rw_k_k, v_rw_k_a, v_rw_r_k, v_rw_ln_g, v_rw_ln_b, v_w_out, v_norm_post_g):
    given = dict(locals())
    w = {nm: given[nm] for nm in WEIGHTS}
    mom = {nm: given["m_" + nm] for nm in WEIGHTS}
    var = {nm: given["v_" + nm] for nm in WEIGHTS}
    sharded = [nm for nm, _ in SHARD_ROWS]

    gathered = _gather_shards(_pack_rows([w[nm] for nm in sharded]).astype(BF16))
    full, off = dict(w), 0
    for nm, cnt in SHARD_ROWS:
        full[nm] = _unshard(nm, gathered[:, off:off + cnt], w[nm].shape)
        off += cnt

    loss_part, grad_x, grads = _local_step(x, positions, loss_target, full)

    gbig = jnp.concatenate([_shard_blocks(nm, grads[nm]) for nm in sharded], axis=1)
    gsmall = _small_pack(grads, loss_part[0, 0:1])
    parts_big, parts_small = _exchange_grads(gbig, gsmall)

    g_b, d_b, m_b, v_b = _reduce_adamw(parts_big, _pack_rows([w[nm] for nm in sharded]), _pack_rows([mom[nm] for nm in sharded]),
                                       _pack_rows([var[nm] for nm in sharded]), PACK_ROWS // 8)
    zero1 = jnp.zeros((1,), F32)
    g_s, d_s, m_s, v_s = _reduce_adamw(parts_small, _small_pack(w, zero1), _small_pack(mom, zero1), _small_pack(var, zero1),
                                       SMALL_ROWS)
    loss = g_s.reshape(-1)[SMALL_N]
    outs = []
    for big, small in ((g_b, g_s), (d_b, d_s), (m_b, m_s), (v_b, v_s)):
        vals = {**_big_unpack(big, w), **_small_unpack(small, w)}
        outs += [vals[nm] for nm in WEIGHTS]
    return (loss, grad_x, *outs)
```

```python
import functools

import jax
import jax.numpy as jnp
from jax import lax
from jax.experimental import pallas as pl
from jax.experimental.pallas import tpu as pltpu

F32 = jnp.float32
BF16 = jnp.bfloat16

D_MODEL = 1024
MLA_HEADS = 4
MLA_NOPE = 128
MLA_ROPE = 64
MLA_V = 128
MLA_WIDTH = MLA_HEADS * MLA_V
Q_LORA = 256
KV_LORA = 128
ROPE_THETA = 10000.0
RW_HEAD = 64
RW_WIDTH = 512
RW_HEADS = RW_WIDTH // RW_HEAD
LORA = 64
RW_COLS = 3 * RW_WIDTH + 2 * LORA
MLA_COLS = Q_LORA + KV_LORA + MLA_ROPE
D_IN = MLA_COLS + RW_COLS + D_MODEL
RW_GN_EPS = 64e-5
NORM_EPS = 1e-6
ATT_SCALE = (MLA_NOPE + MLA_ROPE) ** -0.5
ADAM_LR, ADAM_B1, ADAM_B2, ADAM_EPS, ADAM_WD, ADAM_STEP = 0.001, 0.9, 0.999, 1e-08, 0.01, 10
N_DEV = 8
LANES = 128
MXU = 256

PM_W = 512
WP_COLS = PM_W + RW_COLS + D_MODEL
RW_PIECES = ((0, 512), (512, 1024), (1024, 1536), (1536, 1664))

SHARD_ROWS = (("w_in", 1024 * 392 // LANES), ("mla_w_uq", 256 * 96 // LANES), ("mla_w_ukv", 128 * 128 // LANES),
              ("rw_w2", 64 * 64 // LANES), ("rw_a2", 64 * 64 // LANES), ("w_out", 128 * 1024 // LANES))
PACK_ROWS = sum(r for _, r in SHARD_ROWS)
SMALL = (("norm_pre_g", 1024), ("mla_q_norm_g", 256), ("mla_kv_norm_g", 128), ("rw_mu", 1664), ("rw_w0", 512),
         ("rw_a0", 512), ("rw_k_k", 512), ("rw_k_a", 512), ("rw_r_k", 512), ("rw_ln_g", 512), ("rw_ln_b", 512),
         ("norm_post_g", 1024))
SMALL_N = sum(n for _, n in SMALL)
SMALL_ROWS = 72
WEIGHTS = ("norm_pre_g", "w_in", "mla_q_norm_g", "mla_w_uq", "mla_kv_norm_g", "mla_w_ukv", "rw_mu", "rw_w0", "rw_w2",
           "rw_a0", "rw_a2", "rw_k_k", "rw_k_a", "rw_r_k", "rw_ln_g", "rw_ln_b", "w_out", "norm_post_g")


def _seg_ones():
    r = lax.broadcasted_iota(jnp.int32, (MXU, MXU), 0) >> 6
    c = lax.broadcasted_iota(jnp.int32, (MXU, MXU), 1) >> 6
    return jnp.where(r == c, 1.0, 0.0).astype(BF16)


def _seg_dot(x, ones, passes):
    parts, rem = [], x
    for p in range(passes):
        hb = rem.astype(BF16)
        parts.append(hb)
        if p + 1 < passes:
            rem = rem - hb.astype(F32)
    outs = []
    for j in range(x.shape[1] // MXU):
        acc = None
        for hb in parts:
            d = jnp.dot(hb[:, MXU * j:MXU * (j + 1)], ones, preferred_element_type=F32)
            acc = d if acc is None else acc + d
        outs.append(acc)
    return outs[0] if len(outs) == 1 else jnp.concatenate(outs, axis=1)


def _seg_multi(xs, ones, passes):
    his = [x.astype(BF16) for x in xs]
    hi = jnp.concatenate(his, axis=0)
    if passes == 2:
        lo = jnp.concatenate([(x - h.astype(F32)).astype(BF16) for x, h in zip(xs, his)], axis=0)
        rhs = jnp.concatenate([ones, ones], axis=0)
    halves = []
    for j in range(hi.shape[1] // MXU):
        sl = slice(MXU * j, MXU * (j + 1))
        if passes == 2:
            halves.append(jnp.dot(jnp.concatenate([hi[:, sl], lo[:, sl]], axis=1), rhs, preferred_element_type=F32))
        else:
            halves.append(jnp.dot(hi[:, sl], ones, preferred_element_type=F32))
    full = jnp.concatenate(halves, axis=1)
    m = xs[0].shape[0]
    return [full[m * i:m * (i + 1)] for i in range(len(xs))]


@jax.custom_vjp
def _segsum(x):
    return _seg_dot(x, _seg_ones(), 3)


_segsum.defvjp(lambda x: (_segsum(x), None), lambda _, g: (_segsum(g),))


@jax.custom_vjp
def _bdot(a, w):
    return jnp.dot(a.astype(BF16), w.astype(BF16), preferred_element_type=F32)


def _bdot_fwd(a, w):
    return _bdot(a, w), (a, w)


def _bdot_bwd(res, g):
    a, w = res
    gb = g.astype(BF16)
    da = lax.dot_general(gb, w.astype(BF16), (((1,), (1,)), ((), ())), preferred_element_type=F32)
    dw = lax.dot_general(a.astype(BF16), gb, (((0,), (0,)), ((), ())), preferred_element_type=F32)
    return da, dw


_bdot.defvjp(_bdot_fwd, _bdot_bwd)


def _rot_impl(x):
    w = x.shape[1]
    lane = lax.broadcasted_iota(jnp.int32, x.shape, 1)
    return jnp.where((lane & 63) < 32, -pltpu.roll(x, w - 32, 1), pltpu.roll(x, 32, 1))


@jax.custom_vjp
def _rot(x):
    return _rot_impl(x)


_rot.defvjp(lambda x: (_rot_impl(x), None), lambda _, g: (-_rot_impl(g),))


def _rms(x, g):
    return x * lax.rsqrt(jnp.mean(x * x, axis=-1, keepdims=True) + NORM_EPS) * g


def _shift_rows(p, prev_row):
    row = lax.broadcasted_iota(jnp.int32, p.shape, 0)
    return jnp.where(row == 0, prev_row, pltpu.roll(p, 1, 0))


def _unshift_rows(g, next_row):
    row = lax.broadcasted_iota(jnp.int32, g.shape, 0)
    return jnp.where(row == g.shape[0] - 1, next_row, pltpu.roll(g, g.shape[0] - 1, 0))


def _f_mla(cq, ckv, kr, cos, sin, g_q, wqn, wqr, g_kv, wkv):
    qn = _rms(cq, g_q)
    q_nope = _bdot(qn, wqn)
    q_r = _bdot(qn, wqr)
    cos4 = jnp.concatenate([cos] * MLA_HEADS, axis=1)
    sin4 = jnp.concatenate([sin] * MLA_HEADS, axis=1)
    q_rope = q_r * cos4 + _rot(q_r) * sin4
    kv = _bdot(_rms(ckv, g_kv), wkv)
    k_rope = kr * cos + _rot(kr) * sin
    return q_nope, q_rope, kv, k_rope


def _f_rw(pr, pk, pv, pt, sr, sk, sv, st, mu_r, mu_k, mu_v, mu_t, w0, w2p, a0, a2p, k_k, k_a):
    r = pr + (sr - pr) * mu_r
    k = pk + (sk - pk) * mu_k
    v = pv + (sv - pv) * mu_v
    t = pt + (st - pt) * mu_t
    nwl = -(w0 + _bdot(jnp.tanh(t), w2p))
    softplus = jnp.maximum(nwl, 0.0) + jnp.log(1.0 + jnp.exp(-jnp.abs(nwl)))
    decay = jnp.exp(-jnp.exp(-softplus - 0.5))
    a = jax.nn.sigmoid(a0 + _bdot(t, a2p))
    kk = k * k_k
    kk = kk / jnp.maximum(jnp.sqrt(_segsum(kk * kk)), 1e-12)
    k2 = k * (1.0 + (a - 1.0) * k_a)
    return r, decay, k2, v, -kk, kk * a


def _f_head(ys, r, k, v, ym, z1, z2, x, tgt, ln_g, ln_b, r_k, w1, w2, g_post):
    inv = 1.0 / RW_HEAD
    yc = ys - _segsum(ys) * inv
    var = _segsum(yc * yc) * inv
    y = yc * lax.rsqrt(var + RW_GN_EPS) * ln_g + ln_b
    y_rw = y + _segsum(r * k * r_k) * v
    c1 = ym * (z1 * jax.nn.sigmoid(z1))
    c2 = y_rw * (z2 * jax.nn.sigmoid(z2))
    out = _bdot(c1, w1) + _bdot(c2, w2)
    err = x + _rms(out, g_post) - tgt
    per_row = jnp.sum(err * err, axis=1, keepdims=True)
    return jnp.sum(per_row, axis=0, keepdims=True) * (0.5 / D_MODEL)


def _rows(tm, width):
    return pl.BlockSpec((tm, width), lambda i: (i, 0))


def _whole(shape):
    zeros = (0,) * len(shape)
    return pl.BlockSpec(shape, lambda i: zeros)


def _sds(shape, dtype=F32):
    return jax.ShapeDtypeStruct(shape, dtype)


_ARB1 = pltpu.CompilerParams(dimension_semantics=("arbitrary",))


def _acc(ref, val, first):
    @pl.when(first)
    def _():
        ref[...] = val

    @pl.when(jnp.logical_not(first))
    def _():
        ref[...] += val


def _fwd_a(x2, g_pre, wp, tm):
    n = x2.shape[0]

    def body(x_ref, g_ref, w_ref, u_ref, pm_ref, prw_ref, z_ref):
        ub = _rms(x_ref[...], g_ref[...]).astype(BF16)
        u_ref[...] = ub
        pm_ref[...] = jnp.dot(ub, w_ref[:, 0:PM_W], preferred_element_type=F32)
        prw_ref[...] = jnp.dot(ub, w_ref[:, PM_W:PM_W + RW_COLS], preferred_element_type=F32)
        z_ref[...] = jnp.dot(ub, w_ref[:, PM_W + RW_COLS:WP_COLS], preferred_element_type=F32)

    return pl.pallas_call(
        body, name="fwd_a", grid=(n // tm,),
        in_specs=[_rows(tm, D_MODEL), _whole((1, D_MODEL)), _whole((D_MODEL, WP_COLS))],
        out_specs=[_rows(tm, D_MODEL), _rows(tm, PM_W), _rows(tm, RW_COLS), _rows(tm, D_MODEL)],
        out_shape=[_sds((n, D_MODEL), BF16), _sds((n, PM_W)), _sds((n, RW_COLS)), _sds((n, D_MODEL))],
        compiler_params=_ARB1,
    )(x2, g_pre, wp)


def _rope_tables(pos_row, invf_col, tm):
    n = pos_row.shape[1]

    def body(p_ref, f_ref, c_ref, s_ref):
        ang = f_ref[...] * p_ref[...].astype(F32)
        c_ref[...] = jnp.cos(ang).T
        s_ref[...] = jnp.sin(ang).T

    return pl.pallas_call(
        body, name="rope_tables", grid=(n // tm,),
        in_specs=[pl.BlockSpec((1, tm), lambda i: (0, i)), _whole((LANES, 1))],
        out_specs=[_rows(tm, LANES), _rows(tm, LANES)],
        out_shape=[_sds((n, LANES)), _sds((n, LANES))],
        compiler_params=_ARB1,
    )(pos_row, invf_col)


_B_WEIGHT_SHAPES = ((1, Q_LORA), (Q_LORA, 512), (Q_LORA, 512), (1, KV_LORA), (KV_LORA, 1024), (1, RW_COLS), (1, RW_WIDTH),
                    (LANES, RW_WIDTH), (1, RW_WIDTH), (LANES, RW_WIDTH), (1, RW_WIDTH), (1, RW_WIDTH))


def _halo_prev(tm):
    return pl.BlockSpec((8, RW_COLS), lambda i: (jnp.maximum(i * (tm // 8) - 1, 0), 0))


def _b_operands(pm_ref, prw_ref, halo_ref, wrefs, tile, tiles_per_seq):
    g_q, wqn, wqr, g_kv, wkv, mu, w0, w2p, a0, a2p, k_k, k_a = wrefs
    mla_in = (pm_ref[:, 0:Q_LORA], pm_ref[:, Q_LORA:Q_LORA + KV_LORA], pm_ref[:, Q_LORA + KV_LORA:PM_W])
    mla_w = (g_q[...], wqn[...], wqr[...], g_kv[...], wkv[...])
    keep = jnp.where(tile % tiles_per_seq == 0, 0.0, 1.0)
    prev = halo_ref[7:8, :] * keep
    ps = tuple(prw_ref[:, a:b] for a, b in RW_PIECES)
    ss = tuple(_shift_rows(p, prev[:, a:b]) for p, (a, b) in zip(ps, RW_PIECES))
    rw_w = tuple(mu[:, a:b] for a, b in RW_PIECES) + (w0[...], w2p[...], a0[...], a2p[...], k_k[...], k_a[...])
    return mla_in, mla_w, ps + ss, rw_w


def _fwd_b(pm, prw, cos, sin, bw, tm, tiles_per_seq):
    n = pm.shape[0]

    def body(pm_ref, prw_ref, halo_ref, cos_ref, sin_ref, *refs):
        wrefs, outs = refs[:12], refs[12:]
        mla_in, mla_w, rw_in, rw_w = _b_operands(pm_ref, prw_ref, halo_ref, wrefs, pl.program_id(0), tiles_per_seq)
        res = _f_mla(*mla_in, cos_ref[...], sin_ref[...], *mla_w) + _f_rw(*rw_in, *rw_w)
        for o_ref, val in zip(outs, res):
            o_ref[...] = val.astype(o_ref.dtype)

    widths = (512, 512, 1024, LANES) + (RW_WIDTH,) * 6
    return pl.pallas_call(
        body, name="fwd_b", grid=(n // tm,),
        in_specs=[_rows(tm, PM_W), _rows(tm, RW_COLS), _halo_prev(tm), _rows(tm, LANES), _rows(tm, LANES)]
        + [_whole(s) for s in _B_WEIGHT_SHAPES],
        out_specs=[_rows(tm, w) for w in widths],
        out_shape=[_sds((n, w), BF16 if j < 4 else F32) for j, w in enumerate(widths)],
        compiler_params=_ARB1,
    )(pm, prw, prw, cos, sin, *bw)


def _bwd_b(pm, prw, cos, sin, bw, cts, dkr_heads, tm, tiles_per_seq):
    n = pm.shape[0]

    ct_widths = (512, 512, 1024) + (RW_WIDTH,) * 9
    n_ct = len(ct_widths)

    def body(pm_ref, prw_ref, halo_ref, cos_ref, sin_ref, *refs):
        wrefs, ct_refs, dkr_ref = refs[:12], refs[12:12 + n_ct], refs[12 + n_ct]
        dpm_ref, dprw_ref, dps_ref = refs[13 + n_ct:16 + n_ct]
        wg_refs = refs[16 + n_ct:]
        tile = pl.program_id(0)
        first = tile == 0
        mla_in, mla_w, rw_in, rw_w = _b_operands(pm_ref, prw_ref, halo_ref, wrefs, tile, tiles_per_seq)
        cos, sin = cos_ref[...], sin_ref[...]
        ct = [r[...] for r in ct_refs]
        _, vjp_mla = jax.vjp(lambda *a: _f_mla(*a[:3], cos, sin, *a[3:]), *mla_in, *mla_w)
        dkr = dkr_ref[0] + dkr_ref[1] + dkr_ref[2] + dkr_ref[3]
        d_mla = vjp_mla((ct[0], ct[1], ct[2], dkr))
        dpm_ref[:, 0:Q_LORA] = d_mla[0]
        dpm_ref[:, Q_LORA:Q_LORA + KV_LORA] = d_mla[1]
        dpm_ref[:, Q_LORA + KV_LORA:PM_W] = d_mla[2]
        _, vjp_rw = jax.vjp(_f_rw, *rw_in, *rw_w)
        d_rw = vjp_rw((ct[3] + ct[4], ct[5], ct[6] + ct[7], ct[8] + ct[9], ct[10], ct[11]))
        for j, (a, b) in enumerate(RW_PIECES):
            dprw_ref[:, a:b] = d_rw[j]
            dps_ref[:, a:b] = d_rw[4 + j]
        g_q, wqn, wqr, g_kv, wkv, mu, w0, w2p, a0, a2p, k_k, k_a = wg_refs
        for ref, val in zip((g_q, wqn, wqr, g_kv, wkv), d_mla[3:]):
            _acc(ref, val, first)
        for j, (a, b) in enumerate(RW_PIECES):
            _acc(mu.at[:, a:b], d_rw[8 + j], first)
        for ref, val in zip((w0, w2p, a0, a2p, k_k, k_a), d_rw[12:]):
            _acc(ref, val, first)

    return pl.pallas_call(
        body, name="bwd_b", grid=(n // tm,),
        in_specs=[_rows(tm, PM_W), _rows(tm, RW_COLS), _halo_prev(tm), _rows(tm, LANES), _rows(tm, LANES)]
        + [_whole(s) for s in _B_WEIGHT_SHAPES] + [_rows(tm, w) for w in ct_widths]
        + [pl.BlockSpec((MLA_HEADS, tm, LANES), lambda i: (0, i, 0))],
        out_specs=[_rows(tm, PM_W), _rows(tm, RW_COLS), _rows(tm, RW_COLS)] + [_whole(s) for s in _B_WEIGHT_SHAPES],
        out_shape=[_sds((n, PM_W)), _sds((n, RW_COLS)), _sds((n, RW_COLS))] + [_sds(s) for s in _B_WEIGHT_SHAPES],
        compiler_params=_ARB1,
    )(pm, prw, prw, cos, sin, *bw, *cts, dkr_heads)


def _head(ys, r, k, v, ym, z, x2, tgt, hw, tm):
    n = x2.shape[0]
    h_shapes = ((1, RW_WIDTH), (1, RW_WIDTH), (1, RW_WIDTH), (D_MODEL, D_MODEL), (1, D_MODEL))

    def body(ys_ref, r_ref, k_ref, v_ref, ym_ref, z_ref, x_ref, t_ref, lng, lnb, rk, wout, gpost,
             dys_ref, dr_ref, dk_ref, dv_ref, dym_ref, dz_ref, dx_ref, loss_ref, dlng, dlnb, drk, dwout, dgpost):
        first = pl.program_id(0) == 0
        tgt_v = t_ref[...]
        args = (ys_ref[...], r_ref[...], k_ref[...], v_ref[...], ym_ref[...], z_ref[:, 0:MLA_WIDTH], z_ref[:, MLA_WIDTH:D_MODEL],
                x_ref[...], lng[...], lnb[...], rk[...], wout[0:MLA_WIDTH, :], wout[MLA_WIDTH:D_MODEL, :], gpost[...])
        loss, vjp = jax.vjp(lambda *a: _f_head(*a[:8], tgt_v, *a[8:]), *args)
        d = vjp(jnp.ones((1, 1), F32))
        dys_ref[...] = d[0]
        dr_ref[...] = d[1]
        dk_ref[...] = d[2]
        dv_ref[...] = d[3]
        dym_ref[...] = d[4].astype(BF16)
        dz_ref[:, 0:MLA_WIDTH] = d[5]
        dz_ref[:, MLA_WIDTH:D_MODEL] = d[6]
        dx_ref[...] = d[7]
        _acc(loss_ref, jnp.broadcast_to(loss, (8, LANES)), first)
        _acc(dlng, d[8], first)
        _acc(dlnb, d[9], first)
        _acc(drk, d[10], first)
        _acc(dwout.at[0:MLA_WIDTH, :], d[11], first)
        _acc(dwout.at[MLA_WIDTH:D_MODEL, :], d[12], first)
        _acc(dgpost, d[13], first)

    widths = (RW_WIDTH,) * 4 + (MLA_WIDTH, D_MODEL, D_MODEL)
    return pl.pallas_call(
        body, name="head", grid=(n // tm,),
        in_specs=[_rows(tm, RW_WIDTH)] * 4 + [_rows(tm, MLA_WIDTH), _rows(tm, D_MODEL), _rows(tm, D_MODEL), _rows(tm, D_MODEL)]
        + [_whole(s) for s in h_shapes],
        out_specs=[_rows(tm, w) for w in widths] + [_whole((8, LANES))] + [_whole(s) for s in h_shapes],
        out_shape=[_sds((n, w), BF16 if j == 4 else F32) for j, w in enumerate(widths)] + [_sds((8, LANES))]
        + [_sds(s) for s in h_shapes],
        compiler_params=_ARB1,
    )(ys, r, k, v, ym, z, x2, tgt, *hw)


def _halo_next(tm, n):
    last = n // 8 - 1
    return pl.BlockSpec((8, RW_COLS), lambda i: (jnp.minimum((i + 1) * (tm // 8), last), 0))


def _bwd_a(x2, g_pre, wp, dpm, dprw, dps, dz, dxres, tm, tiles_per_seq):
    n = x2.shape[0]
    nt_dims = (((1,), (1,)), ((), ()))

    def body(x_ref, g_ref, w_ref, dpm_ref, dprw_ref, dps_ref, nxt_ref, dz_ref, dxres_ref, gx_ref, dpb_ref, dg_ref):
        tile = pl.program_id(0)
        keep = jnp.where((tile + 1) % tiles_per_seq == 0, 0.0, 1.0)
        dprw_v = dprw_ref[...] + _unshift_rows(dps_ref[...], nxt_ref[0:1, :] * keep)
        dpm_b, dprw_b, dz_b = dpm_ref[...].astype(BF16), dprw_v.astype(BF16), dz_ref[...].astype(BF16)
        dpb_ref[:, 0:PM_W] = dpm_b
        dpb_ref[:, PM_W:PM_W + RW_COLS] = dprw_b
        dpb_ref[:, PM_W + RW_COLS:WP_COLS] = dz_b
        du = (lax.dot_general(dpm_b, w_ref[:, 0:PM_W], nt_dims, preferred_element_type=F32)
              + lax.dot_general(dprw_b, w_ref[:, PM_W:PM_W + RW_COLS], nt_dims, preferred_element_type=F32)
              + lax.dot_general(dz_b, w_ref[:, PM_W + RW_COLS:WP_COLS], nt_dims, preferred_element_type=F32))
        x = x_ref[...]
        xhat = x * lax.rsqrt(jnp.mean(x * x, axis=-1, keepdims=True) + NORM_EPS)
        dxn = du * g_ref[...]
        dx = (dxn - xhat * jnp.mean(dxn * xhat, axis=-1, keepdims=True)) * lax.rsqrt(jnp.mean(x * x, axis=-1, keepdims=True) + NORM_EPS)
        gx_ref[...] = dx + dxres_ref[...]
        _acc(dg_ref, jnp.sum(du * xhat, axis=0, keepdims=True), tile == 0)

    return pl.pallas_call(
        body, name="bwd_a", grid=(n // tm,),
        in_specs=[_rows(tm, D_MODEL), _whole((1, D_MODEL)), _whole((D_MODEL, WP_COLS)), _rows(tm, PM_W), _rows(tm, RW_COLS),
                  _rows(tm, RW_COLS), _halo_next(tm, n), _rows(tm, D_MODEL), _rows(tm, D_MODEL)],
        out_specs=[_rows(tm, D_MODEL), _rows(tm, WP_COLS), _whole((1, D_MODEL))],
        out_shape=[_sds((n, D_MODEL)), _sds((n, WP_COLS), BF16), _sds((1, D_MODEL))],
        compiler_params=_ARB1,
    )(x2, g_pre, wp, dpm, dprw, dps, dps, dz, dxres)


def _dw_in(u, dpb, tk, tn):
    n = u.shape[0]

    def body(u_ref, d_ref, o_ref):
        val = lax.dot_general(u_ref[...], d_ref[...], (((0,), (0,)), ((), ())), preferred_element_type=F32)
        _acc(o_ref, val, pl.program_id(1) == 0)

    return pl.pallas_call(
        body, name="dw_in", grid=(WP_COLS // tn, n // tk),
        in_specs=[pl.BlockSpec((tk, D_MODEL), lambda j, k: (k, 0)), pl.BlockSpec((tk, tn), lambda j, k: (k, j))],
        out_specs=pl.BlockSpec((D_MODEL, tn), lambda j, k: (0, j)),
        out_shape=_sds((D_MODEL, WP_COLS)),
        compiler_params=pltpu.CompilerParams(dimension_semantics=("arbitrary", "arbitrary")),
    )(u, dpb)


ATT_BLK = 256
_NT = (((1,), (1,)), ((), ()))
_TN = (((0,), (0,)), ((), ()))


def _causal(q0, k0, blk):
    row = q0 + lax.broadcasted_iota(jnp.int32, (blk, blk), 0)
    col = k0 + lax.broadcasted_iota(jnp.int32, (blk, blk), 1)
    return row >= col


def _attn_fwd(qn, qr, kv, kr):
    bsz, t, _ = qn.shape
    blk = min(ATT_BLK, t)

    heads = range(MLA_HEADS)

    def body(qn_ref, qr_ref, kv_ref, kr_ref, o_ref, lse_ref):
        qi = pl.program_id(1)
        q = [jnp.concatenate([qn_ref[:, LANES * h:LANES * (h + 1)], qr_ref[:, LANES * h:LANES * (h + 1)]], axis=1) for h in heads]
        lower = _causal(0, 0, blk)

        def kv_step(j, carry, diagonal):
            ks = pl.multiple_of(j * blk, blk)
            k_rope = kr_ref[pl.ds(ks, blk), :]
            out = []
            for h in heads:
                m, l, acc = carry[h]
                k = jnp.concatenate([kv_ref[pl.ds(ks, blk), 2 * LANES * h:2 * LANES * h + LANES], k_rope], axis=1)
                s = lax.dot_general(q[h], k, _NT, preferred_element_type=F32) * ATT_SCALE
                if diagonal:
                    s = jnp.where(lower, s, -1e30)
                m_new = jnp.maximum(m, jnp.max(s, axis=1, keepdims=True))
                alpha = jnp.exp(m - m_new)
                p = jnp.exp(s - m_new)
                l = alpha * l + jnp.sum(p, axis=1, keepdims=True)
                v = kv_ref[pl.ds(ks, blk), 2 * LANES * h + LANES:2 * LANES * (h + 1)]
                acc = alpha * acc + jnp.dot(p.astype(BF16), v, preferred_element_type=F32)
                out.append((m_new, l, acc))
            return tuple(out)

        one = (jnp.full((blk, 1), -1e30, F32), jnp.zeros((blk, 1), F32), jnp.zeros((blk, MLA_V), F32))
        carry = lax.fori_loop(0, qi, lambda j, c: kv_step(j, c, False), (one,) * MLA_HEADS)
        carry = kv_step(qi, carry, True)
        for h in heads:
            m, l, acc = carry[h]
            o_ref[:, LANES * h:LANES * (h + 1)] = acc / l
            lse_ref[h] = jnp.broadcast_to(m + jnp.log(l), (blk, LANES))

    return pl.pallas_call(
        body, name="attn_fwd", grid=(bsz, t // blk),
        in_specs=[pl.BlockSpec((None, blk, MLA_WIDTH), lambda b, i: (b, i, 0)),
                  pl.BlockSpec((None, blk, MLA_WIDTH), lambda b, i: (b, i, 0)),
                  pl.BlockSpec((None, t, 2 * MLA_WIDTH), lambda b, i: (b, 0, 0)),
                  pl.BlockSpec((None, t, LANES), lambda b, i: (b, 0, 0))],
        out_specs=[pl.BlockSpec((None, blk, MLA_WIDTH), lambda b, i: (b, i, 0)),
                   pl.BlockSpec((None, MLA_HEADS, blk, LANES), lambda b, i: (b, 0, i, 0))],
        out_shape=[_sds((bsz, t, MLA_WIDTH)), _sds((bsz, MLA_HEADS, t, LANES))],
        compiler_params=pltpu.CompilerParams(dimension_semantics=("arbitrary", "arbitrary")),
    )(qn, qr, kv, kr)


def _attn_bwd(qn, qr, kv, kr, o, do, lse):
    bsz, t, _ = qn.shape
    blk = min(ATT_BLK, t)
    nb = t // blk

    def body(qn_ref, qr_ref, kn_ref, kr_ref, v_ref, o_ref, do_ref, lse_ref, dqn_ref, dqr_ref, dkv_ref, dkr_ref, dq_sc, delta_sc):
        dq_sc[...] = jnp.zeros_like(dq_sc)
        delta_sc[...] = jnp.sum(do_ref[...].astype(F32) * o_ref[...], axis=1, keepdims=True)

        def kv_loop(j, _):
            ks = pl.multiple_of(j * blk, blk)
            k = jnp.concatenate([kn_ref[pl.ds(ks, blk), :], kr_ref[pl.ds(ks, blk), :]], axis=1)
            vb = v_ref[pl.ds(ks, blk), :]

            def q_pair(pair, carry):
                dk, dv = carry
                for sub in range(2):
                    i = 2 * pair + sub
                    qs = pl.multiple_of(i * blk, blk)
                    q = jnp.concatenate([qn_ref[pl.ds(qs, blk), :], qr_ref[pl.ds(qs, blk), :]], axis=1)
                    dob = do_ref[pl.ds(qs, blk), :]
                    s = lax.dot_general(q, k, _NT, preferred_element_type=F32) * ATT_SCALE
                    p = jnp.where(_causal(i * blk, j * blk, blk), jnp.exp(s - lse_ref[pl.ds(qs, blk), 0:1]), 0.0)
                    dv = dv + lax.dot_general(p.astype(BF16), dob, _TN, preferred_element_type=F32)
                    dp = lax.dot_general(dob, vb, _NT, preferred_element_type=F32)
                    ds = (p * (dp - delta_sc[pl.ds(qs, blk), :]) * ATT_SCALE).astype(BF16)
                    dq_sc[pl.ds(qs, blk), :] += jnp.dot(ds, k, preferred_element_type=F32)
                    dk = dk + lax.dot_general(ds, q, _TN, preferred_element_type=F32)
                return dk, dv

            dk, dv = lax.fori_loop(j // 2, nb // 2, q_pair, (jnp.zeros((blk, 2 * LANES), F32), jnp.zeros((blk, MLA_V), F32)))
            dkv_ref[pl.ds(ks, blk), 0:LANES] = dk[:, 0:LANES]
            dkv_ref[pl.ds(ks, blk), LANES:2 * LANES] = dv
            dkr_ref[pl.ds(ks, blk), :] = dk[:, LANES:2 * LANES]
            return 0

        lax.fori_loop(0, nb, kv_loop, 0)
        dqn_ref[...] = dq_sc[:, 0:LANES]
        dqr_ref[...] = dq_sc[:, LANES:2 * LANES]

    head_col = lambda b, h: (b, 0, h)
    return pl.pallas_call(
        body, name="attn_bwd", grid=(bsz, MLA_HEADS),
        in_specs=[pl.BlockSpec((None, t, LANES), head_col), pl.BlockSpec((None, t, LANES), head_col),
                  pl.BlockSpec((None, t, LANES), lambda b, h: (b, 0, 2 * h)),
                  pl.BlockSpec((None, t, LANES), lambda b, h: (b, 0, 0)),
                  pl.BlockSpec((None, t, LANES), lambda b, h: (b, 0, 2 * h + 1)),
                  pl.BlockSpec((None, t, LANES), head_col), pl.BlockSpec((None, t, LANES), head_col),
                  pl.BlockSpec((None, None, t, LANES), lambda b, h: (b, h, 0, 0))],
        out_specs=[pl.BlockSpec((None, t, LANES), head_col), pl.BlockSpec((None, t, LANES), head_col),
                   pl.BlockSpec((None, t, 2 * LANES), head_col),
                   pl.BlockSpec((None, None, t, LANES), lambda b, h: (h, b, 0, 0))],
        out_shape=[_sds((bsz, t, MLA_WIDTH)), _sds((bsz, t, MLA_WIDTH)), _sds((bsz, t, 2 * MLA_WIDTH)),
                   _sds((MLA_HEADS, bsz, t, LANES))],
        scratch_shapes=[pltpu.VMEM((t, 2 * LANES), F32), pltpu.VMEM((t, 1), F32)],
        compiler_params=pltpu.CompilerParams(dimension_semantics=("arbitrary", "arbitrary")),
    )(qn, qr, kv, kr, kv, o, do, lse)


SCAN_CHUNK = 16


def _diag_mask():
    row = lax.broadcasted_iota(jnp.int32, (RW_HEAD, RW_WIDTH), 0)
    lane = lax.broadcasted_iota(jnp.int32, (RW_HEAD, RW_WIDTH), 1)
    return jnp.where(row == (lane & (RW_HEAD - 1)), 1.0, 0.0)


def _time_minor(a):
    bsz, t, _ = a.shape
    a = a.reshape(bsz, t // SCAN_CHUNK, SCAN_CHUNK, RW_HEADS, RW_HEAD)
    return a.transpose(0, 1, 4, 3, 2).reshape(bsz, t // SCAN_CHUNK, RW_HEAD, RW_HEADS * SCAN_CHUNK)


def _head_expand():
    l = lax.broadcasted_iota(jnp.int32, (2 * LANES, RW_WIDTH), 0)
    n = lax.broadcasted_iota(jnp.int32, (2 * LANES, RW_WIDTH), 1)
    return jnp.where(((l & (LANES - 1)) >> 4) == (n >> 6), 1.0, 0.0).astype(BF16)


def _col_bcast_chunk(tm_ref, out_sc, expand, seqs):
    step_of_lane = lax.broadcasted_iota(jnp.int32, (RW_HEAD, LANES), 1) & (SCAN_CHUNK - 1)
    tiles = [tm_ref[bi, 0] for bi in seqs]
    for t in range(SCAN_CHUNK):
        parts = []
        for tile in tiles:
            a = jnp.where(step_of_lane == t, tile, 0.0)
            hi = a.astype(BF16)
            parts.append(jnp.concatenate([hi, (a - hi.astype(F32)).astype(BF16)], axis=1))
        out_sc[t] = jnp.dot(jnp.concatenate(parts, axis=0), expand, preferred_element_type=F32)


def _fold8(x):
    acc = x[0:8]
    for j in range(1, x.shape[0] // 8):
        acc = acc + x[8 * j:8 * (j + 1)]
    return acc


def _rows8(at):
    return pl.ds(at * 8 if isinstance(at, int) else pl.multiple_of(at * 8, 8), 8)


def _put8(sc, bi, at, val):
    for j in range(RW_WIDTH // LANES):
        sc[bi * (RW_WIDTH // LANES) + j, _rows8(at), :] = val[:, LANES * j:LANES * (j + 1)]


def _unfold8(sc, bi, steps):
    tiles = []
    for j in range(RW_WIDTH // LANES):
        view = sc.at[bi * (RW_WIDTH // LANES) + j]
        acc = view[pl.ds(0, steps, stride=8), :]
        for s in range(1, 8):
            acc = acc + view[pl.ds(s, steps, stride=8), :]
        tiles.append(acc)
    return jnp.concatenate(tiles, axis=1)


def _scan_fwd(r, w, k, vt, nkk, b):
    bsz, t, _ = r.shape
    tc = SCAN_CHUNK

    def body(r_ref, w_ref, k_ref, n_ref, b_ref, vt_ref, y_ref, st_ref, s_sc, vc_sc, y_sc):
        @pl.when(pl.program_id(0) == 0)
        def _():
            s_sc[...] = jnp.zeros_like(s_sc)

        ones = _seg_ones()
        diag = _diag_mask()
        seqs = range(bsz)
        _col_bcast_chunk(vt_ref, vc_sc, _head_expand(), seqs)

        def put_y(ya, at):
            for bi in seqs:
                _put8(y_sc, bi, at, _fold8(ya[bi] * diag))

        def step(i, _):
            row = lambda ref, bi: ref[bi, pl.ds(i, 1), :]
            prev = jnp.maximum(i - 1, 0)
            s_old = [s_sc[bi] for bi in seqs]
            half = bsz // 2
            sa = (_seg_multi([s_old[bi] * row(n_ref, bi) for bi in seqs[:half]], ones, 1)
                  + _seg_multi([s_old[bi] * row(n_ref, bi) for bi in seqs[half:]], ones, 1))
            put_y(_seg_multi([s_old[bi] * r_ref[bi, pl.ds(prev, 1), :] for bi in seqs], ones, 1), prev)
            vc = vc_sc[i]
            for bi in seqs:
                s_new = (s_old[bi] * row(w_ref, bi) + sa[bi] * row(b_ref, bi)
                         + vc[RW_HEAD * bi:RW_HEAD * (bi + 1)] * row(k_ref, bi))
                s_sc[bi] = s_new
                st_ref[bi, i] = s_new
            return 0

        lax.fori_loop(0, tc, step, 0, unroll=8)
        put_y(_seg_multi([s_sc[bi] * r_ref[bi, tc - 1:tc, :] for bi in seqs], ones, 1), tc - 1)
        for bi in seqs:
            y_ref[bi] = _unfold8(y_sc, bi, tc)

    vec = pl.BlockSpec((bsz, tc, RW_WIDTH), lambda c: (0, c, 0))
    return pl.pallas_call(
        body, name="scan_fwd", grid=(t // tc,),
        in_specs=[vec] * 5 + [pl.BlockSpec((bsz, 1, RW_HEAD, LANES), lambda c: (0, c, 0, 0))],
        out_specs=[vec, pl.BlockSpec((bsz, tc, RW_HEAD, RW_WIDTH), lambda c: (0, c, 0, 0))],
        out_shape=[_sds((bsz, t, RW_WIDTH)), _sds((bsz, t, RW_HEAD, RW_WIDTH))],
        scratch_shapes=[pltpu.VMEM((bsz, RW_HEAD, RW_WIDTH), F32), pltpu.VMEM((tc, bsz * RW_HEAD, RW_WIDTH), F32),
                        pltpu.VMEM((bsz * RW_WIDTH // LANES, tc * 8, LANES), F32)],
        compiler_params=_ARB1,
    )(r, w, k, nkk, b, vt)


def _own_head_rows(x):
    row = lax.broadcasted_iota(jnp.int32, x.shape, 0)
    lane = lax.broadcasted_iota(jnp.int32, x.shape, 1)
    return jnp.sum(jnp.where(row == (lane >> 6), x, 0.0), axis=0, keepdims=True)


def _scan_bwd(r, w, k, v8, nkk, b, st, dyt, dy8):
    bsz, t, _ = r.shape
    tc = SCAN_CHUNK
    nc = t // tc

    def body(r_ref, w_ref, k_ref, n_ref, b_ref, v8_ref, dy8_ref, dyt_ref, st_ref, halo_ref,
             dr_ref, dw_ref, dk_ref, dv_ref, dn_ref, db_ref, g_sc, dc_sc, *part_scs):
        c = pl.program_id(0)

        @pl.when(c == 0)
        def _():
            g_sc[...] = jnp.zeros_like(g_sc)

        ones = _seg_ones()
        diag = _diag_mask()
        has_prev = jnp.where(c == nc - 1, 0.0, 1.0)
        seqs = range(bsz)
        _col_bcast_chunk(dyt_ref, dc_sc, _head_expand(), seqs)
        dw_sc, dv_sc, dn_sc, db_sc = part_scs

        def step(i, s_p):
            static = isinstance(i, int)
            row = lambda ref, bi: ref[bi, i:i + 1, :] if static else ref[bi, pl.ds(i, 1), :]
            put_row = lambda ref, bi, val: ref.__setitem__((bi, slice(i, i + 1) if static else pl.ds(i, 1), slice(None)), val)
            dc_all = dc_sc[i]
            dc = [dc_all[RW_HEAD * bi:RW_HEAD * (bi + 1)] for bi in seqs]
            g = [g_sc[bi] + dc[bi] * row(r_ref, bi) for bi in seqs]
            res = _seg_multi([s_p[bi] * row(n_ref, bi) for bi in seqs] + [g[bi] * row(b_ref, bi) for bi in seqs]
                             + [g[bi] * row(k_ref, bi) for bi in seqs], ones, 1)
            sa, dsa, dvb = res[:bsz], res[bsz:2 * bsz], res[2 * bsz:]
            for bi in seqs:
                dr8 = jnp.dot(dy8_ref[bi, i].astype(BF16), st_ref[bi, i].astype(BF16), preferred_element_type=F32)
                dk8 = jnp.dot(v8_ref[bi, i].astype(BF16), g[bi].astype(BF16), preferred_element_type=F32)
                put_row(dr_ref, bi, _own_head_rows(dr8))
                put_row(dk_ref, bi, _own_head_rows(dk8))
                _put8(dv_sc, bi, i, _fold8(dvb[bi] * diag))
                _put8(dw_sc, bi, i, _fold8(g[bi] * s_p[bi]))
                _put8(db_sc, bi, i, _fold8(g[bi] * sa[bi]))
                _put8(dn_sc, bi, i, _fold8(s_p[bi] * dsa[bi]))
                g_sc[bi] = g[bi] * row(w_ref, bi) + dsa[bi] * row(n_ref, bi)

        def loop_step(ii, _):
            i = tc - 1 - ii
            step(i, [st_ref[bi, i - 1] for bi in seqs])
            return 0

        lax.fori_loop(0, tc - 1, loop_step, 0, unroll=5)
        step(0, [halo_ref[bi, 0] * has_prev for bi in seqs])
        for out_ref, sc in zip((dw_ref, dv_ref, dn_ref, db_ref), part_scs):
            for bi in seqs:
                out_ref[bi] = _unfold8(sc, bi, tc)

    vec = pl.BlockSpec((bsz, tc, RW_WIDTH), lambda c: (0, nc - 1 - c, 0))
    by_head = pl.BlockSpec((bsz, tc, RW_HEADS, RW_HEAD), lambda c: (0, nc - 1 - c, 0, 0))
    tmin = pl.BlockSpec((bsz, 1, RW_HEAD, LANES), lambda c: (0, nc - 1 - c, 0, 0))
    parts = pltpu.VMEM((bsz * RW_WIDTH // LANES, tc * 8, LANES), F32)
    return pl.pallas_call(
        body, name="scan_bwd", grid=(nc,),
        in_specs=[vec] * 5 + [by_head, by_head, tmin,
                              pl.BlockSpec((bsz, tc, RW_HEAD, RW_WIDTH), lambda c: (0, nc - 1 - c, 0, 0)),
                              pl.BlockSpec((bsz, 1, RW_HEAD, RW_WIDTH), lambda c: (0, jnp.maximum((nc - 1 - c) * tc - 1, 0), 0, 0))],
        out_specs=[vec] * 6,
        out_shape=[_sds((bsz, t, RW_WIDTH))] * 6,
        scratch_shapes=[pltpu.VMEM((bsz, RW_HEAD, RW_WIDTH), F32), pltpu.VMEM((tc, bsz * RW_HEAD, RW_WIDTH), F32)] + [parts] * 4,
        compiler_params=_ARB1,
    )(r, w, k, nkk, b, v8, dy8, dyt, st, st)


TOKEN_TILE = 256


def _padded_weights(wt):
    f = lambda a: a.astype(F32)
    w_in = f(wt["w_in"][0])
    zeros = lambda r, c: jnp.zeros((r, c), F32)
    wp = jnp.concatenate([w_in[:, :MLA_COLS], zeros(D_MODEL, PM_W - MLA_COLS), w_in[:, MLA_COLS:]], axis=1)
    w_uq = f(wt["mla_w_uq"][0]).reshape(Q_LORA, MLA_HEADS, MLA_NOPE + MLA_ROPE)
    wqn = w_uq[:, :, :MLA_NOPE].reshape(Q_LORA, MLA_HEADS * MLA_NOPE)
    wqr = jnp.concatenate([w_uq[:, :, MLA_NOPE:], jnp.zeros((Q_LORA, MLA_HEADS, LANES - MLA_ROPE), F32)], axis=2)
    wqr = wqr.reshape(Q_LORA, MLA_HEADS * LANES)
    w2p = jnp.concatenate([f(wt["rw_w2"][0]), zeros(LORA, RW_WIDTH)], axis=0)
    a2p = jnp.concatenate([zeros(LORA, RW_WIDTH), f(wt["rw_a2"][0])], axis=0)
    bw = (f(wt["mla_q_norm_g"]), wqn, wqr, f(wt["mla_kv_norm_g"]), f(wt["mla_w_ukv"][0]), f(wt["rw_mu"]), f(wt["rw_w0"]),
          w2p, f(wt["rw_a0"]), a2p, f(wt["rw_k_k"]), f(wt["rw_k_a"]))
    hw = (f(wt["rw_ln_g"]), f(wt["rw_ln_b"]), f(wt["rw_r_k"]).reshape(1, RW_WIDTH), f(wt["w_out"][0]), f(wt["norm_post_g"]))
    return wp, bw, hw


def _local_step(x, positions, target, wt):
    bsz, t, _ = x.shape
    n = bsz * t
    tm = min(TOKEN_TILE, t)
    tps = t // tm
    wp, bw, hw = _padded_weights(wt)
    wpb = wp.astype(BF16)
    g_pre = wt["norm_pre_g"].astype(F32)
    x2 = x.reshape(n, D_MODEL)
    tgt2 = target.reshape(n, D_MODEL)
    inv_freq = ROPE_THETA ** (-jnp.arange(0, MLA_ROPE, 2, dtype=F32) / MLA_ROPE)
    invf = jnp.tile(inv_freq, LANES // (MLA_ROPE // 2)).reshape(LANES, 1)
    cos, sin = _rope_tables(positions.reshape(1, n), invf, tm)

    u, pm, prw, z = _fwd_a(x2, g_pre, wpb, tm)
    qn, qr, kv, kr, r, w, k, v, nkk, b = _fwd_b(pm, prw, cos, sin, bw, tm, tps)
    b3 = lambda a: a.reshape(bsz, t, a.shape[-1])
    ym, lse = _attn_fwd(b3(qn), b3(qr), b3(kv), b3(kr))
    vt = _time_minor(b3(v))
    ys, st = _scan_fwd(b3(r), b3(w), b3(k), vt, b3(nkk), b3(b))
    (dys, dr_h, dk_h, dv_h, dym, dz, dxres, loss, d_lng, d_lnb, d_rk, d_wout, d_gpost) = _head(
        ys.reshape(n, RW_WIDTH), r, k, v, ym.reshape(n, MLA_WIDTH), z, x2, tgt2, hw, tm)
    dqn, dqr, dkv, dkr_heads = _attn_bwd(b3(qn), b3(qr), b3(kv), b3(kr), ym, b3(dym), lse)
    by_head = lambda a: a.reshape(bsz, t, RW_HEADS, RW_HEAD)
    dr_s, dw_s, dk_s, dv_s, dn_s, db_s = _scan_bwd(b3(r), b3(w), b3(k), by_head(v), b3(nkk), b3(b), st,
                                                   _time_minor(b3(dys)), by_head(dys))
    f2 = lambda a: a.reshape(n, a.shape[-1])
    cts = (f2(dqn), f2(dqr), f2(dkv), f2(dr_s), dr_h, f2(dw_s), f2(dk_s), dk_h, f2(dv_s), dv_h, f2(dn_s), f2(db_s))
    (dpm, dprw, dps, d_gq, d_wqn, d_wqr, d_gkv, d_wkv, d_mu, d_w0, d_w2p, d_a0, d_a2p, d_kk, d_ka) = _bwd_b(
        pm, prw, cos, sin, bw, cts, dkr_heads.reshape(MLA_HEADS, n, LANES), tm, tps)
    grad_x, dpb, d_gpre = _bwd_a(x2, g_pre, wpb, dpm, dprw, dps, dz, dxres, tm, tps)
    d_wp = _dw_in(u, dpb, min(1024, n), 640)

    d_w_in = jnp.concatenate([d_wp[:, :MLA_COLS], d_wp[:, PM_W:]], axis=1)
    d_w_uq = jnp.concatenate([d_wqn.reshape(Q_LORA, MLA_HEADS, MLA_NOPE),
                              d_wqr.reshape(Q_LORA, MLA_HEADS, LANES)[:, :, :MLA_ROPE]], axis=2)
    grads = {
        "norm_pre_g": d_gpre, "w_in": d_w_in[None], "mla_q_norm_g": d_gq,
        "mla_w_uq": d_w_uq.reshape(1, Q_LORA, MLA_HEADS * (MLA_NOPE + MLA_ROPE)), "mla_kv_norm_g": d_gkv,
        "mla_w_ukv": d_wkv[None], "rw_mu": d_mu, "rw_w0": d_w0, "rw_w2": d_w2p[None, :LORA], "rw_a0": d_a0,
        "rw_a2": d_a2p[None, LORA:], "rw_k_k": d_kk, "rw_k_a": d_ka, "rw_r_k": d_rk.reshape(1, RW_HEADS, RW_HEAD),
        "rw_ln_g": d_lng, "rw_ln_b": d_lnb, "w_out": d_wout[None], "norm_post_g": d_gpost,
    }
    return loss, grad_x.reshape(bsz, t, D_MODEL), grads


_MESH = pl.DeviceIdType.MESH


def _gather_shards(packed):
    rows, lanes = packed.shape

    def body(x_ref, out_ref, send_sems, recv_sems, local_sem):
        x, y, c = lax.axis_index("x"), lax.axis_index("y"), lax.axis_index("c")
        me, sibling = (x, y, c), (x, y, 1 - c)
        chips = [(1 - x, y), (x, 1 - y), (1 - x, 1 - y)]

        def slot(px, py, pc):
            return out_ref.at[4 * px + 2 * py + pc]

        def copy(k, block, to, src=None):
            return pltpu.make_async_remote_copy(
                src_ref=slot(*block) if src is None else src, dst_ref=slot(*block),
                send_sem=send_sems.at[k], recv_sem=recv_sems.at[k], device_id=to, device_id_type=_MESH)

        mine = pltpu.make_async_copy(x_ref, slot(*me), local_sem)
        mine.start()
        first = [copy(0, me, sibling, src=x_ref)]
        first += [copy(1 + j, me, (*chip, c), src=x_ref) for j, chip in enumerate(chips)]
        for cp in first:
            cp.start()
        passed = [copy(4 + j, (*chip, c), sibling) for j, chip in enumerate(chips)]
        for j, chip in enumerate(chips):
            copy(1 + j, (*chip, c), me).wait_recv()
            passed[j].start()
        copy(0, sibling, me).wait_recv()
        for j, chip in enumerate(chips):
            copy(4 + j, (*chip, 1 - c), me).wait_recv()
        for cp in first + passed:
            cp.wait_send()
        mine.wait()

    return pl.pallas_call(
        body, name="gather_shards",
        out_shape=_sds((N_DEV, rows, lanes), packed.dtype),
        in_specs=[pl.BlockSpec(memory_space=pltpu.VMEM)],
        out_specs=pl.BlockSpec(memory_space=pltpu.VMEM),
        scratch_shapes=[pltpu.SemaphoreType.DMA((7,)), pltpu.SemaphoreType.DMA((7,)), pltpu.SemaphoreType.DMA],
    )(packed)


def _exchange_grads(gbig, gsmall):
    _, rows, lanes = gbig.shape

    def body(big_ref, small_ref, rbig_ref, rsmall_ref, send_b, recv_b, send_s, recv_s, local_sem):
        x, y, c = lax.axis_index("x"), lax.axis_index("y"), lax.axis_index("c")
        me_lin = 4 * x + 2 * y + c
        mine = pltpu.make_async_copy(big_ref.at[me_lin], rbig_ref.at[0], local_sem)
        mine.start()
        rsmall_ref[me_lin] = small_ref[...]
        copies = []
        for k in range(1, N_DEV):
            px, py, pc = x ^ (k >> 2), y ^ ((k >> 1) & 1), c ^ (k & 1)
            peer = (px, py, pc)
            copies.append(pltpu.make_async_remote_copy(
                src_ref=big_ref.at[4 * px + 2 * py + pc], dst_ref=rbig_ref.at[k],
                send_sem=send_b.at[k - 1], recv_sem=recv_b.at[k - 1], device_id=peer, device_id_type=_MESH))
            copies.append(pltpu.make_async_remote_copy(
                src_ref=small_ref, dst_ref=rsmall_ref.at[me_lin],
                send_sem=send_s.at[k - 1], recv_sem=recv_s.at[k - 1], device_id=peer, device_id_type=_MESH))
        for cp in copies:
            cp.start()
        for cp in copies:
            cp.wait_recv()
        for cp in copies:
            cp.wait_send()
        mine.wait()

    return pl.pallas_call(
        body, name="exchange_grads",
        out_shape=[_sds((N_DEV, rows, lanes), gbig.dtype), _sds((N_DEV, SMALL_ROWS, lanes))],
        in_specs=[pl.BlockSpec(memory_space=pl.ANY), pl.BlockSpec(memory_space=pltpu.VMEM)],
        out_specs=[pl.BlockSpec(memory_space=pl.ANY), pl.BlockSpec(memory_space=pltpu.VMEM)],
        scratch_shapes=[pltpu.SemaphoreType.DMA((7,)), pltpu.SemaphoreType.DMA((7,)), pltpu.SemaphoreType.DMA((7,)),
                        pltpu.SemaphoreType.DMA((7,)), pltpu.SemaphoreType.DMA],
    )(gbig, gsmall)


def _adamw_math(w, g, m, v):
    m = ADAM_B1 * m + (1.0 - ADAM_B1) * g
    v = ADAM_B2 * v + (1.0 - ADAM_B2) * (g * g)
    m_hat = m / (1.0 - ADAM_B1 ** ADAM_STEP)
    v_hat = v / (1.0 - ADAM_B2 ** ADAM_STEP)
    return -ADAM_LR * (m_hat / (jnp.sqrt(v_hat) + ADAM_EPS) + ADAM_WD * w), m, v


def _reduce_adamw(parts, w, m, v, rb):
    _, rows, lanes = parts.shape

    def body(p_ref, w_ref, m_ref, v_ref, g_out, d_out, m_out, v_out):
        g = p_ref[0].astype(F32)
        for s in range(1, N_DEV):
            g = g + p_ref[s].astype(F32)
        g_out[...] = g
        d_out[...], m_out[...], v_out[...] = _adamw_math(w_ref[...], g, m_ref[...], v_ref[...])

    blk = pl.BlockSpec((rb, lanes), lambda i: (i, 0))
    return pl.pallas_call(
        body, name=f"reduce_adamw_{rows}", grid=(rows // rb,),
        in_specs=[pl.BlockSpec((N_DEV, rb, lanes), lambda i: (0, i, 0)), blk, blk, blk],
        out_specs=[blk] * 4, out_shape=[_sds((rows, lanes))] * 4,
        compiler_params=_ARB1,
    )(parts, w, m, v)


def _pack_rows(arrs):
    return jnp.concatenate([a.reshape(-1, LANES) for a in arrs], axis=0)


def _shard_blocks(name, full):
    a = full[0]
    if name == "w_out":
        return a.reshape(N_DEV, -1, LANES)
    rows, cols = a.shape
    return a.reshape(rows, N_DEV, cols // N_DEV).transpose(1, 0, 2).reshape(N_DEV, -1, LANES)


def _unshard(name, blocks, shard_shape):
    _, rows, cols = shard_shape
    a = blocks.reshape(N_DEV, rows, cols)
    if name == "w_out":
        return a.reshape(1, N_DEV * rows, cols)
    return a.transpose(1, 0, 2).reshape(1, rows, N_DEV * cols)


def _small_pack(vals, tail):
    flat = jnp.concatenate([vals[nm].reshape(-1) for nm, _ in SMALL] + [tail])
    return jnp.pad(flat, (0, SMALL_ROWS * LANES - flat.shape[0])).reshape(SMALL_ROWS, LANES)


def _small_unpack(packed, like):
    flat, out, off = packed.reshape(-1), {}, 0
    for nm, cnt in SMALL:
        out[nm] = flat[off:off + cnt].reshape(like[nm].shape)
        off += cnt
    return out


def _big_unpack(packed, like):
    out, off = {}, 0
    for nm, cnt in SHARD_ROWS:
        out[nm] = packed[off:off + cnt].reshape(like[nm].shape)
        off += cnt
    return out


def kernel(x, positions, norm_pre_g, w_in, mla_q_norm_g, mla_w_uq, mla_kv_norm_g, mla_w_ukv, rw_mu, rw_w0, rw_w2, rw_a0, rw_a2, rw_k_k, rw_k_a, rw_r_k, rw_ln_g, rw_ln_b, w_out, norm_post_g, loss_target, m_norm_pre_g, m_w_in, m_mla_q_norm_g, m_mla_w_uq, m_mla_kv_norm_g, m_mla_w_ukv, m_rw_mu, m_rw_w0, m_rw_w2, m_rw_a0, m_rw_a2, m_rw_k_k, m_rw_k_a, m_rw_r_k, m_rw_ln_g, m_rw_ln_b, m_w_out, m_norm_post_g, v_norm_pre_g, v_w_in, v_mla_q_norm_g, v_mla_w_uq, v_mla_kv_norm_g, v_mla_w_ukv, v_rw_mu, v_rw_w0, v_rw_w2, v_rw_a0, v_rw_a2, v_rw_k_k, v_rw_k_a, v_rw_r_k, v_rw_ln_g, v_rw_ln_b, v_w_out, v_norm_post_g):
    given = dict(locals())
    w = {nm: given[nm] for nm in WEIGHTS}
    mom = {nm: given["m_" + nm] for nm in WEIGHTS}
    var = {nm: given["v_" + nm] for nm in WEIGHTS}
    sharded = [nm for nm, _ in SHARD_ROWS]

    gathered = _gather_shards(_pack_rows([w[nm] for nm in sharded]).astype(BF16))
    full, off = dict(w), 0
    for nm, cnt in SHARD_ROWS:
        full[nm] = _unshard(nm, gathered[:, off:off + cnt], w[nm].shape)
        off += cnt

    loss_part, grad_x, grads = _local_step(x, positions, loss_target, full)

    gbig = jnp.concatenate([_shard_blocks(nm, grads[nm]).astype(BF16) for nm in sharded], axis=1)
    gsmall = _small_pack(grads, loss_part[0, 0:1])
    parts_big, parts_small = _exchange_grads(gbig, gsmall)

    g_b, d_b, m_b, v_b = _reduce_adamw(parts_big, _pack_rows([w[nm] for nm in sharded]), _pack_rows([mom[nm] for nm in sharded]),
                                       _pack_rows([var[nm] for nm in sharded]), PACK_ROWS // 4)
    zero1 = jnp.zeros((1,), F32)
    g_s, d_s, m_s, v_s = _reduce_adamw(parts_small, _small_pack(w, zero1), _small_pack(mom, zero1), _small_pack(var, zero1),
                                       SMALL_ROWS)
    loss = g_s.reshape(-1)[SMALL_N]
    outs = []
    for big, small in ((g_b, g_s), (d_b, d_s), (m_b, m_s), (v_b, v_s)):
        vals = {**_big_unpack(big, w), **_small_unpack(small, w)}
        outs += [vals[nm] for nm in WEIGHTS]
    return (loss, grad_x, *outs)
```

```python
import functools

import jax
import jax.numpy as jnp
from jax import lax
from jax.experimental import pallas as pl
from jax.experimental.pallas import tpu as pltpu

F32 = jnp.float32
BF16 = jnp.bfloat16

D_MODEL = 1024
MLA_HEADS = 4
MLA_NOPE = 128
MLA_ROPE = 64
MLA_V = 128
MLA_WIDTH = MLA_HEADS * MLA_V
Q_LORA = 256
KV_LORA = 128
ROPE_THETA = 10000.0
RW_HEAD = 64
RW_WIDTH = 512
RW_HEADS = RW_WIDTH // RW_HEAD
LORA = 64
RW_COLS = 3 * RW_WIDTH + 2 * LORA
MLA_COLS = Q_LORA + KV_LORA + MLA_ROPE
D_IN = MLA_COLS + RW_COLS + D_MODEL
RW_GN_EPS = 64e-5
NORM_EPS = 1e-6
ATT_SCALE = (MLA_NOPE + MLA_ROPE) ** -0.5
ADAM_LR, ADAM_B1, ADAM_B2, ADAM_EPS, ADAM_WD, ADAM_STEP = 0.001, 0.9, 0.999, 1e-08, 0.01, 10
N_DEV = 8
LANES = 128
MXU = 256

PM_W = 512
WP_COLS = PM_W + RW_COLS + D_MODEL
RW_PIECES = ((0, 512), (512, 1024), (1024, 1536), (1536, 1664))

SHARD_ROWS = (("w_in", 1024 * 392 // LANES), ("mla_w_uq", 256 * 96 // LANES), ("mla_w_ukv", 128 * 128 // LANES),
              ("rw_w2", 64 * 64 // LANES), ("rw_a2", 64 * 64 // LANES), ("w_out", 128 * 1024 // LANES))
PACK_ROWS = sum(r for _, r in SHARD_ROWS)
SMALL = (("norm_pre_g", 1024), ("mla_q_norm_g", 256), ("mla_kv_norm_g", 128), ("rw_mu", 1664), ("rw_w0", 512),
         ("rw_a0", 512), ("rw_k_k", 512), ("rw_k_a", 512), ("rw_r_k", 512), ("rw_ln_g", 512), ("rw_ln_b", 512),
         ("norm_post_g", 1024))
SMALL_N = sum(n for _, n in SMALL)
SMALL_ROWS = 72
WEIGHTS = ("norm_pre_g", "w_in", "mla_q_norm_g", "mla_w_uq", "mla_kv_norm_g", "mla_w_ukv", "rw_mu", "rw_w0", "rw_w2",
           "rw_a0", "rw_a2", "rw_k_k", "rw_k_a", "rw_r_k", "rw_ln_g", "rw_ln_b", "w_out", "norm_post_g")


def _seg_ones():
    r = lax.broadcasted_iota(jnp.int32, (MXU, MXU), 0) >> 6
    c = lax.broadcasted_iota(jnp.int32, (MXU, MXU), 1) >> 6
    return jnp.where(r == c, 1.0, 0.0).astype(BF16)


def _seg_dot(x, ones, passes):
    parts, rem = [], x
    for p in range(passes):
        hb = rem.astype(BF16)
        parts.append(hb)
        if p + 1 < passes:
            rem = rem - hb.astype(F32)
    outs = []
    for j in range(x.shape[1] // MXU):
        acc = None
        for hb in parts:
            d = jnp.dot(hb[:, MXU * j:MXU * (j + 1)], ones, preferred_element_type=F32)
            acc = d if acc is None else acc + d
        outs.append(acc)
    return outs[0] if len(outs) == 1 else jnp.concatenate(outs, axis=1)


def _seg_multi(xs, ones, passes):
    his = [x.astype(BF16) for x in xs]
    hi = jnp.concatenate(his, axis=0)
    if passes == 2:
        lo = jnp.concatenate([(x - h.astype(F32)).astype(BF16) for x, h in zip(xs, his)], axis=0)
        rhs = jnp.concatenate([ones, ones], axis=0)
    halves = []
    for j in range(hi.shape[1] // MXU):
        sl = slice(MXU * j, MXU * (j + 1))
        if passes == 2:
            halves.append(jnp.dot(jnp.concatenate([hi[:, sl], lo[:, sl]], axis=1), rhs, preferred_element_type=F32))
        else:
            halves.append(jnp.dot(hi[:, sl], ones, preferred_element_type=F32))
    full = jnp.concatenate(halves, axis=1)
    m = xs[0].shape[0]
    return [full[m * i:m * (i + 1)] for i in range(len(xs))]


@jax.custom_vjp
def _segsum(x):
    return _seg_dot(x, _seg_ones(), 3)


_segsum.defvjp(lambda x: (_segsum(x), None), lambda _, g: (_segsum(g),))


@jax.custom_vjp
def _bdot(a, w):
    return jnp.dot(a.astype(BF16), w.astype(BF16), preferred_element_type=F32)


def _bdot_fwd(a, w):
    return _bdot(a, w), (a, w)


def _bdot_bwd(res, g):
    a, w = res
    gb = g.astype(BF16)
    da = lax.dot_general(gb, w.astype(BF16), (((1,), (1,)), ((), ())), preferred_element_type=F32)
    dw = lax.dot_general(a.astype(BF16), gb, (((0,), (0,)), ((), ())), preferred_element_type=F32)
    return da, dw


_bdot.defvjp(_bdot_fwd, _bdot_bwd)


def _rot_impl(x):
    w = x.shape[1]
    lane = lax.broadcasted_iota(jnp.int32, x.shape, 1)
    return jnp.where((lane & 63) < 32, -pltpu.roll(x, w - 32, 1), pltpu.roll(x, 32, 1))


@jax.custom_vjp
def _rot(x):
    return _rot_impl(x)


_rot.defvjp(lambda x: (_rot_impl(x), None), lambda _, g: (-_rot_impl(g),))


def _rms(x, g):
    return x * lax.rsqrt(jnp.mean(x * x, axis=-1, keepdims=True) + NORM_EPS) * g


def _shift_rows(p, prev_row):
    row = lax.broadcasted_iota(jnp.int32, p.shape, 0)
    return jnp.where(row == 0, prev_row, pltpu.roll(p, 1, 0))


def _unshift_rows(g, next_row):
    row = lax.broadcasted_iota(jnp.int32, g.shape, 0)
    return jnp.where(row == g.shape[0] - 1, next_row, pltpu.roll(g, g.shape[0] - 1, 0))


def _f_mla(cq, ckv, kr, cos, sin, g_q, wqn, wqr, g_kv, wkv):
    qn = _rms(cq, g_q)
    q_nope = _bdot(qn, wqn)
    q_r = _bdot(qn, wqr)
    cos4 = jnp.concatenate([cos] * MLA_HEADS, axis=1)
    sin4 = jnp.concatenate([sin] * MLA_HEADS, axis=1)
    q_rope = q_r * cos4 + _rot(q_r) * sin4
    kv = _bdot(_rms(ckv, g_kv), wkv)
    k_rope = kr * cos + _rot(kr) * sin
    return q_nope, q_rope, kv, k_rope


def _f_rw(pr, pk, pv, pt, sr, sk, sv, st, mu_r, mu_k, mu_v, mu_t, w0, w2p, a0, a2p, k_k, k_a):
    r = pr + (sr - pr) * mu_r
    k = pk + (sk - pk) * mu_k
    v = pv + (sv - pv) * mu_v
    t = pt + (st - pt) * mu_t
    nwl = -(w0 + _bdot(jnp.tanh(t), w2p))
    softplus = jnp.maximum(nwl, 0.0) + jnp.log(1.0 + jnp.exp(-jnp.abs(nwl)))
    decay = jnp.exp(-jnp.exp(-softplus - 0.5))
    a = jax.nn.sigmoid(a0 + _bdot(t, a2p))
    kk = k * k_k
    kk = kk / jnp.maximum(jnp.sqrt(_segsum(kk * kk)), 1e-12)
    k2 = k * (1.0 + (a - 1.0) * k_a)
    return r, decay, k2, v, -kk, kk * a


def _f_head(ys, r, k, v, ym, z1, z2, x, tgt, ln_g, ln_b, r_k, w1, w2, g_post):
    inv = 1.0 / RW_HEAD
    yc = ys - _segsum(ys) * inv
    var = _segsum(yc * yc) * inv
    y = yc * lax.rsqrt(var + RW_GN_EPS) * ln_g + ln_b
    y_rw = y + _segsum(r * k * r_k) * v
    c1 = ym * (z1 * jax.nn.sigmoid(z1))
    c2 = y_rw * (z2 * jax.nn.sigmoid(z2))
    out = _bdot(c1, w1) + _bdot(c2, w2)
    err = x + _rms(out, g_post) - tgt
    per_row = jnp.sum(err * err, axis=1, keepdims=True)
    return jnp.sum(per_row, axis=0, keepdims=True) * (0.5 / D_MODEL)


def _rows(tm, width):
    return pl.BlockSpec((tm, width), lambda i: (i, 0))


def _whole(shape):
    zeros = (0,) * len(shape)
    return pl.BlockSpec(shape, lambda i: zeros)


def _sds(shape, dtype=F32):
    return jax.ShapeDtypeStruct(shape, dtype)


_ARB1 = pltpu.CompilerParams(dimension_semantics=("arbitrary",))


def _acc(ref, val, first):
    @pl.when(first)
    def _():
        ref[...] = val

    @pl.when(jnp.logical_not(first))
    def _():
        ref[...] += val


def _fwd_a(x2, g_pre, wp, tm):
    n = x2.shape[0]

    def body(x_ref, g_ref, w_ref, u_ref, pm_ref, prw_ref, z_ref):
        ub = _rms(x_ref[...], g_ref[...]).astype(BF16)
        u_ref[...] = ub
        pm_ref[...] = jnp.dot(ub, w_ref[:, 0:PM_W], preferred_element_type=F32)
        prw_ref[...] = jnp.dot(ub, w_ref[:, PM_W:PM_W + RW_COLS], preferred_element_type=F32)
        z_ref[...] = jnp.dot(ub, w_ref[:, PM_W + RW_COLS:WP_COLS], preferred_element_type=F32)

    return pl.pallas_call(
        body, name="fwd_a", grid=(n // tm,),
        in_specs=[_rows(tm, D_MODEL), _whole((1, D_MODEL)), _whole((D_MODEL, WP_COLS))],
        out_specs=[_rows(tm, D_MODEL), _rows(tm, PM_W), _rows(tm, RW_COLS), _rows(tm, D_MODEL)],
        out_shape=[_sds((n, D_MODEL), BF16), _sds((n, PM_W)), _sds((n, RW_COLS)), _sds((n, D_MODEL))],
        compiler_params=_ARB1,
    )(x2, g_pre, wp)


def _rope_tables(pos_row, invf_col, tm):
    n = pos_row.shape[1]

    def body(p_ref, f_ref, c_ref, s_ref):
        ang = f_ref[...] * p_ref[...].astype(F32)
        c_ref[...] = jnp.cos(ang).T
        s_ref[...] = jnp.sin(ang).T

    return pl.pallas_call(
        body, name="rope_tables", grid=(n // tm,),
        in_specs=[pl.BlockSpec((1, tm), lambda i: (0, i)), _whole((LANES, 1))],
        out_specs=[_rows(tm, LANES), _rows(tm, LANES)],
        out_shape=[_sds((n, LANES)), _sds((n, LANES))],
        compiler_params=_ARB1,
    )(pos_row, invf_col)


_B_WEIGHT_SHAPES = ((1, Q_LORA), (Q_LORA, 512), (Q_LORA, 512), (1, KV_LORA), (KV_LORA, 1024), (1, RW_COLS), (1, RW_WIDTH),
                    (LANES, RW_WIDTH), (1, RW_WIDTH), (LANES, RW_WIDTH), (1, RW_WIDTH), (1, RW_WIDTH))


def _halo_prev(tm):
    return pl.BlockSpec((8, RW_COLS), lambda i: (jnp.maximum(i * (tm // 8) - 1, 0), 0))


def _b_operands(pm_ref, prw_ref, halo_ref, wrefs, tile, tiles_per_seq):
    g_q, wqn, wqr, g_kv, wkv, mu, w0, w2p, a0, a2p, k_k, k_a = wrefs
    mla_in = (pm_ref[:, 0:Q_LORA], pm_ref[:, Q_LORA:Q_LORA + KV_LORA], pm_ref[:, Q_LORA + KV_LORA:PM_W])
    mla_w = (g_q[...], wqn[...], wqr[...], g_kv[...], wkv[...])
    keep = jnp.where(tile % tiles_per_seq == 0, 0.0, 1.0)
    prev = halo_ref[7:8, :] * keep
    ps = tuple(prw_ref[:, a:b] for a, b in RW_PIECES)
    ss = tuple(_shift_rows(p, prev[:, a:b]) for p, (a, b) in zip(ps, RW_PIECES))
    rw_w = tuple(mu[:, a:b] for a, b in RW_PIECES) + (w0[...], w2p[...], a0[...], a2p[...], k_k[...], k_a[...])
    return mla_in, mla_w, ps + ss, rw_w


def _fwd_b(pm, prw, cos, sin, bw, tm, tiles_per_seq):
    n = pm.shape[0]

    def body(pm_ref, prw_ref, halo_ref, cos_ref, sin_ref, *refs):
        wrefs, outs = refs[:12], refs[12:]
        mla_in, mla_w, rw_in, rw_w = _b_operands(pm_ref, prw_ref, halo_ref, wrefs, pl.program_id(0), tiles_per_seq)
        res = _f_mla(*mla_in, cos_ref[...], sin_ref[...], *mla_w) + _f_rw(*rw_in, *rw_w)
        for o_ref, val in zip(outs, res):
            o_ref[...] = val.astype(o_ref.dtype)

    widths = (512, 512, 1024, LANES) + (RW_WIDTH,) * 6
    return pl.pallas_call(
        body, name="fwd_b", grid=(n // tm,),
        in_specs=[_rows(tm, PM_W), _rows(tm, RW_COLS), _halo_prev(tm), _rows(tm, LANES), _rows(tm, LANES)]
        + [_whole(s) for s in _B_WEIGHT_SHAPES],
        out_specs=[_rows(tm, w) for w in widths],
        out_shape=[_sds((n, w), BF16 if j < 4 else F32) for j, w in enumerate(widths)],
        compiler_params=_ARB1,
    )(pm, prw, prw, cos, sin, *bw)


def _bwd_b(pm, prw, cos, sin, bw, cts, dkr_heads, tm, tiles_per_seq):
    n = pm.shape[0]

    ct_widths = (512, 512, 1024) + (RW_WIDTH,) * 9
    n_ct = len(ct_widths)

    def body(pm_ref, prw_ref, halo_ref, cos_ref, sin_ref, *refs):
        wrefs, ct_refs, dkr_ref = refs[:12], refs[12:12 + n_ct], refs[12 + n_ct]
        dpm_ref, dprw_ref, dps_ref = refs[13 + n_ct:16 + n_ct]
        wg_refs = refs[16 + n_ct:]
        tile = pl.program_id(0)
        first = tile == 0
        mla_in, mla_w, rw_in, rw_w = _b_operands(pm_ref, prw_ref, halo_ref, wrefs, tile, tiles_per_seq)
        cos, sin = cos_ref[...], sin_ref[...]
        ct = [r[...] for r in ct_refs]
        _, vjp_mla = jax.vjp(lambda *a: _f_mla(*a[:3], cos, sin, *a[3:]), *mla_in, *mla_w)
        dkr = dkr_ref[0] + dkr_ref[1] + dkr_ref[2] + dkr_ref[3]
        d_mla = vjp_mla((ct[0], ct[1], ct[2], dkr))
        dpm_ref[:, 0:Q_LORA] = d_mla[0]
        dpm_ref[:, Q_LORA:Q_LORA + KV_LORA] = d_mla[1]
        dpm_ref[:, Q_LORA + KV_LORA:PM_W] = d_mla[2]
        _, vjp_rw = jax.vjp(_f_rw, *rw_in, *rw_w)
        d_rw = vjp_rw((ct[3] + ct[4], ct[5], ct[6] + ct[7], ct[8] + ct[9], ct[10], ct[11]))
        for j, (a, b) in enumerate(RW_PIECES):
            dprw_ref[:, a:b] = d_rw[j]
            dps_ref[:, a:b] = d_rw[4 + j]
        g_q, wqn, wqr, g_kv, wkv, mu, w0, w2p, a0, a2p, k_k, k_a = wg_refs
        for ref, val in zip((g_q, wqn, wqr, g_kv, wkv), d_mla[3:]):
            _acc(ref, val, first)
        for j, (a, b) in enumerate(RW_PIECES):
            _acc(mu.at[:, a:b], d_rw[8 + j], first)
        for ref, val in zip((w0, w2p, a0, a2p, k_k, k_a), d_rw[12:]):
            _acc(ref, val, first)

    return pl.pallas_call(
        body, name="bwd_b", grid=(n // tm,),
        in_specs=[_rows(tm, PM_W), _rows(tm, RW_COLS), _halo_prev(tm), _rows(tm, LANES), _rows(tm, LANES)]
        + [_whole(s) for s in _B_WEIGHT_SHAPES] + [_rows(tm, w) for w in ct_widths]
        + [pl.BlockSpec((MLA_HEADS, tm, LANES), lambda i: (0, i, 0))],
        out_specs=[_rows(tm, PM_W), _rows(tm, RW_COLS), _rows(tm, RW_COLS)] + [_whole(s) for s in _B_WEIGHT_SHAPES],
        out_shape=[_sds((n, PM_W)), _sds((n, RW_COLS)), _sds((n, RW_COLS))] + [_sds(s) for s in _B_WEIGHT_SHAPES],
        compiler_params=_ARB1,
    )(pm, prw, prw, cos, sin, *bw, *cts, dkr_heads)


def _head(ys, r, k, v, ym, z, x2, tgt, hw, tm):
    n = x2.shape[0]
    h_shapes = ((1, RW_WIDTH), (1, RW_WIDTH), (1, RW_WIDTH), (D_MODEL, D_MODEL), (1, D_MODEL))

    def body(ys_ref, r_ref, k_ref, v_ref, ym_ref, z_ref, x_ref, t_ref, lng, lnb, rk, wout, gpost,
             dys_ref, dr_ref, dk_ref, dv_ref, dym_ref, dz_ref, dx_ref, loss_ref, dlng, dlnb, drk, dwout, dgpost):
        first = pl.program_id(0) == 0
        tgt_v = t_ref[...]
        args = (ys_ref[...], r_ref[...], k_ref[...], v_ref[...], ym_ref[...], z_ref[:, 0:MLA_WIDTH], z_ref[:, MLA_WIDTH:D_MODEL],
                x_ref[...], lng[...], lnb[...], rk[...], wout[0:MLA_WIDTH, :], wout[MLA_WIDTH:D_MODEL, :], gpost[...])
        loss, vjp = jax.vjp(lambda *a: _f_head(*a[:8], tgt_v, *a[8:]), *args)
        d = vjp(jnp.ones((1, 1), F32))
        dys_ref[...] = d[0]
        dr_ref[...] = d[1]
        dk_ref[...] = d[2]
        dv_ref[...] = d[3]
        dym_ref[...] = d[4].astype(BF16)
        dz_ref[:, 0:MLA_WIDTH] = d[5]
        dz_ref[:, MLA_WIDTH:D_MODEL] = d[6]
        dx_ref[...] = d[7]
        _acc(loss_ref, jnp.broadcast_to(loss, (8, LANES)), first)
        _acc(dlng, d[8], first)
        _acc(dlnb, d[9], first)
        _acc(drk, d[10], first)
        _acc(dwout.at[0:MLA_WIDTH, :], d[11], first)
        _acc(dwout.at[MLA_WIDTH:D_MODEL, :], d[12], first)
        _acc(dgpost, d[13], first)

    widths = (RW_WIDTH,) * 4 + (MLA_WIDTH, D_MODEL, D_MODEL)
    return pl.pallas_call(
        body, name="head", grid=(n // tm,),
        in_specs=[_rows(tm, RW_WIDTH)] * 4 + [_rows(tm, MLA_WIDTH), _rows(tm, D_MODEL), _rows(tm, D_MODEL), _rows(tm, D_MODEL)]
        + [_whole(s) for s in h_shapes],
        out_specs=[_rows(tm, w) for w in widths] + [_whole((8, LANES))] + [_whole(s) for s in h_shapes],
        out_shape=[_sds((n, w), BF16 if j == 4 else F32) for j, w in enumerate(widths)] + [_sds((8, LANES))]
        + [_sds(s) for s in h_shapes],
        compiler_params=_ARB1,
    )(ys, r, k, v, ym, z, x2, tgt, *hw)


def _halo_next(tm, n):
    last = n // 8 - 1
    return pl.BlockSpec((8, RW_COLS), lambda i: (jnp.minimum((i + 1) * (tm // 8), last), 0))


def _bwd_a(x2, g_pre, wp, dpm, dprw, dps, dz, dxres, tm, tiles_per_seq):
    n = x2.shape[0]
    nt_dims = (((1,), (1,)), ((), ()))

    def body(x_ref, g_ref, w_ref, dpm_ref, dprw_ref, dps_ref, nxt_ref, dz_ref, dxres_ref, gx_ref, dpb_ref, dg_ref):
        tile = pl.program_id(0)
        keep = jnp.where((tile + 1) % tiles_per_seq == 0, 0.0, 1.0)
        dprw_v = dprw_ref[...] + _unshift_rows(dps_ref[...], nxt_ref[0:1, :] * keep)
        dpm_b, dprw_b, dz_b = dpm_ref[...].astype(BF16), dprw_v.astype(BF16), dz_ref[...].astype(BF16)
        dpb_ref[:, 0:PM_W] = dpm_b
        dpb_ref[:, PM_W:PM_W + RW_COLS] = dprw_b
        dpb_ref[:, PM_W + RW_COLS:WP_COLS] = dz_b
        du = (lax.dot_general(dpm_b, w_ref[:, 0:PM_W], nt_dims, preferred_element_type=F32)
              + lax.dot_general(dprw_b, w_ref[:, PM_W:PM_W + RW_COLS], nt_dims, preferred_element_type=F32)
              + lax.dot_general(dz_b, w_ref[:, PM_W + RW_COLS:WP_COLS], nt_dims, preferred_element_type=F32))
        x = x_ref[...]
        xhat = x * lax.rsqrt(jnp.mean(x * x, axis=-1, keepdims=True) + NORM_EPS)
        dxn = du * g_ref[...]
        dx = (dxn - xhat * jnp.mean(dxn * xhat, axis=-1, keepdims=True)) * lax.rsqrt(jnp.mean(x * x, axis=-1, keepdims=True) + NORM_EPS)
        gx_ref[...] = dx + dxres_ref[...]
        _acc(dg_ref, jnp.sum(du * xhat, axis=0, keepdims=True), tile == 0)

    return pl.pallas_call(
        body, name="bwd_a", grid=(n // tm,),
        in_specs=[_rows(tm, D_MODEL), _whole((1, D_MODEL)), _whole((D_MODEL, WP_COLS)), _rows(tm, PM_W), _rows(tm, RW_COLS),
                  _rows(tm, RW_COLS), _halo_next(tm, n), _rows(tm, D_MODEL), _rows(tm, D_MODEL)],
        out_specs=[_rows(tm, D_MODEL), _rows(tm, WP_COLS), _whole((1, D_MODEL))],
        out_shape=[_sds((n, D_MODEL)), _sds((n, WP_COLS), BF16), _sds((1, D_MODEL))],
        compiler_params=_ARB1,
    )(x2, g_pre, wp, dpm, dprw, dps, dps, dz, dxres)


def _dw_in(u, dpb, tk, tn):
    n = u.shape[0]
    steps = n // tk

    def body(u_ref, d_ref, o_ref, acc_sc):
        k = pl.program_id(1)
        _acc(acc_sc, lax.dot_general(u_ref[...], d_ref[...], _TN, preferred_element_type=F32), k == 0)

        @pl.when(k == steps - 1)
        def _():
            o_ref[...] = acc_sc[...].astype(BF16)

    return pl.pallas_call(
        body, name="dw_in", grid=(WP_COLS // tn, steps),
        in_specs=[pl.BlockSpec((tk, D_MODEL), lambda j, k: (k, 0)), pl.BlockSpec((tk, tn), lambda j, k: (k, j))],
        out_specs=pl.BlockSpec((D_MODEL, tn), lambda j, k: (0, j)),
        out_shape=_sds((D_MODEL, WP_COLS), BF16),
        scratch_shapes=[pltpu.VMEM((D_MODEL, tn), F32)],
        compiler_params=pltpu.CompilerParams(dimension_semantics=("arbitrary", "arbitrary")),
    )(u, dpb)


ATT_BLK = 256
_NT = (((1,), (1,)), ((), ()))
_TN = (((0,), (0,)), ((), ()))


def _causal(q0, k0, blk):
    row = q0 + lax.broadcasted_iota(jnp.int32, (blk, blk), 0)
    col = k0 + lax.broadcasted_iota(jnp.int32, (blk, blk), 1)
    return row >= col


def _attn_fwd(qn, qr, kv, kr):
    bsz, t, _ = qn.shape
    blk = min(ATT_BLK, t)

    heads = range(MLA_HEADS)

    def body(qn_ref, qr_ref, kv_ref, kr_ref, o_ref, lse_ref):
        qi = pl.program_id(1)
        q = [jnp.concatenate([qn_ref[:, LANES * h:LANES * (h + 1)], qr_ref[:, LANES * h:LANES * (h + 1)]], axis=1) for h in heads]
        lower = _causal(0, 0, blk)

        def kv_step(j, carry, diagonal):
            ks = pl.multiple_of(j * blk, blk)
            k_rope = kr_ref[pl.ds(ks, blk), :]
            out = []
            for h in heads:
                m, l, acc = carry[h]
                k = jnp.concatenate([kv_ref[pl.ds(ks, blk), 2 * LANES * h:2 * LANES * h + LANES], k_rope], axis=1)
                s = lax.dot_general(q[h], k, _NT, preferred_element_type=F32) * ATT_SCALE
                if diagonal:
                    s = jnp.where(lower, s, -1e30)
                m_new = jnp.maximum(m, jnp.max(s, axis=1, keepdims=True))
                alpha = jnp.exp(m - m_new)
                p = jnp.exp(s - m_new)
                l = alpha * l + jnp.sum(p, axis=1, keepdims=True)
                v = kv_ref[pl.ds(ks, blk), 2 * LANES * h + LANES:2 * LANES * (h + 1)]
                acc = alpha * acc + jnp.dot(p.astype(BF16), v, preferred_element_type=F32)
                out.append((m_new, l, acc))
            return tuple(out)

        one = (jnp.full((blk, 1), -1e30, F32), jnp.zeros((blk, 1), F32), jnp.zeros((blk, MLA_V), F32))
        carry = lax.fori_loop(0, qi, lambda j, c: kv_step(j, c, False), (one,) * MLA_HEADS)
        carry = kv_step(qi, carry, True)
        for h in heads:
            m, l, acc = carry[h]
            o_ref[:, LANES * h:LANES * (h + 1)] = acc / l
            lse_ref[h] = jnp.broadcast_to(m + jnp.log(l), (blk, LANES))

    return pl.pallas_call(
        body, name="attn_fwd", grid=(bsz, t // blk),
        in_specs=[pl.BlockSpec((None, blk, MLA_WIDTH), lambda b, i: (b, i, 0)),
                  pl.BlockSpec((None, blk, MLA_WIDTH), lambda b, i: (b, i, 0)),
                  pl.BlockSpec((None, t, 2 * MLA_WIDTH), lambda b, i: (b, 0, 0)),
                  pl.BlockSpec((None, t, LANES), lambda b, i: (b, 0, 0))],
        out_specs=[pl.BlockSpec((None, blk, MLA_WIDTH), lambda b, i: (b, i, 0)),
                   pl.BlockSpec((None, MLA_HEADS, blk, LANES), lambda b, i: (b, 0, i, 0))],
        out_shape=[_sds((bsz, t, MLA_WIDTH)), _sds((bsz, MLA_HEADS, t, LANES))],
        compiler_params=pltpu.CompilerParams(dimension_semantics=("arbitrary", "arbitrary")),
    )(qn, qr, kv, kr)


def _attn_bwd(qn, qr, kv, kr, o, do, lse):
    bsz, t, _ = qn.shape
    blk = min(ATT_BLK, t)
    nb = t // blk

    def body(qn_ref, qr_ref, kn_ref, kr_ref, v_ref, o_ref, do_ref, lse_ref, dqn_ref, dqr_ref, dkv_ref, dkr_ref, dq_sc, delta_sc):
        dq_sc[...] = jnp.zeros_like(dq_sc)
        delta_sc[...] = jnp.sum(do_ref[...].astype(F32) * o_ref[...], axis=1, keepdims=True)

        def kv_loop(j, _):
            ks = pl.multiple_of(j * blk, blk)
            k = jnp.concatenate([kn_ref[pl.ds(ks, blk), :], kr_ref[pl.ds(ks, blk), :]], axis=1)
            vb = v_ref[pl.ds(ks, blk), :]

            def q_pair(pair, carry):
                dk, dv = carry
                for sub in range(2):
                    i = 2 * pair + sub
                    qs = pl.multiple_of(i * blk, blk)
                    q = jnp.concatenate([qn_ref[pl.ds(qs, blk), :], qr_ref[pl.ds(qs, blk), :]], axis=1)
                    dob = do_ref[pl.ds(qs, blk), :]
                    s = lax.dot_general(q, k, _NT, preferred_element_type=F32) * ATT_SCALE
                    p = jnp.where(_causal(i * blk, j * blk, blk), jnp.exp(s - lse_ref[pl.ds(qs, blk), 0:1]), 0.0)
                    dv = dv + lax.dot_general(p.astype(BF16), dob, _TN, preferred_element_type=F32)
                    dp = lax.dot_general(dob, vb, _NT, preferred_element_type=F32)
                    ds = (p * (dp - delta_sc[pl.ds(qs, blk), :]) * ATT_SCALE).astype(BF16)
                    dq_sc[pl.ds(qs, blk), :] += jnp.dot(ds, k, preferred_element_type=F32)
                    dk = dk + lax.dot_general(ds, q, _TN, preferred_element_type=F32)
                return dk, dv

            dk, dv = lax.fori_loop(j // 2, nb // 2, q_pair, (jnp.zeros((blk, 2 * LANES), F32), jnp.zeros((blk, MLA_V), F32)))
            dkv_ref[pl.ds(ks, blk), 0:LANES] = dk[:, 0:LANES]
            dkv_ref[pl.ds(ks, blk), LANES:2 * LANES] = dv
            dkr_ref[pl.ds(ks, blk), :] = dk[:, LANES:2 * LANES]
            return 0

        lax.fori_loop(0, nb, kv_loop, 0)
        dqn_ref[...] = dq_sc[:, 0:LANES]
        dqr_ref[...] = dq_sc[:, LANES:2 * LANES]

    head_col = lambda b, h: (b, 0, h)
    return pl.pallas_call(
        body, name="attn_bwd", grid=(bsz, MLA_HEADS),
        in_specs=[pl.BlockSpec((None, t, LANES), head_col), pl.BlockSpec((None, t, LANES), head_col),
                  pl.BlockSpec((None, t, LANES), lambda b, h: (b, 0, 2 * h)),
                  pl.BlockSpec((None, t, LANES), lambda b, h: (b, 0, 0)),
                  pl.BlockSpec((None, t, LANES), lambda b, h: (b, 0, 2 * h + 1)),
                  pl.BlockSpec((None, t, LANES), head_col), pl.BlockSpec((None, t, LANES), head_col),
                  pl.BlockSpec((None, None, t, LANES), lambda b, h: (b, h, 0, 0))],
        out_specs=[pl.BlockSpec((None, t, LANES), head_col), pl.BlockSpec((None, t, LANES), head_col),
                   pl.BlockSpec((None, t, 2 * LANES), head_col),
                   pl.BlockSpec((None, None, t, LANES), lambda b, h: (h, b, 0, 0))],
        out_shape=[_sds((bsz, t, MLA_WIDTH)), _sds((bsz, t, MLA_WIDTH)), _sds((bsz, t, 2 * MLA_WIDTH)),
                   _sds((MLA_HEADS, bsz, t, LANES))],
        scratch_shapes=[pltpu.VMEM((t, 2 * LANES), F32), pltpu.VMEM((t, 1), F32)],
        compiler_params=pltpu.CompilerParams(dimension_semantics=("arbitrary", "arbitrary")),
    )(qn, qr, kv, kr, kv, o, do, lse)


SCAN_CHUNK = 16


def _diag_mask():
    row = lax.broadcasted_iota(jnp.int32, (RW_HEAD, RW_WIDTH), 0)
    lane = lax.broadcasted_iota(jnp.int32, (RW_HEAD, RW_WIDTH), 1)
    return jnp.where(row == (lane & (RW_HEAD - 1)), 1.0, 0.0)


def _time_minor(a):
    bsz, t, _ = a.shape
    a = a.reshape(bsz, t // SCAN_CHUNK, SCAN_CHUNK, RW_HEADS, RW_HEAD)
    return a.transpose(0, 1, 4, 3, 2).reshape(bsz, t // SCAN_CHUNK, RW_HEAD, RW_HEADS * SCAN_CHUNK)


def _head_expand():
    l = lax.broadcasted_iota(jnp.int32, (2 * LANES, RW_WIDTH), 0)
    n = lax.broadcasted_iota(jnp.int32, (2 * LANES, RW_WIDTH), 1)
    return jnp.where(((l & (LANES - 1)) >> 4) == (n >> 6), 1.0, 0.0).astype(BF16)


def _col_bcast_chunk(tm_ref, out_sc, expand, seqs):
    step_of_lane = lax.broadcasted_iota(jnp.int32, (RW_HEAD, LANES), 1) & (SCAN_CHUNK - 1)
    tiles = [tm_ref[bi, 0] for bi in seqs]
    for t in range(SCAN_CHUNK):
        parts = []
        for tile in tiles:
            a = jnp.where(step_of_lane == t, tile, 0.0)
            hi = a.astype(BF16)
            parts.append(jnp.concatenate([hi, (a - hi.astype(F32)).astype(BF16)], axis=1))
        out_sc[t] = jnp.dot(jnp.concatenate(parts, axis=0), expand, preferred_element_type=F32)


def _fold8(x):
    acc = x[0:8]
    for j in range(1, x.shape[0] // 8):
        acc = acc + x[8 * j:8 * (j + 1)]
    return acc


def _rows8(at):
    return pl.ds(at * 8 if isinstance(at, int) else pl.multiple_of(at * 8, 8), 8)


def _put8(sc, bi, at, val):
    for j in range(RW_WIDTH // LANES):
        sc[bi * (RW_WIDTH // LANES) + j, _rows8(at), :] = val[:, LANES * j:LANES * (j + 1)]


def _unfold8(sc, bi, steps):
    tiles = []
    for j in range(RW_WIDTH // LANES):
        view = sc.at[bi * (RW_WIDTH // LANES) + j]
        acc = view[pl.ds(0, steps, stride=8), :]
        for s in range(1, 8):
            acc = acc + view[pl.ds(s, steps, stride=8), :]
        tiles.append(acc)
    return jnp.concatenate(tiles, axis=1)


def _scan_fwd(r, w, k, vt, nkk, b):
    bsz, t, _ = r.shape
    tc = SCAN_CHUNK

    def body(r_ref, w_ref, k_ref, n_ref, b_ref, vt_ref, y_ref, st_ref, s_sc, vc_sc, y_sc):
        @pl.when(pl.program_id(0) == 0)
        def _():
            s_sc[...] = jnp.zeros_like(s_sc)

        ones = _seg_ones()
        diag = _diag_mask()
        seqs = range(bsz)
        _col_bcast_chunk(vt_ref, vc_sc, _head_expand(), seqs)

        def put_y(ya, at):
            for bi in seqs:
                _put8(y_sc, bi, at, _fold8(ya[bi] * diag))

        def step(i, _):
            row = lambda ref, bi: ref[bi, pl.ds(i, 1), :]
            prev = jnp.maximum(i - 1, 0)
            s_old = [s_sc[bi] for bi in seqs]
            sa = _seg_multi([s_old[bi] * row(n_ref, bi) for bi in seqs], ones, 1)
            put_y(_seg_multi([s_old[bi] * r_ref[bi, pl.ds(prev, 1), :] for bi in seqs], ones, 1), prev)
            vc = vc_sc[i]
            for bi in seqs:
                s_new = (s_old[bi] * row(w_ref, bi) + sa[bi] * row(b_ref, bi)
                         + vc[RW_HEAD * bi:RW_HEAD * (bi + 1)] * row(k_ref, bi))
                s_sc[bi] = s_new
                st_ref[bi, i] = s_new
            return 0

        lax.fori_loop(0, tc, step, 0, unroll=8)
        put_y(_seg_multi([s_sc[bi] * r_ref[bi, tc - 1:tc, :] for bi in seqs], ones, 1), tc - 1)
        for bi in seqs:
            y_ref[bi] = _unfold8(y_sc, bi, tc)

    vec = pl.BlockSpec((bsz, tc, RW_WIDTH), lambda c: (0, c, 0))
    return pl.pallas_call(
        body, name="scan_fwd", grid=(t // tc,),
        in_specs=[vec] * 5 + [pl.BlockSpec((bsz, 1, RW_HEAD, LANES), lambda c: (0, c, 0, 0))],
        out_specs=[vec, pl.BlockSpec((bsz, tc, RW_HEAD, RW_WIDTH), lambda c: (0, c, 0, 0))],
        out_shape=[_sds((bsz, t, RW_WIDTH)), _sds((bsz, t, RW_HEAD, RW_WIDTH))],
        scratch_shapes=[pltpu.VMEM((bsz, RW_HEAD, RW_WIDTH), F32), pltpu.VMEM((tc, bsz * RW_HEAD, RW_WIDTH), F32),
                        pltpu.VMEM((bsz * RW_WIDTH // LANES, tc * 8, LANES), F32)],
        compiler_params=_ARB1,
    )(r, w, k, nkk, b, vt)


def _own_head_rows(x):
    row = lax.broadcasted_iota(jnp.int32, x.shape, 0)
    lane = lax.broadcasted_iota(jnp.int32, x.shape, 1)
    return jnp.sum(jnp.where(row == (lane >> 6), x, 0.0), axis=0, keepdims=True)


def _scan_bwd(r, w, k, vt, nkk, b, st, dyt):
    bsz, t, _ = r.shape
    tc = SCAN_CHUNK
    nc = t // tc

    def body(r_ref, w_ref, k_ref, n_ref, b_ref, vt_ref, dyt_ref, st_ref, halo_ref,
             dr_ref, dw_ref, dk_ref, dv_ref, dn_ref, db_ref, g_sc, dc_sc, v8_sc, dy8_sc, *part_scs):
        c = pl.program_id(0)

        @pl.when(c == 0)
        def _():
            g_sc[...] = jnp.zeros_like(g_sc)

        ones = _seg_ones()
        diag = _diag_mask()
        has_prev = jnp.where(c == nc - 1, 0.0, 1.0)
        seqs = range(bsz)
        _col_bcast_chunk(dyt_ref, dc_sc, _head_expand(), seqs)
        for bi in seqs:
            v8_sc[bi] = jnp.concatenate([vt_ref[bi, 0].T] * 2, axis=1)
            dy8_sc[bi] = jnp.concatenate([dyt_ref[bi, 0].T] * 2, axis=1)
        by_head = lambda sc, bi, i: sc.at[bi][pl.ds(i, RW_HEADS, stride=SCAN_CHUNK), :][:, 0:RW_HEAD].astype(BF16)
        dw_sc, dv_sc, dn_sc, db_sc = part_scs

        def step(i, s_p):
            static = isinstance(i, int)
            row = lambda ref, bi: ref[bi, i:i + 1, :] if static else ref[bi, pl.ds(i, 1), :]
            put_row = lambda ref, bi, val: ref.__setitem__((bi, slice(i, i + 1) if static else pl.ds(i, 1), slice(None)), val)
            dc_all = dc_sc[i]
            dc = [dc_all[RW_HEAD * bi:RW_HEAD * (bi + 1)] for bi in seqs]
            g = [g_sc[bi] + dc[bi] * row(r_ref, bi) for bi in seqs]
            res = _seg_multi([s_p[bi] * row(n_ref, bi) for bi in seqs] + [g[bi] * row(b_ref, bi) for bi in seqs]
                             + [g[bi] * row(k_ref, bi) for bi in seqs], ones, 1)
            sa, dsa, dvb = res[:bsz], res[bsz:2 * bsz], res[2 * bsz:]
            for bi in seqs:
                dr8 = jnp.dot(by_head(dy8_sc, bi, i), st_ref[bi, i].astype(BF16), preferred_element_type=F32)
                dk8 = jnp.dot(by_head(v8_sc, bi, i), g[bi].astype(BF16), preferred_element_type=F32)
                put_row(dr_ref, bi, _own_head_rows(dr8))
                put_row(dk_ref, bi, _own_head_rows(dk8))
                _put8(dv_sc, bi, i, _fold8(dvb[bi] * diag))
                _put8(dw_sc, bi, i, _fold8(g[bi] * s_p[bi]))
                _put8(db_sc, bi, i, _fold8(g[bi] * sa[bi]))
                _put8(dn_sc, bi, i, _fold8(s_p[bi] * dsa[bi]))
                g_sc[bi] = g[bi] * row(w_ref, bi) + dsa[bi] * row(n_ref, bi)

        def loop_step(ii, _):
            i = tc - 1 - ii
            step(i, [st_ref[bi, i - 1] for bi in seqs])
            return 0

        lax.fori_loop(0, tc - 1, loop_step, 0, unroll=5)
        step(0, [halo_ref[bi, 0] * has_prev for bi in seqs])
        for out_ref, sc in zip((dw_ref, dv_ref, dn_ref, db_ref), part_scs):
            for bi in seqs:
                out_ref[bi] = _unfold8(sc, bi, tc)

    vec = pl.BlockSpec((bsz, tc, RW_WIDTH), lambda c: (0, nc - 1 - c, 0))
    tmin = pl.BlockSpec((bsz, 1, RW_HEAD, LANES), lambda c: (0, nc - 1 - c, 0, 0))
    parts = pltpu.VMEM((bsz * RW_WIDTH // LANES, tc * 8, LANES), F32)
    heads_steps = pltpu.VMEM((bsz, LANES, LANES), F32)
    return pl.pallas_call(
        body, name="scan_bwd", grid=(nc,),
        in_specs=[vec] * 5 + [tmin, tmin,
                              pl.BlockSpec((bsz, tc, RW_HEAD, RW_WIDTH), lambda c: (0, nc - 1 - c, 0, 0)),
                              pl.BlockSpec((bsz, 1, RW_HEAD, RW_WIDTH), lambda c: (0, jnp.maximum((nc - 1 - c) * tc - 1, 0), 0, 0))],
        out_specs=[vec] * 6,
        out_shape=[_sds((bsz, t, RW_WIDTH))] * 6,
        scratch_shapes=[pltpu.VMEM((bsz, RW_HEAD, RW_WIDTH), F32), pltpu.VMEM((tc, bsz * RW_HEAD, RW_WIDTH), F32),
                        heads_steps, heads_steps] + [parts] * 4,
        compiler_params=_ARB1,
    )(r, w, k, nkk, b, vt, dyt, st, st)


TOKEN_TILE = 256


def _padded_weights(wt):
    f = lambda a: a.astype(F32)
    w_in = wt["w_in"][0].astype(BF16)
    zeros = lambda r, c: jnp.zeros((r, c), F32)
    wp = jnp.concatenate([w_in[:, :MLA_COLS], jnp.zeros((D_MODEL, PM_W - MLA_COLS), BF16), w_in[:, MLA_COLS:]], axis=1)
    w_uq = f(wt["mla_w_uq"][0]).reshape(Q_LORA, MLA_HEADS, MLA_NOPE + MLA_ROPE)
    wqn = w_uq[:, :, :MLA_NOPE].reshape(Q_LORA, MLA_HEADS * MLA_NOPE)
    wqr = jnp.concatenate([w_uq[:, :, MLA_NOPE:], jnp.zeros((Q_LORA, MLA_HEADS, LANES - MLA_ROPE), F32)], axis=2)
    wqr = wqr.reshape(Q_LORA, MLA_HEADS * LANES)
    w2p = jnp.concatenate([f(wt["rw_w2"][0]), zeros(LORA, RW_WIDTH)], axis=0)
    a2p = jnp.concatenate([zeros(LORA, RW_WIDTH), f(wt["rw_a2"][0])], axis=0)
    bw = (f(wt["mla_q_norm_g"]), wqn, wqr, f(wt["mla_kv_norm_g"]), f(wt["mla_w_ukv"][0]), f(wt["rw_mu"]), f(wt["rw_w0"]),
          w2p, f(wt["rw_a0"]), a2p, f(wt["rw_k_k"]), f(wt["rw_k_a"]))
    hw = (f(wt["rw_ln_g"]), f(wt["rw_ln_b"]), f(wt["rw_r_k"]).reshape(1, RW_WIDTH), f(wt["w_out"][0]), f(wt["norm_post_g"]))
    return wp, bw, hw


def _local_step(x, positions, target, wt):
    bsz, t, _ = x.shape
    n = bsz * t
    tm = min(TOKEN_TILE, t)
    tps = t // tm
    wp, bw, hw = _padded_weights(wt)
    wpb = wp.astype(BF16)
    g_pre = wt["norm_pre_g"].astype(F32)
    x2 = x.reshape(n, D_MODEL)
    tgt2 = target.reshape(n, D_MODEL)
    inv_freq = ROPE_THETA ** (-jnp.arange(0, MLA_ROPE, 2, dtype=F32) / MLA_ROPE)
    invf = jnp.tile(inv_freq, LANES // (MLA_ROPE // 2)).reshape(LANES, 1)
    cos, sin = _rope_tables(positions.reshape(1, n), invf, tm)

    u, pm, prw, z = _fwd_a(x2, g_pre, wpb, tm)
    qn, qr, kv, kr, r, w, k, v, nkk, b = _fwd_b(pm, prw, cos, sin, bw, tm, tps)
    b3 = lambda a: a.reshape(bsz, t, a.shape[-1])
    ym, lse = _attn_fwd(b3(qn), b3(qr), b3(kv), b3(kr))
    vt = _time_minor(b3(v))
    ys, st = _scan_fwd(b3(r), b3(w), b3(k), vt, b3(nkk), b3(b))
    (dys, dr_h, dk_h, dv_h, dym, dz, dxres, loss, d_lng, d_lnb, d_rk, d_wout, d_gpost) = _head(
        ys.reshape(n, RW_WIDTH), r, k, v, ym.reshape(n, MLA_WIDTH), z, x2, tgt2, hw, tm)
    dqn, dqr, dkv, dkr_heads = _attn_bwd(b3(qn), b3(qr), b3(kv), b3(kr), ym, b3(dym), lse)
    dr_s, dw_s, dk_s, dv_s, dn_s, db_s = _scan_bwd(b3(r), b3(w), b3(k), vt, b3(nkk), b3(b), st, _time_minor(b3(dys)))
    f2 = lambda a: a.reshape(n, a.shape[-1])
    cts = (f2(dqn), f2(dqr), f2(dkv), f2(dr_s), dr_h, f2(dw_s), f2(dk_s), dk_h, f2(dv_s), dv_h, f2(dn_s), f2(db_s))
    (dpm, dprw, dps, d_gq, d_wqn, d_wqr, d_gkv, d_wkv, d_mu, d_w0, d_w2p, d_a0, d_a2p, d_kk, d_ka) = _bwd_b(
        pm, prw, cos, sin, bw, cts, dkr_heads.reshape(MLA_HEADS, n, LANES), tm, tps)
    grad_x, dpb, d_gpre = _bwd_a(x2, g_pre, wpb, dpm, dprw, dps, dz, dxres, tm, tps)
    d_wp = _dw_in(u, dpb, min(1024, n), 640)

    d_w_in = jnp.concatenate([d_wp[:, :MLA_COLS], d_wp[:, PM_W:]], axis=1)
    d_w_uq = jnp.concatenate([d_wqn.reshape(Q_LORA, MLA_HEADS, MLA_NOPE),
                              d_wqr.reshape(Q_LORA, MLA_HEADS, LANES)[:, :, :MLA_ROPE]], axis=2)
    grads = {
        "norm_pre_g": d_gpre, "w_in": d_w_in[None], "mla_q_norm_g": d_gq,
        "mla_w_uq": d_w_uq.reshape(1, Q_LORA, MLA_HEADS * (MLA_NOPE + MLA_ROPE)), "mla_kv_norm_g": d_gkv,
        "mla_w_ukv": d_wkv[None], "rw_mu": d_mu, "rw_w0": d_w0, "rw_w2": d_w2p[None, :LORA], "rw_a0": d_a0,
        "rw_a2": d_a2p[None, LORA:], "rw_k_k": d_kk, "rw_k_a": d_ka, "rw_r_k": d_rk.reshape(1, RW_HEADS, RW_HEAD),
        "rw_ln_g": d_lng, "rw_ln_b": d_lnb, "w_out": d_wout[None], "norm_post_g": d_gpost,
    }
    return loss, grad_x.reshape(bsz, t, D_MODEL), grads


_MESH = pl.DeviceIdType.MESH


def _gather_shards(packed):
    rows, lanes = packed.shape

    def body(x_ref, out_ref, send_sems, recv_sems, local_sem):
        x, y, c = lax.axis_index("x"), lax.axis_index("y"), lax.axis_index("c")
        me, sibling = (x, y, c), (x, y, 1 - c)
        chips = [(1 - x, y), (x, 1 - y), (1 - x, 1 - y)]

        def slot(px, py, pc):
            return out_ref.at[4 * px + 2 * py + pc]

        def copy(k, block, to, src=None):
            return pltpu.make_async_remote_copy(
                src_ref=slot(*block) if src is None else src, dst_ref=slot(*block),
                send_sem=send_sems.at[k], recv_sem=recv_sems.at[k], device_id=to, device_id_type=_MESH)

        mine = pltpu.make_async_copy(x_ref, slot(*me), local_sem)
        mine.start()
        first = [copy(0, me, sibling, src=x_ref)]
        first += [copy(1 + j, me, (*chip, c), src=x_ref) for j, chip in enumerate(chips)]
        for cp in first:
            cp.start()
        passed = [copy(4 + j, (*chip, c), sibling) for j, chip in enumerate(chips)]
        for j, chip in enumerate(chips):
            copy(1 + j, (*chip, c), me).wait_recv()
            passed[j].start()
        copy(0, sibling, me).wait_recv()
        for j, chip in enumerate(chips):
            copy(4 + j, (*chip, 1 - c), me).wait_recv()
        for cp in first + passed:
            cp.wait_send()
        mine.wait()

    return pl.pallas_call(
        body, name="gather_shards",
        out_shape=_sds((N_DEV, rows, lanes), packed.dtype),
        in_specs=[pl.BlockSpec(memory_space=pltpu.VMEM)],
        out_specs=pl.BlockSpec(memory_space=pltpu.VMEM),
        scratch_shapes=[pltpu.SemaphoreType.DMA((7,)), pltpu.SemaphoreType.DMA((7,)), pltpu.SemaphoreType.DMA],
    )(packed)


SMALL_LANES = SMALL_N + LANES


def _exchange_grads(big_blocks, small_grads, loss_tile):
    nb = len(big_blocks)
    ns = len(small_grads)

    def body(*refs):
        big, small, loss_ref = refs[:nb], refs[nb:nb + ns], refs[nb + ns]
        rbig, rsmall = refs[nb + ns + 1:2 * nb + ns + 1], refs[2 * nb + ns + 1]
        send_b, recv_b, send_s, recv_s, local_sems, row_sc = refs[2 * nb + ns + 2:]
        x, y, c = lax.axis_index("x"), lax.axis_index("y"), lax.axis_index("c")
        me_lin = 4 * x + 2 * y + c
        mine = [pltpu.make_async_copy(big[j].at[me_lin], rbig[j].at[0], local_sems.at[j]) for j in range(nb)]
        for cp in mine:
            cp.start()
        off = 0
        for ref, (_, cnt) in zip(small, SMALL):
            row_sc[:, off:off + cnt] = ref[...]
            off += cnt
        row_sc[:, off:off + LANES] = loss_ref[0:1, :]
        rsmall[me_lin] = row_sc[...]
        copies = []
        for k in range(1, N_DEV):
            px, py, pc = x ^ (k >> 2), y ^ ((k >> 1) & 1), c ^ (k & 1)
            peer = (px, py, pc)
            for j in range(nb):
                copies.append(pltpu.make_async_remote_copy(
                    src_ref=big[j].at[4 * px + 2 * py + pc], dst_ref=rbig[j].at[k],
                    send_sem=send_b.at[k - 1, j], recv_sem=recv_b.at[k - 1, j], device_id=peer, device_id_type=_MESH))
            copies.append(pltpu.make_async_remote_copy(
                src_ref=row_sc, dst_ref=rsmall.at[me_lin],
                send_sem=send_s.at[k - 1], recv_sem=recv_s.at[k - 1], device_id=peer, device_id_type=_MESH))
        for cp in copies:
            cp.start()
        for cp in copies:
            cp.wait_recv()
        for cp in copies:
            cp.wait_send()
        for cp in mine:
            cp.wait()

    hbm, vmem = pl.BlockSpec(memory_space=pl.ANY), pl.BlockSpec(memory_space=pltpu.VMEM)
    return pl.pallas_call(
        body, name="exchange_grads",
        out_shape=[_sds(a.shape, a.dtype) for a in big_blocks] + [_sds((N_DEV, 1, SMALL_LANES))],
        in_specs=[hbm] * nb + [vmem] * (ns + 1),
        out_specs=[hbm] * nb + [vmem],
        scratch_shapes=[pltpu.SemaphoreType.DMA((N_DEV - 1, nb)), pltpu.SemaphoreType.DMA((N_DEV - 1, nb)),
                        pltpu.SemaphoreType.DMA((N_DEV - 1,)), pltpu.SemaphoreType.DMA((N_DEV - 1,)),
                        pltpu.SemaphoreType.DMA((nb,)), pltpu.VMEM((1, SMALL_LANES), F32)],
    )(*big_blocks, *small_grads, loss_tile)


def _adamw_math(w, g, m, v):
    m = ADAM_B1 * m + (1.0 - ADAM_B1) * g
    v = ADAM_B2 * v + (1.0 - ADAM_B2) * (g * g)
    m_hat = m / (1.0 - ADAM_B1 ** ADAM_STEP)
    v_hat = v / (1.0 - ADAM_B2 ** ADAM_STEP)
    return -ADAM_LR * (m_hat / (jnp.sqrt(v_hat) + ADAM_EPS) + ADAM_WD * w), m, v


def _reduce_adamw(name, parts, w, m, v, row_blocks):
    _, rows, cols = parts.shape
    rb = rows // row_blocks

    def body(p_ref, w_ref, m_ref, v_ref, g_out, d_out, m_out, v_out):
        g = p_ref[0].astype(F32)
        for s in range(1, N_DEV):
            g = g + p_ref[s].astype(F32)
        g_out[0] = g
        d_out[0], m_out[0], v_out[0] = _adamw_math(w_ref[0], g, m_ref[0], v_ref[0])

    blk = pl.BlockSpec((1, rb, cols), lambda i: (0, i, 0))
    return pl.pallas_call(
        body, name="reduce_adamw_" + name, grid=(row_blocks,),
        in_specs=[pl.BlockSpec((N_DEV, rb, cols), lambda i: (0, i, 0)), blk, blk, blk],
        out_specs=[blk] * 4, out_shape=[_sds((1, rows, cols))] * 4,
        compiler_params=_ARB1,
    )(parts, w, m, v)


def _reduce_adamw_small(rows, ws, ms, vs):
    ns = len(SMALL)

    def body(r_ref, *refs):
        w_refs, m_refs, v_refs, outs = refs[:ns], refs[ns:2 * ns], refs[2 * ns:3 * ns], refs[3 * ns:]
        total = r_ref[0]
        for s in range(1, N_DEV):
            total = total + r_ref[s]
        off = 0
        for j, (_, cnt) in enumerate(SMALL):
            g = total[:, off:off + cnt]
            off += cnt
            outs[4 * j][...] = g
            outs[4 * j + 1][...], outs[4 * j + 2][...], outs[4 * j + 3][...] = _adamw_math(
                w_refs[j][...], g, m_refs[j][...], v_refs[j][...])
        outs[4 * ns][...] = total[:, off:off + LANES]

    vmem = pl.BlockSpec(memory_space=pltpu.VMEM)
    return pl.pallas_call(
        body, name="reduce_adamw_small",
        in_specs=[vmem] * (1 + 3 * ns), out_specs=[vmem] * (4 * ns + 1),
        out_shape=[_sds((1, cnt)) for _, cnt in SMALL for _ in range(4)] + [_sds((1, LANES))],
    )(rows, *ws, *ms, *vs)


def _pack_rows(arrs):
    return jnp.concatenate([a.reshape(-1, LANES) for a in arrs], axis=0)


def _shard_blocks(name, full):
    a = full[0]
    rows, cols = a.shape
    if name == "w_out":
        return a.reshape(N_DEV, rows // N_DEV, cols)
    return a.reshape(rows, N_DEV, cols // N_DEV).transpose(1, 0, 2)


def _unshard(name, blocks, shard_shape):
    _, rows, cols = shard_shape
    a = blocks.reshape(N_DEV, rows, cols)
    if name == "w_out":
        return a.reshape(1, N_DEV * rows, cols)
    return a.transpose(1, 0, 2).reshape(1, rows, N_DEV * cols)


def kernel(x, positions, norm_pre_g, w_in, mla_q_norm_g, mla_w_uq, mla_kv_norm_g, mla_w_ukv, rw_mu, rw_w0, rw_w2, rw_a0, rw_a2, rw_k_k, rw_k_a, rw_r_k, rw_ln_g, rw_ln_b, w_out, norm_post_g, loss_target, m_norm_pre_g, m_w_in, m_mla_q_norm_g, m_mla_w_uq, m_mla_kv_norm_g, m_mla_w_ukv, m_rw_mu, m_rw_w0, m_rw_w2, m_rw_a0, m_rw_a2, m_rw_k_k, m_rw_k_a, m_rw_r_k, m_rw_ln_g, m_rw_ln_b, m_w_out, m_norm_post_g, v_norm_pre_g, v_w_in, v_mla_q_norm_g, v_mla_w_uq, v_mla_kv_norm_g, v_mla_w_ukv, v_rw_mu, v_rw_w0, v_rw_w2, v_rw_a0, v_rw_a2, v_rw_k_k, v_rw_k_a, v_rw_r_k, v_rw_ln_g, v_rw_ln_b, v_w_out, v_norm_post_g):
    given = dict(locals())
    w = {nm: given[nm] for nm in WEIGHTS}
    mom = {nm: given["m_" + nm] for nm in WEIGHTS}
    var = {nm: given["v_" + nm] for nm in WEIGHTS}
    sharded = [nm for nm, _ in SHARD_ROWS]

    gathered = _gather_shards(_pack_rows([w[nm] for nm in sharded]).astype(BF16))
    full, off = dict(w), 0
    for nm, cnt in SHARD_ROWS:
        full[nm] = _unshard(nm, gathered[:, off:off + cnt], w[nm].shape)
        off += cnt

    loss_part, grad_x, grads = _local_step(x, positions, loss_target, full)

    small_names = [nm for nm, _ in SMALL]
    row = lambda a: a.reshape(1, -1)
    got = _exchange_grads([_shard_blocks(nm, grads[nm]).astype(BF16) for nm in sharded],
                          [row(grads[nm]) for nm in small_names], loss_part)
    new = {}
    for nm, parts in zip(sharded, got[:-1]):
        new[nm] = _reduce_adamw(nm, parts, w[nm], mom[nm], var[nm], 4 if nm == "w_in" else 1)
    res = _reduce_adamw_small(got[-1], [row(w[nm]) for nm in small_names], [row(mom[nm]) for nm in small_names],
                              [row(var[nm]) for nm in small_names])
    for j, nm in enumerate(small_names):
        new[nm] = tuple(a.reshape(w[nm].shape) for a in res[4 * j:4 * j + 4])
    loss = res[-1][0, 0]
    return (loss, grad_x, *[new[nm][j] for j in range(4) for nm in WEIGHTS])
```

```python
import functools

import jax
import jax.numpy as jnp
from jax import lax
from jax.experimental import pallas as pl
from jax.experimental.pallas import tpu as pltpu

F32 = jnp.float32
BF16 = jnp.bfloat16

D_MODEL = 1024
MLA_HEADS = 4
MLA_NOPE = 128
MLA_ROPE = 64
MLA_V = 128
MLA_WIDTH = MLA_HEADS * MLA_V
Q_LORA = 256
KV_LORA = 128
ROPE_THETA = 10000.0
RW_HEAD = 64
RW_WIDTH = 512
RW_HEADS = RW_WIDTH // RW_HEAD
LORA = 64
RW_COLS = 3 * RW_WIDTH + 2 * LORA
MLA_COLS = Q_LORA + KV_LORA + MLA_ROPE
D_IN = MLA_COLS + RW_COLS + D_MODEL
RW_GN_EPS = 64e-5
NORM_EPS = 1e-6
ATT_SCALE = (MLA_NOPE + MLA_ROPE) ** -0.5
ADAM_LR, ADAM_B1, ADAM_B2, ADAM_EPS, ADAM_WD, ADAM_STEP = 0.001, 0.9, 0.999, 1e-08, 0.01, 10
N_DEV = 8
LANES = 128
MXU = 256

PM_W = 512
WP_COLS = PM_W + RW_COLS + D_MODEL
RW_PIECES = ((0, 512), (512, 1024), (1024, 1536), (1536, 1664))

SHARD_ROWS = (("w_in", 1024 * 392 // LANES), ("mla_w_uq", 256 * 96 // LANES), ("mla_w_ukv", 128 * 128 // LANES),
              ("rw_w2", 64 * 64 // LANES), ("rw_a2", 64 * 64 // LANES), ("w_out", 128 * 1024 // LANES))
PACK_ROWS = sum(r for _, r in SHARD_ROWS)
SMALL = (("norm_pre_g", 1024), ("mla_q_norm_g", 256), ("mla_kv_norm_g", 128), ("rw_mu", 1664), ("rw_w0", 512),
         ("rw_a0", 512), ("rw_k_k", 512), ("rw_k_a", 512), ("rw_r_k", 512), ("rw_ln_g", 512), ("rw_ln_b", 512),
         ("norm_post_g", 1024))
SMALL_N = sum(n for _, n in SMALL)
SMALL_ROWS = 72
WEIGHTS = ("norm_pre_g", "w_in", "mla_q_norm_g", "mla_w_uq", "mla_kv_norm_g", "mla_w_ukv", "rw_mu", "rw_w0", "rw_w2",
           "rw_a0", "rw_a2", "rw_k_k", "rw_k_a", "rw_r_k", "rw_ln_g", "rw_ln_b", "w_out", "norm_post_g")


def _seg_ones():
    r = lax.broadcasted_iota(jnp.int32, (MXU, MXU), 0) >> 6
    c = lax.broadcasted_iota(jnp.int32, (MXU, MXU), 1) >> 6
    return jnp.where(r == c, 1.0, 0.0).astype(BF16)


def _seg_dot(x, ones, passes):
    parts, rem = [], x
    for p in range(passes):
        hb = rem.astype(BF16)
        parts.append(hb)
        if p + 1 < passes:
            rem = rem - hb.astype(F32)
    outs = []
    for j in range(x.shape[1] // MXU):
        acc = None
        for hb in parts:
            d = jnp.dot(hb[:, MXU * j:MXU * (j + 1)], ones, preferred_element_type=F32)
            acc = d if acc is None else acc + d
        outs.append(acc)
    return outs[0] if len(outs) == 1 else jnp.concatenate(outs, axis=1)


def _seg_multi(xs, ones, passes):
    his = [x.astype(BF16) for x in xs]
    hi = jnp.concatenate(his, axis=0)
    if passes == 2:
        lo = jnp.concatenate([(x - h.astype(F32)).astype(BF16) for x, h in zip(xs, his)], axis=0)
        rhs = jnp.concatenate([ones, ones], axis=0)
    halves = []
    for j in range(hi.shape[1] // MXU):
        sl = slice(MXU * j, MXU * (j + 1))
        if passes == 2:
            halves.append(jnp.dot(jnp.concatenate([hi[:, sl], lo[:, sl]], axis=1), rhs, preferred_element_type=F32))
        else:
            halves.append(jnp.dot(hi[:, sl], ones, preferred_element_type=F32))
    full = jnp.concatenate(halves, axis=1)
    m = xs[0].shape[0]
    return [full[m * i:m * (i + 1)] for i in range(len(xs))]


@jax.custom_vjp
def _segsum(x):
    return _seg_dot(x, _seg_ones(), 3)


_segsum.defvjp(lambda x: (_segsum(x), None), lambda _, g: (_segsum(g),))


@jax.custom_vjp
def _bdot(a, w):
    return jnp.dot(a.astype(BF16), w.astype(BF16), preferred_element_type=F32)


def _bdot_fwd(a, w):
    return _bdot(a, w), (a, w)


def _bdot_bwd(res, g):
    a, w = res
    gb = g.astype(BF16)
    da = lax.dot_general(gb, w.astype(BF16), (((1,), (1,)), ((), ())), preferred_element_type=F32)
    dw = lax.dot_general(a.astype(BF16), gb, (((0,), (0,)), ((), ())), preferred_element_type=F32)
    return da, dw


_bdot.defvjp(_bdot_fwd, _bdot_bwd)


def _rot_impl(x):
    w = x.shape[1]
    lane = lax.broadcasted_iota(jnp.int32, x.shape, 1)
    return jnp.where((lane & 63) < 32, -pltpu.roll(x, w - 32, 1), pltpu.roll(x, 32, 1))


@jax.custom_vjp
def _rot(x):
    return _rot_impl(x)


_rot.defvjp(lambda x: (_rot_impl(x), None), lambda _, g: (-_rot_impl(g),))


def _rms(x, g):
    return x * lax.rsqrt(jnp.mean(x * x, axis=-1, keepdims=True) + NORM_EPS) * g


def _shift_rows(p, prev_row):
    row = lax.broadcasted_iota(jnp.int32, p.shape, 0)
    return jnp.where(row == 0, prev_row, pltpu.roll(p, 1, 0))


def _unshift_rows(g, next_row):
    row = lax.broadcasted_iota(jnp.int32, g.shape, 0)
    return jnp.where(row == g.shape[0] - 1, next_row, pltpu.roll(g, g.shape[0] - 1, 0))


def _f_mla(cq, ckv, kr, cos, sin, g_q, wqn, wqr, g_kv, wkv):
    qn = _rms(cq, g_q)
    q_nope = _bdot(qn, wqn)
    q_r = _bdot(qn, wqr)
    cos4 = jnp.concatenate([cos] * MLA_HEADS, axis=1)
    sin4 = jnp.concatenate([sin] * MLA_HEADS, axis=1)
    q_rope = q_r * cos4 + _rot(q_r) * sin4
    kv = _bdot(_rms(ckv, g_kv), wkv)
    k_rope = kr * cos + _rot(kr) * sin
    return q_nope, q_rope, kv, k_rope


def _f_rw(pr, pk, pv, pt, sr, sk, sv, st, mu_r, mu_k, mu_v, mu_t, w0, w2p, a0, a2p, k_k, k_a):
    r = pr + (sr - pr) * mu_r
    k = pk + (sk - pk) * mu_k
    v = pv + (sv - pv) * mu_v
    t = pt + (st - pt) * mu_t
    nwl = -(w0 + _bdot(jnp.tanh(t), w2p))
    softplus = jnp.maximum(nwl, 0.0) + jnp.log(1.0 + jnp.exp(-jnp.abs(nwl)))
    decay = jnp.exp(-jnp.exp(-softplus - 0.5))
    a = jax.nn.sigmoid(a0 + _bdot(t, a2p))
    kk = k * k_k
    kk = kk / jnp.maximum(jnp.sqrt(_segsum(kk * kk)), 1e-12)
    k2 = k * (1.0 + (a - 1.0) * k_a)
    return r, decay, k2, v, -kk, kk * a


def _f_head(ys, r, k, v, ym, z1, z2, x, tgt, ln_g, ln_b, r_k, w1, w2, g_post):
    inv = 1.0 / RW_HEAD
    yc = ys - _segsum(ys) * inv
    var = _segsum(yc * yc) * inv
    y = yc * lax.rsqrt(var + RW_GN_EPS) * ln_g + ln_b
    y_rw = y + _segsum(r * k * r_k) * v
    c1 = ym * (z1 * jax.nn.sigmoid(z1))
    c2 = y_rw * (z2 * jax.nn.sigmoid(z2))
    out = _bdot(c1, w1) + _bdot(c2, w2)
    err = x + _rms(out, g_post) - tgt
    per_row = jnp.sum(err * err, axis=1, keepdims=True)
    return jnp.sum(per_row, axis=0, keepdims=True) * (0.5 / D_MODEL)


def _rows(tm, width):
    return pl.BlockSpec((tm, width), lambda i: (i, 0))


def _whole(shape):
    zeros = (0,) * len(shape)
    return pl.BlockSpec(shape, lambda i: zeros)


def _sds(shape, dtype=F32):
    return jax.ShapeDtypeStruct(shape, dtype)


_ARB1 = pltpu.CompilerParams(dimension_semantics=("arbitrary",))


def _acc(ref, val, first):
    @pl.when(first)
    def _():
        ref[...] = val

    @pl.when(jnp.logical_not(first))
    def _():
        ref[...] += val


def _fwd_a(x2, g_pre, wp, tm):
    n = x2.shape[0]

    def body(x_ref, g_ref, w_ref, u_ref, pm_ref, prw_ref, z_ref):
        ub = _rms(x_ref[...], g_ref[...]).astype(BF16)
        u_ref[...] = ub
        pm_ref[...] = jnp.dot(ub, w_ref[:, 0:PM_W], preferred_element_type=F32)
        prw_ref[...] = jnp.dot(ub, w_ref[:, PM_W:PM_W + RW_COLS], preferred_element_type=F32)
        z_ref[...] = jnp.dot(ub, w_ref[:, PM_W + RW_COLS:WP_COLS], preferred_element_type=F32)

    return pl.pallas_call(
        body, name="fwd_a", grid=(n // tm,),
        in_specs=[_rows(tm, D_MODEL), _whole((1, D_MODEL)), _whole((D_MODEL, WP_COLS))],
        out_specs=[_rows(tm, D_MODEL), _rows(tm, PM_W), _rows(tm, RW_COLS), _rows(tm, D_MODEL)],
        out_shape=[_sds((n, D_MODEL), BF16), _sds((n, PM_W)), _sds((n, RW_COLS)), _sds((n, D_MODEL))],
        compiler_params=_ARB1,
    )(x2, g_pre, wp)


def _rope_tables(pos_row, invf_col, tm):
    n = pos_row.shape[1]

    def body(p_ref, f_ref, c_ref, s_ref):
        ang = f_ref[...] * p_ref[...].astype(F32)
        c_ref[...] = jnp.cos(ang).T
        s_ref[...] = jnp.sin(ang).T

    return pl.pallas_call(
        body, name="rope_tables", grid=(n // tm,),
        in_specs=[pl.BlockSpec((1, tm), lambda i: (0, i)), _whole((LANES, 1))],
        out_specs=[_rows(tm, LANES), _rows(tm, LANES)],
        out_shape=[_sds((n, LANES)), _sds((n, LANES))],
        compiler_params=_ARB1,
    )(pos_row, invf_col)


_B_WEIGHT_SHAPES = ((1, Q_LORA), (Q_LORA, 512), (Q_LORA, 512), (1, KV_LORA), (KV_LORA, 1024), (1, RW_COLS), (1, RW_WIDTH),
                    (LANES, RW_WIDTH), (1, RW_WIDTH), (LANES, RW_WIDTH), (1, RW_WIDTH), (1, RW_WIDTH))


def _halo_prev(tm):
    return pl.BlockSpec((8, RW_COLS), lambda i: (jnp.maximum(i * (tm // 8) - 1, 0), 0))


def _b_operands(pm_ref, prw_ref, halo_ref, wrefs, tile, tiles_per_seq):
    g_q, wqn, wqr, g_kv, wkv, mu, w0, w2p, a0, a2p, k_k, k_a = wrefs
    mla_in = (pm_ref[:, 0:Q_LORA], pm_ref[:, Q_LORA:Q_LORA + KV_LORA], pm_ref[:, Q_LORA + KV_LORA:PM_W])
    mla_w = (g_q[...], wqn[...], wqr[...], g_kv[...], wkv[...])
    keep = jnp.where(tile % tiles_per_seq == 0, 0.0, 1.0)
    prev = halo_ref[7:8, :] * keep
    ps = tuple(prw_ref[:, a:b] for a, b in RW_PIECES)
    ss = tuple(_shift_rows(p, prev[:, a:b]) for p, (a, b) in zip(ps, RW_PIECES))
    rw_w = tuple(mu[:, a:b] for a, b in RW_PIECES) + (w0[...], w2p[...], a0[...], a2p[...], k_k[...], k_a[...])
    return mla_in, mla_w, ps + ss, rw_w


def _fwd_b(pm, prw, cos, sin, bw, tm, tiles_per_seq):
    n = pm.shape[0]

    def body(pm_ref, prw_ref, halo_ref, cos_ref, sin_ref, *refs):
        wrefs, outs = refs[:12], refs[12:]
        mla_in, mla_w, rw_in, rw_w = _b_operands(pm_ref, prw_ref, halo_ref, wrefs, pl.program_id(0), tiles_per_seq)
        res = _f_mla(*mla_in, cos_ref[...], sin_ref[...], *mla_w) + _f_rw(*rw_in, *rw_w)
        for o_ref, val in zip(outs, res):
            o_ref[...] = val.astype(o_ref.dtype)

    widths = (512, 512, 1024, LANES) + (RW_WIDTH,) * 6
    return pl.pallas_call(
        body, name="fwd_b", grid=(n // tm,),
        in_specs=[_rows(tm, PM_W), _rows(tm, RW_COLS), _halo_prev(tm), _rows(tm, LANES), _rows(tm, LANES)]
        + [_whole(s) for s in _B_WEIGHT_SHAPES],
        out_specs=[_rows(tm, w) for w in widths],
        out_shape=[_sds((n, w), BF16 if j < 4 else F32) for j, w in enumerate(widths)],
        compiler_params=_ARB1,
    )(pm, prw, prw, cos, sin, *bw)


def _bwd_b(pm, prw, cos, sin, bw, cts, dkr_heads, tm, tiles_per_seq):
    n = pm.shape[0]

    ct_widths = (512, 512, 1024) + (RW_WIDTH,) * 9
    n_ct = len(ct_widths)

    def body(pm_ref, prw_ref, halo_ref, cos_ref, sin_ref, *refs):
        wrefs, ct_refs, dkr_ref = refs[:12], refs[12:12 + n_ct], refs[12 + n_ct]
        dpm_ref, dprw_ref, dps_ref = refs[13 + n_ct:16 + n_ct]
        wg_refs = refs[16 + n_ct:]
        tile = pl.program_id(0)
        first = tile == 0
        mla_in, mla_w, rw_in, rw_w = _b_operands(pm_ref, prw_ref, halo_ref, wrefs, tile, tiles_per_seq)
        cos, sin = cos_ref[...], sin_ref[...]
        ct = [r[...] for r in ct_refs]
        _, vjp_mla = jax.vjp(lambda *a: _f_mla(*a[:3], cos, sin, *a[3:]), *mla_in, *mla_w)
        dkr = dkr_ref[0] + dkr_ref[1] + dkr_ref[2] + dkr_ref[3]
        d_mla = vjp_mla((ct[0], ct[1], ct[2], dkr))
        dpm_ref[:, 0:Q_LORA] = d_mla[0]
        dpm_ref[:, Q_LORA:Q_LORA + KV_LORA] = d_mla[1]
        dpm_ref[:, Q_LORA + KV_LORA:PM_W] = d_mla[2]
        _, vjp_rw = jax.vjp(_f_rw, *rw_in, *rw_w)
        d_rw = vjp_rw((ct[3] + ct[4], ct[5], ct[6] + ct[7], ct[8] + ct[9], ct[10], ct[11]))
        for j, (a, b) in enumerate(RW_PIECES):
            dprw_ref[:, a:b] = d_rw[j]
            dps_ref[:, a:b] = d_rw[4 + j]
        g_q, wqn, wqr, g_kv, wkv, mu, w0, w2p, a0, a2p, k_k, k_a = wg_refs
        for ref, val in zip((g_q, wqn, wqr, g_kv, wkv), d_mla[3:]):
            _acc(ref, val, first)
        for j, (a, b) in enumerate(RW_PIECES):
            _acc(mu.at[:, a:b], d_rw[8 + j], first)
        for ref, val in zip((w0, w2p, a0, a2p, k_k, k_a), d_rw[12:]):
            _acc(ref, val, first)

    return pl.pallas_call(
        body, name="bwd_b", grid=(n // tm,),
        in_specs=[_rows(tm, PM_W), _rows(tm, RW_COLS), _halo_prev(tm), _rows(tm, LANES), _rows(tm, LANES)]
        + [_whole(s) for s in _B_WEIGHT_SHAPES] + [_rows(tm, w) for w in ct_widths]
        + [pl.BlockSpec((MLA_HEADS, tm, LANES), lambda i: (0, i, 0))],
        out_specs=[_rows(tm, PM_W), _rows(tm, RW_COLS), _rows(tm, RW_COLS)] + [_whole(s) for s in _B_WEIGHT_SHAPES],
        out_shape=[_sds((n, PM_W)), _sds((n, RW_COLS)), _sds((n, RW_COLS))] + [_sds(s) for s in _B_WEIGHT_SHAPES],
        compiler_params=_ARB1,
    )(pm, prw, prw, cos, sin, *bw, *cts, dkr_heads)


def _head(ys, r, k, v, ym, z, x2, tgt, hw, tm):
    n = x2.shape[0]
    h_shapes = ((1, RW_WIDTH), (1, RW_WIDTH), (1, RW_WIDTH), (D_MODEL, D_MODEL), (1, D_MODEL))

    def body(ys_ref, r_ref, k_ref, v_ref, ym_ref, z_ref, x_ref, t_ref, lng, lnb, rk, wout, gpost,
             dys_ref, dr_ref, dk_ref, dv_ref, dym_ref, dz_ref, dx_ref, loss_ref, dlng, dlnb, drk, dwout, dgpost):
        first = pl.program_id(0) == 0
        tgt_v = t_ref[...]
        args = (ys_ref[...], r_ref[...], k_ref[...], v_ref[...], ym_ref[...], z_ref[:, 0:MLA_WIDTH], z_ref[:, MLA_WIDTH:D_MODEL],
                x_ref[...], lng[...], lnb[...], rk[...], wout[0:MLA_WIDTH, :], wout[MLA_WIDTH:D_MODEL, :], gpost[...])
        loss, vjp = jax.vjp(lambda *a: _f_head(*a[:8], tgt_v, *a[8:]), *args)
        d = vjp(jnp.ones((1, 1), F32))
        dys_ref[...] = d[0]
        dr_ref[...] = d[1]
        dk_ref[...] = d[2]
        dv_ref[...] = d[3]
        dym_ref[...] = d[4].astype(BF16)
        dz_ref[:, 0:MLA_WIDTH] = d[5]
        dz_ref[:, MLA_WIDTH:D_MODEL] = d[6]
        dx_ref[...] = d[7]
        _acc(loss_ref, jnp.broadcast_to(loss, (8, LANES)), first)
        _acc(dlng, d[8], first)
        _acc(dlnb, d[9], first)
        _acc(drk, d[10], first)
        _acc(dwout.at[0:MLA_WIDTH, :], d[11], first)
        _acc(dwout.at[MLA_WIDTH:D_MODEL, :], d[12], first)
        _acc(dgpost, d[13], first)

    widths = (RW_WIDTH,) * 4 + (MLA_WIDTH, D_MODEL, D_MODEL)
    return pl.pallas_call(
        body, name="head", grid=(n // tm,),
        in_specs=[_rows(tm, RW_WIDTH)] * 4 + [_rows(tm, MLA_WIDTH), _rows(tm, D_MODEL), _rows(tm, D_MODEL), _rows(tm, D_MODEL)]
        + [_whole(s) for s in h_shapes],
        out_specs=[_rows(tm, w) for w in widths] + [_whole((8, LANES))] + [_whole(s) for s in h_shapes],
        out_shape=[_sds((n, w), BF16 if j == 4 else F32) for j, w in enumerate(widths)] + [_sds((8, LANES))]
        + [_sds(s) for s in h_shapes],
        compiler_params=_ARB1,
    )(ys, r, k, v, ym, z, x2, tgt, *hw)


def _halo_next(tm, n):
    last = n // 8 - 1
    return pl.BlockSpec((8, RW_COLS), lambda i: (jnp.minimum((i + 1) * (tm // 8), last), 0))


def _bwd_a(x2, g_pre, wp, dpm, dprw, dps, dz, dxres, tm, tiles_per_seq):
    n = x2.shape[0]
    nt_dims = (((1,), (1,)), ((), ()))

    def body(x_ref, g_ref, w_ref, dpm_ref, dprw_ref, dps_ref, nxt_ref, dz_ref, dxres_ref, gx_ref, dpb_ref, dg_ref):
        tile = pl.program_id(0)
        keep = jnp.where((tile + 1) % tiles_per_seq == 0, 0.0, 1.0)
        dprw_v = dprw_ref[...] + _unshift_rows(dps_ref[...], nxt_ref[0:1, :] * keep)
        dpm_b, dprw_b, dz_b = dpm_ref[...].astype(BF16), dprw_v.astype(BF16), dz_ref[...].astype(BF16)
        dpb_ref[:, 0:PM_W] = dpm_b
        dpb_ref[:, PM_W:PM_W + RW_COLS] = dprw_b
        dpb_ref[:, PM_W + RW_COLS:WP_COLS] = dz_b
        du = (lax.dot_general(dpm_b, w_ref[:, 0:PM_W], nt_dims, preferred_element_type=F32)
              + lax.dot_general(dprw_b, w_ref[:, PM_W:PM_W + RW_COLS], nt_dims, preferred_element_type=F32)
              + lax.dot_general(dz_b, w_ref[:, PM_W + RW_COLS:WP_COLS], nt_dims, preferred_element_type=F32))
        x = x_ref[...]
        xhat = x * lax.rsqrt(jnp.mean(x * x, axis=-1, keepdims=True) + NORM_EPS)
        dxn = du * g_ref[...]
        dx = (dxn - xhat * jnp.mean(dxn * xhat, axis=-1, keepdims=True)) * lax.rsqrt(jnp.mean(x * x, axis=-1, keepdims=True) + NORM_EPS)
        gx_ref[...] = dx + dxres_ref[...]
        _acc(dg_ref, jnp.sum(du * xhat, axis=0, keepdims=True), tile == 0)

    return pl.pallas_call(
        body, name="bwd_a", grid=(n // tm,),
        in_specs=[_rows(tm, D_MODEL), _whole((1, D_MODEL)), _whole((D_MODEL, WP_COLS)), _rows(tm, PM_W), _rows(tm, RW_COLS),
                  _rows(tm, RW_COLS), _halo_next(tm, n), _rows(tm, D_MODEL), _rows(tm, D_MODEL)],
        out_specs=[_rows(tm, D_MODEL), _rows(tm, WP_COLS), _whole((1, D_MODEL))],
        out_shape=[_sds((n, D_MODEL)), _sds((n, WP_COLS), BF16), _sds((1, D_MODEL))],
        compiler_params=_ARB1,
    )(x2, g_pre, wp, dpm, dprw, dps, dps, dz, dxres)


def _dw_in(u, dpb, tk, tn):
    n = u.shape[0]
    steps = n // tk

    def body(u_ref, d_ref, o_ref, acc_sc):
        k = pl.program_id(1)
        _acc(acc_sc, lax.dot_general(u_ref[...], d_ref[...], _TN, preferred_element_type=F32), k == 0)

        @pl.when(k == steps - 1)
        def _():
            o_ref[...] = acc_sc[...].astype(BF16)

    return pl.pallas_call(
        body, name="dw_in", grid=(WP_COLS // tn, steps),
        in_specs=[pl.BlockSpec((tk, D_MODEL), lambda j, k: (k, 0)), pl.BlockSpec((tk, tn), lambda j, k: (k, j))],
        out_specs=pl.BlockSpec((D_MODEL, tn), lambda j, k: (0, j)),
        out_shape=_sds((D_MODEL, WP_COLS), BF16),
        scratch_shapes=[pltpu.VMEM((D_MODEL, tn), F32)],
        compiler_params=pltpu.CompilerParams(dimension_semantics=("arbitrary", "arbitrary")),
    )(u, dpb)


ATT_BLK = 256
_NT = (((1,), (1,)), ((), ()))
_TN = (((0,), (0,)), ((), ()))


def _causal(q0, k0, blk):
    row = q0 + lax.broadcasted_iota(jnp.int32, (blk, blk), 0)
    col = k0 + lax.broadcasted_iota(jnp.int32, (blk, blk), 1)
    return row >= col


def _attn_fwd(qn, qr, kv, kr):
    bsz, t, _ = qn.shape
    blk = min(ATT_BLK, t)

    heads = range(MLA_HEADS)

    def body(qn_ref, qr_ref, kv_ref, kr_ref, o_ref, lse_ref):
        qi = pl.program_id(1)
        q = [jnp.concatenate([qn_ref[:, LANES * h:LANES * (h + 1)], qr_ref[:, LANES * h:LANES * (h + 1)]], axis=1) for h in heads]
        lower = _causal(0, 0, blk)

        def kv_step(j, carry, diagonal):
            ks = pl.multiple_of(j * blk, blk)
            k_rope = kr_ref[pl.ds(ks, blk), :]
            out = []
            for h in heads:
                m, l, acc = carry[h]
                k = jnp.concatenate([kv_ref[pl.ds(ks, blk), 2 * LANES * h:2 * LANES * h + LANES], k_rope], axis=1)
                s = lax.dot_general(q[h], k, _NT, preferred_element_type=F32) * ATT_SCALE
                if diagonal:
                    s = jnp.where(lower, s, -1e30)
                m_new = jnp.maximum(m, jnp.max(s, axis=1, keepdims=True))
                alpha = jnp.exp(m - m_new)
                p = jnp.exp(s - m_new)
                l = alpha * l + jnp.sum(p, axis=1, keepdims=True)
                v = kv_ref[pl.ds(ks, blk), 2 * LANES * h + LANES:2 * LANES * (h + 1)]
                acc = alpha * acc + jnp.dot(p.astype(BF16), v, preferred_element_type=F32)
                out.append((m_new, l, acc))
            return tuple(out)

        one = (jnp.full((blk, 1), -1e30, F32), jnp.zeros((blk, 1), F32), jnp.zeros((blk, MLA_V), F32))
        carry = lax.fori_loop(0, qi, lambda j, c: kv_step(j, c, False), (one,) * MLA_HEADS)
        carry = kv_step(qi, carry, True)
        for h in heads:
            m, l, acc = carry[h]
            o_ref[:, LANES * h:LANES * (h + 1)] = acc / l
            lse_ref[h] = jnp.broadcast_to(m + jnp.log(l), (blk, LANES))

    return pl.pallas_call(
        body, name="attn_fwd", grid=(bsz, t // blk),
        in_specs=[pl.BlockSpec((None, blk, MLA_WIDTH), lambda b, i: (b, i, 0)),
                  pl.BlockSpec((None, blk, MLA_WIDTH), lambda b, i: (b, i, 0)),
                  pl.BlockSpec((None, t, 2 * MLA_WIDTH), lambda b, i: (b, 0, 0)),
                  pl.BlockSpec((None, t, LANES), lambda b, i: (b, 0, 0))],
        out_specs=[pl.BlockSpec((None, blk, MLA_WIDTH), lambda b, i: (b, i, 0)),
                   pl.BlockSpec((None, MLA_HEADS, blk, LANES), lambda b, i: (b, 0, i, 0))],
        out_shape=[_sds((bsz, t, MLA_WIDTH)), _sds((bsz, MLA_HEADS, t, LANES))],
        compiler_params=pltpu.CompilerParams(dimension_semantics=("arbitrary", "arbitrary")),
    )(qn, qr, kv, kr)


def _attn_bwd(qn, qr, kv, kr, o, do, lse):
    bsz, t, _ = qn.shape
    blk = min(ATT_BLK, t)
    nb = t // blk

    def body(qn_ref, qr_ref, kn_ref, kr_ref, v_ref, o_ref, do_ref, lse_ref, dqn_ref, dqr_ref, dkv_ref, dkr_ref, dq_sc, delta_sc):
        dq_sc[...] = jnp.zeros_like(dq_sc)
        delta_sc[...] = jnp.sum(do_ref[...].astype(F32) * o_ref[...], axis=1, keepdims=True)

        def kv_loop(j, _):
            ks = pl.multiple_of(j * blk, blk)
            k = jnp.concatenate([kn_ref[pl.ds(ks, blk), :], kr_ref[pl.ds(ks, blk), :]], axis=1)
            vb = v_ref[pl.ds(ks, blk), :]

            def q_pair(pair, carry):
                dk, dv = carry
                for sub in range(2):
                    i = 2 * pair + sub
                    qs = pl.multiple_of(i * blk, blk)
                    q = jnp.concatenate([qn_ref[pl.ds(qs, blk), :], qr_ref[pl.ds(qs, blk), :]], axis=1)
                    dob = do_ref[pl.ds(qs, blk), :]
                    s = lax.dot_general(q, k, _NT, preferred_element_type=F32) * ATT_SCALE
                    p = jnp.where(_causal(i * blk, j * blk, blk), jnp.exp(s - lse_ref[pl.ds(qs, blk), 0:1]), 0.0)
                    dv = dv + lax.dot_general(p.astype(BF16), dob, _TN, preferred_element_type=F32)
                    dp = lax.dot_general(dob, vb, _NT, preferred_element_type=F32)
                    ds = (p * (dp - delta_sc[pl.ds(qs, blk), :]) * ATT_SCALE).astype(BF16)
                    dq_sc[pl.ds(qs, blk), :] += jnp.dot(ds, k, preferred_element_type=F32)
                    dk = dk + lax.dot_general(ds, q, _TN, preferred_element_type=F32)
                return dk, dv

            dk, dv = lax.fori_loop(j // 2, nb // 2, q_pair, (jnp.zeros((blk, 2 * LANES), F32), jnp.zeros((blk, MLA_V), F32)))
            dkv_ref[pl.ds(ks, blk), 0:LANES] = dk[:, 0:LANES]
            dkv_ref[pl.ds(ks, blk), LANES:2 * LANES] = dv
            dkr_ref[pl.ds(ks, blk), :] = dk[:, LANES:2 * LANES]
            return 0

        lax.fori_loop(0, nb, kv_loop, 0)
        dqn_ref[...] = dq_sc[:, 0:LANES]
        dqr_ref[...] = dq_sc[:, LANES:2 * LANES]

    head_col = lambda b, h: (b, 0, h)
    return pl.pallas_call(
        body, name="attn_bwd", grid=(bsz, MLA_HEADS),
        in_specs=[pl.BlockSpec((None, t, LANES), head_col), pl.BlockSpec((None, t, LANES), head_col),
                  pl.BlockSpec((None, t, LANES), lambda b, h: (b, 0, 2 * h)),
                  pl.BlockSpec((None, t, LANES), lambda b, h: (b, 0, 0)),
                  pl.BlockSpec((None, t, LANES), lambda b, h: (b, 0, 2 * h + 1)),
                  pl.BlockSpec((None, t, LANES), head_col), pl.BlockSpec((None, t, LANES), head_col),
                  pl.BlockSpec((None, None, t, LANES), lambda b, h: (b, h, 0, 0))],
        out_specs=[pl.BlockSpec((None, t, LANES), head_col), pl.BlockSpec((None, t, LANES), head_col),
                   pl.BlockSpec((None, t, 2 * LANES), head_col),
                   pl.BlockSpec((None, None, t, LANES), lambda b, h: (h, b, 0, 0))],
        out_shape=[_sds((bsz, t, MLA_WIDTH)), _sds((bsz, t, MLA_WIDTH)), _sds((bsz, t, 2 * MLA_WIDTH)),
                   _sds((MLA_HEADS, bsz, t, LANES))],
        scratch_shapes=[pltpu.VMEM((t, 2 * LANES), F32), pltpu.VMEM((t, 1), F32)],
        compiler_params=pltpu.CompilerParams(dimension_semantics=("arbitrary", "arbitrary")),
    )(qn, qr, kv, kr, kv, o, do, lse)


SCAN_CHUNK = 16


def _diag_mask():
    row = lax.broadcasted_iota(jnp.int32, (RW_HEAD, RW_WIDTH), 0)
    lane = lax.broadcasted_iota(jnp.int32, (RW_HEAD, RW_WIDTH), 1)
    return jnp.where(row == (lane & (RW_HEAD - 1)), 1.0, 0.0)


def _time_minor(a):
    bsz, t, _ = a.shape
    a = a.reshape(bsz, t // SCAN_CHUNK, SCAN_CHUNK, RW_HEADS, RW_HEAD)
    return a.transpose(0, 1, 4, 3, 2).reshape(bsz, t // SCAN_CHUNK, RW_HEAD, RW_HEADS * SCAN_CHUNK)


def _head_expand():
    l = lax.broadcasted_iota(jnp.int32, (2 * LANES, RW_WIDTH), 0)
    n = lax.broadcasted_iota(jnp.int32, (2 * LANES, RW_WIDTH), 1)
    return jnp.where(((l & (LANES - 1)) >> 4) == (n >> 6), 1.0, 0.0).astype(BF16)


BCAST_GROUP = 4


def _outer_chunk(tm_ref, row_ref, out_sc, expand, seqs):
    step_of_lane = lax.broadcasted_iota(jnp.int32, (RW_HEAD, LANES), 1) & (SCAN_CHUNK - 1)
    tiles = [tm_ref[bi, 0] for bi in seqs]
    for t0 in range(0, SCAN_CHUNK, BCAST_GROUP):
        parts = []
        for t in range(t0, t0 + BCAST_GROUP):
            for tile in tiles:
                a = jnp.where(step_of_lane == t, tile, 0.0)
                hi = a.astype(BF16)
                parts.append(jnp.concatenate([hi, (a - hi.astype(F32)).astype(BF16)], axis=1))
        cols = jnp.dot(jnp.concatenate(parts, axis=0), expand, preferred_element_type=F32)
        for j, t in enumerate(range(t0, t0 + BCAST_GROUP)):
            base = j * RW_HEAD * len(seqs)
            out_sc[t] = jnp.concatenate([cols[base + RW_HEAD * bi:base + RW_HEAD * (bi + 1)] * row_ref[bi, t:t + 1, :]
                                         for bi in seqs], axis=0)


def _fold8(x):
    acc = x[0:8]
    for j in range(1, x.shape[0] // 8):
        acc = acc + x[8 * j:8 * (j + 1)]
    return acc


def _rows8(at):
    return pl.ds(at * 8 if isinstance(at, int) else pl.multiple_of(at * 8, 8), 8)


def _put8(sc, bi, at, val):
    for j in range(RW_WIDTH // LANES):
        sc[bi * (RW_WIDTH // LANES) + j, _rows8(at), :] = val[:, LANES * j:LANES * (j + 1)]


def _unfold8(sc, bi, steps):
    tiles = []
    for j in range(RW_WIDTH // LANES):
        view = sc.at[bi * (RW_WIDTH // LANES) + j]
        acc = view[pl.ds(0, steps, stride=8), :]
        for s in range(1, 8):
            acc = acc + view[pl.ds(s, steps, stride=8), :]
        tiles.append(acc)
    return jnp.concatenate(tiles, axis=1)


def _scan_fwd(r, w, k, vt, nkk, b):
    bsz, t, _ = r.shape
    tc = SCAN_CHUNK

    def body(r_ref, w_ref, k_ref, n_ref, b_ref, vt_ref, y_ref, st_ref, s_sc, vc_sc, y_sc):
        @pl.when(pl.program_id(0) == 0)
        def _():
            s_sc[...] = jnp.zeros_like(s_sc)

        ones = _seg_ones()
        diag = _diag_mask()
        seqs = range(bsz)
        _outer_chunk(vt_ref, k_ref, vc_sc, _head_expand(), seqs)

        def put_y(ya, at):
            for bi in seqs:
                _put8(y_sc, bi, at, _fold8(ya[bi] * diag))

        def step(i, _):
            row = lambda ref, bi: ref[bi, pl.ds(i, 1), :]
            prev = jnp.maximum(i - 1, 0)
            s_old = [s_sc[bi] for bi in seqs]
            sa = _seg_multi([s_old[bi] * row(n_ref, bi) for bi in seqs], ones, 1)
            put_y(_seg_multi([s_old[bi] * r_ref[bi, pl.ds(prev, 1), :] for bi in seqs], ones, 1), prev)
            vk = vc_sc[i]
            for bi in seqs:
                s_new = s_old[bi] * row(w_ref, bi) + sa[bi] * row(b_ref, bi) + vk[RW_HEAD * bi:RW_HEAD * (bi + 1)]
                s_sc[bi] = s_new
                st_ref[bi, i] = s_new
            return 0

        lax.fori_loop(0, tc, step, 0, unroll=8)
        put_y(_seg_multi([s_sc[bi] * r_ref[bi, tc - 1:tc, :] for bi in seqs], ones, 1), tc - 1)
        for bi in seqs:
            y_ref[bi] = _unfold8(y_sc, bi, tc)

    vec = pl.BlockSpec((bsz, tc, RW_WIDTH), lambda c: (0, c, 0))
    return pl.pallas_call(
        body, name="scan_fwd", grid=(t // tc,),
        in_specs=[vec] * 5 + [pl.BlockSpec((bsz, 1, RW_HEAD, LANES), lambda c: (0, c, 0, 0))],
        out_specs=[vec, pl.BlockSpec((bsz, tc, RW_HEAD, RW_WIDTH), lambda c: (0, c, 0, 0))],
        out_shape=[_sds((bsz, t, RW_WIDTH)), _sds((bsz, t, RW_HEAD, RW_WIDTH))],
        scratch_shapes=[pltpu.VMEM((bsz, RW_HEAD, RW_WIDTH), F32), pltpu.VMEM((tc, bsz * RW_HEAD, RW_WIDTH), F32),
                        pltpu.VMEM((bsz * RW_WIDTH // LANES, tc * 8, LANES), F32)],
        compiler_params=_ARB1,
    )(r, w, k, nkk, b, vt)


def _own_head_rows(x):
    row = lax.broadcasted_iota(jnp.int32, x.shape, 0)
    lane = lax.broadcasted_iota(jnp.int32, x.shape, 1)
    return jnp.sum(jnp.where(row == (lane >> 6), x, 0.0), axis=0, keepdims=True)


def _scan_bwd(r, w, k, vt, nkk, b, st, dyt):
    bsz, t, _ = r.shape
    tc = SCAN_CHUNK
    nc = t // tc

    def body(r_ref, w_ref, k_ref, n_ref, b_ref, vt_ref, dyt_ref, st_ref, halo_ref,
             dr_ref, dw_ref, dk_ref, dv_ref, dn_ref, db_ref, g_sc, dc_sc, v8_sc, dy8_sc, *part_scs):
        c = pl.program_id(0)

        @pl.when(c == 0)
        def _():
            g_sc[...] = jnp.zeros_like(g_sc)

        ones = _seg_ones()
        diag = _diag_mask()
        has_prev = jnp.where(c == nc - 1, 0.0, 1.0)
        seqs = range(bsz)
        _outer_chunk(dyt_ref, r_ref, dc_sc, _head_expand(), seqs)
        for bi in seqs:
            v8_sc[bi] = jnp.concatenate([vt_ref[bi, 0].T] * 2, axis=1)
            dy8_sc[bi] = jnp.concatenate([dyt_ref[bi, 0].T] * 2, axis=1)
        by_head = lambda sc, bi, i: sc.at[bi][pl.ds(i, RW_HEADS, stride=SCAN_CHUNK), :][:, 0:RW_HEAD].astype(BF16)
        dw_sc, dv_sc, dn_sc, db_sc = part_scs

        def step(i, s_p):
            static = isinstance(i, int)
            row = lambda ref, bi: ref[bi, i:i + 1, :] if static else ref[bi, pl.ds(i, 1), :]
            put_row = lambda ref, bi, val: ref.__setitem__((bi, slice(i, i + 1) if static else pl.ds(i, 1), slice(None)), val)
            dc_all = dc_sc[i]
            dc = [dc_all[RW_HEAD * bi:RW_HEAD * (bi + 1)] for bi in seqs]
            g = [g_sc[bi] + dc[bi] for bi in seqs]
            res = _seg_multi([s_p[bi] * row(n_ref, bi) for bi in seqs] + [g[bi] * row(b_ref, bi) for bi in seqs]
                             + [g[bi] * row(k_ref, bi) for bi in seqs], ones, 1)
            sa, dsa, dvb = res[:bsz], res[bsz:2 * bsz], res[2 * bsz:]
            for bi in seqs:
                dr8 = jnp.dot(by_head(dy8_sc, bi, i), st_ref[bi, i].astype(BF16), preferred_element_type=F32)
                dk8 = jnp.dot(by_head(v8_sc, bi, i), g[bi].astype(BF16), preferred_element_type=F32)
                put_row(dr_ref, bi, _own_head_rows(dr8))
                put_row(dk_ref, bi, _own_head_rows(dk8))
                _put8(dv_sc, bi, i, _fold8(dvb[bi] * diag))
                _put8(dw_sc, bi, i, _fold8(g[bi] * s_p[bi]))
                _put8(db_sc, bi, i, _fold8(g[bi] * sa[bi]))
                _put8(dn_sc, bi, i, _fold8(s_p[bi] * dsa[bi]))
                g_sc[bi] = g[bi] * row(w_ref, bi) + dsa[bi] * row(n_ref, bi)

        def loop_step(ii, _):
            i = tc - 1 - ii
            step(i, [st_ref[bi, i - 1] for bi in seqs])
            return 0

        lax.fori_loop(0, tc - 1, loop_step, 0, unroll=5)
        step(0, [halo_ref[bi, 0] * has_prev for bi in seqs])
        for out_ref, sc in zip((dw_ref, dv_ref, dn_ref, db_ref), part_scs):
            for bi in seqs:
                out_ref[bi] = _unfold8(sc, bi, tc)

    vec = pl.BlockSpec((bsz, tc, RW_WIDTH), lambda c: (0, nc - 1 - c, 0))
    tmin = pl.BlockSpec((bsz, 1, RW_HEAD, LANES), lambda c: (0, nc - 1 - c, 0, 0))
    parts = pltpu.VMEM((bsz * RW_WIDTH // LANES, tc * 8, LANES), F32)
    heads_steps = pltpu.VMEM((bsz, LANES, LANES), F32)
    return pl.pallas_call(
        body, name="scan_bwd", grid=(nc,),
        in_specs=[vec] * 5 + [tmin, tmin,
                              pl.BlockSpec((bsz, tc, RW_HEAD, RW_WIDTH), lambda c: (0, nc - 1 - c, 0, 0)),
                              pl.BlockSpec((bsz, 1, RW_HEAD, RW_WIDTH), lambda c: (0, jnp.maximum((nc - 1 - c) * tc - 1, 0), 0, 0))],
        out_specs=[vec] * 6,
        out_shape=[_sds((bsz, t, RW_WIDTH))] * 6,
        scratch_shapes=[pltpu.VMEM((bsz, RW_HEAD, RW_WIDTH), F32), pltpu.VMEM((tc, bsz * RW_HEAD, RW_WIDTH), F32),
                        heads_steps, heads_steps] + [parts] * 4,
        compiler_params=_ARB1,
    )(r, w, k, nkk, b, vt, dyt, st, st)


TOKEN_TILE = 256


def _padded_weights(wt):
    f = lambda a: a.astype(F32)
    w_in = wt["w_in"][0].astype(BF16)
    zeros = lambda r, c: jnp.zeros((r, c), F32)
    wp = jnp.concatenate([w_in[:, :MLA_COLS], jnp.zeros((D_MODEL, PM_W - MLA_COLS), BF16), w_in[:, MLA_COLS:]], axis=1)
    w_uq = f(wt["mla_w_uq"][0]).reshape(Q_LORA, MLA_HEADS, MLA_NOPE + MLA_ROPE)
    wqn = w_uq[:, :, :MLA_NOPE].reshape(Q_LORA, MLA_HEADS * MLA_NOPE)
    wqr = jnp.concatenate([w_uq[:, :, MLA_NOPE:], jnp.zeros((Q_LORA, MLA_HEADS, LANES - MLA_ROPE), F32)], axis=2)
    wqr = wqr.reshape(Q_LORA, MLA_HEADS * LANES)
    w2p = jnp.concatenate([f(wt["rw_w2"][0]), zeros(LORA, RW_WIDTH)], axis=0)
    a2p = jnp.concatenate([zeros(LORA, RW_WIDTH), f(wt["rw_a2"][0])], axis=0)
    bw = (f(wt["mla_q_norm_g"]), wqn, wqr, f(wt["mla_kv_norm_g"]), f(wt["mla_w_ukv"][0]), f(wt["rw_mu"]), f(wt["rw_w0"]),
          w2p, f(wt["rw_a0"]), a2p, f(wt["rw_k_k"]), f(wt["rw_k_a"]))
    hw = (f(wt["rw_ln_g"]), f(wt["rw_ln_b"]), f(wt["rw_r_k"]).reshape(1, RW_WIDTH), f(wt["w_out"][0]), f(wt["norm_post_g"]))
    return wp, bw, hw


def _local_step(x, positions, target, wt):
    bsz, t, _ = x.shape
    n = bsz * t
    tm = min(TOKEN_TILE, t)
    tps = t // tm
    wp, bw, hw = _padded_weights(wt)
    wpb = wp.astype(BF16)
    g_pre = wt["norm_pre_g"].astype(F32)
    x2 = x.reshape(n, D_MODEL)
    tgt2 = target.reshape(n, D_MODEL)
    inv_freq = ROPE_THETA ** (-jnp.arange(0, MLA_ROPE, 2, dtype=F32) / MLA_ROPE)
    invf = jnp.tile(inv_freq, LANES // (MLA_ROPE // 2)).reshape(LANES, 1)
    cos, sin = _rope_tables(positions.reshape(1, n), invf, tm)

    u, pm, prw, z = _fwd_a(x2, g_pre, wpb, tm)
    qn, qr, kv, kr, r, w, k, v, nkk, b = _fwd_b(pm, prw, cos, sin, bw, tm, tps)
    b3 = lambda a: a.reshape(bsz, t, a.shape[-1])
    ym, lse = _attn_fwd(b3(qn), b3(qr), b3(kv), b3(kr))
    vt = _time_minor(b3(v))
    ys, st = _scan_fwd(b3(r), b3(w), b3(k), vt, b3(nkk), b3(b))
    (dys, dr_h, dk_h, dv_h, dym, dz, dxres, loss, d_lng, d_lnb, d_rk, d_wout, d_gpost) = _head(
        ys.reshape(n, RW_WIDTH), r, k, v, ym.reshape(n, MLA_WIDTH), z, x2, tgt2, hw, tm)
    dqn, dqr, dkv, dkr_heads = _attn_bwd(b3(qn), b3(qr), b3(kv), b3(kr), ym, b3(dym), lse)
    dr_s, dw_s, dk_s, dv_s, dn_s, db_s = _scan_bwd(b3(r), b3(w), b3(k), vt, b3(nkk), b3(b), st, _time_minor(b3(dys)))
    f2 = lambda a: a.reshape(n, a.shape[-1])
    cts = (f2(dqn), f2(dqr), f2(dkv), f2(dr_s), dr_h, f2(dw_s), f2(dk_s), dk_h, f2(dv_s), dv_h, f2(dn_s), f2(db_s))
    (dpm, dprw, dps, d_gq, d_wqn, d_wqr, d_gkv, d_wkv, d_mu, d_w0, d_w2p, d_a0, d_a2p, d_kk, d_ka) = _bwd_b(
        pm, prw, cos, sin, bw, cts, dkr_heads.reshape(MLA_HEADS, n, LANES), tm, tps)
    grad_x, dpb, d_gpre = _bwd_a(x2, g_pre, wpb, dpm, dprw, dps, dz, dxres, tm, tps)
    d_wp = _dw_in(u, dpb, min(1024, n), 640)

    d_w_in = jnp.concatenate([d_wp[:, :MLA_COLS], d_wp[:, PM_W:]], axis=1)
    d_w_uq = jnp.concatenate([d_wqn.reshape(Q_LORA, MLA_HEADS, MLA_NOPE),
                              d_wqr.reshape(Q_LORA, MLA_HEADS, LANES)[:, :, :MLA_ROPE]], axis=2)
    grads = {
        "norm_pre_g": d_gpre, "w_in": d_w_in[None], "mla_q_norm_g": d_gq,
        "mla_w_uq": d_w_uq.reshape(1, Q_LORA, MLA_HEADS * (MLA_NOPE + MLA_ROPE)), "mla_kv_norm_g": d_gkv,
        "mla_w_ukv": d_wkv[None], "rw_mu": d_mu, "rw_w0": d_w0, "rw_w2": d_w2p[None, :LORA], "rw_a0": d_a0,
        "rw_a2": d_a2p[None, LORA:], "rw_k_k": d_kk, "rw_k_a": d_ka, "rw_r_k": d_rk.reshape(1, RW_HEADS, RW_HEAD),
        "rw_ln_g": d_lng, "rw_ln_b": d_lnb, "w_out": d_wout[None], "norm_post_g": d_gpost,
    }
    return loss, grad_x.reshape(bsz, t, D_MODEL), grads


_MESH = pl.DeviceIdType.MESH


def _gather_shards(packed):
    rows, lanes = packed.shape

    def body(x_ref, out_ref, send_sems, recv_sems, local_sem):
        x, y, c = lax.axis_index("x"), lax.axis_index("y"), lax.axis_index("c")
        me, sibling = (x, y, c), (x, y, 1 - c)
        chips = [(1 - x, y), (x, 1 - y), (1 - x, 1 - y)]

        def slot(px, py, pc):
            return out_ref.at[4 * px + 2 * py + pc]

        def copy(k, block, to, src=None):
            return pltpu.make_async_remote_copy(
                src_ref=slot(*block) if src is None else src, dst_ref=slot(*block),
                send_sem=send_sems.at[k], recv_sem=recv_sems.at[k], device_id=to, device_id_type=_MESH)

        mine = pltpu.make_async_copy(x_ref, slot(*me), local_sem)
        mine.start()
        first = [copy(0, me, sibling, src=x_ref)]
        first += [copy(1 + j, me, (*chip, c), src=x_ref) for j, chip in enumerate(chips)]
        for cp in first:
            cp.start()
        passed = [copy(4 + j, (*chip, c), sibling) for j, chip in enumerate(chips)]
        for j, chip in enumerate(chips):
            copy(1 + j, (*chip, c), me).wait_recv()
            passed[j].start()
        copy(0, sibling, me).wait_recv()
        for j, chip in enumerate(chips):
            copy(4 + j, (*chip, 1 - c), me).wait_recv()
        for cp in first + passed:
            cp.wait_send()
        mine.wait()

    return pl.pallas_call(
        body, name="gather_shards",
        out_shape=_sds((N_DEV, rows, lanes), packed.dtype),
        in_specs=[pl.BlockSpec(memory_space=pltpu.VMEM)],
        out_specs=pl.BlockSpec(memory_space=pltpu.VMEM),
        scratch_shapes=[pltpu.SemaphoreType.DMA((7,)), pltpu.SemaphoreType.DMA((7,)), pltpu.SemaphoreType.DMA],
    )(packed)


SMALL_LANES = SMALL_N + LANES


def _exchange_grads(big_blocks, small_grads, loss_tile):
    nb = len(big_blocks)
    ns = len(small_grads)

    def body(*refs):
        big, small, loss_ref = refs[:nb], refs[nb:nb + ns], refs[nb + ns]
        rbig, rsmall = refs[nb + ns + 1:2 * nb + ns + 1], refs[2 * nb + ns + 1]
        send_b, recv_b, send_s, recv_s, local_sems, row_sc = refs[2 * nb + ns + 2:]
        x, y, c = lax.axis_index("x"), lax.axis_index("y"), lax.axis_index("c")
        me_lin = 4 * x + 2 * y + c
        mine = [pltpu.make_async_copy(big[j].at[me_lin], rbig[j].at[0], local_sems.at[j]) for j in range(nb)]
        for cp in mine:
            cp.start()
        off = 0
        for ref, (_, cnt) in zip(small, SMALL):
            row_sc[:, off:off + cnt] = ref[...]
            off += cnt
        row_sc[:, off:off + LANES] = loss_ref[0:1, :]
        rsmall[me_lin] = row_sc[...]
        copies = []
        for k in range(1, N_DEV):
            px, py, pc = x ^ (k >> 2), y ^ ((k >> 1) & 1), c ^ (k & 1)
            peer = (px, py, pc)
            for j in range(nb):
                copies.append(pltpu.make_async_remote_copy(
                    src_ref=big[j].at[4 * px + 2 * py + pc], dst_ref=rbig[j].at[k],
                    send_sem=send_b.at[k - 1, j], recv_sem=recv_b.at[k - 1, j], device_id=peer, device_id_type=_MESH))
            copies.append(pltpu.make_async_remote_copy(
                src_ref=row_sc, dst_ref=rsmall.at[me_lin],
                send_sem=send_s.at[k - 1], recv_sem=recv_s.at[k - 1], device_id=peer, device_id_type=_MESH))
        for cp in copies:
            cp.start()
        for cp in copies:
            cp.wait_recv()
        for cp in copies:
            cp.wait_send()
        for cp in mine:
            cp.wait()

    hbm, vmem = pl.BlockSpec(memory_space=pl.ANY), pl.BlockSpec(memory_space=pltpu.VMEM)
    return pl.pallas_call(
        body, name="exchange_grads",
        out_shape=[_sds(a.shape, a.dtype) for a in big_blocks] + [_sds((N_DEV, 1, SMALL_LANES))],
        in_specs=[hbm] * nb + [vmem] * (ns + 1),
        out_specs=[hbm] * nb + [vmem],
        scratch_shapes=[pltpu.SemaphoreType.DMA((N_DEV - 1, nb)), pltpu.SemaphoreType.DMA((N_DEV - 1, nb)),
                        pltpu.SemaphoreType.DMA((N_DEV - 1,)), pltpu.SemaphoreType.DMA((N_DEV - 1,)),
                        pltpu.SemaphoreType.DMA((nb,)), pltpu.VMEM((1, SMALL_LANES), F32)],
    )(*big_blocks, *small_grads, loss_tile)


def _adamw_math(w, g, m, v):
    m = ADAM_B1 * m + (1.0 - ADAM_B1) * g
    v = ADAM_B2 * v + (1.0 - ADAM_B2) * (g * g)
    m_hat = m / (1.0 - ADAM_B1 ** ADAM_STEP)
    v_hat = v / (1.0 - ADAM_B2 ** ADAM_STEP)
    return -ADAM_LR * (m_hat / (jnp.sqrt(v_hat) + ADAM_EPS) + ADAM_WD * w), m, v


def _reduce_adamw(name, parts, w, m, v, row_blocks):
    _, rows, cols = parts.shape
    rb = rows // row_blocks

    def body(p_ref, w_ref, m_ref, v_ref, g_out, d_out, m_out, v_out):
        g = p_ref[0].astype(F32)
        for s in range(1, N_DEV):
            g = g + p_ref[s].astype(F32)
        g_out[0] = g
        d_out[0], m_out[0], v_out[0] = _adamw_math(w_ref[0], g, m_ref[0], v_ref[0])

    blk = pl.BlockSpec((1, rb, cols), lambda i: (0, i, 0))
    return pl.pallas_call(
        body, name="reduce_adamw_" + name, grid=(row_blocks,),
        in_specs=[pl.BlockSpec((N_DEV, rb, cols), lambda i: (0, i, 0)), blk, blk, blk],
        out_specs=[blk] * 4, out_shape=[_sds((1, rows, cols))] * 4,
        compiler_params=_ARB1,
    )(parts, w, m, v)


def _reduce_adamw_small(rows, ws, ms, vs):
    ns = len(SMALL)

    def body(r_ref, *refs):
        w_refs, m_refs, v_refs, outs = refs[:ns], refs[ns:2 * ns], refs[2 * ns:3 * ns], refs[3 * ns:]
        total = r_ref[0]
        for s in range(1, N_DEV):
            total = total + r_ref[s]
        off = 0
        for j, (_, cnt) in enumerate(SMALL):
            g = total[:, off:off + cnt]
            off += cnt
            outs[4 * j][...] = g
            outs[4 * j + 1][...], outs[4 * j + 2][...], outs[4 * j + 3][...] = _adamw_math(
                w_refs[j][...], g, m_refs[j][...], v_refs[j][...])
        outs[4 * ns][...] = total[:, off:off + LANES]

    vmem = pl.BlockSpec(memory_space=pltpu.VMEM)
    return pl.pallas_call(
        body, name="reduce_adamw_small",
        in_specs=[vmem] * (1 + 3 * ns), out_specs=[vmem] * (4 * ns + 1),
        out_shape=[_sds((1, cnt)) for _, cnt in SMALL for _ in range(4)] + [_sds((1, LANES))],
    )(rows, *ws, *ms, *vs)


def _pack_rows(arrs):
    return jnp.concatenate([a.reshape(-1, LANES) for a in arrs], axis=0)


def _shard_blocks(name, full):
    a = full[0]
    rows, cols = a.shape
    if name == "w_out":
        return a.reshape(N_DEV, rows // N_DEV, cols)
    return a.reshape(rows, N_DEV, cols // N_DEV).transpose(1, 0, 2)


def _unshard(name, blocks, shard_shape):
    _, rows, cols = shard_shape
    a = blocks.reshape(N_DEV, rows, cols)
    if name == "w_out":
        return a.reshape(1, N_DEV * rows, cols)
    return a.transpose(1, 0, 2).reshape(1, rows, N_DEV * cols)


def kernel(x, positions, norm_pre_g, w_in, mla_q_norm_g, mla_w_uq, mla_kv_norm_g, mla_w_ukv, rw_mu, rw_w0, rw_w2, rw_a0, rw_a2, rw_k_k, rw_k_a, rw_r_k, rw_ln_g, rw_ln_b, w_out, norm_post_g, loss_target, m_norm_pre_g, m_w_in, m_mla_q_norm_g, m_mla_w_uq, m_mla_kv_norm_g, m_mla_w_ukv, m_rw_mu, m_rw_w0, m_rw_w2, m_rw_a0, m_rw_a2, m_rw_k_k, m_rw_k_a, m_rw_r_k, m_rw_ln_g, m_rw_ln_b, m_w_out, m_norm_post_g, v_norm_pre_g, v_w_in, v_mla_q_norm_g, v_mla_w_uq, v_mla_kv_norm_g, v_mla_w_ukv, v_rw_mu, v_rw_w0, v_rw_w2, v_rw_a0, v_rw_a2, v_rw_k_k, v_rw_k_a, v_rw_r_k, v_rw_ln_g, v_rw_ln_b, v_w_out, v_norm_post_g):
    given = dict(locals())
    w = {nm: given[nm] for nm in WEIGHTS}
    mom = {nm: given["m_" + nm] for nm in WEIGHTS}
    var = {nm: given["v_" + nm] for nm in WEIGHTS}
    sharded = [nm for nm, _ in SHARD_ROWS]

    gathered = _gather_shards(_pack_rows([w[nm] for nm in sharded]).astype(BF16))
    full, off = dict(w), 0
    for nm, cnt in SHARD_ROWS:
        full[nm] = _unshard(nm, gathered[:, off:off + cnt], w[nm].shape)
        off += cnt

    loss_part, grad_x, grads = _local_step(x, positions, loss_target, full)

    small_names = [nm for nm, _ in SMALL]
    row = lambda a: a.reshape(1, -1)
    got = _exchange_grads([_shard_blocks(nm, grads[nm]).astype(BF16) for nm in sharded],
                          [row(grads[nm]) for nm in small_names], loss_part)
    new = {}
    for nm, parts in zip(sharded, got[:-1]):
        new[nm] = _reduce_adamw(nm, parts, w[nm], mom[nm], var[nm], 4 if nm == "w_in" else 1)
    res = _reduce_adamw_small(got[-1], [row(w[nm]) for nm in small_names], [row(mom[nm]) for nm in small_names],
                              [row(var[nm]) for nm in small_names])
    for j, nm in enumerate(small_names):
        new[nm] = tuple(a.reshape(w[nm].shape) for a in res[4 * j:4 * j + 4])
    loss = res[-1][0, 0]
    return (loss, grad_x, *[new[nm][j] for j in range(4) for nm in WEIGHTS])
```

```python
import functools

import jax
import jax.numpy as jnp
from jax import lax
from jax.experimental import pallas as pl
from jax.experimental.pallas import tpu as pltpu

F32 = jnp.float32
BF16 = jnp.bfloat16

D_MODEL = 1024
MLA_HEADS = 4
MLA_NOPE = 128
MLA_ROPE = 64
MLA_V = 128
MLA_WIDTH = MLA_HEADS * MLA_V
Q_LORA = 256
KV_LORA = 128
ROPE_THETA = 10000.0
RW_HEAD = 64
RW_WIDTH = 512
RW_HEADS = RW_WIDTH // RW_HEAD
LORA = 64
RW_COLS = 3 * RW_WIDTH + 2 * LORA
MLA_COLS = Q_LORA + KV_LORA + MLA_ROPE
D_IN = MLA_COLS + RW_COLS + D_MODEL
RW_GN_EPS = 64e-5
NORM_EPS = 1e-6
ATT_SCALE = (MLA_NOPE + MLA_ROPE) ** -0.5
ADAM_LR, ADAM_B1, ADAM_B2, ADAM_EPS, ADAM_WD, ADAM_STEP = 0.001, 0.9, 0.999, 1e-08, 0.01, 10
N_DEV = 8
LANES = 128
MXU = 256

PM_W = 512
WP_COLS = PM_W + RW_COLS + D_MODEL
RW_PIECES = ((0, 512), (512, 1024), (1024, 1536), (1536, 1664))

SHARD_ROWS = (("w_in", 1024 * 392 // LANES), ("mla_w_uq", 256 * 96 // LANES), ("mla_w_ukv", 128 * 128 // LANES),
              ("rw_w2", 64 * 64 // LANES), ("rw_a2", 64 * 64 // LANES), ("w_out", 128 * 1024 // LANES))
PACK_ROWS = sum(r for _, r in SHARD_ROWS)
SMALL = (("norm_pre_g", 1024), ("mla_q_norm_g", 256), ("mla_kv_norm_g", 128), ("rw_mu", 1664), ("rw_w0", 512),
         ("rw_a0", 512), ("rw_k_k", 512), ("rw_k_a", 512), ("rw_r_k", 512), ("rw_ln_g", 512), ("rw_ln_b", 512),
         ("norm_post_g", 1024))
SMALL_N = sum(n for _, n in SMALL)
SMALL_ROWS = 72
WEIGHTS = ("norm_pre_g", "w_in", "mla_q_norm_g", "mla_w_uq", "mla_kv_norm_g", "mla_w_ukv", "rw_mu", "rw_w0", "rw_w2",
           "rw_a0", "rw_a2", "rw_k_k", "rw_k_a", "rw_r_k", "rw_ln_g", "rw_ln_b", "w_out", "norm_post_g")


def _seg_ones():
    r = lax.broadcasted_iota(jnp.int32, (MXU, MXU), 0) >> 6
    c = lax.broadcasted_iota(jnp.int32, (MXU, MXU), 1) >> 6
    return jnp.where(r == c, 1.0, 0.0).astype(BF16)


def _seg_dot(x, ones, passes):
    parts, rem = [], x
    for p in range(passes):
        hb = rem.astype(BF16)
        parts.append(hb)
        if p + 1 < passes:
            rem = rem - hb.astype(F32)
    outs = []
    for j in range(x.shape[1] // MXU):
        acc = None
        for hb in parts:
            d = jnp.dot(hb[:, MXU * j:MXU * (j + 1)], ones, preferred_element_type=F32)
            acc = d if acc is None else acc + d
        outs.append(acc)
    return outs[0] if len(outs) == 1 else jnp.concatenate(outs, axis=1)


def _seg_multi(xs, ones, passes):
    his = [x.astype(BF16) for x in xs]
    hi = jnp.concatenate(his, axis=0)
    if passes == 2:
        lo = jnp.concatenate([(x - h.astype(F32)).astype(BF16) for x, h in zip(xs, his)], axis=0)
        rhs = jnp.concatenate([ones, ones], axis=0)
    halves = []
    for j in range(hi.shape[1] // MXU):
        sl = slice(MXU * j, MXU * (j + 1))
        if passes == 2:
            halves.append(jnp.dot(jnp.concatenate([hi[:, sl], lo[:, sl]], axis=1), rhs, preferred_element_type=F32))
        else:
            halves.append(jnp.dot(hi[:, sl], ones, preferred_element_type=F32))
    full = jnp.concatenate(halves, axis=1)
    m = xs[0].shape[0]
    return [full[m * i:m * (i + 1)] for i in range(len(xs))]


@jax.custom_vjp
def _segsum(x):
    return _seg_dot(x, _seg_ones(), 3)


_segsum.defvjp(lambda x: (_segsum(x), None), lambda _, g: (_segsum(g),))


@jax.custom_vjp
def _bdot(a, w):
    return jnp.dot(a.astype(BF16), w.astype(BF16), preferred_element_type=F32)


def _bdot_fwd(a, w):
    return _bdot(a, w), (a, w)


def _bdot_bwd(res, g):
    a, w = res
    gb = g.astype(BF16)
    da = lax.dot_general(gb, w.astype(BF16), (((1,), (1,)), ((), ())), preferred_element_type=F32)
    dw = lax.dot_general(a.astype(BF16), gb, (((0,), (0,)), ((), ())), preferred_element_type=F32)
    return da, dw


_bdot.defvjp(_bdot_fwd, _bdot_bwd)


def _rot_impl(x):
    w = x.shape[1]
    lane = lax.broadcasted_iota(jnp.int32, x.shape, 1)
    return jnp.where((lane & 63) < 32, -pltpu.roll(x, w - 32, 1), pltpu.roll(x, 32, 1))


@jax.custom_vjp
def _rot(x):
    return _rot_impl(x)


_rot.defvjp(lambda x: (_rot_impl(x), None), lambda _, g: (-_rot_impl(g),))


def _rms(x, g):
    return x * lax.rsqrt(jnp.mean(x * x, axis=-1, keepdims=True) + NORM_EPS) * g


def _shift_rows(p, prev_row):
    row = lax.broadcasted_iota(jnp.int32, p.shape, 0)
    return jnp.where(row == 0, prev_row, pltpu.roll(p, 1, 0))


def _unshift_rows(g, next_row):
    row = lax.broadcasted_iota(jnp.int32, g.shape, 0)
    return jnp.where(row == g.shape[0] - 1, next_row, pltpu.roll(g, g.shape[0] - 1, 0))


def _f_mla(cq, ckv, kr, cos, sin, g_q, wqn, wqr, g_kv, wkv):
    qn = _rms(cq, g_q)
    q_nope = _bdot(qn, wqn)
    q_r = _bdot(qn, wqr)
    cos4 = jnp.concatenate([cos] * MLA_HEADS, axis=1)
    sin4 = jnp.concatenate([sin] * MLA_HEADS, axis=1)
    q_rope = q_r * cos4 + _rot(q_r) * sin4
    kv = _bdot(_rms(ckv, g_kv), wkv)
    k_rope = kr * cos + _rot(kr) * sin
    return q_nope, q_rope, kv, k_rope


def _f_rw(pr, pk, pv, pt, sr, sk, sv, st, mu_r, mu_k, mu_v, mu_t, w0, w2p, a0, a2p, k_k, k_a):
    r = pr + (sr - pr) * mu_r
    k = pk + (sk - pk) * mu_k
    v = pv + (sv - pv) * mu_v
    t = pt + (st - pt) * mu_t
    nwl = -(w0 + _bdot(jnp.tanh(t), w2p))
    softplus = jnp.maximum(nwl, 0.0) + jnp.log(1.0 + jnp.exp(-jnp.abs(nwl)))
    decay = jnp.exp(-jnp.exp(-softplus - 0.5))
    a = jax.nn.sigmoid(a0 + _bdot(t, a2p))
    kk = k * k_k
    kk = kk / jnp.maximum(jnp.sqrt(_segsum(kk * kk)), 1e-12)
    k2 = k * (1.0 + (a - 1.0) * k_a)
    return r, decay, k2, v, -kk, kk * a


def _f_head(ys, r, k, v, ym, z1, z2, x, tgt, ln_g, ln_b, r_k, w1, w2, g_post):
    inv = 1.0 / RW_HEAD
    yc = ys - _segsum(ys) * inv
    var = _segsum(yc * yc) * inv
    y = yc * lax.rsqrt(var + RW_GN_EPS) * ln_g + ln_b
    y_rw = y + _segsum(r * k * r_k) * v
    c1 = ym * (z1 * jax.nn.sigmoid(z1))
    c2 = y_rw * (z2 * jax.nn.sigmoid(z2))
    out = _bdot(c1, w1) + _bdot(c2, w2)
    err = x + _rms(out, g_post) - tgt
    per_row = jnp.sum(err * err, axis=1, keepdims=True)
    return jnp.sum(per_row, axis=0, keepdims=True) * (0.5 / D_MODEL)


def _rows(tm, width):
    return pl.BlockSpec((tm, width), lambda i: (i, 0))


def _whole(shape):
    zeros = (0,) * len(shape)
    return pl.BlockSpec(shape, lambda i: zeros)


def _sds(shape, dtype=F32):
    return jax.ShapeDtypeStruct(shape, dtype)


_ARB1 = pltpu.CompilerParams(dimension_semantics=("arbitrary",))


def _acc(ref, val, first):
    @pl.when(first)
    def _():
        ref[...] = val

    @pl.when(jnp.logical_not(first))
    def _():
        ref[...] += val


def _fwd_a(x2, g_pre, wp, tm):
    n = x2.shape[0]

    def body(x_ref, g_ref, w_ref, u_ref, pm_ref, prw_ref, z_ref):
        ub = _rms(x_ref[...], g_ref[...]).astype(BF16)
        u_ref[...] = ub
        pm_ref[...] = jnp.dot(ub, w_ref[:, 0:PM_W], preferred_element_type=F32)
        prw_ref[...] = jnp.dot(ub, w_ref[:, PM_W:PM_W + RW_COLS], preferred_element_type=F32)
        z_ref[...] = jnp.dot(ub, w_ref[:, PM_W + RW_COLS:WP_COLS], preferred_element_type=F32)

    return pl.pallas_call(
        body, name="fwd_a", grid=(n // tm,),
        in_specs=[_rows(tm, D_MODEL), _whole((1, D_MODEL)), _whole((D_MODEL, WP_COLS))],
        out_specs=[_rows(tm, D_MODEL), _rows(tm, PM_W), _rows(tm, RW_COLS), _rows(tm, D_MODEL)],
        out_shape=[_sds((n, D_MODEL), BF16), _sds((n, PM_W)), _sds((n, RW_COLS)), _sds((n, D_MODEL))],
        compiler_params=_ARB1,
    )(x2, g_pre, wp)


def _rope_tables(pos_row, invf_col, tm):
    n = pos_row.shape[1]

    def body(p_ref, f_ref, c_ref, s_ref):
        ang = f_ref[...] * p_ref[...].astype(F32)
        c_ref[...] = jnp.cos(ang).T
        s_ref[...] = jnp.sin(ang).T

    return pl.pallas_call(
        body, name="rope_tables", grid=(n // tm,),
        in_specs=[pl.BlockSpec((1, tm), lambda i: (0, i)), _whole((LANES, 1))],
        out_specs=[_rows(tm, LANES), _rows(tm, LANES)],
        out_shape=[_sds((n, LANES)), _sds((n, LANES))],
        compiler_params=_ARB1,
    )(pos_row, invf_col)


_B_WEIGHT_SHAPES = ((1, Q_LORA), (Q_LORA, 512), (Q_LORA, 512), (1, KV_LORA), (KV_LORA, 1024), (1, RW_COLS), (1, RW_WIDTH),
                    (LANES, RW_WIDTH), (1, RW_WIDTH), (LANES, RW_WIDTH), (1, RW_WIDTH), (1, RW_WIDTH))


def _halo_prev(tm):
    return pl.BlockSpec((8, RW_COLS), lambda i: (jnp.maximum(i * (tm // 8) - 1, 0), 0))


def _b_operands(pm_ref, prw_ref, halo_ref, wrefs, tile, tiles_per_seq):
    g_q, wqn, wqr, g_kv, wkv, mu, w0, w2p, a0, a2p, k_k, k_a = wrefs
    mla_in = (pm_ref[:, 0:Q_LORA], pm_ref[:, Q_LORA:Q_LORA + KV_LORA], pm_ref[:, Q_LORA + KV_LORA:PM_W])
    mla_w = (g_q[...], wqn[...], wqr[...], g_kv[...], wkv[...])
    keep = jnp.where(tile % tiles_per_seq == 0, 0.0, 1.0)
    prev = halo_ref[7:8, :] * keep
    ps = tuple(prw_ref[:, a:b] for a, b in RW_PIECES)
    ss = tuple(_shift_rows(p, prev[:, a:b]) for p, (a, b) in zip(ps, RW_PIECES))
    rw_w = tuple(mu[:, a:b] for a, b in RW_PIECES) + (w0[...], w2p[...], a0[...], a2p[...], k_k[...], k_a[...])
    return mla_in, mla_w, ps + ss, rw_w


def _fwd_b(pm, prw, cos, sin, bw, tm, tiles_per_seq):
    n = pm.shape[0]

    def body(pm_ref, prw_ref, halo_ref, cos_ref, sin_ref, *refs):
        wrefs, outs = refs[:12], refs[12:]
        mla_in, mla_w, rw_in, rw_w = _b_operands(pm_ref, prw_ref, halo_ref, wrefs, pl.program_id(0), tiles_per_seq)
        res = _f_mla(*mla_in, cos_ref[...], sin_ref[...], *mla_w) + _f_rw(*rw_in, *rw_w)
        for o_ref, val in zip(outs, res):
            o_ref[...] = val.astype(o_ref.dtype)

    widths = (512, 512, 1024, LANES) + (RW_WIDTH,) * 6
    return pl.pallas_call(
        body, name="fwd_b", grid=(n // tm,),
        in_specs=[_rows(tm, PM_W), _rows(tm, RW_COLS), _halo_prev(tm), _rows(tm, LANES), _rows(tm, LANES)]
        + [_whole(s) for s in _B_WEIGHT_SHAPES],
        out_specs=[_rows(tm, w) for w in widths],
        out_shape=[_sds((n, w), BF16 if j < 4 else F32) for j, w in enumerate(widths)],
        compiler_params=_ARB1,
    )(pm, prw, prw, cos, sin, *bw)


def _bwd_b(pm, prw, cos, sin, bw, cts, dkr_heads, tm, tiles_per_seq):
    n = pm.shape[0]

    ct_widths = (512, 512, 1024) + (RW_WIDTH,) * 9
    n_ct = len(ct_widths)

    def body(pm_ref, prw_ref, halo_ref, cos_ref, sin_ref, *refs):
        wrefs, ct_refs, dkr_ref = refs[:12], refs[12:12 + n_ct], refs[12 + n_ct]
        dpm_ref, dprw_ref, dps_ref = refs[13 + n_ct:16 + n_ct]
        wg_refs = refs[16 + n_ct:]
        tile = pl.program_id(0)
        first = tile == 0
        mla_in, mla_w, rw_in, rw_w = _b_operands(pm_ref, prw_ref, halo_ref, wrefs, tile, tiles_per_seq)
        cos, sin = cos_ref[...], sin_ref[...]
        ct = [r[...] for r in ct_refs]
        _, vjp_mla = jax.vjp(lambda *a: _f_mla(*a[:3], cos, sin, *a[3:]), *mla_in, *mla_w)
        dkr = dkr_ref[0] + dkr_ref[1] + dkr_ref[2] + dkr_ref[3]
        d_mla = vjp_mla((ct[0], ct[1], ct[2], dkr))
        dpm_ref[:, 0:Q_LORA] = d_mla[0]
        dpm_ref[:, Q_LORA:Q_LORA + KV_LORA] = d_mla[1]
        dpm_ref[:, Q_LORA + KV_LORA:PM_W] = d_mla[2]
        _, vjp_rw = jax.vjp(_f_rw, *rw_in, *rw_w)
        d_rw = vjp_rw((ct[3] + ct[4], ct[5], ct[6] + ct[7], ct[8] + ct[9], ct[10], ct[11]))
        for j, (a, b) in enumerate(RW_PIECES):
            dprw_ref[:, a:b] = d_rw[j]
            dps_ref[:, a:b] = d_rw[4 + j]
        g_q, wqn, wqr, g_kv, wkv, mu, w0, w2p, a0, a2p, k_k, k_a = wg_refs
        for ref, val in zip((g_q, wqn, wqr, g_kv, wkv), d_mla[3:]):
            _acc(ref, val, first)
        for j, (a, b) in enumerate(RW_PIECES):
            _acc(mu.at[:, a:b], d_rw[8 + j], first)
        for ref, val in zip((w0, w2p, a0, a2p, k_k, k_a), d_rw[12:]):
            _acc(ref, val, first)

    return pl.pallas_call(
        body, name="bwd_b", grid=(n // tm,),
        in_specs=[_rows(tm, PM_W), _rows(tm, RW_COLS), _halo_prev(tm), _rows(tm, LANES), _rows(tm, LANES)]
        + [_whole(s) for s in _B_WEIGHT_SHAPES] + [_rows(tm, w) for w in ct_widths]
        + [pl.BlockSpec((MLA_HEADS, tm, LANES), lambda i: (0, i, 0))],
        out_specs=[_rows(tm, PM_W), _rows(tm, RW_COLS), _rows(tm, RW_COLS)] + [_whole(s) for s in _B_WEIGHT_SHAPES],
        out_shape=[_sds((n, PM_W)), _sds((n, RW_COLS)), _sds((n, RW_COLS))] + [_sds(s) for s in _B_WEIGHT_SHAPES],
        compiler_params=_ARB1,
    )(pm, prw, prw, cos, sin, *bw, *cts, dkr_heads)


def _head(ys, r, k, v, ym, z, x2, tgt, hw, tm):
    n = x2.shape[0]
    h_shapes = ((1, RW_WIDTH), (1, RW_WIDTH), (1, RW_WIDTH), (D_MODEL, D_MODEL), (1, D_MODEL))

    def body(ys_ref, r_ref, k_ref, v_ref, ym_ref, z_ref, x_ref, t_ref, lng, lnb, rk, wout, gpost,
             dys_ref, dr_ref, dk_ref, dv_ref, dym_ref, dz_ref, dx_ref, loss_ref, dlng, dlnb, drk, dwout, dgpost):
        first = pl.program_id(0) == 0
        tgt_v = t_ref[...]
        args = (ys_ref[...], r_ref[...], k_ref[...], v_ref[...], ym_ref[...], z_ref[:, 0:MLA_WIDTH], z_ref[:, MLA_WIDTH:D_MODEL],
                x_ref[...], lng[...], lnb[...], rk[...], wout[0:MLA_WIDTH, :], wout[MLA_WIDTH:D_MODEL, :], gpost[...])
        loss, vjp = jax.vjp(lambda *a: _f_head(*a[:8], tgt_v, *a[8:]), *args)
        d = vjp(jnp.ones((1, 1), F32))
        dys_ref[...] = d[0]
        dr_ref[...] = d[1]
        dk_ref[...] = d[2]
        dv_ref[...] = d[3]
        dym_ref[...] = d[4].astype(BF16)
        dz_ref[:, 0:MLA_WIDTH] = d[5]
        dz_ref[:, MLA_WIDTH:D_MODEL] = d[6]
        dx_ref[...] = d[7]
        _acc(loss_ref, jnp.broadcast_to(loss, (8, LANES)), first)
        _acc(dlng, d[8], first)
        _acc(dlnb, d[9], first)
        _acc(drk, d[10], first)
        _acc(dwout.at[0:MLA_WIDTH, :], d[11], first)
        _acc(dwout.at[MLA_WIDTH:D_MODEL, :], d[12], first)
        _acc(dgpost, d[13], first)

    widths = (RW_WIDTH,) * 4 + (MLA_WIDTH, D_MODEL, D_MODEL)
    return pl.pallas_call(
        body, name="head", grid=(n // tm,),
        in_specs=[_rows(tm, RW_WIDTH)] * 4 + [_rows(tm, MLA_WIDTH), _rows(tm, D_MODEL), _rows(tm, D_MODEL), _rows(tm, D_MODEL)]
        + [_whole(s) for s in h_shapes],
        out_specs=[_rows(tm, w) for w in widths] + [_whole((8, LANES))] + [_whole(s) for s in h_shapes],
        out_shape=[_sds((n, w), BF16 if j == 4 else F32) for j, w in enumerate(widths)] + [_sds((8, LANES))]
        + [_sds(s) for s in h_shapes],
        compiler_params=_ARB1,
    )(ys, r, k, v, ym, z, x2, tgt, *hw)


def _halo_next(tm, n):
    last = n // 8 - 1
    return pl.BlockSpec((8, RW_COLS), lambda i: (jnp.minimum((i + 1) * (tm // 8), last), 0))


def _bwd_a(x2, g_pre, wp, dpm, dprw, dps, dz, dxres, tm, tiles_per_seq):
    n = x2.shape[0]
    nt_dims = (((1,), (1,)), ((), ()))

    def body(x_ref, g_ref, w_ref, dpm_ref, dprw_ref, dps_ref, nxt_ref, dz_ref, dxres_ref, gx_ref, dpb_ref, dg_ref):
        tile = pl.program_id(0)
        keep = jnp.where((tile + 1) % tiles_per_seq == 0, 0.0, 1.0)
        dprw_v = dprw_ref[...] + _unshift_rows(dps_ref[...], nxt_ref[0:1, :] * keep)
        dpm_b, dprw_b, dz_b = dpm_ref[...].astype(BF16), dprw_v.astype(BF16), dz_ref[...].astype(BF16)
        dpb_ref[:, 0:PM_W] = dpm_b
        dpb_ref[:, PM_W:PM_W + RW_COLS] = dprw_b
        dpb_ref[:, PM_W + RW_COLS:WP_COLS] = dz_b
        du = (lax.dot_general(dpm_b, w_ref[:, 0:PM_W], nt_dims, preferred_element_type=F32)
              + lax.dot_general(dprw_b, w_ref[:, PM_W:PM_W + RW_COLS], nt_dims, preferred_element_type=F32)
              + lax.dot_general(dz_b, w_ref[:, PM_W + RW_COLS:WP_COLS], nt_dims, preferred_element_type=F32))
        x = x_ref[...]
        xhat = x * lax.rsqrt(jnp.mean(x * x, axis=-1, keepdims=True) + NORM_EPS)
        dxn = du * g_ref[...]
        dx = (dxn - xhat * jnp.mean(dxn * xhat, axis=-1, keepdims=True)) * lax.rsqrt(jnp.mean(x * x, axis=-1, keepdims=True) + NORM_EPS)
        gx_ref[...] = dx + dxres_ref[...]
        _acc(dg_ref, jnp.sum(du * xhat, axis=0, keepdims=True), tile == 0)

    return pl.pallas_call(
        body, name="bwd_a", grid=(n // tm,),
        in_specs=[_rows(tm, D_MODEL), _whole((1, D_MODEL)), _whole((D_MODEL, WP_COLS)), _rows(tm, PM_W), _rows(tm, RW_COLS),
                  _rows(tm, RW_COLS), _halo_next(tm, n), _rows(tm, D_MODEL), _rows(tm, D_MODEL)],
        out_specs=[_rows(tm, D_MODEL), _rows(tm, WP_COLS), _whole((1, D_MODEL))],
        out_shape=[_sds((n, D_MODEL)), _sds((n, WP_COLS), BF16), _sds((1, D_MODEL))],
        compiler_params=_ARB1,
    )(x2, g_pre, wp, dpm, dprw, dps, dps, dz, dxres)


def _dw_in(u, dpb, tk, tn):
    n = u.shape[0]
    steps = n // tk

    def body(u_ref, d_ref, o_ref, acc_sc):
        k = pl.program_id(1)
        _acc(acc_sc, lax.dot_general(u_ref[...], d_ref[...], _TN, preferred_element_type=F32), k == 0)

        @pl.when(k == steps - 1)
        def _():
            o_ref[...] = acc_sc[...].astype(BF16)

    return pl.pallas_call(
        body, name="dw_in", grid=(WP_COLS // tn, steps),
        in_specs=[pl.BlockSpec((tk, D_MODEL), lambda j, k: (k, 0)), pl.BlockSpec((tk, tn), lambda j, k: (k, j))],
        out_specs=pl.BlockSpec((D_MODEL, tn), lambda j, k: (0, j)),
        out_shape=_sds((D_MODEL, WP_COLS), BF16),
        scratch_shapes=[pltpu.VMEM((D_MODEL, tn), F32)],
        compiler_params=pltpu.CompilerParams(dimension_semantics=("arbitrary", "arbitrary")),
    )(u, dpb)


ATT_BLK = 256
ATT_BWD_Q = 256
_NT = (((1,), (1,)), ((), ()))
_TN = (((0,), (0,)), ((), ()))


def _causal(q0, k0, blk, blk_k=None):
    blk_k = blk if blk_k is None else blk_k
    row = q0 + lax.broadcasted_iota(jnp.int32, (blk, blk_k), 0)
    col = k0 + lax.broadcasted_iota(jnp.int32, (blk, blk_k), 1)
    return row >= col


def _attn_fwd(qn, qr, kv, kr):
    bsz, t, _ = qn.shape
    blk = min(ATT_BLK, t)

    heads = range(MLA_HEADS)

    def body(qn_ref, qr_ref, kv_ref, kr_ref, o_ref, lse_ref):
        qi = pl.program_id(1)
        q = [jnp.concatenate([qn_ref[:, LANES * h:LANES * (h + 1)], qr_ref[:, LANES * h:LANES * (h + 1)]], axis=1) for h in heads]
        lower = _causal(0, 0, blk)

        def kv_step(j, carry, diagonal):
            ks = pl.multiple_of(j * blk, blk)
            k_rope = kr_ref[pl.ds(ks, blk), :]
            scores = []
            for h in heads:
                k = jnp.concatenate([kv_ref[pl.ds(ks, blk), 2 * LANES * h:2 * LANES * h + LANES], k_rope], axis=1)
                scores.append(lax.dot_general(q[h], k, _NT, preferred_element_type=F32))
            soft = []
            for h in heads:
                m, l, _ = carry[h]
                s = scores[h] * ATT_SCALE
                if diagonal:
                    s = jnp.where(lower, s, -1e30)
                m_new = jnp.maximum(m, jnp.max(s, axis=1, keepdims=True))
                alpha = jnp.exp(m - m_new)
                p = jnp.exp(s - m_new)
                soft.append((m_new, alpha * l + jnp.sum(p, axis=1, keepdims=True), alpha, p.astype(BF16)))
            out = []
            for h in heads:
                m_new, l, alpha, p = soft[h]
                v = kv_ref[pl.ds(ks, blk), 2 * LANES * h + LANES:2 * LANES * (h + 1)]
                out.append((m_new, l, alpha * carry[h][2] + jnp.dot(p, v, preferred_element_type=F32)))
            return tuple(out)

        one = (jnp.full((blk, 1), -1e30, F32), jnp.zeros((blk, 1), F32), jnp.zeros((blk, MLA_V), F32))
        carry = lax.fori_loop(0, qi, lambda j, c: kv_step(j, c, False), (one,) * MLA_HEADS)
        carry = kv_step(qi, carry, True)
        for h in heads:
            m, l, acc = carry[h]
            o_ref[:, LANES * h:LANES * (h + 1)] = acc / l
            lse_ref[h] = jnp.broadcast_to(m + jnp.log(l), (blk, LANES))

    return pl.pallas_call(
        body, name="attn_fwd", grid=(bsz, t // blk),
        in_specs=[pl.BlockSpec((None, blk, MLA_WIDTH), lambda b, i: (b, i, 0)),
                  pl.BlockSpec((None, blk, MLA_WIDTH), lambda b, i: (b, i, 0)),
                  pl.BlockSpec((None, t, 2 * MLA_WIDTH), lambda b, i: (b, 0, 0)),
                  pl.BlockSpec((None, t, LANES), lambda b, i: (b, 0, 0))],
        out_specs=[pl.BlockSpec((None, blk, MLA_WIDTH), lambda b, i: (b, i, 0)),
                   pl.BlockSpec((None, MLA_HEADS, blk, LANES), lambda b, i: (b, 0, i, 0))],
        out_shape=[_sds((bsz, t, MLA_WIDTH)), _sds((bsz, MLA_HEADS, t, LANES))],
        compiler_params=pltpu.CompilerParams(dimension_semantics=("arbitrary", "arbitrary")),
    )(qn, qr, kv, kr)


def _attn_bwd(qn, qr, kv, kr, o, do, lse):
    bsz, t, _ = qn.shape
    blk = min(ATT_BLK, t)
    nb = t // blk
    bq = ATT_BWD_Q

    def body(qn_ref, qr_ref, kn_ref, kr_ref, v_ref, o_ref, do_ref, lse_ref, dqn_ref, dqr_ref, dkv_ref, dkr_ref, dq_sc, delta_sc):
        dq_sc[...] = jnp.zeros_like(dq_sc)
        delta_sc[...] = jnp.sum(do_ref[...].astype(F32) * o_ref[...], axis=1, keepdims=True)

        def kv_loop(j, _):
            ks = pl.multiple_of(j * blk, blk)
            k = jnp.concatenate([kn_ref[pl.ds(ks, blk), :], kr_ref[pl.ds(ks, blk), :]], axis=1)
            vb = v_ref[pl.ds(ks, blk), :]

            def q_pair(pair, carry):
                dk, dv = carry
                subs = range(2)
                qs = [pl.multiple_of((2 * pair + sub) * bq, bq) for sub in subs]
                q = [jnp.concatenate([qn_ref[pl.ds(qs[u], bq), :], qr_ref[pl.ds(qs[u], bq), :]], axis=1) for u in subs]
                dob = [do_ref[pl.ds(qs[u], bq), :] for u in subs]
                s = [lax.dot_general(q[u], k, _NT, preferred_element_type=F32) for u in subs]
                dp = [lax.dot_general(dob[u], vb, _NT, preferred_element_type=F32) for u in subs]
                p = [jnp.where(_causal((2 * pair + u) * bq, j * blk, bq, blk),
                               jnp.exp(s[u] * ATT_SCALE - lse_ref[pl.ds(qs[u], bq), 0:1]), 0.0) for u in subs]
                ds = [(p[u] * (dp[u] - delta_sc[pl.ds(qs[u], bq), :]) * ATT_SCALE).astype(BF16) for u in subs]
                for u in subs:
                    dv = dv + lax.dot_general(p[u].astype(BF16), dob[u], _TN, preferred_element_type=F32)
                for u in subs:
                    dq_sc[pl.ds(qs[u], bq), :] += jnp.dot(ds[u], k, preferred_element_type=F32)
                    dk = dk + lax.dot_general(ds[u], q[u], _TN, preferred_element_type=F32)
                return dk, dv

            first = (j * blk) // (2 * bq)
            dk, dv = lax.fori_loop(first, t // (2 * bq), q_pair, (jnp.zeros((blk, 2 * LANES), F32), jnp.zeros((blk, MLA_V), F32)))
            dkv_ref[pl.ds(ks, blk), 0:LANES] = dk[:, 0:LANES]
            dkv_ref[pl.ds(ks, blk), LANES:2 * LANES] = dv
            dkr_ref[pl.ds(ks, blk), :] = dk[:, LANES:2 * LANES]
            return 0

        lax.fori_loop(0, nb, kv_loop, 0)
        dqn_ref[...] = dq_sc[:, 0:LANES]
        dqr_ref[...] = dq_sc[:, LANES:2 * LANES]

    head_col = lambda b, h: (b, 0, h)
    return pl.pallas_call(
        body, name="attn_bwd", grid=(bsz, MLA_HEADS),
        in_specs=[pl.BlockSpec((None, t, LANES), head_col), pl.BlockSpec((None, t, LANES), head_col),
                  pl.BlockSpec((None, t, LANES), lambda b, h: (b, 0, 2 * h)),
                  pl.BlockSpec((None, t, LANES), lambda b, h: (b, 0, 0)),
                  pl.BlockSpec((None, t, LANES), lambda b, h: (b, 0, 2 * h + 1)),
                  pl.BlockSpec((None, t, LANES), head_col), pl.BlockSpec((None, t, LANES), head_col),
                  pl.BlockSpec((None, None, t, LANES), lambda b, h: (b, h, 0, 0))],
        out_specs=[pl.BlockSpec((None, t, LANES), head_col), pl.BlockSpec((None, t, LANES), head_col),
                   pl.BlockSpec((None, t, 2 * LANES), head_col),
                   pl.BlockSpec((None, None, t, LANES), lambda b, h: (h, b, 0, 0))],
        out_shape=[_sds((bsz, t, MLA_WIDTH)), _sds((bsz, t, MLA_WIDTH)), _sds((bsz, t, 2 * MLA_WIDTH)),
                   _sds((MLA_HEADS, bsz, t, LANES))],
        scratch_shapes=[pltpu.VMEM((t, 2 * LANES), F32), pltpu.VMEM((t, 1), F32)],
        compiler_params=pltpu.CompilerParams(dimension_semantics=("arbitrary", "arbitrary")),
    )(qn, qr, kv, kr, kv, o, do, lse)


SCAN_CHUNK = 16


def _diag_mask():
    row = lax.broadcasted_iota(jnp.int32, (RW_HEAD, RW_WIDTH), 0)
    lane = lax.broadcasted_iota(jnp.int32, (RW_HEAD, RW_WIDTH), 1)
    return jnp.where(row == (lane & (RW_HEAD - 1)), 1.0, 0.0)


def _time_minor(a):
    bsz, t, _ = a.shape
    a = a.reshape(bsz, t // SCAN_CHUNK, SCAN_CHUNK, RW_HEADS, RW_HEAD)
    return a.transpose(0, 1, 4, 3, 2).reshape(bsz, t // SCAN_CHUNK, RW_HEAD, RW_HEADS * SCAN_CHUNK)


def _head_expand():
    l = lax.broadcasted_iota(jnp.int32, (2 * LANES, RW_WIDTH), 0)
    n = lax.broadcasted_iota(jnp.int32, (2 * LANES, RW_WIDTH), 1)
    return jnp.where(((l & (LANES - 1)) >> 4) == (n >> 6), 1.0, 0.0).astype(BF16)


BCAST_GROUP = 4


def _outer_chunk(tm_ref, row_ref, out_sc, expand, seqs):
    step_of_lane = lax.broadcasted_iota(jnp.int32, (RW_HEAD, LANES), 1) & (SCAN_CHUNK - 1)
    tiles = [tm_ref[bi, 0] for bi in seqs]
    for t0 in range(0, SCAN_CHUNK, BCAST_GROUP):
        parts = []
        for t in range(t0, t0 + BCAST_GROUP):
            for tile in tiles:
                a = jnp.where(step_of_lane == t, tile, 0.0)
                hi = a.astype(BF16)
                parts.append(jnp.concatenate([hi, (a - hi.astype(F32)).astype(BF16)], axis=1))
        cols = jnp.dot(jnp.concatenate(parts, axis=0), expand, preferred_element_type=F32)
        for j, t in enumerate(range(t0, t0 + BCAST_GROUP)):
            base = j * RW_HEAD * len(seqs)
            out_sc[t] = jnp.concatenate([cols[base + RW_HEAD * bi:base + RW_HEAD * (bi + 1)] * row_ref[bi, t:t + 1, :]
                                         for bi in seqs], axis=0)


def _fold8(x):
    acc = x[0:8]
    for j in range(1, x.shape[0] // 8):
        acc = acc + x[8 * j:8 * (j + 1)]
    return acc


def _rows8(at):
    return pl.ds(at * 8 if isinstance(at, int) else pl.multiple_of(at * 8, 8), 8)


def _put8(sc, bi, at, val):
    for j in range(RW_WIDTH // LANES):
        sc[bi * (RW_WIDTH // LANES) + j, _rows8(at), :] = val[:, LANES * j:LANES * (j + 1)]


def _unfold8(sc, bi, steps):
    tiles = []
    for j in range(RW_WIDTH // LANES):
        view = sc.at[bi * (RW_WIDTH // LANES) + j]
        acc = view[pl.ds(0, steps, stride=8), :]
        for s in range(1, 8):
            acc = acc + view[pl.ds(s, steps, stride=8), :]
        tiles.append(acc)
    return jnp.concatenate(tiles, axis=1)


def _scan_fwd(r, w, k, vt, nkk, b):
    bsz, t, _ = r.shape
    tc = SCAN_CHUNK

    def body(r_ref, w_ref, k_ref, n_ref, b_ref, vt_ref, y_ref, st_ref, s_sc, vc_sc, y_sc):
        @pl.when(pl.program_id(0) == 0)
        def _():
            s_sc[...] = jnp.zeros_like(s_sc)

        ones = _seg_ones()
        diag = _diag_mask()
        seqs = range(bsz)
        _outer_chunk(vt_ref, k_ref, vc_sc, _head_expand(), seqs)

        def put_y(ya, at):
            for bi in seqs:
                _put8(y_sc, bi, at, _fold8(ya[bi] * diag))

        def step(i, _):
            row = lambda ref, bi: ref[bi, pl.ds(i, 1), :]
            prev = jnp.maximum(i - 1, 0)
            s_old = [s_sc[bi] for bi in seqs]
            sa = _seg_multi([s_old[bi] * row(n_ref, bi) for bi in seqs], ones, 1)
            put_y(_seg_multi([s_old[bi] * r_ref[bi, pl.ds(prev, 1), :] for bi in seqs], ones, 1), prev)
            vk = vc_sc[i]
            for bi in seqs:
                s_new = s_old[bi] * row(w_ref, bi) + sa[bi] * row(b_ref, bi) + vk[RW_HEAD * bi:RW_HEAD * (bi + 1)]
                s_sc[bi] = s_new
                st_ref[bi, i] = s_new
            return 0

        lax.fori_loop(0, tc, step, 0, unroll=8)
        put_y(_seg_multi([s_sc[bi] * r_ref[bi, tc - 1:tc, :] for bi in seqs], ones, 1), tc - 1)
        for bi in seqs:
            y_ref[bi] = _unfold8(y_sc, bi, tc)

    vec = pl.BlockSpec((bsz, tc, RW_WIDTH), lambda c: (0, c, 0))
    return pl.pallas_call(
        body, name="scan_fwd", grid=(t // tc,),
        in_specs=[vec] * 5 + [pl.BlockSpec((bsz, 1, RW_HEAD, LANES), lambda c: (0, c, 0, 0))],
        out_specs=[vec, pl.BlockSpec((bsz, tc, RW_HEAD, RW_WIDTH), lambda c: (0, c, 0, 0))],
        out_shape=[_sds((bsz, t, RW_WIDTH)), _sds((bsz, t, RW_HEAD, RW_WIDTH))],
        scratch_shapes=[pltpu.VMEM((bsz, RW_HEAD, RW_WIDTH), F32), pltpu.VMEM((tc, bsz * RW_HEAD, RW_WIDTH), F32),
                        pltpu.VMEM((bsz * RW_WIDTH // LANES, tc * 8, LANES), F32)],
        compiler_params=_ARB1,
    )(r, w, k, nkk, b, vt)


def _own_head_mask():
    row = lax.broadcasted_iota(jnp.int32, (RW_HEADS, RW_WIDTH), 0)
    lane = lax.broadcasted_iota(jnp.int32, (RW_HEADS, RW_WIDTH), 1)
    return row == (lane >> 6)


def _scan_bwd(r, w, k, vt, nkk, b, st, dyt):
    bsz, t, _ = r.shape
    tc = SCAN_CHUNK
    nc = t // tc

    def body(r_ref, w_ref, k_ref, n_ref, b_ref, vt_ref, dyt_ref, st_ref, halo_ref,
             dr_ref, dw_ref, dk_ref, dv_ref, dn_ref, db_ref, g_sc, dc_sc, v8_sc, dy8_sc, *part_scs):
        c = pl.program_id(0)

        @pl.when(c == 0)
        def _():
            g_sc[...] = jnp.zeros_like(g_sc)

        ones = _seg_ones()
        diag = _diag_mask()
        has_prev = jnp.where(c == nc - 1, 0.0, 1.0)
        seqs = range(bsz)
        _outer_chunk(dyt_ref, r_ref, dc_sc, _head_expand(), seqs)
        for bi in seqs:
            v8_sc[bi] = jnp.concatenate([vt_ref[bi, 0].T] * 2, axis=1)
            dy8_sc[bi] = jnp.concatenate([dyt_ref[bi, 0].T] * 2, axis=1)
        by_head = lambda sc, bi, i: sc.at[bi][pl.ds(i, RW_HEADS, stride=SCAN_CHUNK), :][:, 0:RW_HEAD].astype(BF16)
        dr_sc, dw_sc, dk_sc, dv_sc, dn_sc, db_sc = part_scs
        own = _own_head_mask()

        def step(i, s_p):
            static = isinstance(i, int)
            row = lambda ref, bi: ref[bi, i:i + 1, :] if static else ref[bi, pl.ds(i, 1), :]
            dr8 = [jnp.dot(by_head(dy8_sc, bi, i), st_ref[bi, i].astype(BF16), preferred_element_type=F32) for bi in seqs]
            sa = _seg_multi([s_p[bi] * row(n_ref, bi) for bi in seqs], ones, 1)
            dc_all = dc_sc[i]
            dc = [dc_all[RW_HEAD * bi:RW_HEAD * (bi + 1)] for bi in seqs]
            g = [g_sc[bi] + dc[bi] for bi in seqs]
            res = _seg_multi([g[bi] * row(b_ref, bi) for bi in seqs] + [g[bi] * row(k_ref, bi) for bi in seqs], ones, 1)
            dsa, dvb = res[:bsz], res[bsz:]
            for bi in seqs:
                dk8 = jnp.dot(by_head(v8_sc, bi, i), g[bi].astype(BF16), preferred_element_type=F32)
                _put8(dr_sc, bi, i, jnp.where(own, dr8[bi], 0.0))
                _put8(dk_sc, bi, i, jnp.where(own, dk8, 0.0))
                _put8(dv_sc, bi, i, _fold8(dvb[bi] * diag))
                _put8(dw_sc, bi, i, _fold8(g[bi] * s_p[bi]))
                _put8(db_sc, bi, i, _fold8(g[bi] * sa[bi]))
                _put8(dn_sc, bi, i, _fold8(s_p[bi] * dsa[bi]))
                g_sc[bi] = g[bi] * row(w_ref, bi) + dsa[bi] * row(n_ref, bi)

        def loop_step(ii, _):
            i = tc - 1 - ii
            step(i, [st_ref[bi, i - 1] for bi in seqs])
            return 0

        lax.fori_loop(0, tc - 1, loop_step, 0, unroll=5)
        step(0, [halo_ref[bi, 0] * has_prev for bi in seqs])
        for out_ref, sc in zip((dr_ref, dw_ref, dk_ref, dv_ref, dn_ref, db_ref), part_scs):
            for bi in seqs:
                out_ref[bi] = _unfold8(sc, bi, tc)

    vec = pl.BlockSpec((bsz, tc, RW_WIDTH), lambda c: (0, nc - 1 - c, 0))
    tmin = pl.BlockSpec((bsz, 1, RW_HEAD, LANES), lambda c: (0, nc - 1 - c, 0, 0))
    parts = pltpu.VMEM((bsz * RW_WIDTH // LANES, tc * 8, LANES), F32)
    heads_steps = pltpu.VMEM((bsz, LANES, LANES), F32)
    return pl.pallas_call(
        body, name="scan_bwd", grid=(nc,),
        in_specs=[vec] * 5 + [tmin, tmin,
                              pl.BlockSpec((bsz, tc, RW_HEAD, RW_WIDTH), lambda c: (0, nc - 1 - c, 0, 0)),
                              pl.BlockSpec((bsz, 1, RW_HEAD, RW_WIDTH), lambda c: (0, jnp.maximum((nc - 1 - c) * tc - 1, 0), 0, 0))],
        out_specs=[vec] * 6,
        out_shape=[_sds((bsz, t, RW_WIDTH))] * 6,
        scratch_shapes=[pltpu.VMEM((bsz, RW_HEAD, RW_WIDTH), F32), pltpu.VMEM((tc, bsz * RW_HEAD, RW_WIDTH), F32),
                        heads_steps, heads_steps] + [parts] * 6,
        compiler_params=_ARB1,
    )(r, w, k, nkk, b, vt, dyt, st, st)


TOKEN_TILE = 256


def _padded_weights(wt):
    f = lambda a: a.astype(F32)
    w_in = wt["w_in"][0].astype(BF16)
    zeros = lambda r, c: jnp.zeros((r, c), F32)
    wp = jnp.concatenate([w_in[:, :MLA_COLS], jnp.zeros((D_MODEL, PM_W - MLA_COLS), BF16), w_in[:, MLA_COLS:]], axis=1)
    w_uq = f(wt["mla_w_uq"][0]).reshape(Q_LORA, MLA_HEADS, MLA_NOPE + MLA_ROPE)
    wqn = w_uq[:, :, :MLA_NOPE].reshape(Q_LORA, MLA_HEADS * MLA_NOPE)
    wqr = jnp.concatenate([w_uq[:, :, MLA_NOPE:], jnp.zeros((Q_LORA, MLA_HEADS, LANES - MLA_ROPE), F32)], axis=2)
    wqr = wqr.reshape(Q_LORA, MLA_HEADS * LANES)
    w2p = jnp.concatenate([f(wt["rw_w2"][0]), zeros(LORA, RW_WIDTH)], axis=0)
    a2p = jnp.concatenate([zeros(LORA, RW_WIDTH), f(wt["rw_a2"][0])], axis=0)
    bw = (f(wt["mla_q_norm_g"]), wqn, wqr, f(wt["mla_kv_norm_g"]), f(wt["mla_w_ukv"][0]), f(wt["rw_mu"]), f(wt["rw_w0"]),
          w2p, f(wt["rw_a0"]), a2p, f(wt["rw_k_k"]), f(wt["rw_k_a"]))
    hw = (f(wt["rw_ln_g"]), f(wt["rw_ln_b"]), f(wt["rw_r_k"]).reshape(1, RW_WIDTH), f(wt["w_out"][0]), f(wt["norm_post_g"]))
    return wp, bw, hw


def _local_step(x, positions, target, wt):
    bsz, t, _ = x.shape
    n = bsz * t
    tm = min(TOKEN_TILE, t)
    tps = t // tm
    wp, bw, hw = _padded_weights(wt)
    wpb = wp.astype(BF16)
    g_pre = wt["norm_pre_g"].astype(F32)
    x2 = x.reshape(n, D_MODEL)
    tgt2 = target.reshape(n, D_MODEL)
    inv_freq = ROPE_THETA ** (-jnp.arange(0, MLA_ROPE, 2, dtype=F32) / MLA_ROPE)
    invf = jnp.tile(inv_freq, LANES // (MLA_ROPE // 2)).reshape(LANES, 1)
    cos, sin = _rope_tables(positions.reshape(1, n), invf, tm)

    u, pm, prw, z = _fwd_a(x2, g_pre, wpb, tm)
    qn, qr, kv, kr, r, w, k, v, nkk, b = _fwd_b(pm, prw, cos, sin, bw, tm, tps)
    b3 = lambda a: a.reshape(bsz, t, a.shape[-1])
    ym, lse = _attn_fwd(b3(qn), b3(qr), b3(kv), b3(kr))
    vt = _time_minor(b3(v))
    ys, st = _scan_fwd(b3(r), b3(w), b3(k), vt, b3(nkk), b3(b))
    (dys, dr_h, dk_h, dv_h, dym, dz, dxres, loss, d_lng, d_lnb, d_rk, d_wout, d_gpost) = _head(
        ys.reshape(n, RW_WIDTH), r, k, v, ym.reshape(n, MLA_WIDTH), z, x2, tgt2, hw, tm)
    dqn, dqr, dkv, dkr_heads = _attn_bwd(b3(qn), b3(qr), b3(kv), b3(kr), ym, b3(dym), lse)
    dr_s, dw_s, dk_s, dv_s, dn_s, db_s = _scan_bwd(b3(r), b3(w), b3(k), vt, b3(nkk), b3(b), st, _time_minor(b3(dys)))
    f2 = lambda a: a.reshape(n, a.shape[-1])
    cts = (f2(dqn), f2(dqr), f2(dkv), f2(dr_s), dr_h, f2(dw_s), f2(dk_s), dk_h, f2(dv_s), dv_h, f2(dn_s), f2(db_s))
    (dpm, dprw, dps, d_gq, d_wqn, d_wqr, d_gkv, d_wkv, d_mu, d_w0, d_w2p, d_a0, d_a2p, d_kk, d_ka) = _bwd_b(
        pm, prw, cos, sin, bw, cts, dkr_heads.reshape(MLA_HEADS, n, LANES), tm, tps)
    grad_x, dpb, d_gpre = _bwd_a(x2, g_pre, wpb, dpm, dprw, dps, dz, dxres, tm, tps)
    d_wp = _dw_in(u, dpb, min(1024, n), 640)

    d_w_in = jnp.concatenate([d_wp[:, :MLA_COLS], d_wp[:, PM_W:]], axis=1)
    d_w_uq = jnp.concatenate([d_wqn.reshape(Q_LORA, MLA_HEADS, MLA_NOPE),
                              d_wqr.reshape(Q_LORA, MLA_HEADS, LANES)[:, :, :MLA_ROPE]], axis=2)
    grads = {
        "norm_pre_g": d_gpre, "w_in": d_w_in[None], "mla_q_norm_g": d_gq,
        "mla_w_uq": d_w_uq.reshape(1, Q_LORA, MLA_HEADS * (MLA_NOPE + MLA_ROPE)), "mla_kv_norm_g": d_gkv,
        "mla_w_ukv": d_wkv[None], "rw_mu": d_mu, "rw_w0": d_w0, "rw_w2": d_w2p[None, :LORA], "rw_a0": d_a0,
        "rw_a2": d_a2p[None, LORA:], "rw_k_k": d_kk, "rw_k_a": d_ka, "rw_r_k": d_rk.reshape(1, RW_HEADS, RW_HEAD),
        "rw_ln_g": d_lng, "rw_ln_b": d_lnb, "w_out": d_wout[None], "norm_post_g": d_gpost,
    }
    return loss, grad_x.reshape(bsz, t, D_MODEL), grads


_MESH = pl.DeviceIdType.MESH


def _gather_shards(packed):
    rows, lanes = packed.shape

    def body(x_ref, out_ref, send_sems, recv_sems, local_sem):
        x, y, c = lax.axis_index("x"), lax.axis_index("y"), lax.axis_index("c")
        me, sibling = (x, y, c), (x, y, 1 - c)
        chips = [(1 - x, y), (x, 1 - y), (1 - x, 1 - y)]

        def slot(px, py, pc):
            return out_ref.at[4 * px + 2 * py + pc]

        def copy(k, block, to, src=None):
            return pltpu.make_async_remote_copy(
                src_ref=slot(*block) if src is None else src, dst_ref=slot(*block),
                send_sem=send_sems.at[k], recv_sem=recv_sems.at[k], device_id=to, device_id_type=_MESH)

        mine = pltpu.make_async_copy(x_ref, slot(*me), local_sem)
        mine.start()
        first = [copy(0, me, sibling, src=x_ref)]
        first += [copy(1 + j, me, (*chip, c), src=x_ref) for j, chip in enumerate(chips)]
        for cp in first:
            cp.start()
        passed = [copy(4 + j, (*chip, c), sibling) for j, chip in enumerate(chips)]
        for j, chip in enumerate(chips):
            copy(1 + j, (*chip, c), me).wait_recv()
            passed[j].start()
        copy(0, sibling, me).wait_recv()
        for j, chip in enumerate(chips):
            copy(4 + j, (*chip, 1 - c), me).wait_recv()
        for cp in first + passed:
            cp.wait_send()
        mine.wait()

    return pl.pallas_call(
        body, name="gather_shards",
        out_shape=_sds((N_DEV, rows, lanes), packed.dtype),
        in_specs=[pl.BlockSpec(memory_space=pltpu.VMEM)],
        out_specs=pl.BlockSpec(memory_space=pltpu.VMEM),
        scratch_shapes=[pltpu.SemaphoreType.DMA((7,)), pltpu.SemaphoreType.DMA((7,)), pltpu.SemaphoreType.DMA],
    )(packed)


SMALL_LANES = SMALL_N + LANES


def _exchange_grads(big_blocks, small_grads, loss_tile):
    nb = len(big_blocks)
    ns = len(small_grads)

    def body(*refs):
        big, small, loss_ref = refs[:nb], refs[nb:nb + ns], refs[nb + ns]
        rbig, rsmall = refs[nb + ns + 1:2 * nb + ns + 1], refs[2 * nb + ns + 1]
        send_b, recv_b, send_s, recv_s, local_sems, row_sc = refs[2 * nb + ns + 2:]
        x, y, c = lax.axis_index("x"), lax.axis_index("y"), lax.axis_index("c")
        me_lin = 4 * x + 2 * y + c
        mine = [pltpu.make_async_copy(big[j].at[me_lin], rbig[j].at[0], local_sems.at[j]) for j in range(nb)]
        for cp in mine:
            cp.start()
        off = 0
        for ref, (_, cnt) in zip(small, SMALL):
            row_sc[:, off:off + cnt] = ref[...]
            off += cnt
        row_sc[:, off:off + LANES] = loss_ref[0:1, :]
        rsmall[me_lin] = row_sc[...]
        copies = []
        for k in range(1, N_DEV):
            px, py, pc = x ^ (k >> 2), y ^ ((k >> 1) & 1), c ^ (k & 1)
            peer = (px, py, pc)
            for j in range(nb):
                copies.append(pltpu.make_async_remote_copy(
                    src_ref=big[j].at[4 * px + 2 * py + pc], dst_ref=rbig[j].at[k],
                    send_sem=send_b.at[k - 1, j], recv_sem=recv_b.at[k - 1, j], device_id=peer, device_id_type=_MESH))
            copies.append(pltpu.make_async_remote_copy(
                src_ref=row_sc, dst_ref=rsmall.at[me_lin],
                send_sem=send_s.at[k - 1], recv_sem=recv_s.at[k - 1], device_id=peer, device_id_type=_MESH))
        for cp in copies:
            cp.start()
        for cp in copies:
            cp.wait_recv()
        for cp in copies:
            cp.wait_send()
        for cp in mine:
            cp.wait()

    hbm, vmem = pl.BlockSpec(memory_space=pl.ANY), pl.BlockSpec(memory_space=pltpu.VMEM)
    return pl.pallas_call(
        body, name="exchange_grads",
        out_shape=[_sds(a.shape, a.dtype) for a in big_blocks] + [_sds((N_DEV, 1, SMALL_LANES))],
        in_specs=[hbm] * nb + [vmem] * (ns + 1),
        out_specs=[hbm] * nb + [vmem],
        scratch_shapes=[pltpu.SemaphoreType.DMA((N_DEV - 1, nb)), pltpu.SemaphoreType.DMA((N_DEV - 1, nb)),
                        pltpu.SemaphoreType.DMA((N_DEV - 1,)), pltpu.SemaphoreType.DMA((N_DEV - 1,)),
                        pltpu.SemaphoreType.DMA((nb,)), pltpu.VMEM((1, SMALL_LANES), F32)],
    )(*big_blocks, *small_grads, loss_tile)


def _adamw_math(w, g, m, v):
    m = ADAM_B1 * m + (1.0 - ADAM_B1) * g
    v = ADAM_B2 * v + (1.0 - ADAM_B2) * (g * g)
    m_hat = m / (1.0 - ADAM_B1 ** ADAM_STEP)
    v_hat = v / (1.0 - ADAM_B2 ** ADAM_STEP)
    return -ADAM_LR * (m_hat / (jnp.sqrt(v_hat) + ADAM_EPS) + ADAM_WD * w), m, v


def _reduce_adamw(name, parts, w, m, v, row_blocks):
    _, rows, cols = parts.shape
    rb = rows // row_blocks

    def body(p_ref, w_ref, m_ref, v_ref, g_out, d_out, m_out, v_out):
        g = p_ref[0].astype(F32)
        for s in range(1, N_DEV):
            g = g + p_ref[s].astype(F32)
        g_out[0] = g
        d_out[0], m_out[0], v_out[0] = _adamw_math(w_ref[0], g, m_ref[0], v_ref[0])

    blk = pl.BlockSpec((1, rb, cols), lambda i: (0, i, 0))
    return pl.pallas_call(
        body, name="reduce_adamw_" + name, grid=(row_blocks,),
        in_specs=[pl.BlockSpec((N_DEV, rb, cols), lambda i: (0, i, 0)), blk, blk, blk],
        out_specs=[blk] * 4, out_shape=[_sds((1, rows, cols))] * 4,
        compiler_params=_ARB1,
    )(parts, w, m, v)


def _reduce_adamw_small(rows, ws, ms, vs):
    ns = len(SMALL)

    def body(r_ref, *refs):
        w_refs, m_refs, v_refs, outs = refs[:ns], refs[ns:2 * ns], refs[2 * ns:3 * ns], refs[3 * ns:]
        total = r_ref[0]
        for s in range(1, N_DEV):
            total = total + r_ref[s]
        off = 0
        for j, (_, cnt) in enumerate(SMALL):
            g = total[:, off:off + cnt]
            off += cnt
            outs[4 * j][...] = g
            outs[4 * j + 1][...], outs[4 * j + 2][...], outs[4 * j + 3][...] = _adamw_math(
                w_refs[j][...], g, m_refs[j][...], v_refs[j][...])
        outs[4 * ns][...] = total[:, off:off + LANES]

    vmem = pl.BlockSpec(memory_space=pltpu.VMEM)
    return pl.pallas_call(
        body, name="reduce_adamw_small",
        in_specs=[vmem] * (1 + 3 * ns), out_specs=[vmem] * (4 * ns + 1),
        out_shape=[_sds((1, cnt)) for _, cnt in SMALL for _ in range(4)] + [_sds((1, LANES))],
    )(rows, *ws, *ms, *vs)


def _pack_rows(arrs):
    return jnp.concatenate([a.reshape(-1, LANES) for a in arrs], axis=0)


def _shard_blocks(name, full):
    a = full[0]
    rows, cols = a.shape
    if name == "w_out":
        return a.reshape(N_DEV, rows // N_DEV, cols)
    return a.reshape(rows, N_DEV, cols // N_DEV).transpose(1, 0, 2)


def _unshard(name, blocks, shard_shape):
    _, rows, cols = shard_shape
    a = blocks.reshape(N_DEV, rows, cols)
    if name == "w_out":
        return a.reshape(1, N_DEV * rows, cols)
    return a.transpose(1, 0, 2).reshape(1, rows, N_DEV * cols)


def kernel(x, positions, norm_pre_g, w_in, mla_q_norm_g, mla_w_uq, mla_kv_norm_g, mla_w_ukv, rw_mu, rw_w0, rw_w2, rw_a0, rw_a2, rw_k_k, rw_k_a, rw_r_k, rw_ln_g, rw_ln_b, w_out, norm_post_g, loss_target, m_norm_pre_g, m_w_in, m_mla_q_norm_g, m_mla_w_uq, m_mla_kv_norm_g, m_mla_w_ukv, m_rw_mu, m_rw_w0, m_rw_w2, m_rw_a0, m_rw_a2, m_rw_k_k, m_rw_k_a, m_rw_r_k, m_rw_ln_g, m_rw_ln_b, m_w_out, m_norm_post_g, v_norm_pre_g, v_w_in, v_mla_q_norm_g, v_mla_w_uq, v_mla_kv_norm_g, v_mla_w_ukv, v_rw_mu, v_rw_w0, v_rw_w2, v_rw_a0, v_rw_a2, v_rw_k_k, v_rw_k_a, v_rw_r_k, v_rw_ln_g, v_rw_ln_b, v_w_out, v_norm_post_g):
    given = dict(locals())
    w = {nm: given[nm] for nm in WEIGHTS}
    mom = {nm: given["m_" + nm] for nm in WEIGHTS}
    var = {nm: given["v_" + nm] for nm in WEIGHTS}
    sharded = [nm for nm, _ in SHARD_ROWS]

    gathered = _gather_shards(_pack_rows([w[nm] for nm in sharded]).astype(BF16))
    full, off = dict(w), 0
    for nm, cnt in SHARD_ROWS:
        full[nm] = _unshard(nm, gathered[:, off:off + cnt], w[nm].shape)
        off += cnt

    loss_part, grad_x, grads = _local_step(x, positions, loss_target, full)

    small_names = [nm for nm, _ in SMALL]
    row = lambda a: a.reshape(1, -1)
    got = _exchange_grads([_shard_blocks(nm, grads[nm]).astype(BF16) for nm in sharded],
                          [row(grads[nm]) for nm in small_names], loss_part)
    new = {}
    for nm, parts in zip(sharded, got[:-1]):
        new[nm] = _reduce_adamw(nm, parts, w[nm], mom[nm], var[nm], 4 if nm == "w_in" else 1)
    res = _reduce_adamw_small(got[-1], [row(w[nm]) for nm in small_names], [row(mom[nm]) for nm in small_names],
                              [row(var[nm]) for nm in small_names])
    for j, nm in enumerate(small_names):
        new[nm] = tuple(a.reshape(w[nm].shape) for a in res[4 * j:4 * j + 4])
    loss = res[-1][0, 0]
    return (loss, grad_x, *[new[nm][j] for j in range(4) for nm in WEIGHTS])
```

```python
import functools

import jax
import jax.numpy as jnp
from jax import lax
from jax.experimental import pallas as pl
from jax.experimental.pallas import tpu as pltpu

F32 = jnp.float32
BF16 = jnp.bfloat16

D_MODEL = 1024
MLA_HEADS = 4
MLA_NOPE = 128
MLA_ROPE = 64
MLA_V = 128
MLA_WIDTH = MLA_HEADS * MLA_V
Q_LORA = 256
KV_LORA = 128
ROPE_THETA = 10000.0
RW_HEAD = 64
RW_WIDTH = 512
RW_HEADS = RW_WIDTH // RW_HEAD
LORA = 64
RW_COLS = 3 * RW_WIDTH + 2 * LORA
MLA_COLS = Q_LORA + KV_LORA + MLA_ROPE
D_IN = MLA_COLS + RW_COLS + D_MODEL
RW_GN_EPS = 64e-5
NORM_EPS = 1e-6
ATT_SCALE = (MLA_NOPE + MLA_ROPE) ** -0.5
ADAM_LR, ADAM_B1, ADAM_B2, ADAM_EPS, ADAM_WD, ADAM_STEP = 0.001, 0.9, 0.999, 1e-08, 0.01, 10
N_DEV = 8
LANES = 128
MXU = 256

PM_W = 512
WP_COLS = PM_W + RW_COLS + D_MODEL
RW_PIECES = ((0, 512), (512, 1024), (1024, 1536), (1536, 1664))

SHARD_ROWS = (("w_in", 1024 * 392 // LANES), ("mla_w_uq", 256 * 96 // LANES), ("mla_w_ukv", 128 * 128 // LANES),
              ("rw_w2", 64 * 64 // LANES), ("rw_a2", 64 * 64 // LANES), ("w_out", 128 * 1024 // LANES))
PACK_ROWS = sum(r for _, r in SHARD_ROWS)
SMALL = (("norm_pre_g", 1024), ("mla_q_norm_g", 256), ("mla_kv_norm_g", 128), ("rw_mu", 1664), ("rw_w0", 512),
         ("rw_a0", 512), ("rw_k_k", 512), ("rw_k_a", 512), ("rw_r_k", 512), ("rw_ln_g", 512), ("rw_ln_b", 512),
         ("norm_post_g", 1024))
SMALL_N = sum(n for _, n in SMALL)
SMALL_ROWS = 72
WEIGHTS = ("norm_pre_g", "w_in", "mla_q_norm_g", "mla_w_uq", "mla_kv_norm_g", "mla_w_ukv", "rw_mu", "rw_w0", "rw_w2",
           "rw_a0", "rw_a2", "rw_k_k", "rw_k_a", "rw_r_k", "rw_ln_g", "rw_ln_b", "w_out", "norm_post_g")


def _seg_ones():
    r = lax.broadcasted_iota(jnp.int32, (MXU, MXU), 0) >> 6
    c = lax.broadcasted_iota(jnp.int32, (MXU, MXU), 1) >> 6
    return jnp.where(r == c, 1.0, 0.0).astype(BF16)


def _seg_dot(x, ones, passes):
    parts, rem = [], x
    for p in range(passes):
        hb = rem.astype(BF16)
        parts.append(hb)
        if p + 1 < passes:
            rem = rem - hb.astype(F32)
    outs = []
    for j in range(x.shape[1] // MXU):
        acc = None
        for hb in parts:
            d = jnp.dot(hb[:, MXU * j:MXU * (j + 1)], ones, preferred_element_type=F32)
            acc = d if acc is None else acc + d
        outs.append(acc)
    return outs[0] if len(outs) == 1 else jnp.concatenate(outs, axis=1)


def _seg_multi(xs, ones, passes):
    his = [x.astype(BF16) for x in xs]
    hi = jnp.concatenate(his, axis=0)
    if passes == 2:
        lo = jnp.concatenate([(x - h.astype(F32)).astype(BF16) for x, h in zip(xs, his)], axis=0)
        rhs = jnp.concatenate([ones, ones], axis=0)
    halves = []
    for j in range(hi.shape[1] // MXU):
        sl = slice(MXU * j, MXU * (j + 1))
        if passes == 2:
            halves.append(jnp.dot(jnp.concatenate([hi[:, sl], lo[:, sl]], axis=1), rhs, preferred_element_type=F32))
        else:
            halves.append(jnp.dot(hi[:, sl], ones, preferred_element_type=F32))
    full = jnp.concatenate(halves, axis=1)
    m = xs[0].shape[0]
    return [full[m * i:m * (i + 1)] for i in range(len(xs))]


@jax.custom_vjp
def _segsum(x):
    return _seg_dot(x, _seg_ones(), 2)


_segsum.defvjp(lambda x: (_segsum(x), None), lambda _, g: (_segsum(g),))


@jax.custom_vjp
def _bdot(a, w):
    return jnp.dot(a.astype(BF16), w.astype(BF16), preferred_element_type=F32)


def _bdot_fwd(a, w):
    return _bdot(a, w), (a, w)


def _bdot_bwd(res, g):
    a, w = res
    gb = g.astype(BF16)
    da = lax.dot_general(gb, w.astype(BF16), (((1,), (1,)), ((), ())), preferred_element_type=F32)
    dw = lax.dot_general(a.astype(BF16), gb, (((0,), (0,)), ((), ())), preferred_element_type=F32)
    return da, dw


_bdot.defvjp(_bdot_fwd, _bdot_bwd)


def _rot_impl(x):
    w = x.shape[1]
    lane = lax.broadcasted_iota(jnp.int32, x.shape, 1)
    return jnp.where((lane & 63) < 32, -pltpu.roll(x, w - 32, 1), pltpu.roll(x, 32, 1))


@jax.custom_vjp
def _rot(x):
    return _rot_impl(x)


_rot.defvjp(lambda x: (_rot_impl(x), None), lambda _, g: (-_rot_impl(g),))


def _rms(x, g):
    return x * lax.rsqrt(jnp.mean(x * x, axis=-1, keepdims=True) + NORM_EPS) * g


def _shift_rows(p, prev_row):
    row = lax.broadcasted_iota(jnp.int32, p.shape, 0)
    return jnp.where(row == 0, prev_row, pltpu.roll(p, 1, 0))


def _unshift_rows(g, next_row):
    row = lax.broadcasted_iota(jnp.int32, g.shape, 0)
    return jnp.where(row == g.shape[0] - 1, next_row, pltpu.roll(g, g.shape[0] - 1, 0))


def _f_mla(cq, ckv, kr, cos, sin, g_q, wqn, wqr, g_kv, wkv):
    qn = _rms(cq, g_q)
    q_nope = _bdot(qn, wqn)
    q_r = _bdot(qn, wqr)
    cos4 = jnp.concatenate([cos] * MLA_HEADS, axis=1)
    sin4 = jnp.concatenate([sin] * MLA_HEADS, axis=1)
    q_rope = q_r * cos4 + _rot(q_r) * sin4
    kv = _bdot(_rms(ckv, g_kv), wkv)
    k_rope = kr * cos + _rot(kr) * sin
    return q_nope, q_rope, kv, k_rope


def _f_rw(pr, pk, pv, pt, sr, sk, sv, st, mu_r, mu_k, mu_v, mu_t, w0, w2p, a0, a2p, k_k, k_a):
    r = pr + (sr - pr) * mu_r
    k = pk + (sk - pk) * mu_k
    v = pv + (sv - pv) * mu_v
    t = pt + (st - pt) * mu_t
    nwl = -(w0 + _bdot(jnp.tanh(t), w2p))
    softplus = jnp.maximum(nwl, 0.0) + jnp.log(1.0 + jnp.exp(-jnp.abs(nwl)))
    decay = jnp.exp(-jnp.exp(-softplus - 0.5))
    a = jax.nn.sigmoid(a0 + _bdot(t, a2p))
    kk = k * k_k
    kk = kk / jnp.maximum(jnp.sqrt(_segsum(kk * kk)), 1e-12)
    k2 = k * (1.0 + (a - 1.0) * k_a)
    return r, decay, k2, v, -kk, kk * a


def _f_head(ys, r, k, v, ym, z1, z2, x, tgt, ln_g, ln_b, r_k, w1, w2, g_post):
    inv = 1.0 / RW_HEAD
    yc = ys - _segsum(ys) * inv
    var = _segsum(yc * yc) * inv
    y = yc * lax.rsqrt(var + RW_GN_EPS) * ln_g + ln_b
    y_rw = y + _segsum(r * k * r_k) * v
    c1 = ym * (z1 * jax.nn.sigmoid(z1))
    c2 = y_rw * (z2 * jax.nn.sigmoid(z2))
    out = _bdot(c1, w1) + _bdot(c2, w2)
    err = x + _rms(out, g_post) - tgt
    per_row = jnp.sum(err * err, axis=1, keepdims=True)
    return jnp.sum(per_row, axis=0, keepdims=True) * (0.5 / D_MODEL)


def _rows(tm, width):
    return pl.BlockSpec((tm, width), lambda i: (i, 0))


def _whole(shape):
    zeros = (0,) * len(shape)
    return pl.BlockSpec(shape, lambda i: zeros)


def _sds(shape, dtype=F32):
    return jax.ShapeDtypeStruct(shape, dtype)


_ARB1 = pltpu.CompilerParams(dimension_semantics=("arbitrary",))


def _acc(ref, val, first):
    @pl.when(first)
    def _():
        ref[...] = val

    @pl.when(jnp.logical_not(first))
    def _():
        ref[...] += val


def _fwd_a(x2, g_pre, wp, tm):
    n = x2.shape[0]

    def body(x_ref, g_ref, w_ref, u_ref, pm_ref, prw_ref, z_ref):
        ub = _rms(x_ref[...], g_ref[...]).astype(BF16)
        u_ref[...] = ub
        pm_ref[...] = jnp.dot(ub, w_ref[:, 0:PM_W], preferred_element_type=F32)
        prw_ref[...] = jnp.dot(ub, w_ref[:, PM_W:PM_W + RW_COLS], preferred_element_type=F32)
        z_ref[...] = jnp.dot(ub, w_ref[:, PM_W + RW_COLS:WP_COLS], preferred_element_type=F32)

    return pl.pallas_call(
        body, name="fwd_a", grid=(n // tm,),
        in_specs=[_rows(tm, D_MODEL), _whole((1, D_MODEL)), _whole((D_MODEL, WP_COLS))],
        out_specs=[_rows(tm, D_MODEL), _rows(tm, PM_W), _rows(tm, RW_COLS), _rows(tm, D_MODEL)],
        out_shape=[_sds((n, D_MODEL), BF16), _sds((n, PM_W)), _sds((n, RW_COLS)), _sds((n, D_MODEL))],
        compiler_params=_ARB1,
    )(x2, g_pre, wp)


def _rope_tables(pos_row, invf_col, tm):
    n = pos_row.shape[1]

    def body(p_ref, f_ref, c_ref, s_ref):
        ang = f_ref[...] * p_ref[...].astype(F32)
        c_ref[...] = jnp.cos(ang).T
        s_ref[...] = jnp.sin(ang).T

    return pl.pallas_call(
        body, name="rope_tables", grid=(n // tm,),
        in_specs=[pl.BlockSpec((1, tm), lambda i: (0, i)), _whole((LANES, 1))],
        out_specs=[_rows(tm, LANES), _rows(tm, LANES)],
        out_shape=[_sds((n, LANES)), _sds((n, LANES))],
        compiler_params=_ARB1,
    )(pos_row, invf_col)


_B_WEIGHT_SHAPES = ((1, Q_LORA), (Q_LORA, 512), (Q_LORA, 512), (1, KV_LORA), (KV_LORA, 1024), (1, RW_COLS), (1, RW_WIDTH),
                    (LANES, RW_WIDTH), (1, RW_WIDTH), (LANES, RW_WIDTH), (1, RW_WIDTH), (1, RW_WIDTH))


def _halo_prev(tm):
    return pl.BlockSpec((8, RW_COLS), lambda i: (jnp.maximum(i * (tm // 8) - 1, 0), 0))


def _b_operands(pm_ref, prw_ref, halo_ref, wrefs, tile, tiles_per_seq):
    g_q, wqn, wqr, g_kv, wkv, mu, w0, w2p, a0, a2p, k_k, k_a = wrefs
    mla_in = (pm_ref[:, 0:Q_LORA], pm_ref[:, Q_LORA:Q_LORA + KV_LORA], pm_ref[:, Q_LORA + KV_LORA:PM_W])
    mla_w = (g_q[...], wqn[...], wqr[...], g_kv[...], wkv[...])
    keep = jnp.where(tile % tiles_per_seq == 0, 0.0, 1.0)
    prev = halo_ref[7:8, :] * keep
    ps = tuple(prw_ref[:, a:b] for a, b in RW_PIECES)
    ss = tuple(_shift_rows(p, prev[:, a:b]) for p, (a, b) in zip(ps, RW_PIECES))
    rw_w = tuple(mu[:, a:b] for a, b in RW_PIECES) + (w0[...], w2p[...], a0[...], a2p[...], k_k[...], k_a[...])
    return mla_in, mla_w, ps + ss, rw_w


def _fwd_b(pm, prw, cos, sin, bw, tm, tiles_per_seq):
    n = pm.shape[0]

    def body(pm_ref, prw_ref, halo_ref, cos_ref, sin_ref, *refs):
        wrefs, outs = refs[:12], refs[12:]
        mla_in, mla_w, rw_in, rw_w = _b_operands(pm_ref, prw_ref, halo_ref, wrefs, pl.program_id(0), tiles_per_seq)
        res = _f_mla(*mla_in, cos_ref[...], sin_ref[...], *mla_w) + _f_rw(*rw_in, *rw_w)
        for o_ref, val in zip(outs, res):
            o_ref[...] = val.astype(o_ref.dtype)

    widths = (512, 512, 1024, LANES) + (RW_WIDTH,) * 6
    return pl.pallas_call(
        body, name="fwd_b", grid=(n // tm,),
        in_specs=[_rows(tm, PM_W), _rows(tm, RW_COLS), _halo_prev(tm), _rows(tm, LANES), _rows(tm, LANES)]
        + [_whole(s) for s in _B_WEIGHT_SHAPES],
        out_specs=[_rows(tm, w) for w in widths],
        out_shape=[_sds((n, w), BF16 if j < 4 else F32) for j, w in enumerate(widths)],
        compiler_params=_ARB1,
    )(pm, prw, prw, cos, sin, *bw)


def _bwd_b(pm, prw, cos, sin, bw, cts, dkr_heads, tm, tiles_per_seq):
    n = pm.shape[0]

    ct_widths = (512, 512, 1024) + (RW_WIDTH,) * 9
    n_ct = len(ct_widths)

    def body(pm_ref, prw_ref, halo_ref, cos_ref, sin_ref, *refs):
        wrefs, ct_refs, dkr_ref = refs[:12], refs[12:12 + n_ct], refs[12 + n_ct]
        dpm_ref, dprw_ref, dps_ref = refs[13 + n_ct:16 + n_ct]
        wg_refs = refs[16 + n_ct:]
        tile = pl.program_id(0)
        first = tile == 0
        mla_in, mla_w, rw_in, rw_w = _b_operands(pm_ref, prw_ref, halo_ref, wrefs, tile, tiles_per_seq)
        cos, sin = cos_ref[...], sin_ref[...]
        ct = [r[...] for r in ct_refs]
        _, vjp_mla = jax.vjp(lambda *a: _f_mla(*a[:3], cos, sin, *a[3:]), *mla_in, *mla_w)
        dkr = dkr_ref[0] + dkr_ref[1] + dkr_ref[2] + dkr_ref[3]
        d_mla = vjp_mla((ct[0], ct[1], ct[2], dkr))
        dpm_ref[:, 0:Q_LORA] = d_mla[0]
        dpm_ref[:, Q_LORA:Q_LORA + KV_LORA] = d_mla[1]
        dpm_ref[:, Q_LORA + KV_LORA:PM_W] = d_mla[2]
        _, vjp_rw = jax.vjp(_f_rw, *rw_in, *rw_w)
        d_rw = vjp_rw((ct[3] + ct[4], ct[5], ct[6] + ct[7], ct[8] + ct[9], ct[10], ct[11]))
        for j, (a, b) in enumerate(RW_PIECES):
            dprw_ref[:, a:b] = d_rw[j]
            dps_ref[:, a:b] = d_rw[4 + j]
        g_q, wqn, wqr, g_kv, wkv, mu, w0, w2p, a0, a2p, k_k, k_a = wg_refs
        for ref, val in zip((g_q, wqn, wqr, g_kv, wkv), d_mla[3:]):
            _acc(ref, val, first)
        for j, (a, b) in enumerate(RW_PIECES):
            _acc(mu.at[:, a:b], d_rw[8 + j], first)
        for ref, val in zip((w0, w2p, a0, a2p, k_k, k_a), d_rw[12:]):
            _acc(ref, val, first)

    return pl.pallas_call(
        body, name="bwd_b", grid=(n // tm,),
        in_specs=[_rows(tm, PM_W), _rows(tm, RW_COLS), _halo_prev(tm), _rows(tm, LANES), _rows(tm, LANES)]
        + [_whole(s) for s in _B_WEIGHT_SHAPES] + [_rows(tm, w) for w in ct_widths]
        + [pl.BlockSpec((MLA_HEADS, tm, LANES), lambda i: (0, i, 0))],
        out_specs=[_rows(tm, PM_W), _rows(tm, RW_COLS), _rows(tm, RW_COLS)] + [_whole(s) for s in _B_WEIGHT_SHAPES],
        out_shape=[_sds((n, PM_W)), _sds((n, RW_COLS)), _sds((n, RW_COLS))] + [_sds(s) for s in _B_WEIGHT_SHAPES],
        compiler_params=_ARB1,
    )(pm, prw, prw, cos, sin, *bw, *cts, dkr_heads)


def _head(ys, r, k, v, ym, z, x2, tgt, hw, tm):
    n = x2.shape[0]
    h_shapes = ((1, RW_WIDTH), (1, RW_WIDTH), (1, RW_WIDTH), (D_MODEL, D_MODEL), (1, D_MODEL))

    def body(ys_ref, r_ref, k_ref, v_ref, ym_ref, z_ref, x_ref, t_ref, lng, lnb, rk, wout, gpost,
             dys_ref, dr_ref, dk_ref, dv_ref, dym_ref, dz_ref, dx_ref, loss_ref, dlng, dlnb, drk, dwout, dgpost):
        first = pl.program_id(0) == 0
        tgt_v = t_ref[...]
        args = (ys_ref[...], r_ref[...], k_ref[...], v_ref[...], ym_ref[...], z_ref[:, 0:MLA_WIDTH], z_ref[:, MLA_WIDTH:D_MODEL],
                x_ref[...], lng[...], lnb[...], rk[...], wout[0:MLA_WIDTH, :], wout[MLA_WIDTH:D_MODEL, :], gpost[...])
        loss, vjp = jax.vjp(lambda *a: _f_head(*a[:8], tgt_v, *a[8:]), *args)
        d = vjp(jnp.ones((1, 1), F32))
        dys_ref[...] = d[0]
        dr_ref[...] = d[1]
        dk_ref[...] = d[2]
        dv_ref[...] = d[3]
        dym_ref[...] = d[4].astype(BF16)
        dz_ref[:, 0:MLA_WIDTH] = d[5]
        dz_ref[:, MLA_WIDTH:D_MODEL] = d[6]
        dx_ref[...] = d[7]
        _acc(loss_ref, jnp.broadcast_to(loss, (8, LANES)), first)
        _acc(dlng, d[8], first)
        _acc(dlnb, d[9], first)
        _acc(drk, d[10], first)
        _acc(dwout.at[0:MLA_WIDTH, :], d[11], first)
        _acc(dwout.at[MLA_WIDTH:D_MODEL, :], d[12], first)
        _acc(dgpost, d[13], first)

    widths = (RW_WIDTH,) * 4 + (MLA_WIDTH, D_MODEL, D_MODEL)
    return pl.pallas_call(
        body, name="head", grid=(n // tm,),
        in_specs=[_rows(tm, RW_WIDTH)] * 4 + [_rows(tm, MLA_WIDTH), _rows(tm, D_MODEL), _rows(tm, D_MODEL), _rows(tm, D_MODEL)]
        + [_whole(s) for s in h_shapes],
        out_specs=[_rows(tm, w) for w in widths] + [_whole((8, LANES))] + [_whole(s) for s in h_shapes],
        out_shape=[_sds((n, w), BF16 if j == 4 else F32) for j, w in enumerate(widths)] + [_sds((8, LANES))]
        + [_sds(s) for s in h_shapes],
        compiler_params=_ARB1,
    )(ys, r, k, v, ym, z, x2, tgt, *hw)


def _halo_next(tm, n):
    last = n // 8 - 1
    return pl.BlockSpec((8, RW_COLS), lambda i: (jnp.minimum((i + 1) * (tm // 8), last), 0))


def _bwd_a(x2, g_pre, wp, dpm, dprw, dps, dz, dxres, tm, tiles_per_seq):
    n = x2.shape[0]
    nt_dims = (((1,), (1,)), ((), ()))

    def body(x_ref, g_ref, w_ref, dpm_ref, dprw_ref, dps_ref, nxt_ref, dz_ref, dxres_ref, gx_ref, dpb_ref, dg_ref):
        tile = pl.program_id(0)
        keep = jnp.where((tile + 1) % tiles_per_seq == 0, 0.0, 1.0)
        dprw_v = dprw_ref[...] + _unshift_rows(dps_ref[...], nxt_ref[0:1, :] * keep)
        dpm_b, dprw_b, dz_b = dpm_ref[...].astype(BF16), dprw_v.astype(BF16), dz_ref[...].astype(BF16)
        dpb_ref[:, 0:PM_W] = dpm_b
        dpb_ref[:, PM_W:PM_W + RW_COLS] = dprw_b
        dpb_ref[:, PM_W + RW_COLS:WP_COLS] = dz_b
        du = (lax.dot_general(dpm_b, w_ref[:, 0:PM_W], nt_dims, preferred_element_type=F32)
              + lax.dot_general(dprw_b, w_ref[:, PM_W:PM_W + RW_COLS], nt_dims, preferred_element_type=F32)
              + lax.dot_general(dz_b, w_ref[:, PM_W + RW_COLS:WP_COLS], nt_dims, preferred_element_type=F32))
        x = x_ref[...]
        xhat = x * lax.rsqrt(jnp.mean(x * x, axis=-1, keepdims=True) + NORM_EPS)
        dxn = du * g_ref[...]
        dx = (dxn - xhat * jnp.mean(dxn * xhat, axis=-1, keepdims=True)) * lax.rsqrt(jnp.mean(x * x, axis=-1, keepdims=True) + NORM_EPS)
        gx_ref[...] = dx + dxres_ref[...]
        _acc(dg_ref, jnp.sum(du * xhat, axis=0, keepdims=True), tile == 0)

    return pl.pallas_call(
        body, name="bwd_a", grid=(n // tm,),
        in_specs=[_rows(tm, D_MODEL), _whole((1, D_MODEL)), _whole((D_MODEL, WP_COLS)), _rows(tm, PM_W), _rows(tm, RW_COLS),
                  _rows(tm, RW_COLS), _halo_next(tm, n), _rows(tm, D_MODEL), _rows(tm, D_MODEL)],
        out_specs=[_rows(tm, D_MODEL), _rows(tm, WP_COLS), _whole((1, D_MODEL))],
        out_shape=[_sds((n, D_MODEL)), _sds((n, WP_COLS), BF16), _sds((1, D_MODEL))],
        compiler_params=_ARB1,
    )(x2, g_pre, wp, dpm, dprw, dps, dps, dz, dxres)


def _dw_in(u, dpb, tk, tn):
    n = u.shape[0]
    steps = n // tk

    def body(u_ref, d_ref, o_ref, acc_sc):
        k = pl.program_id(1)
        _acc(acc_sc, lax.dot_general(u_ref[...], d_ref[...], _TN, preferred_element_type=F32), k == 0)

        @pl.when(k == steps - 1)
        def _():
            o_ref[...] = acc_sc[...].astype(BF16)

    return pl.pallas_call(
        body, name="dw_in", grid=(WP_COLS // tn, steps),
        in_specs=[pl.BlockSpec((tk, D_MODEL), lambda j, k: (k, 0)), pl.BlockSpec((tk, tn), lambda j, k: (k, j))],
        out_specs=pl.BlockSpec((D_MODEL, tn), lambda j, k: (0, j)),
        out_shape=_sds((D_MODEL, WP_COLS), BF16),
        scratch_shapes=[pltpu.VMEM((D_MODEL, tn), F32)],
        compiler_params=pltpu.CompilerParams(dimension_semantics=("arbitrary", "arbitrary")),
    )(u, dpb)


ATT_BLK = 256
ATT_BWD_Q = 256
_NT = (((1,), (1,)), ((), ()))
_TN = (((0,), (0,)), ((), ()))


def _causal(q0, k0, blk, blk_k=None):
    blk_k = blk if blk_k is None else blk_k
    row = q0 + lax.broadcasted_iota(jnp.int32, (blk, blk_k), 0)
    col = k0 + lax.broadcasted_iota(jnp.int32, (blk, blk_k), 1)
    return row >= col


def _attn_fwd(qn, qr, kv, kr):
    bsz, t, _ = qn.shape
    blk = min(ATT_BLK, t)

    heads = range(MLA_HEADS)

    def body(qn_ref, qr_ref, kv_ref, kr_ref, o_ref, lse_ref):
        qi = pl.program_id(1)
        q = [jnp.concatenate([qn_ref[:, LANES * h:LANES * (h + 1)], qr_ref[:, LANES * h:LANES * (h + 1)]], axis=1) for h in heads]
        lower = _causal(0, 0, blk)

        def kv_step(j, carry, diagonal):
            ks = pl.multiple_of(j * blk, blk)
            k_rope = kr_ref[pl.ds(ks, blk), :]
            def score(h):
                k = jnp.concatenate([kv_ref[pl.ds(ks, blk), 2 * LANES * h:2 * LANES * h + LANES], k_rope], axis=1)
                return lax.dot_general(q[h], k, _NT, preferred_element_type=F32)

            out = []
            nxt = score(0)
            for h in heads:
                s = nxt * ATT_SCALE
                if h + 1 < MLA_HEADS:
                    nxt = score(h + 1)
                m, l, acc = carry[h]
                if diagonal:
                    s = jnp.where(lower, s, -1e30)
                m_new = jnp.maximum(m, jnp.max(s, axis=1, keepdims=True))
                alpha = jnp.exp(m - m_new)
                p = jnp.exp(s - m_new)
                l = alpha * l + jnp.sum(p, axis=1, keepdims=True)
                v = kv_ref[pl.ds(ks, blk), 2 * LANES * h + LANES:2 * LANES * (h + 1)]
                out.append((m_new, l, alpha * acc + jnp.dot(p.astype(BF16), v, preferred_element_type=F32)))
            return tuple(out)

        one = (jnp.full((blk, 1), -1e30, F32), jnp.zeros((blk, 1), F32), jnp.zeros((blk, MLA_V), F32))
        carry = lax.fori_loop(0, qi, lambda j, c: kv_step(j, c, False), (one,) * MLA_HEADS)
        carry = kv_step(qi, carry, True)
        for h in heads:
            m, l, acc = carry[h]
            o_ref[:, LANES * h:LANES * (h + 1)] = acc / l
            lse_ref[h] = jnp.broadcast_to(m + jnp.log(l), (blk, LANES))

    return pl.pallas_call(
        body, name="attn_fwd", grid=(bsz, t // blk),
        in_specs=[pl.BlockSpec((None, blk, MLA_WIDTH), lambda b, i: (b, i, 0)),
                  pl.BlockSpec((None, blk, MLA_WIDTH), lambda b, i: (b, i, 0)),
                  pl.BlockSpec((None, t, 2 * MLA_WIDTH), lambda b, i: (b, 0, 0)),
                  pl.BlockSpec((None, t, LANES), lambda b, i: (b, 0, 0))],
        out_specs=[pl.BlockSpec((None, blk, MLA_WIDTH), lambda b, i: (b, i, 0)),
                   pl.BlockSpec((None, MLA_HEADS, blk, LANES), lambda b, i: (b, 0, i, 0))],
        out_shape=[_sds((bsz, t, MLA_WIDTH)), _sds((bsz, MLA_HEADS, t, LANES))],
        compiler_params=pltpu.CompilerParams(dimension_semantics=("arbitrary", "arbitrary")),
    )(qn, qr, kv, kr)


def _attn_bwd(qn, qr, kv, kr, o, do, lse):
    bsz, t, _ = qn.shape
    blk = min(ATT_BLK, t)
    nb = t // blk
    bq = ATT_BWD_Q

    def body(qn_ref, qr_ref, kn_ref, kr_ref, v_ref, o_ref, do_ref, lse_ref, dqn_ref, dqr_ref, dkv_ref, dkr_ref, dq_sc, delta_sc):
        dq_sc[...] = jnp.zeros_like(dq_sc)
        delta_sc[...] = jnp.sum(do_ref[...].astype(F32) * o_ref[...], axis=1, keepdims=True)

        def kv_loop(j, _):
            ks = pl.multiple_of(j * blk, blk)
            k = jnp.concatenate([kn_ref[pl.ds(ks, blk), :], kr_ref[pl.ds(ks, blk), :]], axis=1)
            vb = v_ref[pl.ds(ks, blk), :]

            def q_pair(pair, carry):
                dk, dv = carry
                subs = range(2)
                qs = [pl.multiple_of((2 * pair + sub) * bq, bq) for sub in subs]
                q = [jnp.concatenate([qn_ref[pl.ds(qs[u], bq), :], qr_ref[pl.ds(qs[u], bq), :]], axis=1) for u in subs]
                dob = [do_ref[pl.ds(qs[u], bq), :] for u in subs]

                def scores(u):
                    return lax.dot_general(q[u], k, _NT, preferred_element_type=F32)

                def probs(u, s, dp):
                    p = jnp.where(_causal((2 * pair + u) * bq, j * blk, bq, blk),
                                  jnp.exp(s * ATT_SCALE - lse_ref[pl.ds(qs[u], bq), 0:1]), 0.0)
                    return p.astype(BF16), (p * (dp - delta_sc[pl.ds(qs[u], bq), :]) * ATT_SCALE).astype(BF16)

                s = [scores(u) for u in subs]
                dp = [lax.dot_general(dob[u], vb, _NT, preferred_element_type=F32) for u in subs]
                pd = [probs(u, s[u], dp[u]) for u in subs]
                for u in subs:
                    dv = dv + lax.dot_general(pd[u][0], dob[u], _TN, preferred_element_type=F32)
                for u in subs:
                    dq_sc[pl.ds(qs[u], bq), :] += jnp.dot(pd[u][1], k, preferred_element_type=F32)
                    dk = dk + lax.dot_general(pd[u][1], q[u], _TN, preferred_element_type=F32)
                return dk, dv

            first = (j * blk) // (2 * bq)
            dk, dv = lax.fori_loop(first, t // (2 * bq), q_pair, (jnp.zeros((blk, 2 * LANES), F32), jnp.zeros((blk, MLA_V), F32)))
            dkv_ref[pl.ds(ks, blk), 0:LANES] = dk[:, 0:LANES]
            dkv_ref[pl.ds(ks, blk), LANES:2 * LANES] = dv
            dkr_ref[pl.ds(ks, blk), :] = dk[:, LANES:2 * LANES]
            return 0

        lax.fori_loop(0, nb, kv_loop, 0)
        dqn_ref[...] = dq_sc[:, 0:LANES]
        dqr_ref[...] = dq_sc[:, LANES:2 * LANES]

    head_col = lambda b, h: (b, 0, h)
    return pl.pallas_call(
        body, name="attn_bwd", grid=(bsz, MLA_HEADS),
        in_specs=[pl.BlockSpec((None, t, LANES), head_col), pl.BlockSpec((None, t, LANES), head_col),
                  pl.BlockSpec((None, t, LANES), lambda b, h: (b, 0, 2 * h)),
                  pl.BlockSpec((None, t, LANES), lambda b, h: (b, 0, 0)),
                  pl.BlockSpec((None, t, LANES), lambda b, h: (b, 0, 2 * h + 1)),
                  pl.BlockSpec((None, t, LANES), head_col), pl.BlockSpec((None, t, LANES), head_col),
                  pl.BlockSpec((None, None, t, LANES), lambda b, h: (b, h, 0, 0))],
        out_specs=[pl.BlockSpec((None, t, LANES), head_col), pl.BlockSpec((None, t, LANES), head_col),
                   pl.BlockSpec((None, t, 2 * LANES), head_col),
                   pl.BlockSpec((None, None, t, LANES), lambda b, h: (h, b, 0, 0))],
        out_shape=[_sds((bsz, t, MLA_WIDTH)), _sds((bsz, t, MLA_WIDTH)), _sds((bsz, t, 2 * MLA_WIDTH)),
                   _sds((MLA_HEADS, bsz, t, LANES))],
        scratch_shapes=[pltpu.VMEM((t, 2 * LANES), F32), pltpu.VMEM((t, 1), F32)],
        compiler_params=pltpu.CompilerParams(dimension_semantics=("arbitrary", "arbitrary")),
    )(qn, qr, kv, kr, kv, o, do, lse)


SCAN_CHUNK = 16


def _diag_mask():
    row = lax.broadcasted_iota(jnp.int32, (RW_HEAD, RW_WIDTH), 0)
    lane = lax.broadcasted_iota(jnp.int32, (RW_HEAD, RW_WIDTH), 1)
    return jnp.where(row == (lane & (RW_HEAD - 1)), 1.0, 0.0)


def _time_minor(a):
    bsz, t, _ = a.shape
    a = a.reshape(bsz, t // SCAN_CHUNK, SCAN_CHUNK, RW_HEADS, RW_HEAD)
    return a.transpose(0, 1, 4, 3, 2).reshape(bsz, t // SCAN_CHUNK, RW_HEAD, RW_HEADS * SCAN_CHUNK)


def _head_expand():
    l = lax.broadcasted_iota(jnp.int32, (2 * LANES, RW_WIDTH), 0)
    n = lax.broadcasted_iota(jnp.int32, (2 * LANES, RW_WIDTH), 1)
    return jnp.where(((l & (LANES - 1)) >> 4) == (n >> 6), 1.0, 0.0).astype(BF16)


BCAST_GROUP = 4


def _outer_chunk(tm_ref, row_ref, out_sc, expand, seqs):
    step_of_lane = lax.broadcasted_iota(jnp.int32, (RW_HEAD, LANES), 1) & (SCAN_CHUNK - 1)
    tiles = [tm_ref[bi, 0] for bi in seqs]
    for t0 in range(0, SCAN_CHUNK, BCAST_GROUP):
        parts = []
        for t in range(t0, t0 + BCAST_GROUP):
            for tile in tiles:
                a = jnp.where(step_of_lane == t, tile, 0.0)
                hi = a.astype(BF16)
                parts.append(jnp.concatenate([hi, (a - hi.astype(F32)).astype(BF16)], axis=1))
        cols = jnp.dot(jnp.concatenate(parts, axis=0), expand, preferred_element_type=F32)
        for j, t in enumerate(range(t0, t0 + BCAST_GROUP)):
            base = j * RW_HEAD * len(seqs)
            out_sc[t] = jnp.concatenate([cols[base + RW_HEAD * bi:base + RW_HEAD * (bi + 1)] * row_ref[bi, t:t + 1, :]
                                         for bi in seqs], axis=0)


def _fold8(x):
    acc = x[0:8]
    for j in range(1, x.shape[0] // 8):
        acc = acc + x[8 * j:8 * (j + 1)]
    return acc


def _rows8(at):
    return pl.ds(at * 8 if isinstance(at, int) else pl.multiple_of(at * 8, 8), 8)


def _put8(sc, bi, at, val):
    for j in range(RW_WIDTH // LANES):
        sc[bi * (RW_WIDTH // LANES) + j, _rows8(at), :] = val[:, LANES * j:LANES * (j + 1)]


def _unfold8(sc, bi, steps):
    tiles = []
    for j in range(RW_WIDTH // LANES):
        view = sc.at[bi * (RW_WIDTH // LANES) + j]
        acc = view[pl.ds(0, steps, stride=8), :]
        for s in range(1, 8):
            acc = acc + view[pl.ds(s, steps, stride=8), :]
        tiles.append(acc)
    return jnp.concatenate(tiles, axis=1)


def _scan_fwd(r, w, k, vt, nkk, b):
    bsz, t, _ = r.shape
    tc = SCAN_CHUNK

    def body(r_ref, w_ref, k_ref, n_ref, b_ref, vt_ref, y_ref, st_ref, s_sc, vc_sc, y_sc):
        @pl.when(pl.program_id(0) == 0)
        def _():
            s_sc[...] = jnp.zeros_like(s_sc)

        ones = _seg_ones()
        diag = _diag_mask()
        seqs = range(bsz)
        _outer_chunk(vt_ref, k_ref, vc_sc, _head_expand(), seqs)

        def put_y(ya, at):
            for bi in seqs:
                _put8(y_sc, bi, at, _fold8(ya[bi] * diag))

        def step(i, _):
            row = lambda ref, bi: ref[bi, pl.ds(i, 1), :]
            prev = jnp.maximum(i - 1, 0)
            s_old = [s_sc[bi] for bi in seqs]
            s_b = [s_old[bi].astype(BF16) for bi in seqs]
            sa = _seg_multi([s_b[bi] * row(n_ref, bi).astype(BF16) for bi in seqs], ones, 1)
            put_y(_seg_multi([s_b[bi] * r_ref[bi, pl.ds(prev, 1), :].astype(BF16) for bi in seqs], ones, 1), prev)
            vk = vc_sc[i]
            for bi in seqs:
                s_new = s_old[bi] * row(w_ref, bi) + sa[bi] * row(b_ref, bi) + vk[RW_HEAD * bi:RW_HEAD * (bi + 1)]
                s_sc[bi] = s_new
                st_ref[bi, i] = s_new
            return 0

        lax.fori_loop(0, tc, step, 0, unroll=8)
        put_y(_seg_multi([s_sc[bi].astype(BF16) * r_ref[bi, tc - 1:tc, :].astype(BF16) for bi in seqs], ones, 1), tc - 1)
        for bi in seqs:
            y_ref[bi] = _unfold8(y_sc, bi, tc)

    vec = pl.BlockSpec((bsz, tc, RW_WIDTH), lambda c: (0, c, 0))
    return pl.pallas_call(
        body, name="scan_fwd", grid=(t // tc,),
        in_specs=[vec] * 5 + [pl.BlockSpec((bsz, 1, RW_HEAD, LANES), lambda c: (0, c, 0, 0))],
        out_specs=[vec, pl.BlockSpec((bsz, tc, RW_HEAD, RW_WIDTH), lambda c: (0, c, 0, 0))],
        out_shape=[_sds((bsz, t, RW_WIDTH)), _sds((bsz, t, RW_HEAD, RW_WIDTH))],
        scratch_shapes=[pltpu.VMEM((bsz, RW_HEAD, RW_WIDTH), F32), pltpu.VMEM((tc, bsz * RW_HEAD, RW_WIDTH), F32),
                        pltpu.VMEM((bsz * RW_WIDTH // LANES, tc * 8, LANES), F32)],
        compiler_params=_ARB1,
    )(r, w, k, nkk, b, vt)


def _own_head_mask():
    row = lax.broadcasted_iota(jnp.int32, (RW_HEADS, RW_WIDTH), 0)
    lane = lax.broadcasted_iota(jnp.int32, (RW_HEADS, RW_WIDTH), 1)
    return row == (lane >> 6)


def _scan_bwd(r, w, k, vt, nkk, b, st, dyt):
    bsz, t, _ = r.shape
    tc = SCAN_CHUNK
    nc = t // tc

    def body(r_ref, w_ref, k_ref, n_ref, b_ref, vt_ref, dyt_ref, st_ref, halo_ref,
             dr_ref, dw_ref, dk_ref, dv_ref, dn_ref, db_ref, g_sc, dc_sc, v8_sc, dy8_sc, *part_scs):
        c = pl.program_id(0)

        @pl.when(c == 0)
        def _():
            g_sc[...] = jnp.zeros_like(g_sc)

        ones = _seg_ones()
        diag = _diag_mask()
        has_prev = jnp.where(c == nc - 1, 0.0, 1.0)
        seqs = range(bsz)
        _outer_chunk(dyt_ref, r_ref, dc_sc, _head_expand(), seqs)
        for bi in seqs:
            v8_sc[bi] = jnp.concatenate([vt_ref[bi, 0].T] * 2, axis=1)
            dy8_sc[bi] = jnp.concatenate([dyt_ref[bi, 0].T] * 2, axis=1)
        by_head = lambda sc, bi, i: sc.at[bi][pl.ds(i, RW_HEADS, stride=SCAN_CHUNK), :][:, 0:RW_HEAD].astype(BF16)
        dr_sc, dw_sc, dk_sc, dv_sc, dn_sc, db_sc = part_scs
        own = _own_head_mask()

        def step(i, s_p):
            static = isinstance(i, int)
            row = lambda ref, bi: ref[bi, i:i + 1, :] if static else ref[bi, pl.ds(i, 1), :]
            dr8 = [jnp.dot(by_head(dy8_sc, bi, i), st_ref[bi, i].astype(BF16), preferred_element_type=F32) for bi in seqs]
            rowb = lambda ref, bi: row(ref, bi).astype(BF16)
            sa = _seg_multi([s_p[bi].astype(BF16) * rowb(n_ref, bi) for bi in seqs], ones, 1)
            dc_all = dc_sc[i]
            dc = [dc_all[RW_HEAD * bi:RW_HEAD * (bi + 1)] for bi in seqs]
            g = [g_sc[bi] + dc[bi] for bi in seqs]
            g_b = [g[bi].astype(BF16) for bi in seqs]
            res = _seg_multi([g_b[bi] * rowb(b_ref, bi) for bi in seqs] + [g_b[bi] * rowb(k_ref, bi) for bi in seqs], ones, 1)
            dsa, dvb = res[:bsz], res[bsz:]
            for bi in seqs:
                dk8 = jnp.dot(by_head(v8_sc, bi, i), g_b[bi], preferred_element_type=F32)
                _put8(dr_sc, bi, i, jnp.where(own, dr8[bi], 0.0))
                _put8(dk_sc, bi, i, jnp.where(own, dk8, 0.0))
                _put8(dv_sc, bi, i, _fold8(dvb[bi] * diag))
                _put8(dw_sc, bi, i, _fold8(g[bi] * s_p[bi]))
                _put8(db_sc, bi, i, _fold8(g[bi] * sa[bi]))
                _put8(dn_sc, bi, i, _fold8(s_p[bi] * dsa[bi]))
                g_sc[bi] = g[bi] * row(w_ref, bi) + dsa[bi] * row(n_ref, bi)

        def loop_step(ii, _):
            i = tc - 1 - ii
            step(i, [st_ref[bi, i - 1] for bi in seqs])
            return 0

        lax.fori_loop(0, tc - 1, loop_step, 0, unroll=5)
        step(0, [halo_ref[bi, 0] * has_prev for bi in seqs])
        for out_ref, sc in zip((dr_ref, dw_ref, dk_ref, dv_ref, dn_ref, db_ref), part_scs):
            for bi in seqs:
                out_ref[bi] = _unfold8(sc, bi, tc)

    vec = pl.BlockSpec((bsz, tc, RW_WIDTH), lambda c: (0, nc - 1 - c, 0))
    tmin = pl.BlockSpec((bsz, 1, RW_HEAD, LANES), lambda c: (0, nc - 1 - c, 0, 0))
    parts = pltpu.VMEM((bsz * RW_WIDTH // LANES, tc * 8, LANES), F32)
    heads_steps = pltpu.VMEM((bsz, LANES, LANES), F32)
    return pl.pallas_call(
        body, name="scan_bwd", grid=(nc,),
        in_specs=[vec] * 5 + [tmin, tmin,
                              pl.BlockSpec((bsz, tc, RW_HEAD, RW_WIDTH), lambda c: (0, nc - 1 - c, 0, 0)),
                              pl.BlockSpec((bsz, 1, RW_HEAD, RW_WIDTH), lambda c: (0, jnp.maximum((nc - 1 - c) * tc - 1, 0), 0, 0))],
        out_specs=[vec] * 6,
        out_shape=[_sds((bsz, t, RW_WIDTH))] * 6,
        scratch_shapes=[pltpu.VMEM((bsz, RW_HEAD, RW_WIDTH), F32), pltpu.VMEM((tc, bsz * RW_HEAD, RW_WIDTH), F32),
                        heads_steps, heads_steps] + [parts] * 6,
        compiler_params=_ARB1,
    )(r, w, k, nkk, b, vt, dyt, st, st)


TOKEN_TILE = 256


def _padded_weights(wt):
    f = lambda a: a.astype(F32)
    w_in = wt["w_in"][0].astype(BF16)
    zeros = lambda r, c: jnp.zeros((r, c), F32)
    wp = jnp.concatenate([w_in[:, :MLA_COLS], jnp.zeros((D_MODEL, PM_W - MLA_COLS), BF16), w_in[:, MLA_COLS:]], axis=1)
    w_uq = f(wt["mla_w_uq"][0]).reshape(Q_LORA, MLA_HEADS, MLA_NOPE + MLA_ROPE)
    wqn = w_uq[:, :, :MLA_NOPE].reshape(Q_LORA, MLA_HEADS * MLA_NOPE)
    wqr = jnp.concatenate([w_uq[:, :, MLA_NOPE:], jnp.zeros((Q_LORA, MLA_HEADS, LANES - MLA_ROPE), F32)], axis=2)
    wqr = wqr.reshape(Q_LORA, MLA_HEADS * LANES)
    w2p = jnp.concatenate([f(wt["rw_w2"][0]), zeros(LORA, RW_WIDTH)], axis=0)
    a2p = jnp.concatenate([zeros(LORA, RW_WIDTH), f(wt["rw_a2"][0])], axis=0)
    bw = (f(wt["mla_q_norm_g"]), wqn, wqr, f(wt["mla_kv_norm_g"]), f(wt["mla_w_ukv"][0]), f(wt["rw_mu"]), f(wt["rw_w0"]),
          w2p, f(wt["rw_a0"]), a2p, f(wt["rw_k_k"]), f(wt["rw_k_a"]))
    hw = (f(wt["rw_ln_g"]), f(wt["rw_ln_b"]), f(wt["rw_r_k"]).reshape(1, RW_WIDTH), f(wt["w_out"][0]), f(wt["norm_post_g"]))
    return wp, bw, hw


def _local_step(x, positions, target, wt):
    bsz, t, _ = x.shape
    n = bsz * t
    tm = min(TOKEN_TILE, t)
    tps = t // tm
    wp, bw, hw = _padded_weights(wt)
    wpb = wp.astype(BF16)
    g_pre = wt["norm_pre_g"].astype(F32)
    x2 = x.reshape(n, D_MODEL)
    tgt2 = target.reshape(n, D_MODEL)
    inv_freq = ROPE_THETA ** (-jnp.arange(0, MLA_ROPE, 2, dtype=F32) / MLA_ROPE)
    invf = jnp.tile(inv_freq, LANES // (MLA_ROPE // 2)).reshape(LANES, 1)
    cos, sin = _rope_tables(positions.reshape(1, n), invf, tm)

    u, pm, prw, z = _fwd_a(x2, g_pre, wpb, tm)
    qn, qr, kv, kr, r, w, k, v, nkk, b = _fwd_b(pm, prw, cos, sin, bw, tm, tps)
    b3 = lambda a: a.reshape(bsz, t, a.shape[-1])
    ym, lse = _attn_fwd(b3(qn), b3(qr), b3(kv), b3(kr))
    vt = _time_minor(b3(v))
    ys, st = _scan_fwd(b3(r), b3(w), b3(k), vt, b3(nkk), b3(b))
    (dys, dr_h, dk_h, dv_h, dym, dz, dxres, loss, d_lng, d_lnb, d_rk, d_wout, d_gpost) = _head(
        ys.reshape(n, RW_WIDTH), r, k, v, ym.reshape(n, MLA_WIDTH), z, x2, tgt2, hw, tm)
    dqn, dqr, dkv, dkr_heads = _attn_bwd(b3(qn), b3(qr), b3(kv), b3(kr), ym, b3(dym), lse)
    dr_s, dw_s, dk_s, dv_s, dn_s, db_s = _scan_bwd(b3(r), b3(w), b3(k), vt, b3(nkk), b3(b), st, _time_minor(b3(dys)))
    f2 = lambda a: a.reshape(n, a.shape[-1])
    cts = (f2(dqn), f2(dqr), f2(dkv), f2(dr_s), dr_h, f2(dw_s), f2(dk_s), dk_h, f2(dv_s), dv_h, f2(dn_s), f2(db_s))
    (dpm, dprw, dps, d_gq, d_wqn, d_wqr, d_gkv, d_wkv, d_mu, d_w0, d_w2p, d_a0, d_a2p, d_kk, d_ka) = _bwd_b(
        pm, prw, cos, sin, bw, cts, dkr_heads.reshape(MLA_HEADS, n, LANES), tm, tps)
    grad_x, dpb, d_gpre = _bwd_a(x2, g_pre, wpb, dpm, dprw, dps, dz, dxres, tm, tps)
    d_wp = _dw_in(u, dpb, min(1024, n), 640)

    d_w_in = jnp.concatenate([d_wp[:, :MLA_COLS], d_wp[:, PM_W:]], axis=1)
    d_w_uq = jnp.concatenate([d_wqn.reshape(Q_LORA, MLA_HEADS, MLA_NOPE),
                              d_wqr.reshape(Q_LORA, MLA_HEADS, LANES)[:, :, :MLA_ROPE]], axis=2)
    grads = {
        "norm_pre_g": d_gpre, "w_in": d_w_in[None], "mla_q_norm_g": d_gq,
        "mla_w_uq": d_w_uq.reshape(1, Q_LORA, MLA_HEADS * (MLA_NOPE + MLA_ROPE)), "mla_kv_norm_g": d_gkv,
        "mla_w_ukv": d_wkv[None], "rw_mu": d_mu, "rw_w0": d_w0, "rw_w2": d_w2p[None, :LORA], "rw_a0": d_a0,
        "rw_a2": d_a2p[None, LORA:], "rw_k_k": d_kk, "rw_k_a": d_ka, "rw_r_k": d_rk.reshape(1, RW_HEADS, RW_HEAD),
        "rw_ln_g": d_lng, "rw_ln_b": d_lnb, "w_out": d_wout[None], "norm_post_g": d_gpost,
    }
    return loss, grad_x.reshape(bsz, t, D_MODEL), grads


_MESH = pl.DeviceIdType.MESH


def _gather_shards(packed):
    rows, lanes = packed.shape

    def body(x_ref, out_ref, send_sems, recv_sems, local_sem):
        x, y, c = lax.axis_index("x"), lax.axis_index("y"), lax.axis_index("c")
        me, sibling = (x, y, c), (x, y, 1 - c)
        chips = [(1 - x, y), (x, 1 - y), (1 - x, 1 - y)]

        def slot(px, py, pc):
            return out_ref.at[4 * px + 2 * py + pc]

        def copy(k, block, to, src=None):
            return pltpu.make_async_remote_copy(
                src_ref=slot(*block) if src is None else src, dst_ref=slot(*block),
                send_sem=send_sems.at[k], recv_sem=recv_sems.at[k], device_id=to, device_id_type=_MESH)

        mine = pltpu.make_async_copy(x_ref, slot(*me), local_sem)
        mine.start()
        first = [copy(0, me, sibling, src=x_ref)]
        first += [copy(1 + j, me, (*chip, c), src=x_ref) for j, chip in enumerate(chips)]
        for cp in first:
            cp.start()
        passed = [copy(4 + j, (*chip, c), sibling) for j, chip in enumerate(chips)]
        for j, chip in enumerate(chips):
            copy(1 + j, (*chip, c), me).wait_recv()
            passed[j].start()
        copy(0, sibling, me).wait_recv()
        for j, chip in enumerate(chips):
            copy(4 + j, (*chip, 1 - c), me).wait_recv()
        for cp in first + passed:
            cp.wait_send()
        mine.wait()

    return pl.pallas_call(
        body, name="gather_shards",
        out_shape=_sds((N_DEV, rows, lanes), packed.dtype),
        in_specs=[pl.BlockSpec(memory_space=pltpu.VMEM)],
        out_specs=pl.BlockSpec(memory_space=pltpu.VMEM),
        scratch_shapes=[pltpu.SemaphoreType.DMA((7,)), pltpu.SemaphoreType.DMA((7,)), pltpu.SemaphoreType.DMA],
    )(packed)


SMALL_LANES = SMALL_N + LANES


def _exchange_grads(big_blocks, small_grads, loss_tile):
    nb = len(big_blocks)
    ns = len(small_grads)

    def body(*refs):
        big, small, loss_ref = refs[:nb], refs[nb:nb + ns], refs[nb + ns]
        rbig, rsmall = refs[nb + ns + 1:2 * nb + ns + 1], refs[2 * nb + ns + 1]
        send_b, recv_b, send_s, recv_s, local_sems, row_sc = refs[2 * nb + ns + 2:]
        x, y, c = lax.axis_index("x"), lax.axis_index("y"), lax.axis_index("c")
        me_lin = 4 * x + 2 * y + c
        mine = [pltpu.make_async_copy(big[j].at[me_lin], rbig[j].at[0], local_sems.at[j]) for j in range(nb)]
        for cp in mine:
            cp.start()
        off = 0
        for ref, (_, cnt) in zip(small, SMALL):
            row_sc[:, off:off + cnt] = ref[...]
            off += cnt
        row_sc[:, off:off + LANES] = loss_ref[0:1, :]
        rsmall[me_lin] = row_sc[...]
        copies = []
        for k in range(1, N_DEV):
            px, py, pc = x ^ (k >> 2), y ^ ((k >> 1) & 1), c ^ (k & 1)
            peer = (px, py, pc)
            for j in range(nb):
                copies.append(pltpu.make_async_remote_copy(
                    src_ref=big[j].at[4 * px + 2 * py + pc], dst_ref=rbig[j].at[k],
                    send_sem=send_b.at[k - 1, j], recv_sem=recv_b.at[k - 1, j], device_id=peer, device_id_type=_MESH))
            copies.append(pltpu.make_async_remote_copy(
                src_ref=row_sc, dst_ref=rsmall.at[me_lin],
                send_sem=send_s.at[k - 1], recv_sem=recv_s.at[k - 1], device_id=peer, device_id_type=_MESH))
        for cp in copies:
            cp.start()
        for cp in copies:
            cp.wait_recv()
        for cp in copies:
            cp.wait_send()
        for cp in mine:
            cp.wait()

    hbm, vmem = pl.BlockSpec(memory_space=pl.ANY), pl.BlockSpec(memory_space=pltpu.VMEM)
    return pl.pallas_call(
        body, name="exchange_grads",
        out_shape=[_sds(a.shape, a.dtype) for a in big_blocks] + [_sds((N_DEV, 1, SMALL_LANES))],
        in_specs=[hbm] * nb + [vmem] * (ns + 1),
        out_specs=[hbm] * nb + [vmem],
        scratch_shapes=[pltpu.SemaphoreType.DMA((N_DEV - 1, nb)), pltpu.SemaphoreType.DMA((N_DEV - 1, nb)),
                        pltpu.SemaphoreType.DMA((N_DEV - 1,)), pltpu.SemaphoreType.DMA((N_DEV - 1,)),
                        pltpu.SemaphoreType.DMA((nb,)), pltpu.VMEM((1, SMALL_LANES), F32)],
    )(*big_blocks, *small_grads, loss_tile)


def _adamw_math(w, g, m, v):
    m = ADAM_B1 * m + (1.0 - ADAM_B1) * g
    v = ADAM_B2 * v + (1.0 - ADAM_B2) * (g * g)
    m_hat = m / (1.0 - ADAM_B1 ** ADAM_STEP)
    v_hat = v / (1.0 - ADAM_B2 ** ADAM_STEP)
    return -ADAM_LR * (m_hat / (jnp.sqrt(v_hat) + ADAM_EPS) + ADAM_WD * w), m, v


def _reduce_adamw(name, parts, w, m, v, row_blocks):
    _, rows, cols = parts.shape
    rb = rows // row_blocks

    def body(p_ref, w_ref, m_ref, v_ref, g_out, d_out, m_out, v_out):
        g = p_ref[0].astype(F32)
        for s in range(1, N_DEV):
            g = g + p_ref[s].astype(F32)
        g_out[0] = g
        d_out[0], m_out[0], v_out[0] = _adamw_math(w_ref[0], g, m_ref[0], v_ref[0])

    blk = pl.BlockSpec((1, rb, cols), lambda i: (0, i, 0))
    return pl.pallas_call(
        body, name="reduce_adamw_" + name, grid=(row_blocks,),
        in_specs=[pl.BlockSpec((N_DEV, rb, cols), lambda i: (0, i, 0)), blk, blk, blk],
        out_specs=[blk] * 4, out_shape=[_sds((1, rows, cols))] * 4,
        compiler_params=_ARB1,
    )(parts, w, m, v)


def _reduce_adamw_small(rows, ws, ms, vs):
    ns = len(SMALL)

    def body(r_ref, *refs):
        w_refs, m_refs, v_refs, outs = refs[:ns], refs[ns:2 * ns], refs[2 * ns:3 * ns], refs[3 * ns:]
        total = r_ref[0]
        for s in range(1, N_DEV):
            total = total + r_ref[s]
        off = 0
        for j, (_, cnt) in enumerate(SMALL):
            g = total[:, off:off + cnt]
            off += cnt
            outs[4 * j][...] = g
            outs[4 * j + 1][...], outs[4 * j + 2][...], outs[4 * j + 3][...] = _adamw_math(
                w_refs[j][...], g, m_refs[j][...], v_refs[j][...])
        outs[4 * ns][...] = total[:, off:off + LANES]

    vmem = pl.BlockSpec(memory_space=pltpu.VMEM)
    return pl.pallas_call(
        body, name="reduce_adamw_small",
        in_specs=[vmem] * (1 + 3 * ns), out_specs=[vmem] * (4 * ns + 1),
        out_shape=[_sds((1, cnt)) for _, cnt in SMALL for _ in range(4)] + [_sds((1, LANES))],
    )(rows, *ws, *ms, *vs)


def _pack_rows(arrs):
    return jnp.concatenate([a.reshape(-1, LANES) for a in arrs], axis=0)


def _shard_blocks(name, full):
    a = full[0]
    rows, cols = a.shape
    if name == "w_out":
        return a.reshape(N_DEV, rows // N_DEV, cols)
    return a.reshape(rows, N_DEV, cols // N_DEV).transpose(1, 0, 2)


def _unshard(name, blocks, shard_shape):
    _, rows, cols = shard_shape
    a = blocks.reshape(N_DEV, rows, cols)
    if name == "w_out":
        return a.reshape(1, N_DEV * rows, cols)
    return a.transpose(1, 0, 2).reshape(1, rows, N_DEV * cols)


def kernel(x, positions, norm_pre_g, w_in, mla_q_norm_g, mla_w_uq, mla_kv_norm_g, mla_w_ukv, rw_mu, rw_w0, rw_w2, rw_a0, rw_a2, rw_k_k, rw_k_a, rw_r_k, rw_ln_g, rw_ln_b, w_out, norm_post_g, loss_target, m_norm_pre_g, m_w_in, m_mla_q_norm_g, m_mla_w_uq, m_mla_kv_norm_g, m_mla_w_ukv, m_rw_mu, m_rw_w0, m_rw_w2, m_rw_a0, m_rw_a2, m_rw_k_k, m_rw_k_a, m_rw_r_k, m_rw_ln_g, m_rw_ln_b, m_w_out, m_norm_post_g, v_norm_pre_g, v_w_in, v_mla_q_norm_g, v_mla_w_uq, v_mla_kv_norm_g, v_mla_w_ukv, v_rw_mu, v_rw_w0, v_rw_w2, v_rw_a0, v_rw_a2, v_rw_k_k, v_rw_k_a, v_rw_r_k, v_rw_ln_g, v_rw_ln_b, v_w_out, v_norm_post_g):
    given = dict(locals())
    w = {nm: given[nm] for nm in WEIGHTS}
    mom = {nm: given["m_" + nm] for nm in WEIGHTS}
    var = {nm: given["v_" + nm] for nm in WEIGHTS}
    sharded = [nm for nm, _ in SHARD_ROWS]

    gathered = _gather_shards(_pack_rows([w[nm] for nm in sharded]).astype(BF16))
    full, off = dict(w), 0
    for nm, cnt in SHARD_ROWS:
        full[nm] = _unshard(nm, gathered[:, off:off + cnt], w[nm].shape)
        off += cnt

    loss_part, grad_x, grads = _local_step(x, positions, loss_target, full)

    small_names = [nm for nm, _ in SMALL]
    row = lambda a: a.reshape(1, -1)
    got = _exchange_grads([_shard_blocks(nm, grads[nm]).astype(BF16) for nm in sharded],
                          [row(grads[nm]) for nm in small_names], loss_part)
    new = {}
    for nm, parts in zip(sharded, got[:-1]):
        new[nm] = _reduce_adamw(nm, parts, w[nm], mom[nm], var[nm], 4 if nm == "w_in" else 1)
    res = _reduce_adamw_small(got[-1], [row(w[nm]) for nm in small_names], [row(mom[nm]) for nm in small_names],
                              [row(var[nm]) for nm in small_names])
    for j, nm in enumerate(small_names):
        new[nm] = tuple(a.reshape(w[nm].shape) for a in res[4 * j:4 * j + 4])
    loss = res[-1][0, 0]
    return (loss, grad_x, *[new[nm][j] for j in range(4) for nm in WEIGHTS])
```

```python
import functools

import jax
import jax.numpy as jnp
from jax import lax
from jax.experimental import pallas as pl
from jax.experimental.pallas import tpu as pltpu

F32 = jnp.float32
BF16 = jnp.bfloat16

D_MODEL = 1024
MLA_HEADS = 4
MLA_NOPE = 128
MLA_ROPE = 64
MLA_V = 128
MLA_WIDTH = MLA_HEADS * MLA_V
Q_LORA = 256
KV_LORA = 128
ROPE_THETA = 10000.0
RW_HEAD = 64
RW_WIDTH = 512
RW_HEADS = RW_WIDTH // RW_HEAD
LORA = 64
RW_COLS = 3 * RW_WIDTH + 2 * LORA
MLA_COLS = Q_LORA + KV_LORA + MLA_ROPE
D_IN = MLA_COLS + RW_COLS + D_MODEL
RW_GN_EPS = 64e-5
NORM_EPS = 1e-6
ATT_SCALE = (MLA_NOPE + MLA_ROPE) ** -0.5
ADAM_LR, ADAM_B1, ADAM_B2, ADAM_EPS, ADAM_WD, ADAM_STEP = 0.001, 0.9, 0.999, 1e-08, 0.01, 10
N_DEV = 8
LANES = 128
MXU = 256

PM_W = 512
WP_COLS = PM_W + RW_COLS + D_MODEL
RW_PIECES = ((0, 512), (512, 1024), (1024, 1536), (1536, 1664))

SHARDED = ("w_in", "mla_w_uq", "mla_w_ukv", "rw_w2", "rw_a2", "w_out")
SMALL = (("norm_pre_g", 1024), ("mla_q_norm_g", 256), ("mla_kv_norm_g", 128), ("rw_mu", 1664), ("rw_w0", 512),
         ("rw_a0", 512), ("rw_k_k", 512), ("rw_k_a", 512), ("rw_r_k", 512), ("rw_ln_g", 512), ("rw_ln_b", 512),
         ("norm_post_g", 1024))
SMALL_N = sum(n for _, n in SMALL)
WEIGHTS = ("norm_pre_g", "w_in", "mla_q_norm_g", "mla_w_uq", "mla_kv_norm_g", "mla_w_ukv", "rw_mu", "rw_w0", "rw_w2",
           "rw_a0", "rw_a2", "rw_k_k", "rw_k_a", "rw_r_k", "rw_ln_g", "rw_ln_b", "w_out", "norm_post_g")


def _seg_ones():
    r = lax.broadcasted_iota(jnp.int32, (MXU, MXU), 0) >> 6
    c = lax.broadcasted_iota(jnp.int32, (MXU, MXU), 1) >> 6
    return jnp.where(r == c, 1.0, 0.0).astype(BF16)


def _seg_dot(x, ones, passes):
    parts, rem = [], x
    for p in range(passes):
        hb = rem.astype(BF16)
        parts.append(hb)
        if p + 1 < passes:
            rem = rem - hb.astype(F32)
    outs = []
    for j in range(x.shape[1] // MXU):
        acc = None
        for hb in parts:
            d = jnp.dot(hb[:, MXU * j:MXU * (j + 1)], ones, preferred_element_type=F32)
            acc = d if acc is None else acc + d
        outs.append(acc)
    return outs[0] if len(outs) == 1 else jnp.concatenate(outs, axis=1)


def _seg_multi(xs, ones, passes):
    his = [x.astype(BF16) for x in xs]
    hi = jnp.concatenate(his, axis=0)
    if passes == 2:
        lo = jnp.concatenate([(x - h.astype(F32)).astype(BF16) for x, h in zip(xs, his)], axis=0)
        rhs = jnp.concatenate([ones, ones], axis=0)
    halves = []
    for j in range(hi.shape[1] // MXU):
        sl = slice(MXU * j, MXU * (j + 1))
        if passes == 2:
            halves.append(jnp.dot(jnp.concatenate([hi[:, sl], lo[:, sl]], axis=1), rhs, preferred_element_type=F32))
        else:
            halves.append(jnp.dot(hi[:, sl], ones, preferred_element_type=F32))
    full = jnp.concatenate(halves, axis=1)
    m = xs[0].shape[0]
    return [full[m * i:m * (i + 1)] for i in range(len(xs))]


@jax.custom_vjp
def _segsum(x):
    return _seg_dot(x, _seg_ones(), 2)


_segsum.defvjp(lambda x: (_segsum(x), None), lambda _, g: (_segsum(g),))


@jax.custom_vjp
def _bdot(a, w):
    return jnp.dot(a.astype(BF16), w.astype(BF16), preferred_element_type=F32)


def _bdot_fwd(a, w):
    return _bdot(a, w), (a, w)


def _bdot_bwd(res, g):
    a, w = res
    gb = g.astype(BF16)
    da = lax.dot_general(gb, w.astype(BF16), (((1,), (1,)), ((), ())), preferred_element_type=F32)
    dw = lax.dot_general(a.astype(BF16), gb, (((0,), (0,)), ((), ())), preferred_element_type=F32)
    return da, dw


_bdot.defvjp(_bdot_fwd, _bdot_bwd)


def _rot_impl(x):
    w = x.shape[1]
    lane = lax.broadcasted_iota(jnp.int32, x.shape, 1)
    return jnp.where((lane & 63) < 32, -pltpu.roll(x, w - 32, 1), pltpu.roll(x, 32, 1))


@jax.custom_vjp
def _rot(x):
    return _rot_impl(x)


_rot.defvjp(lambda x: (_rot_impl(x), None), lambda _, g: (-_rot_impl(g),))


def _rms(x, g):
    return x * lax.rsqrt(jnp.mean(x * x, axis=-1, keepdims=True) + NORM_EPS) * g


def _shift_rows(p, prev_row):
    row = lax.broadcasted_iota(jnp.int32, p.shape, 0)
    return jnp.where(row == 0, prev_row, pltpu.roll(p, 1, 0))


def _unshift_rows(g, next_row):
    row = lax.broadcasted_iota(jnp.int32, g.shape, 0)
    return jnp.where(row == g.shape[0] - 1, next_row, pltpu.roll(g, g.shape[0] - 1, 0))


def _f_mla(cq, ckv, kr, cos, sin, g_q, wqn, wqr, g_kv, wkv):
    qn = _rms(cq, g_q)
    q_nope = _bdot(qn, wqn)
    q_r = _bdot(qn, wqr)
    cos4 = jnp.concatenate([cos] * MLA_HEADS, axis=1)
    sin4 = jnp.concatenate([sin] * MLA_HEADS, axis=1)
    q_rope = q_r * cos4 + _rot(q_r) * sin4
    kv = _bdot(_rms(ckv, g_kv), wkv)
    k_rope = kr * cos + _rot(kr) * sin
    return q_nope, q_rope, kv, k_rope


def _f_rw(pr, pk, pv, pt, sr, sk, sv, st, mu_r, mu_k, mu_v, mu_t, w0, w2p, a0, a2p, k_k, k_a):
    r = pr + (sr - pr) * mu_r
    k = pk + (sk - pk) * mu_k
    v = pv + (sv - pv) * mu_v
    t = pt + (st - pt) * mu_t
    nwl = -(w0 + _bdot(jnp.tanh(t), w2p))
    softplus = jnp.maximum(nwl, 0.0) + jnp.log(1.0 + jnp.exp(-jnp.abs(nwl)))
    decay = jnp.exp(-jnp.exp(-softplus - 0.5))
    a = jax.nn.sigmoid(a0 + _bdot(t, a2p))
    kk = k * k_k
    kk = kk / jnp.maximum(jnp.sqrt(_segsum(kk * kk)), 1e-12)
    k2 = k * (1.0 + (a - 1.0) * k_a)
    return r, decay, k2, v, -kk, kk * a


def _f_head(ys, r, k, v, ym, z1, z2, x, tgt, ln_g, ln_b, r_k, w1, w2, g_post):
    inv = 1.0 / RW_HEAD
    yc = ys - _segsum(ys) * inv
    var = _segsum(yc * yc) * inv
    y = yc * lax.rsqrt(var + RW_GN_EPS) * ln_g + ln_b
    y_rw = y + _segsum(r * k * r_k) * v
    c1 = ym * (z1 * jax.nn.sigmoid(z1))
    c2 = y_rw * (z2 * jax.nn.sigmoid(z2))
    out = _bdot(c1, w1) + _bdot(c2, w2)
    err = x + _rms(out, g_post) - tgt
    per_row = jnp.sum(err * err, axis=1, keepdims=True)
    return jnp.sum(per_row, axis=0, keepdims=True) * (0.5 / D_MODEL)


def _rows(tm, width):
    return pl.BlockSpec((tm, width), lambda i: (i, 0))


def _whole(shape):
    zeros = (0,) * len(shape)
    return pl.BlockSpec(shape, lambda i: zeros)


def _sds(shape, dtype=F32):
    return jax.ShapeDtypeStruct(shape, dtype)


_ARB1 = pltpu.CompilerParams(dimension_semantics=("arbitrary",))


def _acc(ref, val, first):
    @pl.when(first)
    def _():
        ref[...] = val

    @pl.when(jnp.logical_not(first))
    def _():
        ref[...] += val


def _fwd_a(x2, g_pre, wp, tm):
    n = x2.shape[0]

    def body(x_ref, g_ref, w_ref, ut_ref, pm_ref, prw_ref, z_ref):
        u = _rms(x_ref[...], g_ref[...])
        ub = u.astype(BF16)
        ut_ref[...] = u.T.astype(BF16)
        pm_ref[...] = jnp.dot(ub, w_ref[:, 0:PM_W], preferred_element_type=F32)
        prw_ref[...] = jnp.dot(ub, w_ref[:, PM_W:PM_W + RW_COLS], preferred_element_type=F32)
        z_ref[...] = jnp.dot(ub, w_ref[:, PM_W + RW_COLS:WP_COLS], preferred_element_type=F32)

    return pl.pallas_call(
        body, name="fwd_a", grid=(n // tm,),
        in_specs=[_rows(tm, D_MODEL), _whole((1, D_MODEL)), _whole((D_MODEL, WP_COLS))],
        out_specs=[pl.BlockSpec((D_MODEL, tm), lambda i: (0, i)), _rows(tm, PM_W), _rows(tm, RW_COLS), _rows(tm, D_MODEL)],
        out_shape=[_sds((D_MODEL, n), BF16), _sds((n, PM_W)), _sds((n, RW_COLS)), _sds((n, D_MODEL))],
        compiler_params=_ARB1,
    )(x2, g_pre, wp)


def _rope_tables(pos_row, invf_col, tm):
    n = pos_row.shape[1]

    def body(p_ref, f_ref, c_ref, s_ref):
        ang = f_ref[...] * p_ref[...].astype(F32)
        c_ref[...] = jnp.cos(ang).T
        s_ref[...] = jnp.sin(ang).T

    return pl.pallas_call(
        body, name="rope_tables", grid=(n // tm,),
        in_specs=[pl.BlockSpec((1, tm), lambda i: (0, i)), _whole((LANES, 1))],
        out_specs=[_rows(tm, LANES), _rows(tm, LANES)],
        out_shape=[_sds((n, LANES)), _sds((n, LANES))],
        compiler_params=_ARB1,
    )(pos_row, invf_col)


_B_WEIGHT_SHAPES = ((1, Q_LORA), (Q_LORA, 512), (Q_LORA, 512), (1, KV_LORA), (KV_LORA, 1024), (1, RW_COLS), (1, RW_WIDTH),
                    (LANES, RW_WIDTH), (1, RW_WIDTH), (LANES, RW_WIDTH), (1, RW_WIDTH), (1, RW_WIDTH))


def _halo_prev(tm):
    return pl.BlockSpec((8, RW_COLS), lambda i: (jnp.maximum(i * (tm // 8) - 1, 0), 0))


def _b_operands(pm_ref, prw_ref, halo_ref, wrefs, tile, tiles_per_seq):
    g_q, wqn, wqr, g_kv, wkv, mu, w0, w2p, a0, a2p, k_k, k_a = wrefs
    mla_in = (pm_ref[:, 0:Q_LORA], pm_ref[:, Q_LORA:Q_LORA + KV_LORA], pm_ref[:, Q_LORA + KV_LORA:PM_W])
    mla_w = (g_q[...], wqn[...], wqr[...], g_kv[...], wkv[...])
    keep = jnp.where(tile % tiles_per_seq == 0, 0.0, 1.0)
    prev = halo_ref[7:8, :] * keep
    ps = tuple(prw_ref[:, a:b] for a, b in RW_PIECES)
    ss = tuple(_shift_rows(p, prev[:, a:b]) for p, (a, b) in zip(ps, RW_PIECES))
    rw_w = tuple(mu[:, a:b] for a, b in RW_PIECES) + (w0[...], w2p[...], a0[...], a2p[...], k_k[...], k_a[...])
    return mla_in, mla_w, ps + ss, rw_w


def _fwd_b(pm, prw, cos, sin, bw, tm, tiles_per_seq):
    n = pm.shape[0]

    def body(pm_ref, prw_ref, halo_ref, cos_ref, sin_ref, *refs):
        wrefs, outs = refs[:12], refs[12:]
        mla_in, mla_w, rw_in, rw_w = _b_operands(pm_ref, prw_ref, halo_ref, wrefs, pl.program_id(0), tiles_per_seq)
        res = _f_mla(*mla_in, cos_ref[...], sin_ref[...], *mla_w) + _f_rw(*rw_in, *rw_w)
        for o_ref, val in zip(outs, res):
            o_ref[...] = val.astype(o_ref.dtype)

    widths = (512, 512, 1024, LANES) + (RW_WIDTH,) * 6
    return pl.pallas_call(
        body, name="fwd_b", grid=(n // tm,),
        in_specs=[_rows(tm, PM_W), _rows(tm, RW_COLS), _halo_prev(tm), _rows(tm, LANES), _rows(tm, LANES)]
        + [_whole(s) for s in _B_WEIGHT_SHAPES],
        out_specs=[_rows(tm, w) for w in widths],
        out_shape=[_sds((n, w), BF16 if j < 4 else F32) for j, w in enumerate(widths)],
        compiler_params=_ARB1,
    )(pm, prw, prw, cos, sin, *bw)


def _bwd_b(pm, prw, cos, sin, bw, cts, dkr_heads, tm, tiles_per_seq):
    n = pm.shape[0]

    ct_widths = (512, 512, 1024) + (RW_WIDTH,) * 9
    n_ct = len(ct_widths)

    def body(pm_ref, prw_ref, halo_ref, cos_ref, sin_ref, *refs):
        wrefs, ct_refs, dkr_ref = refs[:12], refs[12:12 + n_ct], refs[12 + n_ct]
        dpm_ref, dprw_ref, dps_ref = refs[13 + n_ct:16 + n_ct]
        wg_refs = refs[16 + n_ct:]
        tile = pl.program_id(0)
        first = tile == 0
        mla_in, mla_w, rw_in, rw_w = _b_operands(pm_ref, prw_ref, halo_ref, wrefs, tile, tiles_per_seq)
        cos, sin = cos_ref[...], sin_ref[...]
        ct = [r[...] for r in ct_refs]
        _, vjp_mla = jax.vjp(lambda *a: _f_mla(*a[:3], cos, sin, *a[3:]), *mla_in, *mla_w)
        dkr = dkr_ref[0] + dkr_ref[1] + dkr_ref[2] + dkr_ref[3]
        d_mla = vjp_mla((ct[0], ct[1], ct[2], dkr))
        dpm_ref[:, 0:Q_LORA] = d_mla[0]
        dpm_ref[:, Q_LORA:Q_LORA + KV_LORA] = d_mla[1]
        dpm_ref[:, Q_LORA + KV_LORA:PM_W] = d_mla[2]
        _, vjp_rw = jax.vjp(_f_rw, *rw_in, *rw_w)
        d_rw = vjp_rw((ct[3] + ct[4], ct[5], ct[6] + ct[7], ct[8] + ct[9], ct[10], ct[11]))
        for j, (a, b) in enumerate(RW_PIECES):
            dprw_ref[:, a:b] = d_rw[j]
            dps_ref[:, a:b] = d_rw[4 + j]
        g_q, wqn, wqr, g_kv, wkv, mu, w0, w2p, a0, a2p, k_k, k_a = wg_refs
        for ref, val in zip((g_q, wqn, wqr, g_kv, wkv), d_mla[3:]):
            _acc(ref, val, first)
        for j, (a, b) in enumerate(RW_PIECES):
            _acc(mu.at[:, a:b], d_rw[8 + j], first)
        for ref, val in zip((w0, w2p, a0, a2p, k_k, k_a), d_rw[12:]):
            _acc(ref, val, first)

    return pl.pallas_call(
        body, name="bwd_b", grid=(n // tm,),
        in_specs=[_rows(tm, PM_W), _rows(tm, RW_COLS), _halo_prev(tm), _rows(tm, LANES), _rows(tm, LANES)]
        + [_whole(s) for s in _B_WEIGHT_SHAPES] + [_rows(tm, w) for w in ct_widths]
        + [pl.BlockSpec((MLA_HEADS, tm, LANES), lambda i: (0, i, 0))],
        out_specs=[_rows(tm, PM_W), _rows(tm, RW_COLS), _rows(tm, RW_COLS)] + [_whole(s) for s in _B_WEIGHT_SHAPES],
        out_shape=[_sds((n, PM_W)), _sds((n, RW_COLS)), _sds((n, RW_COLS))] + [_sds(s) for s in _B_WEIGHT_SHAPES],
        compiler_params=_ARB1,
    )(pm, prw, prw, cos, sin, *bw, *cts, dkr_heads)


def _head(ys, r, k, v, ym, z, x2, tgt, hw, tm):
    n = x2.shape[0]
    h_shapes = ((1, RW_WIDTH), (1, RW_WIDTH), (1, RW_WIDTH), (D_MODEL, D_MODEL), (1, D_MODEL))

    def body(ys_ref, r_ref, k_ref, v_ref, ym_ref, z_ref, x_ref, t_ref, lng, lnb, rk, wout, gpost,
             dys_ref, dr_ref, dk_ref, dv_ref, dym_ref, dz_ref, dx_ref, loss_ref, dlng, dlnb, drk, dwout, dgpost):
        first = pl.program_id(0) == 0
        tgt_v = t_ref[...]
        args = (ys_ref[...], r_ref[...], k_ref[...], v_ref[...], ym_ref[...], z_ref[:, 0:MLA_WIDTH], z_ref[:, MLA_WIDTH:D_MODEL],
                x_ref[...], lng[...], lnb[...], rk[...], wout[0:MLA_WIDTH, :], wout[MLA_WIDTH:D_MODEL, :], gpost[...])
        loss, vjp = jax.vjp(lambda *a: _f_head(*a[:8], tgt_v, *a[8:]), *args)
        d = vjp(jnp.ones((1, 1), F32))
        dys_ref[...] = d[0]
        dr_ref[...] = d[1]
        dk_ref[...] = d[2]
        dv_ref[...] = d[3]
        dym_ref[...] = d[4].astype(BF16)
        dz_ref[:, 0:MLA_WIDTH] = d[5]
        dz_ref[:, MLA_WIDTH:D_MODEL] = d[6]
        dx_ref[...] = d[7]
        _acc(loss_ref, jnp.broadcast_to(loss, (8, LANES)), first)
        _acc(dlng, d[8], first)
        _acc(dlnb, d[9], first)
        _acc(drk, d[10], first)
        _acc(dwout.at[0:MLA_WIDTH, :], d[11], first)
        _acc(dwout.at[MLA_WIDTH:D_MODEL, :], d[12], first)
        _acc(dgpost, d[13], first)

    widths = (RW_WIDTH,) * 4 + (MLA_WIDTH, D_MODEL, D_MODEL)
    return pl.pallas_call(
        body, name="head", grid=(n // tm,),
        in_specs=[_rows(tm, RW_WIDTH)] * 4 + [_rows(tm, MLA_WIDTH), _rows(tm, D_MODEL), _rows(tm, D_MODEL), _rows(tm, D_MODEL)]
        + [_whole(s) for s in h_shapes],
        out_specs=[_rows(tm, w) for w in widths] + [_whole((8, LANES))] + [_whole(s) for s in h_shapes],
        out_shape=[_sds((n, w), BF16 if j == 4 else F32) for j, w in enumerate(widths)] + [_sds((8, LANES))]
        + [_sds(s) for s in h_shapes],
        compiler_params=_ARB1,
    )(ys, r, k, v, ym, z, x2, tgt, *hw)


def _halo_next(tm, n):
    last = n // 8 - 1
    return pl.BlockSpec((8, RW_COLS), lambda i: (jnp.minimum((i + 1) * (tm // 8), last), 0))


def _bwd_a(x2, g_pre, wp, dpm, dprw, dps, dz, dxres, tm, tiles_per_seq):
    n = x2.shape[0]
    nt_dims = (((1,), (1,)), ((), ()))

    def body(x_ref, g_ref, w_ref, dpm_ref, dprw_ref, dps_ref, nxt_ref, dz_ref, dxres_ref, gx_ref, dpb_ref, dg_ref):
        tile = pl.program_id(0)
        keep = jnp.where((tile + 1) % tiles_per_seq == 0, 0.0, 1.0)
        dprw_v = dprw_ref[...] + _unshift_rows(dps_ref[...], nxt_ref[0:1, :] * keep)
        dpm_b, dprw_b, dz_b = dpm_ref[...].astype(BF16), dprw_v.astype(BF16), dz_ref[...].astype(BF16)
        dpb_ref[:, 0:PM_W] = dpm_b
        dpb_ref[:, PM_W:PM_W + RW_COLS] = dprw_b
        dpb_ref[:, PM_W + RW_COLS:WP_COLS] = dz_b
        du = (lax.dot_general(dpm_b, w_ref[:, 0:PM_W], nt_dims, preferred_element_type=F32)
              + lax.dot_general(dprw_b, w_ref[:, PM_W:PM_W + RW_COLS], nt_dims, preferred_element_type=F32)
              + lax.dot_general(dz_b, w_ref[:, PM_W + RW_COLS:WP_COLS], nt_dims, preferred_element_type=F32))
        x = x_ref[...]
        xhat = x * lax.rsqrt(jnp.mean(x * x, axis=-1, keepdims=True) + NORM_EPS)
        dxn = du * g_ref[...]
        dx = (dxn - xhat * jnp.mean(dxn * xhat, axis=-1, keepdims=True)) * lax.rsqrt(jnp.mean(x * x, axis=-1, keepdims=True) + NORM_EPS)
        gx_ref[...] = dx + dxres_ref[...]
        _acc(dg_ref, jnp.sum(du * xhat, axis=0, keepdims=True), tile == 0)

    return pl.pallas_call(
        body, name="bwd_a", grid=(n // tm,),
        in_specs=[_rows(tm, D_MODEL), _whole((1, D_MODEL)), _whole((D_MODEL, WP_COLS)), _rows(tm, PM_W), _rows(tm, RW_COLS),
                  _rows(tm, RW_COLS), _halo_next(tm, n), _rows(tm, D_MODEL), _rows(tm, D_MODEL)],
        out_specs=[_rows(tm, D_MODEL), _rows(tm, WP_COLS), _whole((1, D_MODEL))],
        out_shape=[_sds((n, D_MODEL)), _sds((n, WP_COLS), BF16), _sds((1, D_MODEL))],
        compiler_params=_ARB1,
    )(x2, g_pre, wp, dpm, dprw, dps, dps, dz, dxres)


def _dw_in(ut, dpb, tk, tn):
    n = ut.shape[1]
    steps = n // tk

    def body(u_ref, d_ref, o_ref, acc_sc):
        k = pl.program_id(1)
        _acc(acc_sc, jnp.dot(u_ref[...], d_ref[...], preferred_element_type=F32), k == 0)

        @pl.when(k == steps - 1)
        def _():
            o_ref[...] = acc_sc[...].astype(BF16)

    return pl.pallas_call(
        body, name="dw_in", grid=(WP_COLS // tn, steps),
        in_specs=[pl.BlockSpec((D_MODEL, tk), lambda j, k: (0, k)), pl.BlockSpec((tk, tn), lambda j, k: (k, j))],
        out_specs=pl.BlockSpec((D_MODEL, tn), lambda j, k: (0, j)),
        out_shape=_sds((D_MODEL, WP_COLS), BF16),
        scratch_shapes=[pltpu.VMEM((D_MODEL, tn), F32)],
        compiler_params=pltpu.CompilerParams(dimension_semantics=("arbitrary", "arbitrary")),
    )(ut, dpb)


ATT_BLK = 256
ATT_BWD_Q = 256
_NT = (((1,), (1,)), ((), ()))
_TN = (((0,), (0,)), ((), ()))


def _causal(q0, k0, blk, blk_k=None):
    blk_k = blk if blk_k is None else blk_k
    row = q0 + lax.broadcasted_iota(jnp.int32, (blk, blk_k), 0)
    col = k0 + lax.broadcasted_iota(jnp.int32, (blk, blk_k), 1)
    return row >= col


def _attn_fwd(qn, qr, kv, kr):
    bsz, t, _ = qn.shape
    blk = min(ATT_BLK, t)

    heads = range(MLA_HEADS)

    def body(qn_ref, qr_ref, kv_ref, kr_ref, o_ref, lse_ref):
        qi = pl.program_id(1)
        q = [jnp.concatenate([qn_ref[:, LANES * h:LANES * (h + 1)], qr_ref[:, LANES * h:LANES * (h + 1)]], axis=1) for h in heads]
        lower = _causal(0, 0, blk)

        def kv_step(j, carry, diagonal):
            ks = pl.multiple_of(j * blk, blk)
            k_rope = kr_ref[pl.ds(ks, blk), :]
            def score(h):
                k = jnp.concatenate([kv_ref[pl.ds(ks, blk), 2 * LANES * h:2 * LANES * h + LANES], k_rope], axis=1)
                return lax.dot_general(q[h], k, _NT, preferred_element_type=F32)

            out = []
            nxt = score(0)
            for h in heads:
                s = nxt * ATT_SCALE
                if h + 1 < MLA_HEADS:
                    nxt = score(h + 1)
                m, l, acc = carry[h]
                if diagonal:
                    s = jnp.where(lower, s, -1e30)
                m_new = jnp.maximum(m, jnp.max(s, axis=1, keepdims=True))
                alpha = jnp.exp(m - m_new)
                p = jnp.exp(s - m_new)
                l = alpha * l + jnp.sum(p, axis=1, keepdims=True)
                v = kv_ref[pl.ds(ks, blk), 2 * LANES * h + LANES:2 * LANES * (h + 1)]
                out.append((m_new, l, alpha * acc + jnp.dot(p.astype(BF16), v, preferred_element_type=F32)))
            return tuple(out)

        one = (jnp.full((blk, 1), -1e30, F32), jnp.zeros((blk, 1), F32), jnp.zeros((blk, MLA_V), F32))
        carry = lax.fori_loop(0, qi, lambda j, c: kv_step(j, c, False), (one,) * MLA_HEADS)
        carry = kv_step(qi, carry, True)
        for h in heads:
            m, l, acc = carry[h]
            o_ref[:, LANES * h:LANES * (h + 1)] = acc / l
            lse_ref[h] = jnp.broadcast_to(m + jnp.log(l), (blk, LANES))

    return pl.pallas_call(
        body, name="attn_fwd", grid=(bsz, t // blk),
        in_specs=[pl.BlockSpec((None, blk, MLA_WIDTH), lambda b, i: (b, i, 0)),
                  pl.BlockSpec((None, blk, MLA_WIDTH), lambda b, i: (b, i, 0)),
                  pl.BlockSpec((None, t, 2 * MLA_WIDTH), lambda b, i: (b, 0, 0)),
                  pl.BlockSpec((None, t, LANES), lambda b, i: (b, 0, 0))],
        out_specs=[pl.BlockSpec((None, blk, MLA_WIDTH), lambda b, i: (b, i, 0)),
                   pl.BlockSpec((None, MLA_HEADS, blk, LANES), lambda b, i: (b, 0, i, 0))],
        out_shape=[_sds((bsz, t, MLA_WIDTH)), _sds((bsz, MLA_HEADS, t, LANES))],
        compiler_params=pltpu.CompilerParams(dimension_semantics=("arbitrary", "arbitrary")),
    )(qn, qr, kv, kr)


def _attn_bwd(qn, qr, kv, kr, o, do, lse):
    bsz, t, _ = qn.shape
    blk = min(ATT_BLK, t)
    nb = t // blk
    bq = ATT_BWD_Q

    def body(qn_ref, qr_ref, kn_ref, kr_ref, v_ref, o_ref, do_ref, lse_ref, dqn_ref, dqr_ref, dkv_ref, dkr_ref, dq_sc, delta_sc):
        dq_sc[...] = jnp.zeros_like(dq_sc)
        delta_sc[...] = jnp.sum(do_ref[...].astype(F32) * o_ref[...], axis=1, keepdims=True)

        def kv_loop(j, _):
            ks = pl.multiple_of(j * blk, blk)
            k = jnp.concatenate([kn_ref[pl.ds(ks, blk), :], kr_ref[pl.ds(ks, blk), :]], axis=1)
            vb = v_ref[pl.ds(ks, blk), :]

            def q_pair(pair, carry):
                dk, dv = carry
                subs = range(2)
                qs = [pl.multiple_of((2 * pair + sub) * bq, bq) for sub in subs]
                q = [jnp.concatenate([qn_ref[pl.ds(qs[u], bq), :], qr_ref[pl.ds(qs[u], bq), :]], axis=1) for u in subs]
                dob = [do_ref[pl.ds(qs[u], bq), :] for u in subs]

                def scores(u):
                    return lax.dot_general(q[u], k, _NT, preferred_element_type=F32)

                def probs(u, s, dp):
                    p = jnp.where(_causal((2 * pair + u) * bq, j * blk, bq, blk),
                                  jnp.exp(s * ATT_SCALE - lse_ref[pl.ds(qs[u], bq), 0:1]), 0.0)
                    return p.astype(BF16), (p * (dp - delta_sc[pl.ds(qs[u], bq), :]) * ATT_SCALE).astype(BF16)

                s = [scores(u) for u in subs]
                dp = [lax.dot_general(dob[u], vb, _NT, preferred_element_type=F32) for u in subs]
                pd = [probs(u, s[u], dp[u]) for u in subs]
                for u in subs:
                    dv = dv + lax.dot_general(pd[u][0], dob[u], _TN, preferred_element_type=F32)
                for u in subs:
                    dq_sc[pl.ds(qs[u], bq), :] += jnp.dot(pd[u][1], k, preferred_element_type=F32)
                    dk = dk + lax.dot_general(pd[u][1], q[u], _TN, preferred_element_type=F32)
                return dk, dv

            first = (j * blk) // (2 * bq)
            dk, dv = lax.fori_loop(first, t // (2 * bq), q_pair, (jnp.zeros((blk, 2 * LANES), F32), jnp.zeros((blk, MLA_V), F32)))
            dkv_ref[pl.ds(ks, blk), 0:LANES] = dk[:, 0:LANES]
            dkv_ref[pl.ds(ks, blk), LANES:2 * LANES] = dv
            dkr_ref[pl.ds(ks, blk), :] = dk[:, LANES:2 * LANES]
            return 0

        lax.fori_loop(0, nb, kv_loop, 0)
        dqn_ref[...] = dq_sc[:, 0:LANES]
        dqr_ref[...] = dq_sc[:, LANES:2 * LANES]

    head_col = lambda b, h: (b, 0, h)
    return pl.pallas_call(
        body, name="attn_bwd", grid=(bsz, MLA_HEADS),
        in_specs=[pl.BlockSpec((None, t, LANES), head_col), pl.BlockSpec((None, t, LANES), head_col),
                  pl.BlockSpec((None, t, LANES), lambda b, h: (b, 0, 2 * h)),
                  pl.BlockSpec((None, t, LANES), lambda b, h: (b, 0, 0)),
                  pl.BlockSpec((None, t, LANES), lambda b, h: (b, 0, 2 * h + 1)),
                  pl.BlockSpec((None, t, LANES), head_col), pl.BlockSpec((None, t, LANES), head_col),
                  pl.BlockSpec((None, None, t, LANES), lambda b, h: (b, h, 0, 0))],
        out_specs=[pl.BlockSpec((None, t, LANES), head_col), pl.BlockSpec((None, t, LANES), head_col),
                   pl.BlockSpec((None, t, 2 * LANES), head_col),
                   pl.BlockSpec((None, None, t, LANES), lambda b, h: (h, b, 0, 0))],
        out_shape=[_sds((bsz, t, MLA_WIDTH)), _sds((bsz, t, MLA_WIDTH)), _sds((bsz, t, 2 * MLA_WIDTH)),
                   _sds((MLA_HEADS, bsz, t, LANES))],
        scratch_shapes=[pltpu.VMEM((t, 2 * LANES), F32), pltpu.VMEM((t, 1), F32)],
        compiler_params=pltpu.CompilerParams(dimension_semantics=("arbitrary", "arbitrary")),
    )(qn, qr, kv, kr, kv, o, do, lse)


SCAN_CHUNK = 16


def _diag_mask():
    row = lax.broadcasted_iota(jnp.int32, (RW_HEAD, RW_WIDTH), 0)
    lane = lax.broadcasted_iota(jnp.int32, (RW_HEAD, RW_WIDTH), 1)
    return jnp.where(row == (lane & (RW_HEAD - 1)), 1.0, 0.0)


def _time_minor(a):
    bsz, t, _ = a.shape
    a = a.reshape(bsz, t // SCAN_CHUNK, SCAN_CHUNK, RW_HEADS, RW_HEAD)
    return a.transpose(0, 1, 4, 3, 2).reshape(bsz, t // SCAN_CHUNK, RW_HEAD, RW_HEADS * SCAN_CHUNK)


def _head_expand():
    l = lax.broadcasted_iota(jnp.int32, (2 * LANES, RW_WIDTH), 0)
    n = lax.broadcasted_iota(jnp.int32, (2 * LANES, RW_WIDTH), 1)
    return jnp.where(((l & (LANES - 1)) >> 4) == (n >> 6), 1.0, 0.0).astype(BF16)


BCAST_GROUP = 4


def _outer_chunk(tm_ref, row_ref, out_sc, expand, seqs):
    step_of_lane = lax.broadcasted_iota(jnp.int32, (RW_HEAD, LANES), 1) & (SCAN_CHUNK - 1)
    tiles = [tm_ref[bi, 0] for bi in seqs]
    for t0 in range(0, SCAN_CHUNK, BCAST_GROUP):
        parts = []
        for t in range(t0, t0 + BCAST_GROUP):
            for tile in tiles:
                a = jnp.where(step_of_lane == t, tile, 0.0)
                hi = a.astype(BF16)
                parts.append(jnp.concatenate([hi, (a - hi.astype(F32)).astype(BF16)], axis=1))
        cols = jnp.dot(jnp.concatenate(parts, axis=0), expand, preferred_element_type=F32)
        for j, t in enumerate(range(t0, t0 + BCAST_GROUP)):
            base = j * RW_HEAD * len(seqs)
            out_sc[t] = jnp.concatenate([cols[base + RW_HEAD * bi:base + RW_HEAD * (bi + 1)] * row_ref[bi, t:t + 1, :]
                                         for bi in seqs], axis=0)


def _fold8(x):
    acc = x[0:8]
    for j in range(1, x.shape[0] // 8):
        acc = acc + x[8 * j:8 * (j + 1)]
    return acc


def _rows8(at):
    return pl.ds(at * 8 if isinstance(at, int) else pl.multiple_of(at * 8, 8), 8)


def _put8(sc, bi, at, val):
    for j in range(RW_WIDTH // LANES):
        sc[bi * (RW_WIDTH // LANES) + j, _rows8(at), :] = val[:, LANES * j:LANES * (j + 1)]


def _unfold8(sc, bi, steps):
    tiles = []
    for j in range(RW_WIDTH // LANES):
        view = sc.at[bi * (RW_WIDTH // LANES) + j]
        acc = view[pl.ds(0, steps, stride=8), :]
        for s in range(1, 8):
            acc = acc + view[pl.ds(s, steps, stride=8), :]
        tiles.append(acc)
    return jnp.concatenate(tiles, axis=1)


def _scan_fwd(r, w, k, vt, nkk, b):
    bsz, t, _ = r.shape
    tc = SCAN_CHUNK

    def body(r_ref, w_ref, k_ref, n_ref, b_ref, vt_ref, y_ref, st_ref, s_sc, vc_sc, y_sc):
        @pl.when(pl.program_id(0) == 0)
        def _():
            s_sc[...] = jnp.zeros_like(s_sc)

        ones = _seg_ones()
        diag = _diag_mask()
        seqs = range(bsz)
        _outer_chunk(vt_ref, k_ref, vc_sc, _head_expand(), seqs)

        def put_y(ya, at):
            for bi in seqs:
                _put8(y_sc, bi, at, _fold8(ya[bi] * diag))

        def step(i, _):
            row = lambda ref, bi: ref[bi, pl.ds(i, 1), :]
            prev = jnp.maximum(i - 1, 0)
            s_old = [s_sc[bi] for bi in seqs]
            s_b = [s_old[bi].astype(BF16) for bi in seqs]
            sa = _seg_multi([s_b[bi] * row(n_ref, bi).astype(BF16) for bi in seqs], ones, 1)
            put_y(_seg_multi([s_b[bi] * r_ref[bi, pl.ds(prev, 1), :].astype(BF16) for bi in seqs], ones, 1), prev)
            vk = vc_sc[i]
            for bi in seqs:
                s_new = s_old[bi] * row(w_ref, bi) + sa[bi] * row(b_ref, bi) + vk[RW_HEAD * bi:RW_HEAD * (bi + 1)]
                s_sc[bi] = s_new
                st_ref[bi, i] = s_new
            return 0

        lax.fori_loop(0, tc, step, 0, unroll=8)
        put_y(_seg_multi([s_sc[bi].astype(BF16) * r_ref[bi, tc - 1:tc, :].astype(BF16) for bi in seqs], ones, 1), tc - 1)
        for bi in seqs:
            y_ref[bi] = _unfold8(y_sc, bi, tc)

    vec = pl.BlockSpec((bsz, tc, RW_WIDTH), lambda c: (0, c, 0))
    return pl.pallas_call(
        body, name="scan_fwd", grid=(t // tc,),
        in_specs=[vec] * 5 + [pl.BlockSpec((bsz, 1, RW_HEAD, LANES), lambda c: (0, c, 0, 0))],
        out_specs=[vec, pl.BlockSpec((bsz, tc, RW_HEAD, RW_WIDTH), lambda c: (0, c, 0, 0))],
        out_shape=[_sds((bsz, t, RW_WIDTH)), _sds((bsz, t, RW_HEAD, RW_WIDTH))],
        scratch_shapes=[pltpu.VMEM((bsz, RW_HEAD, RW_WIDTH), F32), pltpu.VMEM((tc, bsz * RW_HEAD, RW_WIDTH), F32),
                        pltpu.VMEM((bsz * RW_WIDTH // LANES, tc * 8, LANES), F32)],
        compiler_params=_ARB1,
    )(r, w, k, nkk, b, vt)


def _own_head_row(x):
    first_half = lax.broadcasted_iota(jnp.int32, (1, LANES), 1) < RW_HEAD
    tiles = [jnp.where(first_half, x[2 * j:2 * j + 1, LANES * j:LANES * (j + 1)], x[2 * j + 1:2 * j + 2, LANES * j:LANES * (j + 1)])
             for j in range(RW_WIDTH // LANES)]
    return jnp.concatenate(tiles, axis=1)


def _scan_bwd(r, w, k, vt, nkk, b, st, dyt):
    bsz, t, _ = r.shape
    tc = SCAN_CHUNK
    nc = t // tc

    def body(r_ref, w_ref, k_ref, n_ref, b_ref, vt_ref, dyt_ref, st_ref, halo_ref,
             dr_ref, dw_ref, dk_ref, dv_ref, dn_ref, db_ref, g_sc, dc_sc, v8_sc, dy8_sc, *part_scs):
        c = pl.program_id(0)

        @pl.when(c == 0)
        def _():
            g_sc[...] = jnp.zeros_like(g_sc)

        ones = _seg_ones()
        diag = _diag_mask()
        has_prev = jnp.where(c == nc - 1, 0.0, 1.0)
        seqs = range(bsz)
        _outer_chunk(dyt_ref, r_ref, dc_sc, _head_expand(), seqs)
        for bi in seqs:
            v8_sc[bi] = jnp.concatenate([vt_ref[bi, 0].T] * 2, axis=1)
            dy8_sc[bi] = jnp.concatenate([dyt_ref[bi, 0].T] * 2, axis=1)
        by_head = lambda sc, bi, i: sc.at[bi][pl.ds(i, RW_HEADS, stride=SCAN_CHUNK), :][:, 0:RW_HEAD].astype(BF16)
        dw_sc, dv_sc, dn_sc, db_sc = part_scs

        def step(i, s_p):
            static = isinstance(i, int)
            row = lambda ref, bi: ref[bi, i:i + 1, :] if static else ref[bi, pl.ds(i, 1), :]
            put_row = lambda ref, bi, val: ref.__setitem__((bi, slice(i, i + 1) if static else pl.ds(i, 1), slice(None)), val)
            dr8 = [jnp.dot(by_head(dy8_sc, bi, i), st_ref[bi, i].astype(BF16), preferred_element_type=F32) for bi in seqs]
            rowb = lambda ref, bi: row(ref, bi).astype(BF16)
            sa = _seg_multi([s_p[bi].astype(BF16) * rowb(n_ref, bi) for bi in seqs], ones, 1)
            dc_all = dc_sc[i]
            dc = [dc_all[RW_HEAD * bi:RW_HEAD * (bi + 1)] for bi in seqs]
            g = [g_sc[bi] + dc[bi] for bi in seqs]
            g_b = [g[bi].astype(BF16) for bi in seqs]
            res = _seg_multi([g_b[bi] * rowb(b_ref, bi) for bi in seqs] + [g_b[bi] * rowb(k_ref, bi) for bi in seqs], ones, 1)
            dsa, dvb = res[:bsz], res[bsz:]
            for bi in seqs:
                dk8 = jnp.dot(by_head(v8_sc, bi, i), g_b[bi], preferred_element_type=F32)
                put_row(dr_ref, bi, _own_head_row(dr8[bi]))
                put_row(dk_ref, bi, _own_head_row(dk8))
                _put8(dv_sc, bi, i, _fold8(dvb[bi] * diag))
                _put8(dw_sc, bi, i, _fold8(g[bi] * s_p[bi]))
                _put8(db_sc, bi, i, _fold8(g[bi] * sa[bi]))
                _put8(dn_sc, bi, i, _fold8(s_p[bi] * dsa[bi]))
                g_sc[bi] = g[bi] * row(w_ref, bi) + dsa[bi] * row(n_ref, bi)

        def loop_step(ii, _):
            i = tc - 1 - ii
            step(i, [st_ref[bi, i - 1] for bi in seqs])
            return 0

        lax.fori_loop(0, tc - 1, loop_step, 0, unroll=5)
        step(0, [halo_ref[bi, 0] * has_prev for bi in seqs])
        for out_ref, sc in zip((dw_ref, dv_ref, dn_ref, db_ref), part_scs):
            for bi in seqs:
                out_ref[bi] = _unfold8(sc, bi, tc)

    vec = pl.BlockSpec((bsz, tc, RW_WIDTH), lambda c: (0, nc - 1 - c, 0))
    tmin = pl.BlockSpec((bsz, 1, RW_HEAD, LANES), lambda c: (0, nc - 1 - c, 0, 0))
    parts = pltpu.VMEM((bsz * RW_WIDTH // LANES, tc * 8, LANES), F32)
    heads_steps = pltpu.VMEM((bsz, LANES, LANES), F32)
    return pl.pallas_call(
        body, name="scan_bwd", grid=(nc,),
        in_specs=[vec] * 5 + [tmin, tmin,
                              pl.BlockSpec((bsz, tc, RW_HEAD, RW_WIDTH), lambda c: (0, nc - 1 - c, 0, 0)),
                              pl.BlockSpec((bsz, 1, RW_HEAD, RW_WIDTH), lambda c: (0, jnp.maximum((nc - 1 - c) * tc - 1, 0), 0, 0))],
        out_specs=[vec] * 6,
        out_shape=[_sds((bsz, t, RW_WIDTH))] * 6,
        scratch_shapes=[pltpu.VMEM((bsz, RW_HEAD, RW_WIDTH), F32), pltpu.VMEM((tc, bsz * RW_HEAD, RW_WIDTH), F32),
                        heads_steps, heads_steps] + [parts] * 4,
        compiler_params=_ARB1,
    )(r, w, k, nkk, b, vt, dyt, st, st)


TOKEN_TILE = 256


def _padded_weights(wt):
    f = lambda a: a.astype(F32)
    w_in = wt["w_in"][0].astype(BF16)
    zeros = lambda r, c: jnp.zeros((r, c), F32)
    wp = jnp.concatenate([w_in[:, :MLA_COLS], jnp.zeros((D_MODEL, PM_W - MLA_COLS), BF16), w_in[:, MLA_COLS:]], axis=1)
    w_uq = f(wt["mla_w_uq"][0]).reshape(Q_LORA, MLA_HEADS, MLA_NOPE + MLA_ROPE)
    wqn = w_uq[:, :, :MLA_NOPE].reshape(Q_LORA, MLA_HEADS * MLA_NOPE)
    wqr = jnp.concatenate([w_uq[:, :, MLA_NOPE:], jnp.zeros((Q_LORA, MLA_HEADS, LANES - MLA_ROPE), F32)], axis=2)
    wqr = wqr.reshape(Q_LORA, MLA_HEADS * LANES)
    w2p = jnp.concatenate([f(wt["rw_w2"][0]), zeros(LORA, RW_WIDTH)], axis=0)
    a2p = jnp.concatenate([zeros(LORA, RW_WIDTH), f(wt["rw_a2"][0])], axis=0)
    bw = (f(wt["mla_q_norm_g"]), wqn, wqr, f(wt["mla_kv_norm_g"]), f(wt["mla_w_ukv"][0]), f(wt["rw_mu"]), f(wt["rw_w0"]),
          w2p, f(wt["rw_a0"]), a2p, f(wt["rw_k_k"]), f(wt["rw_k_a"]))
    hw = (f(wt["rw_ln_g"]), f(wt["rw_ln_b"]), f(wt["rw_r_k"]).reshape(1, RW_WIDTH), f(wt["w_out"][0]), f(wt["norm_post_g"]))
    return wp, bw, hw


def _local_step(x, positions, target, wt):
    bsz, t, _ = x.shape
    n = bsz * t
    tm = min(TOKEN_TILE, t)
    tps = t // tm
    wp, bw, hw = _padded_weights(wt)
    wpb = wp.astype(BF16)
    g_pre = wt["norm_pre_g"].astype(F32)
    x2 = x.reshape(n, D_MODEL)
    tgt2 = target.reshape(n, D_MODEL)
    inv_freq = ROPE_THETA ** (-jnp.arange(0, MLA_ROPE, 2, dtype=F32) / MLA_ROPE)
    invf = jnp.tile(inv_freq, LANES // (MLA_ROPE // 2)).reshape(LANES, 1)
    cos, sin = _rope_tables(positions.reshape(1, n), invf, tm)

    u, pm, prw, z = _fwd_a(x2, g_pre, wpb, tm)
    qn, qr, kv, kr, r, w, k, v, nkk, b = _fwd_b(pm, prw, cos, sin, bw, tm, tps)
    b3 = lambda a: a.reshape(bsz, t, a.shape[-1])
    ym, lse = _attn_fwd(b3(qn), b3(qr), b3(kv), b3(kr))
    vt = _time_minor(b3(v))
    ys, st = _scan_fwd(b3(r), b3(w), b3(k), vt, b3(nkk), b3(b))
    (dys, dr_h, dk_h, dv_h, dym, dz, dxres, loss, d_lng, d_lnb, d_rk, d_wout, d_gpost) = _head(
        ys.reshape(n, RW_WIDTH), r, k, v, ym.reshape(n, MLA_WIDTH), z, x2, tgt2, hw, tm)
    dqn, dqr, dkv, dkr_heads = _attn_bwd(b3(qn), b3(qr), b3(kv), b3(kr), ym, b3(dym), lse)
    dr_s, dw_s, dk_s, dv_s, dn_s, db_s = _scan_bwd(b3(r), b3(w), b3(k), vt, b3(nkk), b3(b), st, _time_minor(b3(dys)))
    f2 = lambda a: a.reshape(n, a.shape[-1])
    cts = (f2(dqn), f2(dqr), f2(dkv), f2(dr_s), dr_h, f2(dw_s), f2(dk_s), dk_h, f2(dv_s), dv_h, f2(dn_s), f2(db_s))
    (dpm, dprw, dps, d_gq, d_wqn, d_wqr, d_gkv, d_wkv, d_mu, d_w0, d_w2p, d_a0, d_a2p, d_kk, d_ka) = _bwd_b(
        pm, prw, cos, sin, bw, cts, dkr_heads.reshape(MLA_HEADS, n, LANES), tm, tps)
    grad_x, dpb, d_gpre = _bwd_a(x2, g_pre, wpb, dpm, dprw, dps, dz, dxres, tm, tps)
    d_wp = _dw_in(u, dpb, min(1024, n), 640)

    d_w_in = jnp.concatenate([d_wp[:, :MLA_COLS], d_wp[:, PM_W:]], axis=1)
    d_w_uq = jnp.concatenate([d_wqn.reshape(Q_LORA, MLA_HEADS, MLA_NOPE),
                              d_wqr.reshape(Q_LORA, MLA_HEADS, LANES)[:, :, :MLA_ROPE]], axis=2)
    grads = {
        "norm_pre_g": d_gpre, "w_in": d_w_in[None], "mla_q_norm_g": d_gq,
        "mla_w_uq": d_w_uq.reshape(1, Q_LORA, MLA_HEADS * (MLA_NOPE + MLA_ROPE)), "mla_kv_norm_g": d_gkv,
        "mla_w_ukv": d_wkv[None], "rw_mu": d_mu, "rw_w0": d_w0, "rw_w2": d_w2p[None, :LORA], "rw_a0": d_a0,
        "rw_a2": d_a2p[None, LORA:], "rw_k_k": d_kk, "rw_k_a": d_ka, "rw_r_k": d_rk.reshape(1, RW_HEADS, RW_HEAD),
        "rw_ln_g": d_lng, "rw_ln_b": d_lnb, "w_out": d_wout[None], "norm_post_g": d_gpost,
    }
    return loss, grad_x.reshape(bsz, t, D_MODEL), grads


_MESH = pl.DeviceIdType.MESH


def _gather_shards(shards):
    na = len(shards)

    def body(*refs):
        x_refs, out_refs = refs[:na], refs[na:2 * na]
        send_sems, recv_sems, local_sems = refs[2 * na:]
        x, y, c = lax.axis_index("x"), lax.axis_index("y"), lax.axis_index("c")
        me, sibling = (x, y, c), (x, y, 1 - c)
        chips = [(1 - x, y), (x, 1 - y), (1 - x, 1 - y)]
        arrays = range(na)

        def slot(a, px, py, pc):
            return out_refs[a].at[4 * px + 2 * py + pc]

        def copy(k, a, block, to, src=None):
            return pltpu.make_async_remote_copy(
                src_ref=slot(a, *block) if src is None else src, dst_ref=slot(a, *block),
                send_sem=send_sems.at[k, a], recv_sem=recv_sems.at[k, a], device_id=to, device_id_type=_MESH)

        mine = [pltpu.make_async_copy(x_refs[a], slot(a, *me), local_sems.at[a]) for a in arrays]
        for cp in mine:
            cp.start()
        first = [copy(0, a, me, sibling, src=x_refs[a]) for a in arrays]
        first += [copy(1 + j, a, me, (*chip, c), src=x_refs[a]) for j, chip in enumerate(chips) for a in arrays]
        for cp in first:
            cp.start()
        passed = []
        for j, chip in enumerate(chips):
            for a in arrays:
                copy(1 + j, a, (*chip, c), me).wait_recv()
                passed.append(copy(4 + j, a, (*chip, c), sibling))
                passed[-1].start()
        for a in arrays:
            copy(0, a, sibling, me).wait_recv()
        for j, chip in enumerate(chips):
            for a in arrays:
                copy(4 + j, a, (*chip, 1 - c), me).wait_recv()
        for cp in first + passed:
            cp.wait_send()
        for cp in mine:
            cp.wait()

    vmem = pl.BlockSpec(memory_space=pltpu.VMEM)
    return pl.pallas_call(
        body, name="gather_shards",
        out_shape=[_sds((N_DEV,) + a.shape, a.dtype) for a in shards],
        in_specs=[vmem] * na, out_specs=[vmem] * na,
        scratch_shapes=[pltpu.SemaphoreType.DMA((7, na)), pltpu.SemaphoreType.DMA((7, na)), pltpu.SemaphoreType.DMA((na,))],
    )(*shards)


SMALL_LANES = SMALL_N + LANES


def _exchange_grads(big_blocks, small_grads, loss_tile):
    nb = len(big_blocks)
    ns = len(small_grads)

    def body(*refs):
        big, small, loss_ref = refs[:nb], refs[nb:nb + ns], refs[nb + ns]
        rbig, rsmall = refs[nb + ns + 1:2 * nb + ns + 1], refs[2 * nb + ns + 1]
        send_b, recv_b, send_s, recv_s, local_sems, row_sc = refs[2 * nb + ns + 2:]
        x, y, c = lax.axis_index("x"), lax.axis_index("y"), lax.axis_index("c")
        me_lin = 4 * x + 2 * y + c
        mine = [pltpu.make_async_copy(big[j].at[me_lin], rbig[j].at[0], local_sems.at[j]) for j in range(nb)]
        for cp in mine:
            cp.start()
        off = 0
        for ref, (_, cnt) in zip(small, SMALL):
            row_sc[:, off:off + cnt] = ref[...]
            off += cnt
        row_sc[:, off:off + LANES] = loss_ref[0:1, :]
        rsmall[me_lin] = row_sc[...]
        copies = []
        for k in range(1, N_DEV):
            px, py, pc = x ^ (k >> 2), y ^ ((k >> 1) & 1), c ^ (k & 1)
            peer = (px, py, pc)
            for j in range(nb):
                copies.append(pltpu.make_async_remote_copy(
                    src_ref=big[j].at[4 * px + 2 * py + pc], dst_ref=rbig[j].at[k],
                    send_sem=send_b.at[k - 1, j], recv_sem=recv_b.at[k - 1, j], device_id=peer, device_id_type=_MESH))
            copies.append(pltpu.make_async_remote_copy(
                src_ref=row_sc, dst_ref=rsmall.at[me_lin],
                send_sem=send_s.at[k - 1], recv_sem=recv_s.at[k - 1], device_id=peer, device_id_type=_MESH))
        for cp in copies:
            cp.start()
        for cp in copies:
            cp.wait_recv()
        for cp in copies:
            cp.wait_send()
        for cp in mine:
            cp.wait()

    hbm, vmem = pl.BlockSpec(memory_space=pl.ANY), pl.BlockSpec(memory_space=pltpu.VMEM)
    return pl.pallas_call(
        body, name="exchange_grads",
        out_shape=[_sds(a.shape, a.dtype) for a in big_blocks] + [_sds((N_DEV, 1, SMALL_LANES))],
        in_specs=[hbm] * nb + [vmem] * (ns + 1),
        out_specs=[hbm] * nb + [vmem],
        scratch_shapes=[pltpu.SemaphoreType.DMA((N_DEV - 1, nb)), pltpu.SemaphoreType.DMA((N_DEV - 1, nb)),
                        pltpu.SemaphoreType.DMA((N_DEV - 1,)), pltpu.SemaphoreType.DMA((N_DEV - 1,)),
                        pltpu.SemaphoreType.DMA((nb,)), pltpu.VMEM((1, SMALL_LANES), F32)],
    )(*big_blocks, *small_grads, loss_tile)


def _adamw_math(w, g, m, v):
    m = ADAM_B1 * m + (1.0 - ADAM_B1) * g
    v = ADAM_B2 * v + (1.0 - ADAM_B2) * (g * g)
    m_hat = m / (1.0 - ADAM_B1 ** ADAM_STEP)
    v_hat = v / (1.0 - ADAM_B2 ** ADAM_STEP)
    return -ADAM_LR * (m_hat / (jnp.sqrt(v_hat) + ADAM_EPS) + ADAM_WD * w), m, v


def _reduce_adamw(name, parts, w, m, v, row_blocks):
    _, rows, cols = parts.shape
    rb = rows // row_blocks

    def body(p_ref, w_ref, m_ref, v_ref, g_out, d_out, m_out, v_out):
        g = p_ref[0].astype(F32)
        for s in range(1, N_DEV):
            g = g + p_ref[s].astype(F32)
        g_out[0] = g
        d_out[0], m_out[0], v_out[0] = _adamw_math(w_ref[0], g, m_ref[0], v_ref[0])

    blk = pl.BlockSpec((1, rb, cols), lambda i: (0, i, 0))
    return pl.pallas_call(
        body, name="reduce_adamw_" + name, grid=(row_blocks,),
        in_specs=[pl.BlockSpec((N_DEV, rb, cols), lambda i: (0, i, 0)), blk, blk, blk],
        out_specs=[blk] * 4, out_shape=[_sds((1, rows, cols))] * 4,
        compiler_params=_ARB1,
    )(parts, w, m, v)


def _reduce_adamw_small(rows, ws, ms, vs):
    ns = len(SMALL)

    def body(r_ref, *refs):
        w_refs, m_refs, v_refs, outs = refs[:ns], refs[ns:2 * ns], refs[2 * ns:3 * ns], refs[3 * ns:]
        total = r_ref[0]
        for s in range(1, N_DEV):
            total = total + r_ref[s]
        off = 0
        for j, (_, cnt) in enumerate(SMALL):
            g = total[:, off:off + cnt]
            off += cnt
            outs[4 * j][...] = g
            outs[4 * j + 1][...], outs[4 * j + 2][...], outs[4 * j + 3][...] = _adamw_math(
                w_refs[j][...], g, m_refs[j][...], v_refs[j][...])
        outs[4 * ns][...] = total[:, off:off + LANES]

    vmem = pl.BlockSpec(memory_space=pltpu.VMEM)
    return pl.pallas_call(
        body, name="reduce_adamw_small",
        in_specs=[vmem] * (1 + 3 * ns), out_specs=[vmem] * (4 * ns + 1),
        out_shape=[_sds((1, cnt)) for _, cnt in SMALL for _ in range(4)] + [_sds((1, LANES))],
    )(rows, *ws, *ms, *vs)


def _shard_blocks(name, full):
    a = full[0]
    rows, cols = a.shape
    if name == "w_out":
        return a.reshape(N_DEV, rows // N_DEV, cols)
    return a.reshape(rows, N_DEV, cols // N_DEV).transpose(1, 0, 2)


def _unshard(name, blocks):
    _, rows, cols = blocks.shape
    if name == "w_out":
        return blocks.reshape(1, N_DEV * rows, cols)
    return blocks.transpose(1, 0, 2).reshape(1, rows, N_DEV * cols)


def kernel(x, positions, norm_pre_g, w_in, mla_q_norm_g, mla_w_uq, mla_kv_norm_g, mla_w_ukv, rw_mu, rw_w0, rw_w2, rw_a0, rw_a2, rw_k_k, rw_k_a, rw_r_k, rw_ln_g, rw_ln_b, w_out, norm_post_g, loss_target, m_norm_pre_g, m_w_in, m_mla_q_norm_g, m_mla_w_uq, m_mla_kv_norm_g, m_mla_w_ukv, m_rw_mu, m_rw_w0, m_rw_w2, m_rw_a0, m_rw_a2, m_rw_k_k, m_rw_k_a, m_rw_r_k, m_rw_ln_g, m_rw_ln_b, m_w_out, m_norm_post_g, v_norm_pre_g, v_w_in, v_mla_q_norm_g, v_mla_w_uq, v_mla_kv_norm_g, v_mla_w_ukv, v_rw_mu, v_rw_w0, v_rw_w2, v_rw_a0, v_rw_a2, v_rw_k_k, v_rw_k_a, v_rw_r_k, v_rw_ln_g, v_rw_ln_b, v_w_out, v_norm_post_g):
    given = dict(locals())
    w = {nm: given[nm] for nm in WEIGHTS}
    mom = {nm: given["m_" + nm] for nm in WEIGHTS}
    var = {nm: given["v_" + nm] for nm in WEIGHTS}
    sharded = list(SHARDED)

    gathered = _gather_shards([w[nm][0].astype(BF16) for nm in sharded])
    full = dict(w)
    for nm, blocks in zip(sharded, gathered):
        full[nm] = _unshard(nm, blocks)

    loss_part, grad_x, grads = _local_step(x, positions, loss_target, full)

    small_names = [nm for nm, _ in SMALL]
    row = lambda a: a.reshape(1, -1)
    got = _exchange_grads([_shard_blocks(nm, grads[nm]).astype(BF16) for nm in sharded],
                          [row(grads[nm]) for nm in small_names], loss_part)
    new = {}
    for nm, parts in zip(sharded, got[:-1]):
        new[nm] = _reduce_adamw(nm, parts, w[nm], mom[nm], var[nm], 4 if nm == "w_in" else 1)
    res = _reduce_adamw_small(got[-1], [row(w[nm]) for nm in small_names], [row(mom[nm]) for nm in small_names],
                              [row(var[nm]) for nm in small_names])
    for j, nm in enumerate(small_names):
        new[nm] = tuple(a.reshape(w[nm].shape) for a in res[4 * j:4 * j + 4])
    loss = res[-1][0, 0]
    return (loss, grad_x, *[new[nm][j] for j in range(4) for nm in WEIGHTS])
```

```python
import functools

import jax
import jax.numpy as jnp
from jax import lax
from jax.experimental import pallas as pl
from jax.experimental.pallas import tpu as pltpu

F32 = jnp.float32
BF16 = jnp.bfloat16

D_MODEL = 1024
MLA_HEADS = 4
MLA_NOPE = 128
MLA_ROPE = 64
MLA_V = 128
MLA_WIDTH = MLA_HEADS * MLA_V
Q_LORA = 256
KV_LORA = 128
ROPE_THETA = 10000.0
RW_HEAD = 64
RW_WIDTH = 512
RW_HEADS = RW_WIDTH // RW_HEAD
LORA = 64
RW_COLS = 3 * RW_WIDTH + 2 * LORA
MLA_COLS = Q_LORA + KV_LORA + MLA_ROPE
D_IN = MLA_COLS + RW_COLS + D_MODEL
RW_GN_EPS = 64e-5
NORM_EPS = 1e-6
ATT_SCALE = (MLA_NOPE + MLA_ROPE) ** -0.5
ADAM_LR, ADAM_B1, ADAM_B2, ADAM_EPS, ADAM_WD, ADAM_STEP = 0.001, 0.9, 0.999, 1e-08, 0.01, 10
N_DEV = 8
LANES = 128
MXU = 256

PM_W = 512
WP_COLS = PM_W + RW_COLS + D_MODEL
RW_PIECES = ((0, 512), (512, 1024), (1024, 1536), (1536, 1664))

SHARDED = ("w_in", "mla_w_uq", "mla_w_ukv", "rw_w2", "rw_a2", "w_out")
SMALL = (("norm_pre_g", 1024), ("mla_q_norm_g", 256), ("mla_kv_norm_g", 128), ("rw_mu", 1664), ("rw_w0", 512),
         ("rw_a0", 512), ("rw_k_k", 512), ("rw_k_a", 512), ("rw_r_k", 512), ("rw_ln_g", 512), ("rw_ln_b", 512),
         ("norm_post_g", 1024))
SMALL_N = sum(n for _, n in SMALL)
WEIGHTS = ("norm_pre_g", "w_in", "mla_q_norm_g", "mla_w_uq", "mla_kv_norm_g", "mla_w_ukv", "rw_mu", "rw_w0", "rw_w2",
           "rw_a0", "rw_a2", "rw_k_k", "rw_k_a", "rw_r_k", "rw_ln_g", "rw_ln_b", "w_out", "norm_post_g")


def _seg_ones():
    r = lax.broadcasted_iota(jnp.int32, (MXU, MXU), 0) >> 6
    c = lax.broadcasted_iota(jnp.int32, (MXU, MXU), 1) >> 6
    return jnp.where(r == c, 1.0, 0.0).astype(BF16)


def _seg_dot(x, ones, passes):
    parts, rem = [], x
    for p in range(passes):
        hb = rem.astype(BF16)
        parts.append(hb)
        if p + 1 < passes:
            rem = rem - hb.astype(F32)
    outs = []
    for j in range(x.shape[1] // MXU):
        acc = None
        for hb in parts:
            d = jnp.dot(hb[:, MXU * j:MXU * (j + 1)], ones, preferred_element_type=F32)
            acc = d if acc is None else acc + d
        outs.append(acc)
    return outs[0] if len(outs) == 1 else jnp.concatenate(outs, axis=1)


def _seg_multi(xs, ones, passes):
    his = [x.astype(BF16) for x in xs]
    hi = jnp.concatenate(his, axis=0)
    if passes == 2:
        lo = jnp.concatenate([(x - h.astype(F32)).astype(BF16) for x, h in zip(xs, his)], axis=0)
        rhs = jnp.concatenate([ones, ones], axis=0)
    halves = []
    for j in range(hi.shape[1] // MXU):
        sl = slice(MXU * j, MXU * (j + 1))
        if passes == 2:
            halves.append(jnp.dot(jnp.concatenate([hi[:, sl], lo[:, sl]], axis=1), rhs, preferred_element_type=F32))
        else:
            halves.append(jnp.dot(hi[:, sl], ones, preferred_element_type=F32))
    full = jnp.concatenate(halves, axis=1)
    m = xs[0].shape[0]
    return [full[m * i:m * (i + 1)] for i in range(len(xs))]


@jax.custom_vjp
def _segsum(x):
    return _seg_dot(x, _seg_ones(), 2)


_segsum.defvjp(lambda x: (_segsum(x), None), lambda _, g: (_segsum(g),))


@jax.custom_vjp
def _bdot(a, w):
    return jnp.dot(a.astype(BF16), w.astype(BF16), preferred_element_type=F32)


def _bdot_fwd(a, w):
    return _bdot(a, w), (a, w)


def _bdot_bwd(res, g):
    a, w = res
    gb = g.astype(BF16)
    da = lax.dot_general(gb, w.astype(BF16), (((1,), (1,)), ((), ())), preferred_element_type=F32)
    dw = lax.dot_general(a.astype(BF16), gb, (((0,), (0,)), ((), ())), preferred_element_type=F32)
    return da, dw


_bdot.defvjp(_bdot_fwd, _bdot_bwd)


def _rot_impl(x):
    w = x.shape[1]
    lane = lax.broadcasted_iota(jnp.int32, x.shape, 1)
    return jnp.where((lane & 63) < 32, -pltpu.roll(x, w - 32, 1), pltpu.roll(x, 32, 1))


@jax.custom_vjp
def _rot(x):
    return _rot_impl(x)


_rot.defvjp(lambda x: (_rot_impl(x), None), lambda _, g: (-_rot_impl(g),))


def _rms(x, g):
    return x * lax.rsqrt(jnp.mean(x * x, axis=-1, keepdims=True) + NORM_EPS) * g


def _shift_rows(p, prev_row):
    row = lax.broadcasted_iota(jnp.int32, p.shape, 0)
    return jnp.where(row == 0, prev_row, pltpu.roll(p, 1, 0))


def _unshift_rows(g, next_row):
    row = lax.broadcasted_iota(jnp.int32, g.shape, 0)
    return jnp.where(row == g.shape[0] - 1, next_row, pltpu.roll(g, g.shape[0] - 1, 0))


def _f_mla(cq, ckv, kr, cos, sin, g_q, wqn, wqr, g_kv, wkv):
    qn = _rms(cq, g_q)
    q_nope = _bdot(qn, wqn)
    q_r = _bdot(qn, wqr)
    cos4 = jnp.concatenate([cos] * MLA_HEADS, axis=1)
    sin4 = jnp.concatenate([sin] * MLA_HEADS, axis=1)
    q_rope = q_r * cos4 + _rot(q_r) * sin4
    kv = _bdot(_rms(ckv, g_kv), wkv)
    k_rope = kr * cos + _rot(kr) * sin
    return q_nope, q_rope, kv, k_rope


def _f_rw(pr, pk, pv, pt, sr, sk, sv, st, mu_r, mu_k, mu_v, mu_t, w0, w2p, a0, a2p, k_k, k_a):
    r = pr + (sr - pr) * mu_r
    k = pk + (sk - pk) * mu_k
    v = pv + (sv - pv) * mu_v
    t = pt + (st - pt) * mu_t
    nwl = -(w0 + _bdot(jnp.tanh(t), w2p))
    softplus = jnp.maximum(nwl, 0.0) + jnp.log(1.0 + jnp.exp(-jnp.abs(nwl)))
    decay = jnp.exp(-jnp.exp(-softplus - 0.5))
    a = jax.nn.sigmoid(a0 + _bdot(t, a2p))
    kk = k * k_k
    kk = kk / jnp.maximum(jnp.sqrt(_segsum(kk * kk)), 1e-12)
    k2 = k * (1.0 + (a - 1.0) * k_a)
    return r, decay, k2, v, -kk, kk * a


def _f_head(ys, r, k, v, ym, z1, z2, x, tgt, ln_g, ln_b, r_k, w1, w2, g_post):
    inv = 1.0 / RW_HEAD
    yc = ys - _segsum(ys) * inv
    var = _segsum(yc * yc) * inv
    y = yc * lax.rsqrt(var + RW_GN_EPS) * ln_g + ln_b
    y_rw = y + _segsum(r * k * r_k) * v
    c1 = ym * (z1 * jax.nn.sigmoid(z1))
    c2 = y_rw * (z2 * jax.nn.sigmoid(z2))
    out = _bdot(c1, w1) + _bdot(c2, w2)
    err = x + _rms(out, g_post) - tgt
    per_row = jnp.sum(err * err, axis=1, keepdims=True)
    return jnp.sum(per_row, axis=0, keepdims=True) * (0.5 / D_MODEL)


def _rows(tm, width):
    return pl.BlockSpec((tm, width), lambda i: (i, 0))


def _whole(shape):
    zeros = (0,) * len(shape)
    return pl.BlockSpec(shape, lambda i: zeros)


def _sds(shape, dtype=F32):
    return jax.ShapeDtypeStruct(shape, dtype)


_ARB1 = pltpu.CompilerParams(dimension_semantics=("arbitrary",))


def _acc(ref, val, first):
    @pl.when(first)
    def _():
        ref[...] = val

    @pl.when(jnp.logical_not(first))
    def _():
        ref[...] += val


def _fwd_a(x2, g_pre, wp, tm):
    n = x2.shape[0]

    def body(x_ref, g_ref, w_ref, ut_ref, pm_ref, prw_ref, z_ref):
        u = _rms(x_ref[...], g_ref[...])
        ub = u.astype(BF16)
        ut_ref[...] = u.T.astype(BF16)
        pm_ref[...] = jnp.dot(ub, w_ref[:, 0:PM_W], preferred_element_type=F32)
        prw_ref[...] = jnp.dot(ub, w_ref[:, PM_W:PM_W + RW_COLS], preferred_element_type=F32)
        z_ref[...] = jnp.dot(ub, w_ref[:, PM_W + RW_COLS:WP_COLS], preferred_element_type=F32)

    return pl.pallas_call(
        body, name="fwd_a", grid=(n // tm,),
        in_specs=[_rows(tm, D_MODEL), _whole((1, D_MODEL)), _whole((D_MODEL, WP_COLS))],
        out_specs=[pl.BlockSpec((D_MODEL, tm), lambda i: (0, i)), _rows(tm, PM_W), _rows(tm, RW_COLS), _rows(tm, D_MODEL)],
        out_shape=[_sds((D_MODEL, n), BF16), _sds((n, PM_W)), _sds((n, RW_COLS)), _sds((n, D_MODEL))],
        compiler_params=_ARB1,
    )(x2, g_pre, wp)


def _rope_tables(pos_row, invf_col, tm):
    n = pos_row.shape[1]

    def body(p_ref, f_ref, c_ref, s_ref):
        ang = f_ref[...] * p_ref[...].astype(F32)
        c_ref[...] = jnp.cos(ang).T
        s_ref[...] = jnp.sin(ang).T

    return pl.pallas_call(
        body, name="rope_tables", grid=(n // tm,),
        in_specs=[pl.BlockSpec((1, tm), lambda i: (0, i)), _whole((LANES, 1))],
        out_specs=[_rows(tm, LANES), _rows(tm, LANES)],
        out_shape=[_sds((n, LANES)), _sds((n, LANES))],
        compiler_params=_ARB1,
    )(pos_row, invf_col)


_B_WEIGHT_SHAPES = ((1, Q_LORA), (Q_LORA, 512), (Q_LORA, 512), (1, KV_LORA), (KV_LORA, 1024), (1, RW_COLS), (1, RW_WIDTH),
                    (LANES, RW_WIDTH), (1, RW_WIDTH), (LANES, RW_WIDTH), (1, RW_WIDTH), (1, RW_WIDTH))


def _halo_prev(tm):
    return pl.BlockSpec((8, RW_COLS), lambda i: (jnp.maximum(i * (tm // 8) - 1, 0), 0))


def _b_operands(pm_ref, prw_ref, halo_ref, wrefs, tile, tiles_per_seq):
    g_q, wqn, wqr, g_kv, wkv, mu, w0, w2p, a0, a2p, k_k, k_a = wrefs
    mla_in = (pm_ref[:, 0:Q_LORA], pm_ref[:, Q_LORA:Q_LORA + KV_LORA], pm_ref[:, Q_LORA + KV_LORA:PM_W])
    mla_w = (g_q[...], wqn[...], wqr[...], g_kv[...], wkv[...])
    keep = jnp.where(tile % tiles_per_seq == 0, 0.0, 1.0)
    prev = halo_ref[7:8, :] * keep
    ps = tuple(prw_ref[:, a:b] for a, b in RW_PIECES)
    ss = tuple(_shift_rows(p, prev[:, a:b]) for p, (a, b) in zip(ps, RW_PIECES))
    rw_w = tuple(mu[:, a:b] for a, b in RW_PIECES) + (w0[...], w2p[...], a0[...], a2p[...], k_k[...], k_a[...])
    return mla_in, mla_w, ps + ss, rw_w


def _fwd_b(pm, prw, cos, sin, bw, tm, tiles_per_seq):
    n = pm.shape[0]

    def body(pm_ref, prw_ref, halo_ref, cos_ref, sin_ref, *refs):
        wrefs, outs = refs[:12], refs[12:]
        mla_in, mla_w, rw_in, rw_w = _b_operands(pm_ref, prw_ref, halo_ref, wrefs, pl.program_id(0), tiles_per_seq)
        res = _f_mla(*mla_in, cos_ref[...], sin_ref[...], *mla_w) + _f_rw(*rw_in, *rw_w)
        for o_ref, val in zip(outs, res):
            o_ref[...] = val.astype(o_ref.dtype)

    widths = (512, 512, 1024, LANES) + (RW_WIDTH,) * 6
    return pl.pallas_call(
        body, name="fwd_b", grid=(n // tm,),
        in_specs=[_rows(tm, PM_W), _rows(tm, RW_COLS), _halo_prev(tm), _rows(tm, LANES), _rows(tm, LANES)]
        + [_whole(s) for s in _B_WEIGHT_SHAPES],
        out_specs=[_rows(tm, w) for w in widths],
        out_shape=[_sds((n, w), BF16 if j < 4 else F32) for j, w in enumerate(widths)],
        compiler_params=_ARB1,
    )(pm, prw, prw, cos, sin, *bw)


def _bwd_b(pm, prw, cos, sin, bw, cts, dkr_heads, tm, tiles_per_seq):
    n = pm.shape[0]

    ct_widths = (512, 512, 1024) + (RW_WIDTH,) * 9
    n_ct = len(ct_widths)

    def body(pm_ref, prw_ref, halo_ref, cos_ref, sin_ref, *refs):
        wrefs, ct_refs, dkr_ref = refs[:12], refs[12:12 + n_ct], refs[12 + n_ct]
        dpm_ref, dprw_ref, dps_ref = refs[13 + n_ct:16 + n_ct]
        wg_refs = refs[16 + n_ct:]
        tile = pl.program_id(0)
        first = tile == 0
        mla_in, mla_w, rw_in, rw_w = _b_operands(pm_ref, prw_ref, halo_ref, wrefs, tile, tiles_per_seq)
        cos, sin = cos_ref[...], sin_ref[...]
        ct = [r[...] for r in ct_refs]
        _, vjp_mla = jax.vjp(lambda *a: _f_mla(*a[:3], cos, sin, *a[3:]), *mla_in, *mla_w)
        dkr = dkr_ref[0] + dkr_ref[1] + dkr_ref[2] + dkr_ref[3]
        d_mla = vjp_mla((ct[0], ct[1], ct[2], dkr))
        dpm_ref[:, 0:Q_LORA] = d_mla[0]
        dpm_ref[:, Q_LORA:Q_LORA + KV_LORA] = d_mla[1]
        dpm_ref[:, Q_LORA + KV_LORA:PM_W] = d_mla[2]
        _, vjp_rw = jax.vjp(_f_rw, *rw_in, *rw_w)
        d_rw = vjp_rw((ct[3] + ct[4], ct[5], ct[6] + ct[7], ct[8] + ct[9], ct[10], ct[11]))
        for j, (a, b) in enumerate(RW_PIECES):
            dprw_ref[:, a:b] = d_rw[j]
            dps_ref[:, a:b] = d_rw[4 + j]
        g_q, wqn, wqr, g_kv, wkv, mu, w0, w2p, a0, a2p, k_k, k_a = wg_refs
        for ref, val in zip((g_q, wqn, wqr, g_kv, wkv), d_mla[3:]):
            _acc(ref, val, first)
        for j, (a, b) in enumerate(RW_PIECES):
            _acc(mu.at[:, a:b], d_rw[8 + j], first)
        for ref, val in zip((w0, w2p, a0, a2p, k_k, k_a), d_rw[12:]):
            _acc(ref, val, first)

    return pl.pallas_call(
        body, name="bwd_b", grid=(n // tm,),
        in_specs=[_rows(tm, PM_W), _rows(tm, RW_COLS), _halo_prev(tm), _rows(tm, LANES), _rows(tm, LANES)]
        + [_whole(s) for s in _B_WEIGHT_SHAPES] + [_rows(tm, w) for w in ct_widths]
        + [pl.BlockSpec((MLA_HEADS, tm, LANES), lambda i: (0, i, 0))],
        out_specs=[_rows(tm, PM_W), _rows(tm, RW_COLS), _rows(tm, RW_COLS)] + [_whole(s) for s in _B_WEIGHT_SHAPES],
        out_shape=[_sds((n, PM_W)), _sds((n, RW_COLS)), _sds((n, RW_COLS))] + [_sds(s) for s in _B_WEIGHT_SHAPES],
        compiler_params=_ARB1,
    )(pm, prw, prw, cos, sin, *bw, *cts, dkr_heads)


def _head(ys, r, k, v, ym, z, x2, tgt, hw, tm):
    n = x2.shape[0]
    h_shapes = ((1, RW_WIDTH), (1, RW_WIDTH), (1, RW_WIDTH), (D_MODEL, D_MODEL), (1, D_MODEL))

    def body(ys_ref, r_ref, k_ref, v_ref, ym_ref, z_ref, x_ref, t_ref, lng, lnb, rk, wout, gpost,
             dys_ref, dr_ref, dk_ref, dv_ref, dym_ref, dz_ref, dx_ref, loss_ref, dlng, dlnb, drk, dwout, dgpost):
        first = pl.program_id(0) == 0
        tgt_v = t_ref[...]
        args = (ys_ref[...], r_ref[...], k_ref[...], v_ref[...], ym_ref[...], z_ref[:, 0:MLA_WIDTH], z_ref[:, MLA_WIDTH:D_MODEL],
                x_ref[...], lng[...], lnb[...], rk[...], wout[0:MLA_WIDTH, :], wout[MLA_WIDTH:D_MODEL, :], gpost[...])
        loss, vjp = jax.vjp(lambda *a: _f_head(*a[:8], tgt_v, *a[8:]), *args)
        d = vjp(jnp.ones((1, 1), F32))
        dys_ref[...] = d[0]
        dr_ref[...] = d[1]
        dk_ref[...] = d[2]
        dv_ref[...] = d[3]
        dym_ref[...] = d[4].astype(BF16)
        dz_ref[:, 0:MLA_WIDTH] = d[5]
        dz_ref[:, MLA_WIDTH:D_MODEL] = d[6]
        dx_ref[...] = d[7]
        _acc(loss_ref, jnp.broadcast_to(loss, (8, LANES)), first)
        _acc(dlng, d[8], first)
        _acc(dlnb, d[9], first)
        _acc(drk, d[10], first)
        _acc(dwout.at[0:MLA_WIDTH, :], d[11], first)
        _acc(dwout.at[MLA_WIDTH:D_MODEL, :], d[12], first)
        _acc(dgpost, d[13], first)

    widths = (RW_WIDTH,) * 4 + (MLA_WIDTH, D_MODEL, D_MODEL)
    return pl.pallas_call(
        body, name="head", grid=(n // tm,),
        in_specs=[_rows(tm, RW_WIDTH)] * 4 + [_rows(tm, MLA_WIDTH), _rows(tm, D_MODEL), _rows(tm, D_MODEL), _rows(tm, D_MODEL)]
        + [_whole(s) for s in h_shapes],
        out_specs=[_rows(tm, w) for w in widths] + [_whole((8, LANES))] + [_whole(s) for s in h_shapes],
        out_shape=[_sds((n, w), BF16 if j == 4 else F32) for j, w in enumerate(widths)] + [_sds((8, LANES))]
        + [_sds(s) for s in h_shapes],
        compiler_params=_ARB1,
    )(ys, r, k, v, ym, z, x2, tgt, *hw)


def _halo_next(tm, n):
    last = n // 8 - 1
    return pl.BlockSpec((8, RW_COLS), lambda i: (jnp.minimum((i + 1) * (tm // 8), last), 0))


def _bwd_a(x2, g_pre, wp, dpm, dprw, dps, dz, dxres, tm, tiles_per_seq):
    n = x2.shape[0]
    nt_dims = (((1,), (1,)), ((), ()))

    def body(x_ref, g_ref, w_ref, dpm_ref, dprw_ref, dps_ref, nxt_ref, dz_ref, dxres_ref, gx_ref, dpb_ref, dg_ref):
        tile = pl.program_id(0)
        keep = jnp.where((tile + 1) % tiles_per_seq == 0, 0.0, 1.0)
        dprw_v = dprw_ref[...] + _unshift_rows(dps_ref[...], nxt_ref[0:1, :] * keep)
        dpm_b, dprw_b, dz_b = dpm_ref[...].astype(BF16), dprw_v.astype(BF16), dz_ref[...].astype(BF16)
        dpb_ref[:, 0:PM_W] = dpm_b
        dpb_ref[:, PM_W:PM_W + RW_COLS] = dprw_b
        dpb_ref[:, PM_W + RW_COLS:WP_COLS] = dz_b
        du = (lax.dot_general(dpm_b, w_ref[:, 0:PM_W], nt_dims, preferred_element_type=F32)
              + lax.dot_general(dprw_b, w_ref[:, PM_W:PM_W + RW_COLS], nt_dims, preferred_element_type=F32)
              + lax.dot_general(dz_b, w_ref[:, PM_W + RW_COLS:WP_COLS], nt_dims, preferred_element_type=F32))
        x = x_ref[...]
        xhat = x * lax.rsqrt(jnp.mean(x * x, axis=-1, keepdims=True) + NORM_EPS)
        dxn = du * g_ref[...]
        dx = (dxn - xhat * jnp.mean(dxn * xhat, axis=-1, keepdims=True)) * lax.rsqrt(jnp.mean(x * x, axis=-1, keepdims=True) + NORM_EPS)
        gx_ref[...] = dx + dxres_ref[...]
        _acc(dg_ref, jnp.sum(du * xhat, axis=0, keepdims=True), tile == 0)

    return pl.pallas_call(
        body, name="bwd_a", grid=(n // tm,),
        in_specs=[_rows(tm, D_MODEL), _whole((1, D_MODEL)), _whole((D_MODEL, WP_COLS)), _rows(tm, PM_W), _rows(tm, RW_COLS),
                  _rows(tm, RW_COLS), _halo_next(tm, n), _rows(tm, D_MODEL), _rows(tm, D_MODEL)],
        out_specs=[_rows(tm, D_MODEL), _rows(tm, WP_COLS), _whole((1, D_MODEL))],
        out_shape=[_sds((n, D_MODEL)), _sds((n, WP_COLS), BF16), _sds((1, D_MODEL))],
        compiler_params=_ARB1,
    )(x2, g_pre, wp, dpm, dprw, dps, dps, dz, dxres)


def _dw_in(ut, dpb, tk, tn):
    n = ut.shape[1]
    steps = n // tk

    def body(u_ref, d_ref, o_ref, acc_sc):
        k = pl.program_id(1)
        _acc(acc_sc, jnp.dot(u_ref[...], d_ref[...], preferred_element_type=F32), k == 0)

        @pl.when(k == steps - 1)
        def _():
            o_ref[...] = acc_sc[...].astype(BF16)

    return pl.pallas_call(
        body, name="dw_in", grid=(WP_COLS // tn, steps),
        in_specs=[pl.BlockSpec((D_MODEL, tk), lambda j, k: (0, k)), pl.BlockSpec((tk, tn), lambda j, k: (k, j))],
        out_specs=pl.BlockSpec((D_MODEL, tn), lambda j, k: (0, j)),
        out_shape=_sds((D_MODEL, WP_COLS), BF16),
        scratch_shapes=[pltpu.VMEM((D_MODEL, tn), F32)],
        compiler_params=pltpu.CompilerParams(dimension_semantics=("arbitrary", "arbitrary")),
    )(ut, dpb)


ATT_BLK = 256
_NT = (((1,), (1,)), ((), ()))
_TN = (((0,), (0,)), ((), ()))


def _causal(q0, k0, blk, blk_k=None):
    blk_k = blk if blk_k is None else blk_k
    row = q0 + lax.broadcasted_iota(jnp.int32, (blk, blk_k), 0)
    col = k0 + lax.broadcasted_iota(jnp.int32, (blk, blk_k), 1)
    return row >= col


def _attn_fwd(qn, qr, kv, kr):
    bsz, t, _ = qn.shape
    blk = min(ATT_BLK, t)

    heads = range(MLA_HEADS)

    def body(qn_ref, qr_ref, kv_ref, kr_ref, o_ref, lse_ref):
        qi = pl.program_id(1)
        q = [jnp.concatenate([qn_ref[:, LANES * h:LANES * (h + 1)], qr_ref[:, LANES * h:LANES * (h + 1)]], axis=1) for h in heads]
        lower = _causal(0, 0, blk)

        def kv_step(j, carry, diagonal):
            ks = pl.multiple_of(j * blk, blk)
            k_rope = kr_ref[pl.ds(ks, blk), :]
            def score(h):
                k = jnp.concatenate([kv_ref[pl.ds(ks, blk), 2 * LANES * h:2 * LANES * h + LANES], k_rope], axis=1)
                return lax.dot_general(q[h], k, _NT, preferred_element_type=F32)

            out = []
            nxt = score(0)
            for h in heads:
                s = nxt * ATT_SCALE
                if h + 1 < MLA_HEADS:
                    nxt = score(h + 1)
                m, l, acc = carry[h]
                if diagonal:
                    s = jnp.where(lower, s, -1e30)
                m_new = jnp.maximum(m, jnp.max(s, axis=1, keepdims=True))
                alpha = jnp.exp(m - m_new)
                p = jnp.exp(s - m_new)
                l = alpha * l + jnp.sum(p, axis=1, keepdims=True)
                v = kv_ref[pl.ds(ks, blk), 2 * LANES * h + LANES:2 * LANES * (h + 1)]
                out.append((m_new, l, alpha * acc + jnp.dot(p.astype(BF16), v, preferred_element_type=F32)))
            return tuple(out)

        one = (jnp.full((blk, 1), -1e30, F32), jnp.zeros((blk, 1), F32), jnp.zeros((blk, MLA_V), F32))
        carry = lax.fori_loop(0, qi, lambda j, c: kv_step(j, c, False), (one,) * MLA_HEADS)
        carry = kv_step(qi, carry, True)
        for h in heads:
            m, l, acc = carry[h]
            o_ref[:, LANES * h:LANES * (h + 1)] = acc / l
            lse_ref[h] = jnp.broadcast_to(m + jnp.log(l), (blk, LANES))

    return pl.pallas_call(
        body, name="attn_fwd", grid=(bsz, t // blk),
        in_specs=[pl.BlockSpec((None, blk, MLA_WIDTH), lambda b, i: (b, i, 0)),
                  pl.BlockSpec((None, blk, MLA_WIDTH), lambda b, i: (b, i, 0)),
                  pl.BlockSpec((None, t, 2 * MLA_WIDTH), lambda b, i: (b, 0, 0)),
                  pl.BlockSpec((None, t, LANES), lambda b, i: (b, 0, 0))],
        out_specs=[pl.BlockSpec((None, blk, MLA_WIDTH), lambda b, i: (b, i, 0)),
                   pl.BlockSpec((None, MLA_HEADS, blk, LANES), lambda b, i: (b, 0, i, 0))],
        out_shape=[_sds((bsz, t, MLA_WIDTH)), _sds((bsz, MLA_HEADS, t, LANES))],
        compiler_params=pltpu.CompilerParams(dimension_semantics=("arbitrary", "arbitrary")),
    )(qn, qr, kv, kr)


def _attn_bwd(qn, qr, kv, kr, o, do, lse):
    bsz, t, _ = qn.shape
    blk = min(ATT_BLK, t)
    nb = t // blk
    assert nb % 2 == 0, "query blocks are taken in pairs"

    def body(qn_ref, qr_ref, kn_ref, kr_ref, v_ref, o_ref, do_ref, lse_ref, dqn_ref, dqr_ref, dkv_ref, dkr_ref, dq_sc, delta_sc):
        dq_sc[...] = jnp.zeros_like(dq_sc)
        delta_sc[...] = jnp.sum(do_ref[...].astype(F32) * o_ref[...], axis=1, keepdims=True)

        lower = _causal(0, 0, blk)

        def q_blocks(j, k, vb, carry, blocks, diagonal):
            dk, dv = carry
            us = range(len(blocks))
            qs = [i * blk if isinstance(i, int) else pl.multiple_of(i * blk, blk) for i in blocks]
            q = [jnp.concatenate([qn_ref[pl.ds(qs[u], blk), :], qr_ref[pl.ds(qs[u], blk), :]], axis=1) for u in us]
            dob = [do_ref[pl.ds(qs[u], blk), :] for u in us]
            s = [lax.dot_general(q[u], k, _NT, preferred_element_type=F32) for u in us]
            dp = [lax.dot_general(dob[u], vb, _NT, preferred_element_type=F32) for u in us]
            pb, ds = [], []
            for u in us:
                p = jnp.exp(s[u] * ATT_SCALE - lse_ref[pl.ds(qs[u], blk), 0:1])
                if diagonal == u:
                    p = jnp.where(lower, p, 0.0)
                pb.append(p.astype(BF16))
                ds.append((p * (dp[u] - delta_sc[pl.ds(qs[u], blk), :]) * ATT_SCALE).astype(BF16))
            for u in us:
                dv = dv + lax.dot_general(pb[u], dob[u], _TN, preferred_element_type=F32)
            for u in us:
                dq_sc[pl.ds(qs[u], blk), :] += jnp.dot(ds[u], k, preferred_element_type=F32)
                dk = dk + lax.dot_general(ds[u], q[u], _TN, preferred_element_type=F32)
            return dk, dv

        for j in range(nb):
            ks = j * blk
            k = jnp.concatenate([kn_ref[ks:ks + blk, :], kr_ref[ks:ks + blk, :]], axis=1)
            vb = v_ref[ks:ks + blk, :]
            carry = (jnp.zeros((blk, 2 * LANES), F32), jnp.zeros((blk, MLA_V), F32))
            if j % 2 == 0:
                carry = q_blocks(j, k, vb, carry, [j, j + 1], 0)
            else:
                carry = q_blocks(j, k, vb, carry, [j], 0)
            pairs_from = j // 2 + 1
            if nb // 2 - pairs_from > 0:
                carry = lax.fori_loop(pairs_from, nb // 2,
                                      lambda pr, c, j=j, k=k, vb=vb: q_blocks(j, k, vb, c, [2 * pr, 2 * pr + 1], None),
                                      carry, unroll=2)
            dk, dv = carry
            dkv_ref[ks:ks + blk, 0:LANES] = dk[:, 0:LANES]
            dkv_ref[ks:ks + blk, LANES:2 * LANES] = dv
            dkr_ref[ks:ks + blk, :] = dk[:, LANES:2 * LANES]
        dqn_ref[...] = dq_sc[:, 0:LANES]
        dqr_ref[...] = dq_sc[:, LANES:2 * LANES]

    head_col = lambda b, h: (b, 0, h)
    return pl.pallas_call(
        body, name="attn_bwd", grid=(bsz, MLA_HEADS),
        in_specs=[pl.BlockSpec((None, t, LANES), head_col), pl.BlockSpec((None, t, LANES), head_col),
                  pl.BlockSpec((None, t, LANES), lambda b, h: (b, 0, 2 * h)),
                  pl.BlockSpec((None, t, LANES), lambda b, h: (b, 0, 0)),
                  pl.BlockSpec((None, t, LANES), lambda b, h: (b, 0, 2 * h + 1)),
                  pl.BlockSpec((None, t, LANES), head_col), pl.BlockSpec((None, t, LANES), head_col),
                  pl.BlockSpec((None, None, t, LANES), lambda b, h: (b, h, 0, 0))],
        out_specs=[pl.BlockSpec((None, t, LANES), head_col), pl.BlockSpec((None, t, LANES), head_col),
                   pl.BlockSpec((None, t, 2 * LANES), head_col),
                   pl.BlockSpec((None, None, t, LANES), lambda b, h: (h, b, 0, 0))],
        out_shape=[_sds((bsz, t, MLA_WIDTH)), _sds((bsz, t, MLA_WIDTH)), _sds((bsz, t, 2 * MLA_WIDTH)),
                   _sds((MLA_HEADS, bsz, t, LANES))],
        scratch_shapes=[pltpu.VMEM((t, 2 * LANES), F32), pltpu.VMEM((t, 1), F32)],
        compiler_params=pltpu.CompilerParams(dimension_semantics=("arbitrary", "arbitrary")),
    )(qn, qr, kv, kr, kv, o, do, lse)


SCAN_CHUNK = 16


def _diag_mask():
    row = lax.broadcasted_iota(jnp.int32, (RW_HEAD, RW_WIDTH), 0)
    lane = lax.broadcasted_iota(jnp.int32, (RW_HEAD, RW_WIDTH), 1)
    return jnp.where(row == (lane & (RW_HEAD - 1)), 1.0, 0.0)


def _time_minor(a):
    bsz, t, _ = a.shape
    a = a.reshape(bsz, t // SCAN_CHUNK, SCAN_CHUNK, RW_HEADS, RW_HEAD)
    return a.transpose(0, 1, 4, 3, 2).reshape(bsz, t // SCAN_CHUNK, RW_HEAD, RW_HEADS * SCAN_CHUNK)


def _head_expand():
    l = lax.broadcasted_iota(jnp.int32, (2 * LANES, RW_WIDTH), 0)
    n = lax.broadcasted_iota(jnp.int32, (2 * LANES, RW_WIDTH), 1)
    return jnp.where(((l & (LANES - 1)) >> 4) == (n >> 6), 1.0, 0.0).astype(BF16)


BCAST_GROUP = 4


def _outer_chunk(tm_ref, row_ref, out_sc, expand, seqs):
    step_of_lane = lax.broadcasted_iota(jnp.int32, (RW_HEAD, LANES), 1) & (SCAN_CHUNK - 1)
    tiles = [tm_ref[bi, 0] for bi in seqs]
    for t0 in range(0, SCAN_CHUNK, BCAST_GROUP):
        parts = []
        for t in range(t0, t0 + BCAST_GROUP):
            for tile in tiles:
                a = jnp.where(step_of_lane == t, tile, 0.0)
                hi = a.astype(BF16)
                parts.append(jnp.concatenate([hi, (a - hi.astype(F32)).astype(BF16)], axis=1))
        cols = jnp.dot(jnp.concatenate(parts, axis=0), expand, preferred_element_type=F32)
        for j, t in enumerate(range(t0, t0 + BCAST_GROUP)):
            base = j * RW_HEAD * len(seqs)
            out_sc[t] = jnp.concatenate([cols[base + RW_HEAD * bi:base + RW_HEAD * (bi + 1)] * row_ref[bi, t:t + 1, :]
                                         for bi in seqs], axis=0)


def _fold8(x):
    acc = x[0:8]
    for j in range(1, x.shape[0] // 8):
        acc = acc + x[8 * j:8 * (j + 1)]
    return acc


def _rows8(at):
    return pl.ds(at * 8 if isinstance(at, int) else pl.multiple_of(at * 8, 8), 8)


def _put8(sc, bi, at, val):
    for j in range(RW_WIDTH // LANES):
        sc[bi * (RW_WIDTH // LANES) + j, _rows8(at), :] = val[:, LANES * j:LANES * (j + 1)]


def _unfold8(sc, bi, steps):
    tiles = []
    for j in range(RW_WIDTH // LANES):
        view = sc.at[bi * (RW_WIDTH // LANES) + j]
        acc = view[pl.ds(0, steps, stride=8), :]
        for s in range(1, 8):
            acc = acc + view[pl.ds(s, steps, stride=8), :]
        tiles.append(acc)
    return jnp.concatenate(tiles, axis=1)


def _scan_fwd(r, w, k, vt, nkk, b):
    bsz, t, _ = r.shape
    tc = SCAN_CHUNK

    def body(r_ref, w_ref, k_ref, n_ref, b_ref, vt_ref, y_ref, st_ref, s_sc, vc_sc, y_sc):
        @pl.when(pl.program_id(0) == 0)
        def _():
            s_sc[...] = jnp.zeros_like(s_sc)

        ones = _seg_ones()
        diag = _diag_mask()
        seqs = range(bsz)
        _outer_chunk(vt_ref, k_ref, vc_sc, _head_expand(), seqs)

        def put_y(ya, at):
            for bi in seqs:
                _put8(y_sc, bi, at, _fold8(ya[bi] * diag))

        def step(i, _):
            row = lambda ref, bi: ref[bi, pl.ds(i, 1), :]
            prev = jnp.maximum(i - 1, 0)
            s_old = [s_sc[bi] for bi in seqs]
            s_b = [s_old[bi].astype(BF16) for bi in seqs]
            sa = _seg_multi([s_b[bi] * row(n_ref, bi).astype(BF16) for bi in seqs], ones, 1)
            put_y(_seg_multi([s_b[bi] * r_ref[bi, pl.ds(prev, 1), :].astype(BF16) for bi in seqs], ones, 1), prev)
            vk = vc_sc[i]
            for bi in seqs:
                s_new = s_old[bi] * row(w_ref, bi) + sa[bi] * row(b_ref, bi) + vk[RW_HEAD * bi:RW_HEAD * (bi + 1)]
                s_sc[bi] = s_new
                st_ref[bi, i] = s_new
            return 0

        lax.fori_loop(0, tc, step, 0, unroll=8)
        put_y(_seg_multi([s_sc[bi].astype(BF16) * r_ref[bi, tc - 1:tc, :].astype(BF16) for bi in seqs], ones, 1), tc - 1)
        for bi in seqs:
            y_ref[bi] = _unfold8(y_sc, bi, tc)

    vec = pl.BlockSpec((bsz, tc, RW_WIDTH), lambda c: (0, c, 0))
    return pl.pallas_call(
        body, name="scan_fwd", grid=(t // tc,),
        in_specs=[vec] * 5 + [pl.BlockSpec((bsz, 1, RW_HEAD, LANES), lambda c: (0, c, 0, 0))],
        out_specs=[vec, pl.BlockSpec((bsz, tc, RW_HEAD, RW_WIDTH), lambda c: (0, c, 0, 0))],
        out_shape=[_sds((bsz, t, RW_WIDTH)), _sds((bsz, t, RW_HEAD, RW_WIDTH))],
        scratch_shapes=[pltpu.VMEM((bsz, RW_HEAD, RW_WIDTH), F32), pltpu.VMEM((tc, bsz * RW_HEAD, RW_WIDTH), F32),
                        pltpu.VMEM((bsz * RW_WIDTH // LANES, tc * 8, LANES), F32)],
        compiler_params=_ARB1,
    )(r, w, k, nkk, b, vt)


def _own_head_row(x):
    first_half = lax.broadcasted_iota(jnp.int32, (1, LANES), 1) < RW_HEAD
    tiles = [jnp.where(first_half, x[2 * j:2 * j + 1, LANES * j:LANES * (j + 1)], x[2 * j + 1:2 * j + 2, LANES * j:LANES * (j + 1)])
             for j in range(RW_WIDTH // LANES)]
    return jnp.concatenate(tiles, axis=1)


def _scan_bwd(r, w, k, vt, nkk, b, st, dyt):
    bsz, t, _ = r.shape
    tc = SCAN_CHUNK
    nc = t // tc

    def body(r_ref, w_ref, k_ref, n_ref, b_ref, vt_ref, dyt_ref, st_ref, halo_ref,
             dr_ref, dw_ref, dk_ref, dv_ref, dn_ref, db_ref, g_sc, dc_sc, v8_sc, dy8_sc, *part_scs):
        c = pl.program_id(0)

        @pl.when(c == 0)
        def _():
            g_sc[...] = jnp.zeros_like(g_sc)

        ones = _seg_ones()
        diag = _diag_mask()
        has_prev = jnp.where(c == nc - 1, 0.0, 1.0)
        seqs = range(bsz)
        _outer_chunk(dyt_ref, r_ref, dc_sc, _head_expand(), seqs)
        for bi in seqs:
            v8_sc[bi] = jnp.concatenate([vt_ref[bi, 0].T] * 2, axis=1)
            dy8_sc[bi] = jnp.concatenate([dyt_ref[bi, 0].T] * 2, axis=1)
        by_head = lambda sc, bi, i: sc.at[bi][pl.ds(i, RW_HEADS, stride=SCAN_CHUNK), :][:, 0:RW_HEAD].astype(BF16)
        dw_sc, dv_sc, dn_sc, db_sc = part_scs

        def step(i, s_p):
            static = isinstance(i, int)
            row = lambda ref, bi: ref[bi, i:i + 1, :] if static else ref[bi, pl.ds(i, 1), :]
            put_row = lambda ref, bi, val: ref.__setitem__((bi, slice(i, i + 1) if static else pl.ds(i, 1), slice(None)), val)
            dr8 = [jnp.dot(by_head(dy8_sc, bi, i), st_ref[bi, i].astype(BF16), preferred_element_type=F32) for bi in seqs]
            rowb = lambda ref, bi: row(ref, bi).astype(BF16)
            sa = _seg_multi([s_p[bi].astype(BF16) * rowb(n_ref, bi) for bi in seqs], ones, 1)
            dc_all = dc_sc[i]
            dc = [dc_all[RW_HEAD * bi:RW_HEAD * (bi + 1)] for bi in seqs]
            g = [g_sc[bi] + dc[bi] for bi in seqs]
            g_b = [g[bi].astype(BF16) for bi in seqs]
            res = _seg_multi([g_b[bi] * rowb(b_ref, bi) for bi in seqs] + [g_b[bi] * rowb(k_ref, bi) for bi in seqs], ones, 1)
            dsa, dvb = res[:bsz], res[bsz:]
            for bi in seqs:
                dk8 = jnp.dot(by_head(v8_sc, bi, i), g_b[bi], preferred_element_type=F32)
                put_row(dr_ref, bi, _own_head_row(dr8[bi]))
                put_row(dk_ref, bi, _own_head_row(dk8))
                _put8(dv_sc, bi, i, _fold8(dvb[bi] * diag))
                _put8(dw_sc, bi, i, _fold8(g[bi] * s_p[bi]))
                _put8(db_sc, bi, i, _fold8(g[bi] * sa[bi]))
                _put8(dn_sc, bi, i, _fold8(s_p[bi] * dsa[bi]))
                g_sc[bi] = g[bi] * row(w_ref, bi) + dsa[bi] * row(n_ref, bi)

        def loop_step(ii, _):
            i = tc - 1 - ii
            step(i, [st_ref[bi, i - 1] for bi in seqs])
            return 0

        lax.fori_loop(0, tc - 1, loop_step, 0, unroll=5)
        step(0, [halo_ref[bi, 0] * has_prev for bi in seqs])
        for out_ref, sc in zip((dw_ref, dv_ref, dn_ref, db_ref), part_scs):
            for bi in seqs:
                out_ref[bi] = _unfold8(sc, bi, tc)

    vec = pl.BlockSpec((bsz, tc, RW_WIDTH), lambda c: (0, nc - 1 - c, 0))
    tmin = pl.BlockSpec((bsz, 1, RW_HEAD, LANES), lambda c: (0, nc - 1 - c, 0, 0))
    parts = pltpu.VMEM((bsz * RW_WIDTH // LANES, tc * 8, LANES), F32)
    heads_steps = pltpu.VMEM((bsz, LANES, LANES), F32)
    return pl.pallas_call(
        body, name="scan_bwd", grid=(nc,),
        in_specs=[vec] * 5 + [tmin, tmin,
                              pl.BlockSpec((bsz, tc, RW_HEAD, RW_WIDTH), lambda c: (0, nc - 1 - c, 0, 0)),
                              pl.BlockSpec((bsz, 1, RW_HEAD, RW_WIDTH), lambda c: (0, jnp.maximum((nc - 1 - c) * tc - 1, 0), 0, 0))],
        out_specs=[vec] * 6,
        out_shape=[_sds((bsz, t, RW_WIDTH))] * 6,
        scratch_shapes=[pltpu.VMEM((bsz, RW_HEAD, RW_WIDTH), F32), pltpu.VMEM((tc, bsz * RW_HEAD, RW_WIDTH), F32),
                        heads_steps, heads_steps] + [parts] * 4,
        compiler_params=_ARB1,
    )(r, w, k, nkk, b, vt, dyt, st, st)


TOKEN_TILE = 256


def _padded_weights(wt):
    f = lambda a: a.astype(F32)
    w_in = wt["w_in"][0].astype(BF16)
    zeros = lambda r, c: jnp.zeros((r, c), F32)
    wp = jnp.concatenate([w_in[:, :MLA_COLS], jnp.zeros((D_MODEL, PM_W - MLA_COLS), BF16), w_in[:, MLA_COLS:]], axis=1)
    w_uq = f(wt["mla_w_uq"][0]).reshape(Q_LORA, MLA_HEADS, MLA_NOPE + MLA_ROPE)
    wqn = w_uq[:, :, :MLA_NOPE].reshape(Q_LORA, MLA_HEADS * MLA_NOPE)
    wqr = jnp.concatenate([w_uq[:, :, MLA_NOPE:], jnp.zeros((Q_LORA, MLA_HEADS, LANES - MLA_ROPE), F32)], axis=2)
    wqr = wqr.reshape(Q_LORA, MLA_HEADS * LANES)
    w2p = jnp.concatenate([f(wt["rw_w2"][0]), zeros(LORA, RW_WIDTH)], axis=0)
    a2p = jnp.concatenate([zeros(LORA, RW_WIDTH), f(wt["rw_a2"][0])], axis=0)
    bw = (f(wt["mla_q_norm_g"]), wqn, wqr, f(wt["mla_kv_norm_g"]), f(wt["mla_w_ukv"][0]), f(wt["rw_mu"]), f(wt["rw_w0"]),
          w2p, f(wt["rw_a0"]), a2p, f(wt["rw_k_k"]), f(wt["rw_k_a"]))
    hw = (f(wt["rw_ln_g"]), f(wt["rw_ln_b"]), f(wt["rw_r_k"]).reshape(1, RW_WIDTH), f(wt["w_out"][0]), f(wt["norm_post_g"]))
    return wp, bw, hw


def _local_step(x, positions, target, wt):
    bsz, t, _ = x.shape
    n = bsz * t
    tm = min(TOKEN_TILE, t)
    tps = t // tm
    wp, bw, hw = _padded_weights(wt)
    wpb = wp.astype(BF16)
    g_pre = wt["norm_pre_g"].astype(F32)
    x2 = x.reshape(n, D_MODEL)
    tgt2 = target.reshape(n, D_MODEL)
    inv_freq = ROPE_THETA ** (-jnp.arange(0, MLA_ROPE, 2, dtype=F32) / MLA_ROPE)
    invf = jnp.tile(inv_freq, LANES // (MLA_ROPE // 2)).reshape(LANES, 1)
    cos, sin = _rope_tables(positions.reshape(1, n), invf, tm)

    u, pm, prw, z = _fwd_a(x2, g_pre, wpb, tm)
    qn, qr, kv, kr, r, w, k, v, nkk, b = _fwd_b(pm, prw, cos, sin, bw, tm, tps)
    b3 = lambda a: a.reshape(bsz, t, a.shape[-1])
    ym, lse = _attn_fwd(b3(qn), b3(qr), b3(kv), b3(kr))
    vt = _time_minor(b3(v))
    ys, st = _scan_fwd(b3(r), b3(w), b3(k), vt, b3(nkk), b3(b))
    (dys, dr_h, dk_h, dv_h, dym, dz, dxres, loss, d_lng, d_lnb, d_rk, d_wout, d_gpost) = _head(
        ys.reshape(n, RW_WIDTH), r, k, v, ym.reshape(n, MLA_WIDTH), z, x2, tgt2, hw, tm)
    dqn, dqr, dkv, dkr_heads = _attn_bwd(b3(qn), b3(qr), b3(kv), b3(kr), ym, b3(dym), lse)
    dr_s, dw_s, dk_s, dv_s, dn_s, db_s = _scan_bwd(b3(r), b3(w), b3(k), vt, b3(nkk), b3(b), st, _time_minor(b3(dys)))
    f2 = lambda a: a.reshape(n, a.shape[-1])
    cts = (f2(dqn), f2(dqr), f2(dkv), f2(dr_s), dr_h, f2(dw_s), f2(dk_s), dk_h, f2(dv_s), dv_h, f2(dn_s), f2(db_s))
    (dpm, dprw, dps, d_gq, d_wqn, d_wqr, d_gkv, d_wkv, d_mu, d_w0, d_w2p, d_a0, d_a2p, d_kk, d_ka) = _bwd_b(
        pm, prw, cos, sin, bw, cts, dkr_heads.reshape(MLA_HEADS, n, LANES), tm, tps)
    grad_x, dpb, d_gpre = _bwd_a(x2, g_pre, wpb, dpm, dprw, dps, dz, dxres, tm, tps)
    d_wp = _dw_in(u, dpb, min(1024, n), 640)

    d_w_in = jnp.concatenate([d_wp[:, :MLA_COLS], d_wp[:, PM_W:]], axis=1)
    d_w_uq = jnp.concatenate([d_wqn.reshape(Q_LORA, MLA_HEADS, MLA_NOPE),
                              d_wqr.reshape(Q_LORA, MLA_HEADS, LANES)[:, :, :MLA_ROPE]], axis=2)
    grads = {
        "norm_pre_g": d_gpre, "w_in": d_w_in[None], "mla_q_norm_g": d_gq,
        "mla_w_uq": d_w_uq.reshape(1, Q_LORA, MLA_HEADS * (MLA_NOPE + MLA_ROPE)), "mla_kv_norm_g": d_gkv,
        "mla_w_ukv": d_wkv[None], "rw_mu": d_mu, "rw_w0": d_w0, "rw_w2": d_w2p[None, :LORA], "rw_a0": d_a0,
        "rw_a2": d_a2p[None, LORA:], "rw_k_k": d_kk, "rw_k_a": d_ka, "rw_r_k": d_rk.reshape(1, RW_HEADS, RW_HEAD),
        "rw_ln_g": d_lng, "rw_ln_b": d_lnb, "w_out": d_wout[None], "norm_post_g": d_gpost,
    }
    return loss, grad_x.reshape(bsz, t, D_MODEL), grads


_MESH = pl.DeviceIdType.MESH


def _gather_shards(shards):
    na = len(shards)

    def body(*refs):
        x_refs, out_refs = refs[:na], refs[na:2 * na]
        send_sems, recv_sems, local_sems = refs[2 * na:]
        x, y, c = lax.axis_index("x"), lax.axis_index("y"), lax.axis_index("c")
        me, sibling = (x, y, c), (x, y, 1 - c)
        chips = [(1 - x, y), (x, 1 - y), (1 - x, 1 - y)]
        arrays = range(na)

        def slot(a, px, py, pc):
            return out_refs[a].at[4 * px + 2 * py + pc]

        def copy(k, a, block, to, src=None):
            return pltpu.make_async_remote_copy(
                src_ref=slot(a, *block) if src is None else src, dst_ref=slot(a, *block),
                send_sem=send_sems.at[k, a], recv_sem=recv_sems.at[k, a], device_id=to, device_id_type=_MESH)

        mine = [pltpu.make_async_copy(x_refs[a], slot(a, *me), local_sems.at[a]) for a in arrays]
        for cp in mine:
            cp.start()
        first = [copy(0, a, me, sibling, src=x_refs[a]) for a in arrays]
        first += [copy(1 + j, a, me, (*chip, c), src=x_refs[a]) for j, chip in enumerate(chips) for a in arrays]
        for cp in first:
            cp.start()
        passed = []
        for j, chip in enumerate(chips):
            for a in arrays:
                copy(1 + j, a, (*chip, c), me).wait_recv()
                passed.append(copy(4 + j, a, (*chip, c), sibling))
                passed[-1].start()
        for a in arrays:
            copy(0, a, sibling, me).wait_recv()
        for j, chip in enumerate(chips):
            for a in arrays:
                copy(4 + j, a, (*chip, 1 - c), me).wait_recv()
        for cp in first + passed:
            cp.wait_send()
        for cp in mine:
            cp.wait()

    vmem = pl.BlockSpec(memory_space=pltpu.VMEM)
    return pl.pallas_call(
        body, name="gather_shards",
        out_shape=[_sds((N_DEV,) + a.shape, a.dtype) for a in shards],
        in_specs=[vmem] * na, out_specs=[vmem] * na,
        scratch_shapes=[pltpu.SemaphoreType.DMA((7, na)), pltpu.SemaphoreType.DMA((7, na)), pltpu.SemaphoreType.DMA((na,))],
    )(*shards)


SMALL_LANES = SMALL_N + LANES


def _exchange_grads(big_blocks, small_grads, loss_tile):
    nb = len(big_blocks)
    ns = len(small_grads)

    def body(*refs):
        big, small, loss_ref = refs[:nb], refs[nb:nb + ns], refs[nb + ns]
        rbig, rsmall = refs[nb + ns + 1:2 * nb + ns + 1], refs[2 * nb + ns + 1]
        send_b, recv_b, send_s, recv_s, local_sems, row_sc = refs[2 * nb + ns + 2:]
        x, y, c = lax.axis_index("x"), lax.axis_index("y"), lax.axis_index("c")
        me_lin = 4 * x + 2 * y + c
        mine = [pltpu.make_async_copy(big[j].at[me_lin], rbig[j].at[0], local_sems.at[j]) for j in range(nb)]
        for cp in mine:
            cp.start()
        off = 0
        for ref, (_, cnt) in zip(small, SMALL):
            row_sc[:, off:off + cnt] = ref[...]
            off += cnt
        row_sc[:, off:off + LANES] = loss_ref[0:1, :]
        rsmall[me_lin] = row_sc[...]
        copies = []
        for k in range(1, N_DEV):
            px, py, pc = x ^ (k >> 2), y ^ ((k >> 1) & 1), c ^ (k & 1)
            peer = (px, py, pc)
            for j in range(nb):
                copies.append(pltpu.make_async_remote_copy(
                    src_ref=big[j].at[4 * px + 2 * py + pc], dst_ref=rbig[j].at[k],
                    send_sem=send_b.at[k - 1, j], recv_sem=recv_b.at[k - 1, j], device_id=peer, device_id_type=_MESH))
            copies.append(pltpu.make_async_remote_copy(
                src_ref=row_sc, dst_ref=rsmall.at[me_lin],
                send_sem=send_s.at[k - 1], recv_sem=recv_s.at[k - 1], device_id=peer, device_id_type=_MESH))
        for cp in copies:
            cp.start()
        for cp in copies:
            cp.wait_recv()
        for cp in copies:
            cp.wait_send()
        for cp in mine:
            cp.wait()

    hbm, vmem = pl.BlockSpec(memory_space=pl.ANY), pl.BlockSpec(memory_space=pltpu.VMEM)
    return pl.pallas_call(
        body, name="exchange_grads",
        out_shape=[_sds(a.shape, a.dtype) for a in big_blocks] + [_sds((N_DEV, 1, SMALL_LANES))],
        in_specs=[hbm] * nb + [vmem] * (ns + 1),
        out_specs=[hbm] * nb + [vmem],
        scratch_shapes=[pltpu.SemaphoreType.DMA((N_DEV - 1, nb)), pltpu.SemaphoreType.DMA((N_DEV - 1, nb)),
                        pltpu.SemaphoreType.DMA((N_DEV - 1,)), pltpu.SemaphoreType.DMA((N_DEV - 1,)),
                        pltpu.SemaphoreType.DMA((nb,)), pltpu.VMEM((1, SMALL_LANES), F32)],
    )(*big_blocks, *small_grads, loss_tile)


def _adamw_math(w, g, m, v):
    m = ADAM_B1 * m + (1.0 - ADAM_B1) * g
    v = ADAM_B2 * v + (1.0 - ADAM_B2) * (g * g)
    m_hat = m / (1.0 - ADAM_B1 ** ADAM_STEP)
    v_hat = v / (1.0 - ADAM_B2 ** ADAM_STEP)
    return -ADAM_LR * (m_hat / (jnp.sqrt(v_hat) + ADAM_EPS) + ADAM_WD * w), m, v


def _reduce_adamw(name, parts, w, m, v, row_blocks):
    _, rows, cols = parts.shape
    rb = rows // row_blocks

    def body(p_ref, w_ref, m_ref, v_ref, g_out, d_out, m_out, v_out):
        g = p_ref[0].astype(F32)
        for s in range(1, N_DEV):
            g = g + p_ref[s].astype(F32)
        g_out[0] = g
        d_out[0], m_out[0], v_out[0] = _adamw_math(w_ref[0], g, m_ref[0], v_ref[0])

    blk = pl.BlockSpec((1, rb, cols), lambda i: (0, i, 0))
    return pl.pallas_call(
        body, name="reduce_adamw_" + name, grid=(row_blocks,),
        in_specs=[pl.BlockSpec((N_DEV, rb, cols), lambda i: (0, i, 0)), blk, blk, blk],
        out_specs=[blk] * 4, out_shape=[_sds((1, rows, cols))] * 4,
        compiler_params=_ARB1,
    )(parts, w, m, v)


def _reduce_adamw_small(rows, ws, ms, vs):
    ns = len(SMALL)

    def body(r_ref, *refs):
        w_refs, m_refs, v_refs, outs = refs[:ns], refs[ns:2 * ns], refs[2 * ns:3 * ns], refs[3 * ns:]
        total = r_ref[0]
        for s in range(1, N_DEV):
            total = total + r_ref[s]
        off = 0
        for j, (_, cnt) in enumerate(SMALL):
            g = total[:, off:off + cnt]
            off += cnt
            outs[4 * j][...] = g
            outs[4 * j + 1][...], outs[4 * j + 2][...], outs[4 * j + 3][...] = _adamw_math(
                w_refs[j][...], g, m_refs[j][...], v_refs[j][...])
        outs[4 * ns][...] = total[:, off:off + LANES]

    vmem = pl.BlockSpec(memory_space=pltpu.VMEM)
    return pl.pallas_call(
        body, name="reduce_adamw_small",
        in_specs=[vmem] * (1 + 3 * ns), out_specs=[vmem] * (4 * ns + 1),
        out_shape=[_sds((1, cnt)) for _, cnt in SMALL for _ in range(4)] + [_sds((1, LANES))],
    )(rows, *ws, *ms, *vs)


def _shard_blocks(name, full):
    a = full[0]
    rows, cols = a.shape
    if name == "w_out":
        return a.reshape(N_DEV, rows // N_DEV, cols)
    return a.reshape(rows, N_DEV, cols // N_DEV).transpose(1, 0, 2)


def _unshard(name, blocks):
    _, rows, cols = blocks.shape
    if name == "w_out":
        return blocks.reshape(1, N_DEV * rows, cols)
    return blocks.transpose(1, 0, 2).reshape(1, rows, N_DEV * cols)


def kernel(x, positions, norm_pre_g, w_in, mla_q_norm_g, mla_w_uq, mla_kv_norm_g, mla_w_ukv, rw_mu, rw_w0, rw_w2, rw_a0, rw_a2, rw_k_k, rw_k_a, rw_r_k, rw_ln_g, rw_ln_b, w_out, norm_post_g, loss_target, m_norm_pre_g, m_w_in, m_mla_q_norm_g, m_mla_w_uq, m_mla_kv_norm_g, m_mla_w_ukv, m_rw_mu, m_rw_w0, m_rw_w2, m_rw_a0, m_rw_a2, m_rw_k_k, m_rw_k_a, m_rw_r_k, m_rw_ln_g, m_rw_ln_b, m_w_out, m_norm_post_g, v_norm_pre_g, v_w_in, v_mla_q_norm_g, v_mla_w_uq, v_mla_kv_norm_g, v_mla_w_ukv, v_rw_mu, v_rw_w0, v_rw_w2, v_rw_a0, v_rw_a2, v_rw_k_k, v_rw_k_a, v_rw_r_k, v_rw_ln_g, v_rw_ln_b, v_w_out, v_norm_post_g):
    given = dict(locals())
    w = {nm: given[nm] for nm in WEIGHTS}
    mom = {nm: given["m_" + nm] for nm in WEIGHTS}
    var = {nm: given["v_" + nm] for nm in WEIGHTS}
    sharded = list(SHARDED)

    gathered = _gather_shards([w[nm][0].astype(BF16) for nm in sharded])
    full = dict(w)
    for nm, blocks in zip(sharded, gathered):
        full[nm] = _unshard(nm, blocks)

    loss_part, grad_x, grads = _local_step(x, positions, loss_target, full)

    small_names = [nm for nm, _ in SMALL]
    row = lambda a: a.reshape(1, -1)
    got = _exchange_grads([_shard_blocks(nm, grads[nm]).astype(BF16) for nm in sharded],
                          [row(grads[nm]) for nm in small_names], loss_part)
    new = {}
    for nm, parts in zip(sharded, got[:-1]):
        new[nm] = _reduce_adamw(nm, parts, w[nm], mom[nm], var[nm], 4 if nm == "w_in" else 1)
    res = _reduce_adamw_small(got[-1], [row(w[nm]) for nm in small_names], [row(mom[nm]) for nm in small_names],
                              [row(var[nm]) for nm in small_names])
    for j, nm in enumerate(small_names):
        new[nm] = tuple(a.reshape(w[nm].shape) for a in res[4 * j:4 * j + 4])
    loss = res[-1][0, 0]
    return (loss, grad_x, *[new[nm][j] for j in range(4) for nm in WEIGHTS])
```

```python
import functools

import jax
import jax.numpy as jnp
from jax import lax
from jax.experimental import pallas as pl
from jax.experimental.pallas import tpu as pltpu

F32 = jnp.float32
BF16 = jnp.bfloat16

D_MODEL = 1024
MLA_HEADS = 4
MLA_NOPE = 128
MLA_ROPE = 64
MLA_V = 128
MLA_WIDTH = MLA_HEADS * MLA_V
Q_LORA = 256
KV_LORA = 128
ROPE_THETA = 10000.0
RW_HEAD = 64
RW_WIDTH = 512
RW_HEADS = RW_WIDTH // RW_HEAD
LORA = 64
RW_COLS = 3 * RW_WIDTH + 2 * LORA
MLA_COLS = Q_LORA + KV_LORA + MLA_ROPE
D_IN = MLA_COLS + RW_COLS + D_MODEL
RW_GN_EPS = 64e-5
NORM_EPS = 1e-6
ATT_SCALE = (MLA_NOPE + MLA_ROPE) ** -0.5
ADAM_LR, ADAM_B1, ADAM_B2, ADAM_EPS, ADAM_WD, ADAM_STEP = 0.001, 0.9, 0.999, 1e-08, 0.01, 10
N_DEV = 8
LANES = 128
MXU = 256

PM_W = 512
WP_COLS = PM_W + RW_COLS + D_MODEL
RW_PIECES = ((0, 512), (512, 1024), (1024, 1536), (1536, 1664))

SHARDED = ("w_in", "mla_w_uq", "mla_w_ukv", "rw_w2", "rw_a2", "w_out")
SMALL = (("norm_pre_g", 1024), ("mla_q_norm_g", 256), ("mla_kv_norm_g", 128), ("rw_mu", 1664), ("rw_w0", 512),
         ("rw_a0", 512), ("rw_k_k", 512), ("rw_k_a", 512), ("rw_r_k", 512), ("rw_ln_g", 512), ("rw_ln_b", 512),
         ("norm_post_g", 1024))
SMALL_N = sum(n for _, n in SMALL)
WEIGHTS = ("norm_pre_g", "w_in", "mla_q_norm_g", "mla_w_uq", "mla_kv_norm_g", "mla_w_ukv", "rw_mu", "rw_w0", "rw_w2",
           "rw_a0", "rw_a2", "rw_k_k", "rw_k_a", "rw_r_k", "rw_ln_g", "rw_ln_b", "w_out", "norm_post_g")


def _seg_ones():
    r = lax.broadcasted_iota(jnp.int32, (MXU, MXU), 0) >> 6
    c = lax.broadcasted_iota(jnp.int32, (MXU, MXU), 1) >> 6
    return jnp.where(r == c, 1.0, 0.0).astype(BF16)


def _seg_dot(x, ones, passes):
    parts, rem = [], x
    for p in range(passes):
        hb = rem.astype(BF16)
        parts.append(hb)
        if p + 1 < passes:
            rem = rem - hb.astype(F32)
    outs = []
    for j in range(x.shape[1] // MXU):
        acc = None
        for hb in parts:
            d = jnp.dot(hb[:, MXU * j:MXU * (j + 1)], ones, preferred_element_type=F32)
            acc = d if acc is None else acc + d
        outs.append(acc)
    return outs[0] if len(outs) == 1 else jnp.concatenate(outs, axis=1)


def _seg_multi(xs, ones, passes):
    his = [x.astype(BF16) for x in xs]
    hi = jnp.concatenate(his, axis=0)
    if passes == 2:
        lo = jnp.concatenate([(x - h.astype(F32)).astype(BF16) for x, h in zip(xs, his)], axis=0)
        rhs = jnp.concatenate([ones, ones], axis=0)
    halves = []
    for j in range(hi.shape[1] // MXU):
        sl = slice(MXU * j, MXU * (j + 1))
        if passes == 2:
            halves.append(jnp.dot(jnp.concatenate([hi[:, sl], lo[:, sl]], axis=1), rhs, preferred_element_type=F32))
        else:
            halves.append(jnp.dot(hi[:, sl], ones, preferred_element_type=F32))
    full = jnp.concatenate(halves, axis=1)
    m = xs[0].shape[0]
    return [full[m * i:m * (i + 1)] for i in range(len(xs))]


@jax.custom_vjp
def _segsum(x):
    return _seg_dot(x, _seg_ones(), 2)


_segsum.defvjp(lambda x: (_segsum(x), None), lambda _, g: (_segsum(g),))


@jax.custom_vjp
def _bdot(a, w):
    return jnp.dot(a.astype(BF16), w.astype(BF16), preferred_element_type=F32)


def _bdot_fwd(a, w):
    return _bdot(a, w), (a, w)


def _bdot_bwd(res, g):
    a, w = res
    gb = g.astype(BF16)
    da = lax.dot_general(gb, w.astype(BF16), (((1,), (1,)), ((), ())), preferred_element_type=F32)
    dw = lax.dot_general(a.astype(BF16), gb, (((0,), (0,)), ((), ())), preferred_element_type=F32)
    return da, dw


_bdot.defvjp(_bdot_fwd, _bdot_bwd)


def _rot_impl(x):
    w = x.shape[1]
    lane = lax.broadcasted_iota(jnp.int32, x.shape, 1)
    return jnp.where((lane & 63) < 32, -pltpu.roll(x, w - 32, 1), pltpu.roll(x, 32, 1))


@jax.custom_vjp
def _rot(x):
    return _rot_impl(x)


_rot.defvjp(lambda x: (_rot_impl(x), None), lambda _, g: (-_rot_impl(g),))


def _rms(x, g):
    return x * lax.rsqrt(jnp.mean(x * x, axis=-1, keepdims=True) + NORM_EPS) * g


def _shift_rows(p, prev_row):
    row = lax.broadcasted_iota(jnp.int32, p.shape, 0)
    return jnp.where(row == 0, prev_row, pltpu.roll(p, 1, 0))


def _unshift_rows(g, next_row):
    row = lax.broadcasted_iota(jnp.int32, g.shape, 0)
    return jnp.where(row == g.shape[0] - 1, next_row, pltpu.roll(g, g.shape[0] - 1, 0))


def _f_mla(cq, ckv, kr, cos, sin, g_q, wqn, wqr, g_kv, wkv):
    qn = _rms(cq, g_q)
    q_nope = _bdot(qn, wqn)
    q_r = _bdot(qn, wqr)
    cos4 = jnp.concatenate([cos] * MLA_HEADS, axis=1)
    sin4 = jnp.concatenate([sin] * MLA_HEADS, axis=1)
    q_rope = q_r * cos4 + _rot(q_r) * sin4
    kv = _bdot(_rms(ckv, g_kv), wkv)
    k_rope = kr * cos + _rot(kr) * sin
    return q_nope, q_rope, kv, k_rope


def _f_rw(pr, pk, pv, pt, sr, sk, sv, st, mu_r, mu_k, mu_v, mu_t, w0, w2p, a0, a2p, k_k, k_a):
    r = pr + (sr - pr) * mu_r
    k = pk + (sk - pk) * mu_k
    v = pv + (sv - pv) * mu_v
    t = pt + (st - pt) * mu_t
    nwl = -(w0 + _bdot(jnp.tanh(t), w2p))
    softplus = jnp.maximum(nwl, 0.0) + jnp.log(1.0 + jnp.exp(-jnp.abs(nwl)))
    decay = jnp.exp(-jnp.exp(-softplus - 0.5))
    a = jax.nn.sigmoid(a0 + _bdot(t, a2p))
    kk = k * k_k
    kk = kk / jnp.maximum(jnp.sqrt(_segsum(kk * kk)), 1e-12)
    k2 = k * (1.0 + (a - 1.0) * k_a)
    return r, decay, k2, v, -kk, kk * a


def _f_head(ys, r, k, v, ym, z1, z2, x, tgt, ln_g, ln_b, r_k, w1, w2, g_post):
    inv = 1.0 / RW_HEAD
    yc = ys - _segsum(ys) * inv
    var = _segsum(yc * yc) * inv
    y = yc * lax.rsqrt(var + RW_GN_EPS) * ln_g + ln_b
    y_rw = y + _segsum(r * k * r_k) * v
    c1 = ym * (z1 * jax.nn.sigmoid(z1))
    c2 = y_rw * (z2 * jax.nn.sigmoid(z2))
    out = _bdot(c1, w1) + _bdot(c2, w2)
    err = x + _rms(out, g_post) - tgt
    per_row = jnp.sum(err * err, axis=1, keepdims=True)
    return jnp.sum(per_row, axis=0, keepdims=True) * (0.5 / D_MODEL)


def _rows(tm, width):
    return pl.BlockSpec((tm, width), lambda i: (i, 0))


def _whole(shape):
    zeros = (0,) * len(shape)
    return pl.BlockSpec(shape, lambda i: zeros)


def _sds(shape, dtype=F32):
    return jax.ShapeDtypeStruct(shape, dtype)


_ARB1 = pltpu.CompilerParams(dimension_semantics=("arbitrary",))


def _acc(ref, val, first):
    @pl.when(first)
    def _():
        ref[...] = val

    @pl.when(jnp.logical_not(first))
    def _():
        ref[...] += val


def _fwd_a(x2, g_pre, wp, tm):
    n = x2.shape[0]

    def body(x_ref, g_ref, w_ref, ut_ref, pm_ref, prw_ref, z_ref):
        u = _rms(x_ref[...], g_ref[...])
        ub = u.astype(BF16)
        ut_ref[...] = u.T.astype(BF16)
        pm_ref[...] = jnp.dot(ub, w_ref[:, 0:PM_W], preferred_element_type=F32)
        prw_ref[...] = jnp.dot(ub, w_ref[:, PM_W:PM_W + RW_COLS], preferred_element_type=F32)
        z_ref[...] = jnp.dot(ub, w_ref[:, PM_W + RW_COLS:WP_COLS], preferred_element_type=F32)

    return pl.pallas_call(
        body, name="fwd_a", grid=(n // tm,),
        in_specs=[_rows(tm, D_MODEL), _whole((1, D_MODEL)), _whole((D_MODEL, WP_COLS))],
        out_specs=[pl.BlockSpec((D_MODEL, tm), lambda i: (0, i)), _rows(tm, PM_W), _rows(tm, RW_COLS), _rows(tm, D_MODEL)],
        out_shape=[_sds((D_MODEL, n), BF16), _sds((n, PM_W)), _sds((n, RW_COLS)), _sds((n, D_MODEL))],
        compiler_params=_ARB1,
    )(x2, g_pre, wp)


def _rope_tables(pos_row, invf_col, tm):
    n = pos_row.shape[1]

    def body(p_ref, f_ref, c_ref, s_ref):
        ang = f_ref[...] * p_ref[...].astype(F32)
        c_ref[...] = jnp.cos(ang).T
        s_ref[...] = jnp.sin(ang).T

    return pl.pallas_call(
        body, name="rope_tables", grid=(n // tm,),
        in_specs=[pl.BlockSpec((1, tm), lambda i: (0, i)), _whole((LANES, 1))],
        out_specs=[_rows(tm, LANES), _rows(tm, LANES)],
        out_shape=[_sds((n, LANES)), _sds((n, LANES))],
        compiler_params=_ARB1,
    )(pos_row, invf_col)


_B_WEIGHT_SHAPES = ((1, Q_LORA), (Q_LORA, 512), (Q_LORA, 512), (1, KV_LORA), (KV_LORA, 1024), (1, RW_COLS), (1, RW_WIDTH),
                    (LANES, RW_WIDTH), (1, RW_WIDTH), (LANES, RW_WIDTH), (1, RW_WIDTH), (1, RW_WIDTH))


def _halo_prev(tm):
    return pl.BlockSpec((8, RW_COLS), lambda i: (jnp.maximum(i * (tm // 8) - 1, 0), 0))


def _b_operands(pm_ref, prw_ref, halo_ref, wrefs, tile, tiles_per_seq):
    g_q, wqn, wqr, g_kv, wkv, mu, w0, w2p, a0, a2p, k_k, k_a = wrefs
    mla_in = (pm_ref[:, 0:Q_LORA], pm_ref[:, Q_LORA:Q_LORA + KV_LORA], pm_ref[:, Q_LORA + KV_LORA:PM_W])
    mla_w = (g_q[...], wqn[...], wqr[...], g_kv[...], wkv[...])
    keep = jnp.where(tile % tiles_per_seq == 0, 0.0, 1.0)
    prev = halo_ref[7:8, :] * keep
    ps = tuple(prw_ref[:, a:b] for a, b in RW_PIECES)
    ss = tuple(_shift_rows(p, prev[:, a:b]) for p, (a, b) in zip(ps, RW_PIECES))
    rw_w = tuple(mu[:, a:b] for a, b in RW_PIECES) + (w0[...], w2p[...], a0[...], a2p[...], k_k[...], k_a[...])
    return mla_in, mla_w, ps + ss, rw_w


def _fwd_b(pm, prw, cos, sin, bw, tm, tiles_per_seq):
    n = pm.shape[0]

    def body(pm_ref, prw_ref, halo_ref, cos_ref, sin_ref, *refs):
        wrefs, outs = refs[:12], refs[12:]
        mla_in, mla_w, rw_in, rw_w = _b_operands(pm_ref, prw_ref, halo_ref, wrefs, pl.program_id(0), tiles_per_seq)
        res = _f_mla(*mla_in, cos_ref[...], sin_ref[...], *mla_w) + _f_rw(*rw_in, *rw_w)
        for o_ref, val in zip(outs, res):
            o_ref[...] = val.astype(o_ref.dtype)

    widths = (512, 512, 1024, LANES) + (RW_WIDTH,) * 6
    return pl.pallas_call(
        body, name="fwd_b", grid=(n // tm,),
        in_specs=[_rows(tm, PM_W), _rows(tm, RW_COLS), _halo_prev(tm), _rows(tm, LANES), _rows(tm, LANES)]
        + [_whole(s) for s in _B_WEIGHT_SHAPES],
        out_specs=[_rows(tm, w) for w in widths],
        out_shape=[_sds((n, w), BF16 if j < 4 else F32) for j, w in enumerate(widths)],
        compiler_params=_ARB1,
    )(pm, prw, prw, cos, sin, *bw)


def _bwd_b(pm, prw, cos, sin, bw, cts, dkr_heads, tm, tiles_per_seq):
    n = pm.shape[0]

    ct_widths = (512, 512, 1024) + (RW_WIDTH,) * 9
    n_ct = len(ct_widths)

    def body(pm_ref, prw_ref, halo_ref, cos_ref, sin_ref, *refs):
        wrefs, ct_refs, dkr_ref = refs[:12], refs[12:12 + n_ct], refs[12 + n_ct]
        dpm_ref, dprw_ref, dps_ref = refs[13 + n_ct:16 + n_ct]
        wg_refs = refs[16 + n_ct:]
        tile = pl.program_id(0)
        first = tile == 0
        mla_in, mla_w, rw_in, rw_w = _b_operands(pm_ref, prw_ref, halo_ref, wrefs, tile, tiles_per_seq)
        cos, sin = cos_ref[...], sin_ref[...]
        ct = [r[...] for r in ct_refs]
        _, vjp_mla = jax.vjp(lambda *a: _f_mla(*a[:3], cos, sin, *a[3:]), *mla_in, *mla_w)
        dkr = dkr_ref[0] + dkr_ref[1] + dkr_ref[2] + dkr_ref[3]
        d_mla = vjp_mla((ct[0], ct[1], ct[2], dkr))
        dpm_ref[:, 0:Q_LORA] = d_mla[0]
        dpm_ref[:, Q_LORA:Q_LORA + KV_LORA] = d_mla[1]
        dpm_ref[:, Q_LORA + KV_LORA:PM_W] = d_mla[2]
        _, vjp_rw = jax.vjp(_f_rw, *rw_in, *rw_w)
        d_rw = vjp_rw((ct[3] + ct[4], ct[5], ct[6] + ct[7], ct[8] + ct[9], ct[10], ct[11]))
        for j, (a, b) in enumerate(RW_PIECES):
            dprw_ref[:, a:b] = d_rw[j]
            dps_ref[:, a:b] = d_rw[4 + j]
        g_q, wqn, wqr, g_kv, wkv, mu, w0, w2p, a0, a2p, k_k, k_a = wg_refs
        for ref, val in zip((g_q, wqn, wqr, g_kv, wkv), d_mla[3:]):
            _acc(ref, val, first)
        for j, (a, b) in enumerate(RW_PIECES):
            _acc(mu.at[:, a:b], d_rw[8 + j], first)
        for ref, val in zip((w0, w2p, a0, a2p, k_k, k_a), d_rw[12:]):
            _acc(ref, val, first)

    return pl.pallas_call(
        body, name="bwd_b", grid=(n // tm,),
        in_specs=[_rows(tm, PM_W), _rows(tm, RW_COLS), _halo_prev(tm), _rows(tm, LANES), _rows(tm, LANES)]
        + [_whole(s) for s in _B_WEIGHT_SHAPES] + [_rows(tm, w) for w in ct_widths]
        + [pl.BlockSpec((MLA_HEADS, tm, LANES), lambda i: (0, i, 0))],
        out_specs=[_rows(tm, PM_W), _rows(tm, RW_COLS), _rows(tm, RW_COLS)] + [_whole(s) for s in _B_WEIGHT_SHAPES],
        out_shape=[_sds((n, PM_W)), _sds((n, RW_COLS)), _sds((n, RW_COLS))] + [_sds(s) for s in _B_WEIGHT_SHAPES],
        compiler_params=_ARB1,
    )(pm, prw, prw, cos, sin, *bw, *cts, dkr_heads)


def _head(ys, r, k, v, ym, z, x2, tgt, hw, tm):
    n = x2.shape[0]
    h_shapes = ((1, RW_WIDTH), (1, RW_WIDTH), (1, RW_WIDTH), (D_MODEL, D_MODEL), (1, D_MODEL))

    def body(ys_ref, r_ref, k_ref, v_ref, ym_ref, z_ref, x_ref, t_ref, lng, lnb, rk, wout, gpost,
             dys_ref, dr_ref, dk_ref, dv_ref, dym_ref, dz_ref, dx_ref, loss_ref, dlng, dlnb, drk, dwout, dgpost):
        first = pl.program_id(0) == 0
        tgt_v = t_ref[...]
        args = (ys_ref[...], r_ref[...], k_ref[...], v_ref[...], ym_ref[...], z_ref[:, 0:MLA_WIDTH], z_ref[:, MLA_WIDTH:D_MODEL],
                x_ref[...], lng[...], lnb[...], rk[...], wout[0:MLA_WIDTH, :], wout[MLA_WIDTH:D_MODEL, :], gpost[...])
        loss, vjp = jax.vjp(lambda *a: _f_head(*a[:8], tgt_v, *a[8:]), *args)
        d = vjp(jnp.ones((1, 1), F32))
        dys_ref[...] = d[0]
        dr_ref[...] = d[1]
        dk_ref[...] = d[2]
        dv_ref[...] = d[3]
        dym_ref[...] = d[4].astype(BF16)
        dz_ref[:, 0:MLA_WIDTH] = d[5]
        dz_ref[:, MLA_WIDTH:D_MODEL] = d[6]
        dx_ref[...] = d[7]
        _acc(loss_ref, jnp.broadcast_to(loss, (8, LANES)), first)
        _acc(dlng, d[8], first)
        _acc(dlnb, d[9], first)
        _acc(drk, d[10], first)
        _acc(dwout.at[0:MLA_WIDTH, :], d[11], first)
        _acc(dwout.at[MLA_WIDTH:D_MODEL, :], d[12], first)
        _acc(dgpost, d[13], first)

    widths = (RW_WIDTH,) * 4 + (MLA_WIDTH, D_MODEL, D_MODEL)
    return pl.pallas_call(
        body, name="head", grid=(n // tm,),
        in_specs=[_rows(tm, RW_WIDTH)] * 4 + [_rows(tm, MLA_WIDTH), _rows(tm, D_MODEL), _rows(tm, D_MODEL), _rows(tm, D_MODEL)]
        + [_whole(s) for s in h_shapes],
        out_specs=[_rows(tm, w) for w in widths] + [_whole((8, LANES))] + [_whole(s) for s in h_shapes],
        out_shape=[_sds((n, w), BF16 if j == 4 else F32) for j, w in enumerate(widths)] + [_sds((8, LANES))]
        + [_sds(s) for s in h_shapes],
        compiler_params=_ARB1,
    )(ys, r, k, v, ym, z, x2, tgt, *hw)


def _halo_next(tm, n):
    last = n // 8 - 1
    return pl.BlockSpec((8, RW_COLS), lambda i: (jnp.minimum((i + 1) * (tm // 8), last), 0))


def _bwd_a(x2, g_pre, wp, dpm, dprw, dps, dz, dxres, tm, tiles_per_seq):
    n = x2.shape[0]
    nt_dims = (((1,), (1,)), ((), ()))

    def body(x_ref, g_ref, w_ref, dpm_ref, dprw_ref, dps_ref, nxt_ref, dz_ref, dxres_ref, gx_ref, dpb_ref, dg_ref):
        tile = pl.program_id(0)
        keep = jnp.where((tile + 1) % tiles_per_seq == 0, 0.0, 1.0)
        dprw_v = dprw_ref[...] + _unshift_rows(dps_ref[...], nxt_ref[0:1, :] * keep)
        dpm_b, dprw_b, dz_b = dpm_ref[...].astype(BF16), dprw_v.astype(BF16), dz_ref[...].astype(BF16)
        dpb_ref[:, 0:PM_W] = dpm_b
        dpb_ref[:, PM_W:PM_W + RW_COLS] = dprw_b
        dpb_ref[:, PM_W + RW_COLS:WP_COLS] = dz_b
        du = (lax.dot_general(dpm_b, w_ref[:, 0:PM_W], nt_dims, preferred_element_type=F32)
              + lax.dot_general(dprw_b, w_ref[:, PM_W:PM_W + RW_COLS], nt_dims, preferred_element_type=F32)
              + lax.dot_general(dz_b, w_ref[:, PM_W + RW_COLS:WP_COLS], nt_dims, preferred_element_type=F32))
        x = x_ref[...]
        xhat = x * lax.rsqrt(jnp.mean(x * x, axis=-1, keepdims=True) + NORM_EPS)
        dxn = du * g_ref[...]
        dx = (dxn - xhat * jnp.mean(dxn * xhat, axis=-1, keepdims=True)) * lax.rsqrt(jnp.mean(x * x, axis=-1, keepdims=True) + NORM_EPS)
        gx_ref[...] = dx + dxres_ref[...]
        _acc(dg_ref, jnp.sum(du * xhat, axis=0, keepdims=True), tile == 0)

    return pl.pallas_call(
        body, name="bwd_a", grid=(n // tm,),
        in_specs=[_rows(tm, D_MODEL), _whole((1, D_MODEL)), _whole((D_MODEL, WP_COLS)), _rows(tm, PM_W), _rows(tm, RW_COLS),
                  _rows(tm, RW_COLS), _halo_next(tm, n), _rows(tm, D_MODEL), _rows(tm, D_MODEL)],
        out_specs=[_rows(tm, D_MODEL), _rows(tm, WP_COLS), _whole((1, D_MODEL))],
        out_shape=[_sds((n, D_MODEL)), _sds((n, WP_COLS), BF16), _sds((1, D_MODEL))],
        compiler_params=_ARB1,
    )(x2, g_pre, wp, dpm, dprw, dps, dps, dz, dxres)


def _dw_in(ut, dpb, tk, tn):
    n = ut.shape[1]
    steps = n // tk

    def body(u_ref, d_ref, o_ref, acc_sc):
        k = pl.program_id(1)
        _acc(acc_sc, jnp.dot(u_ref[...], d_ref[...], preferred_element_type=F32), k == 0)

        @pl.when(k == steps - 1)
        def _():
            o_ref[...] = acc_sc[...].astype(BF16)

    return pl.pallas_call(
        body, name="dw_in", grid=(WP_COLS // tn, steps),
        in_specs=[pl.BlockSpec((D_MODEL, tk), lambda j, k: (0, k)), pl.BlockSpec((tk, tn), lambda j, k: (k, j))],
        out_specs=pl.BlockSpec((D_MODEL, tn), lambda j, k: (0, j)),
        out_shape=_sds((D_MODEL, WP_COLS), BF16),
        scratch_shapes=[pltpu.VMEM((D_MODEL, tn), F32)],
        compiler_params=pltpu.CompilerParams(dimension_semantics=("arbitrary", "arbitrary")),
    )(ut, dpb)


ATT_BLK = 256
_NT = (((1,), (1,)), ((), ()))
_TN = (((0,), (0,)), ((), ()))


def _causal(q0, k0, blk, blk_k=None):
    blk_k = blk if blk_k is None else blk_k
    row = q0 + lax.broadcasted_iota(jnp.int32, (blk, blk_k), 0)
    col = k0 + lax.broadcasted_iota(jnp.int32, (blk, blk_k), 1)
    return row >= col


def _attn_fwd(qn, qr, kv, kr):
    bsz, t, _ = qn.shape
    blk = min(ATT_BLK, t)

    heads = range(MLA_HEADS)

    def body(qn_ref, qr_ref, kv_ref, kr_ref, o_ref, lse_ref):
        qi = pl.program_id(1)
        q = [jnp.concatenate([qn_ref[:, LANES * h:LANES * (h + 1)], qr_ref[:, LANES * h:LANES * (h + 1)]], axis=1) for h in heads]
        lower = _causal(0, 0, blk)

        def kv_step(j, carry, diagonal):
            ks = pl.multiple_of(j * blk, blk)
            k_rope = kr_ref[pl.ds(ks, blk), :]
            def score(h):
                k = jnp.concatenate([kv_ref[pl.ds(ks, blk), 2 * LANES * h:2 * LANES * h + LANES], k_rope], axis=1)
                return lax.dot_general(q[h], k, _NT, preferred_element_type=F32)

            out = []
            nxt = score(0)
            for h in heads:
                s = nxt * ATT_SCALE
                if h + 1 < MLA_HEADS:
                    nxt = score(h + 1)
                m, l, acc = carry[h]
                if diagonal:
                    s = jnp.where(lower, s, -1e30)
                m_new = jnp.maximum(m, jnp.max(s, axis=1, keepdims=True))
                alpha = jnp.exp(m - m_new)
                p = jnp.exp(s - m_new)
                l = alpha * l + jnp.sum(p, axis=1, keepdims=True)
                v = kv_ref[pl.ds(ks, blk), 2 * LANES * h + LANES:2 * LANES * (h + 1)]
                out.append((m_new, l, alpha * acc + jnp.dot(p.astype(BF16), v, preferred_element_type=F32)))
            return tuple(out)

        one = (jnp.full((blk, 1), -1e30, F32), jnp.zeros((blk, 1), F32), jnp.zeros((blk, MLA_V), F32))
        carry = lax.fori_loop(0, qi // 2, lambda pr, c: kv_step(2 * pr + 1, kv_step(2 * pr, c, False), False),
                              (one,) * MLA_HEADS)
        carry = lax.cond(qi % 2 == 1, lambda c: kv_step(qi - 1, c, False), lambda c: c, carry)
        carry = kv_step(qi, carry, True)
        for h in heads:
            m, l, acc = carry[h]
            o_ref[:, LANES * h:LANES * (h + 1)] = acc / l
            lse_ref[h] = jnp.broadcast_to(m + jnp.log(l), (blk, LANES))

    return pl.pallas_call(
        body, name="attn_fwd", grid=(bsz, t // blk),
        in_specs=[pl.BlockSpec((None, blk, MLA_WIDTH), lambda b, i: (b, i, 0)),
                  pl.BlockSpec((None, blk, MLA_WIDTH), lambda b, i: (b, i, 0)),
                  pl.BlockSpec((None, t, 2 * MLA_WIDTH), lambda b, i: (b, 0, 0)),
                  pl.BlockSpec((None, t, LANES), lambda b, i: (b, 0, 0))],
        out_specs=[pl.BlockSpec((None, blk, MLA_WIDTH), lambda b, i: (b, i, 0)),
                   pl.BlockSpec((None, MLA_HEADS, blk, LANES), lambda b, i: (b, 0, i, 0))],
        out_shape=[_sds((bsz, t, MLA_WIDTH)), _sds((bsz, MLA_HEADS, t, LANES))],
        compiler_params=pltpu.CompilerParams(dimension_semantics=("arbitrary", "arbitrary")),
    )(qn, qr, kv, kr)


def _attn_bwd(qn, qr, kv, kr, o, do, lse):
    bsz, t, _ = qn.shape
    blk = min(ATT_BLK, t)
    nb = t // blk
    assert nb % 2 == 0, "query blocks are taken in pairs"

    def body(qn_ref, qr_ref, kn_ref, kr_ref, v_ref, o_ref, do_ref, lse_ref, dqn_ref, dqr_ref, dkv_ref, dkr_ref, dq_sc, delta_sc):
        dq_sc[...] = jnp.zeros_like(dq_sc)
        delta_sc[...] = jnp.sum(do_ref[...].astype(F32) * o_ref[...], axis=1, keepdims=True)

        lower = _causal(0, 0, blk)

        def q_blocks(j, k, vb, carry, blocks, diagonal):
            dk, dv = carry
            us = range(len(blocks))
            qs = [i * blk if isinstance(i, int) else pl.multiple_of(i * blk, blk) for i in blocks]
            q = [jnp.concatenate([qn_ref[pl.ds(qs[u], blk), :], qr_ref[pl.ds(qs[u], blk), :]], axis=1) for u in us]
            dob = [do_ref[pl.ds(qs[u], blk), :] for u in us]
            s = [lax.dot_general(q[u], k, _NT, preferred_element_type=F32) for u in us]
            dp = [lax.dot_general(dob[u], vb, _NT, preferred_element_type=F32) for u in us]
            pb, ds = [], []
            for u in us:
                p = jnp.exp(s[u] * ATT_SCALE - lse_ref[pl.ds(qs[u], blk), 0:1])
                if diagonal == u:
                    p = jnp.where(lower, p, 0.0)
                pb.append(p.astype(BF16))
                ds.append((p * (dp[u] - delta_sc[pl.ds(qs[u], blk), :]) * ATT_SCALE).astype(BF16))
            for u in us:
                dv = dv + lax.dot_general(pb[u], dob[u], _TN, preferred_element_type=F32)
            for u in us:
                dq_sc[pl.ds(qs[u], blk), :] += jnp.dot(ds[u], k, preferred_element_type=F32)
                dk = dk + lax.dot_general(ds[u], q[u], _TN, preferred_element_type=F32)
            return dk, dv

        for j in range(nb):
            ks = j * blk
            k = jnp.concatenate([kn_ref[ks:ks + blk, :], kr_ref[ks:ks + blk, :]], axis=1)
            vb = v_ref[ks:ks + blk, :]
            carry = (jnp.zeros((blk, 2 * LANES), F32), jnp.zeros((blk, MLA_V), F32))
            if j % 2 == 0:
                carry = q_blocks(j, k, vb, carry, [j, j + 1], 0)
            else:
                carry = q_blocks(j, k, vb, carry, [j], 0)
            pairs_from = j // 2 + 1
            if nb // 2 - pairs_from > 0:
                carry = lax.fori_loop(pairs_from, nb // 2,
                                      lambda pr, c, j=j, k=k, vb=vb: q_blocks(j, k, vb, c, [2 * pr, 2 * pr + 1], None),
                                      carry, unroll=2)
            dk, dv = carry
            dkv_ref[ks:ks + blk, 0:LANES] = dk[:, 0:LANES]
            dkv_ref[ks:ks + blk, LANES:2 * LANES] = dv
            dkr_ref[ks:ks + blk, :] = dk[:, LANES:2 * LANES]
        dqn_ref[...] = dq_sc[:, 0:LANES]
        dqr_ref[...] = dq_sc[:, LANES:2 * LANES]

    head_col = lambda b, h: (b, 0, h)
    return pl.pallas_call(
        body, name="attn_bwd", grid=(bsz, MLA_HEADS),
        in_specs=[pl.BlockSpec((None, t, LANES), head_col), pl.BlockSpec((None, t, LANES), head_col),
                  pl.BlockSpec((None, t, LANES), lambda b, h: (b, 0, 2 * h)),
                  pl.BlockSpec((None, t, LANES), lambda b, h: (b, 0, 0)),
                  pl.BlockSpec((None, t, LANES), lambda b, h: (b, 0, 2 * h + 1)),
                  pl.BlockSpec((None, t, LANES), head_col), pl.BlockSpec((None, t, LANES), head_col),
                  pl.BlockSpec((None, None, t, LANES), lambda b, h: (b, h, 0, 0))],
        out_specs=[pl.BlockSpec((None, t, LANES), head_col), pl.BlockSpec((None, t, LANES), head_col),
                   pl.BlockSpec((None, t, 2 * LANES), head_col),
                   pl.BlockSpec((None, None, t, LANES), lambda b, h: (h, b, 0, 0))],
        out_shape=[_sds((bsz, t, MLA_WIDTH)), _sds((bsz, t, MLA_WIDTH)), _sds((bsz, t, 2 * MLA_WIDTH)),
                   _sds((MLA_HEADS, bsz, t, LANES))],
        scratch_shapes=[pltpu.VMEM((t, 2 * LANES), F32), pltpu.VMEM((t, 1), F32)],
        compiler_params=pltpu.CompilerParams(dimension_semantics=("arbitrary", "arbitrary")),
    )(qn, qr, kv, kr, kv, o, do, lse)


SCAN_CHUNK = 16


def _diag_mask():
    row = lax.broadcasted_iota(jnp.int32, (RW_HEAD, RW_WIDTH), 0)
    lane = lax.broadcasted_iota(jnp.int32, (RW_HEAD, RW_WIDTH), 1)
    return jnp.where(row == (lane & (RW_HEAD - 1)), 1.0, 0.0)


def _time_minor(a):
    bsz, t, _ = a.shape
    a = a.reshape(bsz, t // SCAN_CHUNK, SCAN_CHUNK, RW_HEADS, RW_HEAD)
    return a.transpose(0, 1, 4, 3, 2).reshape(bsz, t // SCAN_CHUNK, RW_HEAD, RW_HEADS * SCAN_CHUNK)


def _head_expand():
    l = lax.broadcasted_iota(jnp.int32, (2 * LANES, RW_WIDTH), 0)
    n = lax.broadcasted_iota(jnp.int32, (2 * LANES, RW_WIDTH), 1)
    return jnp.where(((l & (LANES - 1)) >> 4) == (n >> 6), 1.0, 0.0).astype(BF16)


BCAST_GROUP = 4


def _outer_chunk(tm_ref, row_ref, out_sc, expand, seqs):
    step_of_lane = lax.broadcasted_iota(jnp.int32, (RW_HEAD, LANES), 1) & (SCAN_CHUNK - 1)
    tiles = [tm_ref[bi, 0] for bi in seqs]
    for t0 in range(0, SCAN_CHUNK, BCAST_GROUP):
        parts = []
        for t in range(t0, t0 + BCAST_GROUP):
            for tile in tiles:
                a = jnp.where(step_of_lane == t, tile, 0.0)
                hi = a.astype(BF16)
                parts.append(jnp.concatenate([hi, (a - hi.astype(F32)).astype(BF16)], axis=1))
        cols = jnp.dot(jnp.concatenate(parts, axis=0), expand, preferred_element_type=F32)
        for j, t in enumerate(range(t0, t0 + BCAST_GROUP)):
            base = j * RW_HEAD * len(seqs)
            out_sc[t] = jnp.concatenate([cols[base + RW_HEAD * bi:base + RW_HEAD * (bi + 1)] * row_ref[bi, t:t + 1, :]
                                         for bi in seqs], axis=0)


def _fold8(x):
    acc = x[0:8]
    for j in range(1, x.shape[0] // 8):
        acc = acc + x[8 * j:8 * (j + 1)]
    return acc


def _rows8(at):
    return pl.ds(at * 8 if isinstance(at, int) else pl.multiple_of(at * 8, 8), 8)


def _put8(sc, bi, at, val):
    for j in range(RW_WIDTH // LANES):
        sc[bi * (RW_WIDTH // LANES) + j, _rows8(at), :] = val[:, LANES * j:LANES * (j + 1)]


def _unfold8(sc, bi, steps):
    tiles = []
    for j in range(RW_WIDTH // LANES):
        view = sc.at[bi * (RW_WIDTH // LANES) + j]
        acc = view[pl.ds(0, steps, stride=8), :]
        for s in range(1, 8):
            acc = acc + view[pl.ds(s, steps, stride=8), :]
        tiles.append(acc)
    return jnp.concatenate(tiles, axis=1)


def _scan_fwd(r, w, k, vt, nkk, b):
    bsz, t, _ = r.shape
    tc = SCAN_CHUNK

    def body(r_ref, w_ref, k_ref, n_ref, b_ref, vt_ref, y_ref, st_ref, s_sc, vc_sc, y_sc):
        @pl.when(pl.program_id(0) == 0)
        def _():
            s_sc[...] = jnp.zeros_like(s_sc)

        ones = _seg_ones()
        diag = _diag_mask()
        seqs = range(bsz)
        _outer_chunk(vt_ref, k_ref, vc_sc, _head_expand(), seqs)

        def put_y(ya, at):
            for bi in seqs:
                _put8(y_sc, bi, at, _fold8(ya[bi] * diag))

        def step(i, _):
            row = lambda ref, bi: ref[bi, pl.ds(i, 1), :]
            prev = jnp.maximum(i - 1, 0)
            s_old = [s_sc[bi] for bi in seqs]
            s_b = [s_old[bi].astype(BF16) for bi in seqs]
            sa = _seg_multi([s_b[bi] * row(n_ref, bi).astype(BF16) for bi in seqs], ones, 1)
            put_y(_seg_multi([s_b[bi] * r_ref[bi, pl.ds(prev, 1), :].astype(BF16) for bi in seqs], ones, 1), prev)
            vk = vc_sc[i]
            for bi in seqs:
                s_new = s_old[bi] * row(w_ref, bi) + sa[bi] * row(b_ref, bi) + vk[RW_HEAD * bi:RW_HEAD * (bi + 1)]
                s_sc[bi] = s_new
                st_ref[bi, i] = s_new
            return 0

        lax.fori_loop(0, tc, step, 0, unroll=8)
        put_y(_seg_multi([s_sc[bi].astype(BF16) * r_ref[bi, tc - 1:tc, :].astype(BF16) for bi in seqs], ones, 1), tc - 1)
        for bi in seqs:
            y_ref[bi] = _unfold8(y_sc, bi, tc)

    vec = pl.BlockSpec((bsz, tc, RW_WIDTH), lambda c: (0, c, 0))
    return pl.pallas_call(
        body, name="scan_fwd", grid=(t // tc,),
        in_specs=[vec] * 5 + [pl.BlockSpec((bsz, 1, RW_HEAD, LANES), lambda c: (0, c, 0, 0))],
        out_specs=[vec, pl.BlockSpec((bsz, tc, RW_HEAD, RW_WIDTH), lambda c: (0, c, 0, 0))],
        out_shape=[_sds((bsz, t, RW_WIDTH)), _sds((bsz, t, RW_HEAD, RW_WIDTH))],
        scratch_shapes=[pltpu.VMEM((bsz, RW_HEAD, RW_WIDTH), F32), pltpu.VMEM((tc, bsz * RW_HEAD, RW_WIDTH), F32),
                        pltpu.VMEM((bsz * RW_WIDTH // LANES, tc * 8, LANES), F32)],
        compiler_params=_ARB1,
    )(r, w, k, nkk, b, vt)


def _own_head_row(x):
    first_half = lax.broadcasted_iota(jnp.int32, (1, LANES), 1) < RW_HEAD
    tiles = [jnp.where(first_half, x[2 * j:2 * j + 1, LANES * j:LANES * (j + 1)], x[2 * j + 1:2 * j + 2, LANES * j:LANES * (j + 1)])
             for j in range(RW_WIDTH // LANES)]
    return jnp.concatenate(tiles, axis=1)


def _scan_bwd(r, w, k, vt, nkk, b, st, dyt):
    bsz, t, _ = r.shape
    tc = SCAN_CHUNK
    nc = t // tc

    def body(r_ref, w_ref, k_ref, n_ref, b_ref, vt_ref, dyt_ref, st_ref, halo_ref,
             dr_ref, dw_ref, dk_ref, dv_ref, dn_ref, db_ref, g_sc, dc_sc, v8_sc, dy8_sc, *part_scs):
        c = pl.program_id(0)

        @pl.when(c == 0)
        def _():
            g_sc[...] = jnp.zeros_like(g_sc)

        ones = _seg_ones()
        diag = _diag_mask()
        has_prev = jnp.where(c == nc - 1, 0.0, 1.0)
        seqs = range(bsz)
        _outer_chunk(dyt_ref, r_ref, dc_sc, _head_expand(), seqs)
        for bi in seqs:
            v8_sc[bi] = jnp.concatenate([vt_ref[bi, 0].T] * 2, axis=1)
            dy8_sc[bi] = jnp.concatenate([dyt_ref[bi, 0].T] * 2, axis=1)
        by_head = lambda sc, bi, i: sc.at[bi][pl.ds(i, RW_HEADS, stride=SCAN_CHUNK), :][:, 0:RW_HEAD].astype(BF16)
        dw_sc, dv_sc, dn_sc, db_sc = part_scs

        def step(i, s_p):
            static = isinstance(i, int)
            row = lambda ref, bi: ref[bi, i:i + 1, :] if static else ref[bi, pl.ds(i, 1), :]
            put_row = lambda ref, bi, val: ref.__setitem__((bi, slice(i, i + 1) if static else pl.ds(i, 1), slice(None)), val)
            dr8 = [jnp.dot(by_head(dy8_sc, bi, i), st_ref[bi, i].astype(BF16), preferred_element_type=F32) for bi in seqs]
            rowb = lambda ref, bi: row(ref, bi).astype(BF16)
            sa = _seg_multi([s_p[bi].astype(BF16) * rowb(n_ref, bi) for bi in seqs], ones, 1)
            dc_all = dc_sc[i]
            dc = [dc_all[RW_HEAD * bi:RW_HEAD * (bi + 1)] for bi in seqs]
            g = [g_sc[bi] + dc[bi] for bi in seqs]
            g_b = [g[bi].astype(BF16) for bi in seqs]
            res = _seg_multi([g_b[bi] * rowb(b_ref, bi) for bi in seqs] + [g_b[bi] * rowb(k_ref, bi) for bi in seqs], ones, 1)
            dsa, dvb = res[:bsz], res[bsz:]
            for bi in seqs:
                dk8 = jnp.dot(by_head(v8_sc, bi, i), g_b[bi], preferred_element_type=F32)
                put_row(dr_ref, bi, _own_head_row(dr8[bi]))
                put_row(dk_ref, bi, _own_head_row(dk8))
                _put8(dv_sc, bi, i, _fold8(dvb[bi] * diag))
                _put8(dw_sc, bi, i, _fold8(g[bi] * s_p[bi]))
                _put8(db_sc, bi, i, _fold8(g[bi] * sa[bi]))
                _put8(dn_sc, bi, i, _fold8(s_p[bi] * dsa[bi]))
                g_sc[bi] = g[bi] * row(w_ref, bi) + dsa[bi] * row(n_ref, bi)

        def loop_step(ii, _):
            i = tc - 1 - ii
            step(i, [st_ref[bi, i - 1] for bi in seqs])
            return 0

        lax.fori_loop(0, tc - 1, loop_step, 0, unroll=5)
        step(0, [halo_ref[bi, 0] * has_prev for bi in seqs])
        for out_ref, sc in zip((dw_ref, dv_ref, dn_ref, db_ref), part_scs):
            for bi in seqs:
                out_ref[bi] = _unfold8(sc, bi, tc)

    vec = pl.BlockSpec((bsz, tc, RW_WIDTH), lambda c: (0, nc - 1 - c, 0))
    tmin = pl.BlockSpec((bsz, 1, RW_HEAD, LANES), lambda c: (0, nc - 1 - c, 0, 0))
    parts = pltpu.VMEM((bsz * RW_WIDTH // LANES, tc * 8, LANES), F32)
    heads_steps = pltpu.VMEM((bsz, LANES, LANES), F32)
    return pl.pallas_call(
        body, name="scan_bwd", grid=(nc,),
        in_specs=[vec] * 5 + [tmin, tmin,
                              pl.BlockSpec((bsz, tc, RW_HEAD, RW_WIDTH), lambda c: (0, nc - 1 - c, 0, 0)),
                              pl.BlockSpec((bsz, 1, RW_HEAD, RW_WIDTH), lambda c: (0, jnp.maximum((nc - 1 - c) * tc - 1, 0), 0, 0))],
        out_specs=[vec] * 6,
        out_shape=[_sds((bsz, t, RW_WIDTH))] * 6,
        scratch_shapes=[pltpu.VMEM((bsz, RW_HEAD, RW_WIDTH), F32), pltpu.VMEM((tc, bsz * RW_HEAD, RW_WIDTH), F32),
                        heads_steps, heads_steps] + [parts] * 4,
        compiler_params=_ARB1,
    )(r, w, k, nkk, b, vt, dyt, st, st)


TOKEN_TILE = 256


def _padded_weights(wt):
    f = lambda a: a.astype(F32)
    w_in = wt["w_in"][0].astype(BF16)
    zeros = lambda r, c: jnp.zeros((r, c), F32)
    wp = jnp.concatenate([w_in[:, :MLA_COLS], jnp.zeros((D_MODEL, PM_W - MLA_COLS), BF16), w_in[:, MLA_COLS:]], axis=1)
    w_uq = f(wt["mla_w_uq"][0]).reshape(Q_LORA, MLA_HEADS, MLA_NOPE + MLA_ROPE)
    wqn = w_uq[:, :, :MLA_NOPE].reshape(Q_LORA, MLA_HEADS * MLA_NOPE)
    wqr = jnp.concatenate([w_uq[:, :, MLA_NOPE:], jnp.zeros((Q_LORA, MLA_HEADS, LANES - MLA_ROPE), F32)], axis=2)
    wqr = wqr.reshape(Q_LORA, MLA_HEADS * LANES)
    w2p = jnp.concatenate([f(wt["rw_w2"][0]), zeros(LORA, RW_WIDTH)], axis=0)
    a2p = jnp.concatenate([zeros(LORA, RW_WIDTH), f(wt["rw_a2"][0])], axis=0)
    bw = (f(wt["mla_q_norm_g"]), wqn, wqr, f(wt["mla_kv_norm_g"]), f(wt["mla_w_ukv"][0]), f(wt["rw_mu"]), f(wt["rw_w0"]),
          w2p, f(wt["rw_a0"]), a2p, f(wt["rw_k_k"]), f(wt["rw_k_a"]))
    hw = (f(wt["rw_ln_g"]), f(wt["rw_ln_b"]), f(wt["rw_r_k"]).reshape(1, RW_WIDTH), f(wt["w_out"][0]), f(wt["norm_post_g"]))
    return wp, bw, hw


def _local_step(x, positions, target, wt):
    bsz, t, _ = x.shape
    n = bsz * t
    tm = min(TOKEN_TILE, t)
    tps = t // tm
    wp, bw, hw = _padded_weights(wt)
    wpb = wp.astype(BF16)
    g_pre = wt["norm_pre_g"].astype(F32)
    x2 = x.reshape(n, D_MODEL)
    tgt2 = target.reshape(n, D_MODEL)
    inv_freq = ROPE_THETA ** (-jnp.arange(0, MLA_ROPE, 2, dtype=F32) / MLA_ROPE)
    invf = jnp.tile(inv_freq, LANES // (MLA_ROPE // 2)).reshape(LANES, 1)
    cos, sin = _rope_tables(positions.reshape(1, n), invf, tm)

    u, pm, prw, z = _fwd_a(x2, g_pre, wpb, tm)
    qn, qr, kv, kr, r, w, k, v, nkk, b = _fwd_b(pm, prw, cos, sin, bw, tm, tps)
    b3 = lambda a: a.reshape(bsz, t, a.shape[-1])
    ym, lse = _attn_fwd(b3(qn), b3(qr), b3(kv), b3(kr))
    vt = _time_minor(b3(v))
    ys, st = _scan_fwd(b3(r), b3(w), b3(k), vt, b3(nkk), b3(b))
    (dys, dr_h, dk_h, dv_h, dym, dz, dxres, loss, d_lng, d_lnb, d_rk, d_wout, d_gpost) = _head(
        ys.reshape(n, RW_WIDTH), r, k, v, ym.reshape(n, MLA_WIDTH), z, x2, tgt2, hw, tm)
    dqn, dqr, dkv, dkr_heads = _attn_bwd(b3(qn), b3(qr), b3(kv), b3(kr), ym, b3(dym), lse)
    dr_s, dw_s, dk_s, dv_s, dn_s, db_s = _scan_bwd(b3(r), b3(w), b3(k), vt, b3(nkk), b3(b), st, _time_minor(b3(dys)))
    f2 = lambda a: a.reshape(n, a.shape[-1])
    cts = (f2(dqn), f2(dqr), f2(dkv), f2(dr_s), dr_h, f2(dw_s), f2(dk_s), dk_h, f2(dv_s), dv_h, f2(dn_s), f2(db_s))
    (dpm, dprw, dps, d_gq, d_wqn, d_wqr, d_gkv, d_wkv, d_mu, d_w0, d_w2p, d_a0, d_a2p, d_kk, d_ka) = _bwd_b(
        pm, prw, cos, sin, bw, cts, dkr_heads.reshape(MLA_HEADS, n, LANES), tm, tps)
    grad_x, dpb, d_gpre = _bwd_a(x2, g_pre, wpb, dpm, dprw, dps, dz, dxres, tm, tps)
    d_wp = _dw_in(u, dpb, min(1024, n), 640)

    d_w_in = jnp.concatenate([d_wp[:, :MLA_COLS], d_wp[:, PM_W:]], axis=1)
    d_w_uq = jnp.concatenate([d_wqn.reshape(Q_LORA, MLA_HEADS, MLA_NOPE),
                              d_wqr.reshape(Q_LORA, MLA_HEADS, LANES)[:, :, :MLA_ROPE]], axis=2)
    grads = {
        "norm_pre_g": d_gpre, "w_in": d_w_in[None], "mla_q_norm_g": d_gq,
        "mla_w_uq": d_w_uq.reshape(1, Q_LORA, MLA_HEADS * (MLA_NOPE + MLA_ROPE)), "mla_kv_norm_g": d_gkv,
        "mla_w_ukv": d_wkv[None], "rw_mu": d_mu, "rw_w0": d_w0, "rw_w2": d_w2p[None, :LORA], "rw_a0": d_a0,
        "rw_a2": d_a2p[None, LORA:], "rw_k_k": d_kk, "rw_k_a": d_ka, "rw_r_k": d_rk.reshape(1, RW_HEADS, RW_HEAD),
        "rw_ln_g": d_lng, "rw_ln_b": d_lnb, "w_out": d_wout[None], "norm_post_g": d_gpost,
    }
    return loss, grad_x.reshape(bsz, t, D_MODEL), grads


_MESH = pl.DeviceIdType.MESH


def _gather_shards(shards):
    na = len(shards)

    def body(*refs):
        x_refs, out_refs = refs[:na], refs[na:2 * na]
        send_sems, recv_sems, local_sems = refs[2 * na:]
        x, y, c = lax.axis_index("x"), lax.axis_index("y"), lax.axis_index("c")
        me, sibling = (x, y, c), (x, y, 1 - c)
        chips = [(1 - x, y), (x, 1 - y), (1 - x, 1 - y)]
        arrays = range(na)

        def slot(a, px, py, pc):
            return out_refs[a].at[4 * px + 2 * py + pc]

        def copy(k, a, block, to, src=None):
            return pltpu.make_async_remote_copy(
                src_ref=slot(a, *block) if src is None else src, dst_ref=slot(a, *block),
                send_sem=send_sems.at[k, a], recv_sem=recv_sems.at[k, a], device_id=to, device_id_type=_MESH)

        mine = [pltpu.make_async_copy(x_refs[a], slot(a, *me), local_sems.at[a]) for a in arrays]
        for cp in mine:
            cp.start()
        first = [copy(0, a, me, sibling, src=x_refs[a]) for a in arrays]
        first += [copy(1 + j, a, me, (*chip, c), src=x_refs[a]) for j, chip in enumerate(chips) for a in arrays]
        for cp in first:
            cp.start()
        passed = []
        for j, chip in enumerate(chips):
            for a in arrays:
                copy(1 + j, a, (*chip, c), me).wait_recv()
                passed.append(copy(4 + j, a, (*chip, c), sibling))
                passed[-1].start()
        for a in arrays:
            copy(0, a, sibling, me).wait_recv()
        for j, chip in enumerate(chips):
            for a in arrays:
                copy(4 + j, a, (*chip, 1 - c), me).wait_recv()
        for cp in first + passed:
            cp.wait_send()
        for cp in mine:
            cp.wait()

    vmem = pl.BlockSpec(memory_space=pltpu.VMEM)
    return pl.pallas_call(
        body, name="gather_shards",
        out_shape=[_sds((N_DEV,) + a.shape, a.dtype) for a in shards],
        in_specs=[vmem] * na, out_specs=[vmem] * na,
        scratch_shapes=[pltpu.SemaphoreType.DMA((7, na)), pltpu.SemaphoreType.DMA((7, na)), pltpu.SemaphoreType.DMA((na,))],
    )(*shards)


SMALL_LANES = SMALL_N + LANES


def _exchange_grads(big_blocks, small_grads, loss_tile):
    nb = len(big_blocks)
    ns = len(small_grads)

    def body(*refs):
        big, small, loss_ref = refs[:nb], refs[nb:nb + ns], refs[nb + ns]
        rbig, rsmall = refs[nb + ns + 1:2 * nb + ns + 1], refs[2 * nb + ns + 1]
        send_b, recv_b, send_s, recv_s, local_sems, row_sc = refs[2 * nb + ns + 2:]
        x, y, c = lax.axis_index("x"), lax.axis_index("y"), lax.axis_index("c")
        me_lin = 4 * x + 2 * y + c
        mine = [pltpu.make_async_copy(big[j].at[me_lin], rbig[j].at[0], local_sems.at[j]) for j in range(nb)]
        for cp in mine:
            cp.start()
        off = 0
        for ref, (_, cnt) in zip(small, SMALL):
            row_sc[:, off:off + cnt] = ref[...]
            off += cnt
        row_sc[:, off:off + LANES] = loss_ref[0:1, :]
        rsmall[me_lin] = row_sc[...]
        copies = []
        for k in range(1, N_DEV):
            px, py, pc = x ^ (k >> 2), y ^ ((k >> 1) & 1), c ^ (k & 1)
            peer = (px, py, pc)
            for j in range(nb):
                copies.append(pltpu.make_async_remote_copy(
                    src_ref=big[j].at[4 * px + 2 * py + pc], dst_ref=rbig[j].at[k],
                    send_sem=send_b.at[k - 1, j], recv_sem=recv_b.at[k - 1, j], device_id=peer, device_id_type=_MESH))
            copies.append(pltpu.make_async_remote_copy(
                src_ref=row_sc, dst_ref=rsmall.at[me_lin],
                send_sem=send_s.at[k - 1], recv_sem=recv_s.at[k - 1], device_id=peer, device_id_type=_MESH))
        for cp in copies:
            cp.start()
        for cp in copies:
            cp.wait_recv()
        for cp in copies:
            cp.wait_send()
        for cp in mine:
            cp.wait()

    hbm, vmem = pl.BlockSpec(memory_space=pl.ANY), pl.BlockSpec(memory_space=pltpu.VMEM)
    return pl.pallas_call(
        body, name="exchange_grads",
        out_shape=[_sds(a.shape, a.dtype) for a in big_blocks] + [_sds((N_DEV, 1, SMALL_LANES))],
        in_specs=[hbm] * nb + [vmem] * (ns + 1),
        out_specs=[hbm] * nb + [vmem],
        scratch_shapes=[pltpu.SemaphoreType.DMA((N_DEV - 1, nb)), pltpu.SemaphoreType.DMA((N_DEV - 1, nb)),
                        pltpu.SemaphoreType.DMA((N_DEV - 1,)), pltpu.SemaphoreType.DMA((N_DEV - 1,)),
                        pltpu.SemaphoreType.DMA((nb,)), pltpu.VMEM((1, SMALL_LANES), F32)],
    )(*big_blocks, *small_grads, loss_tile)


def _adamw_math(w, g, m, v):
    m = ADAM_B1 * m + (1.0 - ADAM_B1) * g
    v = ADAM_B2 * v + (1.0 - ADAM_B2) * (g * g)
    m_hat = m / (1.0 - ADAM_B1 ** ADAM_STEP)
    v_hat = v / (1.0 - ADAM_B2 ** ADAM_STEP)
    return -ADAM_LR * (m_hat / (jnp.sqrt(v_hat) + ADAM_EPS) + ADAM_WD * w), m, v


def _reduce_adamw(name, parts, w, m, v, row_blocks):
    _, rows, cols = parts.shape
    rb = rows // row_blocks

    def body(p_ref, w_ref, m_ref, v_ref, g_out, d_out, m_out, v_out):
        g = p_ref[0].astype(F32)
        for s in range(1, N_DEV):
            g = g + p_ref[s].astype(F32)
        g_out[0] = g
        d_out[0], m_out[0], v_out[0] = _adamw_math(w_ref[0], g, m_ref[0], v_ref[0])

    blk = pl.BlockSpec((1, rb, cols), lambda i: (0, i, 0))
    return pl.pallas_call(
        body, name="reduce_adamw_" + name, grid=(row_blocks,),
        in_specs=[pl.BlockSpec((N_DEV, rb, cols), lambda i: (0, i, 0)), blk, blk, blk],
        out_specs=[blk] * 4, out_shape=[_sds((1, rows, cols))] * 4,
        compiler_params=_ARB1,
    )(parts, w, m, v)


def _reduce_adamw_small(rows, ws, ms, vs):
    ns = len(SMALL)

    def body(r_ref, *refs):
        w_refs, m_refs, v_refs, outs = refs[:ns], refs[ns:2 * ns], refs[2 * ns:3 * ns], refs[3 * ns:]
        total = r_ref[0]
        for s in range(1, N_DEV):
            total = total + r_ref[s]
        off = 0
        for j, (_, cnt) in enumerate(SMALL):
            g = total[:, off:off + cnt]
            off += cnt
            outs[4 * j][...] = g
            outs[4 * j + 1][...], outs[4 * j + 2][...], outs[4 * j + 3][...] = _adamw_math(
                w_refs[j][...], g, m_refs[j][...], v_refs[j][...])
        outs[4 * ns][...] = total[:, off:off + LANES]

    vmem = pl.BlockSpec(memory_space=pltpu.VMEM)
    return pl.pallas_call(
        body, name="reduce_adamw_small",
        in_specs=[vmem] * (1 + 3 * ns), out_specs=[vmem] * (4 * ns + 1),
        out_shape=[_sds((1, cnt)) for _, cnt in SMALL for _ in range(4)] + [_sds((1, LANES))],
    )(rows, *ws, *ms, *vs)


def _shard_blocks(name, full):
    a = full[0]
    rows, cols = a.shape
    if name == "w_out":
        return a.reshape(N_DEV, rows // N_DEV, cols)
    return a.reshape(rows, N_DEV, cols // N_DEV).transpose(1, 0, 2)


def _unshard(name, blocks):
    _, rows, cols = blocks.shape
    if name == "w_out":
        return blocks.reshape(1, N_DEV * rows, cols)
    return blocks.transpose(1, 0, 2).reshape(1, rows, N_DEV * cols)


def kernel(x, positions, norm_pre_g, w_in, mla_q_norm_g, mla_w_uq, mla_kv_norm_g, mla_w_ukv, rw_mu, rw_w0, rw_w2, rw_a0, rw_a2, rw_k_k, rw_k_a, rw_r_k, rw_ln_g, rw_ln_b, w_out, norm_post_g, loss_target, m_norm_pre_g, m_w_in, m_mla_q_norm_g, m_mla_w_uq, m_mla_kv_norm_g, m_mla_w_ukv, m_rw_mu, m_rw_w0, m_rw_w2, m_rw_a0, m_rw_a2, m_rw_k_k, m_rw_k_a, m_rw_r_k, m_rw_ln_g, m_rw_ln_b, m_w_out, m_norm_post_g, v_norm_pre_g, v_w_in, v_mla_q_norm_g, v_mla_w_uq, v_mla_kv_norm_g, v_mla_w_ukv, v_rw_mu, v_rw_w0, v_rw_w2, v_rw_a0, v_rw_a2, v_rw_k_k, v_rw_k_a, v_rw_r_k, v_rw_ln_g, v_rw_ln_b, v_w_out, v_norm_post_g):
    given = dict(locals())
    w = {nm: given[nm] for nm in WEIGHTS}
    mom = {nm: given["m_" + nm] for nm in WEIGHTS}
    var = {nm: given["v_" + nm] for nm in WEIGHTS}
    sharded = list(SHARDED)

    gathered = _gather_shards([w[nm][0].astype(BF16) for nm in sharded])
    full = dict(w)
    for nm, blocks in zip(sharded, gathered):
        full[nm] = _unshard(nm, blocks)

    loss_part, grad_x, grads = _local_step(x, positions, loss_target, full)

    small_names = [nm for nm, _ in SMALL]
    row = lambda a: a.reshape(1, -1)
    got = _exchange_grads([_shard_blocks(nm, grads[nm]).astype(BF16) for nm in sharded],
                          [row(grads[nm]) for nm in small_names], loss_part)
    new = {}
    for nm, parts in zip(sharded, got[:-1]):
        new[nm] = _reduce_adamw(nm, parts, w[nm], mom[nm], var[nm], 4 if nm == "w_in" else 1)
    res = _reduce_adamw_small(got[-1], [row(w[nm]) for nm in small_names], [row(mom[nm]) for nm in small_names],
                              [row(var[nm]) for nm in small_names])
    for j, nm in enumerate(small_names):
        new[nm] = tuple(a.reshape(w[nm].shape) for a in res[4 * j:4 * j + 4])
    loss = res[-1][0, 0]
    return (loss, grad_x, *[new[nm][j] for j in range(4) for nm in WEIGHTS])
```

```python
import functools

import jax
import jax.numpy as jnp
from jax import lax
from jax.experimental import pallas as pl
from jax.experimental.pallas import tpu as pltpu

F32 = jnp.float32
BF16 = jnp.bfloat16

D_MODEL = 1024
MLA_HEADS = 4
MLA_NOPE = 128
MLA_ROPE = 64
MLA_V = 128
MLA_WIDTH = MLA_HEADS * MLA_V
Q_LORA = 256
KV_LORA = 128
ROPE_THETA = 10000.0
RW_HEAD = 64
RW_WIDTH = 512
RW_HEADS = RW_WIDTH // RW_HEAD
LORA = 64
RW_COLS = 3 * RW_WIDTH + 2 * LORA
MLA_COLS = Q_LORA + KV_LORA + MLA_ROPE
D_IN = MLA_COLS + RW_COLS + D_MODEL
RW_GN_EPS = 64e-5
NORM_EPS = 1e-6
ATT_SCALE = (MLA_NOPE + MLA_ROPE) ** -0.5
ADAM_LR, ADAM_B1, ADAM_B2, ADAM_EPS, ADAM_WD, ADAM_STEP = 0.001, 0.9, 0.999, 1e-08, 0.01, 10
N_DEV = 8
LANES = 128
MXU = 256

PM_W = 512
WP_COLS = PM_W + RW_COLS + D_MODEL
RW_PIECES = ((0, 512), (512, 1024), (1024, 1536), (1536, 1664))

SHARDED = ("w_in", "mla_w_uq", "mla_w_ukv", "rw_w2", "rw_a2", "w_out")
SMALL = (("norm_pre_g", 1024), ("mla_q_norm_g", 256), ("mla_kv_norm_g", 128), ("rw_mu", 1664), ("rw_w0", 512),
         ("rw_a0", 512), ("rw_k_k", 512), ("rw_k_a", 512), ("rw_r_k", 512), ("rw_ln_g", 512), ("rw_ln_b", 512),
         ("norm_post_g", 1024))
SMALL_N = sum(n for _, n in SMALL)
WEIGHTS = ("norm_pre_g", "w_in", "mla_q_norm_g", "mla_w_uq", "mla_kv_norm_g", "mla_w_ukv", "rw_mu", "rw_w0", "rw_w2",
           "rw_a0", "rw_a2", "rw_k_k", "rw_k_a", "rw_r_k", "rw_ln_g", "rw_ln_b", "w_out", "norm_post_g")


def _seg_ones():
    r = lax.broadcasted_iota(jnp.int32, (MXU, MXU), 0) >> 6
    c = lax.broadcasted_iota(jnp.int32, (MXU, MXU), 1) >> 6
    return jnp.where(r == c, 1.0, 0.0).astype(BF16)


def _seg_dot(x, ones, passes):
    parts, rem = [], x
    for p in range(passes):
        hb = rem.astype(BF16)
        parts.append(hb)
        if p + 1 < passes:
            rem = rem - hb.astype(F32)
    outs = []
    for j in range(x.shape[1] // MXU):
        acc = None
        for hb in parts:
            d = jnp.dot(hb[:, MXU * j:MXU * (j + 1)], ones, preferred_element_type=F32)
            acc = d if acc is None else acc + d
        outs.append(acc)
    return outs[0] if len(outs) == 1 else jnp.concatenate(outs, axis=1)


def _seg_multi(xs, ones, passes):
    his = [x.astype(BF16) for x in xs]
    hi = jnp.concatenate(his, axis=0)
    if passes == 2:
        lo = jnp.concatenate([(x - h.astype(F32)).astype(BF16) for x, h in zip(xs, his)], axis=0)
        rhs = jnp.concatenate([ones, ones], axis=0)
    halves = []
    for j in range(hi.shape[1] // MXU):
        sl = slice(MXU * j, MXU * (j + 1))
        if passes == 2:
            halves.append(jnp.dot(jnp.concatenate([hi[:, sl], lo[:, sl]], axis=1), rhs, preferred_element_type=F32))
        else:
            halves.append(jnp.dot(hi[:, sl], ones, preferred_element_type=F32))
    full = jnp.concatenate(halves, axis=1)
    m = xs[0].shape[0]
    return [full[m * i:m * (i + 1)] for i in range(len(xs))]


@jax.custom_vjp
def _segsum(x):
    return _seg_dot(x, _seg_ones(), 2)


_segsum.defvjp(lambda x: (_segsum(x), None), lambda _, g: (_segsum(g),))


@jax.custom_vjp
def _bdot(a, w):
    return jnp.dot(a.astype(BF16), w.astype(BF16), preferred_element_type=F32)


def _bdot_fwd(a, w):
    return _bdot(a, w), (a, w)


def _bdot_bwd(res, g):
    a, w = res
    gb = g.astype(BF16)
    da = lax.dot_general(gb, w.astype(BF16), (((1,), (1,)), ((), ())), preferred_element_type=F32)
    dw = lax.dot_general(a.astype(BF16), gb, (((0,), (0,)), ((), ())), preferred_element_type=F32)
    return da, dw


_bdot.defvjp(_bdot_fwd, _bdot_bwd)


def _rot_impl(x):
    w = x.shape[1]
    lane = lax.broadcasted_iota(jnp.int32, x.shape, 1)
    return jnp.where((lane & 63) < 32, -pltpu.roll(x, w - 32, 1), pltpu.roll(x, 32, 1))


@jax.custom_vjp
def _rot(x):
    return _rot_impl(x)


_rot.defvjp(lambda x: (_rot_impl(x), None), lambda _, g: (-_rot_impl(g),))


def _rms(x, g):
    return x * lax.rsqrt(jnp.mean(x * x, axis=-1, keepdims=True) + NORM_EPS) * g


def _shift_rows(p, prev_row):
    row = lax.broadcasted_iota(jnp.int32, p.shape, 0)
    return jnp.where(row == 0, prev_row, pltpu.roll(p, 1, 0))


def _unshift_rows(g, next_row):
    row = lax.broadcasted_iota(jnp.int32, g.shape, 0)
    return jnp.where(row == g.shape[0] - 1, next_row, pltpu.roll(g, g.shape[0] - 1, 0))


def _f_mla(cq, ckv, kr, cos, sin, g_q, wqn, wqr, g_kv, wkv):
    qn = _rms(cq, g_q)
    q_nope = _bdot(qn, wqn)
    q_r = _bdot(qn, wqr)
    cos4 = jnp.concatenate([cos] * MLA_HEADS, axis=1)
    sin4 = jnp.concatenate([sin] * MLA_HEADS, axis=1)
    q_rope = q_r * cos4 + _rot(q_r) * sin4
    kv = _bdot(_rms(ckv, g_kv), wkv)
    k_rope = kr * cos + _rot(kr) * sin
    return q_nope, q_rope, kv, k_rope


def _f_rw(pr, pk, pv, pt, sr, sk, sv, st, mu_r, mu_k, mu_v, mu_t, w0, w2p, a0, a2p, k_k, k_a):
    r = pr + (sr - pr) * mu_r
    k = pk + (sk - pk) * mu_k
    v = pv + (sv - pv) * mu_v
    t = pt + (st - pt) * mu_t
    nwl = -(w0 + _bdot(jnp.tanh(t), w2p))
    softplus = jnp.maximum(nwl, 0.0) + jnp.log(1.0 + jnp.exp(-jnp.abs(nwl)))
    decay = jnp.exp(-jnp.exp(-softplus - 0.5))
    a = jax.nn.sigmoid(a0 + _bdot(t, a2p))
    kk = k * k_k
    kk = kk / jnp.maximum(jnp.sqrt(_segsum(kk * kk)), 1e-12)
    k2 = k * (1.0 + (a - 1.0) * k_a)
    return r, decay, k2, v, -kk, kk * a


def _f_head(ys, r, k, v, ym, z1, z2, x, tgt, ln_g, ln_b, r_k, w1, w2, g_post):
    inv = 1.0 / RW_HEAD
    yc = ys - _segsum(ys) * inv
    var = _segsum(yc * yc) * inv
    y = yc * lax.rsqrt(var + RW_GN_EPS) * ln_g + ln_b
    y_rw = y + _segsum(r * k * r_k) * v
    c1 = ym * (z1 * jax.nn.sigmoid(z1))
    c2 = y_rw * (z2 * jax.nn.sigmoid(z2))
    out = _bdot(c1, w1) + _bdot(c2, w2)
    err = x + _rms(out, g_post) - tgt
    per_row = jnp.sum(err * err, axis=1, keepdims=True)
    return jnp.sum(per_row, axis=0, keepdims=True) * (0.5 / D_MODEL)


def _rows(tm, width):
    return pl.BlockSpec((tm, width), lambda i: (i, 0))


def _whole(shape):
    zeros = (0,) * len(shape)
    return pl.BlockSpec(shape, lambda i: zeros)


def _sds(shape, dtype=F32):
    return jax.ShapeDtypeStruct(shape, dtype)


_ARB1 = pltpu.CompilerParams(dimension_semantics=("arbitrary",))


def _acc(ref, val, first):
    @pl.when(first)
    def _():
        ref[...] = val

    @pl.when(jnp.logical_not(first))
    def _():
        ref[...] += val


def _fwd_a(x2, g_pre, wp, tm):
    n = x2.shape[0]

    def body(x_ref, g_ref, w_ref, ut_ref, pm_ref, prw_ref, z_ref):
        u = _rms(x_ref[...], g_ref[...])
        ub = u.astype(BF16)
        ut_ref[...] = u.T.astype(BF16)
        pm_ref[...] = jnp.dot(ub, w_ref[:, 0:PM_W], preferred_element_type=F32)
        prw_ref[...] = jnp.dot(ub, w_ref[:, PM_W:PM_W + RW_COLS], preferred_element_type=F32)
        z_ref[...] = jnp.dot(ub, w_ref[:, PM_W + RW_COLS:WP_COLS], preferred_element_type=F32)

    return pl.pallas_call(
        body, name="fwd_a", grid=(n // tm,),
        in_specs=[_rows(tm, D_MODEL), _whole((1, D_MODEL)), _whole((D_MODEL, WP_COLS))],
        out_specs=[pl.BlockSpec((D_MODEL, tm), lambda i: (0, i)), _rows(tm, PM_W), _rows(tm, RW_COLS), _rows(tm, D_MODEL)],
        out_shape=[_sds((D_MODEL, n), BF16), _sds((n, PM_W)), _sds((n, RW_COLS)), _sds((n, D_MODEL))],
        compiler_params=_ARB1,
    )(x2, g_pre, wp)


def _rope_tables(pos_row, invf_col, tm):
    n = pos_row.shape[1]

    def body(p_ref, f_ref, c_ref, s_ref):
        distinct = MLA_ROPE // 2
        ang = f_ref[0:distinct, :] * p_ref[...].astype(F32)
        c_ref[...] = jnp.concatenate([jnp.cos(ang)] * (LANES // distinct), axis=0).T
        s_ref[...] = jnp.concatenate([jnp.sin(ang)] * (LANES // distinct), axis=0).T

    return pl.pallas_call(
        body, name="rope_tables", grid=(n // tm,),
        in_specs=[pl.BlockSpec((1, tm), lambda i: (0, i)), _whole((LANES, 1))],
        out_specs=[_rows(tm, LANES), _rows(tm, LANES)],
        out_shape=[_sds((n, LANES)), _sds((n, LANES))],
        compiler_params=_ARB1,
    )(pos_row, invf_col)


_B_WEIGHT_SHAPES = ((1, Q_LORA), (Q_LORA, 512), (Q_LORA, 512), (1, KV_LORA), (KV_LORA, 1024), (1, RW_COLS), (1, RW_WIDTH),
                    (LANES, RW_WIDTH), (1, RW_WIDTH), (LANES, RW_WIDTH), (1, RW_WIDTH), (1, RW_WIDTH))


def _halo_prev(tm):
    return pl.BlockSpec((8, RW_COLS), lambda i: (jnp.maximum(i * (tm // 8) - 1, 0), 0))


def _b_operands(pm_ref, prw_ref, halo_ref, wrefs, tile, tiles_per_seq):
    g_q, wqn, wqr, g_kv, wkv, mu, w0, w2p, a0, a2p, k_k, k_a = wrefs
    mla_in = (pm_ref[:, 0:Q_LORA], pm_ref[:, Q_LORA:Q_LORA + KV_LORA], pm_ref[:, Q_LORA + KV_LORA:PM_W])
    mla_w = (g_q[...], wqn[...], wqr[...], g_kv[...], wkv[...])
    keep = jnp.where(tile % tiles_per_seq == 0, 0.0, 1.0)
    prev = halo_ref[7:8, :] * keep
    ps = tuple(prw_ref[:, a:b] for a, b in RW_PIECES)
    ss = tuple(_shift_rows(p, prev[:, a:b]) for p, (a, b) in zip(ps, RW_PIECES))
    rw_w = tuple(mu[:, a:b] for a, b in RW_PIECES) + (w0[...], w2p[...], a0[...], a2p[...], k_k[...], k_a[...])
    return mla_in, mla_w, ps + ss, rw_w


def _fwd_b(pm, prw, cos, sin, bw, tm, tiles_per_seq):
    n = pm.shape[0]

    def body(pm_ref, prw_ref, halo_ref, cos_ref, sin_ref, *refs):
        wrefs, outs = refs[:12], refs[12:]
        mla_in, mla_w, rw_in, rw_w = _b_operands(pm_ref, prw_ref, halo_ref, wrefs, pl.program_id(0), tiles_per_seq)
        res = _f_mla(*mla_in, cos_ref[...], sin_ref[...], *mla_w) + _f_rw(*rw_in, *rw_w)
        for o_ref, val in zip(outs, res):
            o_ref[...] = val.astype(o_ref.dtype)

    widths = (512, 512, 1024, LANES) + (RW_WIDTH,) * 6
    return pl.pallas_call(
        body, name="fwd_b", grid=(n // tm,),
        in_specs=[_rows(tm, PM_W), _rows(tm, RW_COLS), _halo_prev(tm), _rows(tm, LANES), _rows(tm, LANES)]
        + [_whole(s) for s in _B_WEIGHT_SHAPES],
        out_specs=[_rows(tm, w) for w in widths],
        out_shape=[_sds((n, w), BF16 if j < 4 else F32) for j, w in enumerate(widths)],
        compiler_params=_ARB1,
    )(pm, prw, prw, cos, sin, *bw)


def _bwd_b(pm, prw, cos, sin, bw, cts, dkr_heads, tm, tiles_per_seq):
    n = pm.shape[0]

    ct_widths = (512, 512, 1024) + (RW_WIDTH,) * 9
    n_ct = len(ct_widths)

    def body(pm_ref, prw_ref, halo_ref, cos_ref, sin_ref, *refs):
        wrefs, ct_refs, dkr_ref = refs[:12], refs[12:12 + n_ct], refs[12 + n_ct]
        dpm_ref, dprw_ref, dps_ref = refs[13 + n_ct:16 + n_ct]
        wg_refs = refs[16 + n_ct:]
        tile = pl.program_id(0)
        first = tile == 0
        mla_in, mla_w, rw_in, rw_w = _b_operands(pm_ref, prw_ref, halo_ref, wrefs, tile, tiles_per_seq)
        cos, sin = cos_ref[...], sin_ref[...]
        ct = [r[...] for r in ct_refs]
        _, vjp_mla = jax.vjp(lambda *a: _f_mla(*a[:3], cos, sin, *a[3:]), *mla_in, *mla_w)
        dkr = dkr_ref[0] + dkr_ref[1] + dkr_ref[2] + dkr_ref[3]
        d_mla = vjp_mla((ct[0], ct[1], ct[2], dkr))
        dpm_ref[:, 0:Q_LORA] = d_mla[0]
        dpm_ref[:, Q_LORA:Q_LORA + KV_LORA] = d_mla[1]
        dpm_ref[:, Q_LORA + KV_LORA:PM_W] = d_mla[2]
        _, vjp_rw = jax.vjp(_f_rw, *rw_in, *rw_w)
        d_rw = vjp_rw((ct[3] + ct[4], ct[5], ct[6] + ct[7], ct[8] + ct[9], ct[10], ct[11]))
        for j, (a, b) in enumerate(RW_PIECES):
            dprw_ref[:, a:b] = d_rw[j]
            dps_ref[:, a:b] = d_rw[4 + j]
        g_q, wqn, wqr, g_kv, wkv, mu, w0, w2p, a0, a2p, k_k, k_a = wg_refs
        for ref, val in zip((g_q, wqn, wqr, g_kv, wkv), d_mla[3:]):
            _acc(ref, val, first)
        for j, (a, b) in enumerate(RW_PIECES):
            _acc(mu.at[:, a:b], d_rw[8 + j], first)
        for ref, val in zip((w0, w2p, a0, a2p, k_k, k_a), d_rw[12:]):
            _acc(ref, val, first)

    return pl.pallas_call(
        body, name="bwd_b", grid=(n // tm,),
        in_specs=[_rows(tm, PM_W), _rows(tm, RW_COLS), _halo_prev(tm), _rows(tm, LANES), _rows(tm, LANES)]
        + [_whole(s) for s in _B_WEIGHT_SHAPES] + [_rows(tm, w) for w in ct_widths]
        + [pl.BlockSpec((MLA_HEADS, tm, LANES), lambda i: (0, i, 0))],
        out_specs=[_rows(tm, PM_W), _rows(tm, RW_COLS), _rows(tm, RW_COLS)] + [_whole(s) for s in _B_WEIGHT_SHAPES],
        out_shape=[_sds((n, PM_W)), _sds((n, RW_COLS)), _sds((n, RW_COLS))] + [_sds(s) for s in _B_WEIGHT_SHAPES],
        compiler_params=_ARB1,
    )(pm, prw, prw, cos, sin, *bw, *cts, dkr_heads)


def _head(ys, r, k, v, ym, z, x2, tgt, hw, tm):
    n = x2.shape[0]
    h_shapes = ((1, RW_WIDTH), (1, RW_WIDTH), (1, RW_WIDTH), (D_MODEL, D_MODEL), (1, D_MODEL))

    def body(ys_ref, r_ref, k_ref, v_ref, ym_ref, z_ref, x_ref, t_ref, lng, lnb, rk, wout, gpost,
             dys_ref, dr_ref, dk_ref, dv_ref, dym_ref, dz_ref, dx_ref, loss_ref, dlng, dlnb, drk, dwout, dgpost):
        first = pl.program_id(0) == 0
        tgt_v = t_ref[...]
        args = (ys_ref[...], r_ref[...], k_ref[...], v_ref[...], ym_ref[...], z_ref[:, 0:MLA_WIDTH], z_ref[:, MLA_WIDTH:D_MODEL],
                x_ref[...], lng[...], lnb[...], rk[...], wout[0:MLA_WIDTH, :], wout[MLA_WIDTH:D_MODEL, :], gpost[...])
        loss, vjp = jax.vjp(lambda *a: _f_head(*a[:8], tgt_v, *a[8:]), *args)
        d = vjp(jnp.ones((1, 1), F32))
        dys_ref[...] = d[0]
        dr_ref[...] = d[1]
        dk_ref[...] = d[2]
        dv_ref[...] = d[3]
        dym_ref[...] = d[4].astype(BF16)
        dz_ref[:, 0:MLA_WIDTH] = d[5]
        dz_ref[:, MLA_WIDTH:D_MODEL] = d[6]
        dx_ref[...] = d[7]
        _acc(loss_ref, jnp.broadcast_to(loss, (8, LANES)), first)
        _acc(dlng, d[8], first)
        _acc(dlnb, d[9], first)
        _acc(drk, d[10], first)
        _acc(dwout.at[0:MLA_WIDTH, :], d[11], first)
        _acc(dwout.at[MLA_WIDTH:D_MODEL, :], d[12], first)
        _acc(dgpost, d[13], first)

    widths = (RW_WIDTH,) * 4 + (MLA_WIDTH, D_MODEL, D_MODEL)
    return pl.pallas_call(
        body, name="head", grid=(n // tm,),
        in_specs=[_rows(tm, RW_WIDTH)] * 4 + [_rows(tm, MLA_WIDTH), _rows(tm, D_MODEL), _rows(tm, D_MODEL), _rows(tm, D_MODEL)]
        + [_whole(s) for s in h_shapes],
        out_specs=[_rows(tm, w) for w in widths] + [_whole((8, LANES))] + [_whole(s) for s in h_shapes],
        out_shape=[_sds((n, w), BF16 if j == 4 else F32) for j, w in enumerate(widths)] + [_sds((8, LANES))]
        + [_sds(s) for s in h_shapes],
        compiler_params=_ARB1,
    )(ys, r, k, v, ym, z, x2, tgt, *hw)


def _halo_next(tm, n):
    last = n // 8 - 1
    return pl.BlockSpec((8, RW_COLS), lambda i: (jnp.minimum((i + 1) * (tm // 8), last), 0))


def _bwd_a(x2, g_pre, wp, dpm, dprw, dps, dz, dxres, tm, tiles_per_seq):
    n = x2.shape[0]
    nt_dims = (((1,), (1,)), ((), ()))

    def body(x_ref, g_ref, w_ref, dpm_ref, dprw_ref, dps_ref, nxt_ref, dz_ref, dxres_ref, gx_ref, dpb_ref, dg_ref):
        tile = pl.program_id(0)
        keep = jnp.where((tile + 1) % tiles_per_seq == 0, 0.0, 1.0)
        dprw_v = dprw_ref[...] + _unshift_rows(dps_ref[...], nxt_ref[0:1, :] * keep)
        dpm_b, dprw_b, dz_b = dpm_ref[...].astype(BF16), dprw_v.astype(BF16), dz_ref[...].astype(BF16)
        dpb_ref[:, 0:PM_W] = dpm_b
        dpb_ref[:, PM_W:PM_W + RW_COLS] = dprw_b
        dpb_ref[:, PM_W + RW_COLS:WP_COLS] = dz_b
        du = (lax.dot_general(dpm_b, w_ref[:, 0:PM_W], nt_dims, preferred_element_type=F32)
              + lax.dot_general(dprw_b, w_ref[:, PM_W:PM_W + RW_COLS], nt_dims, preferred_element_type=F32)
              + lax.dot_general(dz_b, w_ref[:, PM_W + RW_COLS:WP_COLS], nt_dims, preferred_element_type=F32))
        x = x_ref[...]
        xhat = x * lax.rsqrt(jnp.mean(x * x, axis=-1, keepdims=True) + NORM_EPS)
        dxn = du * g_ref[...]
        dx = (dxn - xhat * jnp.mean(dxn * xhat, axis=-1, keepdims=True)) * lax.rsqrt(jnp.mean(x * x, axis=-1, keepdims=True) + NORM_EPS)
        gx_ref[...] = dx + dxres_ref[...]
        _acc(dg_ref, jnp.sum(du * xhat, axis=0, keepdims=True), tile == 0)

    return pl.pallas_call(
        body, name="bwd_a", grid=(n // tm,),
        in_specs=[_rows(tm, D_MODEL), _whole((1, D_MODEL)), _whole((D_MODEL, WP_COLS)), _rows(tm, PM_W), _rows(tm, RW_COLS),
                  _rows(tm, RW_COLS), _halo_next(tm, n), _rows(tm, D_MODEL), _rows(tm, D_MODEL)],
        out_specs=[_rows(tm, D_MODEL), _rows(tm, WP_COLS), _whole((1, D_MODEL))],
        out_shape=[_sds((n, D_MODEL)), _sds((n, WP_COLS), BF16), _sds((1, D_MODEL))],
        compiler_params=_ARB1,
    )(x2, g_pre, wp, dpm, dprw, dps, dps, dz, dxres)


def _dw_in(ut, dpb, tk, tn):
    n = ut.shape[1]
    steps = n // tk

    def body(u_ref, d_ref, o_ref, acc_sc):
        k = pl.program_id(1)
        _acc(acc_sc, jnp.dot(u_ref[...], d_ref[...], preferred_element_type=F32), k == 0)

        @pl.when(k == steps - 1)
        def _():
            o_ref[...] = acc_sc[...].astype(BF16)

    return pl.pallas_call(
        body, name="dw_in", grid=(WP_COLS // tn, steps),
        in_specs=[pl.BlockSpec((D_MODEL, tk), lambda j, k: (0, k)), pl.BlockSpec((tk, tn), lambda j, k: (k, j))],
        out_specs=pl.BlockSpec((D_MODEL, tn), lambda j, k: (0, j)),
        out_shape=_sds((D_MODEL, WP_COLS), BF16),
        scratch_shapes=[pltpu.VMEM((D_MODEL, tn), F32)],
        compiler_params=pltpu.CompilerParams(dimension_semantics=("arbitrary", "arbitrary")),
    )(ut, dpb)


ATT_BLK = 256
_NT = (((1,), (1,)), ((), ()))
_TN = (((0,), (0,)), ((), ()))


def _causal(q0, k0, blk, blk_k=None):
    blk_k = blk if blk_k is None else blk_k
    row = q0 + lax.broadcasted_iota(jnp.int32, (blk, blk_k), 0)
    col = k0 + lax.broadcasted_iota(jnp.int32, (blk, blk_k), 1)
    return row >= col


def _attn_fwd(qn, qr, kv, kr):
    bsz, t, _ = qn.shape
    blk = min(ATT_BLK, t)

    heads = range(MLA_HEADS)

    def body(qn_ref, qr_ref, kv_ref, kr_ref, o_ref, lse_ref):
        qi = pl.program_id(1)
        q = [jnp.concatenate([qn_ref[:, LANES * h:LANES * (h + 1)], qr_ref[:, LANES * h:LANES * (h + 1)]], axis=1) for h in heads]
        lower = _causal(0, 0, blk)

        def kv_step(j, carry, diagonal):
            ks = pl.multiple_of(j * blk, blk)
            k_rope = kr_ref[pl.ds(ks, blk), :]
            def score(h):
                k = jnp.concatenate([kv_ref[pl.ds(ks, blk), 2 * LANES * h:2 * LANES * h + LANES], k_rope], axis=1)
                return lax.dot_general(q[h], k, _NT, preferred_element_type=F32)

            out = []
            nxt = score(0)
            for h in heads:
                s = nxt * ATT_SCALE
                if h + 1 < MLA_HEADS:
                    nxt = score(h + 1)
                m, l, acc = carry[h]
                if diagonal:
                    s = jnp.where(lower, s, -1e30)
                m_new = jnp.maximum(m, jnp.max(s, axis=1, keepdims=True))
                alpha = jnp.exp(m - m_new)
                p = jnp.exp(s - m_new)
                l = alpha * l + jnp.sum(p, axis=1, keepdims=True)
                v = kv_ref[pl.ds(ks, blk), 2 * LANES * h + LANES:2 * LANES * (h + 1)]
                out.append((m_new, l, alpha * acc + jnp.dot(p.astype(BF16), v, preferred_element_type=F32)))
            return tuple(out)

        one = (jnp.full((blk, 1), -1e30, F32), jnp.zeros((blk, 1), F32), jnp.zeros((blk, MLA_V), F32))
        carry = lax.fori_loop(0, qi // 2, lambda pr, c: kv_step(2 * pr + 1, kv_step(2 * pr, c, False), False),
                              (one,) * MLA_HEADS)
        carry = lax.cond(qi % 2 == 1, lambda c: kv_step(qi - 1, c, False), lambda c: c, carry)
        carry = kv_step(qi, carry, True)
        for h in heads:
            m, l, acc = carry[h]
            o_ref[:, LANES * h:LANES * (h + 1)] = acc / l
            lse_ref[h] = jnp.broadcast_to(m + jnp.log(l), (blk, LANES))

    return pl.pallas_call(
        body, name="attn_fwd", grid=(bsz, t // blk),
        in_specs=[pl.BlockSpec((None, blk, MLA_WIDTH), lambda b, i: (b, i, 0)),
                  pl.BlockSpec((None, blk, MLA_WIDTH), lambda b, i: (b, i, 0)),
                  pl.BlockSpec((None, t, 2 * MLA_WIDTH), lambda b, i: (b, 0, 0)),
                  pl.BlockSpec((None, t, LANES), lambda b, i: (b, 0, 0))],
        out_specs=[pl.BlockSpec((None, blk, MLA_WIDTH), lambda b, i: (b, i, 0)),
                   pl.BlockSpec((None, MLA_HEADS, blk, LANES), lambda b, i: (b, 0, i, 0))],
        out_shape=[_sds((bsz, t, MLA_WIDTH)), _sds((bsz, MLA_HEADS, t, LANES))],
        compiler_params=pltpu.CompilerParams(dimension_semantics=("arbitrary", "arbitrary")),
    )(qn, qr, kv, kr)


def _attn_bwd(qn, qr, kv, kr, o, do, lse):
    bsz, t, _ = qn.shape
    blk = min(ATT_BLK, t)
    nb = t // blk
    assert nb % 2 == 0, "query blocks are taken in pairs"

    def body(qn_ref, qr_ref, kn_ref, kr_ref, v_ref, o_ref, do_ref, lse_ref, dqn_ref, dqr_ref, dkv_ref, dkr_ref, dq_sc, delta_sc):
        dq_sc[...] = jnp.zeros_like(dq_sc)
        delta_sc[...] = jnp.sum(do_ref[...].astype(F32) * o_ref[...], axis=1, keepdims=True)

        lower = _causal(0, 0, blk)

        def q_blocks(j, k, vb, carry, blocks, diagonal):
            dk, dv = carry
            us = range(len(blocks))
            qs = [i * blk if isinstance(i, int) else pl.multiple_of(i * blk, blk) for i in blocks]
            q = [jnp.concatenate([qn_ref[pl.ds(qs[u], blk), :], qr_ref[pl.ds(qs[u], blk), :]], axis=1) for u in us]
            dob = [do_ref[pl.ds(qs[u], blk), :] for u in us]
            s = [lax.dot_general(q[u], k, _NT, preferred_element_type=F32) for u in us]
            dp = [lax.dot_general(dob[u], vb, _NT, preferred_element_type=F32) for u in us]
            pb, ds = [], []
            for u in us:
                p = jnp.exp(s[u] * ATT_SCALE - lse_ref[pl.ds(qs[u], blk), 0:1])
                if diagonal == u:
                    p = jnp.where(lower, p, 0.0)
                pb.append(p.astype(BF16))
                ds.append((p * (dp[u] - delta_sc[pl.ds(qs[u], blk), :]) * ATT_SCALE).astype(BF16))
            for u in us:
                dv = dv + lax.dot_general(pb[u], dob[u], _TN, preferred_element_type=F32)
            for u in us:
                dq_sc[pl.ds(qs[u], blk), :] += jnp.dot(ds[u], k, preferred_element_type=F32)
                dk = dk + lax.dot_general(ds[u], q[u], _TN, preferred_element_type=F32)
            return dk, dv

        for j in range(nb):
            ks = j * blk
            k = jnp.concatenate([kn_ref[ks:ks + blk, :], kr_ref[ks:ks + blk, :]], axis=1)
            vb = v_ref[ks:ks + blk, :]
            carry = (jnp.zeros((blk, 2 * LANES), F32), jnp.zeros((blk, MLA_V), F32))
            if j % 2 == 0:
                carry = q_blocks(j, k, vb, carry, [j, j + 1], 0)
            else:
                carry = q_blocks(j, k, vb, carry, [j], 0)
            pairs_from = j // 2 + 1
            if nb // 2 - pairs_from > 0:
                carry = lax.fori_loop(pairs_from, nb // 2,
                                      lambda pr, c, j=j, k=k, vb=vb: q_blocks(j, k, vb, c, [2 * pr, 2 * pr + 1], None),
                                      carry, unroll=2)
            dk, dv = carry
            dkv_ref[ks:ks + blk, 0:LANES] = dk[:, 0:LANES]
            dkv_ref[ks:ks + blk, LANES:2 * LANES] = dv
            dkr_ref[ks:ks + blk, :] = dk[:, LANES:2 * LANES]
        dqn_ref[...] = dq_sc[:, 0:LANES]
        dqr_ref[...] = dq_sc[:, LANES:2 * LANES]

    head_col = lambda b, h: (b, 0, h)
    return pl.pallas_call(
        body, name="attn_bwd", grid=(bsz, MLA_HEADS),
        in_specs=[pl.BlockSpec((None, t, LANES), head_col), pl.BlockSpec((None, t, LANES), head_col),
                  pl.BlockSpec((None, t, LANES), lambda b, h: (b, 0, 2 * h)),
                  pl.BlockSpec((None, t, LANES), lambda b, h: (b, 0, 0)),
                  pl.BlockSpec((None, t, LANES), lambda b, h: (b, 0, 2 * h + 1)),
                  pl.BlockSpec((None, t, LANES), head_col), pl.BlockSpec((None, t, LANES), head_col),
                  pl.BlockSpec((None, None, t, LANES), lambda b, h: (b, h, 0, 0))],
        out_specs=[pl.BlockSpec((None, t, LANES), head_col), pl.BlockSpec((None, t, LANES), head_col),
                   pl.BlockSpec((None, t, 2 * LANES), head_col),
                   pl.BlockSpec((None, None, t, LANES), lambda b, h: (h, b, 0, 0))],
        out_shape=[_sds((bsz, t, MLA_WIDTH)), _sds((bsz, t, MLA_WIDTH)), _sds((bsz, t, 2 * MLA_WIDTH)),
                   _sds((MLA_HEADS, bsz, t, LANES))],
        scratch_shapes=[pltpu.VMEM((t, 2 * LANES), F32), pltpu.VMEM((t, 1), F32)],
        compiler_params=pltpu.CompilerParams(dimension_semantics=("arbitrary", "arbitrary")),
    )(qn, qr, kv, kr, kv, o, do, lse)


SCAN_CHUNK = 16


def _diag_mask():
    row = lax.broadcasted_iota(jnp.int32, (RW_HEAD, RW_WIDTH), 0)
    lane = lax.broadcasted_iota(jnp.int32, (RW_HEAD, RW_WIDTH), 1)
    return jnp.where(row == (lane & (RW_HEAD - 1)), 1.0, 0.0)


def _time_minor(a):
    bsz, t, _ = a.shape
    a = a.reshape(bsz, t // SCAN_CHUNK, SCAN_CHUNK, RW_HEADS, RW_HEAD)
    return a.transpose(0, 1, 4, 3, 2).reshape(bsz, t // SCAN_CHUNK, RW_HEAD, RW_HEADS * SCAN_CHUNK)


def _head_expand():
    l = lax.broadcasted_iota(jnp.int32, (2 * LANES, RW_WIDTH), 0)
    n = lax.broadcasted_iota(jnp.int32, (2 * LANES, RW_WIDTH), 1)
    return jnp.where(((l & (LANES - 1)) >> 4) == (n >> 6), 1.0, 0.0).astype(BF16)


BCAST_GROUP = 4


def _outer_chunk(tm_ref, row_ref, out_sc, expand, seqs):
    step_of_lane = lax.broadcasted_iota(jnp.int32, (RW_HEAD, LANES), 1) & (SCAN_CHUNK - 1)
    tiles = [tm_ref[bi, 0] for bi in seqs]
    for t0 in range(0, SCAN_CHUNK, BCAST_GROUP):
        parts = []
        for t in range(t0, t0 + BCAST_GROUP):
            for tile in tiles:
                a = jnp.where(step_of_lane == t, tile, 0.0)
                hi = a.astype(BF16)
                parts.append(jnp.concatenate([hi, (a - hi.astype(F32)).astype(BF16)], axis=1))
        cols = jnp.dot(jnp.concatenate(parts, axis=0), expand, preferred_element_type=F32)
        for j, t in enumerate(range(t0, t0 + BCAST_GROUP)):
            base = j * RW_HEAD * len(seqs)
            out_sc[t] = jnp.concatenate([cols[base + RW_HEAD * bi:base + RW_HEAD * (bi + 1)] * row_ref[bi, t:t + 1, :]
                                         for bi in seqs], axis=0)


def _fold8(x):
    acc = x[0:8]
    for j in range(1, x.shape[0] // 8):
        acc = acc + x[8 * j:8 * (j + 1)]
    return acc


def _rows8(at):
    return pl.ds(at * 8 if isinstance(at, int) else pl.multiple_of(at * 8, 8), 8)


def _put8(sc, bi, at, val):
    for j in range(RW_WIDTH // LANES):
        sc[bi * (RW_WIDTH // LANES) + j, _rows8(at), :] = val[:, LANES * j:LANES * (j + 1)]


def _unfold8(sc, bi, steps):
    tiles = []
    for j in range(RW_WIDTH // LANES):
        view = sc.at[bi * (RW_WIDTH // LANES) + j]
        acc = view[pl.ds(0, steps, stride=8), :]
        for s in range(1, 8):
            acc = acc + view[pl.ds(s, steps, stride=8), :]
        tiles.append(acc)
    return jnp.concatenate(tiles, axis=1)


def _scan_fwd(r, w, k, vt, nkk, b):
    bsz, t, _ = r.shape
    tc = SCAN_CHUNK

    def body(r_ref, w_ref, k_ref, n_ref, b_ref, vt_ref, y_ref, st_ref, s_sc, vc_sc, y_sc):
        @pl.when(pl.program_id(0) == 0)
        def _():
            s_sc[...] = jnp.zeros_like(s_sc)

        ones = _seg_ones()
        diag = _diag_mask()
        seqs = range(bsz)
        _outer_chunk(vt_ref, k_ref, vc_sc, _head_expand(), seqs)

        def put_y(ya, at):
            for bi in seqs:
                _put8(y_sc, bi, at, _fold8(ya[bi] * diag))

        def step(i, _):
            row = lambda ref, bi: ref[bi, pl.ds(i, 1), :]
            prev = jnp.maximum(i - 1, 0)
            s_old = [s_sc[bi] for bi in seqs]
            s_b = [s_old[bi].astype(BF16) for bi in seqs]
            sa = _seg_multi([s_b[bi] * row(n_ref, bi).astype(BF16) for bi in seqs], ones, 1)
            put_y(_seg_multi([s_b[bi] * r_ref[bi, pl.ds(prev, 1), :].astype(BF16) for bi in seqs], ones, 1), prev)
            vk = vc_sc[i]
            for bi in seqs:
                s_new = s_old[bi] * row(w_ref, bi) + sa[bi] * row(b_ref, bi) + vk[RW_HEAD * bi:RW_HEAD * (bi + 1)]
                s_sc[bi] = s_new
                st_ref[bi, i] = s_new
            return 0

        lax.fori_loop(0, tc, step, 0, unroll=8)
        put_y(_seg_multi([s_sc[bi].astype(BF16) * r_ref[bi, tc - 1:tc, :].astype(BF16) for bi in seqs], ones, 1), tc - 1)
        for bi in seqs:
            y_ref[bi] = _unfold8(y_sc, bi, tc)

    vec = pl.BlockSpec((bsz, tc, RW_WIDTH), lambda c: (0, c, 0))
    return pl.pallas_call(
        body, name="scan_fwd", grid=(t // tc,),
        in_specs=[vec] * 5 + [pl.BlockSpec((bsz, 1, RW_HEAD, LANES), lambda c: (0, c, 0, 0))],
        out_specs=[vec, pl.BlockSpec((bsz, tc, RW_HEAD, RW_WIDTH), lambda c: (0, c, 0, 0))],
        out_shape=[_sds((bsz, t, RW_WIDTH)), _sds((bsz, t, RW_HEAD, RW_WIDTH))],
        scratch_shapes=[pltpu.VMEM((bsz, RW_HEAD, RW_WIDTH), F32), pltpu.VMEM((tc, bsz * RW_HEAD, RW_WIDTH), F32),
                        pltpu.VMEM((bsz * RW_WIDTH // LANES, tc * 8, LANES), F32)],
        compiler_params=_ARB1,
    )(r, w, k, nkk, b, vt)


def _own_head_row(x):
    first_half = lax.broadcasted_iota(jnp.int32, (1, LANES), 1) < RW_HEAD
    tiles = [jnp.where(first_half, x[2 * j:2 * j + 1, LANES * j:LANES * (j + 1)], x[2 * j + 1:2 * j + 2, LANES * j:LANES * (j + 1)])
             for j in range(RW_WIDTH // LANES)]
    return jnp.concatenate(tiles, axis=1)


def _scan_bwd(r, w, k, vt, nkk, b, st, dyt):
    bsz, t, _ = r.shape
    tc = SCAN_CHUNK
    nc = t // tc

    def body(r_ref, w_ref, k_ref, n_ref, b_ref, vt_ref, dyt_ref, st_ref, halo_ref,
             dr_ref, dw_ref, dk_ref, dv_ref, dn_ref, db_ref, g_sc, dc_sc, v8_sc, dy8_sc, *part_scs):
        c = pl.program_id(0)

        @pl.when(c == 0)
        def _():
            g_sc[...] = jnp.zeros_like(g_sc)

        ones = _seg_ones()
        diag = _diag_mask()
        has_prev = jnp.where(c == nc - 1, 0.0, 1.0)
        seqs = range(bsz)
        _outer_chunk(dyt_ref, r_ref, dc_sc, _head_expand(), seqs)
        for bi in seqs:
            v8_sc[bi] = jnp.concatenate([vt_ref[bi, 0].T] * 2, axis=1)
            dy8_sc[bi] = jnp.concatenate([dyt_ref[bi, 0].T] * 2, axis=1)
        by_head = lambda sc, bi, i: sc.at[bi][pl.ds(i, RW_HEADS, stride=SCAN_CHUNK), :][:, 0:RW_HEAD].astype(BF16)
        dw_sc, dv_sc, dn_sc, db_sc = part_scs

        def step(i, s_p):
            static = isinstance(i, int)
            row = lambda ref, bi: ref[bi, i:i + 1, :] if static else ref[bi, pl.ds(i, 1), :]
            put_row = lambda ref, bi, val: ref.__setitem__((bi, slice(i, i + 1) if static else pl.ds(i, 1), slice(None)), val)
            dr8 = [jnp.dot(by_head(dy8_sc, bi, i), st_ref[bi, i].astype(BF16), preferred_element_type=F32) for bi in seqs]
            rowb = lambda ref, bi: row(ref, bi).astype(BF16)
            sa = _seg_multi([s_p[bi].astype(BF16) * rowb(n_ref, bi) for bi in seqs], ones, 1)
            dc_all = dc_sc[i]
            dc = [dc_all[RW_HEAD * bi:RW_HEAD * (bi + 1)] for bi in seqs]
            g = [g_sc[bi] + dc[bi] for bi in seqs]
            g_b = [g[bi].astype(BF16) for bi in seqs]
            res = _seg_multi([g_b[bi] * rowb(b_ref, bi) for bi in seqs] + [g_b[bi] * rowb(k_ref, bi) for bi in seqs], ones, 1)
            dsa, dvb = res[:bsz], res[bsz:]
            for bi in seqs:
                dk8 = jnp.dot(by_head(v8_sc, bi, i), g_b[bi], preferred_element_type=F32)
                put_row(dr_ref, bi, _own_head_row(dr8[bi]))
                put_row(dk_ref, bi, _own_head_row(dk8))
                _put8(dv_sc, bi, i, _fold8(dvb[bi] * diag))
                _put8(dw_sc, bi, i, _fold8(g[bi] * s_p[bi]))
                _put8(db_sc, bi, i, _fold8(g[bi] * sa[bi]))
                _put8(dn_sc, bi, i, _fold8(s_p[bi] * dsa[bi]))
                g_sc[bi] = g[bi] * row(w_ref, bi) + dsa[bi] * row(n_ref, bi)

        def loop_step(ii, _):
            i = tc - 1 - ii
            step(i, [st_ref[bi, i - 1] for bi in seqs])
            return 0

        lax.fori_loop(0, tc - 1, loop_step, 0, unroll=5)
        step(0, [halo_ref[bi, 0] * has_prev for bi in seqs])
        for out_ref, sc in zip((dw_ref, dv_ref, dn_ref, db_ref), part_scs):
            for bi in seqs:
                out_ref[bi] = _unfold8(sc, bi, tc)

    vec = pl.BlockSpec((bsz, tc, RW_WIDTH), lambda c: (0, nc - 1 - c, 0))
    tmin = pl.BlockSpec((bsz, 1, RW_HEAD, LANES), lambda c: (0, nc - 1 - c, 0, 0))
    parts = pltpu.VMEM((bsz * RW_WIDTH // LANES, tc * 8, LANES), F32)
    heads_steps = pltpu.VMEM((bsz, LANES, LANES), F32)
    return pl.pallas_call(
        body, name="scan_bwd", grid=(nc,),
        in_specs=[vec] * 5 + [tmin, tmin,
                              pl.BlockSpec((bsz, tc, RW_HEAD, RW_WIDTH), lambda c: (0, nc - 1 - c, 0, 0)),
                              pl.BlockSpec((bsz, 1, RW_HEAD, RW_WIDTH), lambda c: (0, jnp.maximum((nc - 1 - c) * tc - 1, 0), 0, 0))],
        out_specs=[vec] * 6,
        out_shape=[_sds((bsz, t, RW_WIDTH))] * 6,
        scratch_shapes=[pltpu.VMEM((bsz, RW_HEAD, RW_WIDTH), F32), pltpu.VMEM((tc, bsz * RW_HEAD, RW_WIDTH), F32),
                        heads_steps, heads_steps] + [parts] * 4,
        compiler_params=_ARB1,
    )(r, w, k, nkk, b, vt, dyt, st, st)


TOKEN_TILE = 256
VJP_TILE = 256


def _padded_weights(wt):
    f = lambda a: a.astype(F32)
    w_in = wt["w_in"][0].astype(BF16)
    zeros = lambda r, c: jnp.zeros((r, c), F32)
    wp = jnp.concatenate([w_in[:, :MLA_COLS], jnp.zeros((D_MODEL, PM_W - MLA_COLS), BF16), w_in[:, MLA_COLS:]], axis=1)
    w_uq = f(wt["mla_w_uq"][0]).reshape(Q_LORA, MLA_HEADS, MLA_NOPE + MLA_ROPE)
    wqn = w_uq[:, :, :MLA_NOPE].reshape(Q_LORA, MLA_HEADS * MLA_NOPE)
    wqr = jnp.concatenate([w_uq[:, :, MLA_NOPE:], jnp.zeros((Q_LORA, MLA_HEADS, LANES - MLA_ROPE), F32)], axis=2)
    wqr = wqr.reshape(Q_LORA, MLA_HEADS * LANES)
    w2p = jnp.concatenate([f(wt["rw_w2"][0]), zeros(LORA, RW_WIDTH)], axis=0)
    a2p = jnp.concatenate([zeros(LORA, RW_WIDTH), f(wt["rw_a2"][0])], axis=0)
    bw = (f(wt["mla_q_norm_g"]), wqn, wqr, f(wt["mla_kv_norm_g"]), f(wt["mla_w_ukv"][0]), f(wt["rw_mu"]), f(wt["rw_w0"]),
          w2p, f(wt["rw_a0"]), a2p, f(wt["rw_k_k"]), f(wt["rw_k_a"]))
    hw = (f(wt["rw_ln_g"]), f(wt["rw_ln_b"]), f(wt["rw_r_k"]).reshape(1, RW_WIDTH), f(wt["w_out"][0]), f(wt["norm_post_g"]))
    return wp, bw, hw


def _local_step(x, positions, target, wt):
    bsz, t, _ = x.shape
    n = bsz * t
    tm = min(TOKEN_TILE, t)
    tps = t // tm
    ts = min(VJP_TILE, t)
    wp, bw, hw = _padded_weights(wt)
    wpb = wp.astype(BF16)
    g_pre = wt["norm_pre_g"].astype(F32)
    x2 = x.reshape(n, D_MODEL)
    tgt2 = target.reshape(n, D_MODEL)
    inv_freq = ROPE_THETA ** (-jnp.arange(0, MLA_ROPE, 2, dtype=F32) / MLA_ROPE)
    invf = jnp.tile(inv_freq, LANES // (MLA_ROPE // 2)).reshape(LANES, 1)
    cos, sin = _rope_tables(positions.reshape(1, n), invf, tm)

    u, pm, prw, z = _fwd_a(x2, g_pre, wpb, tm)
    qn, qr, kv, kr, r, w, k, v, nkk, b = _fwd_b(pm, prw, cos, sin, bw, tm, tps)
    b3 = lambda a: a.reshape(bsz, t, a.shape[-1])
    ym, lse = _attn_fwd(b3(qn), b3(qr), b3(kv), b3(kr))
    vt = _time_minor(b3(v))
    ys, st = _scan_fwd(b3(r), b3(w), b3(k), vt, b3(nkk), b3(b))
    (dys, dr_h, dk_h, dv_h, dym, dz, dxres, loss, d_lng, d_lnb, d_rk, d_wout, d_gpost) = _head(
        ys.reshape(n, RW_WIDTH), r, k, v, ym.reshape(n, MLA_WIDTH), z, x2, tgt2, hw, ts)
    dqn, dqr, dkv, dkr_heads = _attn_bwd(b3(qn), b3(qr), b3(kv), b3(kr), ym, b3(dym), lse)
    dr_s, dw_s, dk_s, dv_s, dn_s, db_s = _scan_bwd(b3(r), b3(w), b3(k), vt, b3(nkk), b3(b), st, _time_minor(b3(dys)))
    f2 = lambda a: a.reshape(n, a.shape[-1])
    cts = (f2(dqn), f2(dqr), f2(dkv), f2(dr_s), dr_h, f2(dw_s), f2(dk_s), dk_h, f2(dv_s), dv_h, f2(dn_s), f2(db_s))
    (dpm, dprw, dps, d_gq, d_wqn, d_wqr, d_gkv, d_wkv, d_mu, d_w0, d_w2p, d_a0, d_a2p, d_kk, d_ka) = _bwd_b(
        pm, prw, cos, sin, bw, cts, dkr_heads.reshape(MLA_HEADS, n, LANES), ts, t // ts)
    grad_x, dpb, d_gpre = _bwd_a(x2, g_pre, wpb, dpm, dprw, dps, dz, dxres, tm, tps)
    d_wp = _dw_in(u, dpb, min(1024, n), 640)

    d_w_in = jnp.concatenate([d_wp[:, :MLA_COLS], d_wp[:, PM_W:]], axis=1)
    d_w_uq = jnp.concatenate([d_wqn.reshape(Q_LORA, MLA_HEADS, MLA_NOPE),
                              d_wqr.reshape(Q_LORA, MLA_HEADS, LANES)[:, :, :MLA_ROPE]], axis=2)
    grads = {
        "norm_pre_g": d_gpre, "w_in": d_w_in[None], "mla_q_norm_g": d_gq,
        "mla_w_uq": d_w_uq.reshape(1, Q_LORA, MLA_HEADS * (MLA_NOPE + MLA_ROPE)), "mla_kv_norm_g": d_gkv,
        "mla_w_ukv": d_wkv[None], "rw_mu": d_mu, "rw_w0": d_w0, "rw_w2": d_w2p[None, :LORA], "rw_a0": d_a0,
        "rw_a2": d_a2p[None, LORA:], "rw_k_k": d_kk, "rw_k_a": d_ka, "rw_r_k": d_rk.reshape(1, RW_HEADS, RW_HEAD),
        "rw_ln_g": d_lng, "rw_ln_b": d_lnb, "w_out": d_wout[None], "norm_post_g": d_gpost,
    }
    return loss, grad_x.reshape(bsz, t, D_MODEL), grads


_MESH = pl.DeviceIdType.MESH


def _gather_shards(shards):
    na = len(shards)

    def body(*refs):
        x_refs, out_refs = refs[:na], refs[na:2 * na]
        send_sems, recv_sems, local_sems = refs[2 * na:]
        x, y, c = lax.axis_index("x"), lax.axis_index("y"), lax.axis_index("c")
        me, sibling = (x, y, c), (x, y, 1 - c)
        chips = [(1 - x, y), (x, 1 - y), (1 - x, 1 - y)]
        arrays = range(na)

        def slot(a, px, py, pc):
            return out_refs[a].at[4 * px + 2 * py + pc]

        def copy(k, a, block, to, src=None):
            return pltpu.make_async_remote_copy(
                src_ref=slot(a, *block) if src is None else src, dst_ref=slot(a, *block),
                send_sem=send_sems.at[k, a], recv_sem=recv_sems.at[k, a], device_id=to, device_id_type=_MESH)

        mine = [pltpu.make_async_copy(x_refs[a], slot(a, *me), local_sems.at[a]) for a in arrays]
        for cp in mine:
            cp.start()
        first = [copy(0, a, me, sibling, src=x_refs[a]) for a in arrays]
        first += [copy(1 + j, a, me, (*chip, c), src=x_refs[a]) for j, chip in enumerate(chips) for a in arrays]
        for cp in first:
            cp.start()
        passed = []
        for j, chip in enumerate(chips):
            for a in arrays:
                copy(1 + j, a, (*chip, c), me).wait_recv()
                passed.append(copy(4 + j, a, (*chip, c), sibling))
                passed[-1].start()
        for a in arrays:
            copy(0, a, sibling, me).wait_recv()
        for j, chip in enumerate(chips):
            for a in arrays:
                copy(4 + j, a, (*chip, 1 - c), me).wait_recv()
        for cp in first + passed:
            cp.wait_send()
        for cp in mine:
            cp.wait()

    vmem = pl.BlockSpec(memory_space=pltpu.VMEM)
    return pl.pallas_call(
        body, name="gather_shards",
        out_shape=[_sds((N_DEV,) + a.shape, a.dtype) for a in shards],
        in_specs=[vmem] * na, out_specs=[vmem] * na,
        scratch_shapes=[pltpu.SemaphoreType.DMA((7, na)), pltpu.SemaphoreType.DMA((7, na)), pltpu.SemaphoreType.DMA((na,))],
    )(*shards)


SMALL_LANES = SMALL_N + LANES


def _exchange_grads(big_blocks, small_grads, loss_tile):
    nb = len(big_blocks)
    ns = len(small_grads)

    def body(*refs):
        big, small, loss_ref = refs[:nb], refs[nb:nb + ns], refs[nb + ns]
        rbig, rsmall = refs[nb + ns + 1:2 * nb + ns + 1], refs[2 * nb + ns + 1]
        send_b, recv_b, send_s, recv_s, local_sems, row_sc = refs[2 * nb + ns + 2:]
        x, y, c = lax.axis_index("x"), lax.axis_index("y"), lax.axis_index("c")
        me_lin = 4 * x + 2 * y + c
        mine = [pltpu.make_async_copy(big[j].at[me_lin], rbig[j].at[0], local_sems.at[j]) for j in range(nb)]
        for cp in mine:
            cp.start()
        off = 0
        for ref, (_, cnt) in zip(small, SMALL):
            row_sc[:, off:off + cnt] = ref[...]
            off += cnt
        row_sc[:, off:off + LANES] = loss_ref[0:1, :]
        rsmall[me_lin] = row_sc[...]
        copies = []
        for k in range(1, N_DEV):
            px, py, pc = x ^ (k >> 2), y ^ ((k >> 1) & 1), c ^ (k & 1)
            peer = (px, py, pc)
            for j in range(nb):
                copies.append(pltpu.make_async_remote_copy(
                    src_ref=big[j].at[4 * px + 2 * py + pc], dst_ref=rbig[j].at[k],
                    send_sem=send_b.at[k - 1, j], recv_sem=recv_b.at[k - 1, j], device_id=peer, device_id_type=_MESH))
            copies.append(pltpu.make_async_remote_copy(
                src_ref=row_sc, dst_ref=rsmall.at[me_lin],
                send_sem=send_s.at[k - 1], recv_sem=recv_s.at[k - 1], device_id=peer, device_id_type=_MESH))
        for cp in copies:
            cp.start()
        for cp in copies:
            cp.wait_recv()
        for cp in copies:
            cp.wait_send()
        for cp in mine:
            cp.wait()

    hbm, vmem = pl.BlockSpec(memory_space=pl.ANY), pl.BlockSpec(memory_space=pltpu.VMEM)
    return pl.pallas_call(
        body, name="exchange_grads",
        out_shape=[_sds(a.shape, a.dtype) for a in big_blocks] + [_sds((N_DEV, 1, SMALL_LANES))],
        in_specs=[hbm] * nb + [vmem] * (ns + 1),
        out_specs=[hbm] * nb + [vmem],
        scratch_shapes=[pltpu.SemaphoreType.DMA((N_DEV - 1, nb)), pltpu.SemaphoreType.DMA((N_DEV - 1, nb)),
                        pltpu.SemaphoreType.DMA((N_DEV - 1,)), pltpu.SemaphoreType.DMA((N_DEV - 1,)),
                        pltpu.SemaphoreType.DMA((nb,)), pltpu.VMEM((1, SMALL_LANES), F32)],
    )(*big_blocks, *small_grads, loss_tile)


def _adamw_math(w, g, m, v):
    m = ADAM_B1 * m + (1.0 - ADAM_B1) * g
    v = ADAM_B2 * v + (1.0 - ADAM_B2) * (g * g)
    m_hat = m / (1.0 - ADAM_B1 ** ADAM_STEP)
    v_hat = v / (1.0 - ADAM_B2 ** ADAM_STEP)
    return -ADAM_LR * (m_hat / (jnp.sqrt(v_hat) + ADAM_EPS) + ADAM_WD * w), m, v


def _reduce_adamw(name, parts, w, m, v, row_blocks):
    _, rows, cols = parts.shape
    rb = rows // row_blocks

    def body(p_ref, w_ref, m_ref, v_ref, g_out, d_out, m_out, v_out):
        g = p_ref[0].astype(F32)
        for s in range(1, N_DEV):
            g = g + p_ref[s].astype(F32)
        g_out[0] = g
        d_out[0], m_out[0], v_out[0] = _adamw_math(w_ref[0], g, m_ref[0], v_ref[0])

    blk = pl.BlockSpec((1, rb, cols), lambda i: (0, i, 0))
    return pl.pallas_call(
        body, name="reduce_adamw_" + name, grid=(row_blocks,),
        in_specs=[pl.BlockSpec((N_DEV, rb, cols), lambda i: (0, i, 0)), blk, blk, blk],
        out_specs=[blk] * 4, out_shape=[_sds((1, rows, cols))] * 4,
        compiler_params=_ARB1,
    )(parts, w, m, v)


def _reduce_adamw_small(rows, ws, ms, vs):
    ns = len(SMALL)

    def body(r_ref, *refs):
        w_refs, m_refs, v_refs, outs = refs[:ns], refs[ns:2 * ns], refs[2 * ns:3 * ns], refs[3 * ns:]
        total = r_ref[0]
        for s in range(1, N_DEV):
            total = total + r_ref[s]
        off = 0
        for j, (_, cnt) in enumerate(SMALL):
            g = total[:, off:off + cnt]
            off += cnt
            outs[4 * j][...] = g
            outs[4 * j + 1][...], outs[4 * j + 2][...], outs[4 * j + 3][...] = _adamw_math(
                w_refs[j][...], g, m_refs[j][...], v_refs[j][...])
        outs[4 * ns][...] = total[:, off:off + LANES]

    vmem = pl.BlockSpec(memory_space=pltpu.VMEM)
    return pl.pallas_call(
        body, name="reduce_adamw_small",
        in_specs=[vmem] * (1 + 3 * ns), out_specs=[vmem] * (4 * ns + 1),
        out_shape=[_sds((1, cnt)) for _, cnt in SMALL for _ in range(4)] + [_sds((1, LANES))],
    )(rows, *ws, *ms, *vs)


def _shard_blocks(name, full):
    a = full[0]
    rows, cols = a.shape
    if name == "w_out":
        return a.reshape(N_DEV, rows // N_DEV, cols)
    return a.reshape(rows, N_DEV, cols // N_DEV).transpose(1, 0, 2)


def _unshard(name, blocks):
    _, rows, cols = blocks.shape
    if name == "w_out":
        return blocks.reshape(1, N_DEV * rows, cols)
    return blocks.transpose(1, 0, 2).reshape(1, rows, N_DEV * cols)


def kernel(x, positions, norm_pre_g, w_in, mla_q_norm_g, mla_w_uq, mla_kv_norm_g, mla_w_ukv, rw_mu, rw_w0, rw_w2, rw_a0, rw_a2, rw_k_k, rw_k_a, rw_r_k, rw_ln_g, rw_ln_b, w_out, norm_post_g, loss_target, m_norm_pre_g, m_w_in, m_mla_q_norm_g, m_mla_w_uq, m_mla_kv_norm_g, m_mla_w_ukv, m_rw_mu, m_rw_w0, m_rw_w2, m_rw_a0, m_rw_a2, m_rw_k_k, m_rw_k_a, m_rw_r_k, m_rw_ln_g, m_rw_ln_b, m_w_out, m_norm_post_g, v_norm_pre_g, v_w_in, v_mla_q_norm_g, v_mla_w_uq, v_mla_kv_norm_g, v_mla_w_ukv, v_rw_mu, v_rw_w0, v_rw_w2, v_rw_a0, v_rw_a2, v_rw_k_k, v_rw_k_a, v_rw_r_k, v_rw_ln_g, v_rw_ln_b, v_w_out, v_norm_post_g):
    given = dict(locals())
    w = {nm: given[nm] for nm in WEIGHTS}
    mom = {nm: given["m_" + nm] for nm in WEIGHTS}
    var = {nm: given["v_" + nm] for nm in WEIGHTS}
    sharded = list(SHARDED)

    gathered = _gather_shards([w[nm][0].astype(BF16) for nm in sharded])
    full = dict(w)
    for nm, blocks in zip(sharded, gathered):
        full[nm] = _unshard(nm, blocks)

    loss_part, grad_x, grads = _local_step(x, positions, loss_target, full)

    small_names = [nm for nm, _ in SMALL]
    row = lambda a: a.reshape(1, -1)
    got = _exchange_grads([_shard_blocks(nm, grads[nm]).astype(BF16) for nm in sharded],
                          [row(grads[nm]) for nm in small_names], loss_part)
    new = {}
    for nm, parts in zip(sharded, got[:-1]):
        new[nm] = _reduce_adamw(nm, parts, w[nm], mom[nm], var[nm], 4 if nm == "w_in" else 1)
    res = _reduce_adamw_small(got[-1], [row(w[nm]) for nm in small_names], [row(mom[nm]) for nm in small_names],
                              [row(var[nm]) for nm in small_names])
    for j, nm in enumerate(small_names):
        new[nm] = tuple(a.reshape(w[nm].shape) for a in res[4 * j:4 * j + 4])
    loss = res[-1][0, 0]
    return (loss, grad_x, *[new[nm][j] for j in range(4) for nm in WEIGHTS])
```

```python
import functools

import jax
import jax.numpy as jnp
from jax import lax
from jax.experimental import pallas as pl
from jax.experimental.pallas import tpu as pltpu

F32 = jnp.float32
BF16 = jnp.bfloat16

D_MODEL = 1024
MLA_HEADS = 4
MLA_NOPE = 128
MLA_ROPE = 64
MLA_V = 128
MLA_WIDTH = MLA_HEADS * MLA_V
Q_LORA = 256
KV_LORA = 128
ROPE_THETA = 10000.0
RW_HEAD = 64
RW_WIDTH = 512
RW_HEADS = RW_WIDTH // RW_HEAD
LORA = 64
RW_COLS = 3 * RW_WIDTH + 2 * LORA
MLA_COLS = Q_LORA + KV_LORA + MLA_ROPE
D_IN = MLA_COLS + RW_COLS + D_MODEL
RW_GN_EPS = 64e-5
NORM_EPS = 1e-6
ATT_SCALE = (MLA_NOPE + MLA_ROPE) ** -0.5
ADAM_LR, ADAM_B1, ADAM_B2, ADAM_EPS, ADAM_WD, ADAM_STEP = 0.001, 0.9, 0.999, 1e-08, 0.01, 10
N_DEV = 8
LANES = 128
MXU = 256

PM_W = 512
WP_COLS = PM_W + RW_COLS + D_MODEL
RW_PIECES = ((0, 512), (512, 1024), (1024, 1536), (1536, 1664))

SHARDED = ("w_in", "mla_w_uq", "mla_w_ukv", "rw_w2", "rw_a2", "w_out")
SMALL = (("norm_pre_g", 1024), ("mla_q_norm_g", 256), ("mla_kv_norm_g", 128), ("rw_mu", 1664), ("rw_w0", 512),
         ("rw_a0", 512), ("rw_k_k", 512), ("rw_k_a", 512), ("rw_r_k", 512), ("rw_ln_g", 512), ("rw_ln_b", 512),
         ("norm_post_g", 1024))
SMALL_N = sum(n for _, n in SMALL)
WEIGHTS = ("norm_pre_g", "w_in", "mla_q_norm_g", "mla_w_uq", "mla_kv_norm_g", "mla_w_ukv", "rw_mu", "rw_w0", "rw_w2",
           "rw_a0", "rw_a2", "rw_k_k", "rw_k_a", "rw_r_k", "rw_ln_g", "rw_ln_b", "w_out", "norm_post_g")


def _seg_ones():
    r = lax.broadcasted_iota(jnp.int32, (MXU, MXU), 0) >> 6
    c = lax.broadcasted_iota(jnp.int32, (MXU, MXU), 1) >> 6
    return jnp.where(r == c, 1.0, 0.0).astype(BF16)


def _seg_dot(x, ones, passes):
    parts, rem = [], x
    for p in range(passes):
        hb = rem.astype(BF16)
        parts.append(hb)
        if p + 1 < passes:
            rem = rem - hb.astype(F32)
    outs = []
    for j in range(x.shape[1] // MXU):
        acc = None
        for hb in parts:
            d = jnp.dot(hb[:, MXU * j:MXU * (j + 1)], ones, preferred_element_type=F32)
            acc = d if acc is None else acc + d
        outs.append(acc)
    return outs[0] if len(outs) == 1 else jnp.concatenate(outs, axis=1)


def _seg_multi(xs, ones, passes):
    his = [x.astype(BF16) for x in xs]
    hi = jnp.concatenate(his, axis=0)
    if passes == 2:
        lo = jnp.concatenate([(x - h.astype(F32)).astype(BF16) for x, h in zip(xs, his)], axis=0)
        rhs = jnp.concatenate([ones, ones], axis=0)
    halves = []
    for j in range(hi.shape[1] // MXU):
        sl = slice(MXU * j, MXU * (j + 1))
        if passes == 2:
            halves.append(jnp.dot(jnp.concatenate([hi[:, sl], lo[:, sl]], axis=1), rhs, preferred_element_type=F32))
        else:
            halves.append(jnp.dot(hi[:, sl], ones, preferred_element_type=F32))
    full = jnp.concatenate(halves, axis=1)
    m = xs[0].shape[0]
    return [full[m * i:m * (i + 1)] for i in range(len(xs))]


@jax.custom_vjp
def _segsum(x):
    return _seg_dot(x, _seg_ones(), 2)


_segsum.defvjp(lambda x: (_segsum(x), None), lambda _, g: (_segsum(g),))


@jax.custom_vjp
def _bdot(a, w):
    return jnp.dot(a.astype(BF16), w.astype(BF16), preferred_element_type=F32)


def _bdot_fwd(a, w):
    return _bdot(a, w), (a, w)


def _bdot_bwd(res, g):
    a, w = res
    gb = g.astype(BF16)
    da = lax.dot_general(gb, w.astype(BF16), (((1,), (1,)), ((), ())), preferred_element_type=F32)
    dw = lax.dot_general(a.astype(BF16), gb, (((0,), (0,)), ((), ())), preferred_element_type=F32)
    return da, dw


_bdot.defvjp(_bdot_fwd, _bdot_bwd)


def _rot_impl(x):
    w = x.shape[1]
    lane = lax.broadcasted_iota(jnp.int32, x.shape, 1)
    return jnp.where((lane & 63) < 32, -pltpu.roll(x, w - 32, 1), pltpu.roll(x, 32, 1))


@jax.custom_vjp
def _rot(x):
    return _rot_impl(x)


_rot.defvjp(lambda x: (_rot_impl(x), None), lambda _, g: (-_rot_impl(g),))


def _rms(x, g):
    return x * lax.rsqrt(jnp.mean(x * x, axis=-1, keepdims=True) + NORM_EPS) * g


def _shift_rows(p, prev_row):
    row = lax.broadcasted_iota(jnp.int32, p.shape, 0)
    return jnp.where(row == 0, prev_row, pltpu.roll(p, 1, 0))


def _unshift_rows(g, next_row):
    row = lax.broadcasted_iota(jnp.int32, g.shape, 0)
    return jnp.where(row == g.shape[0] - 1, next_row, pltpu.roll(g, g.shape[0] - 1, 0))


def _f_mla(cq, ckv, kr, cos, sin, g_q, wqn, wqr, g_kv, wkv):
    qn = _rms(cq, g_q)
    q_nope = _bdot(qn, wqn)
    q_r = _bdot(qn, wqr)
    cos4 = jnp.concatenate([cos] * MLA_HEADS, axis=1)
    sin4 = jnp.concatenate([sin] * MLA_HEADS, axis=1)
    q_rope = q_r * cos4 + _rot(q_r) * sin4
    kv = _bdot(_rms(ckv, g_kv), wkv)
    k_rope = kr * cos + _rot(kr) * sin
    return q_nope, q_rope, kv, k_rope


def _f_rw(pr, pk, pv, pt, sr, sk, sv, st, mu_r, mu_k, mu_v, mu_t, w0, w2p, a0, a2p, k_k, k_a):
    r = pr + (sr - pr) * mu_r
    k = pk + (sk - pk) * mu_k
    v = pv + (sv - pv) * mu_v
    t = pt + (st - pt) * mu_t
    nwl = -(w0 + _bdot(jnp.tanh(t), w2p))
    softplus = jnp.maximum(nwl, 0.0) + jnp.log(1.0 + jnp.exp(-jnp.abs(nwl)))
    decay = jnp.exp(-jnp.exp(-softplus - 0.5))
    a = jax.nn.sigmoid(a0 + _bdot(t, a2p))
    kk = k * k_k
    kk = kk / jnp.maximum(jnp.sqrt(_segsum(kk * kk)), 1e-12)
    k2 = k * (1.0 + (a - 1.0) * k_a)
    return r, decay, k2, v, -kk, kk * a


def _f_head(ys, r, k, v, ym, z1, z2, x, tgt, ln_g, ln_b, r_k, w1, w2, g_post):
    inv = 1.0 / RW_HEAD
    yc = ys - _segsum(ys) * inv
    var = _segsum(yc * yc) * inv
    y = yc * lax.rsqrt(var + RW_GN_EPS) * ln_g + ln_b
    y_rw = y + _segsum(r * k * r_k) * v
    c1 = ym * (z1 * jax.nn.sigmoid(z1))
    c2 = y_rw * (z2 * jax.nn.sigmoid(z2))
    out = _bdot(c1, w1) + _bdot(c2, w2)
    err = x + _rms(out, g_post) - tgt
    per_row = jnp.sum(err * err, axis=1, keepdims=True)
    return jnp.sum(per_row, axis=0, keepdims=True) * (0.5 / D_MODEL)


def _rows(tm, width):
    return pl.BlockSpec((tm, width), lambda i: (i, 0))


def _whole(shape):
    zeros = (0,) * len(shape)
    return pl.BlockSpec(shape, lambda i: zeros)


def _sds(shape, dtype=F32):
    return jax.ShapeDtypeStruct(shape, dtype)


_ARB1 = pltpu.CompilerParams(dimension_semantics=("arbitrary",))


def _acc(ref, val, first):
    @pl.when(first)
    def _():
        ref[...] = val

    @pl.when(jnp.logical_not(first))
    def _():
        ref[...] += val


def _fwd_a(x2, g_pre, wp, tm):
    n = x2.shape[0]

    def body(x_ref, g_ref, w_ref, ut_ref, pm_ref, prw_ref, z_ref):
        u = _rms(x_ref[...], g_ref[...])
        ub = u.astype(BF16)
        ut_ref[...] = u.T.astype(BF16)
        pm_ref[...] = jnp.dot(ub, w_ref[:, 0:PM_W], preferred_element_type=F32)
        prw_ref[...] = jnp.dot(ub, w_ref[:, PM_W:PM_W + RW_COLS], preferred_element_type=F32)
        z_ref[...] = jnp.dot(ub, w_ref[:, PM_W + RW_COLS:WP_COLS], preferred_element_type=F32)

    return pl.pallas_call(
        body, name="fwd_a", grid=(n // tm,),
        in_specs=[_rows(tm, D_MODEL), _whole((1, D_MODEL)), _whole((D_MODEL, WP_COLS))],
        out_specs=[pl.BlockSpec((D_MODEL, tm), lambda i: (0, i)), _rows(tm, PM_W), _rows(tm, RW_COLS), _rows(tm, D_MODEL)],
        out_shape=[_sds((D_MODEL, n), BF16), _sds((n, PM_W)), _sds((n, RW_COLS)), _sds((n, D_MODEL))],
        compiler_params=_ARB1,
    )(x2, g_pre, wp)


def _rope_tables(pos_row, invf_col, tm):
    n = pos_row.shape[1]

    def body(p_ref, f_ref, c_ref, s_ref):
        distinct = MLA_ROPE // 2
        ang = f_ref[0:distinct, :] * p_ref[...].astype(F32)
        c_ref[...] = jnp.concatenate([jnp.cos(ang)] * (LANES // distinct), axis=0).T
        s_ref[...] = jnp.concatenate([jnp.sin(ang)] * (LANES // distinct), axis=0).T

    return pl.pallas_call(
        body, name="rope_tables", grid=(n // tm,),
        in_specs=[pl.BlockSpec((1, tm), lambda i: (0, i)), _whole((LANES, 1))],
        out_specs=[_rows(tm, LANES), _rows(tm, LANES)],
        out_shape=[_sds((n, LANES)), _sds((n, LANES))],
        compiler_params=_ARB1,
    )(pos_row, invf_col)


_B_WEIGHT_SHAPES = ((1, Q_LORA), (Q_LORA, 512), (Q_LORA, 512), (1, KV_LORA), (KV_LORA, 1024), (1, RW_COLS), (1, RW_WIDTH),
                    (LANES, RW_WIDTH), (1, RW_WIDTH), (LANES, RW_WIDTH), (1, RW_WIDTH), (1, RW_WIDTH))


def _halo_prev(tm):
    return pl.BlockSpec((8, RW_COLS), lambda i: (jnp.maximum(i * (tm // 8) - 1, 0), 0))


def _b_operands(pm_ref, prw_ref, halo_ref, wrefs, tile, tiles_per_seq):
    g_q, wqn, wqr, g_kv, wkv, mu, w0, w2p, a0, a2p, k_k, k_a = wrefs
    mla_in = (pm_ref[:, 0:Q_LORA], pm_ref[:, Q_LORA:Q_LORA + KV_LORA], pm_ref[:, Q_LORA + KV_LORA:PM_W])
    mla_w = (g_q[...], wqn[...], wqr[...], g_kv[...], wkv[...])
    keep = jnp.where(tile % tiles_per_seq == 0, 0.0, 1.0)
    prev = halo_ref[7:8, :] * keep
    ps = tuple(prw_ref[:, a:b] for a, b in RW_PIECES)
    ss = tuple(_shift_rows(p, prev[:, a:b]) for p, (a, b) in zip(ps, RW_PIECES))
    rw_w = tuple(mu[:, a:b] for a, b in RW_PIECES) + (w0[...], w2p[...], a0[...], a2p[...], k_k[...], k_a[...])
    return mla_in, mla_w, ps + ss, rw_w


def _fwd_b(pm, prw, cos, sin, bw, tm, tiles_per_seq):
    n = pm.shape[0]

    def body(pm_ref, prw_ref, halo_ref, cos_ref, sin_ref, *refs):
        wrefs, outs = refs[:12], refs[12:]
        mla_in, mla_w, rw_in, rw_w = _b_operands(pm_ref, prw_ref, halo_ref, wrefs, pl.program_id(0), tiles_per_seq)
        res = _f_mla(*mla_in, cos_ref[...], sin_ref[...], *mla_w) + _f_rw(*rw_in, *rw_w)
        for o_ref, val in zip(outs, res):
            o_ref[...] = val.astype(o_ref.dtype)

    widths = (512, 512, 1024, LANES) + (RW_WIDTH,) * 6
    return pl.pallas_call(
        body, name="fwd_b", grid=(n // tm,),
        in_specs=[_rows(tm, PM_W), _rows(tm, RW_COLS), _halo_prev(tm), _rows(tm, LANES), _rows(tm, LANES)]
        + [_whole(s) for s in _B_WEIGHT_SHAPES],
        out_specs=[_rows(tm, w) for w in widths],
        out_shape=[_sds((n, w), BF16 if j < 4 else F32) for j, w in enumerate(widths)],
        compiler_params=_ARB1,
    )(pm, prw, prw, cos, sin, *bw)


def _bwd_b(pm, prw, cos, sin, bw, cts, dkr_heads, tm, tiles_per_seq):
    n = pm.shape[0]

    ct_widths = (512, 512, 1024) + (RW_WIDTH,) * 9
    n_ct = len(ct_widths)

    def body(pm_ref, prw_ref, halo_ref, cos_ref, sin_ref, *refs):
        wrefs, ct_refs, dkr_ref = refs[:12], refs[12:12 + n_ct], refs[12 + n_ct]
        dpm_ref, dprw_ref, dps_ref = refs[13 + n_ct:16 + n_ct]
        wg_refs = refs[16 + n_ct:]
        tile = pl.program_id(0)
        first = tile == 0
        mla_in, mla_w, rw_in, rw_w = _b_operands(pm_ref, prw_ref, halo_ref, wrefs, tile, tiles_per_seq)
        cos, sin = cos_ref[...], sin_ref[...]
        ct = [r[...] for r in ct_refs]
        _, vjp_mla = jax.vjp(lambda *a: _f_mla(*a[:3], cos, sin, *a[3:]), *mla_in, *mla_w)
        dkr = dkr_ref[0] + dkr_ref[1] + dkr_ref[2] + dkr_ref[3]
        d_mla = vjp_mla((ct[0], ct[1], ct[2], dkr))
        dpm_ref[:, 0:Q_LORA] = d_mla[0]
        dpm_ref[:, Q_LORA:Q_LORA + KV_LORA] = d_mla[1]
        dpm_ref[:, Q_LORA + KV_LORA:PM_W] = d_mla[2]
        _, vjp_rw = jax.vjp(_f_rw, *rw_in, *rw_w)
        d_rw = vjp_rw((ct[3] + ct[4], ct[5], ct[6] + ct[7], ct[8] + ct[9], ct[10], ct[11]))
        for j, (a, b) in enumerate(RW_PIECES):
            dprw_ref[:, a:b] = d_rw[j]
            dps_ref[:, a:b] = d_rw[4 + j]
        g_q, wqn, wqr, g_kv, wkv, mu, w0, w2p, a0, a2p, k_k, k_a = wg_refs
        for ref, val in zip((g_q, wqn, wqr, g_kv, wkv), d_mla[3:]):
            _acc(ref, val, first)
        for j, (a, b) in enumerate(RW_PIECES):
            _acc(mu.at[:, a:b], d_rw[8 + j], first)
        for ref, val in zip((w0, w2p, a0, a2p, k_k, k_a), d_rw[12:]):
            _acc(ref, val, first)

    return pl.pallas_call(
        body, name="bwd_b", grid=(n // tm,),
        in_specs=[_rows(tm, PM_W), _rows(tm, RW_COLS), _halo_prev(tm), _rows(tm, LANES), _rows(tm, LANES)]
        + [_whole(s) for s in _B_WEIGHT_SHAPES] + [_rows(tm, w) for w in ct_widths]
        + [pl.BlockSpec((MLA_HEADS, tm, LANES), lambda i: (0, i, 0))],
        out_specs=[_rows(tm, PM_W), _rows(tm, RW_COLS), _rows(tm, RW_COLS)] + [_whole(s) for s in _B_WEIGHT_SHAPES],
        out_shape=[_sds((n, PM_W)), _sds((n, RW_COLS)), _sds((n, RW_COLS))] + [_sds(s) for s in _B_WEIGHT_SHAPES],
        compiler_params=_ARB1,
    )(pm, prw, prw, cos, sin, *bw, *cts, dkr_heads)


def _head(ys, r, k, v, ym, z, x2, tgt, hw, tm):
    n = x2.shape[0]
    h_shapes = ((1, RW_WIDTH), (1, RW_WIDTH), (1, RW_WIDTH), (D_MODEL, D_MODEL), (1, D_MODEL))

    def body(ys_ref, r_ref, k_ref, v_ref, ym_ref, z_ref, x_ref, t_ref, lng, lnb, rk, wout, gpost,
             dys_ref, dr_ref, dk_ref, dv_ref, dym_ref, dz_ref, dx_ref, loss_ref, dlng, dlnb, drk, dwout, dgpost):
        first = pl.program_id(0) == 0
        tgt_v = t_ref[...]
        args = (ys_ref[...], r_ref[...], k_ref[...], v_ref[...], ym_ref[...], z_ref[:, 0:MLA_WIDTH], z_ref[:, MLA_WIDTH:D_MODEL],
                x_ref[...], lng[...], lnb[...], rk[...], wout[0:MLA_WIDTH, :], wout[MLA_WIDTH:D_MODEL, :], gpost[...])
        loss, vjp = jax.vjp(lambda *a: _f_head(*a[:8], tgt_v, *a[8:]), *args)
        d = vjp(jnp.ones((1, 1), F32))
        dys_ref[...] = d[0]
        dr_ref[...] = d[1]
        dk_ref[...] = d[2]
        dv_ref[...] = d[3]
        dym_ref[...] = d[4].astype(BF16)
        dz_ref[:, 0:MLA_WIDTH] = d[5]
        dz_ref[:, MLA_WIDTH:D_MODEL] = d[6]
        dx_ref[...] = d[7]
        _acc(loss_ref, jnp.broadcast_to(loss, (8, LANES)), first)
        _acc(dlng, d[8], first)
        _acc(dlnb, d[9], first)
        _acc(drk, d[10], first)
        _acc(dwout.at[0:MLA_WIDTH, :], d[11], first)
        _acc(dwout.at[MLA_WIDTH:D_MODEL, :], d[12], first)
        _acc(dgpost, d[13], first)

    widths = (RW_WIDTH,) * 4 + (MLA_WIDTH, D_MODEL, D_MODEL)
    return pl.pallas_call(
        body, name="head", grid=(n // tm,),
        in_specs=[_rows(tm, RW_WIDTH)] * 4 + [_rows(tm, MLA_WIDTH), _rows(tm, D_MODEL), _rows(tm, D_MODEL), _rows(tm, D_MODEL)]
        + [_whole(s) for s in h_shapes],
        out_specs=[_rows(tm, w) for w in widths] + [_whole((8, LANES))] + [_whole(s) for s in h_shapes],
        out_shape=[_sds((n, w), BF16 if j == 4 else F32) for j, w in enumerate(widths)] + [_sds((8, LANES))]
        + [_sds(s) for s in h_shapes],
        compiler_params=_ARB1,
    )(ys, r, k, v, ym, z, x2, tgt, *hw)


def _halo_next(tm, n):
    last = n // 8 - 1
    return pl.BlockSpec((8, RW_COLS), lambda i: (jnp.minimum((i + 1) * (tm // 8), last), 0))


def _bwd_a(x2, g_pre, wp, dpm, dprw, dps, dz, dxres, tm, tiles_per_seq):
    n = x2.shape[0]
    nt_dims = (((1,), (1,)), ((), ()))

    def body(x_ref, g_ref, w_ref, dpm_ref, dprw_ref, dps_ref, nxt_ref, dz_ref, dxres_ref, gx_ref, dpb_ref, dg_ref):
        tile = pl.program_id(0)
        keep = jnp.where((tile + 1) % tiles_per_seq == 0, 0.0, 1.0)
        dprw_v = dprw_ref[...] + _unshift_rows(dps_ref[...], nxt_ref[0:1, :] * keep)
        dpm_b, dprw_b, dz_b = dpm_ref[...].astype(BF16), dprw_v.astype(BF16), dz_ref[...].astype(BF16)
        dpb_ref[:, 0:PM_W] = dpm_b
        dpb_ref[:, PM_W:PM_W + RW_COLS] = dprw_b
        dpb_ref[:, PM_W + RW_COLS:WP_COLS] = dz_b
        du = (lax.dot_general(dpm_b, w_ref[:, 0:PM_W], nt_dims, preferred_element_type=F32)
              + lax.dot_general(dprw_b, w_ref[:, PM_W:PM_W + RW_COLS], nt_dims, preferred_element_type=F32)
              + lax.dot_general(dz_b, w_ref[:, PM_W + RW_COLS:WP_COLS], nt_dims, preferred_element_type=F32))
        x = x_ref[...]
        xhat = x * lax.rsqrt(jnp.mean(x * x, axis=-1, keepdims=True) + NORM_EPS)
        dxn = du * g_ref[...]
        dx = (dxn - xhat * jnp.mean(dxn * xhat, axis=-1, keepdims=True)) * lax.rsqrt(jnp.mean(x * x, axis=-1, keepdims=True) + NORM_EPS)
        gx_ref[...] = dx + dxres_ref[...]
        _acc(dg_ref, jnp.sum(du * xhat, axis=0, keepdims=True), tile == 0)

    return pl.pallas_call(
        body, name="bwd_a", grid=(n // tm,),
        in_specs=[_rows(tm, D_MODEL), _whole((1, D_MODEL)), _whole((D_MODEL, WP_COLS)), _rows(tm, PM_W), _rows(tm, RW_COLS),
                  _rows(tm, RW_COLS), _halo_next(tm, n), _rows(tm, D_MODEL), _rows(tm, D_MODEL)],
        out_specs=[_rows(tm, D_MODEL), _rows(tm, WP_COLS), _whole((1, D_MODEL))],
        out_shape=[_sds((n, D_MODEL)), _sds((n, WP_COLS), BF16), _sds((1, D_MODEL))],
        compiler_params=_ARB1,
    )(x2, g_pre, wp, dpm, dprw, dps, dps, dz, dxres)


def _dw_in(ut, dpb, tk, tn):
    n = ut.shape[1]
    steps = n // tk

    def body(u_ref, d_ref, o_ref, acc_sc):
        k = pl.program_id(1)
        _acc(acc_sc, jnp.dot(u_ref[...], d_ref[...], preferred_element_type=F32), k == 0)

        @pl.when(k == steps - 1)
        def _():
            o_ref[...] = acc_sc[...].astype(BF16)

    return pl.pallas_call(
        body, name="dw_in", grid=(WP_COLS // tn, steps),
        in_specs=[pl.BlockSpec((D_MODEL, tk), lambda j, k: (0, k)), pl.BlockSpec((tk, tn), lambda j, k: (k, j))],
        out_specs=pl.BlockSpec((D_MODEL, tn), lambda j, k: (0, j)),
        out_shape=_sds((D_MODEL, WP_COLS), BF16),
        scratch_shapes=[pltpu.VMEM((D_MODEL, tn), F32)],
        compiler_params=pltpu.CompilerParams(dimension_semantics=("arbitrary", "arbitrary")),
    )(ut, dpb)


ATT_BLK = 256
_NT = (((1,), (1,)), ((), ()))
_TN = (((0,), (0,)), ((), ()))


def _causal(q0, k0, blk, blk_k=None):
    blk_k = blk if blk_k is None else blk_k
    row = q0 + lax.broadcasted_iota(jnp.int32, (blk, blk_k), 0)
    col = k0 + lax.broadcasted_iota(jnp.int32, (blk, blk_k), 1)
    return row >= col


def _attn_fwd(qn, qr, kv, kr):
    bsz, t, _ = qn.shape
    blk = min(ATT_BLK, t)

    heads = range(MLA_HEADS)

    def body(qn_ref, qr_ref, kv_ref, kr_ref, o_ref, lse_ref):
        qi = pl.program_id(1)
        q = [jnp.concatenate([qn_ref[:, LANES * h:LANES * (h + 1)], qr_ref[:, LANES * h:LANES * (h + 1)]], axis=1) for h in heads]
        lower = _causal(0, 0, blk)

        def kv_step(j, carry, diagonal):
            ks = pl.multiple_of(j * blk, blk)
            k_rope = kr_ref[pl.ds(ks, blk), :]
            def score(h):
                k = jnp.concatenate([kv_ref[pl.ds(ks, blk), 2 * LANES * h:2 * LANES * h + LANES], k_rope], axis=1)
                return lax.dot_general(q[h], k, _NT, preferred_element_type=F32)

            out = []
            nxt = score(0)
            for h in heads:
                s = nxt * ATT_SCALE
                if h + 1 < MLA_HEADS:
                    nxt = score(h + 1)
                m, l, acc = carry[h]
                if diagonal:
                    s = jnp.where(lower, s, -1e30)
                m_new = jnp.maximum(m, jnp.max(s, axis=1, keepdims=True))
                alpha = jnp.exp(m - m_new)
                p = jnp.exp(s - m_new)
                l = alpha * l + jnp.sum(p, axis=1, keepdims=True)
                v = kv_ref[pl.ds(ks, blk), 2 * LANES * h + LANES:2 * LANES * (h + 1)]
                out.append((m_new, l, alpha * acc + jnp.dot(p.astype(BF16), v, preferred_element_type=F32)))
            return tuple(out)

        one = (jnp.full((blk, 1), -1e30, F32), jnp.zeros((blk, 1), F32), jnp.zeros((blk, MLA_V), F32))
        carry = lax.fori_loop(0, qi // 2, lambda pr, c: kv_step(2 * pr + 1, kv_step(2 * pr, c, False), False),
                              (one,) * MLA_HEADS)
        carry = lax.cond(qi % 2 == 1, lambda c: kv_step(qi - 1, c, False), lambda c: c, carry)
        carry = kv_step(qi, carry, True)
        for h in heads:
            m, l, acc = carry[h]
            o_ref[:, LANES * h:LANES * (h + 1)] = acc / l
            lse_ref[h] = jnp.broadcast_to(m + jnp.log(l), (blk, LANES))

    return pl.pallas_call(
        body, name="attn_fwd", grid=(bsz, t // blk),
        in_specs=[pl.BlockSpec((None, blk, MLA_WIDTH), lambda b, i: (b, i, 0)),
                  pl.BlockSpec((None, blk, MLA_WIDTH), lambda b, i: (b, i, 0)),
                  pl.BlockSpec((None, t, 2 * MLA_WIDTH), lambda b, i: (b, 0, 0)),
                  pl.BlockSpec((None, t, LANES), lambda b, i: (b, 0, 0))],
        out_specs=[pl.BlockSpec((None, blk, MLA_WIDTH), lambda b, i: (b, i, 0)),
                   pl.BlockSpec((None, MLA_HEADS, blk, LANES), lambda b, i: (b, 0, i, 0))],
        out_shape=[_sds((bsz, t, MLA_WIDTH)), _sds((bsz, MLA_HEADS, t, LANES))],
        compiler_params=pltpu.CompilerParams(dimension_semantics=("arbitrary", "arbitrary")),
    )(qn, qr, kv, kr)


def _attn_bwd(qn, qr, kv, kr, o, do, lse):
    bsz, t, _ = qn.shape
    blk = min(ATT_BLK, t)
    nb = t // blk
    assert nb % 2 == 0, "query blocks are taken in pairs"

    def body(qn_ref, qr_ref, kn_ref, kr_ref, v_ref, o_ref, do_ref, lse_ref, dqn_ref, dqr_ref, dkv_ref, dkr_ref, dq_sc, delta_sc):
        dq_sc[...] = jnp.zeros_like(dq_sc)
        delta_sc[...] = jnp.sum(do_ref[...].astype(F32) * o_ref[...], axis=1, keepdims=True)

        lower = _causal(0, 0, blk)

        def q_blocks(j, k, vb, carry, blocks, diagonal):
            dk, dv = carry
            us = range(len(blocks))
            qs = [i * blk if isinstance(i, int) else pl.multiple_of(i * blk, blk) for i in blocks]
            q = [jnp.concatenate([qn_ref[pl.ds(qs[u], blk), :], qr_ref[pl.ds(qs[u], blk), :]], axis=1) for u in us]
            dob = [do_ref[pl.ds(qs[u], blk), :] for u in us]
            s = [lax.dot_general(q[u], k, _NT, preferred_element_type=F32) for u in us]
            dp = [lax.dot_general(dob[u], vb, _NT, preferred_element_type=F32) for u in us]
            pb, ds = [], []
            for u in us:
                p = jnp.exp(s[u] * ATT_SCALE - lse_ref[pl.ds(qs[u], blk), 0:1])
                if diagonal == u:
                    p = jnp.where(lower, p, 0.0)
                pb.append(p.astype(BF16))
                ds.append((p * (dp[u] - delta_sc[pl.ds(qs[u], blk), :]) * ATT_SCALE).astype(BF16))
            for u in us:
                dv = dv + lax.dot_general(pb[u], dob[u], _TN, preferred_element_type=F32)
            for u in us:
                dq_sc[pl.ds(qs[u], blk), :] += jnp.dot(ds[u], k, preferred_element_type=F32)
                dk = dk + lax.dot_general(ds[u], q[u], _TN, preferred_element_type=F32)
            return dk, dv

        for j in range(nb):
            ks = j * blk
            k = jnp.concatenate([kn_ref[ks:ks + blk, :], kr_ref[ks:ks + blk, :]], axis=1)
            vb = v_ref[ks:ks + blk, :]
            carry = (jnp.zeros((blk, 2 * LANES), F32), jnp.zeros((blk, MLA_V), F32))
            if j % 2 == 0:
                carry = q_blocks(j, k, vb, carry, [j, j + 1], 0)
            else:
                carry = q_blocks(j, k, vb, carry, [j], 0)
            pairs_from = j // 2 + 1
            if nb // 2 - pairs_from > 0:
                carry = lax.fori_loop(pairs_from, nb // 2,
                                      lambda pr, c, j=j, k=k, vb=vb: q_blocks(j, k, vb, c, [2 * pr, 2 * pr + 1], None),
                                      carry, unroll=2)
            dk, dv = carry
            dkv_ref[ks:ks + blk, 0:LANES] = dk[:, 0:LANES]
            dkv_ref[ks:ks + blk, LANES:2 * LANES] = dv
            dkr_ref[ks:ks + blk, :] = dk[:, LANES:2 * LANES]
        dqn_ref[...] = dq_sc[:, 0:LANES]
        dqr_ref[...] = dq_sc[:, LANES:2 * LANES]

    head_col = lambda b, h: (b, 0, h)
    return pl.pallas_call(
        body, name="attn_bwd", grid=(bsz, MLA_HEADS),
        in_specs=[pl.BlockSpec((None, t, LANES), head_col), pl.BlockSpec((None, t, LANES), head_col),
                  pl.BlockSpec((None, t, LANES), lambda b, h: (b, 0, 2 * h)),
                  pl.BlockSpec((None, t, LANES), lambda b, h: (b, 0, 0)),
                  pl.BlockSpec((None, t, LANES), lambda b, h: (b, 0, 2 * h + 1)),
                  pl.BlockSpec((None, t, LANES), head_col), pl.BlockSpec((None, t, LANES), head_col),
                  pl.BlockSpec((None, None, t, LANES), lambda b, h: (b, h, 0, 0))],
        out_specs=[pl.BlockSpec((None, t, LANES), head_col), pl.BlockSpec((None, t, LANES), head_col),
                   pl.BlockSpec((None, t, 2 * LANES), head_col),
                   pl.BlockSpec((None, None, t, LANES), lambda b, h: (h, b, 0, 0))],
        out_shape=[_sds((bsz, t, MLA_WIDTH)), _sds((bsz, t, MLA_WIDTH)), _sds((bsz, t, 2 * MLA_WIDTH)),
                   _sds((MLA_HEADS, bsz, t, LANES))],
        scratch_shapes=[pltpu.VMEM((t, 2 * LANES), F32), pltpu.VMEM((t, 1), F32)],
        compiler_params=pltpu.CompilerParams(dimension_semantics=("arbitrary", "arbitrary")),
    )(qn, qr, kv, kr, kv, o, do, lse)


SCAN_CHUNK = 16


def _diag_mask():
    row = lax.broadcasted_iota(jnp.int32, (RW_HEAD, RW_WIDTH), 0)
    lane = lax.broadcasted_iota(jnp.int32, (RW_HEAD, RW_WIDTH), 1)
    return jnp.where(row == (lane & (RW_HEAD - 1)), 1.0, 0.0)


def _time_minor(a):
    bsz, t, _ = a.shape
    a = a.reshape(bsz, t // SCAN_CHUNK, SCAN_CHUNK, RW_HEADS, RW_HEAD)
    return a.transpose(0, 1, 4, 3, 2).reshape(bsz, t // SCAN_CHUNK, RW_HEAD, RW_HEADS * SCAN_CHUNK)


def _head_expand():
    l = lax.broadcasted_iota(jnp.int32, (2 * LANES, RW_WIDTH), 0)
    n = lax.broadcasted_iota(jnp.int32, (2 * LANES, RW_WIDTH), 1)
    return jnp.where(((l & (LANES - 1)) >> 4) == (n >> 6), 1.0, 0.0).astype(BF16)


BCAST_GROUP = 4


def _outer_chunk(tm_ref, row_ref, out_sc, expand, seqs):
    step_of_lane = lax.broadcasted_iota(jnp.int32, (RW_HEAD, LANES), 1) & (SCAN_CHUNK - 1)
    tiles = [tm_ref[bi, 0] for bi in seqs]
    for t0 in range(0, SCAN_CHUNK, BCAST_GROUP):
        parts = []
        for t in range(t0, t0 + BCAST_GROUP):
            for tile in tiles:
                a = jnp.where(step_of_lane == t, tile, 0.0)
                hi = a.astype(BF16)
                parts.append(jnp.concatenate([hi, (a - hi.astype(F32)).astype(BF16)], axis=1))
        cols = jnp.dot(jnp.concatenate(parts, axis=0), expand, preferred_element_type=F32)
        for j, t in enumerate(range(t0, t0 + BCAST_GROUP)):
            base = j * RW_HEAD * len(seqs)
            out_sc[t] = jnp.concatenate([cols[base + RW_HEAD * bi:base + RW_HEAD * (bi + 1)] * row_ref[bi, t:t + 1, :]
                                         for bi in seqs], axis=0)


def _fold8(x):
    acc = x[0:8]
    for j in range(1, x.shape[0] // 8):
        acc = acc + x[8 * j:8 * (j + 1)]
    return acc


def _rows8(at):
    return pl.ds(at * 8 if isinstance(at, int) else pl.multiple_of(at * 8, 8), 8)


def _put8(sc, bi, at, val):
    for j in range(RW_WIDTH // LANES):
        sc[bi * (RW_WIDTH // LANES) + j, _rows8(at), :] = val[:, LANES * j:LANES * (j + 1)]


def _unfold8(sc, bi, steps):
    tiles = []
    for j in range(RW_WIDTH // LANES):
        view = sc.at[bi * (RW_WIDTH // LANES) + j]
        acc = view[pl.ds(0, steps, stride=8), :]
        for s in range(1, 8):
            acc = acc + view[pl.ds(s, steps, stride=8), :]
        tiles.append(acc)
    return jnp.concatenate(tiles, axis=1)


def _scan_fwd(r, w, k, vt, nkk, b):
    bsz, t, _ = r.shape
    tc = SCAN_CHUNK

    def body(r_ref, w_ref, k_ref, n_ref, b_ref, vt_ref, y_ref, st_ref, s_sc, vc_sc, y_sc):
        @pl.when(pl.program_id(0) == 0)
        def _():
            s_sc[...] = jnp.zeros_like(s_sc)

        ones = _seg_ones()
        diag = _diag_mask()
        seqs = range(bsz)
        _outer_chunk(vt_ref, k_ref, vc_sc, _head_expand(), seqs)

        def put_y(ya, at):
            for bi in seqs:
                _put8(y_sc, bi, at, _fold8(ya[bi] * diag))

        def step(i, _):
            row = lambda ref, bi: ref[bi, pl.ds(i, 1), :]
            prev = jnp.maximum(i - 1, 0)
            s_old = [s_sc[bi] for bi in seqs]
            s_b = [s_old[bi].astype(BF16) for bi in seqs]
            sa = _seg_multi([s_b[bi] * row(n_ref, bi).astype(BF16) for bi in seqs], ones, 1)
            put_y(_seg_multi([s_b[bi] * r_ref[bi, pl.ds(prev, 1), :].astype(BF16) for bi in seqs], ones, 1), prev)
            vk = vc_sc[i]
            for bi in seqs:
                s_new = s_old[bi] * row(w_ref, bi) + sa[bi] * row(b_ref, bi) + vk[RW_HEAD * bi:RW_HEAD * (bi + 1)]
                s_sc[bi] = s_new
                st_ref[bi, i] = s_new
            return 0

        lax.fori_loop(0, tc, step, 0, unroll=8)
        put_y(_seg_multi([s_sc[bi].astype(BF16) * r_ref[bi, tc - 1:tc, :].astype(BF16) for bi in seqs], ones, 1), tc - 1)
        for bi in seqs:
            y_ref[bi] = _unfold8(y_sc, bi, tc)

    vec = pl.BlockSpec((bsz, tc, RW_WIDTH), lambda c: (0, c, 0))
    return pl.pallas_call(
        body, name="scan_fwd", grid=(t // tc,),
        in_specs=[vec] * 5 + [pl.BlockSpec((bsz, 1, RW_HEAD, LANES), lambda c: (0, c, 0, 0))],
        out_specs=[vec, pl.BlockSpec((bsz, tc, RW_HEAD, RW_WIDTH), lambda c: (0, c, 0, 0))],
        out_shape=[_sds((bsz, t, RW_WIDTH)), _sds((bsz, t, RW_HEAD, RW_WIDTH))],
        scratch_shapes=[pltpu.VMEM((bsz, RW_HEAD, RW_WIDTH), F32), pltpu.VMEM((tc, bsz * RW_HEAD, RW_WIDTH), F32),
                        pltpu.VMEM((bsz * RW_WIDTH // LANES, tc * 8, LANES), F32)],
        compiler_params=_ARB1,
    )(r, w, k, nkk, b, vt)


def _own_head_row(x):
    first_half = lax.broadcasted_iota(jnp.int32, (1, LANES), 1) < RW_HEAD
    tiles = [jnp.where(first_half, x[2 * j:2 * j + 1, LANES * j:LANES * (j + 1)], x[2 * j + 1:2 * j + 2, LANES * j:LANES * (j + 1)])
             for j in range(RW_WIDTH // LANES)]
    return jnp.concatenate(tiles, axis=1)


def _scan_bwd(r, w, k, vt, nkk, b, st, dyt):
    bsz, t, _ = r.shape
    tc = SCAN_CHUNK
    nc = t // tc

    def body(r_ref, w_ref, k_ref, n_ref, b_ref, vt_ref, dyt_ref, st_ref, halo_ref,
             dr_ref, dw_ref, dk_ref, dv_ref, dn_ref, db_ref, g_sc, dc_sc, v8_sc, dy8_sc, *part_scs):
        c = pl.program_id(0)

        @pl.when(c == 0)
        def _():
            g_sc[...] = jnp.zeros_like(g_sc)

        ones = _seg_ones()
        diag = _diag_mask()
        has_prev = jnp.where(c == nc - 1, 0.0, 1.0)
        seqs = range(bsz)
        _outer_chunk(dyt_ref, r_ref, dc_sc, _head_expand(), seqs)
        for bi in seqs:
            v8_sc[bi] = jnp.concatenate([vt_ref[bi, 0].T] * 2, axis=1)
            dy8_sc[bi] = jnp.concatenate([dyt_ref[bi, 0].T] * 2, axis=1)
        by_head = lambda sc, bi, i: sc.at[bi][pl.ds(i, RW_HEADS, stride=SCAN_CHUNK), :][:, 0:RW_HEAD].astype(BF16)
        dw_sc, dv_sc, dn_sc, db_sc = part_scs

        def step(i, s_p):
            static = isinstance(i, int)
            row = lambda ref, bi: ref[bi, i:i + 1, :] if static else ref[bi, pl.ds(i, 1), :]
            put_row = lambda ref, bi, val: ref.__setitem__((bi, slice(i, i + 1) if static else pl.ds(i, 1), slice(None)), val)
            dr8 = [jnp.dot(by_head(dy8_sc, bi, i), st_ref[bi, i].astype(BF16), preferred_element_type=F32) for bi in seqs]
            rowb = lambda ref, bi: row(ref, bi).astype(BF16)
            sa = _seg_multi([s_p[bi].astype(BF16) * rowb(n_ref, bi) for bi in seqs], ones, 1)
            dc_all = dc_sc[i]
            dc = [dc_all[RW_HEAD * bi:RW_HEAD * (bi + 1)] for bi in seqs]
            g = [g_sc[bi] + dc[bi] for bi in seqs]
            g_b = [g[bi].astype(BF16) for bi in seqs]
            res = _seg_multi([g_b[bi] * rowb(b_ref, bi) for bi in seqs] + [g_b[bi] * rowb(k_ref, bi) for bi in seqs], ones, 1)
            dsa, dvb = res[:bsz], res[bsz:]
            for bi in seqs:
                dk8 = jnp.dot(by_head(v8_sc, bi, i), g_b[bi], preferred_element_type=F32)
                put_row(dr_ref, bi, _own_head_row(dr8[bi]))
                put_row(dk_ref, bi, _own_head_row(dk8))
                _put8(dv_sc, bi, i, _fold8(dvb[bi] * diag))
                _put8(dw_sc, bi, i, _fold8(g[bi] * s_p[bi]))
                _put8(db_sc, bi, i, _fold8(g[bi] * sa[bi]))
                _put8(dn_sc, bi, i, _fold8(s_p[bi] * dsa[bi]))
                g_sc[bi] = g[bi] * row(w_ref, bi) + dsa[bi] * row(n_ref, bi)

        def loop_step(ii, _):
            i = tc - 1 - ii
            step(i, [st_ref[bi, i - 1] for bi in seqs])
            return 0

        lax.fori_loop(0, tc - 1, loop_step, 0, unroll=5)
        step(0, [halo_ref[bi, 0] * has_prev for bi in seqs])
        for out_ref, sc in zip((dw_ref, dv_ref, dn_ref, db_ref), part_scs):
            for bi in seqs:
                out_ref[bi] = _unfold8(sc, bi, tc)

    vec = pl.BlockSpec((bsz, tc, RW_WIDTH), lambda c: (0, nc - 1 - c, 0))
    tmin = pl.BlockSpec((bsz, 1, RW_HEAD, LANES), lambda c: (0, nc - 1 - c, 0, 0))
    parts = pltpu.VMEM((bsz * RW_WIDTH // LANES, tc * 8, LANES), F32)
    heads_steps = pltpu.VMEM((bsz, LANES, LANES), F32)
    return pl.pallas_call(
        body, name="scan_bwd", grid=(nc,),
        in_specs=[vec] * 5 + [tmin, tmin,
                              pl.BlockSpec((bsz, tc, RW_HEAD, RW_WIDTH), lambda c: (0, nc - 1 - c, 0, 0)),
                              pl.BlockSpec((bsz, 1, RW_HEAD, RW_WIDTH), lambda c: (0, jnp.maximum((nc - 1 - c) * tc - 1, 0), 0, 0))],
        out_specs=[vec] * 6,
        out_shape=[_sds((bsz, t, RW_WIDTH))] * 6,
        scratch_shapes=[pltpu.VMEM((bsz, RW_HEAD, RW_WIDTH), F32), pltpu.VMEM((tc, bsz * RW_HEAD, RW_WIDTH), F32),
                        heads_steps, heads_steps] + [parts] * 4,
        compiler_params=_ARB1,
    )(r, w, k, nkk, b, vt, dyt, st, st)


TOKEN_TILE = 256
VJP_TILE = 256


def _padded_weights(wt):
    f = lambda a: a.astype(F32)
    w_in = wt["w_in"][0].astype(BF16)
    zeros = lambda r, c: jnp.zeros((r, c), F32)
    wp = jnp.concatenate([w_in[:, :MLA_COLS], jnp.zeros((D_MODEL, PM_W - MLA_COLS), BF16), w_in[:, MLA_COLS:]], axis=1)
    w_uq = f(wt["mla_w_uq"][0]).reshape(Q_LORA, MLA_HEADS, MLA_NOPE + MLA_ROPE)
    wqn = w_uq[:, :, :MLA_NOPE].reshape(Q_LORA, MLA_HEADS * MLA_NOPE)
    wqr = jnp.concatenate([w_uq[:, :, MLA_NOPE:], jnp.zeros((Q_LORA, MLA_HEADS, LANES - MLA_ROPE), F32)], axis=2)
    wqr = wqr.reshape(Q_LORA, MLA_HEADS * LANES)
    w2p = jnp.concatenate([f(wt["rw_w2"][0]), zeros(LORA, RW_WIDTH)], axis=0)
    a2p = jnp.concatenate([zeros(LORA, RW_WIDTH), f(wt["rw_a2"][0])], axis=0)
    bw = (f(wt["mla_q_norm_g"]), wqn, wqr, f(wt["mla_kv_norm_g"]), f(wt["mla_w_ukv"][0]), f(wt["rw_mu"]), f(wt["rw_w0"]),
          w2p, f(wt["rw_a0"]), a2p, f(wt["rw_k_k"]), f(wt["rw_k_a"]))
    hw = (f(wt["rw_ln_g"]), f(wt["rw_ln_b"]), f(wt["rw_r_k"]).reshape(1, RW_WIDTH), f(wt["w_out"][0]), f(wt["norm_post_g"]))
    return wp, bw, hw


def _local_step(x, positions, target, wt):
    bsz, t, _ = x.shape
    n = bsz * t
    tm = min(TOKEN_TILE, t)
    tps = t // tm
    ts = min(VJP_TILE, t)
    wp, bw, hw = _padded_weights(wt)
    wpb = wp.astype(BF16)
    g_pre = wt["norm_pre_g"].astype(F32)
    x2 = x.reshape(n, D_MODEL)
    tgt2 = target.reshape(n, D_MODEL)
    inv_freq = ROPE_THETA ** (-jnp.arange(0, MLA_ROPE, 2, dtype=F32) / MLA_ROPE)
    invf = jnp.tile(inv_freq, LANES // (MLA_ROPE // 2)).reshape(LANES, 1)
    cos, sin = _rope_tables(positions.reshape(1, n), invf, tm)

    u, pm, prw, z = _fwd_a(x2, g_pre, wpb, tm)
    qn, qr, kv, kr, r, w, k, v, nkk, b = _fwd_b(pm, prw, cos, sin, bw, tm, tps)
    b3 = lambda a: a.reshape(bsz, t, a.shape[-1])
    ym, lse = _attn_fwd(b3(qn), b3(qr), b3(kv), b3(kr))
    vt = _time_minor(b3(v))
    ys, st = _scan_fwd(b3(r), b3(w), b3(k), vt, b3(nkk), b3(b))
    (dys, dr_h, dk_h, dv_h, dym, dz, dxres, loss, d_lng, d_lnb, d_rk, d_wout, d_gpost) = _head(
        ys.reshape(n, RW_WIDTH), r, k, v, ym.reshape(n, MLA_WIDTH), z, x2, tgt2, hw, ts)
    dqn, dqr, dkv, dkr_heads = _attn_bwd(b3(qn), b3(qr), b3(kv), b3(kr), ym, b3(dym), lse)
    dr_s, dw_s, dk_s, dv_s, dn_s, db_s = _scan_bwd(b3(r), b3(w), b3(k), vt, b3(nkk), b3(b), st, _time_minor(b3(dys)))
    f2 = lambda a: a.reshape(n, a.shape[-1])
    cts = (f2(dqn), f2(dqr), f2(dkv), f2(dr_s), dr_h, f2(dw_s), f2(dk_s), dk_h, f2(dv_s), dv_h, f2(dn_s), f2(db_s))
    (dpm, dprw, dps, d_gq, d_wqn, d_wqr, d_gkv, d_wkv, d_mu, d_w0, d_w2p, d_a0, d_a2p, d_kk, d_ka) = _bwd_b(
        pm, prw, cos, sin, bw, cts, dkr_heads.reshape(MLA_HEADS, n, LANES), ts, t // ts)
    grad_x, dpb, d_gpre = _bwd_a(x2, g_pre, wpb, dpm, dprw, dps, dz, dxres, tm, tps)
    d_wp = _dw_in(u, dpb, min(1024, n), 640)

    d_w_in = jnp.concatenate([d_wp[:, :MLA_COLS], d_wp[:, PM_W:]], axis=1)
    d_w_uq = jnp.concatenate([d_wqn.reshape(Q_LORA, MLA_HEADS, MLA_NOPE),
                              d_wqr.reshape(Q_LORA, MLA_HEADS, LANES)[:, :, :MLA_ROPE]], axis=2)
    grads = {
        "norm_pre_g": d_gpre, "w_in": d_w_in[None], "mla_q_norm_g": d_gq,
        "mla_w_uq": d_w_uq.reshape(1, Q_LORA, MLA_HEADS * (MLA_NOPE + MLA_ROPE)), "mla_kv_norm_g": d_gkv,
        "mla_w_ukv": d_wkv[None], "rw_mu": d_mu, "rw_w0": d_w0, "rw_w2": d_w2p[None, :LORA], "rw_a0": d_a0,
        "rw_a2": d_a2p[None, LORA:], "rw_k_k": d_kk, "rw_k_a": d_ka, "rw_r_k": d_rk.reshape(1, RW_HEADS, RW_HEAD),
        "rw_ln_g": d_lng, "rw_ln_b": d_lnb, "w_out": d_wout[None], "norm_post_g": d_gpost,
    }
    return loss, grad_x.reshape(bsz, t, D_MODEL), grads


_MESH = pl.DeviceIdType.MESH


def _gather_shards(shards):
    na = len(shards)

    def body(*refs):
        x_refs, out_refs = refs[:na], refs[na:2 * na]
        send_sems, recv_sems, local_sems = refs[2 * na:]
        x, y, c = lax.axis_index("x"), lax.axis_index("y"), lax.axis_index("c")
        me, sibling = (x, y, c), (x, y, 1 - c)
        chips = [(1 - x, y), (x, 1 - y), (1 - x, 1 - y)]
        arrays = range(na)

        def slot(a, px, py, pc):
            return out_refs[a].at[4 * px + 2 * py + pc]

        def copy(k, a, block, to, src=None):
            return pltpu.make_async_remote_copy(
                src_ref=slot(a, *block) if src is None else src, dst_ref=slot(a, *block),
                send_sem=send_sems.at[k, a], recv_sem=recv_sems.at[k, a], device_id=to, device_id_type=_MESH)

        mine = [pltpu.make_async_copy(x_refs[a], slot(a, *me), local_sems.at[a]) for a in arrays]
        for cp in mine:
            cp.start()
        first = [copy(0, a, me, sibling, src=x_refs[a]) for a in arrays]
        first += [copy(1 + j, a, me, (*chip, c), src=x_refs[a]) for j, chip in enumerate(chips) for a in arrays]
        for cp in first:
            cp.start()
        passed = []
        for j, chip in enumerate(chips):
            for a in arrays:
                copy(1 + j, a, (*chip, c), me).wait_recv()
                passed.append(copy(4 + j, a, (*chip, c), sibling))
                passed[-1].start()
        for a in arrays:
            copy(0, a, sibling, me).wait_recv()
        for j, chip in enumerate(chips):
            for a in arrays:
                copy(4 + j, a, (*chip, 1 - c), me).wait_recv()
        for cp in first + passed:
            cp.wait_send()
        for cp in mine:
            cp.wait()

    vmem = pl.BlockSpec(memory_space=pltpu.VMEM)
    return pl.pallas_call(
        body, name="gather_shards",
        out_shape=[_sds((N_DEV,) + a.shape, a.dtype) for a in shards],
        in_specs=[vmem] * na, out_specs=[vmem] * na,
        scratch_shapes=[pltpu.SemaphoreType.DMA((7, na)), pltpu.SemaphoreType.DMA((7, na)), pltpu.SemaphoreType.DMA((na,))],
    )(*shards)


SMALL_LANES = SMALL_N + LANES


N_CHIP = 4


def _exchange_grads(big_blocks, small_grads, loss_tile):
    nb = len(big_blocks)
    ns = len(small_grads)

    def body(*refs):
        big, small, loss_ref = refs[:nb], refs[nb:nb + ns], refs[nb + ns]
        out, rsmall = refs[nb + ns + 1:2 * nb + ns + 1], refs[2 * nb + ns + 1]
        scratch = refs[2 * nb + ns + 2:]
        stage, sums = scratch[:nb], scratch[nb:2 * nb]
        send1, recv1, send2, recv2, send_s, recv_s, row_sc = scratch[2 * nb:]
        x, y, c = lax.axis_index("x"), lax.axis_index("y"), lax.axis_index("c")
        me_lin = 4 * x + 2 * y + c
        my_chip = 2 * x + y
        sibling = (x, y, 1 - c)
        leaves = range(nb)

        to_sibling = [pltpu.make_async_remote_copy(
            src_ref=big[j].at[1 - c], dst_ref=stage[j], send_sem=send1.at[j], recv_sem=recv1.at[j],
            device_id=sibling, device_id_type=_MESH) for j in leaves]
        for cp in to_sibling:
            cp.start()
        off = 0
        for ref, (_, cnt) in zip(small, SMALL):
            row_sc[:, off:off + cnt] = ref[...]
            off += cnt
        row_sc[:, off:off + LANES] = loss_ref[0:1, :]
        rsmall[me_lin] = row_sc[...]
        rows = []
        for k in range(1, N_DEV):
            peer = (x ^ (k >> 2), y ^ ((k >> 1) & 1), c ^ (k & 1))
            rows.append(pltpu.make_async_remote_copy(
                src_ref=row_sc, dst_ref=rsmall.at[me_lin], send_sem=send_s.at[k - 1], recv_sem=recv_s.at[k - 1],
                device_id=peer, device_id_type=_MESH))
        for cp in rows:
            cp.start()
        to_chips = []
        for j in leaves:
            to_sibling[j].wait_recv()
            sums[j][...] = (big[j][c].astype(F32) + stage[j][...].astype(F32)).astype(BF16)
            out[j][0] = sums[j][my_chip]
            for q in range(1, N_CHIP):
                px, py = x ^ (q >> 1), y ^ (q & 1)
                to_chips.append(pltpu.make_async_remote_copy(
                    src_ref=sums[j].at[2 * px + py], dst_ref=out[j].at[q], send_sem=send2.at[q - 1, j],
                    recv_sem=recv2.at[q - 1, j], device_id=(px, py, c), device_id_type=_MESH))
                to_chips[-1].start()
        for cp in to_chips + rows:
            cp.wait_recv()
        for cp in to_sibling + to_chips + rows:
            cp.wait_send()

    vmem = pl.BlockSpec(memory_space=pltpu.VMEM)
    shard = [a.shape[2:] for a in big_blocks]
    return pl.pallas_call(
        body, name="exchange_grads",
        out_shape=[_sds((N_CHIP,) + s, BF16) for s in shard] + [_sds((N_DEV, 1, SMALL_LANES))],
        in_specs=[vmem] * (nb + ns + 1), out_specs=[vmem] * (nb + 1),
        scratch_shapes=[pltpu.VMEM((N_CHIP,) + s, BF16) for s in shard] * 2
        + [pltpu.SemaphoreType.DMA((nb,)), pltpu.SemaphoreType.DMA((nb,)),
           pltpu.SemaphoreType.DMA((N_CHIP - 1, nb)), pltpu.SemaphoreType.DMA((N_CHIP - 1, nb)),
           pltpu.SemaphoreType.DMA((N_DEV - 1,)), pltpu.SemaphoreType.DMA((N_DEV - 1,)), pltpu.VMEM((1, SMALL_LANES), F32)],
    )(*big_blocks, *small_grads, loss_tile)


def _adamw_math(w, g, m, v):
    m = ADAM_B1 * m + (1.0 - ADAM_B1) * g
    v = ADAM_B2 * v + (1.0 - ADAM_B2) * (g * g)
    m_hat = m / (1.0 - ADAM_B1 ** ADAM_STEP)
    v_hat = v / (1.0 - ADAM_B2 ** ADAM_STEP)
    return -ADAM_LR * (m_hat / (jnp.sqrt(v_hat) + ADAM_EPS) + ADAM_WD * w), m, v


def _reduce_adamw(name, parts, w, m, v, row_blocks):
    slots, rows, cols = parts.shape
    rb = rows // row_blocks

    def body(p_ref, w_ref, m_ref, v_ref, g_out, d_out, m_out, v_out):
        g = p_ref[0].astype(F32)
        for s in range(1, slots):
            g = g + p_ref[s].astype(F32)
        g_out[0] = g
        d_out[0], m_out[0], v_out[0] = _adamw_math(w_ref[0], g, m_ref[0], v_ref[0])

    blk = pl.BlockSpec((1, rb, cols), lambda i: (0, i, 0))
    return pl.pallas_call(
        body, name="reduce_adamw_" + name, grid=(row_blocks,),
        in_specs=[pl.BlockSpec((slots, rb, cols), lambda i: (0, i, 0)), blk, blk, blk],
        out_specs=[blk] * 4, out_shape=[_sds((1, rows, cols))] * 4,
        compiler_params=_ARB1,
    )(parts, w, m, v)


def _reduce_adamw_small(rows, ws, ms, vs):
    ns = len(SMALL)

    def body(r_ref, *refs):
        w_refs, m_refs, v_refs, outs = refs[:ns], refs[ns:2 * ns], refs[2 * ns:3 * ns], refs[3 * ns:]
        total = r_ref[0]
        for s in range(1, N_DEV):
            total = total + r_ref[s]
        off = 0
        for j, (_, cnt) in enumerate(SMALL):
            g = total[:, off:off + cnt]
            off += cnt
            outs[4 * j][...] = g
            outs[4 * j + 1][...], outs[4 * j + 2][...], outs[4 * j + 3][...] = _adamw_math(
                w_refs[j][...], g, m_refs[j][...], v_refs[j][...])
        outs[4 * ns][...] = total[:, off:off + LANES]

    vmem = pl.BlockSpec(memory_space=pltpu.VMEM)
    return pl.pallas_call(
        body, name="reduce_adamw_small",
        in_specs=[vmem] * (1 + 3 * ns), out_specs=[vmem] * (4 * ns + 1),
        out_shape=[_sds((1, cnt)) for _, cnt in SMALL for _ in range(4)] + [_sds((1, LANES))],
    )(rows, *ws, *ms, *vs)


def _shard_blocks(name, full):
    a = full[0]
    rows, cols = a.shape
    if name == "w_out":
        return a.reshape(N_CHIP, 2, rows // N_DEV, cols).transpose(1, 0, 2, 3)
    return a.reshape(rows, N_CHIP, 2, cols // N_DEV).transpose(2, 1, 0, 3)


def _unshard(name, blocks):
    _, rows, cols = blocks.shape
    if name == "w_out":
        return blocks.reshape(1, N_DEV * rows, cols)
    return blocks.transpose(1, 0, 2).reshape(1, rows, N_DEV * cols)


def kernel(x, positions, norm_pre_g, w_in, mla_q_norm_g, mla_w_uq, mla_kv_norm_g, mla_w_ukv, rw_mu, rw_w0, rw_w2, rw_a0, rw_a2, rw_k_k, rw_k_a, rw_r_k, rw_ln_g, rw_ln_b, w_out, norm_post_g, loss_target, m_norm_pre_g, m_w_in, m_mla_q_norm_g, m_mla_w_uq, m_mla_kv_norm_g, m_mla_w_ukv, m_rw_mu, m_rw_w0, m_rw_w2, m_rw_a0, m_rw_a2, m_rw_k_k, m_rw_k_a, m_rw_r_k, m_rw_ln_g, m_rw_ln_b, m_w_out, m_norm_post_g, v_norm_pre_g, v_w_in, v_mla_q_norm_g, v_mla_w_uq, v_mla_kv_norm_g, v_mla_w_ukv, v_rw_mu, v_rw_w0, v_rw_w2, v_rw_a0, v_rw_a2, v_rw_k_k, v_rw_k_a, v_rw_r_k, v_rw_ln_g, v_rw_ln_b, v_w_out, v_norm_post_g):
    given = dict(locals())
    w = {nm: given[nm] for nm in WEIGHTS}
    mom = {nm: given["m_" + nm] for nm in WEIGHTS}
    var = {nm: given["v_" + nm] for nm in WEIGHTS}
    sharded = list(SHARDED)

    gathered = _gather_shards([w[nm][0].astype(BF16) for nm in sharded])
    full = dict(w)
    for nm, blocks in zip(sharded, gathered):
        full[nm] = _unshard(nm, blocks)

    loss_part, grad_x, grads = _local_step(x, positions, loss_target, full)

    small_names = [nm for nm, _ in SMALL]
    row = lambda a: a.reshape(1, -1)
    got = _exchange_grads([_shard_blocks(nm, grads[nm]).astype(BF16) for nm in sharded],
                          [row(grads[nm]) for nm in small_names], loss_part)
    new = {}
    for nm, parts in zip(sharded, got[:-1]):
        new[nm] = _reduce_adamw(nm, parts, w[nm], mom[nm], var[nm], 4 if nm == "w_in" else 1)
    res = _reduce_adamw_small(got[-1], [row(w[nm]) for nm in small_names], [row(mom[nm]) for nm in small_names],
                              [row(var[nm]) for nm in small_names])
    for j, nm in enumerate(small_names):
        new[nm] = tuple(a.reshape(w[nm].shape) for a in res[4 * j:4 * j + 4])
    loss = res[-1][0, 0]
    return (loss, grad_x, *[new[nm][j] for j in range(4) for nm in WEIGHTS])
```

```python
import jax
import jax.numpy as jnp
from jax import lax
from jax.experimental import pallas as pl
from jax.experimental.pallas import tpu as pltpu

F32 = jnp.float32
BF16 = jnp.bfloat16

D_MODEL = 1024
MLA_HEADS = 4
MLA_NOPE = 128
MLA_ROPE = 64
MLA_V = 128
MLA_WIDTH = MLA_HEADS * MLA_V
Q_LORA = 256
KV_LORA = 128
ROPE_THETA = 10000.0
RW_HEAD = 64
RW_WIDTH = 512
RW_HEADS = RW_WIDTH // RW_HEAD
LORA = 64
RW_COLS = 3 * RW_WIDTH + 2 * LORA
MLA_COLS = Q_LORA + KV_LORA + MLA_ROPE
RW_GN_EPS = 64e-5
NORM_EPS = 1e-6
ATT_SCALE = (MLA_NOPE + MLA_ROPE) ** -0.5
ADAM_LR, ADAM_B1, ADAM_B2, ADAM_EPS, ADAM_WD, ADAM_STEP = 0.001, 0.9, 0.999, 1e-08, 0.01, 10
N_DEV = 8
LANES = 128
MXU = 256

PM_W = 512
WP_COLS = PM_W + RW_COLS + D_MODEL
RW_PIECES = ((0, 512), (512, 1024), (1024, 1536), (1536, 1664))

SHARDED = ("w_in", "mla_w_uq", "mla_w_ukv", "rw_w2", "rw_a2", "w_out")
SMALL = (("norm_pre_g", 1024), ("mla_q_norm_g", 256), ("mla_kv_norm_g", 128), ("rw_mu", 1664), ("rw_w0", 512),
         ("rw_a0", 512), ("rw_k_k", 512), ("rw_k_a", 512), ("rw_r_k", 512), ("rw_ln_g", 512), ("rw_ln_b", 512),
         ("norm_post_g", 1024))
SMALL_N = sum(n for _, n in SMALL)
WEIGHTS = ("norm_pre_g", "w_in", "mla_q_norm_g", "mla_w_uq", "mla_kv_norm_g", "mla_w_ukv", "rw_mu", "rw_w0", "rw_w2",
           "rw_a0", "rw_a2", "rw_k_k", "rw_k_a", "rw_r_k", "rw_ln_g", "rw_ln_b", "w_out", "norm_post_g")


def _seg_ones():
    r = lax.broadcasted_iota(jnp.int32, (MXU, MXU), 0) >> 6
    c = lax.broadcasted_iota(jnp.int32, (MXU, MXU), 1) >> 6
    return jnp.where(r == c, 1.0, 0.0).astype(BF16)


def _seg_dot(x, ones, passes):
    parts, rem = [], x
    for p in range(passes):
        hb = rem.astype(BF16)
        parts.append(hb)
        if p + 1 < passes:
            rem = rem - hb.astype(F32)
    outs = []
    for j in range(x.shape[1] // MXU):
        acc = None
        for hb in parts:
            d = jnp.dot(hb[:, MXU * j:MXU * (j + 1)], ones, preferred_element_type=F32)
            acc = d if acc is None else acc + d
        outs.append(acc)
    return outs[0] if len(outs) == 1 else jnp.concatenate(outs, axis=1)


def _seg_multi(xs, ones, passes):
    his = [x.astype(BF16) for x in xs]
    hi = jnp.concatenate(his, axis=0)
    if passes == 2:
        lo = jnp.concatenate([(x - h.astype(F32)).astype(BF16) for x, h in zip(xs, his)], axis=0)
        rhs = jnp.concatenate([ones, ones], axis=0)
    halves = []
    for j in range(hi.shape[1] // MXU):
        sl = slice(MXU * j, MXU * (j + 1))
        if passes == 2:
            halves.append(jnp.dot(jnp.concatenate([hi[:, sl], lo[:, sl]], axis=1), rhs, preferred_element_type=F32))
        else:
            halves.append(jnp.dot(hi[:, sl], ones, preferred_element_type=F32))
    full = jnp.concatenate(halves, axis=1)
    m = xs[0].shape[0]
    return [full[m * i:m * (i + 1)] for i in range(len(xs))]


@jax.custom_vjp
def _segsum(x):
    return _seg_dot(x, _seg_ones(), 2)


_segsum.defvjp(lambda x: (_segsum(x), None), lambda _, g: (_segsum(g),))


@jax.custom_vjp
def _bdot(a, w):
    return jnp.dot(a.astype(BF16), w.astype(BF16), preferred_element_type=F32)


def _bdot_fwd(a, w):
    return _bdot(a, w), (a, w)


def _bdot_bwd(res, g):
    a, w = res
    gb = g.astype(BF16)
    da = lax.dot_general(gb, w.astype(BF16), (((1,), (1,)), ((), ())), preferred_element_type=F32)
    dw = lax.dot_general(a.astype(BF16), gb, (((0,), (0,)), ((), ())), preferred_element_type=F32)
    return da, dw


_bdot.defvjp(_bdot_fwd, _bdot_bwd)


def _rot_impl(x):
    w = x.shape[1]
    lane = lax.broadcasted_iota(jnp.int32, x.shape, 1)
    return jnp.where((lane & 63) < 32, -pltpu.roll(x, w - 32, 1), pltpu.roll(x, 32, 1))


@jax.custom_vjp
def _rot(x):
    return _rot_impl(x)


_rot.defvjp(lambda x: (_rot_impl(x), None), lambda _, g: (-_rot_impl(g),))


def _rms(x, g):
    return x * lax.rsqrt(jnp.mean(x * x, axis=-1, keepdims=True) + NORM_EPS) * g


def _shift_rows(p, prev_row):
    row = lax.broadcasted_iota(jnp.int32, p.shape, 0)
    return jnp.where(row == 0, prev_row, pltpu.roll(p, 1, 0))


def _unshift_rows(g, next_row):
    row = lax.broadcasted_iota(jnp.int32, g.shape, 0)
    return jnp.where(row == g.shape[0] - 1, next_row, pltpu.roll(g, g.shape[0] - 1, 0))


def _f_mla(cq, ckv, kr, cos, sin, g_q, wqn, wqr, g_kv, wkv):
    qn = _rms(cq, g_q)
    q_nope = _bdot(qn, wqn)
    q_r = _bdot(qn, wqr)
    cos4 = jnp.concatenate([cos] * MLA_HEADS, axis=1)
    sin4 = jnp.concatenate([sin] * MLA_HEADS, axis=1)
    q_rope = q_r * cos4 + _rot(q_r) * sin4
    kv = _bdot(_rms(ckv, g_kv), wkv)
    k_rope = kr * cos + _rot(kr) * sin
    return q_nope, q_rope, kv, k_rope


def _f_rw(pr, pk, pv, pt, sr, sk, sv, st, mu_r, mu_k, mu_v, mu_t, w0, w2p, a0, a2p, k_k, k_a):
    r = pr + (sr - pr) * mu_r
    k = pk + (sk - pk) * mu_k
    v = pv + (sv - pv) * mu_v
    t = pt + (st - pt) * mu_t
    nwl = -(w0 + _bdot(jnp.tanh(t), w2p))
    softplus = jnp.maximum(nwl, 0.0) + jnp.log(1.0 + jnp.exp(-jnp.abs(nwl)))
    decay = jnp.exp(-jnp.exp(-softplus - 0.5))
    a = jax.nn.sigmoid(a0 + _bdot(t, a2p))
    kk = k * k_k
    kk = kk / jnp.maximum(jnp.sqrt(_segsum(kk * kk)), 1e-12)
    k2 = k * (1.0 + (a - 1.0) * k_a)
    return r, decay, k2, v, -kk, kk * a


def _f_head(ys, r, k, v, ym, z1, z2, x, tgt, ln_g, ln_b, r_k, w1, w2, g_post):
    inv = 1.0 / RW_HEAD
    yc = ys - _segsum(ys) * inv
    var = _segsum(yc * yc) * inv
    y = yc * lax.rsqrt(var + RW_GN_EPS) * ln_g + ln_b
    y_rw = y + _segsum(r * k * r_k) * v
    c1 = ym * (z1 * jax.nn.sigmoid(z1))
    c2 = y_rw * (z2 * jax.nn.sigmoid(z2))
    out = _bdot(c1, w1) + _bdot(c2, w2)
    err = x + _rms(out, g_post) - tgt
    per_row = jnp.sum(err * err, axis=1, keepdims=True)
    return jnp.sum(per_row, axis=0, keepdims=True) * (0.5 / D_MODEL)


def _rows(tm, width):
    return pl.BlockSpec((tm, width), lambda i: (i, 0))


def _whole(shape):
    zeros = (0,) * len(shape)
    return pl.BlockSpec(shape, lambda i: zeros)


def _sds(shape, dtype=F32):
    return jax.ShapeDtypeStruct(shape, dtype)


_ARB1 = pltpu.CompilerParams(dimension_semantics=("arbitrary",))


def _acc(ref, val, first):
    @pl.when(first)
    def _():
        ref[...] = val

    @pl.when(jnp.logical_not(first))
    def _():
        ref[...] += val


def _fwd_a(x2, g_pre, wp, tm):
    n = x2.shape[0]

    def body(x_ref, g_ref, w_ref, ut_ref, pm_ref, prw_ref, z_ref):
        u = _rms(x_ref[...], g_ref[...])
        ub = u.astype(BF16)
        ut_ref[...] = u.T.astype(BF16)
        pm_ref[...] = jnp.dot(ub, w_ref[:, 0:PM_W], preferred_element_type=F32)
        prw_ref[...] = jnp.dot(ub, w_ref[:, PM_W:PM_W + RW_COLS], preferred_element_type=F32)
        z_ref[...] = jnp.dot(ub, w_ref[:, PM_W + RW_COLS:WP_COLS], preferred_element_type=F32)

    return pl.pallas_call(
        body, name="fwd_a", grid=(n // tm,),
        in_specs=[_rows(tm, D_MODEL), _whole((1, D_MODEL)), _whole((D_MODEL, WP_COLS))],
        out_specs=[pl.BlockSpec((D_MODEL, tm), lambda i: (0, i)), _rows(tm, PM_W), _rows(tm, RW_COLS), _rows(tm, D_MODEL)],
        out_shape=[_sds((D_MODEL, n), BF16), _sds((n, PM_W)), _sds((n, RW_COLS)), _sds((n, D_MODEL))],
        compiler_params=_ARB1,
    )(x2, g_pre, wp)


def _rope_tables(pos_row, invf_col, tm):
    n = pos_row.shape[1]

    def body(p_ref, f_ref, c_ref, s_ref):
        distinct = MLA_ROPE // 2
        ang = f_ref[0:distinct, :] * p_ref[...].astype(F32)
        c_ref[...] = jnp.concatenate([jnp.cos(ang)] * (LANES // distinct), axis=0).T
        s_ref[...] = jnp.concatenate([jnp.sin(ang)] * (LANES // distinct), axis=0).T

    return pl.pallas_call(
        body, name="rope_tables", grid=(n // tm,),
        in_specs=[pl.BlockSpec((1, tm), lambda i: (0, i)), _whole((LANES, 1))],
        out_specs=[_rows(tm, LANES), _rows(tm, LANES)],
        out_shape=[_sds((n, LANES)), _sds((n, LANES))],
        compiler_params=_ARB1,
    )(pos_row, invf_col)


_B_WEIGHT_SHAPES = ((1, Q_LORA), (Q_LORA, 512), (Q_LORA, 512), (1, KV_LORA), (KV_LORA, 1024), (1, RW_COLS), (1, RW_WIDTH),
                    (LANES, RW_WIDTH), (1, RW_WIDTH), (LANES, RW_WIDTH), (1, RW_WIDTH), (1, RW_WIDTH))


def _halo_prev(tm):
    return pl.BlockSpec((8, RW_COLS), lambda i: (jnp.maximum(i * (tm // 8) - 1, 0), 0))


def _b_operands(pm_ref, prw_ref, halo_ref, wrefs, tile, tiles_per_seq):
    g_q, wqn, wqr, g_kv, wkv, mu, w0, w2p, a0, a2p, k_k, k_a = wrefs
    mla_in = (pm_ref[:, 0:Q_LORA], pm_ref[:, Q_LORA:Q_LORA + KV_LORA], pm_ref[:, Q_LORA + KV_LORA:PM_W])
    mla_w = (g_q[...], wqn[...], wqr[...], g_kv[...], wkv[...])
    keep = jnp.where(tile % tiles_per_seq == 0, 0.0, 1.0)
    prev = halo_ref[7:8, :] * keep
    ps = tuple(prw_ref[:, a:b] for a, b in RW_PIECES)
    ss = tuple(_shift_rows(p, prev[:, a:b]) for p, (a, b) in zip(ps, RW_PIECES))
    rw_w = tuple(mu[:, a:b] for a, b in RW_PIECES) + (w0[...], w2p[...], a0[...], a2p[...], k_k[...], k_a[...])
    return mla_in, mla_w, ps + ss, rw_w


def _fwd_b(pm, prw, cos, sin, bw, tm, tiles_per_seq):
    n = pm.shape[0]

    def body(pm_ref, prw_ref, halo_ref, cos_ref, sin_ref, *refs):
        wrefs, outs = refs[:12], refs[12:]
        mla_in, mla_w, rw_in, rw_w = _b_operands(pm_ref, prw_ref, halo_ref, wrefs, pl.program_id(0), tiles_per_seq)
        res = _f_mla(*mla_in, cos_ref[...], sin_ref[...], *mla_w) + _f_rw(*rw_in, *rw_w)
        for o_ref, val in zip(outs, res):
            o_ref[...] = val.astype(o_ref.dtype)

    widths = (512, 512, 1024, LANES) + (RW_WIDTH,) * 6
    return pl.pallas_call(
        body, name="fwd_b", grid=(n // tm,),
        in_specs=[_rows(tm, PM_W), _rows(tm, RW_COLS), _halo_prev(tm), _rows(tm, LANES), _rows(tm, LANES)]
        + [_whole(s) for s in _B_WEIGHT_SHAPES],
        out_specs=[_rows(tm, w) for w in widths],
        out_shape=[_sds((n, w), BF16 if j < 4 else F32) for j, w in enumerate(widths)],
        compiler_params=_ARB1,
    )(pm, prw, prw, cos, sin, *bw)


def _bwd_b(pm, prw, cos, sin, bw, cts, dkr_heads, tm, tiles_per_seq):
    n = pm.shape[0]

    ct_widths = (512, 512, 1024) + (RW_WIDTH,) * 9
    n_ct = len(ct_widths)

    def body(pm_ref, prw_ref, halo_ref, cos_ref, sin_ref, *refs):
        wrefs, ct_refs, dkr_ref = refs[:12], refs[12:12 + n_ct], refs[12 + n_ct]
        dpm_ref, dprw_ref, dps_ref = refs[13 + n_ct:16 + n_ct]
        wg_refs = refs[16 + n_ct:]
        tile = pl.program_id(0)
        first = tile == 0
        mla_in, mla_w, rw_in, rw_w = _b_operands(pm_ref, prw_ref, halo_ref, wrefs, tile, tiles_per_seq)
        cos, sin = cos_ref[...], sin_ref[...]
        ct = [r[...] for r in ct_refs]
        _, vjp_mla = jax.vjp(lambda *a: _f_mla(*a[:3], cos, sin, *a[3:]), *mla_in, *mla_w)
        dkr = dkr_ref[0] + dkr_ref[1] + dkr_ref[2] + dkr_ref[3]
        d_mla = vjp_mla((ct[0], ct[1], ct[2], dkr))
        dpm_ref[:, 0:Q_LORA] = d_mla[0]
        dpm_ref[:, Q_LORA:Q_LORA + KV_LORA] = d_mla[1]
        dpm_ref[:, Q_LORA + KV_LORA:PM_W] = d_mla[2]
        _, vjp_rw = jax.vjp(_f_rw, *rw_in, *rw_w)
        d_rw = vjp_rw((ct[3] + ct[4], ct[5], ct[6] + ct[7], ct[8] + ct[9], ct[10], ct[11]))
        for j, (a, b) in enumerate(RW_PIECES):
            dprw_ref[:, a:b] = d_rw[j]
            dps_ref[:, a:b] = d_rw[4 + j]
        g_q, wqn, wqr, g_kv, wkv, mu, w0, w2p, a0, a2p, k_k, k_a = wg_refs
        for ref, val in zip((g_q, wqn, wqr, g_kv, wkv), d_mla[3:]):
            _acc(ref, val, first)
        for j, (a, b) in enumerate(RW_PIECES):
            _acc(mu.at[:, a:b], d_rw[8 + j], first)
        for ref, val in zip((w0, w2p, a0, a2p, k_k, k_a), d_rw[12:]):
            _acc(ref, val, first)

    return pl.pallas_call(
        body, name="bwd_b", grid=(n // tm,),
        in_specs=[_rows(tm, PM_W), _rows(tm, RW_COLS), _halo_prev(tm), _rows(tm, LANES), _rows(tm, LANES)]
        + [_whole(s) for s in _B_WEIGHT_SHAPES] + [_rows(tm, w) for w in ct_widths]
        + [pl.BlockSpec((MLA_HEADS, tm, LANES), lambda i: (0, i, 0))],
        out_specs=[_rows(tm, PM_W), _rows(tm, RW_COLS), _rows(tm, RW_COLS)] + [_whole(s) for s in _B_WEIGHT_SHAPES],
        out_shape=[_sds((n, PM_W)), _sds((n, RW_COLS)), _sds((n, RW_COLS))] + [_sds(s) for s in _B_WEIGHT_SHAPES],
        compiler_params=_ARB1,
    )(pm, prw, prw, cos, sin, *bw, *cts, dkr_heads)


def _head(ys, r, k, v, ym, z, x2, tgt, hw, tm):
    n = x2.shape[0]
    h_shapes = ((1, RW_WIDTH), (1, RW_WIDTH), (1, RW_WIDTH), (D_MODEL, D_MODEL), (1, D_MODEL))

    def body(ys_ref, r_ref, k_ref, v_ref, ym_ref, z_ref, x_ref, t_ref, lng, lnb, rk, wout, gpost,
             dys_ref, dr_ref, dk_ref, dv_ref, dym_ref, dz_ref, dx_ref, loss_ref, dlng, dlnb, drk, dwout, dgpost):
        first = pl.program_id(0) == 0
        tgt_v = t_ref[...]
        args = (ys_ref[...], r_ref[...], k_ref[...], v_ref[...], ym_ref[...], z_ref[:, 0:MLA_WIDTH], z_ref[:, MLA_WIDTH:D_MODEL],
                x_ref[...], lng[...], lnb[...], rk[...], wout[0:MLA_WIDTH, :], wout[MLA_WIDTH:D_MODEL, :], gpost[...])
        loss, vjp = jax.vjp(lambda *a: _f_head(*a[:8], tgt_v, *a[8:]), *args)
        d = vjp(jnp.ones((1, 1), F32))
        dys_ref[...] = d[0]
        dr_ref[...] = d[1]
        dk_ref[...] = d[2]
        dv_ref[...] = d[3]
        dym_ref[...] = d[4].astype(BF16)
        dz_ref[:, 0:MLA_WIDTH] = d[5]
        dz_ref[:, MLA_WIDTH:D_MODEL] = d[6]
        dx_ref[...] = d[7]
        _acc(loss_ref, jnp.broadcast_to(loss, (8, LANES)), first)
        _acc(dlng, d[8], first)
        _acc(dlnb, d[9], first)
        _acc(drk, d[10], first)
        _acc(dwout.at[0:MLA_WIDTH, :], d[11], first)
        _acc(dwout.at[MLA_WIDTH:D_MODEL, :], d[12], first)
        _acc(dgpost, d[13], first)

    widths = (RW_WIDTH,) * 4 + (MLA_WIDTH, D_MODEL, D_MODEL)
    return pl.pallas_call(
        body, name="head", grid=(n // tm,),
        in_specs=[_rows(tm, RW_WIDTH)] * 4 + [_rows(tm, MLA_WIDTH), _rows(tm, D_MODEL), _rows(tm, D_MODEL), _rows(tm, D_MODEL)]
        + [_whole(s) for s in h_shapes],
        out_specs=[_rows(tm, w) for w in widths] + [_whole((8, LANES))] + [_whole(s) for s in h_shapes],
        out_shape=[_sds((n, w), BF16 if j == 4 else F32) for j, w in enumerate(widths)] + [_sds((8, LANES))]
        + [_sds(s) for s in h_shapes],
        compiler_params=_ARB1,
    )(ys, r, k, v, ym, z, x2, tgt, *hw)


def _halo_next(tm, n):
    last = n // 8 - 1
    return pl.BlockSpec((8, RW_COLS), lambda i: (jnp.minimum((i + 1) * (tm // 8), last), 0))


def _bwd_a(x2, g_pre, wp, dpm, dprw, dps, dz, dxres, tm, tiles_per_seq):
    n = x2.shape[0]
    nt_dims = (((1,), (1,)), ((), ()))

    def body(x_ref, g_ref, w_ref, dpm_ref, dprw_ref, dps_ref, nxt_ref, dz_ref, dxres_ref, gx_ref, dpb_ref, dg_ref):
        tile = pl.program_id(0)
        keep = jnp.where((tile + 1) % tiles_per_seq == 0, 0.0, 1.0)
        dprw_v = dprw_ref[...] + _unshift_rows(dps_ref[...], nxt_ref[0:1, :] * keep)
        dpm_b, dprw_b, dz_b = dpm_ref[...].astype(BF16), dprw_v.astype(BF16), dz_ref[...].astype(BF16)
        dpb_ref[:, 0:PM_W] = dpm_b
        dpb_ref[:, PM_W:PM_W + RW_COLS] = dprw_b
        dpb_ref[:, PM_W + RW_COLS:WP_COLS] = dz_b
        du = (lax.dot_general(dpm_b, w_ref[:, 0:PM_W], nt_dims, preferred_element_type=F32)
              + lax.dot_general(dprw_b, w_ref[:, PM_W:PM_W + RW_COLS], nt_dims, preferred_element_type=F32)
              + lax.dot_general(dz_b, w_ref[:, PM_W + RW_COLS:WP_COLS], nt_dims, preferred_element_type=F32))
        x = x_ref[...]
        xhat = x * lax.rsqrt(jnp.mean(x * x, axis=-1, keepdims=True) + NORM_EPS)
        dxn = du * g_ref[...]
        dx = (dxn - xhat * jnp.mean(dxn * xhat, axis=-1, keepdims=True)) * lax.rsqrt(jnp.mean(x * x, axis=-1, keepdims=True) + NORM_EPS)
        gx_ref[...] = dx + dxres_ref[...]
        _acc(dg_ref, jnp.sum(du * xhat, axis=0, keepdims=True), tile == 0)

    return pl.pallas_call(
        body, name="bwd_a", grid=(n // tm,),
        in_specs=[_rows(tm, D_MODEL), _whole((1, D_MODEL)), _whole((D_MODEL, WP_COLS)), _rows(tm, PM_W), _rows(tm, RW_COLS),
                  _rows(tm, RW_COLS), _halo_next(tm, n), _rows(tm, D_MODEL), _rows(tm, D_MODEL)],
        out_specs=[_rows(tm, D_MODEL), _rows(tm, WP_COLS), _whole((1, D_MODEL))],
        out_shape=[_sds((n, D_MODEL)), _sds((n, WP_COLS), BF16), _sds((1, D_MODEL))],
        compiler_params=_ARB1,
    )(x2, g_pre, wp, dpm, dprw, dps, dps, dz, dxres)


def _dw_in(ut, dpb, tk, tn):
    n = ut.shape[1]
    steps = n // tk

    def body(u_ref, d_ref, o_ref, acc_sc):
        k = pl.program_id(1)
        _acc(acc_sc, jnp.dot(u_ref[...], d_ref[...], preferred_element_type=F32), k == 0)

        @pl.when(k == steps - 1)
        def _():
            o_ref[...] = acc_sc[...].astype(BF16)

    return pl.pallas_call(
        body, name="dw_in", grid=(WP_COLS // tn, steps),
        in_specs=[pl.BlockSpec((D_MODEL, tk), lambda j, k: (0, k)), pl.BlockSpec((tk, tn), lambda j, k: (k, j))],
        out_specs=pl.BlockSpec((D_MODEL, tn), lambda j, k: (0, j)),
        out_shape=_sds((D_MODEL, WP_COLS), BF16),
        scratch_shapes=[pltpu.VMEM((D_MODEL, tn), F32)],
        compiler_params=pltpu.CompilerParams(dimension_semantics=("arbitrary", "arbitrary")),
    )(ut, dpb)


ATT_BLK = 256
_NT = (((1,), (1,)), ((), ()))
_TN = (((0,), (0,)), ((), ()))


def _causal(q0, k0, blk, blk_k=None):
    blk_k = blk if blk_k is None else blk_k
    row = q0 + lax.broadcasted_iota(jnp.int32, (blk, blk_k), 0)
    col = k0 + lax.broadcasted_iota(jnp.int32, (blk, blk_k), 1)
    return row >= col


def _attn_fwd(qn, qr, kv, kr):
    bsz, t, _ = qn.shape
    blk = min(ATT_BLK, t)

    heads = range(MLA_HEADS)

    def body(qn_ref, qr_ref, kv_ref, kr_ref, o_ref, lse_ref):
        qi = pl.program_id(1)
        q = [jnp.concatenate([qn_ref[:, LANES * h:LANES * (h + 1)], qr_ref[:, LANES * h:LANES * (h + 1)]], axis=1) for h in heads]
        lower = _causal(0, 0, blk)

        def kv_step(j, carry, diagonal):
            ks = pl.multiple_of(j * blk, blk)
            k_rope = kr_ref[pl.ds(ks, blk), :]
            def score(h):
                k = jnp.concatenate([kv_ref[pl.ds(ks, blk), 2 * LANES * h:2 * LANES * h + LANES], k_rope], axis=1)
                return lax.dot_general(q[h], k, _NT, preferred_element_type=F32)

            out = []
            nxt = score(0)
            for h in heads:
                s = nxt * ATT_SCALE
                if h + 1 < MLA_HEADS:
                    nxt = score(h + 1)
                m, l, acc = carry[h]
                if diagonal:
                    s = jnp.where(lower, s, -1e30)
                m_new = jnp.maximum(m, jnp.max(s, axis=1, keepdims=True))
                alpha = jnp.exp(m - m_new)
                p = jnp.exp(s - m_new)
                l = alpha * l + jnp.sum(p, axis=1, keepdims=True)
                v = kv_ref[pl.ds(ks, blk), 2 * LANES * h + LANES:2 * LANES * (h + 1)]
                out.append((m_new, l, alpha * acc + jnp.dot(p.astype(BF16), v, preferred_element_type=F32)))
            return tuple(out)

        one = (jnp.full((blk, 1), -1e30, F32), jnp.zeros((blk, 1), F32), jnp.zeros((blk, MLA_V), F32))
        carry = lax.fori_loop(0, qi // 2, lambda pr, c: kv_step(2 * pr + 1, kv_step(2 * pr, c, False), False),
                              (one,) * MLA_HEADS)
        carry = lax.cond(qi % 2 == 1, lambda c: kv_step(qi - 1, c, False), lambda c: c, carry)
        carry = kv_step(qi, carry, True)
        for h in heads:
            m, l, acc = carry[h]
            o_ref[:, LANES * h:LANES * (h + 1)] = acc / l
            lse_ref[h] = jnp.broadcast_to(m + jnp.log(l), (blk, LANES))

    return pl.pallas_call(
        body, name="attn_fwd", grid=(bsz, t // blk),
        in_specs=[pl.BlockSpec((None, blk, MLA_WIDTH), lambda b, i: (b, i, 0)),
                  pl.BlockSpec((None, blk, MLA_WIDTH), lambda b, i: (b, i, 0)),
                  pl.BlockSpec((None, t, 2 * MLA_WIDTH), lambda b, i: (b, 0, 0)),
                  pl.BlockSpec((None, t, LANES), lambda b, i: (b, 0, 0))],
        out_specs=[pl.BlockSpec((None, blk, MLA_WIDTH), lambda b, i: (b, i, 0)),
                   pl.BlockSpec((None, MLA_HEADS, blk, LANES), lambda b, i: (b, 0, i, 0))],
        out_shape=[_sds((bsz, t, MLA_WIDTH)), _sds((bsz, MLA_HEADS, t, LANES))],
        compiler_params=pltpu.CompilerParams(dimension_semantics=("arbitrary", "arbitrary")),
    )(qn, qr, kv, kr)


def _attn_bwd(qn, qr, kv, kr, o, do, lse):
    bsz, t, _ = qn.shape
    blk = min(ATT_BLK, t)
    nb = t // blk
    assert nb % 2 == 0, "query blocks are taken in pairs"

    def body(qn_ref, qr_ref, kn_ref, kr_ref, v_ref, o_ref, do_ref, lse_ref, dqn_ref, dqr_ref, dkv_ref, dkr_ref, dq_sc, delta_sc):
        dq_sc[...] = jnp.zeros_like(dq_sc)
        delta_sc[...] = jnp.sum(do_ref[...].astype(F32) * o_ref[...], axis=1, keepdims=True)

        lower = _causal(0, 0, blk)

        def q_blocks(j, k, vb, carry, blocks, diagonal):
            dk, dv = carry
            us = range(len(blocks))
            qs = [i * blk if isinstance(i, int) else pl.multiple_of(i * blk, blk) for i in blocks]
            q = [jnp.concatenate([qn_ref[pl.ds(qs[u], blk), :], qr_ref[pl.ds(qs[u], blk), :]], axis=1) for u in us]
            dob = [do_ref[pl.ds(qs[u], blk), :] for u in us]
            s = [lax.dot_general(q[u], k, _NT, preferred_element_type=F32) for u in us]
            dp = [lax.dot_general(dob[u], vb, _NT, preferred_element_type=F32) for u in us]
            pb, ds = [], []
            for u in us:
                p = jnp.exp(s[u] * ATT_SCALE - lse_ref[pl.ds(qs[u], blk), 0:1])
                if diagonal == u:
                    p = jnp.where(lower, p, 0.0)
                pb.append(p.astype(BF16))
                ds.append((p * (dp[u] - delta_sc[pl.ds(qs[u], blk), :]) * ATT_SCALE).astype(BF16))
            for u in us:
                dv = dv + lax.dot_general(pb[u], dob[u], _TN, preferred_element_type=F32)
            for u in us:
                dq_sc[pl.ds(qs[u], blk), :] += jnp.dot(ds[u], k, preferred_element_type=F32)
                dk = dk + lax.dot_general(ds[u], q[u], _TN, preferred_element_type=F32)
            return dk, dv

        for j in range(nb):
            ks = j * blk
            k = jnp.concatenate([kn_ref[ks:ks + blk, :], kr_ref[ks:ks + blk, :]], axis=1)
            vb = v_ref[ks:ks + blk, :]
            carry = (jnp.zeros((blk, 2 * LANES), F32), jnp.zeros((blk, MLA_V), F32))
            if j % 2 == 0:
                carry = q_blocks(j, k, vb, carry, [j, j + 1], 0)
            else:
                carry = q_blocks(j, k, vb, carry, [j], 0)
            pairs_from = j // 2 + 1
            if nb // 2 - pairs_from > 0:
                carry = lax.fori_loop(pairs_from, nb // 2,
                                      lambda pr, c, j=j, k=k, vb=vb: q_blocks(j, k, vb, c, [2 * pr, 2 * pr + 1], None),
                                      carry, unroll=2)
            dk, dv = carry
            dkv_ref[ks:ks + blk, 0:LANES] = dk[:, 0:LANES]
            dkv_ref[ks:ks + blk, LANES:2 * LANES] = dv
            dkr_ref[ks:ks + blk, :] = dk[:, LANES:2 * LANES]
        dqn_ref[...] = dq_sc[:, 0:LANES]
        dqr_ref[...] = dq_sc[:, LANES:2 * LANES]

    head_col = lambda b, h: (b, 0, h)
    return pl.pallas_call(
        body, name="attn_bwd", grid=(bsz, MLA_HEADS),
        in_specs=[pl.BlockSpec((None, t, LANES), head_col), pl.BlockSpec((None, t, LANES), head_col),
                  pl.BlockSpec((None, t, LANES), lambda b, h: (b, 0, 2 * h)),
                  pl.BlockSpec((None, t, LANES), lambda b, h: (b, 0, 0)),
                  pl.BlockSpec((None, t, LANES), lambda b, h: (b, 0, 2 * h + 1)),
                  pl.BlockSpec((None, t, LANES), head_col), pl.BlockSpec((None, t, LANES), head_col),
                  pl.BlockSpec((None, None, t, LANES), lambda b, h: (b, h, 0, 0))],
        out_specs=[pl.BlockSpec((None, t, LANES), head_col), pl.BlockSpec((None, t, LANES), head_col),
                   pl.BlockSpec((None, t, 2 * LANES), head_col),
                   pl.BlockSpec((None, None, t, LANES), lambda b, h: (h, b, 0, 0))],
        out_shape=[_sds((bsz, t, MLA_WIDTH)), _sds((bsz, t, MLA_WIDTH)), _sds((bsz, t, 2 * MLA_WIDTH)),
                   _sds((MLA_HEADS, bsz, t, LANES))],
        scratch_shapes=[pltpu.VMEM((t, 2 * LANES), F32), pltpu.VMEM((t, 1), F32)],
        compiler_params=pltpu.CompilerParams(dimension_semantics=("arbitrary", "arbitrary")),
    )(qn, qr, kv, kr, kv, o, do, lse)


SCAN_CHUNK = 16


def _diag_mask():
    row = lax.broadcasted_iota(jnp.int32, (RW_HEAD, RW_WIDTH), 0)
    lane = lax.broadcasted_iota(jnp.int32, (RW_HEAD, RW_WIDTH), 1)
    return jnp.where(row == (lane & (RW_HEAD - 1)), 1.0, 0.0)


def _time_minor(a):
    bsz, t, _ = a.shape
    a = a.reshape(bsz, t // SCAN_CHUNK, SCAN_CHUNK, RW_HEADS, RW_HEAD)
    return a.transpose(0, 1, 4, 3, 2).reshape(bsz, t // SCAN_CHUNK, RW_HEAD, RW_HEADS * SCAN_CHUNK)


def _head_expand():
    l = lax.broadcasted_iota(jnp.int32, (2 * LANES, RW_WIDTH), 0)
    n = lax.broadcasted_iota(jnp.int32, (2 * LANES, RW_WIDTH), 1)
    return jnp.where(((l & (LANES - 1)) >> 4) == (n >> 6), 1.0, 0.0).astype(BF16)


BCAST_GROUP = 4


def _outer_chunk(tm_ref, row_ref, out_sc, expand, seqs):
    step_of_lane = lax.broadcasted_iota(jnp.int32, (RW_HEAD, LANES), 1) & (SCAN_CHUNK - 1)
    tiles = [tm_ref[bi, 0] for bi in seqs]
    for t0 in range(0, SCAN_CHUNK, BCAST_GROUP):
        parts = []
        for t in range(t0, t0 + BCAST_GROUP):
            for tile in tiles:
                a = jnp.where(step_of_lane == t, tile, 0.0)
                hi = a.astype(BF16)
                parts.append(jnp.concatenate([hi, (a - hi.astype(F32)).astype(BF16)], axis=1))
        cols = jnp.dot(jnp.concatenate(parts, axis=0), expand, preferred_element_type=F32)
        for j, t in enumerate(range(t0, t0 + BCAST_GROUP)):
            base = j * RW_HEAD * len(seqs)
            out_sc[t] = jnp.concatenate([cols[base + RW_HEAD * bi:base + RW_HEAD * (bi + 1)] * row_ref[bi, t:t + 1, :]
                                         for bi in seqs], axis=0)


def _fold8(x):
    acc = x[0:8]
    for j in range(1, x.shape[0] // 8):
        acc = acc + x[8 * j:8 * (j + 1)]
    return acc


def _rows8(at):
    return pl.ds(at * 8 if isinstance(at, int) else pl.multiple_of(at * 8, 8), 8)


def _put8(sc, bi, at, val):
    for j in range(RW_WIDTH // LANES):
        sc[bi * (RW_WIDTH // LANES) + j, _rows8(at), :] = val[:, LANES * j:LANES * (j + 1)]


def _unfold8(sc, bi, steps):
    tiles = []
    for j in range(RW_WIDTH // LANES):
        view = sc.at[bi * (RW_WIDTH // LANES) + j]
        acc = view[pl.ds(0, steps, stride=8), :]
        for s in range(1, 8):
            acc = acc + view[pl.ds(s, steps, stride=8), :]
        tiles.append(acc)
    return jnp.concatenate(tiles, axis=1)


def _scan_fwd(r, w, k, vt, nkk, b):
    bsz, t, _ = r.shape
    tc = SCAN_CHUNK

    def body(r_ref, w_ref, k_ref, n_ref, b_ref, vt_ref, y_ref, st_ref, s_sc, vc_sc, y_sc):
        @pl.when(pl.program_id(0) == 0)
        def _():
            s_sc[...] = jnp.zeros_like(s_sc)

        ones = _seg_ones()
        diag = _diag_mask()
        seqs = range(bsz)
        _outer_chunk(vt_ref, k_ref, vc_sc, _head_expand(), seqs)

        def put_y(ya, at):
            for bi in seqs:
                _put8(y_sc, bi, at, _fold8(ya[bi] * diag))

        def step(i, _):
            row = lambda ref, bi: ref[bi, pl.ds(i, 1), :]
            prev = jnp.maximum(i - 1, 0)
            s_old = [s_sc[bi] for bi in seqs]
            s_b = [s_old[bi].astype(BF16) for bi in seqs]
            sa = _seg_multi([s_b[bi] * row(n_ref, bi).astype(BF16) for bi in seqs], ones, 1)
            put_y(_seg_multi([s_b[bi] * r_ref[bi, pl.ds(prev, 1), :].astype(BF16) for bi in seqs], ones, 1), prev)
            vk = vc_sc[i]
            for bi in seqs:
                s_new = s_old[bi] * row(w_ref, bi) + sa[bi] * row(b_ref, bi) + vk[RW_HEAD * bi:RW_HEAD * (bi + 1)]
                s_sc[bi] = s_new
                st_ref[bi, i] = s_new
            return 0

        lax.fori_loop(0, tc, step, 0, unroll=8)
        put_y(_seg_multi([s_sc[bi].astype(BF16) * r_ref[bi, tc - 1:tc, :].astype(BF16) for bi in seqs], ones, 1), tc - 1)
        for bi in seqs:
            y_ref[bi] = _unfold8(y_sc, bi, tc)

    vec = pl.BlockSpec((bsz, tc, RW_WIDTH), lambda c: (0, c, 0))
    return pl.pallas_call(
        body, name="scan_fwd", grid=(t // tc,),
        in_specs=[vec] * 5 + [pl.BlockSpec((bsz, 1, RW_HEAD, LANES), lambda c: (0, c, 0, 0))],
        out_specs=[vec, pl.BlockSpec((bsz, tc, RW_HEAD, RW_WIDTH), lambda c: (0, c, 0, 0))],
        out_shape=[_sds((bsz, t, RW_WIDTH)), _sds((bsz, t, RW_HEAD, RW_WIDTH))],
        scratch_shapes=[pltpu.VMEM((bsz, RW_HEAD, RW_WIDTH), F32), pltpu.VMEM((tc, bsz * RW_HEAD, RW_WIDTH), F32),
                        pltpu.VMEM((bsz * RW_WIDTH // LANES, tc * 8, LANES), F32)],
        compiler_params=_ARB1,
    )(r, w, k, nkk, b, vt)


def _own_head_row(x):
    first_half = lax.broadcasted_iota(jnp.int32, (1, LANES), 1) < RW_HEAD
    tiles = [jnp.where(first_half, x[2 * j:2 * j + 1, LANES * j:LANES * (j + 1)], x[2 * j + 1:2 * j + 2, LANES * j:LANES * (j + 1)])
             for j in range(RW_WIDTH // LANES)]
    return jnp.concatenate(tiles, axis=1)


def _scan_bwd(r, w, k, vt, nkk, b, st, dyt):
    bsz, t, _ = r.shape
    tc = SCAN_CHUNK
    nc = t // tc

    def body(r_ref, w_ref, k_ref, n_ref, b_ref, vt_ref, dyt_ref, st_ref, halo_ref,
             dr_ref, dw_ref, dk_ref, dv_ref, dn_ref, db_ref, g_sc, dc_sc, v8_sc, dy8_sc, *part_scs):
        c = pl.program_id(0)

        @pl.when(c == 0)
        def _():
            g_sc[...] = jnp.zeros_like(g_sc)

        ones = _seg_ones()
        diag = _diag_mask()
        has_prev = jnp.where(c == nc - 1, 0.0, 1.0)
        seqs = range(bsz)
        _outer_chunk(dyt_ref, r_ref, dc_sc, _head_expand(), seqs)
        for bi in seqs:
            v8_sc[bi] = jnp.concatenate([vt_ref[bi, 0].T] * 2, axis=1)
            dy8_sc[bi] = jnp.concatenate([dyt_ref[bi, 0].T] * 2, axis=1)
        by_head = lambda sc, bi, i: sc.at[bi][pl.ds(i, RW_HEADS, stride=SCAN_CHUNK), :][:, 0:RW_HEAD].astype(BF16)
        dw_sc, dv_sc, dn_sc, db_sc = part_scs

        def step(i, s_p, s_t_b=None):
            static = isinstance(i, int)
            row = lambda ref, bi: ref[bi, i:i + 1, :] if static else ref[bi, pl.ds(i, 1), :]
            put_row = lambda ref, bi, val: ref.__setitem__((bi, slice(i, i + 1) if static else pl.ds(i, 1), slice(None)), val)
            if s_t_b is None:
                s_t_b = [st_ref[bi, i].astype(BF16) for bi in seqs]
            s_p_b = [s_p[bi].astype(BF16) for bi in seqs]
            dr8 = [jnp.dot(by_head(dy8_sc, bi, i), s_t_b[bi], preferred_element_type=F32) for bi in seqs]
            rowb = lambda ref, bi: row(ref, bi).astype(BF16)
            sa = _seg_multi([s_p_b[bi] * rowb(n_ref, bi) for bi in seqs], ones, 1)
            dc_all = dc_sc[i]
            dc = [dc_all[RW_HEAD * bi:RW_HEAD * (bi + 1)] for bi in seqs]
            g = [g_sc[bi] + dc[bi] for bi in seqs]
            g_b = [g[bi].astype(BF16) for bi in seqs]
            res = _seg_multi([g_b[bi] * rowb(b_ref, bi) for bi in seqs] + [g_b[bi] * rowb(k_ref, bi) for bi in seqs], ones, 1)
            dsa, dvb = res[:bsz], res[bsz:]
            for bi in seqs:
                dk8 = jnp.dot(by_head(v8_sc, bi, i), g_b[bi], preferred_element_type=F32)
                put_row(dr_ref, bi, _own_head_row(dr8[bi]))
                put_row(dk_ref, bi, _own_head_row(dk8))
                _put8(dv_sc, bi, i, _fold8(dvb[bi] * diag))
                _put8(dw_sc, bi, i, _fold8(g[bi] * s_p[bi]))
                _put8(db_sc, bi, i, _fold8(g[bi] * sa[bi]))
                _put8(dn_sc, bi, i, _fold8(s_p[bi] * dsa[bi]))
                g_sc[bi] = g[bi] * row(w_ref, bi) + dsa[bi] * row(n_ref, bi)
            return s_p_b

        group = 5

        def loop_trip(trip, _):
            top = tc - 1 - trip * group
            s_b = None
            for u in range(group):
                i = top - u
                s_b = step(i, [st_ref[bi, i - 1] for bi in seqs], s_b)
            return 0

        lax.fori_loop(0, (tc - 1) // group, loop_trip, 0)
        step(0, [halo_ref[bi, 0] * has_prev for bi in seqs])
        for out_ref, sc in zip((dw_ref, dv_ref, dn_ref, db_ref), part_scs):
            for bi in seqs:
                out_ref[bi] = _unfold8(sc, bi, tc)

    vec = pl.BlockSpec((bsz, tc, RW_WIDTH), lambda c: (0, nc - 1 - c, 0))
    tmin = pl.BlockSpec((bsz, 1, RW_HEAD, LANES), lambda c: (0, nc - 1 - c, 0, 0))
    parts = pltpu.VMEM((bsz * RW_WIDTH // LANES, tc * 8, LANES), F32)
    heads_steps = pltpu.VMEM((bsz, LANES, LANES), F32)
    return pl.pallas_call(
        body, name="scan_bwd", grid=(nc,),
        in_specs=[vec] * 5 + [tmin, tmin,
                              pl.BlockSpec((bsz, tc, RW_HEAD, RW_WIDTH), lambda c: (0, nc - 1 - c, 0, 0)),
                              pl.BlockSpec((bsz, 1, RW_HEAD, RW_WIDTH), lambda c: (0, jnp.maximum((nc - 1 - c) * tc - 1, 0), 0, 0))],
        out_specs=[vec] * 6,
        out_shape=[_sds((bsz, t, RW_WIDTH))] * 6,
        scratch_shapes=[pltpu.VMEM((bsz, RW_HEAD, RW_WIDTH), F32), pltpu.VMEM((tc, bsz * RW_HEAD, RW_WIDTH), F32),
                        heads_steps, heads_steps] + [parts] * 4,
        compiler_params=_ARB1,
    )(r, w, k, nkk, b, vt, dyt, st, st)


TOKEN_TILE = 256
VJP_TILE = 256


def _padded_weights(wt):
    f = lambda a: a.astype(F32)
    w_in = wt["w_in"][0].astype(BF16)
    zeros = lambda r, c: jnp.zeros((r, c), F32)
    wp = jnp.concatenate([w_in[:, :MLA_COLS], jnp.zeros((D_MODEL, PM_W - MLA_COLS), BF16), w_in[:, MLA_COLS:]], axis=1)
    w_uq = f(wt["mla_w_uq"][0]).reshape(Q_LORA, MLA_HEADS, MLA_NOPE + MLA_ROPE)
    wqn = w_uq[:, :, :MLA_NOPE].reshape(Q_LORA, MLA_HEADS * MLA_NOPE)
    wqr = jnp.concatenate([w_uq[:, :, MLA_NOPE:], jnp.zeros((Q_LORA, MLA_HEADS, LANES - MLA_ROPE), F32)], axis=2)
    wqr = wqr.reshape(Q_LORA, MLA_HEADS * LANES)
    w2p = jnp.concatenate([f(wt["rw_w2"][0]), zeros(LORA, RW_WIDTH)], axis=0)
    a2p = jnp.concatenate([zeros(LORA, RW_WIDTH), f(wt["rw_a2"][0])], axis=0)
    bw = (f(wt["mla_q_norm_g"]), wqn, wqr, f(wt["mla_kv_norm_g"]), f(wt["mla_w_ukv"][0]), f(wt["rw_mu"]), f(wt["rw_w0"]),
          w2p, f(wt["rw_a0"]), a2p, f(wt["rw_k_k"]), f(wt["rw_k_a"]))
    hw = (f(wt["rw_ln_g"]), f(wt["rw_ln_b"]), f(wt["rw_r_k"]).reshape(1, RW_WIDTH), f(wt["w_out"][0]), f(wt["norm_post_g"]))
    return wp, bw, hw


def _local_step(x, positions, target, wt):
    bsz, t, _ = x.shape
    n = bsz * t
    tm = min(TOKEN_TILE, t)
    tps = t // tm
    ts = min(VJP_TILE, t)
    wp, bw, hw = _padded_weights(wt)
    wpb = wp.astype(BF16)
    g_pre = wt["norm_pre_g"].astype(F32)
    x2 = x.reshape(n, D_MODEL)
    tgt2 = target.reshape(n, D_MODEL)
    inv_freq = ROPE_THETA ** (-jnp.arange(0, MLA_ROPE, 2, dtype=F32) / MLA_ROPE)
    invf = jnp.tile(inv_freq, LANES // (MLA_ROPE // 2)).reshape(LANES, 1)
    cos, sin = _rope_tables(positions.reshape(1, n), invf, tm)

    u, pm, prw, z = _fwd_a(x2, g_pre, wpb, tm)
    qn, qr, kv, kr, r, w, k, v, nkk, b = _fwd_b(pm, prw, cos, sin, bw, tm, tps)
    b3 = lambda a: a.reshape(bsz, t, a.shape[-1])
    ym, lse = _attn_fwd(b3(qn), b3(qr), b3(kv), b3(kr))
    vt = _time_minor(b3(v))
    ys, st = _scan_fwd(b3(r), b3(w), b3(k), vt, b3(nkk), b3(b))
    (dys, dr_h, dk_h, dv_h, dym, dz, dxres, loss, d_lng, d_lnb, d_rk, d_wout, d_gpost) = _head(
        ys.reshape(n, RW_WIDTH), r, k, v, ym.reshape(n, MLA_WIDTH), z, x2, tgt2, hw, ts)
    dqn, dqr, dkv, dkr_heads = _attn_bwd(b3(qn), b3(qr), b3(kv), b3(kr), ym, b3(dym), lse)
    dr_s, dw_s, dk_s, dv_s, dn_s, db_s = _scan_bwd(b3(r), b3(w), b3(k), vt, b3(nkk), b3(b), st, _time_minor(b3(dys)))
    f2 = lambda a: a.reshape(n, a.shape[-1])
    cts = (f2(dqn), f2(dqr), f2(dkv), f2(dr_s), dr_h, f2(dw_s), f2(dk_s), dk_h, f2(dv_s), dv_h, f2(dn_s), f2(db_s))
    (dpm, dprw, dps, d_gq, d_wqn, d_wqr, d_gkv, d_wkv, d_mu, d_w0, d_w2p, d_a0, d_a2p, d_kk, d_ka) = _bwd_b(
        pm, prw, cos, sin, bw, cts, dkr_heads.reshape(MLA_HEADS, n, LANES), ts, t // ts)
    grad_x, dpb, d_gpre = _bwd_a(x2, g_pre, wpb, dpm, dprw, dps, dz, dxres, tm, tps)
    d_wp = _dw_in(u, dpb, min(1024, n), 640)

    d_w_in = jnp.concatenate([d_wp[:, :MLA_COLS], d_wp[:, PM_W:]], axis=1)
    d_w_uq = jnp.concatenate([d_wqn.reshape(Q_LORA, MLA_HEADS, MLA_NOPE),
                              d_wqr.reshape(Q_LORA, MLA_HEADS, LANES)[:, :, :MLA_ROPE]], axis=2)
    grads = {
        "norm_pre_g": d_gpre, "w_in": d_w_in[None], "mla_q_norm_g": d_gq,
        "mla_w_uq": d_w_uq.reshape(1, Q_LORA, MLA_HEADS * (MLA_NOPE + MLA_ROPE)), "mla_kv_norm_g": d_gkv,
        "mla_w_ukv": d_wkv[None], "rw_mu": d_mu, "rw_w0": d_w0, "rw_w2": d_w2p[None, :LORA], "rw_a0": d_a0,
        "rw_a2": d_a2p[None, LORA:], "rw_k_k": d_kk, "rw_k_a": d_ka, "rw_r_k": d_rk.reshape(1, RW_HEADS, RW_HEAD),
        "rw_ln_g": d_lng, "rw_ln_b": d_lnb, "w_out": d_wout[None], "norm_post_g": d_gpost,
    }
    return loss, grad_x.reshape(bsz, t, D_MODEL), grads


_MESH = pl.DeviceIdType.MESH


def _gather_shards(shards):
    na = len(shards)

    def body(*refs):
        x_refs, out_refs = refs[:na], refs[na:2 * na]
        send_sems, recv_sems, local_sems = refs[2 * na:]
        x, y, c = lax.axis_index("x"), lax.axis_index("y"), lax.axis_index("c")
        me, sibling = (x, y, c), (x, y, 1 - c)
        chips = [(1 - x, y), (x, 1 - y), (1 - x, 1 - y)]
        arrays = range(na)

        def slot(a, px, py, pc):
            return out_refs[a].at[4 * px + 2 * py + pc]

        def copy(k, a, block, to, src=None):
            return pltpu.make_async_remote_copy(
                src_ref=slot(a, *block) if src is None else src, dst_ref=slot(a, *block),
                send_sem=send_sems.at[k, a], recv_sem=recv_sems.at[k, a], device_id=to, device_id_type=_MESH)

        mine = [pltpu.make_async_copy(x_refs[a], slot(a, *me), local_sems.at[a]) for a in arrays]
        for cp in mine:
            cp.start()
        first = [copy(0, a, me, sibling, src=x_refs[a]) for a in arrays]
        first += [copy(1 + j, a, me, (*chip, c), src=x_refs[a]) for j, chip in enumerate(chips) for a in arrays]
        for cp in first:
            cp.start()
        passed = []
        for j, chip in enumerate(chips):
            for a in arrays:
                copy(1 + j, a, (*chip, c), me).wait_recv()
                passed.append(copy(4 + j, a, (*chip, c), sibling))
                passed[-1].start()
        for a in arrays:
            copy(0, a, sibling, me).wait_recv()
        for j, chip in enumerate(chips):
            for a in arrays:
                copy(4 + j, a, (*chip, 1 - c), me).wait_recv()
        for cp in first + passed:
            cp.wait_send()
        for cp in mine:
            cp.wait()

    vmem = pl.BlockSpec(memory_space=pltpu.VMEM)
    return pl.pallas_call(
        body, name="gather_shards",
        out_shape=[_sds((N_DEV,) + a.shape, a.dtype) for a in shards],
        in_specs=[vmem] * na, out_specs=[vmem] * na,
        scratch_shapes=[pltpu.SemaphoreType.DMA((7, na)), pltpu.SemaphoreType.DMA((7, na)), pltpu.SemaphoreType.DMA((na,))],
    )(*shards)


SMALL_LANES = SMALL_N + LANES


N_CHIP = 4


def _exchange_grads(big_blocks, small_grads, loss_tile):
    nb = len(big_blocks)
    ns = len(small_grads)

    def body(*refs):
        big, small, loss_ref = refs[:nb], refs[nb:nb + ns], refs[nb + ns]
        out, rsmall = refs[nb + ns + 1:2 * nb + ns + 1], refs[2 * nb + ns + 1]
        scratch = refs[2 * nb + ns + 2:]
        stage, sums = scratch[:nb], scratch[nb:2 * nb]
        send1, recv1, send2, recv2, send_s, recv_s, row_sc = scratch[2 * nb:]
        x, y, c = lax.axis_index("x"), lax.axis_index("y"), lax.axis_index("c")
        me_lin = 4 * x + 2 * y + c
        my_chip = 2 * x + y
        sibling = (x, y, 1 - c)
        leaves = range(nb)

        to_sibling = [pltpu.make_async_remote_copy(
            src_ref=big[j].at[1 - c], dst_ref=stage[j], send_sem=send1.at[j], recv_sem=recv1.at[j],
            device_id=sibling, device_id_type=_MESH) for j in leaves]
        for cp in to_sibling:
            cp.start()
        off = 0
        for ref, (_, cnt) in zip(small, SMALL):
            row_sc[:, off:off + cnt] = ref[...]
            off += cnt
        row_sc[:, off:off + LANES] = loss_ref[0:1, :]
        rsmall[me_lin] = row_sc[...]
        rows = []
        for k in range(1, N_DEV):
            peer = (x ^ (k >> 2), y ^ ((k >> 1) & 1), c ^ (k & 1))
            rows.append(pltpu.make_async_remote_copy(
                src_ref=row_sc, dst_ref=rsmall.at[me_lin], send_sem=send_s.at[k - 1], recv_sem=recv_s.at[k - 1],
                device_id=peer, device_id_type=_MESH))
        for cp in rows:
            cp.start()
        to_chips = []
        for j in leaves:
            to_sibling[j].wait_recv()
            sums[j][...] = (big[j][c].astype(F32) + stage[j][...].astype(F32)).astype(BF16)
            out[j][0] = sums[j][my_chip]
            for q in range(1, N_CHIP):
                px, py = x ^ (q >> 1), y ^ (q & 1)
                to_chips.append(pltpu.make_async_remote_copy(
                    src_ref=sums[j].at[2 * px + py], dst_ref=out[j].at[q], send_sem=send2.at[q - 1, j],
                    recv_sem=recv2.at[q - 1, j], device_id=(px, py, c), device_id_type=_MESH))
                to_chips[-1].start()
        for cp in to_chips + rows:
            cp.wait_recv()
        for cp in to_sibling + to_chips + rows:
            cp.wait_send()

    vmem = pl.BlockSpec(memory_space=pltpu.VMEM)
    shard = [a.shape[2:] for a in big_blocks]
    return pl.pallas_call(
        body, name="exchange_grads",
        out_shape=[_sds((N_CHIP,) + s, BF16) for s in shard] + [_sds((N_DEV, 1, SMALL_LANES))],
        in_specs=[vmem] * (nb + ns + 1), out_specs=[vmem] * (nb + 1),
        scratch_shapes=[pltpu.VMEM((N_CHIP,) + s, BF16) for s in shard] * 2
        + [pltpu.SemaphoreType.DMA((nb,)), pltpu.SemaphoreType.DMA((nb,)),
           pltpu.SemaphoreType.DMA((N_CHIP - 1, nb)), pltpu.SemaphoreType.DMA((N_CHIP - 1, nb)),
           pltpu.SemaphoreType.DMA((N_DEV - 1,)), pltpu.SemaphoreType.DMA((N_DEV - 1,)), pltpu.VMEM((1, SMALL_LANES), F32)],
    )(*big_blocks, *small_grads, loss_tile)


def _adamw_math(w, g, m, v):
    m = ADAM_B1 * m + (1.0 - ADAM_B1) * g
    v = ADAM_B2 * v + (1.0 - ADAM_B2) * (g * g)
    m_hat = m / (1.0 - ADAM_B1 ** ADAM_STEP)
    v_hat = v / (1.0 - ADAM_B2 ** ADAM_STEP)
    return -ADAM_LR * (m_hat / (jnp.sqrt(v_hat) + ADAM_EPS) + ADAM_WD * w), m, v


def _reduce_adamw(name, parts, w, m, v, row_blocks):
    slots, rows, cols = parts.shape
    rb = rows // row_blocks

    def body(p_ref, w_ref, m_ref, v_ref, g_out, d_out, m_out, v_out):
        g = p_ref[0].astype(F32)
        for s in range(1, slots):
            g = g + p_ref[s].astype(F32)
        g_out[0] = g
        d_out[0], m_out[0], v_out[0] = _adamw_math(w_ref[0], g, m_ref[0], v_ref[0])

    blk = pl.BlockSpec((1, rb, cols), lambda i: (0, i, 0))
    return pl.pallas_call(
        body, name="reduce_adamw_" + name, grid=(row_blocks,),
        in_specs=[pl.BlockSpec((slots, rb, cols), lambda i: (0, i, 0)), blk, blk, blk],
        out_specs=[blk] * 4, out_shape=[_sds((1, rows, cols))] * 4,
        compiler_params=_ARB1,
    )(parts, w, m, v)


def _reduce_adamw_small(rows, ws, ms, vs):
    ns = len(SMALL)

    def body(r_ref, *refs):
        w_refs, m_refs, v_refs, outs = refs[:ns], refs[ns:2 * ns], refs[2 * ns:3 * ns], refs[3 * ns:]
        total = r_ref[0]
        for s in range(1, N_DEV):
            total = total + r_ref[s]
        off = 0
        for j, (_, cnt) in enumerate(SMALL):
            g = total[:, off:off + cnt]
            off += cnt
            outs[4 * j][...] = g
            outs[4 * j + 1][...], outs[4 * j + 2][...], outs[4 * j + 3][...] = _adamw_math(
                w_refs[j][...], g, m_refs[j][...], v_refs[j][...])
        outs[4 * ns][...] = total[:, off:off + LANES]

    vmem = pl.BlockSpec(memory_space=pltpu.VMEM)
    return pl.pallas_call(
        body, name="reduce_adamw_small",
        in_specs=[vmem] * (1 + 3 * ns), out_specs=[vmem] * (4 * ns + 1),
        out_shape=[_sds((1, cnt)) for _, cnt in SMALL for _ in range(4)] + [_sds((1, LANES))],
    )(rows, *ws, *ms, *vs)


def _shard_blocks(name, full):
    a = full[0]
    rows, cols = a.shape
    if name == "w_out":
        return a.reshape(N_CHIP, 2, rows // N_DEV, cols).transpose(1, 0, 2, 3)
    return a.reshape(rows, N_CHIP, 2, cols // N_DEV).transpose(2, 1, 0, 3)


def _unshard(name, blocks):
    _, rows, cols = blocks.shape
    if name == "w_out":
        return blocks.reshape(1, N_DEV * rows, cols)
    return blocks.transpose(1, 0, 2).reshape(1, rows, N_DEV * cols)


def kernel(x, positions, norm_pre_g, w_in, mla_q_norm_g, mla_w_uq, mla_kv_norm_g, mla_w_ukv, rw_mu, rw_w0, rw_w2, rw_a0, rw_a2, rw_k_k, rw_k_a, rw_r_k, rw_ln_g, rw_ln_b, w_out, norm_post_g, loss_target, m_norm_pre_g, m_w_in, m_mla_q_norm_g, m_mla_w_uq, m_mla_kv_norm_g, m_mla_w_ukv, m_rw_mu, m_rw_w0, m_rw_w2, m_rw_a0, m_rw_a2, m_rw_k_k, m_rw_k_a, m_rw_r_k, m_rw_ln_g, m_rw_ln_b, m_w_out, m_norm_post_g, v_norm_pre_g, v_w_in, v_mla_q_norm_g, v_mla_w_uq, v_mla_kv_norm_g, v_mla_w_ukv, v_rw_mu, v_rw_w0, v_rw_w2, v_rw_a0, v_rw_a2, v_rw_k_k, v_rw_k_a, v_rw_r_k, v_rw_ln_g, v_rw_ln_b, v_w_out, v_norm_post_g):
    given = dict(locals())
    w = {nm: given[nm] for nm in WEIGHTS}
    mom = {nm: given["m_" + nm] for nm in WEIGHTS}
    var = {nm: given["v_" + nm] for nm in WEIGHTS}
    sharded = list(SHARDED)

    gathered = _gather_shards([w[nm][0].astype(BF16) for nm in sharded])
    full = dict(w)
    for nm, blocks in zip(sharded, gathered):
        full[nm] = _unshard(nm, blocks)

    loss_part, grad_x, grads = _local_step(x, positions, loss_target, full)

    small_names = [nm for nm, _ in SMALL]
    row = lambda a: a.reshape(1, -1)
    got = _exchange_grads([_shard_blocks(nm, grads[nm]).astype(BF16) for nm in sharded],
                          [row(grads[nm]) for nm in small_names], loss_part)
    new = {}
    for nm, parts in zip(sharded, got[:-1]):
        new[nm] = _reduce_adamw(nm, parts, w[nm], mom[nm], var[nm], 4 if nm == "w_in" else 1)
    res = _reduce_adamw_small(got[-1], [row(w[nm]) for nm in small_names], [row(mom[nm]) for nm in small_names],
                              [row(var[nm]) for nm in small_names])
    for j, nm in enumerate(small_names):
        new[nm] = tuple(a.reshape(w[nm].shape) for a in res[4 * j:4 * j + 4])
    loss = res[-1][0, 0]
    return (loss, grad_x, *[new[nm][j] for j in range(4) for nm in WEIGHTS])
```

```python
import jax
import jax.numpy as jnp
from jax import lax
from jax.experimental import pallas as pl
from jax.experimental.pallas import tpu as pltpu

F32 = jnp.float32
BF16 = jnp.bfloat16

D_MODEL = 1024
MLA_HEADS = 4
MLA_NOPE = 128
MLA_ROPE = 64
MLA_V = 128
MLA_WIDTH = MLA_HEADS * MLA_V
Q_LORA = 256
KV_LORA = 128
ROPE_THETA = 10000.0
RW_HEAD = 64
RW_WIDTH = 512
RW_HEADS = RW_WIDTH // RW_HEAD
LORA = 64
RW_COLS = 3 * RW_WIDTH + 2 * LORA
MLA_COLS = Q_LORA + KV_LORA + MLA_ROPE
RW_GN_EPS = 64e-5
NORM_EPS = 1e-6
ATT_SCALE = (MLA_NOPE + MLA_ROPE) ** -0.5
ADAM_LR, ADAM_B1, ADAM_B2, ADAM_EPS, ADAM_WD, ADAM_STEP = 0.001, 0.9, 0.999, 1e-08, 0.01, 10
N_DEV = 8
LANES = 128
MXU = 256

PM_W = 512
WP_COLS = PM_W + RW_COLS + D_MODEL
RW_PIECES = ((0, 512), (512, 1024), (1024, 1536), (1536, 1664))

SHARDED = ("w_in", "mla_w_uq", "mla_w_ukv", "rw_w2", "rw_a2", "w_out")
SMALL = (("norm_pre_g", 1024), ("mla_q_norm_g", 256), ("mla_kv_norm_g", 128), ("rw_mu", 1664), ("rw_w0", 512),
         ("rw_a0", 512), ("rw_k_k", 512), ("rw_k_a", 512), ("rw_r_k", 512), ("rw_ln_g", 512), ("rw_ln_b", 512),
         ("norm_post_g", 1024))
SMALL_N = sum(n for _, n in SMALL)
WEIGHTS = ("norm_pre_g", "w_in", "mla_q_norm_g", "mla_w_uq", "mla_kv_norm_g", "mla_w_ukv", "rw_mu", "rw_w0", "rw_w2",
           "rw_a0", "rw_a2", "rw_k_k", "rw_k_a", "rw_r_k", "rw_ln_g", "rw_ln_b", "w_out", "norm_post_g")


def _seg_ones():
    r = lax.broadcasted_iota(jnp.int32, (MXU, MXU), 0) >> 6
    c = lax.broadcasted_iota(jnp.int32, (MXU, MXU), 1) >> 6
    return jnp.where(r == c, 1.0, 0.0).astype(BF16)


def _seg_dot(x, ones, passes):
    parts, rem = [], x
    for p in range(passes):
        hb = rem.astype(BF16)
        parts.append(hb)
        if p + 1 < passes:
            rem = rem - hb.astype(F32)
    outs = []
    for j in range(x.shape[1] // MXU):
        acc = None
        for hb in parts:
            d = jnp.dot(hb[:, MXU * j:MXU * (j + 1)], ones, preferred_element_type=F32)
            acc = d if acc is None else acc + d
        outs.append(acc)
    return outs[0] if len(outs) == 1 else jnp.concatenate(outs, axis=1)


def _seg_multi(xs, ones, passes):
    his = [x.astype(BF16) for x in xs]
    hi = jnp.concatenate(his, axis=0)
    if passes == 2:
        lo = jnp.concatenate([(x - h.astype(F32)).astype(BF16) for x, h in zip(xs, his)], axis=0)
        rhs = jnp.concatenate([ones, ones], axis=0)
    halves = []
    for j in range(hi.shape[1] // MXU):
        sl = slice(MXU * j, MXU * (j + 1))
        if passes == 2:
            halves.append(jnp.dot(jnp.concatenate([hi[:, sl], lo[:, sl]], axis=1), rhs, preferred_element_type=F32))
        else:
            halves.append(jnp.dot(hi[:, sl], ones, preferred_element_type=F32))
    full = jnp.concatenate(halves, axis=1)
    m = xs[0].shape[0]
    return [full[m * i:m * (i + 1)] for i in range(len(xs))]


@jax.custom_vjp
def _segsum(x):
    return _seg_dot(x, _seg_ones(), 2)


_segsum.defvjp(lambda x: (_segsum(x), None), lambda _, g: (_segsum(g),))


@jax.custom_vjp
def _bdot(a, w):
    return jnp.dot(a.astype(BF16), w.astype(BF16), preferred_element_type=F32)


def _bdot_fwd(a, w):
    return _bdot(a, w), (a, w)


def _bdot_bwd(res, g):
    a, w = res
    gb = g.astype(BF16)
    da = lax.dot_general(gb, w.astype(BF16), (((1,), (1,)), ((), ())), preferred_element_type=F32)
    dw = lax.dot_general(a.astype(BF16), gb, (((0,), (0,)), ((), ())), preferred_element_type=F32)
    return da, dw


_bdot.defvjp(_bdot_fwd, _bdot_bwd)


def _rot_impl(x):
    w = x.shape[1]
    lane = lax.broadcasted_iota(jnp.int32, x.shape, 1)
    return jnp.where((lane & 63) < 32, -pltpu.roll(x, w - 32, 1), pltpu.roll(x, 32, 1))


@jax.custom_vjp
def _rot(x):
    return _rot_impl(x)


_rot.defvjp(lambda x: (_rot_impl(x), None), lambda _, g: (-_rot_impl(g),))


def _rms(x, g):
    return x * lax.rsqrt(jnp.mean(x * x, axis=-1, keepdims=True) + NORM_EPS) * g


def _shift_rows(p, prev_row):
    row = lax.broadcasted_iota(jnp.int32, p.shape, 0)
    return jnp.where(row == 0, prev_row, pltpu.roll(p, 1, 0))


def _unshift_rows(g, next_row):
    row = lax.broadcasted_iota(jnp.int32, g.shape, 0)
    return jnp.where(row == g.shape[0] - 1, next_row, pltpu.roll(g, g.shape[0] - 1, 0))


def _f_mla(cq, ckv, kr, cos, sin, g_q, wqn, wqr, g_kv, wkv):
    qn = _rms(cq, g_q)
    q_nope = _bdot(qn, wqn)
    q_r = _bdot(qn, wqr)
    cos4 = jnp.concatenate([cos] * MLA_HEADS, axis=1)
    sin4 = jnp.concatenate([sin] * MLA_HEADS, axis=1)
    q_rope = q_r * cos4 + _rot(q_r) * sin4
    kv = _bdot(_rms(ckv, g_kv), wkv)
    k_rope = kr * cos + _rot(kr) * sin
    return q_nope, q_rope, kv, k_rope


def _f_rw(pr, pk, pv, pt, sr, sk, sv, st, mu_r, mu_k, mu_v, mu_t, w0, w2p, a0, a2p, k_k, k_a):
    r = pr + (sr - pr) * mu_r
    k = pk + (sk - pk) * mu_k
    v = pv + (sv - pv) * mu_v
    t = pt + (st - pt) * mu_t
    nwl = -(w0 + _bdot(jnp.tanh(t), w2p))
    softplus = jnp.maximum(nwl, 0.0) + jnp.log(1.0 + jnp.exp(-jnp.abs(nwl)))
    decay = jnp.exp(-jnp.exp(-softplus - 0.5))
    a = jax.nn.sigmoid(a0 + _bdot(t, a2p))
    kk = k * k_k
    kk = kk / jnp.maximum(jnp.sqrt(_segsum(kk * kk)), 1e-12)
    k2 = k * (1.0 + (a - 1.0) * k_a)
    return r, decay, k2, v, -kk, kk * a


def _f_head(ys, r, k, v, ym, z1, z2, x, tgt, ln_g, ln_b, r_k, w1, w2, g_post):
    inv = 1.0 / RW_HEAD
    yc = ys - _segsum(ys) * inv
    var = _segsum(yc * yc) * inv
    y = yc * lax.rsqrt(var + RW_GN_EPS) * ln_g + ln_b
    y_rw = y + _segsum(r * k * r_k) * v
    c1 = ym * (z1 * jax.nn.sigmoid(z1))
    c2 = y_rw * (z2 * jax.nn.sigmoid(z2))
    out = _bdot(c1, w1) + _bdot(c2, w2)
    err = x + _rms(out, g_post) - tgt
    per_row = jnp.sum(err * err, axis=1, keepdims=True)
    return jnp.sum(per_row, axis=0, keepdims=True) * (0.5 / D_MODEL)


def _rows(tm, width):
    return pl.BlockSpec((tm, width), lambda i: (i, 0))


def _whole(shape):
    zeros = (0,) * len(shape)
    return pl.BlockSpec(shape, lambda i: zeros)


def _sds(shape, dtype=F32):
    return jax.ShapeDtypeStruct(shape, dtype)


_ARB1 = pltpu.CompilerParams(dimension_semantics=("arbitrary",))


def _acc(ref, val, first):
    @pl.when(first)
    def _():
        ref[...] = val

    @pl.when(jnp.logical_not(first))
    def _():
        ref[...] += val


def _fwd_a(x2, g_pre, wp, tm):
    n = x2.shape[0]

    def body(x_ref, g_ref, w_ref, ut_ref, pm_ref, prw_ref, z_ref):
        u = _rms(x_ref[...], g_ref[...])
        ub = u.astype(BF16)
        ut_ref[...] = u.T.astype(BF16)
        pm_ref[...] = jnp.dot(ub, w_ref[:, 0:PM_W], preferred_element_type=F32)
        prw_ref[...] = jnp.dot(ub, w_ref[:, PM_W:PM_W + RW_COLS], preferred_element_type=F32)
        z_ref[...] = jnp.dot(ub, w_ref[:, PM_W + RW_COLS:WP_COLS], preferred_element_type=F32)

    return pl.pallas_call(
        body, name="fwd_a", grid=(n // tm,),
        in_specs=[_rows(tm, D_MODEL), _whole((1, D_MODEL)), _whole((D_MODEL, WP_COLS))],
        out_specs=[pl.BlockSpec((D_MODEL, tm), lambda i: (0, i)), _rows(tm, PM_W), _rows(tm, RW_COLS), _rows(tm, D_MODEL)],
        out_shape=[_sds((D_MODEL, n), BF16), _sds((n, PM_W)), _sds((n, RW_COLS)), _sds((n, D_MODEL))],
        compiler_params=_ARB1,
    )(x2, g_pre, wp)


def _rope_tables(pos_row, invf_col, tm):
    n = pos_row.shape[1]

    def body(p_ref, f_ref, c_ref, s_ref):
        distinct = MLA_ROPE // 2
        ang = f_ref[0:distinct, :] * p_ref[...].astype(F32)
        c_ref[...] = jnp.concatenate([jnp.cos(ang)] * (LANES // distinct), axis=0).T
        s_ref[...] = jnp.concatenate([jnp.sin(ang)] * (LANES // distinct), axis=0).T

    return pl.pallas_call(
        body, name="rope_tables", grid=(n // tm,),
        in_specs=[pl.BlockSpec((1, tm), lambda i: (0, i)), _whole((LANES, 1))],
        out_specs=[_rows(tm, LANES), _rows(tm, LANES)],
        out_shape=[_sds((n, LANES)), _sds((n, LANES))],
        compiler_params=_ARB1,
    )(pos_row, invf_col)


_B_WEIGHT_SHAPES = ((1, Q_LORA), (Q_LORA, 512), (Q_LORA, 512), (1, KV_LORA), (KV_LORA, 1024), (1, RW_COLS), (1, RW_WIDTH),
                    (LANES, RW_WIDTH), (1, RW_WIDTH), (LANES, RW_WIDTH), (1, RW_WIDTH), (1, RW_WIDTH))


def _halo_prev(tm):
    return pl.BlockSpec((8, RW_COLS), lambda i: (jnp.maximum(i * (tm // 8) - 1, 0), 0))


def _b_operands(pm_ref, prw_ref, halo_ref, wrefs, tile, tiles_per_seq):
    g_q, wqn, wqr, g_kv, wkv, mu, w0, w2p, a0, a2p, k_k, k_a = wrefs
    mla_in = (pm_ref[:, 0:Q_LORA], pm_ref[:, Q_LORA:Q_LORA + KV_LORA], pm_ref[:, Q_LORA + KV_LORA:PM_W])
    mla_w = (g_q[...], wqn[...], wqr[...], g_kv[...], wkv[...])
    keep = jnp.where(tile % tiles_per_seq == 0, 0.0, 1.0)
    prev = halo_ref[7:8, :] * keep
    ps = tuple(prw_ref[:, a:b] for a, b in RW_PIECES)
    ss = tuple(_shift_rows(p, prev[:, a:b]) for p, (a, b) in zip(ps, RW_PIECES))
    rw_w = tuple(mu[:, a:b] for a, b in RW_PIECES) + (w0[...], w2p[...], a0[...], a2p[...], k_k[...], k_a[...])
    return mla_in, mla_w, ps + ss, rw_w


def _fwd_b(pm, prw, cos, sin, bw, tm, tiles_per_seq):
    n = pm.shape[0]

    def body(pm_ref, prw_ref, halo_ref, cos_ref, sin_ref, *refs):
        wrefs, outs = refs[:12], refs[12:]
        mla_in, mla_w, rw_in, rw_w = _b_operands(pm_ref, prw_ref, halo_ref, wrefs, pl.program_id(0), tiles_per_seq)
        res = _f_mla(*mla_in, cos_ref[...], sin_ref[...], *mla_w) + _f_rw(*rw_in, *rw_w)
        for o_ref, val in zip(outs, res):
            o_ref[...] = val.astype(o_ref.dtype)

    widths = (512, 512, 1024, LANES) + (RW_WIDTH,) * 6
    return pl.pallas_call(
        body, name="fwd_b", grid=(n // tm,),
        in_specs=[_rows(tm, PM_W), _rows(tm, RW_COLS), _halo_prev(tm), _rows(tm, LANES), _rows(tm, LANES)]
        + [_whole(s) for s in _B_WEIGHT_SHAPES],
        out_specs=[_rows(tm, w) for w in widths],
        out_shape=[_sds((n, w), BF16 if j < 4 else F32) for j, w in enumerate(widths)],
        compiler_params=_ARB1,
    )(pm, prw, prw, cos, sin, *bw)


def _bwd_b(pm, prw, cos, sin, bw, cts, dkr_heads, tm, tiles_per_seq):
    n = pm.shape[0]

    ct_widths = (512, 512, 1024) + (RW_WIDTH,) * 9
    n_ct = len(ct_widths)

    def body(pm_ref, prw_ref, halo_ref, cos_ref, sin_ref, *refs):
        wrefs, ct_refs, dkr_ref = refs[:12], refs[12:12 + n_ct], refs[12 + n_ct]
        dpm_ref, dprw_ref, dps_ref = refs[13 + n_ct:16 + n_ct]
        wg_refs = refs[16 + n_ct:]
        tile = pl.program_id(0)
        first = tile == 0
        mla_in, mla_w, rw_in, rw_w = _b_operands(pm_ref, prw_ref, halo_ref, wrefs, tile, tiles_per_seq)
        cos, sin = cos_ref[...], sin_ref[...]
        ct = [r[...] for r in ct_refs]
        _, vjp_mla = jax.vjp(lambda *a: _f_mla(*a[:3], cos, sin, *a[3:]), *mla_in, *mla_w)
        dkr = dkr_ref[0] + dkr_ref[1] + dkr_ref[2] + dkr_ref[3]
        d_mla = vjp_mla((ct[0], ct[1], ct[2], dkr))
        dpm_ref[:, 0:Q_LORA] = d_mla[0]
        dpm_ref[:, Q_LORA:Q_LORA + KV_LORA] = d_mla[1]
        dpm_ref[:, Q_LORA + KV_LORA:PM_W] = d_mla[2]
        _, vjp_rw = jax.vjp(_f_rw, *rw_in, *rw_w)
        d_rw = vjp_rw((ct[3] + ct[4], ct[5], ct[6] + ct[7], ct[8] + ct[9], ct[10], ct[11]))
        for j, (a, b) in enumerate(RW_PIECES):
            dprw_ref[:, a:b] = d_rw[j]
            dps_ref[:, a:b] = d_rw[4 + j]
        g_q, wqn, wqr, g_kv, wkv, mu, w0, w2p, a0, a2p, k_k, k_a = wg_refs
        for ref, val in zip((g_q, wqn, wqr, g_kv, wkv), d_mla[3:]):
            _acc(ref, val, first)
        for j, (a, b) in enumerate(RW_PIECES):
            _acc(mu.at[:, a:b], d_rw[8 + j], first)
        for ref, val in zip((w0, w2p, a0, a2p, k_k, k_a), d_rw[12:]):
            _acc(ref, val, first)

    return pl.pallas_call(
        body, name="bwd_b", grid=(n // tm,),
        in_specs=[_rows(tm, PM_W), _rows(tm, RW_COLS), _halo_prev(tm), _rows(tm, LANES), _rows(tm, LANES)]
        + [_whole(s) for s in _B_WEIGHT_SHAPES] + [_rows(tm, w) for w in ct_widths]
        + [pl.BlockSpec((MLA_HEADS, tm, LANES), lambda i: (0, i, 0))],
        out_specs=[_rows(tm, PM_W), _rows(tm, RW_COLS), _rows(tm, RW_COLS)] + [_whole(s) for s in _B_WEIGHT_SHAPES],
        out_shape=[_sds((n, PM_W)), _sds((n, RW_COLS)), _sds((n, RW_COLS))] + [_sds(s) for s in _B_WEIGHT_SHAPES],
        compiler_params=_ARB1,
    )(pm, prw, prw, cos, sin, *bw, *cts, dkr_heads)


def _head(ys, r, k, v, ym, z, x2, tgt, hw, tm):
    n = x2.shape[0]
    h_shapes = ((1, RW_WIDTH), (1, RW_WIDTH), (1, RW_WIDTH), (D_MODEL, D_MODEL), (1, D_MODEL))

    def body(ys_ref, r_ref, k_ref, v_ref, ym_ref, z_ref, x_ref, t_ref, lng, lnb, rk, wout, gpost,
             dys_ref, dr_ref, dk_ref, dv_ref, dym_ref, dz_ref, dx_ref, loss_ref, dlng, dlnb, drk, dwout, dgpost):
        first = pl.program_id(0) == 0
        tgt_v = t_ref[...]
        args = (ys_ref[...], r_ref[...], k_ref[...], v_ref[...], ym_ref[...], z_ref[:, 0:MLA_WIDTH], z_ref[:, MLA_WIDTH:D_MODEL],
                x_ref[...], lng[...], lnb[...], rk[...], wout[0:MLA_WIDTH, :], wout[MLA_WIDTH:D_MODEL, :], gpost[...])
        loss, vjp = jax.vjp(lambda *a: _f_head(*a[:8], tgt_v, *a[8:]), *args)
        d = vjp(jnp.ones((1, 1), F32))
        dys_ref[...] = d[0]
        dr_ref[...] = d[1]
        dk_ref[...] = d[2]
        dv_ref[...] = d[3]
        dym_ref[...] = d[4].astype(BF16)
        dz_ref[:, 0:MLA_WIDTH] = d[5]
        dz_ref[:, MLA_WIDTH:D_MODEL] = d[6]
        dx_ref[...] = d[7]
        _acc(loss_ref, jnp.broadcast_to(loss, (8, LANES)), first)
        _acc(dlng, d[8], first)
        _acc(dlnb, d[9], first)
        _acc(drk, d[10], first)
        _acc(dwout.at[0:MLA_WIDTH, :], d[11], first)
        _acc(dwout.at[MLA_WIDTH:D_MODEL, :], d[12], first)
        _acc(dgpost, d[13], first)

    widths = (RW_WIDTH,) * 4 + (MLA_WIDTH, D_MODEL, D_MODEL)
    return pl.pallas_call(
        body, name="head", grid=(n // tm,),
        in_specs=[_rows(tm, RW_WIDTH)] * 4 + [_rows(tm, MLA_WIDTH), _rows(tm, D_MODEL), _rows(tm, D_MODEL), _rows(tm, D_MODEL)]
        + [_whole(s) for s in h_shapes],
        out_specs=[_rows(tm, w) for w in widths] + [_whole((8, LANES))] + [_whole(s) for s in h_shapes],
        out_shape=[_sds((n, w), BF16 if j == 4 else F32) for j, w in enumerate(widths)] + [_sds((8, LANES))]
        + [_sds(s) for s in h_shapes],
        compiler_params=_ARB1,
    )(ys, r, k, v, ym, z, x2, tgt, *hw)


def _halo_next(tm, n):
    last = n // 8 - 1
    return pl.BlockSpec((8, RW_COLS), lambda i: (jnp.minimum((i + 1) * (tm // 8), last), 0))


def _bwd_a(x2, g_pre, wp, dpm, dprw, dps, dz, dxres, tm, tiles_per_seq):
    n = x2.shape[0]
    nt_dims = (((1,), (1,)), ((), ()))

    def body(x_ref, g_ref, w_ref, dpm_ref, dprw_ref, dps_ref, nxt_ref, dz_ref, dxres_ref, gx_ref, dpb_ref, dg_ref):
        tile = pl.program_id(0)
        keep = jnp.where((tile + 1) % tiles_per_seq == 0, 0.0, 1.0)
        dprw_v = dprw_ref[...] + _unshift_rows(dps_ref[...], nxt_ref[0:1, :] * keep)
        dpm_b, dprw_b, dz_b = dpm_ref[...].astype(BF16), dprw_v.astype(BF16), dz_ref[...].astype(BF16)
        dpb_ref[:, 0:PM_W] = dpm_b
        dpb_ref[:, PM_W:PM_W + RW_COLS] = dprw_b
        dpb_ref[:, PM_W + RW_COLS:WP_COLS] = dz_b
        du = (lax.dot_general(dpm_b, w_ref[:, 0:PM_W], nt_dims, preferred_element_type=F32)
              + lax.dot_general(dprw_b, w_ref[:, PM_W:PM_W + RW_COLS], nt_dims, preferred_element_type=F32)
              + lax.dot_general(dz_b, w_ref[:, PM_W + RW_COLS:WP_COLS], nt_dims, preferred_element_type=F32))
        x = x_ref[...]
        xhat = x * lax.rsqrt(jnp.mean(x * x, axis=-1, keepdims=True) + NORM_EPS)
        dxn = du * g_ref[...]
        dx = (dxn - xhat * jnp.mean(dxn * xhat, axis=-1, keepdims=True)) * lax.rsqrt(jnp.mean(x * x, axis=-1, keepdims=True) + NORM_EPS)
        gx_ref[...] = dx + dxres_ref[...]
        _acc(dg_ref, jnp.sum(du * xhat, axis=0, keepdims=True), tile == 0)

    return pl.pallas_call(
        body, name="bwd_a", grid=(n // tm,),
        in_specs=[_rows(tm, D_MODEL), _whole((1, D_MODEL)), _whole((D_MODEL, WP_COLS)), _rows(tm, PM_W), _rows(tm, RW_COLS),
                  _rows(tm, RW_COLS), _halo_next(tm, n), _rows(tm, D_MODEL), _rows(tm, D_MODEL)],
        out_specs=[_rows(tm, D_MODEL), _rows(tm, WP_COLS), _whole((1, D_MODEL))],
        out_shape=[_sds((n, D_MODEL)), _sds((n, WP_COLS), BF16), _sds((1, D_MODEL))],
        compiler_params=_ARB1,
    )(x2, g_pre, wp, dpm, dprw, dps, dps, dz, dxres)


def _dw_in(ut, dpb, tk, tn):
    n = ut.shape[1]
    steps = n // tk

    def body(u_ref, d_ref, o_ref, acc_sc):
        k = pl.program_id(1)
        _acc(acc_sc, jnp.dot(u_ref[...], d_ref[...], preferred_element_type=F32), k == 0)

        @pl.when(k == steps - 1)
        def _():
            o_ref[...] = acc_sc[...].astype(BF16)

    return pl.pallas_call(
        body, name="dw_in", grid=(WP_COLS // tn, steps),
        in_specs=[pl.BlockSpec((D_MODEL, tk), lambda j, k: (0, k)), pl.BlockSpec((tk, tn), lambda j, k: (k, j))],
        out_specs=pl.BlockSpec((D_MODEL, tn), lambda j, k: (0, j)),
        out_shape=_sds((D_MODEL, WP_COLS), BF16),
        scratch_shapes=[pltpu.VMEM((D_MODEL, tn), F32)],
        compiler_params=pltpu.CompilerParams(dimension_semantics=("arbitrary", "arbitrary")),
    )(ut, dpb)


ATT_BLK = 256
_NT = (((1,), (1,)), ((), ()))
_TN = (((0,), (0,)), ((), ()))


def _causal(q0, k0, blk, blk_k=None):
    blk_k = blk if blk_k is None else blk_k
    row = q0 + lax.broadcasted_iota(jnp.int32, (blk, blk_k), 0)
    col = k0 + lax.broadcasted_iota(jnp.int32, (blk, blk_k), 1)
    return row >= col


def _attn_fwd(qn, qr, kv, kr):
    bsz, t, _ = qn.shape
    blk = min(ATT_BLK, t)

    heads = range(MLA_HEADS)

    def body(qn_ref, qr_ref, kv_ref, kr_ref, o_ref, lse_ref):
        qi = pl.program_id(1)
        q = [jnp.concatenate([qn_ref[:, LANES * h:LANES * (h + 1)], qr_ref[:, LANES * h:LANES * (h + 1)]], axis=1) for h in heads]
        lower = _causal(0, 0, blk)

        def kv_step(j, carry, diagonal):
            ks = pl.multiple_of(j * blk, blk)
            k_rope = kr_ref[pl.ds(ks, blk), :]
            def score(h):
                k = jnp.concatenate([kv_ref[pl.ds(ks, blk), 2 * LANES * h:2 * LANES * h + LANES], k_rope], axis=1)
                return lax.dot_general(q[h], k, _NT, preferred_element_type=F32)

            out = []
            nxt = score(0)
            for h in heads:
                s = nxt * ATT_SCALE
                if h + 1 < MLA_HEADS:
                    nxt = score(h + 1)
                m, l, acc = carry[h]
                if diagonal:
                    s = jnp.where(lower, s, -1e30)
                m_new = jnp.maximum(m, jnp.max(s, axis=1, keepdims=True))
                alpha = jnp.exp(m - m_new)
                p = jnp.exp(s - m_new)
                l = alpha * l + jnp.sum(p, axis=1, keepdims=True)
                v = kv_ref[pl.ds(ks, blk), 2 * LANES * h + LANES:2 * LANES * (h + 1)]
                out.append((m_new, l, alpha * acc + jnp.dot(p.astype(BF16), v, preferred_element_type=F32)))
            return tuple(out)

        one = (jnp.full((blk, 1), -1e30, F32), jnp.zeros((blk, 1), F32), jnp.zeros((blk, MLA_V), F32))
        carry = lax.fori_loop(0, qi // 2, lambda pr, c: kv_step(2 * pr + 1, kv_step(2 * pr, c, False), False),
                              (one,) * MLA_HEADS)
        carry = lax.cond(qi % 2 == 1, lambda c: kv_step(qi - 1, c, False), lambda c: c, carry)
        carry = kv_step(qi, carry, True)
        for h in heads:
            m, l, acc = carry[h]
            o_ref[:, LANES * h:LANES * (h + 1)] = acc / l
            lse_ref[h] = jnp.broadcast_to(m + jnp.log(l), (blk, LANES))

    return pl.pallas_call(
        body, name="attn_fwd", grid=(bsz, t // blk),
        in_specs=[pl.BlockSpec((None, blk, MLA_WIDTH), lambda b, i: (b, i, 0)),
                  pl.BlockSpec((None, blk, MLA_WIDTH), lambda b, i: (b, i, 0)),
                  pl.BlockSpec((None, t, 2 * MLA_WIDTH), lambda b, i: (b, 0, 0)),
                  pl.BlockSpec((None, t, LANES), lambda b, i: (b, 0, 0))],
        out_specs=[pl.BlockSpec((None, blk, MLA_WIDTH), lambda b, i: (b, i, 0)),
                   pl.BlockSpec((None, MLA_HEADS, blk, LANES), lambda b, i: (b, 0, i, 0))],
        out_shape=[_sds((bsz, t, MLA_WIDTH)), _sds((bsz, MLA_HEADS, t, LANES))],
        compiler_params=pltpu.CompilerParams(dimension_semantics=("arbitrary", "arbitrary")),
    )(qn, qr, kv, kr)


def _attn_bwd(qn, qr, kv, kr, o, do, lse):
    bsz, t, _ = qn.shape
    blk = min(ATT_BLK, t)
    nb = t // blk
    assert nb % 2 == 0, "query blocks are taken in pairs"

    def body(qn_ref, qr_ref, kn_ref, kr_ref, v_ref, o_ref, do_ref, lse_ref, dqn_ref, dqr_ref, dkv_ref, dkr_ref, dq_sc, delta_sc):
        dq_sc[...] = jnp.zeros_like(dq_sc)
        delta_sc[...] = jnp.sum(do_ref[...].astype(F32) * o_ref[...], axis=1, keepdims=True)

        lower = _causal(0, 0, blk)

        def q_blocks(j, k, vb, carry, blocks, diagonal):
            dk, dv = carry
            us = range(len(blocks))
            qs = [i * blk if isinstance(i, int) else pl.multiple_of(i * blk, blk) for i in blocks]
            q = [jnp.concatenate([qn_ref[pl.ds(qs[u], blk), :], qr_ref[pl.ds(qs[u], blk), :]], axis=1) for u in us]
            dob = [do_ref[pl.ds(qs[u], blk), :] for u in us]
            s = [lax.dot_general(q[u], k, _NT, preferred_element_type=F32) for u in us]
            dp = [lax.dot_general(dob[u], vb, _NT, preferred_element_type=F32) for u in us]
            pb, ds = [], []
            for u in us:
                p = jnp.exp(s[u] * ATT_SCALE - lse_ref[pl.ds(qs[u], blk), 0:1])
                if diagonal == u:
                    p = jnp.where(lower, p, 0.0)
                pb.append(p.astype(BF16))
                ds.append((p * (dp[u] - delta_sc[pl.ds(qs[u], blk), :]) * ATT_SCALE).astype(BF16))
            for u in us:
                dv = dv + lax.dot_general(pb[u], dob[u], _TN, preferred_element_type=F32)
            for u in us:
                dq_sc[pl.ds(qs[u], blk), :] += jnp.dot(ds[u], k, preferred_element_type=F32)
                dk = dk + lax.dot_general(ds[u], q[u], _TN, preferred_element_type=F32)
            return dk, dv

        for j in range(nb):
            ks = j * blk
            k = jnp.concatenate([kn_ref[ks:ks + blk, :], kr_ref[ks:ks + blk, :]], axis=1)
            vb = v_ref[ks:ks + blk, :]
            carry = (jnp.zeros((blk, 2 * LANES), F32), jnp.zeros((blk, MLA_V), F32))
            if j % 2 == 0:
                carry = q_blocks(j, k, vb, carry, [j, j + 1], 0)
            else:
                carry = q_blocks(j, k, vb, carry, [j], 0)
            pairs_from = j // 2 + 1
            if nb // 2 - pairs_from > 0:
                carry = lax.fori_loop(pairs_from, nb // 2,
                                      lambda pr, c, j=j, k=k, vb=vb: q_blocks(j, k, vb, c, [2 * pr, 2 * pr + 1], None),
                                      carry, unroll=nb // 2 - pairs_from)
            dk, dv = carry
            dkv_ref[ks:ks + blk, 0:LANES] = dk[:, 0:LANES]
            dkv_ref[ks:ks + blk, LANES:2 * LANES] = dv
            dkr_ref[ks:ks + blk, :] = dk[:, LANES:2 * LANES]
        dqn_ref[...] = dq_sc[:, 0:LANES]
        dqr_ref[...] = dq_sc[:, LANES:2 * LANES]

    head_col = lambda b, h: (b, 0, h)
    return pl.pallas_call(
        body, name="attn_bwd", grid=(bsz, MLA_HEADS),
        in_specs=[pl.BlockSpec((None, t, LANES), head_col), pl.BlockSpec((None, t, LANES), head_col),
                  pl.BlockSpec((None, t, LANES), lambda b, h: (b, 0, 2 * h)),
                  pl.BlockSpec((None, t, LANES), lambda b, h: (b, 0, 0)),
                  pl.BlockSpec((None, t, LANES), lambda b, h: (b, 0, 2 * h + 1)),
                  pl.BlockSpec((None, t, LANES), head_col), pl.BlockSpec((None, t, LANES), head_col),
                  pl.BlockSpec((None, None, t, LANES), lambda b, h: (b, h, 0, 0))],
        out_specs=[pl.BlockSpec((None, t, LANES), head_col), pl.BlockSpec((None, t, LANES), head_col),
                   pl.BlockSpec((None, t, 2 * LANES), head_col),
                   pl.BlockSpec((None, None, t, LANES), lambda b, h: (h, b, 0, 0))],
        out_shape=[_sds((bsz, t, MLA_WIDTH)), _sds((bsz, t, MLA_WIDTH)), _sds((bsz, t, 2 * MLA_WIDTH)),
                   _sds((MLA_HEADS, bsz, t, LANES))],
        scratch_shapes=[pltpu.VMEM((t, 2 * LANES), F32), pltpu.VMEM((t, 1), F32)],
        compiler_params=pltpu.CompilerParams(dimension_semantics=("arbitrary", "arbitrary")),
    )(qn, qr, kv, kr, kv, o, do, lse)


SCAN_CHUNK = 16


def _diag_mask():
    row = lax.broadcasted_iota(jnp.int32, (RW_HEAD, RW_WIDTH), 0)
    lane = lax.broadcasted_iota(jnp.int32, (RW_HEAD, RW_WIDTH), 1)
    return jnp.where(row == (lane & (RW_HEAD - 1)), 1.0, 0.0)


def _time_minor(a):
    bsz, t, _ = a.shape
    a = a.reshape(bsz, t // SCAN_CHUNK, SCAN_CHUNK, RW_HEADS, RW_HEAD)
    return a.transpose(0, 1, 4, 3, 2).reshape(bsz, t // SCAN_CHUNK, RW_HEAD, RW_HEADS * SCAN_CHUNK)


def _head_expand():
    l = lax.broadcasted_iota(jnp.int32, (2 * LANES, RW_WIDTH), 0)
    n = lax.broadcasted_iota(jnp.int32, (2 * LANES, RW_WIDTH), 1)
    return jnp.where(((l & (LANES - 1)) >> 4) == (n >> 6), 1.0, 0.0).astype(BF16)


BCAST_GROUP = 4


def _outer_chunk(tm_ref, row_ref, out_sc, expand, seqs):
    step_of_lane = lax.broadcasted_iota(jnp.int32, (RW_HEAD, LANES), 1) & (SCAN_CHUNK - 1)
    tiles = [tm_ref[bi, 0] for bi in seqs]
    for t0 in range(0, SCAN_CHUNK, BCAST_GROUP):
        parts = []
        for t in range(t0, t0 + BCAST_GROUP):
            for tile in tiles:
                a = jnp.where(step_of_lane == t, tile, 0.0)
                hi = a.astype(BF16)
                parts.append(jnp.concatenate([hi, (a - hi.astype(F32)).astype(BF16)], axis=1))
        cols = jnp.dot(jnp.concatenate(parts, axis=0), expand, preferred_element_type=F32)
        for j, t in enumerate(range(t0, t0 + BCAST_GROUP)):
            base = j * RW_HEAD * len(seqs)
            out_sc[t] = jnp.concatenate([cols[base + RW_HEAD * bi:base + RW_HEAD * (bi + 1)] * row_ref[bi, t:t + 1, :]
                                         for bi in seqs], axis=0)


def _fold8(x):
    acc = x[0:8]
    for j in range(1, x.shape[0] // 8):
        acc = acc + x[8 * j:8 * (j + 1)]
    return acc


def _rows8(at):
    return pl.ds(at * 8 if isinstance(at, int) else pl.multiple_of(at * 8, 8), 8)


def _put8(sc, bi, at, val):
    for j in range(RW_WIDTH // LANES):
        sc[bi * (RW_WIDTH // LANES) + j, _rows8(at), :] = val[:, LANES * j:LANES * (j + 1)]


def _unfold8(sc, bi, steps):
    tiles = []
    for j in range(RW_WIDTH // LANES):
        view = sc.at[bi * (RW_WIDTH // LANES) + j]
        acc = view[pl.ds(0, steps, stride=8), :]
        for s in range(1, 8):
            acc = acc + view[pl.ds(s, steps, stride=8), :]
        tiles.append(acc)
    return jnp.concatenate(tiles, axis=1)


def _scan_fwd(r, w, k, vt, nkk, b):
    bsz, t, _ = r.shape
    tc = SCAN_CHUNK

    def body(r_ref, w_ref, k_ref, n_ref, b_ref, vt_ref, y_ref, st_ref, s_sc, vc_sc, y_sc):
        @pl.when(pl.program_id(0) == 0)
        def _():
            s_sc[...] = jnp.zeros_like(s_sc)

        ones = _seg_ones()
        diag = _diag_mask()
        seqs = range(bsz)
        _outer_chunk(vt_ref, k_ref, vc_sc, _head_expand(), seqs)

        def put_y(ya, at):
            for bi in seqs:
                _put8(y_sc, bi, at, _fold8(ya[bi] * diag))

        def step(i, _):
            row = lambda ref, bi: ref[bi, pl.ds(i, 1), :]
            prev = jnp.maximum(i - 1, 0)
            s_old = [s_sc[bi] for bi in seqs]
            s_b = [s_old[bi].astype(BF16) for bi in seqs]
            sa = _seg_multi([s_b[bi] * row(n_ref, bi).astype(BF16) for bi in seqs], ones, 1)
            put_y(_seg_multi([s_b[bi] * r_ref[bi, pl.ds(prev, 1), :].astype(BF16) for bi in seqs], ones, 1), prev)
            vk = vc_sc[i]
            for bi in seqs:
                s_new = s_old[bi] * row(w_ref, bi) + sa[bi] * row(b_ref, bi) + vk[RW_HEAD * bi:RW_HEAD * (bi + 1)]
                s_sc[bi] = s_new
                st_ref[bi, i] = s_new
            return 0

        lax.fori_loop(0, tc, step, 0, unroll=8)
        put_y(_seg_multi([s_sc[bi].astype(BF16) * r_ref[bi, tc - 1:tc, :].astype(BF16) for bi in seqs], ones, 1), tc - 1)
        for bi in seqs:
            y_ref[bi] = _unfold8(y_sc, bi, tc)

    vec = pl.BlockSpec((bsz, tc, RW_WIDTH), lambda c: (0, c, 0))
    return pl.pallas_call(
        body, name="scan_fwd", grid=(t // tc,),
        in_specs=[vec] * 5 + [pl.BlockSpec((bsz, 1, RW_HEAD, LANES), lambda c: (0, c, 0, 0))],
        out_specs=[vec, pl.BlockSpec((bsz, tc, RW_HEAD, RW_WIDTH), lambda c: (0, c, 0, 0))],
        out_shape=[_sds((bsz, t, RW_WIDTH)), _sds((bsz, t, RW_HEAD, RW_WIDTH))],
        scratch_shapes=[pltpu.VMEM((bsz, RW_HEAD, RW_WIDTH), F32), pltpu.VMEM((tc, bsz * RW_HEAD, RW_WIDTH), F32),
                        pltpu.VMEM((bsz * RW_WIDTH // LANES, tc * 8, LANES), F32)],
        compiler_params=_ARB1,
    )(r, w, k, nkk, b, vt)


def _own_head_row(x):
    first_half = lax.broadcasted_iota(jnp.int32, (1, LANES), 1) < RW_HEAD
    tiles = [jnp.where(first_half, x[2 * j:2 * j + 1, LANES * j:LANES * (j + 1)], x[2 * j + 1:2 * j + 2, LANES * j:LANES * (j + 1)])
             for j in range(RW_WIDTH // LANES)]
    return jnp.concatenate(tiles, axis=1)


def _scan_bwd(r, w, k, vt, nkk, b, st, dyt):
    bsz, t, _ = r.shape
    tc = SCAN_CHUNK
    nc = t // tc

    def body(r_ref, w_ref, k_ref, n_ref, b_ref, vt_ref, dyt_ref, st_ref, halo_ref,
             dr_ref, dw_ref, dk_ref, dv_ref, dn_ref, db_ref, g_sc, dc_sc, v8_sc, dy8_sc, *part_scs):
        c = pl.program_id(0)

        @pl.when(c == 0)
        def _():
            g_sc[...] = jnp.zeros_like(g_sc)

        ones = _seg_ones()
        diag = _diag_mask()
        has_prev = jnp.where(c == nc - 1, 0.0, 1.0)
        seqs = range(bsz)
        _outer_chunk(dyt_ref, r_ref, dc_sc, _head_expand(), seqs)
        for bi in seqs:
            v8_sc[bi] = jnp.concatenate([vt_ref[bi, 0].T] * 2, axis=1)
            dy8_sc[bi] = jnp.concatenate([dyt_ref[bi, 0].T] * 2, axis=1)
        by_head = lambda sc, bi, i: sc.at[bi][pl.ds(i, RW_HEADS, stride=SCAN_CHUNK), :][:, 0:RW_HEAD].astype(BF16)
        dw_sc, dv_sc, dn_sc, db_sc = part_scs

        def step(i, s_p, s_t_b=None):
            static = isinstance(i, int)
            row = lambda ref, bi: ref[bi, i:i + 1, :] if static else ref[bi, pl.ds(i, 1), :]
            put_row = lambda ref, bi, val: ref.__setitem__((bi, slice(i, i + 1) if static else pl.ds(i, 1), slice(None)), val)
            if s_t_b is None:
                s_t_b = [st_ref[bi, i].astype(BF16) for bi in seqs]
            s_p_b = [s_p[bi].astype(BF16) for bi in seqs]
            dr8 = [jnp.dot(by_head(dy8_sc, bi, i), s_t_b[bi], preferred_element_type=F32) for bi in seqs]
            rowb = lambda ref, bi: row(ref, bi).astype(BF16)
            sa = _seg_multi([s_p_b[bi] * rowb(n_ref, bi) for bi in seqs], ones, 1)
            dc_all = dc_sc[i]
            dc = [dc_all[RW_HEAD * bi:RW_HEAD * (bi + 1)] for bi in seqs]
            g = [g_sc[bi] + dc[bi] for bi in seqs]
            g_b = [g[bi].astype(BF16) for bi in seqs]
            res = _seg_multi([g_b[bi] * rowb(b_ref, bi) for bi in seqs] + [g_b[bi] * rowb(k_ref, bi) for bi in seqs], ones, 1)
            dsa, dvb = res[:bsz], res[bsz:]
            for bi in seqs:
                dk8 = jnp.dot(by_head(v8_sc, bi, i), g_b[bi], preferred_element_type=F32)
                put_row(dr_ref, bi, _own_head_row(dr8[bi]))
                put_row(dk_ref, bi, _own_head_row(dk8))
                _put8(dv_sc, bi, i, _fold8(dvb[bi] * diag))
                _put8(dw_sc, bi, i, _fold8(g[bi] * s_p[bi]))
                _put8(db_sc, bi, i, _fold8(g[bi] * sa[bi]))
                _put8(dn_sc, bi, i, _fold8(s_p[bi] * dsa[bi]))
                g_sc[bi] = g[bi] * row(w_ref, bi) + dsa[bi] * row(n_ref, bi)
            return s_p_b

        group = 5

        def loop_trip(trip, _):
            top = tc - 1 - trip * group
            s_b = None
            for u in range(group):
                i = top - u
                s_b = step(i, [st_ref[bi, i - 1] for bi in seqs], s_b)
            return 0

        lax.fori_loop(0, (tc - 1) // group, loop_trip, 0)
        step(0, [halo_ref[bi, 0] * has_prev for bi in seqs])
        for out_ref, sc in zip((dw_ref, dv_ref, dn_ref, db_ref), part_scs):
            for bi in seqs:
                out_ref[bi] = _unfold8(sc, bi, tc)

    vec = pl.BlockSpec((bsz, tc, RW_WIDTH), lambda c: (0, nc - 1 - c, 0))
    tmin = pl.BlockSpec((bsz, 1, RW_HEAD, LANES), lambda c: (0, nc - 1 - c, 0, 0))
    parts = pltpu.VMEM((bsz * RW_WIDTH // LANES, tc * 8, LANES), F32)
    heads_steps = pltpu.VMEM((bsz, LANES, LANES), F32)
    return pl.pallas_call(
        body, name="scan_bwd", grid=(nc,),
        in_specs=[vec] * 5 + [tmin, tmin,
                              pl.BlockSpec((bsz, tc, RW_HEAD, RW_WIDTH), lambda c: (0, nc - 1 - c, 0, 0)),
                              pl.BlockSpec((bsz, 1, RW_HEAD, RW_WIDTH), lambda c: (0, jnp.maximum((nc - 1 - c) * tc - 1, 0), 0, 0))],
        out_specs=[vec] * 6,
        out_shape=[_sds((bsz, t, RW_WIDTH))] * 6,
        scratch_shapes=[pltpu.VMEM((bsz, RW_HEAD, RW_WIDTH), F32), pltpu.VMEM((tc, bsz * RW_HEAD, RW_WIDTH), F32),
                        heads_steps, heads_steps] + [parts] * 4,
        compiler_params=_ARB1,
    )(r, w, k, nkk, b, vt, dyt, st, st)


TOKEN_TILE = 256
VJP_TILE = 256


def _padded_weights(wt):
    f = lambda a: a.astype(F32)
    w_in = wt["w_in"][0].astype(BF16)
    zeros = lambda r, c: jnp.zeros((r, c), F32)
    wp = jnp.concatenate([w_in[:, :MLA_COLS], jnp.zeros((D_MODEL, PM_W - MLA_COLS), BF16), w_in[:, MLA_COLS:]], axis=1)
    w_uq = f(wt["mla_w_uq"][0]).reshape(Q_LORA, MLA_HEADS, MLA_NOPE + MLA_ROPE)
    wqn = w_uq[:, :, :MLA_NOPE].reshape(Q_LORA, MLA_HEADS * MLA_NOPE)
    wqr = jnp.concatenate([w_uq[:, :, MLA_NOPE:], jnp.zeros((Q_LORA, MLA_HEADS, LANES - MLA_ROPE), F32)], axis=2)
    wqr = wqr.reshape(Q_LORA, MLA_HEADS * LANES)
    w2p = jnp.concatenate([f(wt["rw_w2"][0]), zeros(LORA, RW_WIDTH)], axis=0)
    a2p = jnp.concatenate([zeros(LORA, RW_WIDTH), f(wt["rw_a2"][0])], axis=0)
    bw = (f(wt["mla_q_norm_g"]), wqn, wqr, f(wt["mla_kv_norm_g"]), f(wt["mla_w_ukv"][0]), f(wt["rw_mu"]), f(wt["rw_w0"]),
          w2p, f(wt["rw_a0"]), a2p, f(wt["rw_k_k"]), f(wt["rw_k_a"]))
    hw = (f(wt["rw_ln_g"]), f(wt["rw_ln_b"]), f(wt["rw_r_k"]).reshape(1, RW_WIDTH), f(wt["w_out"][0]), f(wt["norm_post_g"]))
    return wp, bw, hw


def _local_step(x, positions, target, wt):
    bsz, t, _ = x.shape
    n = bsz * t
    tm = min(TOKEN_TILE, t)
    tps = t // tm
    ts = min(VJP_TILE, t)
    wp, bw, hw = _padded_weights(wt)
    wpb = wp.astype(BF16)
    g_pre = wt["norm_pre_g"].astype(F32)
    x2 = x.reshape(n, D_MODEL)
    tgt2 = target.reshape(n, D_MODEL)
    inv_freq = ROPE_THETA ** (-jnp.arange(0, MLA_ROPE, 2, dtype=F32) / MLA_ROPE)
    invf = jnp.tile(inv_freq, LANES // (MLA_ROPE // 2)).reshape(LANES, 1)
    cos, sin = _rope_tables(positions.reshape(1, n), invf, tm)

    u, pm, prw, z = _fwd_a(x2, g_pre, wpb, tm)
    qn, qr, kv, kr, r, w, k, v, nkk, b = _fwd_b(pm, prw, cos, sin, bw, tm, tps)
    b3 = lambda a: a.reshape(bsz, t, a.shape[-1])
    ym, lse = _attn_fwd(b3(qn), b3(qr), b3(kv), b3(kr))
    vt = _time_minor(b3(v))
    ys, st = _scan_fwd(b3(r), b3(w), b3(k), vt, b3(nkk), b3(b))
    (dys, dr_h, dk_h, dv_h, dym, dz, dxres, loss, d_lng, d_lnb, d_rk, d_wout, d_gpost) = _head(
        ys.reshape(n, RW_WIDTH), r, k, v, ym.reshape(n, MLA_WIDTH), z, x2, tgt2, hw, ts)
    dqn, dqr, dkv, dkr_heads = _attn_bwd(b3(qn), b3(qr), b3(kv), b3(kr), ym, b3(dym), lse)
    dr_s, dw_s, dk_s, dv_s, dn_s, db_s = _scan_bwd(b3(r), b3(w), b3(k), vt, b3(nkk), b3(b), st, _time_minor(b3(dys)))
    f2 = lambda a: a.reshape(n, a.shape[-1])
    cts = (f2(dqn), f2(dqr), f2(dkv), f2(dr_s), dr_h, f2(dw_s), f2(dk_s), dk_h, f2(dv_s), dv_h, f2(dn_s), f2(db_s))
    (dpm, dprw, dps, d_gq, d_wqn, d_wqr, d_gkv, d_wkv, d_mu, d_w0, d_w2p, d_a0, d_a2p, d_kk, d_ka) = _bwd_b(
        pm, prw, cos, sin, bw, cts, dkr_heads.reshape(MLA_HEADS, n, LANES), ts, t // ts)
    grad_x, dpb, d_gpre = _bwd_a(x2, g_pre, wpb, dpm, dprw, dps, dz, dxres, tm, tps)
    d_wp = _dw_in(u, dpb, min(1024, n), 640)

    d_w_in = jnp.concatenate([d_wp[:, :MLA_COLS], d_wp[:, PM_W:]], axis=1)
    d_w_uq = jnp.concatenate([d_wqn.reshape(Q_LORA, MLA_HEADS, MLA_NOPE),
                              d_wqr.reshape(Q_LORA, MLA_HEADS, LANES)[:, :, :MLA_ROPE]], axis=2)
    grads = {
        "norm_pre_g": d_gpre, "w_in": d_w_in[None], "mla_q_norm_g": d_gq,
        "mla_w_uq": d_w_uq.reshape(1, Q_LORA, MLA_HEADS * (MLA_NOPE + MLA_ROPE)), "mla_kv_norm_g": d_gkv,
        "mla_w_ukv": d_wkv[None], "rw_mu": d_mu, "rw_w0": d_w0, "rw_w2": d_w2p[None, :LORA], "rw_a0": d_a0,
        "rw_a2": d_a2p[None, LORA:], "rw_k_k": d_kk, "rw_k_a": d_ka, "rw_r_k": d_rk.reshape(1, RW_HEADS, RW_HEAD),
        "rw_ln_g": d_lng, "rw_ln_b": d_lnb, "w_out": d_wout[None], "norm_post_g": d_gpost,
    }
    return loss, grad_x.reshape(bsz, t, D_MODEL), grads


_MESH = pl.DeviceIdType.MESH


def _gather_shards(shards):
    na = len(shards)

    def body(*refs):
        x_refs, out_refs = refs[:na], refs[na:2 * na]
        send_sems, recv_sems, local_sems = refs[2 * na:]
        x, y, c = lax.axis_index("x"), lax.axis_index("y"), lax.axis_index("c")
        me, sibling = (x, y, c), (x, y, 1 - c)
        chips = [(1 - x, y), (x, 1 - y), (1 - x, 1 - y)]
        arrays = range(na)

        def slot(a, px, py, pc):
            return out_refs[a].at[4 * px + 2 * py + pc]

        def copy(k, a, block, to, src=None):
            return pltpu.make_async_remote_copy(
                src_ref=slot(a, *block) if src is None else src, dst_ref=slot(a, *block),
                send_sem=send_sems.at[k, a], recv_sem=recv_sems.at[k, a], device_id=to, device_id_type=_MESH)

        mine = [pltpu.make_async_copy(x_refs[a], slot(a, *me), local_sems.at[a]) for a in arrays]
        for cp in mine:
            cp.start()
        first = [copy(0, a, me, sibling, src=x_refs[a]) for a in arrays]
        first += [copy(1 + j, a, me, (*chip, c), src=x_refs[a]) for j, chip in enumerate(chips) for a in arrays]
        for cp in first:
            cp.start()
        passed = []
        for j, chip in enumerate(chips):
            for a in arrays:
                copy(1 + j, a, (*chip, c), me).wait_recv()
                passed.append(copy(4 + j, a, (*chip, c), sibling))
                passed[-1].start()
        for a in arrays:
            copy(0, a, sibling, me).wait_recv()
        for j, chip in enumerate(chips):
            for a in arrays:
                copy(4 + j, a, (*chip, 1 - c), me).wait_recv()
        for cp in first + passed:
            cp.wait_send()
        for cp in mine:
            cp.wait()

    vmem = pl.BlockSpec(memory_space=pltpu.VMEM)
    return pl.pallas_call(
        body, name="gather_shards",
        out_shape=[_sds((N_DEV,) + a.shape, a.dtype) for a in shards],
        in_specs=[vmem] * na, out_specs=[vmem] * na,
        scratch_shapes=[pltpu.SemaphoreType.DMA((7, na)), pltpu.SemaphoreType.DMA((7, na)), pltpu.SemaphoreType.DMA((na,))],
    )(*shards)


SMALL_LANES = SMALL_N + LANES


N_CHIP = 4


def _exchange_grads(big_blocks, small_grads, loss_tile):
    nb = len(big_blocks)
    ns = len(small_grads)

    def body(*refs):
        big, small, loss_ref = refs[:nb], refs[nb:nb + ns], refs[nb + ns]
        out, rsmall = refs[nb + ns + 1:2 * nb + ns + 1], refs[2 * nb + ns + 1]
        scratch = refs[2 * nb + ns + 2:]
        stage, sums = scratch[:nb], scratch[nb:2 * nb]
        send1, recv1, send2, recv2, send_s, recv_s, row_sc = scratch[2 * nb:]
        x, y, c = lax.axis_index("x"), lax.axis_index("y"), lax.axis_index("c")
        me_lin = 4 * x + 2 * y + c
        my_chip = 2 * x + y
        sibling = (x, y, 1 - c)
        leaves = range(nb)

        to_sibling = [pltpu.make_async_remote_copy(
            src_ref=big[j].at[1 - c], dst_ref=stage[j], send_sem=send1.at[j], recv_sem=recv1.at[j],
            device_id=sibling, device_id_type=_MESH) for j in leaves]
        for cp in to_sibling:
            cp.start()
        off = 0
        for ref, (_, cnt) in zip(small, SMALL):
            row_sc[:, off:off + cnt] = ref[...]
            off += cnt
        row_sc[:, off:off + LANES] = loss_ref[0:1, :]
        rsmall[me_lin] = row_sc[...]
        rows = []
        for k in range(1, N_DEV):
            peer = (x ^ (k >> 2), y ^ ((k >> 1) & 1), c ^ (k & 1))
            rows.append(pltpu.make_async_remote_copy(
                src_ref=row_sc, dst_ref=rsmall.at[me_lin], send_sem=send_s.at[k - 1], recv_sem=recv_s.at[k - 1],
                device_id=peer, device_id_type=_MESH))
        for cp in rows:
            cp.start()
        to_chips = []
        for j in leaves:
            to_sibling[j].wait_recv()
            sums[j][...] = (big[j][c].astype(F32) + stage[j][...].astype(F32)).astype(BF16)
            out[j][0] = sums[j][my_chip]
            for q in range(1, N_CHIP):
                px, py = x ^ (q >> 1), y ^ (q & 1)
                to_chips.append(pltpu.make_async_remote_copy(
                    src_ref=sums[j].at[2 * px + py], dst_ref=out[j].at[q], send_sem=send2.at[q - 1, j],
                    recv_sem=recv2.at[q - 1, j], device_id=(px, py, c), device_id_type=_MESH))
                to_chips[-1].start()
        for cp in to_chips + rows:
            cp.wait_recv()
        for cp in to_sibling + to_chips + rows:
            cp.wait_send()

    vmem = pl.BlockSpec(memory_space=pltpu.VMEM)
    shard = [a.shape[2:] for a in big_blocks]
    return pl.pallas_call(
        body, name="exchange_grads",
        out_shape=[_sds((N_CHIP,) + s, BF16) for s in shard] + [_sds((N_DEV, 1, SMALL_LANES))],
        in_specs=[vmem] * (nb + ns + 1), out_specs=[vmem] * (nb + 1),
        scratch_shapes=[pltpu.VMEM((N_CHIP,) + s, BF16) for s in shard] * 2
        + [pltpu.SemaphoreType.DMA((nb,)), pltpu.SemaphoreType.DMA((nb,)),
           pltpu.SemaphoreType.DMA((N_CHIP - 1, nb)), pltpu.SemaphoreType.DMA((N_CHIP - 1, nb)),
           pltpu.SemaphoreType.DMA((N_DEV - 1,)), pltpu.SemaphoreType.DMA((N_DEV - 1,)), pltpu.VMEM((1, SMALL_LANES), F32)],
    )(*big_blocks, *small_grads, loss_tile)


def _adamw_math(w, g, m, v):
    m = ADAM_B1 * m + (1.0 - ADAM_B1) * g
    v = ADAM_B2 * v + (1.0 - ADAM_B2) * (g * g)
    m_hat = m / (1.0 - ADAM_B1 ** ADAM_STEP)
    v_hat = v / (1.0 - ADAM_B2 ** ADAM_STEP)
    return -ADAM_LR * (m_hat / (jnp.sqrt(v_hat) + ADAM_EPS) + ADAM_WD * w), m, v


def _reduce_adamw(name, parts, w, m, v, row_blocks):
    slots, rows, cols = parts.shape
    rb = rows // row_blocks

    def body(p_ref, w_ref, m_ref, v_ref, g_out, d_out, m_out, v_out):
        g = p_ref[0].astype(F32)
        for s in range(1, slots):
            g = g + p_ref[s].astype(F32)
        g_out[0] = g
        d_out[0], m_out[0], v_out[0] = _adamw_math(w_ref[0], g, m_ref[0], v_ref[0])

    blk = pl.BlockSpec((1, rb, cols), lambda i: (0, i, 0))
    return pl.pallas_call(
        body, name="reduce_adamw_" + name, grid=(row_blocks,),
        in_specs=[pl.BlockSpec((slots, rb, cols), lambda i: (0, i, 0)), blk, blk, blk],
        out_specs=[blk] * 4, out_shape=[_sds((1, rows, cols))] * 4,
        compiler_params=_ARB1,
    )(parts, w, m, v)


def _reduce_adamw_small(rows, ws, ms, vs):
    ns = len(SMALL)

    def body(r_ref, *refs):
        w_refs, m_refs, v_refs, outs = refs[:ns], refs[ns:2 * ns], refs[2 * ns:3 * ns], refs[3 * ns:]
        total = r_ref[0]
        for s in range(1, N_DEV):
            total = total + r_ref[s]
        off = 0
        for j, (_, cnt) in enumerate(SMALL):
            g = total[:, off:off + cnt]
            off += cnt
            outs[4 * j][...] = g
            outs[4 * j + 1][...], outs[4 * j + 2][...], outs[4 * j + 3][...] = _adamw_math(
                w_refs[j][...], g, m_refs[j][...], v_refs[j][...])
        outs[4 * ns][...] = total[:, off:off + LANES]

    vmem = pl.BlockSpec(memory_space=pltpu.VMEM)
    return pl.pallas_call(
        body, name="reduce_adamw_small",
        in_specs=[vmem] * (1 + 3 * ns), out_specs=[vmem] * (4 * ns + 1),
        out_shape=[_sds((1, cnt)) for _, cnt in SMALL for _ in range(4)] + [_sds((1, LANES))],
    )(rows, *ws, *ms, *vs)


def _shard_blocks(name, full):
    a = full[0]
    rows, cols = a.shape
    if name == "w_out":
        return a.reshape(N_CHIP, 2, rows // N_DEV, cols).transpose(1, 0, 2, 3)
    return a.reshape(rows, N_CHIP, 2, cols // N_DEV).transpose(2, 1, 0, 3)


def _unshard(name, blocks):
    _, rows, cols = blocks.shape
    if name == "w_out":
        return blocks.reshape(1, N_DEV * rows, cols)
    return blocks.transpose(1, 0, 2).reshape(1, rows, N_DEV * cols)


def kernel(x, positions, norm_pre_g, w_in, mla_q_norm_g, mla_w_uq, mla_kv_norm_g, mla_w_ukv, rw_mu, rw_w0, rw_w2, rw_a0, rw_a2, rw_k_k, rw_k_a, rw_r_k, rw_ln_g, rw_ln_b, w_out, norm_post_g, loss_target, m_norm_pre_g, m_w_in, m_mla_q_norm_g, m_mla_w_uq, m_mla_kv_norm_g, m_mla_w_ukv, m_rw_mu, m_rw_w0, m_rw_w2, m_rw_a0, m_rw_a2, m_rw_k_k, m_rw_k_a, m_rw_r_k, m_rw_ln_g, m_rw_ln_b, m_w_out, m_norm_post_g, v_norm_pre_g, v_w_in, v_mla_q_norm_g, v_mla_w_uq, v_mla_kv_norm_g, v_mla_w_ukv, v_rw_mu, v_rw_w0, v_rw_w2, v_rw_a0, v_rw_a2, v_rw_k_k, v_rw_k_a, v_rw_r_k, v_rw_ln_g, v_rw_ln_b, v_w_out, v_norm_post_g):
    given = dict(locals())
    w = {nm: given[nm] for nm in WEIGHTS}
    mom = {nm: given["m_" + nm] for nm in WEIGHTS}
    var = {nm: given["v_" + nm] for nm in WEIGHTS}
    sharded = list(SHARDED)

    gathered = _gather_shards([w[nm][0].astype(BF16) for nm in sharded])
    full = dict(w)
    for nm, blocks in zip(sharded, gathered):
        full[nm] = _unshard(nm, blocks)

    loss_part, grad_x, grads = _local_step(x, positions, loss_target, full)

    small_names = [nm for nm, _ in SMALL]
    row = lambda a: a.reshape(1, -1)
    got = _exchange_grads([_shard_blocks(nm, grads[nm]).astype(BF16) for nm in sharded],
                          [row(grads[nm]) for nm in small_names], loss_part)
    new = {}
    for nm, parts in zip(sharded, got[:-1]):
        new[nm] = _reduce_adamw(nm, parts, w[nm], mom[nm], var[nm], 4 if nm == "w_in" else 1)
    res = _reduce_adamw_small(got[-1], [row(w[nm]) for nm in small_names], [row(mom[nm]) for nm in small_names],
                              [row(var[nm]) for nm in small_names])
    for j, nm in enumerate(small_names):
        new[nm] = tuple(a.reshape(w[nm].shape) for a in res[4 * j:4 * j + 4])
    loss = res[-1][0, 0]
    return (loss, grad_x, *[new[nm][j] for j in range(4) for nm in WEIGHTS])
```

```python
import jax
import jax.numpy as jnp
from jax import lax
from jax.experimental import pallas as pl
from jax.experimental.pallas import tpu as pltpu

F32 = jnp.float32
BF16 = jnp.bfloat16

D_MODEL = 1024
MLA_HEADS = 4
MLA_NOPE = 128
MLA_ROPE = 64
MLA_V = 128
MLA_WIDTH = MLA_HEADS * MLA_V
Q_LORA = 256
KV_LORA = 128
ROPE_THETA = 10000.0
RW_HEAD = 64
RW_WIDTH = 512
RW_HEADS = RW_WIDTH // RW_HEAD
LORA = 64
RW_COLS = 3 * RW_WIDTH + 2 * LORA
MLA_COLS = Q_LORA + KV_LORA + MLA_ROPE
RW_GN_EPS = 64e-5
NORM_EPS = 1e-6
ATT_SCALE = (MLA_NOPE + MLA_ROPE) ** -0.5
ADAM_LR, ADAM_B1, ADAM_B2, ADAM_EPS, ADAM_WD, ADAM_STEP = 0.001, 0.9, 0.999, 1e-08, 0.01, 10
N_DEV = 8
LANES = 128
MXU = 256

PM_W = 512
WP_COLS = PM_W + RW_COLS + D_MODEL
RW_PIECES = ((0, 512), (512, 1024), (1024, 1536), (1536, 1664))

SHARDED = ("w_in", "mla_w_uq", "mla_w_ukv", "rw_w2", "rw_a2", "w_out")
SMALL = (("norm_pre_g", 1024), ("mla_q_norm_g", 256), ("mla_kv_norm_g", 128), ("rw_mu", 1664), ("rw_w0", 512),
         ("rw_a0", 512), ("rw_k_k", 512), ("rw_k_a", 512), ("rw_r_k", 512), ("rw_ln_g", 512), ("rw_ln_b", 512),
         ("norm_post_g", 1024))
SMALL_N = sum(n for _, n in SMALL)
WEIGHTS = ("norm_pre_g", "w_in", "mla_q_norm_g", "mla_w_uq", "mla_kv_norm_g", "mla_w_ukv", "rw_mu", "rw_w0", "rw_w2",
           "rw_a0", "rw_a2", "rw_k_k", "rw_k_a", "rw_r_k", "rw_ln_g", "rw_ln_b", "w_out", "norm_post_g")


def _seg_ones():
    r = lax.broadcasted_iota(jnp.int32, (MXU, MXU), 0) >> 6
    c = lax.broadcasted_iota(jnp.int32, (MXU, MXU), 1) >> 6
    return jnp.where(r == c, 1.0, 0.0).astype(BF16)


def _seg_dot(x, ones, passes):
    parts, rem = [], x
    for p in range(passes):
        hb = rem.astype(BF16)
        parts.append(hb)
        if p + 1 < passes:
            rem = rem - hb.astype(F32)
    outs = []
    for j in range(x.shape[1] // MXU):
        acc = None
        for hb in parts:
            d = jnp.dot(hb[:, MXU * j:MXU * (j + 1)], ones, preferred_element_type=F32)
            acc = d if acc is None else acc + d
        outs.append(acc)
    return outs[0] if len(outs) == 1 else jnp.concatenate(outs, axis=1)


def _seg_multi(xs, ones, passes):
    his = [x.astype(BF16) for x in xs]
    hi = jnp.concatenate(his, axis=0)
    if passes == 2:
        lo = jnp.concatenate([(x - h.astype(F32)).astype(BF16) for x, h in zip(xs, his)], axis=0)
        rhs = jnp.concatenate([ones, ones], axis=0)
    halves = []
    for j in range(hi.shape[1] // MXU):
        sl = slice(MXU * j, MXU * (j + 1))
        if passes == 2:
            halves.append(jnp.dot(jnp.concatenate([hi[:, sl], lo[:, sl]], axis=1), rhs, preferred_element_type=F32))
        else:
            halves.append(jnp.dot(hi[:, sl], ones, preferred_element_type=F32))
    full = jnp.concatenate(halves, axis=1)
    m = xs[0].shape[0]
    return [full[m * i:m * (i + 1)] for i in range(len(xs))]


@jax.custom_vjp
def _segsum(x):
    return _seg_dot(x, _seg_ones(), 2)


_segsum.defvjp(lambda x: (_segsum(x), None), lambda _, g: (_segsum(g),))


@jax.custom_vjp
def _bdot(a, w):
    return jnp.dot(a.astype(BF16), w.astype(BF16), preferred_element_type=F32)


def _bdot_fwd(a, w):
    return _bdot(a, w), (a, w)


def _bdot_bwd(res, g):
    a, w = res
    gb = g.astype(BF16)
    da = lax.dot_general(gb, w.astype(BF16), (((1,), (1,)), ((), ())), preferred_element_type=F32)
    dw = lax.dot_general(a.astype(BF16), gb, (((0,), (0,)), ((), ())), preferred_element_type=F32)
    return da, dw


_bdot.defvjp(_bdot_fwd, _bdot_bwd)


def _rot_impl(x):
    w = x.shape[1]
    lane = lax.broadcasted_iota(jnp.int32, x.shape, 1)
    return jnp.where((lane & 63) < 32, -pltpu.roll(x, w - 32, 1), pltpu.roll(x, 32, 1))


@jax.custom_vjp
def _rot(x):
    return _rot_impl(x)


_rot.defvjp(lambda x: (_rot_impl(x), None), lambda _, g: (-_rot_impl(g),))


def _rms(x, g):
    return x * lax.rsqrt(jnp.mean(x * x, axis=-1, keepdims=True) + NORM_EPS) * g


def _shift_rows(p, prev_row):
    row = lax.broadcasted_iota(jnp.int32, p.shape, 0)
    return jnp.where(row == 0, prev_row, pltpu.roll(p, 1, 0))


def _unshift_rows(g, next_row):
    row = lax.broadcasted_iota(jnp.int32, g.shape, 0)
    return jnp.where(row == g.shape[0] - 1, next_row, pltpu.roll(g, g.shape[0] - 1, 0))


def _f_mla(cq, ckv, kr, cos, sin, g_q, wqn, wqr, g_kv, wkv):
    qn = _rms(cq, g_q)
    q_nope = _bdot(qn, wqn)
    q_r = _bdot(qn, wqr)
    cos4 = jnp.concatenate([cos] * MLA_HEADS, axis=1)
    sin4 = jnp.concatenate([sin] * MLA_HEADS, axis=1)
    q_rope = q_r * cos4 + _rot(q_r) * sin4
    kv = _bdot(_rms(ckv, g_kv), wkv)
    k_rope = kr * cos + _rot(kr) * sin
    return q_nope, q_rope, kv, k_rope


def _f_rw(pr, pk, pv, pt, sr, sk, sv, st, mu_r, mu_k, mu_v, mu_t, w0, w2p, a0, a2p, k_k, k_a):
    r = pr + (sr - pr) * mu_r
    k = pk + (sk - pk) * mu_k
    v = pv + (sv - pv) * mu_v
    t = pt + (st - pt) * mu_t
    nwl = -(w0 + _bdot(jnp.tanh(t), w2p))
    softplus = jnp.maximum(nwl, 0.0) + jnp.log(1.0 + jnp.exp(-jnp.abs(nwl)))
    decay = jnp.exp(-jnp.exp(-softplus - 0.5))
    a = jax.nn.sigmoid(a0 + _bdot(t, a2p))
    kk = k * k_k
    kk = kk / jnp.maximum(jnp.sqrt(_segsum(kk * kk)), 1e-12)
    k2 = k * (1.0 + (a - 1.0) * k_a)
    return r, decay, k2, v, -kk, kk * a


def _f_head(ys, r, k, v, ym, z1, z2, x, tgt, ln_g, ln_b, r_k, w1, w2, g_post):
    inv = 1.0 / RW_HEAD
    yc = ys - _segsum(ys) * inv
    var = _segsum(yc * yc) * inv
    y = yc * lax.rsqrt(var + RW_GN_EPS) * ln_g + ln_b
    y_rw = y + _segsum(r * k * r_k) * v
    c1 = ym * (z1 * jax.nn.sigmoid(z1))
    c2 = y_rw * (z2 * jax.nn.sigmoid(z2))
    out = _bdot(c1, w1) + _bdot(c2, w2)
    err = x + _rms(out, g_post) - tgt
    per_row = jnp.sum(err * err, axis=1, keepdims=True)
    return jnp.sum(per_row, axis=0, keepdims=True) * (0.5 / D_MODEL)


def _rows(tm, width):
    return pl.BlockSpec((tm, width), lambda i: (i, 0))


def _whole(shape):
    zeros = (0,) * len(shape)
    return pl.BlockSpec(shape, lambda i: zeros)


def _sds(shape, dtype=F32):
    return jax.ShapeDtypeStruct(shape, dtype)


_ARB1 = pltpu.CompilerParams(dimension_semantics=("arbitrary",))


def _acc(ref, val, first):
    @pl.when(first)
    def _():
        ref[...] = val

    @pl.when(jnp.logical_not(first))
    def _():
        ref[...] += val


def _fwd_a(x2, g_pre, wp, tm):
    n = x2.shape[0]

    def body(x_ref, g_ref, w_ref, ut_ref, pm_ref, prw_ref, z_ref):
        u = _rms(x_ref[...], g_ref[...])
        ub = u.astype(BF16)
        ut_ref[...] = u.T.astype(BF16)
        pm_ref[...] = jnp.dot(ub, w_ref[:, 0:PM_W], preferred_element_type=F32)
        prw_ref[...] = jnp.dot(ub, w_ref[:, PM_W:PM_W + RW_COLS], preferred_element_type=F32)
        z_ref[...] = jnp.dot(ub, w_ref[:, PM_W + RW_COLS:WP_COLS], preferred_element_type=F32)

    return pl.pallas_call(
        body, name="fwd_a", grid=(n // tm,),
        in_specs=[_rows(tm, D_MODEL), _whole((1, D_MODEL)), _whole((D_MODEL, WP_COLS))],
        out_specs=[pl.BlockSpec((D_MODEL, tm), lambda i: (0, i)), _rows(tm, PM_W), _rows(tm, RW_COLS), _rows(tm, D_MODEL)],
        out_shape=[_sds((D_MODEL, n), BF16), _sds((n, PM_W)), _sds((n, RW_COLS)), _sds((n, D_MODEL))],
        compiler_params=_ARB1,
    )(x2, g_pre, wp)


def _rope_tables(pos_row, invf_col, tm):
    n = pos_row.shape[1]

    def body(p_ref, f_ref, c_ref, s_ref):
        distinct = MLA_ROPE // 2
        ang = f_ref[0:distinct, :] * p_ref[...].astype(F32)
        c_ref[...] = jnp.concatenate([jnp.cos(ang)] * (LANES // distinct), axis=0).T
        s_ref[...] = jnp.concatenate([jnp.sin(ang)] * (LANES // distinct), axis=0).T

    return pl.pallas_call(
        body, name="rope_tables", grid=(n // tm,),
        in_specs=[pl.BlockSpec((1, tm), lambda i: (0, i)), _whole((LANES, 1))],
        out_specs=[_rows(tm, LANES), _rows(tm, LANES)],
        out_shape=[_sds((n, LANES)), _sds((n, LANES))],
        compiler_params=_ARB1,
    )(pos_row, invf_col)


_B_WEIGHT_SHAPES = ((1, Q_LORA), (Q_LORA, 512), (Q_LORA, 512), (1, KV_LORA), (KV_LORA, 1024), (1, RW_COLS), (1, RW_WIDTH),
                    (LANES, RW_WIDTH), (1, RW_WIDTH), (LANES, RW_WIDTH), (1, RW_WIDTH), (1, RW_WIDTH))


def _halo_prev(tm):
    return pl.BlockSpec((8, RW_COLS), lambda i: (jnp.maximum(i * (tm // 8) - 1, 0), 0))


def _b_operands(pm_ref, prw_ref, halo_ref, wrefs, tile, tiles_per_seq):
    g_q, wqn, wqr, g_kv, wkv, mu, w0, w2p, a0, a2p, k_k, k_a = wrefs
    mla_in = (pm_ref[:, 0:Q_LORA], pm_ref[:, Q_LORA:Q_LORA + KV_LORA], pm_ref[:, Q_LORA + KV_LORA:PM_W])
    mla_w = (g_q[...], wqn[...], wqr[...], g_kv[...], wkv[...])
    keep = jnp.where(tile % tiles_per_seq == 0, 0.0, 1.0)
    prev = halo_ref[7:8, :] * keep
    ps = tuple(prw_ref[:, a:b] for a, b in RW_PIECES)
    ss = tuple(_shift_rows(p, prev[:, a:b]) for p, (a, b) in zip(ps, RW_PIECES))
    rw_w = tuple(mu[:, a:b] for a, b in RW_PIECES) + (w0[...], w2p[...], a0[...], a2p[...], k_k[...], k_a[...])
    return mla_in, mla_w, ps + ss, rw_w


def _fwd_b(pm, prw, cos, sin, bw, tm, tiles_per_seq):
    n = pm.shape[0]

    def body(pm_ref, prw_ref, halo_ref, cos_ref, sin_ref, *refs):
        wrefs, outs = refs[:12], refs[12:]
        mla_in, mla_w, rw_in, rw_w = _b_operands(pm_ref, prw_ref, halo_ref, wrefs, pl.program_id(0), tiles_per_seq)
        res = _f_mla(*mla_in, cos_ref[...], sin_ref[...], *mla_w) + _f_rw(*rw_in, *rw_w)
        for o_ref, val in zip(outs, res):
            o_ref[...] = val.astype(o_ref.dtype)

    widths = (512, 512, 1024, LANES) + (RW_WIDTH,) * 6
    return pl.pallas_call(
        body, name="fwd_b", grid=(n // tm,),
        in_specs=[_rows(tm, PM_W), _rows(tm, RW_COLS), _halo_prev(tm), _rows(tm, LANES), _rows(tm, LANES)]
        + [_whole(s) for s in _B_WEIGHT_SHAPES],
        out_specs=[_rows(tm, w) for w in widths],
        out_shape=[_sds((n, w), BF16 if j < 4 else F32) for j, w in enumerate(widths)],
        compiler_params=_ARB1,
    )(pm, prw, prw, cos, sin, *bw)


def _bwd_b(pm, prw, cos, sin, bw, cts, dkr_heads, tm, tiles_per_seq):
    n = pm.shape[0]

    ct_widths = (512, 512, 1024) + (RW_WIDTH,) * 9
    n_ct = len(ct_widths)

    def body(pm_ref, prw_ref, halo_ref, cos_ref, sin_ref, *refs):
        wrefs, ct_refs, dkr_ref = refs[:12], refs[12:12 + n_ct], refs[12 + n_ct]
        dpm_ref, dprw_ref, dps_ref = refs[13 + n_ct:16 + n_ct]
        wg_refs = refs[16 + n_ct:]
        tile = pl.program_id(0)
        first = tile == 0
        mla_in, mla_w, rw_in, rw_w = _b_operands(pm_ref, prw_ref, halo_ref, wrefs, tile, tiles_per_seq)
        cos, sin = cos_ref[...], sin_ref[...]
        ct = [r[...] for r in ct_refs]
        _, vjp_mla = jax.vjp(lambda *a: _f_mla(*a[:3], cos, sin, *a[3:]), *mla_in, *mla_w)
        dkr = dkr_ref[0] + dkr_ref[1] + dkr_ref[2] + dkr_ref[3]
        d_mla = vjp_mla((ct[0], ct[1], ct[2], dkr))
        dpm_ref[:, 0:Q_LORA] = d_mla[0]
        dpm_ref[:, Q_LORA:Q_LORA + KV_LORA] = d_mla[1]
        dpm_ref[:, Q_LORA + KV_LORA:PM_W] = d_mla[2]
        _, vjp_rw = jax.vjp(_f_rw, *rw_in, *rw_w)
        d_rw = vjp_rw((ct[3] + ct[4], ct[5], ct[6] + ct[7], ct[8] + ct[9], ct[10], ct[11]))
        for j, (a, b) in enumerate(RW_PIECES):
            dprw_ref[:, a:b] = d_rw[j]
            dps_ref[:, a:b] = d_rw[4 + j]
        g_q, wqn, wqr, g_kv, wkv, mu, w0, w2p, a0, a2p, k_k, k_a = wg_refs
        for ref, val in zip((g_q, wqn, wqr, g_kv, wkv), d_mla[3:]):
            _acc(ref, val, first)
        for j, (a, b) in enumerate(RW_PIECES):
            _acc(mu.at[:, a:b], d_rw[8 + j], first)
        for ref, val in zip((w0, w2p, a0, a2p, k_k, k_a), d_rw[12:]):
            _acc(ref, val, first)

    return pl.pallas_call(
        body, name="bwd_b", grid=(n // tm,),
        in_specs=[_rows(tm, PM_W), _rows(tm, RW_COLS), _halo_prev(tm), _rows(tm, LANES), _rows(tm, LANES)]
        + [_whole(s) for s in _B_WEIGHT_SHAPES] + [_rows(tm, w) for w in ct_widths]
        + [pl.BlockSpec((MLA_HEADS, tm, LANES), lambda i: (0, i, 0))],
        out_specs=[_rows(tm, PM_W), _rows(tm, RW_COLS), _rows(tm, RW_COLS)] + [_whole(s) for s in _B_WEIGHT_SHAPES],
        out_shape=[_sds((n, PM_W)), _sds((n, RW_COLS)), _sds((n, RW_COLS))] + [_sds(s) for s in _B_WEIGHT_SHAPES],
        compiler_params=_ARB1,
    )(pm, prw, prw, cos, sin, *bw, *cts, dkr_heads)


def _head(ys, r, k, v, ym, z, x2, tgt, hw, tm):
    n = x2.shape[0]
    h_shapes = ((1, RW_WIDTH), (1, RW_WIDTH), (1, RW_WIDTH), (D_MODEL, D_MODEL), (1, D_MODEL))

    def body(ys_ref, r_ref, k_ref, v_ref, ym_ref, z_ref, x_ref, t_ref, lng, lnb, rk, wout, gpost,
             dys_ref, dr_ref, dk_ref, dv_ref, dym_ref, dz_ref, dx_ref, loss_ref, dlng, dlnb, drk, dwout, dgpost):
        first = pl.program_id(0) == 0
        tgt_v = t_ref[...]
        args = (ys_ref[...], r_ref[...], k_ref[...], v_ref[...], ym_ref[...], z_ref[:, 0:MLA_WIDTH], z_ref[:, MLA_WIDTH:D_MODEL],
                x_ref[...], lng[...], lnb[...], rk[...], wout[0:MLA_WIDTH, :], wout[MLA_WIDTH:D_MODEL, :], gpost[...])
        loss, vjp = jax.vjp(lambda *a: _f_head(*a[:8], tgt_v, *a[8:]), *args)
        d = vjp(jnp.ones((1, 1), F32))
        dys_ref[...] = d[0]
        dr_ref[...] = d[1]
        dk_ref[...] = d[2]
        dv_ref[...] = d[3]
        dym_ref[...] = d[4].astype(BF16)
        dz_ref[:, 0:MLA_WIDTH] = d[5]
        dz_ref[:, MLA_WIDTH:D_MODEL] = d[6]
        dx_ref[...] = d[7]
        _acc(loss_ref, jnp.broadcast_to(loss, (8, LANES)), first)
        _acc(dlng, d[8], first)
        _acc(dlnb, d[9], first)
        _acc(drk, d[10], first)
        _acc(dwout.at[0:MLA_WIDTH, :], d[11], first)
        _acc(dwout.at[MLA_WIDTH:D_MODEL, :], d[12], first)
        _acc(dgpost, d[13], first)

    widths = (RW_WIDTH,) * 4 + (MLA_WIDTH, D_MODEL, D_MODEL)
    return pl.pallas_call(
        body, name="head", grid=(n // tm,),
        in_specs=[_rows(tm, RW_WIDTH)] * 4 + [_rows(tm, MLA_WIDTH), _rows(tm, D_MODEL), _rows(tm, D_MODEL), _rows(tm, D_MODEL)]
        + [_whole(s) for s in h_shapes],
        out_specs=[_rows(tm, w) for w in widths] + [_whole((8, LANES))] + [_whole(s) for s in h_shapes],
        out_shape=[_sds((n, w), BF16 if j == 4 else F32) for j, w in enumerate(widths)] + [_sds((8, LANES))]
        + [_sds(s) for s in h_shapes],
        compiler_params=_ARB1,
    )(ys, r, k, v, ym, z, x2, tgt, *hw)


def _halo_next(tm, n):
    last = n // 8 - 1
    return pl.BlockSpec((8, RW_COLS), lambda i: (jnp.minimum((i + 1) * (tm // 8), last), 0))


def _bwd_a(x2, g_pre, wp, dpm, dprw, dps, dz, dxres, tm, tiles_per_seq):
    n = x2.shape[0]
    nt_dims = (((1,), (1,)), ((), ()))

    def body(x_ref, g_ref, w_ref, dpm_ref, dprw_ref, dps_ref, nxt_ref, dz_ref, dxres_ref, gx_ref, dpb_ref, dg_ref):
        tile = pl.program_id(0)
        keep = jnp.where((tile + 1) % tiles_per_seq == 0, 0.0, 1.0)
        dprw_v = dprw_ref[...] + _unshift_rows(dps_ref[...], nxt_ref[0:1, :] * keep)
        dpm_b, dprw_b, dz_b = dpm_ref[...].astype(BF16), dprw_v.astype(BF16), dz_ref[...].astype(BF16)
        dpb_ref[:, 0:PM_W] = dpm_b
        dpb_ref[:, PM_W:PM_W + RW_COLS] = dprw_b
        dpb_ref[:, PM_W + RW_COLS:WP_COLS] = dz_b
        du = (lax.dot_general(dpm_b, w_ref[:, 0:PM_W], nt_dims, preferred_element_type=F32)
              + lax.dot_general(dprw_b, w_ref[:, PM_W:PM_W + RW_COLS], nt_dims, preferred_element_type=F32)
              + lax.dot_general(dz_b, w_ref[:, PM_W + RW_COLS:WP_COLS], nt_dims, preferred_element_type=F32))
        x = x_ref[...]
        xhat = x * lax.rsqrt(jnp.mean(x * x, axis=-1, keepdims=True) + NORM_EPS)
        dxn = du * g_ref[...]
        dx = (dxn - xhat * jnp.mean(dxn * xhat, axis=-1, keepdims=True)) * lax.rsqrt(jnp.mean(x * x, axis=-1, keepdims=True) + NORM_EPS)
        gx_ref[...] = dx + dxres_ref[...]
        _acc(dg_ref, jnp.sum(du * xhat, axis=0, keepdims=True), tile == 0)

    return pl.pallas_call(
        body, name="bwd_a", grid=(n // tm,),
        in_specs=[_rows(tm, D_MODEL), _whole((1, D_MODEL)), _whole((D_MODEL, WP_COLS)), _rows(tm, PM_W), _rows(tm, RW_COLS),
                  _rows(tm, RW_COLS), _halo_next(tm, n), _rows(tm, D_MODEL), _rows(tm, D_MODEL)],
        out_specs=[_rows(tm, D_MODEL), _rows(tm, WP_COLS), _whole((1, D_MODEL))],
        out_shape=[_sds((n, D_MODEL)), _sds((n, WP_COLS), BF16), _sds((1, D_MODEL))],
        compiler_params=_ARB1,
    )(x2, g_pre, wp, dpm, dprw, dps, dps, dz, dxres)


def _dw_in(ut, dpb, tk, tn):
    n = ut.shape[1]
    steps = n // tk

    def body(u_ref, d_ref, o_ref, acc_sc):
        k = pl.program_id(1)
        _acc(acc_sc, jnp.dot(u_ref[...], d_ref[...], preferred_element_type=F32), k == 0)

        @pl.when(k == steps - 1)
        def _():
            o_ref[...] = acc_sc[...].astype(BF16)

    return pl.pallas_call(
        body, name="dw_in", grid=(WP_COLS // tn, steps),
        in_specs=[pl.BlockSpec((D_MODEL, tk), lambda j, k: (0, k)), pl.BlockSpec((tk, tn), lambda j, k: (k, j))],
        out_specs=pl.BlockSpec((D_MODEL, tn), lambda j, k: (0, j)),
        out_shape=_sds((D_MODEL, WP_COLS), BF16),
        scratch_shapes=[pltpu.VMEM((D_MODEL, tn), F32)],
        compiler_params=pltpu.CompilerParams(dimension_semantics=("arbitrary", "arbitrary")),
    )(ut, dpb)


ATT_BLK = 256
_NT = (((1,), (1,)), ((), ()))
_TN = (((0,), (0,)), ((), ()))


def _causal(q0, k0, blk, blk_k=None):
    blk_k = blk if blk_k is None else blk_k
    row = q0 + lax.broadcasted_iota(jnp.int32, (blk, blk_k), 0)
    col = k0 + lax.broadcasted_iota(jnp.int32, (blk, blk_k), 1)
    return row >= col


def _attn_fwd(qn, qr, kv, kr):
    bsz, t, _ = qn.shape
    blk = min(ATT_BLK, t)

    heads = range(MLA_HEADS)

    def body(qn_ref, qr_ref, kv_ref, kr_ref, o_ref, lse_ref):
        qi = pl.program_id(1)
        q = [jnp.concatenate([qn_ref[:, LANES * h:LANES * (h + 1)], qr_ref[:, LANES * h:LANES * (h + 1)]], axis=1) for h in heads]
        lower = _causal(0, 0, blk)

        def kv_step(j, carry, diagonal):
            ks = j * blk
            k_rope = kr_ref[pl.ds(ks, blk), :]
            def score(h):
                k = jnp.concatenate([kv_ref[pl.ds(ks, blk), 2 * LANES * h:2 * LANES * h + LANES], k_rope], axis=1)
                return lax.dot_general(q[h], k, _NT, preferred_element_type=F32)

            out = []
            nxt = score(0)
            for h in heads:
                s = nxt * ATT_SCALE
                if h + 1 < MLA_HEADS:
                    nxt = score(h + 1)
                m, l, acc = carry[h]
                if diagonal:
                    s = jnp.where(lower, s, -1e30)
                m_new = jnp.maximum(m, jnp.max(s, axis=1, keepdims=True))
                alpha = jnp.exp(m - m_new)
                p = jnp.exp(s - m_new)
                l = alpha * l + jnp.sum(p, axis=1, keepdims=True)
                v = kv_ref[pl.ds(ks, blk), 2 * LANES * h + LANES:2 * LANES * (h + 1)]
                out.append((m_new, l, alpha * acc + jnp.dot(p.astype(BF16), v, preferred_element_type=F32)))
            return tuple(out)

        one = (jnp.full((blk, 1), -1e30, F32), jnp.zeros((blk, 1), F32), jnp.zeros((blk, MLA_V), F32))

        def whole(n_before):
            def run():
                c = (one,) * MLA_HEADS
                for j in range(n_before):
                    c = kv_step(j, c, False)
                return kv_step(n_before, c, True)
            return run

        carry = lax.switch(qi, [whole(nq) for nq in range(t // blk)])
        for h in heads:
            m, l, acc = carry[h]
            o_ref[:, LANES * h:LANES * (h + 1)] = acc / l
            lse_ref[h] = jnp.broadcast_to(m + jnp.log(l), (blk, LANES))

    return pl.pallas_call(
        body, name="attn_fwd", grid=(bsz, t // blk),
        in_specs=[pl.BlockSpec((None, blk, MLA_WIDTH), lambda b, i: (b, i, 0)),
                  pl.BlockSpec((None, blk, MLA_WIDTH), lambda b, i: (b, i, 0)),
                  pl.BlockSpec((None, t, 2 * MLA_WIDTH), lambda b, i: (b, 0, 0)),
                  pl.BlockSpec((None, t, LANES), lambda b, i: (b, 0, 0))],
        out_specs=[pl.BlockSpec((None, blk, MLA_WIDTH), lambda b, i: (b, i, 0)),
                   pl.BlockSpec((None, MLA_HEADS, blk, LANES), lambda b, i: (b, 0, i, 0))],
        out_shape=[_sds((bsz, t, MLA_WIDTH)), _sds((bsz, MLA_HEADS, t, LANES))],
        compiler_params=pltpu.CompilerParams(dimension_semantics=("arbitrary", "arbitrary")),
    )(qn, qr, kv, kr)


def _attn_bwd(qn, qr, kv, kr, o, do, lse):
    bsz, t, _ = qn.shape
    blk = min(ATT_BLK, t)
    nb = t // blk
    assert nb % 2 == 0, "query blocks are taken in pairs"

    def body(qn_ref, qr_ref, kn_ref, kr_ref, v_ref, o_ref, do_ref, lse_ref, dqn_ref, dqr_ref, dkv_ref, dkr_ref, dq_sc, delta_sc):
        dq_sc[...] = jnp.zeros_like(dq_sc)
        delta_sc[...] = jnp.sum(do_ref[...].astype(F32) * o_ref[...], axis=1, keepdims=True)

        lower = _causal(0, 0, blk)

        def q_blocks(j, k, vb, carry, blocks, diagonal):
            dk, dv = carry
            us = range(len(blocks))
            qs = [i * blk if isinstance(i, int) else pl.multiple_of(i * blk, blk) for i in blocks]
            q = [jnp.concatenate([qn_ref[pl.ds(qs[u], blk), :], qr_ref[pl.ds(qs[u], blk), :]], axis=1) for u in us]
            dob = [do_ref[pl.ds(qs[u], blk), :] for u in us]
            s = [lax.dot_general(q[u], k, _NT, preferred_element_type=F32) for u in us]
            dp = [lax.dot_general(dob[u], vb, _NT, preferred_element_type=F32) for u in us]
            pb, ds = [], []
            for u in us:
                p = jnp.exp(s[u] * ATT_SCALE - lse_ref[pl.ds(qs[u], blk), 0:1])
                if diagonal == u:
                    p = jnp.where(lower, p, 0.0)
                pb.append(p.astype(BF16))
                ds.append((p * (dp[u] - delta_sc[pl.ds(qs[u], blk), :]) * ATT_SCALE).astype(BF16))
            for u in us:
                dv = dv + lax.dot_general(pb[u], dob[u], _TN, preferred_element_type=F32)
            for u in us:
                dq_sc[pl.ds(qs[u], blk), :] += jnp.dot(ds[u], k, preferred_element_type=F32)
                dk = dk + lax.dot_general(ds[u], q[u], _TN, preferred_element_type=F32)
            return dk, dv

        for j in range(nb):
            ks = j * blk
            k = jnp.concatenate([kn_ref[ks:ks + blk, :], kr_ref[ks:ks + blk, :]], axis=1)
            vb = v_ref[ks:ks + blk, :]
            carry = (jnp.zeros((blk, 2 * LANES), F32), jnp.zeros((blk, MLA_V), F32))
            if j % 2 == 0:
                carry = q_blocks(j, k, vb, carry, [j, j + 1], 0)
            else:
                carry = q_blocks(j, k, vb, carry, [j], 0)
            pairs_from = j // 2 + 1
            if nb // 2 - pairs_from > 0:
                carry = lax.fori_loop(pairs_from, nb // 2,
                                      lambda pr, c, j=j, k=k, vb=vb: q_blocks(j, k, vb, c, [2 * pr, 2 * pr + 1], None),
                                      carry, unroll=nb // 2 - pairs_from)
            dk, dv = carry
            dkv_ref[ks:ks + blk, 0:LANES] = dk[:, 0:LANES]
            dkv_ref[ks:ks + blk, LANES:2 * LANES] = dv
            dkr_ref[ks:ks + blk, :] = dk[:, LANES:2 * LANES]
        dqn_ref[...] = dq_sc[:, 0:LANES]
        dqr_ref[...] = dq_sc[:, LANES:2 * LANES]

    head_col = lambda b, h: (b, 0, h)
    return pl.pallas_call(
        body, name="attn_bwd", grid=(bsz, MLA_HEADS),
        in_specs=[pl.BlockSpec((None, t, LANES), head_col), pl.BlockSpec((None, t, LANES), head_col),
                  pl.BlockSpec((None, t, LANES), lambda b, h: (b, 0, 2 * h)),
                  pl.BlockSpec((None, t, LANES), lambda b, h: (b, 0, 0)),
                  pl.BlockSpec((None, t, LANES), lambda b, h: (b, 0, 2 * h + 1)),
                  pl.BlockSpec((None, t, LANES), head_col), pl.BlockSpec((None, t, LANES), head_col),
                  pl.BlockSpec((None, None, t, LANES), lambda b, h: (b, h, 0, 0))],
        out_specs=[pl.BlockSpec((None, t, LANES), head_col), pl.BlockSpec((None, t, LANES), head_col),
                   pl.BlockSpec((None, t, 2 * LANES), head_col),
                   pl.BlockSpec((None, None, t, LANES), lambda b, h: (h, b, 0, 0))],
        out_shape=[_sds((bsz, t, MLA_WIDTH)), _sds((bsz, t, MLA_WIDTH)), _sds((bsz, t, 2 * MLA_WIDTH)),
                   _sds((MLA_HEADS, bsz, t, LANES))],
        scratch_shapes=[pltpu.VMEM((t, 2 * LANES), F32), pltpu.VMEM((t, 1), F32)],
        compiler_params=pltpu.CompilerParams(dimension_semantics=("arbitrary", "arbitrary")),
    )(qn, qr, kv, kr, kv, o, do, lse)


SCAN_CHUNK = 16


def _diag_mask():
    row = lax.broadcasted_iota(jnp.int32, (RW_HEAD, RW_WIDTH), 0)
    lane = lax.broadcasted_iota(jnp.int32, (RW_HEAD, RW_WIDTH), 1)
    return jnp.where(row == (lane & (RW_HEAD - 1)), 1.0, 0.0)


def _time_minor(a):
    bsz, t, _ = a.shape
    a = a.reshape(bsz, t // SCAN_CHUNK, SCAN_CHUNK, RW_HEADS, RW_HEAD)
    return a.transpose(0, 1, 4, 3, 2).reshape(bsz, t // SCAN_CHUNK, RW_HEAD, RW_HEADS * SCAN_CHUNK)


def _head_expand():
    l = lax.broadcasted_iota(jnp.int32, (2 * LANES, RW_WIDTH), 0)
    n = lax.broadcasted_iota(jnp.int32, (2 * LANES, RW_WIDTH), 1)
    return jnp.where(((l & (LANES - 1)) >> 4) == (n >> 6), 1.0, 0.0).astype(BF16)


BCAST_GROUP = 4


def _outer_chunk(tm_ref, row_ref, out_sc, expand, seqs):
    step_of_lane = lax.broadcasted_iota(jnp.int32, (RW_HEAD, LANES), 1) & (SCAN_CHUNK - 1)
    tiles = [tm_ref[bi, 0] for bi in seqs]
    for t0 in range(0, SCAN_CHUNK, BCAST_GROUP):
        parts = []
        for t in range(t0, t0 + BCAST_GROUP):
            for tile in tiles:
                a = jnp.where(step_of_lane == t, tile, 0.0)
                hi = a.astype(BF16)
                parts.append(jnp.concatenate([hi, (a - hi.astype(F32)).astype(BF16)], axis=1))
        cols = jnp.dot(jnp.concatenate(parts, axis=0), expand, preferred_element_type=F32)
        for j, t in enumerate(range(t0, t0 + BCAST_GROUP)):
            base = j * RW_HEAD * len(seqs)
            out_sc[t] = jnp.concatenate([cols[base + RW_HEAD * bi:base + RW_HEAD * (bi + 1)] * row_ref[bi, t:t + 1, :]
                                         for bi in seqs], axis=0)


def _fold8(x):
    acc = x[0:8]
    for j in range(1, x.shape[0] // 8):
        acc = acc + x[8 * j:8 * (j + 1)]
    return acc


def _rows8(at):
    return pl.ds(at * 8 if isinstance(at, int) else pl.multiple_of(at * 8, 8), 8)


def _put8(sc, bi, at, val):
    for j in range(RW_WIDTH // LANES):
        sc[bi * (RW_WIDTH // LANES) + j, _rows8(at), :] = val[:, LANES * j:LANES * (j + 1)]


def _unfold8(sc, bi, steps):
    tiles = []
    for j in range(RW_WIDTH // LANES):
        view = sc.at[bi * (RW_WIDTH // LANES) + j]
        acc = view[pl.ds(0, steps, stride=8), :]
        for s in range(1, 8):
            acc = acc + view[pl.ds(s, steps, stride=8), :]
        tiles.append(acc)
    return jnp.concatenate(tiles, axis=1)


def _scan_fwd(r, w, k, vt, nkk, b):
    bsz, t, _ = r.shape
    tc = SCAN_CHUNK

    def body(r_ref, w_ref, k_ref, n_ref, b_ref, vt_ref, y_ref, st_ref, s_sc, vc_sc, y_sc):
        @pl.when(pl.program_id(0) == 0)
        def _():
            s_sc[...] = jnp.zeros_like(s_sc)

        ones = _seg_ones()
        diag = _diag_mask()
        seqs = range(bsz)
        _outer_chunk(vt_ref, k_ref, vc_sc, _head_expand(), seqs)

        def put_y(ya, at):
            for bi in seqs:
                _put8(y_sc, bi, at, _fold8(ya[bi] * diag))

        def step(i, _):
            row = lambda ref, bi: ref[bi, pl.ds(i, 1), :]
            prev = jnp.maximum(i - 1, 0)
            s_old = [s_sc[bi] for bi in seqs]
            s_b = [s_old[bi].astype(BF16) for bi in seqs]
            sa = _seg_multi([s_b[bi] * row(n_ref, bi).astype(BF16) for bi in seqs], ones, 1)
            put_y(_seg_multi([s_b[bi] * r_ref[bi, pl.ds(prev, 1), :].astype(BF16) for bi in seqs], ones, 1), prev)
            vk = vc_sc[i]
            for bi in seqs:
                s_new = s_old[bi] * row(w_ref, bi) + sa[bi] * row(b_ref, bi) + vk[RW_HEAD * bi:RW_HEAD * (bi + 1)]
                s_sc[bi] = s_new
                st_ref[bi, i] = s_new
            return 0

        lax.fori_loop(0, tc, step, 0, unroll=8)
        put_y(_seg_multi([s_sc[bi].astype(BF16) * r_ref[bi, tc - 1:tc, :].astype(BF16) for bi in seqs], ones, 1), tc - 1)
        for bi in seqs:
            y_ref[bi] = _unfold8(y_sc, bi, tc)

    vec = pl.BlockSpec((bsz, tc, RW_WIDTH), lambda c: (0, c, 0))
    return pl.pallas_call(
        body, name="scan_fwd", grid=(t // tc,),
        in_specs=[vec] * 5 + [pl.BlockSpec((bsz, 1, RW_HEAD, LANES), lambda c: (0, c, 0, 0))],
        out_specs=[vec, pl.BlockSpec((bsz, tc, RW_HEAD, RW_WIDTH), lambda c: (0, c, 0, 0))],
        out_shape=[_sds((bsz, t, RW_WIDTH)), _sds((bsz, t, RW_HEAD, RW_WIDTH))],
        scratch_shapes=[pltpu.VMEM((bsz, RW_HEAD, RW_WIDTH), F32), pltpu.VMEM((tc, bsz * RW_HEAD, RW_WIDTH), F32),
                        pltpu.VMEM((bsz * RW_WIDTH // LANES, tc * 8, LANES), F32)],
        compiler_params=_ARB1,
    )(r, w, k, nkk, b, vt)


def _own_head_row(x):
    first_half = lax.broadcasted_iota(jnp.int32, (1, LANES), 1) < RW_HEAD
    tiles = [jnp.where(first_half, x[2 * j:2 * j + 1, LANES * j:LANES * (j + 1)], x[2 * j + 1:2 * j + 2, LANES * j:LANES * (j + 1)])
             for j in range(RW_WIDTH // LANES)]
    return jnp.concatenate(tiles, axis=1)


def _scan_bwd(r, w, k, vt, nkk, b, st, dyt):
    bsz, t, _ = r.shape
    tc = SCAN_CHUNK
    nc = t // tc

    def body(r_ref, w_ref, k_ref, n_ref, b_ref, vt_ref, dyt_ref, st_ref, halo_ref,
             dr_ref, dw_ref, dk_ref, dv_ref, dn_ref, db_ref, g_sc, dc_sc, v8_sc, dy8_sc, *part_scs):
        c = pl.program_id(0)

        @pl.when(c == 0)
        def _():
            g_sc[...] = jnp.zeros_like(g_sc)

        ones = _seg_ones()
        diag = _diag_mask()
        has_prev = jnp.where(c == nc - 1, 0.0, 1.0)
        seqs = range(bsz)
        _outer_chunk(dyt_ref, r_ref, dc_sc, _head_expand(), seqs)
        for bi in seqs:
            v8_sc[bi] = jnp.concatenate([vt_ref[bi, 0].T] * 2, axis=1)
            dy8_sc[bi] = jnp.concatenate([dyt_ref[bi, 0].T] * 2, axis=1)
        by_head = lambda sc, bi, i: sc.at[bi][pl.ds(i, RW_HEADS, stride=SCAN_CHUNK), :][:, 0:RW_HEAD].astype(BF16)
        dw_sc, dv_sc, dn_sc, db_sc = part_scs

        def step(i, s_p, s_t_b=None):
            static = isinstance(i, int)
            row = lambda ref, bi: ref[bi, i:i + 1, :] if static else ref[bi, pl.ds(i, 1), :]
            put_row = lambda ref, bi, val: ref.__setitem__((bi, slice(i, i + 1) if static else pl.ds(i, 1), slice(None)), val)
            if s_t_b is None:
                s_t_b = [st_ref[bi, i].astype(BF16) for bi in seqs]
            s_p_b = [s_p[bi].astype(BF16) for bi in seqs]
            dr8 = [jnp.dot(by_head(dy8_sc, bi, i), s_t_b[bi], preferred_element_type=F32) for bi in seqs]
            rowb = lambda ref, bi: row(ref, bi).astype(BF16)
            sa = _seg_multi([s_p_b[bi] * rowb(n_ref, bi) for bi in seqs], ones, 1)
            dc_all = dc_sc[i]
            dc = [dc_all[RW_HEAD * bi:RW_HEAD * (bi + 1)] for bi in seqs]
            g = [g_sc[bi] + dc[bi] for bi in seqs]
            g_b = [g[bi].astype(BF16) for bi in seqs]
            res = _seg_multi([g_b[bi] * rowb(b_ref, bi) for bi in seqs] + [g_b[bi] * rowb(k_ref, bi) for bi in seqs], ones, 1)
            dsa, dvb = res[:bsz], res[bsz:]
            for bi in seqs:
                dk8 = jnp.dot(by_head(v8_sc, bi, i), g_b[bi], preferred_element_type=F32)
                put_row(dr_ref, bi, _own_head_row(dr8[bi]))
                put_row(dk_ref, bi, _own_head_row(dk8))
                _put8(dv_sc, bi, i, _fold8(dvb[bi] * diag))
                _put8(dw_sc, bi, i, _fold8(g[bi] * s_p[bi]))
                _put8(db_sc, bi, i, _fold8(g[bi] * sa[bi]))
                _put8(dn_sc, bi, i, _fold8(s_p[bi] * dsa[bi]))
                g_sc[bi] = g[bi] * row(w_ref, bi) + dsa[bi] * row(n_ref, bi)
            return s_p_b

        group = 5

        def loop_trip(trip, _):
            top = tc - 1 - trip * group
            s_b = None
            for u in range(group):
                i = top - u
                s_b = step(i, [st_ref[bi, i - 1] for bi in seqs], s_b)
            return 0

        lax.fori_loop(0, (tc - 1) // group, loop_trip, 0)
        step(0, [halo_ref[bi, 0] * has_prev for bi in seqs])
        for out_ref, sc in zip((dw_ref, dv_ref, dn_ref, db_ref), part_scs):
            for bi in seqs:
                out_ref[bi] = _unfold8(sc, bi, tc)

    vec = pl.BlockSpec((bsz, tc, RW_WIDTH), lambda c: (0, nc - 1 - c, 0))
    tmin = pl.BlockSpec((bsz, 1, RW_HEAD, LANES), lambda c: (0, nc - 1 - c, 0, 0))
    parts = pltpu.VMEM((bsz * RW_WIDTH // LANES, tc * 8, LANES), F32)
    heads_steps = pltpu.VMEM((bsz, LANES, LANES), F32)
    return pl.pallas_call(
        body, name="scan_bwd", grid=(nc,),
        in_specs=[vec] * 5 + [tmin, tmin,
                              pl.BlockSpec((bsz, tc, RW_HEAD, RW_WIDTH), lambda c: (0, nc - 1 - c, 0, 0)),
                              pl.BlockSpec((bsz, 1, RW_HEAD, RW_WIDTH), lambda c: (0, jnp.maximum((nc - 1 - c) * tc - 1, 0), 0, 0))],
        out_specs=[vec] * 6,
        out_shape=[_sds((bsz, t, RW_WIDTH))] * 6,
        scratch_shapes=[pltpu.VMEM((bsz, RW_HEAD, RW_WIDTH), F32), pltpu.VMEM((tc, bsz * RW_HEAD, RW_WIDTH), F32),
                        heads_steps, heads_steps] + [parts] * 4,
        compiler_params=_ARB1,
    )(r, w, k, nkk, b, vt, dyt, st, st)


TOKEN_TILE = 256
VJP_TILE = 256


def _padded_weights(wt):
    f = lambda a: a.astype(F32)
    w_in = wt["w_in"][0].astype(BF16)
    zeros = lambda r, c: jnp.zeros((r, c), F32)
    wp = jnp.concatenate([w_in[:, :MLA_COLS], jnp.zeros((D_MODEL, PM_W - MLA_COLS), BF16), w_in[:, MLA_COLS:]], axis=1)
    w_uq = f(wt["mla_w_uq"][0]).reshape(Q_LORA, MLA_HEADS, MLA_NOPE + MLA_ROPE)
    wqn = w_uq[:, :, :MLA_NOPE].reshape(Q_LORA, MLA_HEADS * MLA_NOPE)
    wqr = jnp.concatenate([w_uq[:, :, MLA_NOPE:], jnp.zeros((Q_LORA, MLA_HEADS, LANES - MLA_ROPE), F32)], axis=2)
    wqr = wqr.reshape(Q_LORA, MLA_HEADS * LANES)
    w2p = jnp.concatenate([f(wt["rw_w2"][0]), zeros(LORA, RW_WIDTH)], axis=0)
    a2p = jnp.concatenate([zeros(LORA, RW_WIDTH), f(wt["rw_a2"][0])], axis=0)
    bw = (f(wt["mla_q_norm_g"]), wqn, wqr, f(wt["mla_kv_norm_g"]), f(wt["mla_w_ukv"][0]), f(wt["rw_mu"]), f(wt["rw_w0"]),
          w2p, f(wt["rw_a0"]), a2p, f(wt["rw_k_k"]), f(wt["rw_k_a"]))
    hw = (f(wt["rw_ln_g"]), f(wt["rw_ln_b"]), f(wt["rw_r_k"]).reshape(1, RW_WIDTH), f(wt["w_out"][0]), f(wt["norm_post_g"]))
    return wp, bw, hw


def _local_step(x, positions, target, wt):
    bsz, t, _ = x.shape
    n = bsz * t
    tm = min(TOKEN_TILE, t)
    tps = t // tm
    ts = min(VJP_TILE, t)
    wp, bw, hw = _padded_weights(wt)
    wpb = wp.astype(BF16)
    g_pre = wt["norm_pre_g"].astype(F32)
    x2 = x.reshape(n, D_MODEL)
    tgt2 = target.reshape(n, D_MODEL)
    inv_freq = ROPE_THETA ** (-jnp.arange(0, MLA_ROPE, 2, dtype=F32) / MLA_ROPE)
    invf = jnp.tile(inv_freq, LANES // (MLA_ROPE // 2)).reshape(LANES, 1)
    cos, sin = _rope_tables(positions.reshape(1, n), invf, tm)

    u, pm, prw, z = _fwd_a(x2, g_pre, wpb, tm)
    qn, qr, kv, kr, r, w, k, v, nkk, b = _fwd_b(pm, prw, cos, sin, bw, tm, tps)
    b3 = lambda a: a.reshape(bsz, t, a.shape[-1])
    ym, lse = _attn_fwd(b3(qn), b3(qr), b3(kv), b3(kr))
    vt = _time_minor(b3(v))
    ys, st = _scan_fwd(b3(r), b3(w), b3(k), vt, b3(nkk), b3(b))
    (dys, dr_h, dk_h, dv_h, dym, dz, dxres, loss, d_lng, d_lnb, d_rk, d_wout, d_gpost) = _head(
        ys.reshape(n, RW_WIDTH), r, k, v, ym.reshape(n, MLA_WIDTH), z, x2, tgt2, hw, ts)
    dqn, dqr, dkv, dkr_heads = _attn_bwd(b3(qn), b3(qr), b3(kv), b3(kr), ym, b3(dym), lse)
    dr_s, dw_s, dk_s, dv_s, dn_s, db_s = _scan_bwd(b3(r), b3(w), b3(k), vt, b3(nkk), b3(b), st, _time_minor(b3(dys)))
    f2 = lambda a: a.reshape(n, a.shape[-1])
    cts = (f2(dqn), f2(dqr), f2(dkv), f2(dr_s), dr_h, f2(dw_s), f2(dk_s), dk_h, f2(dv_s), dv_h, f2(dn_s), f2(db_s))
    (dpm, dprw, dps, d_gq, d_wqn, d_wqr, d_gkv, d_wkv, d_mu, d_w0, d_w2p, d_a0, d_a2p, d_kk, d_ka) = _bwd_b(
        pm, prw, cos, sin, bw, cts, dkr_heads.reshape(MLA_HEADS, n, LANES), ts, t // ts)
    grad_x, dpb, d_gpre = _bwd_a(x2, g_pre, wpb, dpm, dprw, dps, dz, dxres, tm, tps)
    d_wp = _dw_in(u, dpb, min(1024, n), 640)

    d_w_in = jnp.concatenate([d_wp[:, :MLA_COLS], d_wp[:, PM_W:]], axis=1)
    d_w_uq = jnp.concatenate([d_wqn.reshape(Q_LORA, MLA_HEADS, MLA_NOPE),
                              d_wqr.reshape(Q_LORA, MLA_HEADS, LANES)[:, :, :MLA_ROPE]], axis=2)
    grads = {
        "norm_pre_g": d_gpre, "w_in": d_w_in[None], "mla_q_norm_g": d_gq,
        "mla_w_uq": d_w_uq.reshape(1, Q_LORA, MLA_HEADS * (MLA_NOPE + MLA_ROPE)), "mla_kv_norm_g": d_gkv,
        "mla_w_ukv": d_wkv[None], "rw_mu": d_mu, "rw_w0": d_w0, "rw_w2": d_w2p[None, :LORA], "rw_a0": d_a0,
        "rw_a2": d_a2p[None, LORA:], "rw_k_k": d_kk, "rw_k_a": d_ka, "rw_r_k": d_rk.reshape(1, RW_HEADS, RW_HEAD),
        "rw_ln_g": d_lng, "rw_ln_b": d_lnb, "w_out": d_wout[None], "norm_post_g": d_gpost,
    }
    return loss, grad_x.reshape(bsz, t, D_MODEL), grads


_MESH = pl.DeviceIdType.MESH


def _gather_shards(shards):
    na = len(shards)

    def body(*refs):
        x_refs, out_refs = refs[:na], refs[na:2 * na]
        send_sems, recv_sems, local_sems = refs[2 * na:]
        x, y, c = lax.axis_index("x"), lax.axis_index("y"), lax.axis_index("c")
        me, sibling = (x, y, c), (x, y, 1 - c)
        chips = [(1 - x, y), (x, 1 - y), (1 - x, 1 - y)]
        arrays = range(na)

        def slot(a, px, py, pc):
            return out_refs[a].at[4 * px + 2 * py + pc]

        def copy(k, a, block, to, src=None):
            return pltpu.make_async_remote_copy(
                src_ref=slot(a, *block) if src is None else src, dst_ref=slot(a, *block),
                send_sem=send_sems.at[k, a], recv_sem=recv_sems.at[k, a], device_id=to, device_id_type=_MESH)

        mine = [pltpu.make_async_copy(x_refs[a], slot(a, *me), local_sems.at[a]) for a in arrays]
        for cp in mine:
            cp.start()
        first = [copy(0, a, me, sibling, src=x_refs[a]) for a in arrays]
        first += [copy(1 + j, a, me, (*chip, c), src=x_refs[a]) for j, chip in enumerate(chips) for a in arrays]
        for cp in first:
            cp.start()
        passed = []
        for j, chip in enumerate(chips):
            for a in arrays:
                copy(1 + j, a, (*chip, c), me).wait_recv()
                passed.append(copy(4 + j, a, (*chip, c), sibling))
                passed[-1].start()
        for a in arrays:
            copy(0, a, sibling, me).wait_recv()
        for j, chip in enumerate(chips):
            for a in arrays:
                copy(4 + j, a, (*chip, 1 - c), me).wait_recv()
        for cp in first + passed:
            cp.wait_send()
        for cp in mine:
            cp.wait()

    vmem = pl.BlockSpec(memory_space=pltpu.VMEM)
    return pl.pallas_call(
        body, name="gather_shards",
        out_shape=[_sds((N_DEV,) + a.shape, a.dtype) for a in shards],
        in_specs=[vmem] * na, out_specs=[vmem] * na,
        scratch_shapes=[pltpu.SemaphoreType.DMA((7, na)), pltpu.SemaphoreType.DMA((7, na)), pltpu.SemaphoreType.DMA((na,))],
    )(*shards)


SMALL_LANES = SMALL_N + LANES


N_CHIP = 4


def _exchange_grads(big_blocks, small_grads, loss_tile):
    nb = len(big_blocks)
    ns = len(small_grads)

    def body(*refs):
        big, small, loss_ref = refs[:nb], refs[nb:nb + ns], refs[nb + ns]
        out, rsmall = refs[nb + ns + 1:2 * nb + ns + 1], refs[2 * nb + ns + 1]
        scratch = refs[2 * nb + ns + 2:]
        stage, sums = scratch[:nb], scratch[nb:2 * nb]
        send1, recv1, send2, recv2, send_s, recv_s, row_sc = scratch[2 * nb:]
        x, y, c = lax.axis_index("x"), lax.axis_index("y"), lax.axis_index("c")
        me_lin = 4 * x + 2 * y + c
        my_chip = 2 * x + y
        sibling = (x, y, 1 - c)
        leaves = range(nb)

        to_sibling = [pltpu.make_async_remote_copy(
            src_ref=big[j].at[1 - c], dst_ref=stage[j], send_sem=send1.at[j], recv_sem=recv1.at[j],
            device_id=sibling, device_id_type=_MESH) for j in leaves]
        for cp in to_sibling:
            cp.start()
        off = 0
        for ref, (_, cnt) in zip(small, SMALL):
            row_sc[:, off:off + cnt] = ref[...]
            off += cnt
        row_sc[:, off:off + LANES] = loss_ref[0:1, :]
        rsmall[me_lin] = row_sc[...]
        rows = []
        for k in range(1, N_DEV):
            peer = (x ^ (k >> 2), y ^ ((k >> 1) & 1), c ^ (k & 1))
            rows.append(pltpu.make_async_remote_copy(
                src_ref=row_sc, dst_ref=rsmall.at[me_lin], send_sem=send_s.at[k - 1], recv_sem=recv_s.at[k - 1],
                device_id=peer, device_id_type=_MESH))
        for cp in rows:
            cp.start()
        to_chips = []
        for j in leaves:
            to_sibling[j].wait_recv()
            sums[j][...] = (big[j][c].astype(F32) + stage[j][...].astype(F32)).astype(BF16)
            out[j][0] = sums[j][my_chip]
            for q in range(1, N_CHIP):
                px, py = x ^ (q >> 1), y ^ (q & 1)
                to_chips.append(pltpu.make_async_remote_copy(
                    src_ref=sums[j].at[2 * px + py], dst_ref=out[j].at[q], send_sem=send2.at[q - 1, j],
                    recv_sem=recv2.at[q - 1, j], device_id=(px, py, c), device_id_type=_MESH))
                to_chips[-1].start()
        for cp in to_chips + rows:
            cp.wait_recv()
        for cp in to_sibling + to_chips + rows:
            cp.wait_send()

    vmem = pl.BlockSpec(memory_space=pltpu.VMEM)
    shard = [a.shape[2:] for a in big_blocks]
    return pl.pallas_call(
        body, name="exchange_grads",
        out_shape=[_sds((N_CHIP,) + s, BF16) for s in shard] + [_sds((N_DEV, 1, SMALL_LANES))],
        in_specs=[vmem] * (nb + ns + 1), out_specs=[vmem] * (nb + 1),
        scratch_shapes=[pltpu.VMEM((N_CHIP,) + s, BF16) for s in shard] * 2
        + [pltpu.SemaphoreType.DMA((nb,)), pltpu.SemaphoreType.DMA((nb,)),
           pltpu.SemaphoreType.DMA((N_CHIP - 1, nb)), pltpu.SemaphoreType.DMA((N_CHIP - 1, nb)),
           pltpu.SemaphoreType.DMA((N_DEV - 1,)), pltpu.SemaphoreType.DMA((N_DEV - 1,)), pltpu.VMEM((1, SMALL_LANES), F32)],
    )(*big_blocks, *small_grads, loss_tile)


def _adamw_math(w, g, m, v):
    m = ADAM_B1 * m + (1.0 - ADAM_B1) * g
    v = ADAM_B2 * v + (1.0 - ADAM_B2) * (g * g)
    m_hat = m / (1.0 - ADAM_B1 ** ADAM_STEP)
    v_hat = v / (1.0 - ADAM_B2 ** ADAM_STEP)
    return -ADAM_LR * (m_hat / (jnp.sqrt(v_hat) + ADAM_EPS) + ADAM_WD * w), m, v


def _reduce_adamw(name, parts, w, m, v, row_blocks):
    slots, rows, cols = parts.shape
    rb = rows // row_blocks

    def body(p_ref, w_ref, m_ref, v_ref, g_out, d_out, m_out, v_out):
        g = p_ref[0].astype(F32)
        for s in range(1, slots):
            g = g + p_ref[s].astype(F32)
        g_out[0] = g
        d_out[0], m_out[0], v_out[0] = _adamw_math(w_ref[0], g, m_ref[0], v_ref[0])

    blk = pl.BlockSpec((1, rb, cols), lambda i: (0, i, 0))
    return pl.pallas_call(
        body, name="reduce_adamw_" + name, grid=(row_blocks,),
        in_specs=[pl.BlockSpec((slots, rb, cols), lambda i: (0, i, 0)), blk, blk, blk],
        out_specs=[blk] * 4, out_shape=[_sds((1, rows, cols))] * 4,
        compiler_params=_ARB1,
    )(parts, w, m, v)


def _reduce_adamw_small(rows, ws, ms, vs):
    ns = len(SMALL)

    def body(r_ref, *refs):
        w_refs, m_refs, v_refs, outs = refs[:ns], refs[ns:2 * ns], refs[2 * ns:3 * ns], refs[3 * ns:]
        total = r_ref[0]
        for s in range(1, N_DEV):
            total = total + r_ref[s]
        off = 0
        for j, (_, cnt) in enumerate(SMALL):
            g = total[:, off:off + cnt]
            off += cnt
            outs[4 * j][...] = g
            outs[4 * j + 1][...], outs[4 * j + 2][...], outs[4 * j + 3][...] = _adamw_math(
                w_refs[j][...], g, m_refs[j][...], v_refs[j][...])
        outs[4 * ns][...] = total[:, off:off + LANES]

    vmem = pl.BlockSpec(memory_space=pltpu.VMEM)
    return pl.pallas_call(
        body, name="reduce_adamw_small",
        in_specs=[vmem] * (1 + 3 * ns), out_specs=[vmem] * (4 * ns + 1),
        out_shape=[_sds((1, cnt)) for _, cnt in SMALL for _ in range(4)] + [_sds((1, LANES))],
    )(rows, *ws, *ms, *vs)


def _shard_blocks(name, full):
    a = full[0]
    rows, cols = a.shape
    if name == "w_out":
        return a.reshape(N_CHIP, 2, rows // N_DEV, cols).transpose(1, 0, 2, 3)
    return a.reshape(rows, N_CHIP, 2, cols // N_DEV).transpose(2, 1, 0, 3)


def _unshard(name, blocks):
    _, rows, cols = blocks.shape
    if name == "w_out":
        return blocks.reshape(1, N_DEV * rows, cols)
    return blocks.transpose(1, 0, 2).reshape(1, rows, N_DEV * cols)


def kernel(x, positions, norm_pre_g, w_in, mla_q_norm_g, mla_w_uq, mla_kv_norm_g, mla_w_ukv, rw_mu, rw_w0, rw_w2, rw_a0, rw_a2, rw_k_k, rw_k_a, rw_r_k, rw_ln_g, rw_ln_b, w_out, norm_post_g, loss_target, m_norm_pre_g, m_w_in, m_mla_q_norm_g, m_mla_w_uq, m_mla_kv_norm_g, m_mla_w_ukv, m_rw_mu, m_rw_w0, m_rw_w2, m_rw_a0, m_rw_a2, m_rw_k_k, m_rw_k_a, m_rw_r_k, m_rw_ln_g, m_rw_ln_b, m_w_out, m_norm_post_g, v_norm_pre_g, v_w_in, v_mla_q_norm_g, v_mla_w_uq, v_mla_kv_norm_g, v_mla_w_ukv, v_rw_mu, v_rw_w0, v_rw_w2, v_rw_a0, v_rw_a2, v_rw_k_k, v_rw_k_a, v_rw_r_k, v_rw_ln_g, v_rw_ln_b, v_w_out, v_norm_post_g):
    given = dict(locals())
    w = {nm: given[nm] for nm in WEIGHTS}
    mom = {nm: given["m_" + nm] for nm in WEIGHTS}
    var = {nm: given["v_" + nm] for nm in WEIGHTS}
    sharded = list(SHARDED)

    gathered = _gather_shards([w[nm][0].astype(BF16) for nm in sharded])
    full = dict(w)
    for nm, blocks in zip(sharded, gathered):
        full[nm] = _unshard(nm, blocks)

    loss_part, grad_x, grads = _local_step(x, positions, loss_target, full)

    small_names = [nm for nm, _ in SMALL]
    row = lambda a: a.reshape(1, -1)
    got = _exchange_grads([_shard_blocks(nm, grads[nm]).astype(BF16) for nm in sharded],
                          [row(grads[nm]) for nm in small_names], loss_part)
    new = {}
    for nm, parts in zip(sharded, got[:-1]):
        new[nm] = _reduce_adamw(nm, parts, w[nm], mom[nm], var[nm], 4 if nm == "w_in" else 1)
    res = _reduce_adamw_small(got[-1], [row(w[nm]) for nm in small_names], [row(mom[nm]) for nm in small_names],
                              [row(var[nm]) for nm in small_names])
    for j, nm in enumerate(small_names):
        new[nm] = tuple(a.reshape(w[nm].shape) for a in res[4 * j:4 * j + 4])
    loss = res[-1][0, 0]
    return (loss, grad_x, *[new[nm][j] for j in range(4) for nm in WEIGHTS])
```

```python
import jax
import jax.numpy as jnp
from jax import lax
from jax.experimental import pallas as pl
from jax.experimental.pallas import tpu as pltpu

F32 = jnp.float32
BF16 = jnp.bfloat16

D_MODEL = 1024
MLA_HEADS = 4
MLA_NOPE = 128
MLA_ROPE = 64
MLA_V = 128
MLA_WIDTH = MLA_HEADS * MLA_V
Q_LORA = 256
KV_LORA = 128
ROPE_THETA = 10000.0
RW_HEAD = 64
RW_WIDTH = 512
RW_HEADS = RW_WIDTH // RW_HEAD
LORA = 64
RW_COLS = 3 * RW_WIDTH + 2 * LORA
MLA_COLS = Q_LORA + KV_LORA + MLA_ROPE
RW_GN_EPS = 64e-5
NORM_EPS = 1e-6
ATT_SCALE = (MLA_NOPE + MLA_ROPE) ** -0.5
ADAM_LR, ADAM_B1, ADAM_B2, ADAM_EPS, ADAM_WD, ADAM_STEP = 0.001, 0.9, 0.999, 1e-08, 0.01, 10
N_DEV = 8
LANES = 128
MXU = 256

PM_W = 512
WP_COLS = PM_W + RW_COLS + D_MODEL
RW_PIECES = ((0, 512), (512, 1024), (1024, 1536), (1536, 1664))

SHARDED = ("w_in", "mla_w_uq", "mla_w_ukv", "rw_w2", "rw_a2", "w_out")
SMALL = (("norm_pre_g", 1024), ("mla_q_norm_g", 256), ("mla_kv_norm_g", 128), ("rw_mu", 1664), ("rw_w0", 512),
         ("rw_a0", 512), ("rw_k_k", 512), ("rw_k_a", 512), ("rw_r_k", 512), ("rw_ln_g", 512), ("rw_ln_b", 512),
         ("norm_post_g", 1024))
SMALL_N = sum(n for _, n in SMALL)
WEIGHTS = ("norm_pre_g", "w_in", "mla_q_norm_g", "mla_w_uq", "mla_kv_norm_g", "mla_w_ukv", "rw_mu", "rw_w0", "rw_w2",
           "rw_a0", "rw_a2", "rw_k_k", "rw_k_a", "rw_r_k", "rw_ln_g", "rw_ln_b", "w_out", "norm_post_g")


def _seg_ones():
    r = lax.broadcasted_iota(jnp.int32, (MXU, MXU), 0) >> 6
    c = lax.broadcasted_iota(jnp.int32, (MXU, MXU), 1) >> 6
    return jnp.where(r == c, 1.0, 0.0).astype(BF16)


def _seg_dot(x, ones, passes):
    parts, rem = [], x
    for p in range(passes):
        hb = rem.astype(BF16)
        parts.append(hb)
        if p + 1 < passes:
            rem = rem - hb.astype(F32)
    outs = []
    for j in range(x.shape[1] // MXU):
        acc = None
        for hb in parts:
            d = jnp.dot(hb[:, MXU * j:MXU * (j + 1)], ones, preferred_element_type=F32)
            acc = d if acc is None else acc + d
        outs.append(acc)
    return outs[0] if len(outs) == 1 else jnp.concatenate(outs, axis=1)


def _seg_multi(xs, ones, passes):
    his = [x.astype(BF16) for x in xs]
    hi = jnp.concatenate(his, axis=0)
    if passes == 2:
        lo = jnp.concatenate([(x - h.astype(F32)).astype(BF16) for x, h in zip(xs, his)], axis=0)
        rhs = jnp.concatenate([ones, ones], axis=0)
    halves = []
    for j in range(hi.shape[1] // MXU):
        sl = slice(MXU * j, MXU * (j + 1))
        if passes == 2:
            halves.append(jnp.dot(jnp.concatenate([hi[:, sl], lo[:, sl]], axis=1), rhs, preferred_element_type=F32))
        else:
            halves.append(jnp.dot(hi[:, sl], ones, preferred_element_type=F32))
    full = jnp.concatenate(halves, axis=1)
    m = xs[0].shape[0]
    return [full[m * i:m * (i + 1)] for i in range(len(xs))]


@jax.custom_vjp
def _segsum(x):
    return _seg_dot(x, _seg_ones(), 2)


_segsum.defvjp(lambda x: (_segsum(x), None), lambda _, g: (_segsum(g),))


@jax.custom_vjp
def _bdot(a, w):
    return jnp.dot(a.astype(BF16), w.astype(BF16), preferred_element_type=F32)


def _bdot_fwd(a, w):
    return _bdot(a, w), (a, w)


def _bdot_bwd(res, g):
    a, w = res
    gb = g.astype(BF16)
    da = lax.dot_general(gb, w.astype(BF16), (((1,), (1,)), ((), ())), preferred_element_type=F32)
    dw = lax.dot_general(a.astype(BF16), gb, (((0,), (0,)), ((), ())), preferred_element_type=F32)
    return da, dw


_bdot.defvjp(_bdot_fwd, _bdot_bwd)


def _rot_impl(x):
    w = x.shape[1]
    lane = lax.broadcasted_iota(jnp.int32, x.shape, 1)
    return jnp.where((lane & 63) < 32, -pltpu.roll(x, w - 32, 1), pltpu.roll(x, 32, 1))


@jax.custom_vjp
def _rot(x):
    return _rot_impl(x)


_rot.defvjp(lambda x: (_rot_impl(x), None), lambda _, g: (-_rot_impl(g),))


def _rms(x, g):
    return x * lax.rsqrt(jnp.mean(x * x, axis=-1, keepdims=True) + NORM_EPS) * g


def _shift_rows(p, prev_row):
    row = lax.broadcasted_iota(jnp.int32, p.shape, 0)
    return jnp.where(row == 0, prev_row, pltpu.roll(p, 1, 0))


def _unshift_rows(g, next_row):
    row = lax.broadcasted_iota(jnp.int32, g.shape, 0)
    return jnp.where(row == g.shape[0] - 1, next_row, pltpu.roll(g, g.shape[0] - 1, 0))


def _f_mla(cq, ckv, kr, cos, sin, g_q, wqn, wqr, g_kv, wkv):
    qn = _rms(cq, g_q)
    q_nope = _bdot(qn, wqn)
    q_r = _bdot(qn, wqr)
    cos4 = jnp.concatenate([cos] * MLA_HEADS, axis=1)
    sin4 = jnp.concatenate([sin] * MLA_HEADS, axis=1)
    q_rope = q_r * cos4 + _rot(q_r) * sin4
    kv = _bdot(_rms(ckv, g_kv), wkv)
    k_rope = kr * cos + _rot(kr) * sin
    return q_nope, q_rope, kv, k_rope


def _f_rw(pr, pk, pv, pt, sr, sk, sv, st, mu_r, mu_k, mu_v, mu_t, w0, w2p, a0, a2p, k_k, k_a):
    r = pr + (sr - pr) * mu_r
    k = pk + (sk - pk) * mu_k
    v = pv + (sv - pv) * mu_v
    t = pt + (st - pt) * mu_t
    nwl = -(w0 + _bdot(jnp.tanh(t), w2p))
    softplus = jnp.maximum(nwl, 0.0) + jnp.log(1.0 + jnp.exp(-jnp.abs(nwl)))
    decay = jnp.exp(-jnp.exp(-softplus - 0.5))
    a = jax.nn.sigmoid(a0 + _bdot(t, a2p))
    kk = k * k_k
    kk = kk / jnp.maximum(jnp.sqrt(_segsum(kk * kk)), 1e-12)
    k2 = k * (1.0 + (a - 1.0) * k_a)
    return r, decay, k2, v, -kk, kk * a


def _f_head(ys, r, k, v, ym, z1, z2, x, tgt, ln_g, ln_b, r_k, w1, w2, g_post):
    inv = 1.0 / RW_HEAD
    yc = ys - _segsum(ys) * inv
    var = _segsum(yc * yc) * inv
    y = yc * lax.rsqrt(var + RW_GN_EPS) * ln_g + ln_b
    y_rw = y + _segsum(r * k * r_k) * v
    c1 = ym * (z1 * jax.nn.sigmoid(z1))
    c2 = y_rw * (z2 * jax.nn.sigmoid(z2))
    out = _bdot(c1, w1) + _bdot(c2, w2)
    err = x + _rms(out, g_post) - tgt
    per_row = jnp.sum(err * err, axis=1, keepdims=True)
    return jnp.sum(per_row, axis=0, keepdims=True) * (0.5 / D_MODEL)


def _rows(tm, width):
    return pl.BlockSpec((tm, width), lambda i: (i, 0))


def _whole(shape):
    zeros = (0,) * len(shape)
    return pl.BlockSpec(shape, lambda i: zeros)


def _sds(shape, dtype=F32):
    return jax.ShapeDtypeStruct(shape, dtype)


_ARB1 = pltpu.CompilerParams(dimension_semantics=("arbitrary",))


def _acc(ref, val, first):
    @pl.when(first)
    def _():
        ref[...] = val

    @pl.when(jnp.logical_not(first))
    def _():
        ref[...] += val


def _fwd_a(x2, g_pre, wp, tm):
    n = x2.shape[0]

    def body(x_ref, g_ref, w_ref, ut_ref, pm_ref, prw_ref, z_ref):
        u = _rms(x_ref[...], g_ref[...])
        ub = u.astype(BF16)
        ut_ref[...] = u.T.astype(BF16)
        pm_ref[...] = jnp.dot(ub, w_ref[:, 0:PM_W], preferred_element_type=F32)
        prw_ref[...] = jnp.dot(ub, w_ref[:, PM_W:PM_W + RW_COLS], preferred_element_type=F32)
        z_ref[...] = jnp.dot(ub, w_ref[:, PM_W + RW_COLS:WP_COLS], preferred_element_type=F32)

    return pl.pallas_call(
        body, name="fwd_a", grid=(n // tm,),
        in_specs=[_rows(tm, D_MODEL), _whole((1, D_MODEL)), _whole((D_MODEL, WP_COLS))],
        out_specs=[pl.BlockSpec((D_MODEL, tm), lambda i: (0, i)), _rows(tm, PM_W), _rows(tm, RW_COLS), _rows(tm, D_MODEL)],
        out_shape=[_sds((D_MODEL, n), BF16), _sds((n, PM_W)), _sds((n, RW_COLS)), _sds((n, D_MODEL))],
        compiler_params=_ARB1,
    )(x2, g_pre, wp)


def _rope_tables(pos_row, invf_col, tm):
    n = pos_row.shape[1]

    def body(p_ref, f_ref, c_ref, s_ref):
        distinct = MLA_ROPE // 2
        ang = f_ref[0:distinct, :] * p_ref[...].astype(F32)
        c_ref[...] = jnp.concatenate([jnp.cos(ang)] * (LANES // distinct), axis=0).T
        s_ref[...] = jnp.concatenate([jnp.sin(ang)] * (LANES // distinct), axis=0).T

    return pl.pallas_call(
        body, name="rope_tables", grid=(n // tm,),
        in_specs=[pl.BlockSpec((1, tm), lambda i: (0, i)), _whole((LANES, 1))],
        out_specs=[_rows(tm, LANES), _rows(tm, LANES)],
        out_shape=[_sds((n, LANES)), _sds((n, LANES))],
        compiler_params=_ARB1,
    )(pos_row, invf_col)


_B_WEIGHT_SHAPES = ((1, Q_LORA), (Q_LORA, 512), (Q_LORA, 512), (1, KV_LORA), (KV_LORA, 1024), (1, RW_COLS), (1, RW_WIDTH),
                    (LANES, RW_WIDTH), (1, RW_WIDTH), (LANES, RW_WIDTH), (1, RW_WIDTH), (1, RW_WIDTH))


def _halo_prev(tm):
    return pl.BlockSpec((8, RW_COLS), lambda i: (jnp.maximum(i * (tm // 8) - 1, 0), 0))


def _b_operands(pm_ref, prw_ref, halo_ref, wrefs, tile, tiles_per_seq):
    g_q, wqn, wqr, g_kv, wkv, mu, w0, w2p, a0, a2p, k_k, k_a = wrefs
    mla_in = (pm_ref[:, 0:Q_LORA], pm_ref[:, Q_LORA:Q_LORA + KV_LORA], pm_ref[:, Q_LORA + KV_LORA:PM_W])
    mla_w = (g_q[...], wqn[...], wqr[...], g_kv[...], wkv[...])
    keep = jnp.where(tile % tiles_per_seq == 0, 0.0, 1.0)
    prev = halo_ref[7:8, :] * keep
    ps = tuple(prw_ref[:, a:b] for a, b in RW_PIECES)
    ss = tuple(_shift_rows(p, prev[:, a:b]) for p, (a, b) in zip(ps, RW_PIECES))
    rw_w = tuple(mu[:, a:b] for a, b in RW_PIECES) + (w0[...], w2p[...], a0[...], a2p[...], k_k[...], k_a[...])
    return mla_in, mla_w, ps + ss, rw_w


def _fwd_b(pm, prw, cos, sin, bw, tm, tiles_per_seq):
    n = pm.shape[0]

    def body(pm_ref, prw_ref, halo_ref, cos_ref, sin_ref, *refs):
        wrefs, outs = refs[:12], refs[12:]
        mla_in, mla_w, rw_in, rw_w = _b_operands(pm_ref, prw_ref, halo_ref, wrefs, pl.program_id(0), tiles_per_seq)
        res = _f_mla(*mla_in, cos_ref[...], sin_ref[...], *mla_w) + _f_rw(*rw_in, *rw_w)
        for o_ref, val in zip(outs, res):
            o_ref[...] = val.astype(o_ref.dtype)

    widths = (512, 512, 1024, LANES) + (RW_WIDTH,) * 6
    return pl.pallas_call(
        body, name="fwd_b", grid=(n // tm,),
        in_specs=[_rows(tm, PM_W), _rows(tm, RW_COLS), _halo_prev(tm), _rows(tm, LANES), _rows(tm, LANES)]
        + [_whole(s) for s in _B_WEIGHT_SHAPES],
        out_specs=[_rows(tm, w) for w in widths],
        out_shape=[_sds((n, w), BF16 if j < 4 else F32) for j, w in enumerate(widths)],
        compiler_params=_ARB1,
    )(pm, prw, prw, cos, sin, *bw)


def _bwd_b(pm, prw, cos, sin, bw, cts, dkr_heads, tm, tiles_per_seq):
    n = pm.shape[0]

    ct_widths = (512, 512, 1024) + (RW_WIDTH,) * 9
    n_ct = len(ct_widths)

    def body(pm_ref, prw_ref, halo_ref, cos_ref, sin_ref, *refs):
        wrefs, ct_refs, dkr_ref = refs[:12], refs[12:12 + n_ct], refs[12 + n_ct]
        dpm_ref, dprw_ref, dps_ref = refs[13 + n_ct:16 + n_ct]
        wg_refs = refs[16 + n_ct:]
        tile = pl.program_id(0)
        first = tile == 0
        mla_in, mla_w, rw_in, rw_w = _b_operands(pm_ref, prw_ref, halo_ref, wrefs, tile, tiles_per_seq)
        cos, sin = cos_ref[...], sin_ref[...]
        ct = [r[...] for r in ct_refs]
        _, vjp_mla = jax.vjp(lambda *a: _f_mla(*a[:3], cos, sin, *a[3:]), *mla_in, *mla_w)
        dkr = dkr_ref[0] + dkr_ref[1] + dkr_ref[2] + dkr_ref[3]
        d_mla = vjp_mla((ct[0], ct[1], ct[2], dkr))
        dpm_ref[:, 0:Q_LORA] = d_mla[0]
        dpm_ref[:, Q_LORA:Q_LORA + KV_LORA] = d_mla[1]
        dpm_ref[:, Q_LORA + KV_LORA:PM_W] = d_mla[2]
        _, vjp_rw = jax.vjp(_f_rw, *rw_in, *rw_w)
        d_rw = vjp_rw((ct[3] + ct[4], ct[5], ct[6] + ct[7], ct[8] + ct[9], ct[10], ct[11]))
        for j, (a, b) in enumerate(RW_PIECES):
            dprw_ref[:, a:b] = d_rw[j]
            dps_ref[:, a:b] = d_rw[4 + j]
        g_q, wqn, wqr, g_kv, wkv, mu, w0, w2p, a0, a2p, k_k, k_a = wg_refs
        for ref, val in zip((g_q, wqn, wqr, g_kv, wkv), d_mla[3:]):
            _acc(ref, val, first)
        for j, (a, b) in enumerate(RW_PIECES):
            _acc(mu.at[:, a:b], d_rw[8 + j], first)
        for ref, val in zip((w0, w2p, a0, a2p, k_k, k_a), d_rw[12:]):
            _acc(ref, val, first)

    return pl.pallas_call(
        body, name="bwd_b", grid=(n // tm,),
        in_specs=[_rows(tm, PM_W), _rows(tm, RW_COLS), _halo_prev(tm), _rows(tm, LANES), _rows(tm, LANES)]
        + [_whole(s) for s in _B_WEIGHT_SHAPES] + [_rows(tm, w) for w in ct_widths]
        + [pl.BlockSpec((MLA_HEADS, tm, LANES), lambda i: (0, i, 0))],
        out_specs=[_rows(tm, PM_W), _rows(tm, RW_COLS), _rows(tm, RW_COLS)] + [_whole(s) for s in _B_WEIGHT_SHAPES],
        out_shape=[_sds((n, PM_W)), _sds((n, RW_COLS)), _sds((n, RW_COLS))] + [_sds(s) for s in _B_WEIGHT_SHAPES],
        compiler_params=_ARB1,
    )(pm, prw, prw, cos, sin, *bw, *cts, dkr_heads)


def _head(ys, r, k, v, ym, z, x2, tgt, hw, tm):
    n = x2.shape[0]
    h_shapes = ((1, RW_WIDTH), (1, RW_WIDTH), (1, RW_WIDTH), (D_MODEL, D_MODEL), (1, D_MODEL))

    def body(ys_ref, r_ref, k_ref, v_ref, ym_ref, z_ref, x_ref, t_ref, lng, lnb, rk, wout, gpost,
             dys_ref, dr_ref, dk_ref, dv_ref, dym_ref, dz_ref, dx_ref, loss_ref, dlng, dlnb, drk, dwout, dgpost):
        first = pl.program_id(0) == 0
        tgt_v = t_ref[...]
        args = (ys_ref[...], r_ref[...], k_ref[...], v_ref[...], ym_ref[...], z_ref[:, 0:MLA_WIDTH], z_ref[:, MLA_WIDTH:D_MODEL],
                x_ref[...], lng[...], lnb[...], rk[...], wout[0:MLA_WIDTH, :], wout[MLA_WIDTH:D_MODEL, :], gpost[...])
        loss, vjp = jax.vjp(lambda *a: _f_head(*a[:8], tgt_v, *a[8:]), *args)
        d = vjp(jnp.ones((1, 1), F32))
        dys_ref[...] = d[0]
        dr_ref[...] = d[1]
        dk_ref[...] = d[2]
        dv_ref[...] = d[3]
        dym_ref[...] = d[4].astype(BF16)
        dz_ref[:, 0:MLA_WIDTH] = d[5]
        dz_ref[:, MLA_WIDTH:D_MODEL] = d[6]
        dx_ref[...] = d[7]
        _acc(loss_ref, jnp.broadcast_to(loss, (8, LANES)), first)
        _acc(dlng, d[8], first)
        _acc(dlnb, d[9], first)
        _acc(drk, d[10], first)
        _acc(dwout.at[0:MLA_WIDTH, :], d[11], first)
        _acc(dwout.at[MLA_WIDTH:D_MODEL, :], d[12], first)
        _acc(dgpost, d[13], first)

    widths = (RW_WIDTH,) * 4 + (MLA_WIDTH, D_MODEL, D_MODEL)
    return pl.pallas_call(
        body, name="head", grid=(n // tm,),
        in_specs=[_rows(tm, RW_WIDTH)] * 4 + [_rows(tm, MLA_WIDTH), _rows(tm, D_MODEL), _rows(tm, D_MODEL), _rows(tm, D_MODEL)]
        + [_whole(s) for s in h_shapes],
        out_specs=[_rows(tm, w) for w in widths] + [_whole((8, LANES))] + [_whole(s) for s in h_shapes],
        out_shape=[_sds((n, w), BF16 if j == 4 else F32) for j, w in enumerate(widths)] + [_sds((8, LANES))]
        + [_sds(s) for s in h_shapes],
        compiler_params=_ARB1,
    )(ys, r, k, v, ym, z, x2, tgt, *hw)


def _halo_next(tm, n):
    last = n // 8 - 1
    return pl.BlockSpec((8, RW_COLS), lambda i: (jnp.minimum((i + 1) * (tm // 8), last), 0))


def _bwd_a(x2, g_pre, wp, dpm, dprw, dps, dz, dxres, tm, tiles_per_seq):
    n = x2.shape[0]
    nt_dims = (((1,), (1,)), ((), ()))

    def body(x_ref, g_ref, w_ref, dpm_ref, dprw_ref, dps_ref, nxt_ref, dz_ref, dxres_ref, gx_ref, dpb_ref, dg_ref):
        tile = pl.program_id(0)
        keep = jnp.where((tile + 1) % tiles_per_seq == 0, 0.0, 1.0)
        dprw_v = dprw_ref[...] + _unshift_rows(dps_ref[...], nxt_ref[0:1, :] * keep)
        dpm_b, dprw_b, dz_b = dpm_ref[...].astype(BF16), dprw_v.astype(BF16), dz_ref[...].astype(BF16)
        dpb_ref[:, 0:PM_W] = dpm_b
        dpb_ref[:, PM_W:PM_W + RW_COLS] = dprw_b
        dpb_ref[:, PM_W + RW_COLS:WP_COLS] = dz_b
        du = (lax.dot_general(dpm_b, w_ref[:, 0:PM_W], nt_dims, preferred_element_type=F32)
              + lax.dot_general(dprw_b, w_ref[:, PM_W:PM_W + RW_COLS], nt_dims, preferred_element_type=F32)
              + lax.dot_general(dz_b, w_ref[:, PM_W + RW_COLS:WP_COLS], nt_dims, preferred_element_type=F32))
        x = x_ref[...]
        xhat = x * lax.rsqrt(jnp.mean(x * x, axis=-1, keepdims=True) + NORM_EPS)
        dxn = du * g_ref[...]
        dx = (dxn - xhat * jnp.mean(dxn * xhat, axis=-1, keepdims=True)) * lax.rsqrt(jnp.mean(x * x, axis=-1, keepdims=True) + NORM_EPS)
        gx_ref[...] = dx + dxres_ref[...]
        _acc(dg_ref, jnp.sum(du * xhat, axis=0, keepdims=True), tile == 0)

    return pl.pallas_call(
        body, name="bwd_a", grid=(n // tm,),
        in_specs=[_rows(tm, D_MODEL), _whole((1, D_MODEL)), _whole((D_MODEL, WP_COLS)), _rows(tm, PM_W), _rows(tm, RW_COLS),
                  _rows(tm, RW_COLS), _halo_next(tm, n), _rows(tm, D_MODEL), _rows(tm, D_MODEL)],
        out_specs=[_rows(tm, D_MODEL), _rows(tm, WP_COLS), _whole((1, D_MODEL))],
        out_shape=[_sds((n, D_MODEL)), _sds((n, WP_COLS), BF16), _sds((1, D_MODEL))],
        compiler_params=_ARB1,
    )(x2, g_pre, wp, dpm, dprw, dps, dps, dz, dxres)


def _dw_in(ut, dpb, tk, tn):
    n = ut.shape[1]
    steps = n // tk

    def body(u_ref, d_ref, o_ref, acc_sc):
        k = pl.program_id(1)
        _acc(acc_sc, jnp.dot(u_ref[...], d_ref[...], preferred_element_type=F32), k == 0)

        @pl.when(k == steps - 1)
        def _():
            o_ref[...] = acc_sc[...].astype(BF16)

    return pl.pallas_call(
        body, name="dw_in", grid=(WP_COLS // tn, steps),
        in_specs=[pl.BlockSpec((D_MODEL, tk), lambda j, k: (0, k)), pl.BlockSpec((tk, tn), lambda j, k: (k, j))],
        out_specs=pl.BlockSpec((D_MODEL, tn), lambda j, k: (0, j)),
        out_shape=_sds((D_MODEL, WP_COLS), BF16),
        scratch_shapes=[pltpu.VMEM((D_MODEL, tn), F32)],
        compiler_params=pltpu.CompilerParams(dimension_semantics=("arbitrary", "arbitrary")),
    )(ut, dpb)


ATT_BLK = 256
_NT = (((1,), (1,)), ((), ()))
_TN = (((0,), (0,)), ((), ()))


def _causal(q0, k0, blk, blk_k=None):
    blk_k = blk if blk_k is None else blk_k
    row = q0 + lax.broadcasted_iota(jnp.int32, (blk, blk_k), 0)
    col = k0 + lax.broadcasted_iota(jnp.int32, (blk, blk_k), 1)
    return row >= col


def _attn_fwd(qn, qr, kv, kr):
    bsz, t, _ = qn.shape
    blk = min(ATT_BLK, t)

    heads = range(MLA_HEADS)

    def body(qn_ref, qr_ref, kv_ref, kr_ref, o_ref, lse_ref):
        qi = pl.program_id(1)
        q = [jnp.concatenate([qn_ref[:, LANES * h:LANES * (h + 1)], qr_ref[:, LANES * h:LANES * (h + 1)]], axis=1) for h in heads]
        lower = _causal(0, 0, blk)

        def kv_step(j, carry, diagonal):
            ks = pl.multiple_of(j * blk, blk)
            k_rope = kr_ref[pl.ds(ks, blk), :]
            def score(h):
                k = jnp.concatenate([kv_ref[pl.ds(ks, blk), 2 * LANES * h:2 * LANES * h + LANES], k_rope], axis=1)
                return lax.dot_general(q[h], k, _NT, preferred_element_type=F32)

            out = []
            nxt = score(0)
            for h in heads:
                s = nxt * ATT_SCALE
                if h + 1 < MLA_HEADS:
                    nxt = score(h + 1)
                m, l, acc = carry[h]
                if diagonal:
                    s = jnp.where(lower, s, -1e30)
                m_new = jnp.maximum(m, jnp.max(s, axis=1, keepdims=True))
                alpha = jnp.exp(m - m_new)
                p = jnp.exp(s - m_new)
                l = alpha * l + jnp.sum(p, axis=1, keepdims=True)
                v = kv_ref[pl.ds(ks, blk), 2 * LANES * h + LANES:2 * LANES * (h + 1)]
                out.append((m_new, l, alpha * acc + jnp.dot(p.astype(BF16), v, preferred_element_type=F32)))
            return tuple(out)

        one = (jnp.full((blk, 1), -1e30, F32), jnp.zeros((blk, 1), F32), jnp.zeros((blk, MLA_V), F32))
        carry = lax.fori_loop(0, qi // 2, lambda pr, c: kv_step(2 * pr + 1, kv_step(2 * pr, c, False), False),
                              (one,) * MLA_HEADS)
        carry = lax.cond(qi % 2 == 1, lambda c: kv_step(qi - 1, c, False), lambda c: c, carry)
        carry = kv_step(qi, carry, True)
        for h in heads:
            m, l, acc = carry[h]
            o_ref[:, LANES * h:LANES * (h + 1)] = acc / l
            lse_ref[h] = jnp.broadcast_to(m + jnp.log(l), (blk, LANES))

    return pl.pallas_call(
        body, name="attn_fwd", grid=(bsz, t // blk),
        in_specs=[pl.BlockSpec((None, blk, MLA_WIDTH), lambda b, i: (b, i, 0)),
                  pl.BlockSpec((None, blk, MLA_WIDTH), lambda b, i: (b, i, 0)),
                  pl.BlockSpec((None, t, 2 * MLA_WIDTH), lambda b, i: (b, 0, 0)),
                  pl.BlockSpec((None, t, LANES), lambda b, i: (b, 0, 0))],
        out_specs=[pl.BlockSpec((None, blk, MLA_WIDTH), lambda b, i: (b, i, 0)),
                   pl.BlockSpec((None, MLA_HEADS, blk, LANES), lambda b, i: (b, 0, i, 0))],
        out_shape=[_sds((bsz, t, MLA_WIDTH)), _sds((bsz, MLA_HEADS, t, LANES))],
        compiler_params=pltpu.CompilerParams(dimension_semantics=("arbitrary", "arbitrary")),
    )(qn, qr, kv, kr)


def _attn_bwd(qn, qr, kv, kr, o, do, lse):
    bsz, t, _ = qn.shape
    blk = min(ATT_BLK, t)
    nb = t // blk
    assert nb % 2 == 0, "query blocks are taken in pairs"

    def body(qn_ref, qr_ref, kn_ref, kr_ref, v_ref, o_ref, do_ref, lse_ref, dqn_ref, dqr_ref, dkv_ref, dkr_ref, dq_sc, delta_sc):
        dq_sc[...] = jnp.zeros_like(dq_sc)
        delta_sc[...] = jnp.sum(do_ref[...].astype(F32) * o_ref[...], axis=1, keepdims=True)

        lower = _causal(0, 0, blk)

        def q_blocks(j, k, vb, carry, blocks, diagonal):
            dk, dv = carry
            us = range(len(blocks))
            qs = [i * blk if isinstance(i, int) else pl.multiple_of(i * blk, blk) for i in blocks]
            q = [jnp.concatenate([qn_ref[pl.ds(qs[u], blk), :], qr_ref[pl.ds(qs[u], blk), :]], axis=1) for u in us]
            dob = [do_ref[pl.ds(qs[u], blk), :] for u in us]
            s = [lax.dot_general(q[u], k, _NT, preferred_element_type=F32) for u in us]
            dp = [lax.dot_general(dob[u], vb, _NT, preferred_element_type=F32) for u in us]
            pb, ds = [], []
            for u in us:
                p = jnp.exp(s[u] * ATT_SCALE - lse_ref[pl.ds(qs[u], blk), 0:1])
                if diagonal == u:
                    p = jnp.where(lower, p, 0.0)
                pb.append(p.astype(BF16))
                ds.append((p * (dp[u] - delta_sc[pl.ds(qs[u], blk), :]) * ATT_SCALE).astype(BF16))
            for u in us:
                dv = dv + lax.dot_general(pb[u], dob[u], _TN, preferred_element_type=F32)
            for u in us:
                dq_sc[pl.ds(qs[u], blk), :] += jnp.dot(ds[u], k, preferred_element_type=F32)
                dk = dk + lax.dot_general(ds[u], q[u], _TN, preferred_element_type=F32)
            return dk, dv

        for j in range(nb):
            ks = j * blk
            k = jnp.concatenate([kn_ref[ks:ks + blk, :], kr_ref[ks:ks + blk, :]], axis=1)
            vb = v_ref[ks:ks + blk, :]
            carry = (jnp.zeros((blk, 2 * LANES), F32), jnp.zeros((blk, MLA_V), F32))
            if j % 2 == 0:
                carry = q_blocks(j, k, vb, carry, [j, j + 1], 0)
            else:
                carry = q_blocks(j, k, vb, carry, [j], 0)
            pairs_from = j // 2 + 1
            if nb // 2 - pairs_from > 0:
                carry = lax.fori_loop(pairs_from, nb // 2,
                                      lambda pr, c, j=j, k=k, vb=vb: q_blocks(j, k, vb, c, [2 * pr, 2 * pr + 1], None),
                                      carry, unroll=nb // 2 - pairs_from)
            dk, dv = carry
            dkv_ref[ks:ks + blk, 0:LANES] = dk[:, 0:LANES]
            dkv_ref[ks:ks + blk, LANES:2 * LANES] = dv
            dkr_ref[ks:ks + blk, :] = dk[:, LANES:2 * LANES]
        dqn_ref[...] = dq_sc[:, 0:LANES]
        dqr_ref[...] = dq_sc[:, LANES:2 * LANES]

    head_col = lambda b, h: (b, 0, h)
    return pl.pallas_call(
        body, name="attn_bwd", grid=(bsz, MLA_HEADS),
        in_specs=[pl.BlockSpec((None, t, LANES), head_col), pl.BlockSpec((None, t, LANES), head_col),
                  pl.BlockSpec((None, t, LANES), lambda b, h: (b, 0, 2 * h)),
                  pl.BlockSpec((None, t, LANES), lambda b, h: (b, 0, 0)),
                  pl.BlockSpec((None, t, LANES), lambda b, h: (b, 0, 2 * h + 1)),
                  pl.BlockSpec((None, t, LANES), head_col), pl.BlockSpec((None, t, LANES), head_col),
                  pl.BlockSpec((None, None, t, LANES), lambda b, h: (b, h, 0, 0))],
        out_specs=[pl.BlockSpec((None, t, LANES), head_col), pl.BlockSpec((None, t, LANES), head_col),
                   pl.BlockSpec((None, t, 2 * LANES), head_col),
                   pl.BlockSpec((None, None, t, LANES), lambda b, h: (h, b, 0, 0))],
        out_shape=[_sds((bsz, t, MLA_WIDTH)), _sds((bsz, t, MLA_WIDTH)), _sds((bsz, t, 2 * MLA_WIDTH)),
                   _sds((MLA_HEADS, bsz, t, LANES))],
        scratch_shapes=[pltpu.VMEM((t, 2 * LANES), F32), pltpu.VMEM((t, 1), F32)],
        compiler_params=pltpu.CompilerParams(dimension_semantics=("arbitrary", "arbitrary")),
    )(qn, qr, kv, kr, kv, o, do, lse)


SCAN_CHUNK = 16


def _diag_mask():
    row = lax.broadcasted_iota(jnp.int32, (RW_HEAD, RW_WIDTH), 0)
    lane = lax.broadcasted_iota(jnp.int32, (RW_HEAD, RW_WIDTH), 1)
    return jnp.where(row == (lane & (RW_HEAD - 1)), 1.0, 0.0)


def _time_minor(a):
    bsz, t, _ = a.shape
    a = a.reshape(bsz, t // SCAN_CHUNK, SCAN_CHUNK, RW_HEADS, RW_HEAD)
    return a.transpose(0, 1, 4, 3, 2).reshape(bsz, t // SCAN_CHUNK, RW_HEAD, RW_HEADS * SCAN_CHUNK)


def _head_expand():
    l = lax.broadcasted_iota(jnp.int32, (2 * LANES, RW_WIDTH), 0)
    n = lax.broadcasted_iota(jnp.int32, (2 * LANES, RW_WIDTH), 1)
    return jnp.where(((l & (LANES - 1)) >> 4) == (n >> 6), 1.0, 0.0).astype(BF16)


BCAST_GROUP = 4


def _outer_chunk(tm_ref, row_ref, out_sc, expand, seqs):
    step_of_lane = lax.broadcasted_iota(jnp.int32, (RW_HEAD, LANES), 1) & (SCAN_CHUNK - 1)
    tiles = [tm_ref[bi, 0] for bi in seqs]
    for t0 in range(0, SCAN_CHUNK, BCAST_GROUP):
        parts = []
        for t in range(t0, t0 + BCAST_GROUP):
            for tile in tiles:
                a = jnp.where(step_of_lane == t, tile, 0.0)
                hi = a.astype(BF16)
                parts.append(jnp.concatenate([hi, (a - hi.astype(F32)).astype(BF16)], axis=1))
        cols = jnp.dot(jnp.concatenate(parts, axis=0), expand, preferred_element_type=F32)
        for j, t in enumerate(range(t0, t0 + BCAST_GROUP)):
            base = j * RW_HEAD * len(seqs)
            out_sc[t] = jnp.concatenate([cols[base + RW_HEAD * bi:base + RW_HEAD * (bi + 1)] * row_ref[bi, t:t + 1, :]
                                         for bi in seqs], axis=0)


def _fold8(x):
    acc = x[0:8]
    for j in range(1, x.shape[0] // 8):
        acc = acc + x[8 * j:8 * (j + 1)]
    return acc


def _rows8(at):
    return pl.ds(at * 8 if isinstance(at, int) else pl.multiple_of(at * 8, 8), 8)


def _put8(sc, bi, at, val):
    for j in range(RW_WIDTH // LANES):
        sc[bi * (RW_WIDTH // LANES) + j, _rows8(at), :] = val[:, LANES * j:LANES * (j + 1)]


def _unfold8(sc, bi, steps):
    tiles = []
    for j in range(RW_WIDTH // LANES):
        view = sc.at[bi * (RW_WIDTH // LANES) + j]
        acc = view[pl.ds(0, steps, stride=8), :]
        for s in range(1, 8):
            acc = acc + view[pl.ds(s, steps, stride=8), :]
        tiles.append(acc)
    return jnp.concatenate(tiles, axis=1)


def _scan_fwd(r, w, k, vt, nkk, b):
    bsz, t, _ = r.shape
    tc = SCAN_CHUNK

    def body(r_ref, w_ref, k_ref, n_ref, b_ref, vt_ref, y_ref, st_ref, s_sc, vc_sc, y_sc):
        @pl.when(pl.program_id(0) == 0)
        def _():
            s_sc[...] = jnp.zeros_like(s_sc)

        ones = _seg_ones()
        diag = _diag_mask()
        seqs = range(bsz)
        _outer_chunk(vt_ref, k_ref, vc_sc, _head_expand(), seqs)

        def put_y(ya, at):
            for bi in seqs:
                _put8(y_sc, bi, at, _fold8(ya[bi] * diag))

        def step(i, _):
            row = lambda ref, bi: ref[bi, pl.ds(i, 1), :]
            prev = jnp.maximum(i - 1, 0)
            s_old = [s_sc[bi] for bi in seqs]
            s_b = [s_old[bi].astype(BF16) for bi in seqs]
            sa = _seg_multi([s_b[bi] * row(n_ref, bi).astype(BF16) for bi in seqs], ones, 1)
            put_y(_seg_multi([s_b[bi] * r_ref[bi, pl.ds(prev, 1), :].astype(BF16) for bi in seqs], ones, 1), prev)
            vk = vc_sc[i]
            for bi in seqs:
                s_new = s_old[bi] * row(w_ref, bi) + sa[bi] * row(b_ref, bi) + vk[RW_HEAD * bi:RW_HEAD * (bi + 1)]
                s_sc[bi] = s_new
                st_ref[bi, i] = s_new.astype(BF16)
            return 0

        lax.fori_loop(0, tc, step, 0, unroll=8)
        put_y(_seg_multi([s_sc[bi].astype(BF16) * r_ref[bi, tc - 1:tc, :].astype(BF16) for bi in seqs], ones, 1), tc - 1)
        for bi in seqs:
            y_ref[bi] = _unfold8(y_sc, bi, tc)

    vec = pl.BlockSpec((bsz, tc, RW_WIDTH), lambda c: (0, c, 0))
    return pl.pallas_call(
        body, name="scan_fwd", grid=(t // tc,),
        in_specs=[vec] * 5 + [pl.BlockSpec((bsz, 1, RW_HEAD, LANES), lambda c: (0, c, 0, 0))],
        out_specs=[vec, pl.BlockSpec((bsz, tc, RW_HEAD, RW_WIDTH), lambda c: (0, c, 0, 0))],
        out_shape=[_sds((bsz, t, RW_WIDTH)), _sds((bsz, t, RW_HEAD, RW_WIDTH), BF16)],
        scratch_shapes=[pltpu.VMEM((bsz, RW_HEAD, RW_WIDTH), F32), pltpu.VMEM((tc, bsz * RW_HEAD, RW_WIDTH), F32),
                        pltpu.VMEM((bsz * RW_WIDTH // LANES, tc * 8, LANES), F32)],
        compiler_params=_ARB1,
    )(r, w, k, nkk, b, vt)


def _own_head_row(x):
    first_half = lax.broadcasted_iota(jnp.int32, (1, LANES), 1) < RW_HEAD
    tiles = [jnp.where(first_half, x[2 * j:2 * j + 1, LANES * j:LANES * (j + 1)], x[2 * j + 1:2 * j + 2, LANES * j:LANES * (j + 1)])
             for j in range(RW_WIDTH // LANES)]
    return jnp.concatenate(tiles, axis=1)


def _scan_bwd(r, w, k, vt, nkk, b, st, dyt):
    bsz, t, _ = r.shape
    tc = SCAN_CHUNK
    nc = t // tc

    def body(r_ref, w_ref, k_ref, n_ref, b_ref, vt_ref, dyt_ref, st_ref, halo_ref,
             dr_ref, dw_ref, dk_ref, dv_ref, dn_ref, db_ref, g_sc, dc_sc, v8_sc, dy8_sc, *part_scs):
        c = pl.program_id(0)

        @pl.when(c == 0)
        def _():
            g_sc[...] = jnp.zeros_like(g_sc)

        ones = _seg_ones()
        diag = _diag_mask()
        has_prev = jnp.where(c == nc - 1, 0.0, 1.0)
        seqs = range(bsz)
        _outer_chunk(dyt_ref, r_ref, dc_sc, _head_expand(), seqs)
        for bi in seqs:
            v8_sc[bi] = jnp.concatenate([vt_ref[bi, 0].T] * 2, axis=1)
            dy8_sc[bi] = jnp.concatenate([dyt_ref[bi, 0].T] * 2, axis=1)
        by_head = lambda sc, bi, i: sc.at[bi][pl.ds(i, RW_HEADS, stride=SCAN_CHUNK), :][:, 0:RW_HEAD].astype(BF16)
        dw_sc, dv_sc, dn_sc, db_sc = part_scs

        def step(i, s_p, s_t_b=None):
            static = isinstance(i, int)
            row = lambda ref, bi: ref[bi, i:i + 1, :] if static else ref[bi, pl.ds(i, 1), :]
            put_row = lambda ref, bi, val: ref.__setitem__((bi, slice(i, i + 1) if static else pl.ds(i, 1), slice(None)), val)
            if s_t_b is None:
                s_t_b = [st_ref[bi, i] for bi in seqs]
            s_p_b = s_p
            s_p = [s_p_b[bi].astype(F32) for bi in seqs]
            dr8 = [jnp.dot(by_head(dy8_sc, bi, i), s_t_b[bi], preferred_element_type=F32) for bi in seqs]
            rowb = lambda ref, bi: row(ref, bi).astype(BF16)
            sa = _seg_multi([s_p_b[bi] * rowb(n_ref, bi) for bi in seqs], ones, 1)
            dc_all = dc_sc[i]
            dc = [dc_all[RW_HEAD * bi:RW_HEAD * (bi + 1)] for bi in seqs]
            g = [g_sc[bi] + dc[bi] for bi in seqs]
            g_b = [g[bi].astype(BF16) for bi in seqs]
            res = _seg_multi([g_b[bi] * rowb(b_ref, bi) for bi in seqs] + [g_b[bi] * rowb(k_ref, bi) for bi in seqs], ones, 1)
            dsa, dvb = res[:bsz], res[bsz:]
            for bi in seqs:
                dk8 = jnp.dot(by_head(v8_sc, bi, i), g_b[bi], preferred_element_type=F32)
                put_row(dr_ref, bi, _own_head_row(dr8[bi]))
                put_row(dk_ref, bi, _own_head_row(dk8))
                _put8(dv_sc, bi, i, _fold8(dvb[bi] * diag))
                _put8(dw_sc, bi, i, _fold8(g[bi] * s_p[bi]))
                _put8(db_sc, bi, i, _fold8(g[bi] * sa[bi]))
                _put8(dn_sc, bi, i, _fold8(s_p[bi] * dsa[bi]))
                g_sc[bi] = g[bi] * row(w_ref, bi) + dsa[bi] * row(n_ref, bi)
            return s_p_b

        group = 5

        def loop_trip(trip, _):
            top = tc - 1 - trip * group
            s_b = None
            for u in range(group):
                i = top - u
                s_b = step(i, [st_ref[bi, i - 1] for bi in seqs], s_b)
            return 0

        lax.fori_loop(0, (tc - 1) // group, loop_trip, 0)
        step(0, [halo_ref[bi, 0] * has_prev.astype(BF16) for bi in seqs])
        for out_ref, sc in zip((dw_ref, dv_ref, dn_ref, db_ref), part_scs):
            for bi in seqs:
                out_ref[bi] = _unfold8(sc, bi, tc)

    vec = pl.BlockSpec((bsz, tc, RW_WIDTH), lambda c: (0, nc - 1 - c, 0))
    tmin = pl.BlockSpec((bsz, 1, RW_HEAD, LANES), lambda c: (0, nc - 1 - c, 0, 0))
    parts = pltpu.VMEM((bsz * RW_WIDTH // LANES, tc * 8, LANES), F32)
    heads_steps = pltpu.VMEM((bsz, LANES, LANES), F32)
    return pl.pallas_call(
        body, name="scan_bwd", grid=(nc,),
        in_specs=[vec] * 5 + [tmin, tmin,
                              pl.BlockSpec((bsz, tc, RW_HEAD, RW_WIDTH), lambda c: (0, nc - 1 - c, 0, 0)),
                              pl.BlockSpec((bsz, 1, RW_HEAD, RW_WIDTH), lambda c: (0, jnp.maximum((nc - 1 - c) * tc - 1, 0), 0, 0))],
        out_specs=[vec] * 6,
        out_shape=[_sds((bsz, t, RW_WIDTH))] * 6,
        scratch_shapes=[pltpu.VMEM((bsz, RW_HEAD, RW_WIDTH), F32), pltpu.VMEM((tc, bsz * RW_HEAD, RW_WIDTH), F32),
                        heads_steps, heads_steps] + [parts] * 4,
        compiler_params=_ARB1,
    )(r, w, k, nkk, b, vt, dyt, st, st)


TOKEN_TILE = 256
VJP_TILE = 256


def _padded_weights(wt):
    f = lambda a: a.astype(F32)
    w_in = wt["w_in"][0].astype(BF16)
    zeros = lambda r, c: jnp.zeros((r, c), F32)
    wp = jnp.concatenate([w_in[:, :MLA_COLS], jnp.zeros((D_MODEL, PM_W - MLA_COLS), BF16), w_in[:, MLA_COLS:]], axis=1)
    w_uq = f(wt["mla_w_uq"][0]).reshape(Q_LORA, MLA_HEADS, MLA_NOPE + MLA_ROPE)
    wqn = w_uq[:, :, :MLA_NOPE].reshape(Q_LORA, MLA_HEADS * MLA_NOPE)
    wqr = jnp.concatenate([w_uq[:, :, MLA_NOPE:], jnp.zeros((Q_LORA, MLA_HEADS, LANES - MLA_ROPE), F32)], axis=2)
    wqr = wqr.reshape(Q_LORA, MLA_HEADS * LANES)
    w2p = jnp.concatenate([f(wt["rw_w2"][0]), zeros(LORA, RW_WIDTH)], axis=0)
    a2p = jnp.concatenate([zeros(LORA, RW_WIDTH), f(wt["rw_a2"][0])], axis=0)
    bw = (f(wt["mla_q_norm_g"]), wqn, wqr, f(wt["mla_kv_norm_g"]), f(wt["mla_w_ukv"][0]), f(wt["rw_mu"]), f(wt["rw_w0"]),
          w2p, f(wt["rw_a0"]), a2p, f(wt["rw_k_k"]), f(wt["rw_k_a"]))
    hw = (f(wt["rw_ln_g"]), f(wt["rw_ln_b"]), f(wt["rw_r_k"]).reshape(1, RW_WIDTH), f(wt["w_out"][0]), f(wt["norm_post_g"]))
    return wp, bw, hw


def _local_step(x, positions, target, wt):
    bsz, t, _ = x.shape
    n = bsz * t
    tm = min(TOKEN_TILE, t)
    tps = t // tm
    ts = min(VJP_TILE, t)
    wp, bw, hw = _padded_weights(wt)
    wpb = wp.astype(BF16)
    g_pre = wt["norm_pre_g"].astype(F32)
    x2 = x.reshape(n, D_MODEL)
    tgt2 = target.reshape(n, D_MODEL)
    inv_freq = ROPE_THETA ** (-jnp.arange(0, MLA_ROPE, 2, dtype=F32) / MLA_ROPE)
    invf = jnp.tile(inv_freq, LANES // (MLA_ROPE // 2)).reshape(LANES, 1)
    cos, sin = _rope_tables(positions.reshape(1, n), invf, tm)

    u, pm, prw, z = _fwd_a(x2, g_pre, wpb, tm)
    qn, qr, kv, kr, r, w, k, v, nkk, b = _fwd_b(pm, prw, cos, sin, bw, tm, tps)
    b3 = lambda a: a.reshape(bsz, t, a.shape[-1])
    ym, lse = _attn_fwd(b3(qn), b3(qr), b3(kv), b3(kr))
    vt = _time_minor(b3(v))
    ys, st = _scan_fwd(b3(r), b3(w), b3(k), vt, b3(nkk), b3(b))
    (dys, dr_h, dk_h, dv_h, dym, dz, dxres, loss, d_lng, d_lnb, d_rk, d_wout, d_gpost) = _head(
        ys.reshape(n, RW_WIDTH), r, k, v, ym.reshape(n, MLA_WIDTH), z, x2, tgt2, hw, ts)
    dqn, dqr, dkv, dkr_heads = _attn_bwd(b3(qn), b3(qr), b3(kv), b3(kr), ym, b3(dym), lse)
    dr_s, dw_s, dk_s, dv_s, dn_s, db_s = _scan_bwd(b3(r), b3(w), b3(k), vt, b3(nkk), b3(b), st, _time_minor(b3(dys)))
    f2 = lambda a: a.reshape(n, a.shape[-1])
    cts = (f2(dqn), f2(dqr), f2(dkv), f2(dr_s), dr_h, f2(dw_s), f2(dk_s), dk_h, f2(dv_s), dv_h, f2(dn_s), f2(db_s))
    (dpm, dprw, dps, d_gq, d_wqn, d_wqr, d_gkv, d_wkv, d_mu, d_w0, d_w2p, d_a0, d_a2p, d_kk, d_ka) = _bwd_b(
        pm, prw, cos, sin, bw, cts, dkr_heads.reshape(MLA_HEADS, n, LANES), ts, t // ts)
    grad_x, dpb, d_gpre = _bwd_a(x2, g_pre, wpb, dpm, dprw, dps, dz, dxres, tm, tps)
    d_wp = _dw_in(u, dpb, min(1024, n), 640)

    d_w_in = jnp.concatenate([d_wp[:, :MLA_COLS], d_wp[:, PM_W:]], axis=1)
    d_w_uq = jnp.concatenate([d_wqn.reshape(Q_LORA, MLA_HEADS, MLA_NOPE),
                              d_wqr.reshape(Q_LORA, MLA_HEADS, LANES)[:, :, :MLA_ROPE]], axis=2)
    grads = {
        "norm_pre_g": d_gpre, "w_in": d_w_in[None], "mla_q_norm_g": d_gq,
        "mla_w_uq": d_w_uq.reshape(1, Q_LORA, MLA_HEADS * (MLA_NOPE + MLA_ROPE)), "mla_kv_norm_g": d_gkv,
        "mla_w_ukv": d_wkv[None], "rw_mu": d_mu, "rw_w0": d_w0, "rw_w2": d_w2p[None, :LORA], "rw_a0": d_a0,
        "rw_a2": d_a2p[None, LORA:], "rw_k_k": d_kk, "rw_k_a": d_ka, "rw_r_k": d_rk.reshape(1, RW_HEADS, RW_HEAD),
        "rw_ln_g": d_lng, "rw_ln_b": d_lnb, "w_out": d_wout[None], "norm_post_g": d_gpost,
    }
    return loss, grad_x.reshape(bsz, t, D_MODEL), grads


_MESH = pl.DeviceIdType.MESH


def _gather_shards(shards):
    na = len(shards)

    def body(*refs):
        x_refs, out_refs = refs[:na], refs[na:2 * na]
        send_sems, recv_sems, local_sems = refs[2 * na:]
        x, y, c = lax.axis_index("x"), lax.axis_index("y"), lax.axis_index("c")
        me, sibling = (x, y, c), (x, y, 1 - c)
        chips = [(1 - x, y), (x, 1 - y), (1 - x, 1 - y)]
        arrays = range(na)

        def slot(a, px, py, pc):
            return out_refs[a].at[4 * px + 2 * py + pc]

        def copy(k, a, block, to, src=None):
            return pltpu.make_async_remote_copy(
                src_ref=slot(a, *block) if src is None else src, dst_ref=slot(a, *block),
                send_sem=send_sems.at[k, a], recv_sem=recv_sems.at[k, a], device_id=to, device_id_type=_MESH)

        mine = [pltpu.make_async_copy(x_refs[a], slot(a, *me), local_sems.at[a]) for a in arrays]
        for cp in mine:
            cp.start()
        first = [copy(0, a, me, sibling, src=x_refs[a]) for a in arrays]
        first += [copy(1 + j, a, me, (*chip, c), src=x_refs[a]) for j, chip in enumerate(chips) for a in arrays]
        for cp in first:
            cp.start()
        passed = []
        for j, chip in enumerate(chips):
            for a in arrays:
                copy(1 + j, a, (*chip, c), me).wait_recv()
                passed.append(copy(4 + j, a, (*chip, c), sibling))
                passed[-1].start()
        for a in arrays:
            copy(0, a, sibling, me).wait_recv()
        for j, chip in enumerate(chips):
            for a in arrays:
                copy(4 + j, a, (*chip, 1 - c), me).wait_recv()
        for cp in first + passed:
            cp.wait_send()
        for cp in mine:
            cp.wait()

    vmem = pl.BlockSpec(memory_space=pltpu.VMEM)
    return pl.pallas_call(
        body, name="gather_shards",
        out_shape=[_sds((N_DEV,) + a.shape, a.dtype) for a in shards],
        in_specs=[vmem] * na, out_specs=[vmem] * na,
        scratch_shapes=[pltpu.SemaphoreType.DMA((7, na)), pltpu.SemaphoreType.DMA((7, na)), pltpu.SemaphoreType.DMA((na,))],
    )(*shards)


SMALL_LANES = SMALL_N + LANES


N_CHIP = 4


def _exchange_grads(big_blocks, small_grads, loss_tile):
    nb = len(big_blocks)
    ns = len(small_grads)

    def body(*refs):
        big, small, loss_ref = refs[:nb], refs[nb:nb + ns], refs[nb + ns]
        out, rsmall = refs[nb + ns + 1:2 * nb + ns + 1], refs[2 * nb + ns + 1]
        scratch = refs[2 * nb + ns + 2:]
        stage, sums = scratch[:nb], scratch[nb:2 * nb]
        send1, recv1, send2, recv2, send_s, recv_s, row_sc = scratch[2 * nb:]
        x, y, c = lax.axis_index("x"), lax.axis_index("y"), lax.axis_index("c")
        me_lin = 4 * x + 2 * y + c
        my_chip = 2 * x + y
        sibling = (x, y, 1 - c)
        leaves = range(nb)

        to_sibling = [pltpu.make_async_remote_copy(
            src_ref=big[j].at[1 - c], dst_ref=stage[j], send_sem=send1.at[j], recv_sem=recv1.at[j],
            device_id=sibling, device_id_type=_MESH) for j in leaves]
        for cp in to_sibling:
            cp.start()
        off = 0
        for ref, (_, cnt) in zip(small, SMALL):
            row_sc[:, off:off + cnt] = ref[...]
            off += cnt
        row_sc[:, off:off + LANES] = loss_ref[0:1, :]
        rsmall[me_lin] = row_sc[...]
        rows = []
        for k in range(1, N_DEV):
            peer = (x ^ (k >> 2), y ^ ((k >> 1) & 1), c ^ (k & 1))
            rows.append(pltpu.make_async_remote_copy(
                src_ref=row_sc, dst_ref=rsmall.at[me_lin], send_sem=send_s.at[k - 1], recv_sem=recv_s.at[k - 1],
                device_id=peer, device_id_type=_MESH))
        for cp in rows:
            cp.start()
        to_chips = []
        for j in leaves:
            to_sibling[j].wait_recv()
            sums[j][...] = (big[j][c].astype(F32) + stage[j][...].astype(F32)).astype(BF16)
            out[j][0] = sums[j][my_chip]
            for q in range(1, N_CHIP):
                px, py = x ^ (q >> 1), y ^ (q & 1)
                to_chips.append(pltpu.make_async_remote_copy(
                    src_ref=sums[j].at[2 * px + py], dst_ref=out[j].at[q], send_sem=send2.at[q - 1, j],
                    recv_sem=recv2.at[q - 1, j], device_id=(px, py, c), device_id_type=_MESH))
                to_chips[-1].start()
        for cp in to_chips + rows:
            cp.wait_recv()
        for cp in to_sibling + to_chips + rows:
            cp.wait_send()

    vmem = pl.BlockSpec(memory_space=pltpu.VMEM)
    shard = [a.shape[2:] for a in big_blocks]
    return pl.pallas_call(
        body, name="exchange_grads",
        out_shape=[_sds((N_CHIP,) + s, BF16) for s in shard] + [_sds((N_DEV, 1, SMALL_LANES))],
        in_specs=[vmem] * (nb + ns + 1), out_specs=[vmem] * (nb + 1),
        scratch_shapes=[pltpu.VMEM((N_CHIP,) + s, BF16) for s in shard] * 2
        + [pltpu.SemaphoreType.DMA((nb,)), pltpu.SemaphoreType.DMA((nb,)),
           pltpu.SemaphoreType.DMA((N_CHIP - 1, nb)), pltpu.SemaphoreType.DMA((N_CHIP - 1, nb)),
           pltpu.SemaphoreType.DMA((N_DEV - 1,)), pltpu.SemaphoreType.DMA((N_DEV - 1,)), pltpu.VMEM((1, SMALL_LANES), F32)],
    )(*big_blocks, *small_grads, loss_tile)


def _adamw_math(w, g, m, v):
    m = ADAM_B1 * m + (1.0 - ADAM_B1) * g
    v = ADAM_B2 * v + (1.0 - ADAM_B2) * (g * g)
    m_hat = m / (1.0 - ADAM_B1 ** ADAM_STEP)
    v_hat = v / (1.0 - ADAM_B2 ** ADAM_STEP)
    return -ADAM_LR * (m_hat / (jnp.sqrt(v_hat) + ADAM_EPS) + ADAM_WD * w), m, v


def _reduce_adamw(name, parts, w, m, v, row_blocks):
    slots, rows, cols = parts.shape
    rb = rows // row_blocks

    def body(p_ref, w_ref, m_ref, v_ref, g_out, d_out, m_out, v_out):
        g = p_ref[0].astype(F32)
        for s in range(1, slots):
            g = g + p_ref[s].astype(F32)
        g_out[0] = g
        d_out[0], m_out[0], v_out[0] = _adamw_math(w_ref[0], g, m_ref[0], v_ref[0])

    blk = pl.BlockSpec((1, rb, cols), lambda i: (0, i, 0))
    return pl.pallas_call(
        body, name="reduce_adamw_" + name, grid=(row_blocks,),
        in_specs=[pl.BlockSpec((slots, rb, cols), lambda i: (0, i, 0)), blk, blk, blk],
        out_specs=[blk] * 4, out_shape=[_sds((1, rows, cols))] * 4,
        compiler_params=_ARB1,
    )(parts, w, m, v)


def _reduce_adamw_small(rows, ws, ms, vs):
    ns = len(SMALL)

    def body(r_ref, *refs):
        w_refs, m_refs, v_refs, outs = refs[:ns], refs[ns:2 * ns], refs[2 * ns:3 * ns], refs[3 * ns:]
        total = r_ref[0]
        for s in range(1, N_DEV):
            total = total + r_ref[s]
        off = 0
        for j, (_, cnt) in enumerate(SMALL):
            g = total[:, off:off + cnt]
            off += cnt
            outs[4 * j][...] = g
            outs[4 * j + 1][...], outs[4 * j + 2][...], outs[4 * j + 3][...] = _adamw_math(
                w_refs[j][...], g, m_refs[j][...], v_refs[j][...])
        outs[4 * ns][...] = total[:, off:off + LANES]

    vmem = pl.BlockSpec(memory_space=pltpu.VMEM)
    return pl.pallas_call(
        body, name="reduce_adamw_small",
        in_specs=[vmem] * (1 + 3 * ns), out_specs=[vmem] * (4 * ns + 1),
        out_shape=[_sds((1, cnt)) for _, cnt in SMALL for _ in range(4)] + [_sds((1, LANES))],
    )(rows, *ws, *ms, *vs)


def _shard_blocks(name, full):
    a = full[0]
    rows, cols = a.shape
    if name == "w_out":
        return a.reshape(N_CHIP, 2, rows // N_DEV, cols).transpose(1, 0, 2, 3)
    return a.reshape(rows, N_CHIP, 2, cols // N_DEV).transpose(2, 1, 0, 3)


def _unshard(name, blocks):
    _, rows, cols = blocks.shape
    if name == "w_out":
        return blocks.reshape(1, N_DEV * rows, cols)
    return blocks.transpose(1, 0, 2).reshape(1, rows, N_DEV * cols)


def kernel(x, positions, norm_pre_g, w_in, mla_q_norm_g, mla_w_uq, mla_kv_norm_g, mla_w_ukv, rw_mu, rw_w0, rw_w2, rw_a0, rw_a2, rw_k_k, rw_k_a, rw_r_k, rw_ln_g, rw_ln_b, w_out, norm_post_g, loss_target, m_norm_pre_g, m_w_in, m_mla_q_norm_g, m_mla_w_uq, m_mla_kv_norm_g, m_mla_w_ukv, m_rw_mu, m_rw_w0, m_rw_w2, m_rw_a0, m_rw_a2, m_rw_k_k, m_rw_k_a, m_rw_r_k, m_rw_ln_g, m_rw_ln_b, m_w_out, m_norm_post_g, v_norm_pre_g, v_w_in, v_mla_q_norm_g, v_mla_w_uq, v_mla_kv_norm_g, v_mla_w_ukv, v_rw_mu, v_rw_w0, v_rw_w2, v_rw_a0, v_rw_a2, v_rw_k_k, v_rw_k_a, v_rw_r_k, v_rw_ln_g, v_rw_ln_b, v_w_out, v_norm_post_g):
    given = dict(locals())
    w = {nm: given[nm] for nm in WEIGHTS}
    mom = {nm: given["m_" + nm] for nm in WEIGHTS}
    var = {nm: given["v_" + nm] for nm in WEIGHTS}
    sharded = list(SHARDED)

    gathered = _gather_shards([w[nm][0].astype(BF16) for nm in sharded])
    full = dict(w)
    for nm, blocks in zip(sharded, gathered):
        full[nm] = _unshard(nm, blocks)

    loss_part, grad_x, grads = _local_step(x, positions, loss_target, full)

    small_names = [nm for nm, _ in SMALL]
    row = lambda a: a.reshape(1, -1)
    got = _exchange_grads([_shard_blocks(nm, grads[nm]).astype(BF16) for nm in sharded],
                          [row(grads[nm]) for nm in small_names], loss_part)
    new = {}
    for nm, parts in zip(sharded, got[:-1]):
        new[nm] = _reduce_adamw(nm, parts, w[nm], mom[nm], var[nm], 4 if nm == "w_in" else 1)
    res = _reduce_adamw_small(got[-1], [row(w[nm]) for nm in small_names], [row(mom[nm]) for nm in small_names],
                              [row(var[nm]) for nm in small_names])
    for j, nm in enumerate(small_names):
        new[nm] = tuple(a.reshape(w[nm].shape) for a in res[4 * j:4 * j + 4])
    loss = res[-1][0, 0]
    return (loss, grad_x, *[new[nm][j] for j in range(4) for nm in WEIGHTS])
```
